```python
import math
import jax
import jax.numpy as jnp
from jax import lax
import numpy as np


D_MODEL = 1024
BATCH = 16
SEQ = 2048
DEPTH = 4

CTX_LEN = 256
GRID_W = 64
HEAD_DIM = 64
ROPE_BASE = 10000.0
EPS = 1e-6
NEG_INF = -1e30

GDN_HEADS = D_MODEL // 128
GDN_DIM = 64
GDN_CHUNK = 64
SHORT_CONV = 3
DIFF_HEADS = D_MODEL // 256
DIFF_DIM = 64
DIFF_VDIM = 2 * DIFF_DIM
Q_BLOCK = 128
SWA_HEADS = D_MODEL // 128
SWA_KV_HEADS = 2
SWA_GROUP = SWA_HEADS // SWA_KV_HEADS
SWA_WINDOW = 128
SWA_BLOCK = 128
HY_CH = D_MODEL // 2
HY_ORDER = 2
HY_BANDS = 16
HY_EMB = 1 + 2 * HY_BANDS
HY_HID = 64
HY_MIN_DECAY = math.log(1e-2) / 1.5
HY_MAX_DECAY = math.log(1e-2) / 0.3
D_FF = ((8 * D_MODEL // 3 + 127) // 128) * 128
FFN_CONV = 3
N_MOD = 6

AB_SPLITS = (3 * GDN_HEADS * GDN_DIM, GDN_HEADS * GDN_DIM, 2 * GDN_HEADS, 2 * GDN_HEADS,
             DIFF_HEADS * 2 * DIFF_DIM, DIFF_HEADS * 2 * DIFF_DIM, DIFF_HEADS * DIFF_VDIM)
AB_OUT = GDN_HEADS * GDN_DIM + DIFF_HEADS * DIFF_VDIM
CD_SPLITS = (SWA_HEADS * HEAD_DIM, SWA_KV_HEADS * HEAD_DIM, SWA_KV_HEADS * HEAD_DIM, 3 * HY_CH)
CD_OUT = SWA_HEADS * HEAD_DIM + HY_CH

kernel_name = 'hybrid_dit_deltanet_diffattn_swa_hyena'


def rmsnorm(x, g):
    xf = x.astype(jnp.float32)
    y = xf * lax.rsqrt(jnp.mean(xf * xf, axis=-1, keepdims=True) + EPS)
    return (y * g.astype(jnp.float32)).astype(x.dtype)


def l2norm(x):
    return x * lax.rsqrt(jnp.sum(x * x, axis=-1, keepdims=True) + EPS)


def modulate(h, shift, scale):
    return h * (1.0 + scale) + shift


def split_cols(t, sizes):
    return jnp.split(t, np.cumsum(sizes)[:-1].tolist(), axis=-1)


def dwconv(x, w):
    k = w.shape[0]
    r = k // 2
    n = x.shape[1]
    xp = jnp.pad(x, ((0, 0), (r, r), (0, 0)))
    return sum(xp[:, i:i + n] * w[i] for i in range(k))


def axial_rope_angles(n_tokens):
    rows = n_tokens // GRID_W
    row = jnp.repeat(jnp.arange(rows, dtype=jnp.float32), GRID_W)
    col = jnp.tile(jnp.arange(GRID_W, dtype=jnp.float32), rows)
    half = HEAD_DIM // 2
    inv = ROPE_BASE ** (-jnp.arange(0, half, 2, dtype=jnp.float32) / half)
    return row[:, None] * inv, col[:, None] * inv


def rope_rotate(x, ang):
    cos = jnp.cos(ang)[:, None, :].astype(x.dtype)
    sin = jnp.sin(ang)[:, None, :].astype(x.dtype)
    x1, x2 = jnp.split(x, 2, axis=-1)
    return jnp.concatenate([x1 * cos - x2 * sin, x2 * cos + x1 * sin], axis=-1)


def rope2d(x, ang_row, ang_col):
    xr, xc = jnp.split(x, 2, axis=-1)
    return jnp.concatenate([rope_rotate(xr, ang_row), rope_rotate(xc, ang_col)], axis=-1)


def gated_delta_chunked(q, k, v, g, beta, s0):
    b, h, n, dk = q.shape
    dv = v.shape[-1]
    c = GDN_CHUNK
    nc = n // c
    f32 = jnp.float32
    q = q.astype(f32).reshape(b, h, nc, c, dk)
    k = k.astype(f32).reshape(b, h, nc, c, dk)
    v = v.astype(f32).reshape(b, h, nc, c, dv)
    g = g.astype(f32).reshape(b, h, nc, c)
    beta = beta.astype(f32).reshape(b, h, nc, c)
    gc = jnp.cumsum(g, axis=-1)
    pos = jnp.arange(c)
    incl = pos[:, None] >= pos[None, :]
    strict = pos[:, None] > pos[None, :]
    diff = gc[..., :, None] - gc[..., None, :]
    decay = jnp.where(incl, jnp.exp(jnp.where(incl, diff, 0.0)), 0.0)
    kb = k * beta[..., None]
    a = jnp.where(strict, jnp.einsum('bhnid,bhnjd->bhnij', kb, k) * decay, 0.0)
    rhs = jnp.concatenate([v * beta[..., None], kb * jnp.exp(gc)[..., None]], axis=-1)
    sol = lax.linalg.triangular_solve(a, rhs, left_side=True, lower=True, unit_diagonal=True)
    u, w = sol[..., :dv], sol[..., dv:]
    qk = jnp.where(incl, jnp.einsum('bhnid,bhnjd->bhnij', q, k) * decay, 0.0)
    g_last = gc[..., -1]
    q_dec = q * jnp.exp(gc)[..., None]
    k_tail = k * jnp.exp(g_last[..., None] - gc)[..., None]

    def step(s, xs):
        w_n, u_n, qk_n, q_n, k_n, gl_n = xs
        v_new = u_n - jnp.einsum('bhcd,bhde->bhce', w_n, s)
        o_n = jnp.einsum('bhcd,bhde->bhce', q_n, s) + jnp.einsum('bhij,bhje->bhie', qk_n, v_new)
        s = s * jnp.exp(gl_n)[..., None, None] + jnp.einsum('bhcd,bhce->bhde', k_n, v_new)
        return s, o_n

    xs = tuple(jnp.moveaxis(t, 2, 0) for t in (w, u, qk, q_dec, k_tail, g_last))
    s_final, o = lax.scan(step, s0.astype(f32), xs)
    return jnp.moveaxis(o, 0, 2).reshape(b, h, n, dv), s_final


def gdn_prep(p_qkv, p_beta, p_alpha, conv_w, a_log, dt_bias):
    b, n, _ = p_qkv.shape
    qkv = jax.nn.silu(dwconv(p_qkv, conv_w)).astype(jnp.float32)
    q, k, v = (t.reshape(b, n, GDN_HEADS, GDN_DIM).transpose(0, 2, 1, 3) for t in jnp.split(qkv, 3, axis=-1))
    q = l2norm(q) * GDN_DIM ** -0.5
    k = l2norm(k)
    to_dir = lambda t: t.astype(jnp.float32).reshape(b, n, 2, GDN_HEADS).transpose(2, 0, 3, 1)
    beta = jax.nn.sigmoid(to_dir(p_beta))
    g = -jnp.exp(a_log.astype(jnp.float32))[:, None, :, None] * jax.nn.softplus(
        to_dir(p_alpha) + dt_bias.astype(jnp.float32)[:, None, :, None])
    return q, k, v, g, beta


def gdn_bidir(q, k, v, g, beta, s0):
    flip = lambda t: jnp.flip(t, axis=2)
    o_f, s_f = gated_delta_chunked(q, k, v, g[0], beta[0], s0[0])
    o_b, s_b = gated_delta_chunked(flip(q), flip(k), flip(v), flip(g[1]), flip(beta[1]), s0[1])
    return o_f + flip(o_b), jnp.stack([s_f, s_b])


def gdn_out(o, gate, norm_g):
    b, h, n, d = o.shape
    o = rmsnorm(o.transpose(0, 2, 1, 3).astype(gate.dtype), norm_g)
    return (o * jax.nn.silu(gate.reshape(b, n, h, d))).reshape(b, n, h * d)


def diff_lambda_value(lam_p, lam_init):
    lp = lam_p.astype(jnp.float32)
    return jnp.exp(jnp.sum(lp[0] * lp[1])) - jnp.exp(jnp.sum(lp[2] * lp[3])) + lam_init


def diff_attention(q, k, v, lam):
    s = jnp.einsum('bqhtd,bmhtd->bhtqm', q, k).astype(jnp.float32) * DIFF_DIM ** -0.5
    p = jax.nn.softmax(s, axis=-1)
    a = p[:, :, 0] - lam * p[:, :, 1]
    return jnp.einsum('bhqm,bmhe->bqhe', a.astype(v.dtype), v)


def diff_attention_blocks(q, k, v, lam):
    b, n = q.shape[:2]
    qb = jnp.moveaxis(q.reshape(b, n // Q_BLOCK, Q_BLOCK, *q.shape[2:]), 1, 0)
    o = lax.map(lambda t: diff_attention(t, k, v, lam), qb)
    return jnp.moveaxis(o, 0, 1).reshape(b, n, *o.shape[3:])


def diff_out(o, norm_g, lam_init):
    b, n, h, d = o.shape
    return (rmsnorm(o, norm_g) * (1.0 - lam_init)).reshape(b, n, h * d)


def rope_pairs(t, rope):
    b, n, h, two, d = t.shape
    return rope2d(t.reshape(b, n, h * two, d), *rope).reshape(b, n, h, two, d)


def sink_probs(s, sink):
    col = jnp.broadcast_to(sink.astype(jnp.float32).reshape(SWA_KV_HEADS, SWA_GROUP, 1, 1), s.shape[:-1] + (1,))
    return jax.nn.softmax(jnp.concatenate([col, s], axis=-1), axis=-1)[..., 1:]


def swa_context(q, k, v, sink):
    b, n = q.shape[:2]
    qg = q.reshape(b, n, SWA_KV_HEADS, SWA_GROUP, HEAD_DIM)
    s = jnp.einsum('bqkgd,bmkd->bkgqm', qg, k).astype(jnp.float32) * HEAD_DIM ** -0.5
    p = sink_probs(s, sink)
    return jnp.einsum('bkgqm,bmkd->bqkgd', p.astype(v.dtype), v).reshape(b, n, SWA_HEADS * HEAD_DIM)


def swa_latent(q, k, v, kc, vc, sink):
    b, n = q.shape[:2]
    nb = n // SWA_BLOCK
    qb = jnp.moveaxis(q.reshape(b, nb, SWA_BLOCK, SWA_KV_HEADS, SWA_GROUP, HEAD_DIM), 1, 0)
    pad = ((0, 0), (SWA_BLOCK, SWA_BLOCK), (0, 0), (0, 0))
    kp, vp = jnp.pad(k, pad), jnp.pad(v, pad)
    span = 3 * SWA_BLOCK
    qi = jnp.arange(SWA_BLOCK)[:, None]
    m = jnp.arange(span)[None, :]
    in_window = jnp.abs(qi + SWA_BLOCK - m) <= SWA_WINDOW
    ctx_ok = jnp.ones((SWA_BLOCK, kc.shape[1]), bool)

    def block(args):
        i, qblk = args
        start = i * SWA_BLOCK
        kw = jnp.concatenate([lax.dynamic_slice_in_dim(kp, start, span, axis=1), kc], axis=1)
        vw = jnp.concatenate([lax.dynamic_slice_in_dim(vp, start, span, axis=1), vc], axis=1)
        j = start - SWA_BLOCK + m
        valid = jnp.concatenate([in_window & (j >= 0) & (j < n), ctx_ok], axis=1)
        s = jnp.einsum('bqkgd,bmkd->bkgqm', qblk, kw).astype(jnp.float32) * HEAD_DIM ** -0.5
        p = sink_probs(jnp.where(valid, s, NEG_INF), sink)
        return jnp.einsum('bkgqm,bmkd->bqkgd', p.astype(vw.dtype), vw)

    o = lax.map(block, (jnp.arange(nb), qb))
    return jnp.moveaxis(o, 0, 1).reshape(b, n, SWA_HEADS * HEAD_DIM)


def hyena_filters(n, w1, b1, w2, b2, w3, freq):
    f32 = jnp.float32
    pos = jnp.arange(n, dtype=f32)
    t = pos / max(n - 1, 1)
    ang = (2.0 * math.pi * pos / n)[:, None] * jnp.linspace(1e-4, HY_BANDS - 1, HY_BANDS, dtype=f32)[None, :]
    feats = jnp.concatenate([t[:, None], jnp.cos(ang), -jnp.sin(ang)], axis=-1)
    h = jnp.sin(freq[0].astype(f32) * (feats @ w1.astype(f32) + b1.astype(f32)))
    h = jnp.sin(freq[1].astype(f32) * (h @ w2.astype(f32) + b2.astype(f32)))
    h = (h @ w3.astype(f32)).reshape(n, HY_ORDER, 2, HY_CH)
    deltas = jnp.abs(jnp.linspace(HY_MIN_DECAY, HY_MAX_DECAY, HY_CH, dtype=f32))
    h = h * jnp.exp(-t[:, None, None, None] * deltas)
    kern = jnp.concatenate([h[:, :, 0], jnp.zeros((1, HY_ORDER, HY_CH), f32), jnp.flip(h[1:, :, 1], axis=0)], axis=0)
    kern = kern * lax.rsqrt(jnp.sum(kern * kern, axis=0, keepdims=True) + EPS)
    return jnp.fft.rfft(kern, axis=0)


def fft_long_conv(z, kf):
    n = z.shape[1]
    zf = jnp.fft.rfft(z.astype(jnp.float32), n=2 * n, axis=1)
    return jnp.fft.irfft(zf * kf[None], n=2 * n, axis=1)[:, :n].astype(z.dtype)


def hyena(u, conv_w, kf, bias):
    v, x1, x2 = jnp.split(dwconv(u, conv_w), 3, axis=-1)
    z = v
    for o, gate in enumerate((x1, x2)):
        z = gate * (fft_long_conv(z, kf[:, o]) + z * bias[o])
    return z


def mixer_ab(hx, hz, w_in, w_out, conv_w, a_log, dt_bias, gdn_g, lam_p, diff_g, lam_init, rope, with_ctx_out):
    b, n, _ = hx.shape
    nc = hz.shape[1]
    px = split_cols(hx @ w_in, AB_SPLITS)
    pz = split_cols(hz @ w_in, AB_SPLITS)
    qz, kz, vz, gz, bz = gdn_prep(pz[0], pz[2], pz[3], conv_w, a_log, dt_bias)
    qx, kx, vx, gx, bx = gdn_prep(px[0], px[2], px[3], conv_w, a_log, dt_bias)
    s0 = jnp.zeros((2, b, GDN_HEADS, GDN_DIM, GDN_DIM), jnp.float32)
    oaz, s_ctx = gdn_bidir(qz, kz, vz, gz, bz, s0)
    oax, _ = gdn_bidir(qx, kx, vx, gx, bx, s_ctx)
    lam = diff_lambda_value(lam_p, lam_init)
    dq_x = rope_pairs(px[4].reshape(b, n, DIFF_HEADS, 2, DIFF_DIM), rope)
    dk_x = rope_pairs(px[5].reshape(b, n, DIFF_HEADS, 2, DIFF_DIM), rope)
    dv_x = px[6].reshape(b, n, DIFF_HEADS, DIFF_VDIM)
    dk_z = pz[5].reshape(b, nc, DIFF_HEADS, 2, DIFF_DIM)
    dv_z = pz[6].reshape(b, nc, DIFF_HEADS, DIFF_VDIM)
    obx = diff_attention_blocks(dq_x, jnp.concatenate([dk_x, dk_z], axis=1), jnp.concatenate([dv_x, dv_z], axis=1), lam)
    out_x = jnp.concatenate([gdn_out(oax, px[1], gdn_g), diff_out(obx, diff_g, lam_init)], axis=-1) @ w_out
    if not with_ctx_out:
        return out_x, None
    dq_z = pz[4].reshape(b, nc, DIFF_HEADS, 2, DIFF_DIM)
    obz = diff_attention(dq_z, dk_z, dv_z, lam)
    out_z = jnp.concatenate([gdn_out(oaz, pz[1], gdn_g), diff_out(obz, diff_g, lam_init)], axis=-1) @ w_out
    return out_x, out_z


def mixer_cd(hx, hz, w_in, w_out, sink, hy_conv, hy_w1, hy_b1, hy_w2, hy_b2, hy_w3, hy_freq, hy_bias, rope, with_ctx_out):
    b, n, _ = hx.shape
    nc = hz.shape[1]
    qx, kx, vx, ux = split_cols(hx @ w_in, CD_SPLITS)
    qx = rope2d(qx.reshape(b, n, SWA_HEADS, HEAD_DIM), *rope)
    kx = rope2d(kx.reshape(b, n, SWA_KV_HEADS, HEAD_DIM), *rope)
    vx = vx.reshape(b, n, SWA_KV_HEADS, HEAD_DIM)
    if with_ctx_out:
        qz, kz, vz, uz = split_cols(hz @ w_in, CD_SPLITS)
    else:
        kv_lo = CD_SPLITS[0]
        kz, vz = split_cols(hz @ w_in[:, kv_lo:kv_lo + CD_SPLITS[1] + CD_SPLITS[2]], CD_SPLITS[1:3])
    kz = kz.reshape(b, nc, SWA_KV_HEADS, HEAD_DIM)
    vz = vz.reshape(b, nc, SWA_KV_HEADS, HEAD_DIM)
    ocx = swa_latent(qx, kx, vx, kz, vz, sink)
    odx = hyena(ux, hy_conv, hyena_filters(n, hy_w1, hy_b1, hy_w2, hy_b2, hy_w3, hy_freq), hy_bias)
    out_x = jnp.concatenate([ocx, odx], axis=-1) @ w_out
    if not with_ctx_out:
        return out_x, None
    ocz = swa_context(qz.reshape(b, nc, SWA_HEADS, HEAD_DIM), kz, vz, sink)
    odz = hyena(uz, hy_conv, hyena_filters(nc, hy_w1, hy_b1, hy_w2, hy_b2, hy_w3, hy_freq), hy_bias)
    out_z = jnp.concatenate([ocz, odz], axis=-1) @ w_out
    return out_x, out_z


def conv_ffn(h, w_up, conv_w, w_down):
    a, g = jnp.split(dwconv(h @ w_up, conv_w), 2, axis=-1)
    return (jax.nn.silu(g) * a) @ w_down


def setup_inputs(seed: int = 0) -> dict:
    key = jax.random.key(seed)
    ks = iter(jax.random.split(key, 40))
    f32 = jnp.float32
    nrm = lambda shape, scale: jax.random.normal(next(ks), shape, f32) * scale
    n_even, n_odd = (DEPTH + 1) // 2, DEPTH // 2
    d = D_MODEL
    p_ab, p_cd = sum(AB_SPLITS), sum(CD_SPLITS)
    dt = jnp.exp(jax.random.uniform(next(ks), (n_even, 2, GDN_HEADS), f32, minval=math.log(1e-3), maxval=math.log(1e-1)))
    return {
        'x': nrm((BATCH, SEQ, d), 1.0),
        'c': nrm((BATCH, d), 1.0),
        'ctx': nrm((BATCH, CTX_LEN, d), 1.0),
        'c_ctx': nrm((d,), 1.0),
        'w_mod': nrm((DEPTH, d, N_MOD * d), 0.5 * d ** -0.5),
        'b_mod': nrm((DEPTH, N_MOD * d), 0.02),
        'norm_g': 1.0 + nrm((DEPTH, 4, d), 0.02),
        'ffn_w_up': nrm((DEPTH, d, 2 * D_FF), d ** -0.5),
        'ffn_conv': nrm((DEPTH, FFN_CONV, 2 * D_FF), FFN_CONV ** -0.5),
        'ffn_w_down': nrm((DEPTH, D_FF, d), D_FF ** -0.5),
        'ab_w_in': nrm((n_even, d, p_ab), d ** -0.5),
        'ab_w_out': nrm((n_even, AB_OUT, d), AB_OUT ** -0.5),
        'gdn_conv': nrm((n_even, SHORT_CONV, 3 * GDN_HEADS * GDN_DIM), SHORT_CONV ** -0.5),
        'gdn_a_log': jnp.log(jax.random.uniform(next(ks), (n_even, 2, GDN_HEADS), f32, minval=1.0, maxval=16.0)),
        'gdn_dt_bias': dt + jnp.log(-jnp.expm1(-dt)),
        'gdn_norm_g': 1.0 + nrm((n_even, GDN_DIM), 0.02),
        'diff_lambda': nrm((n_even, 4, DIFF_DIM), 0.1),
        'diff_norm_g': 1.0 + nrm((n_even, DIFF_VDIM), 0.02),
        'cd_w_in': nrm((n_odd, d, p_cd), d ** -0.5),
        'cd_w_out': nrm((n_odd, CD_OUT, d), CD_OUT ** -0.5),
        'swa_sink': nrm((n_odd, SWA_HEADS), 1.0),
        'hy_conv': nrm((n_odd, SHORT_CONV, 3 * HY_CH), SHORT_CONV ** -0.5),
        'hy_w1': nrm((n_odd, HY_EMB, HY_HID), HY_EMB ** -0.5),
        'hy_b1': nrm((n_odd, HY_HID), 0.1),
        'hy_w2': nrm((n_odd, HY_HID, HY_HID), HY_HID ** -0.5),
        'hy_b2': nrm((n_odd, HY_HID), 0.1),
        'hy_w3': nrm((n_odd, HY_HID, HY_ORDER * 2 * HY_CH), HY_HID ** -0.5),
        'hy_freq': 1.0 + nrm((n_odd, 2, HY_HID), 0.1),
        'hy_bias': nrm((n_odd, HY_ORDER, HY_CH), 1.0),
    }


def reference(x, c, ctx, c_ctx, w_mod, b_mod, norm_g, ffn_w_up, ffn_conv, ffn_w_down,
              ab_w_in, ab_w_out, gdn_conv, gdn_a_log, gdn_dt_bias, gdn_norm_g, diff_lambda, diff_norm_g,
              cd_w_in, cd_w_out, swa_sink, hy_conv, hy_w1, hy_b1, hy_w2, hy_b2, hy_w3, hy_freq, hy_bias):
    b, n, d = x.shape
    rope = axial_rope_angles(n)
    z = ctx
    sc, scc = jax.nn.silu(c), jax.nn.silu(c_ctx)
    for l in range(DEPTH):
        last = l == DEPTH - 1
        i = l // 2
        mx = (sc @ w_mod[l] + b_mod[l]).reshape(b, 1, N_MOD, d)
        mz = (scc @ w_mod[l] + b_mod[l]).reshape(1, 1, N_MOD, d)
        hx = modulate(rmsnorm(x, norm_g[l, 0]), mx[:, :, 0], mx[:, :, 1])
        hz = modulate(rmsnorm(z, norm_g[l, 0]), mz[:, :, 0], mz[:, :, 1])
        if l % 2 == 0:
            lam_init = 0.8 - 0.6 * math.exp(-0.3 * l)
            ox, oz = mixer_ab(hx, hz, ab_w_in[i], ab_w_out[i], gdn_conv[i], gdn_a_log[i], gdn_dt_bias[i], gdn_norm_g[i],
                              diff_lambda[i], diff_norm_g[i], lam_init, rope, not last)
        else:
            ox, oz = mixer_cd(hx, hz, cd_w_in[i], cd_w_out[i], swa_sink[i], hy_conv[i], hy_w1[i], hy_b1[i], hy_w2[i],
                              hy_b2[i], hy_w3[i], hy_freq[i], hy_bias[i], rope, not last)
        x = x + mx[:, :, 2] * rmsnorm(ox, norm_g[l, 1])
        hx = modulate(rmsnorm(x, norm_g[l, 2]), mx[:, :, 3], mx[:, :, 4])
        x = x + mx[:, :, 5] * rmsnorm(conv_ffn(hx, ffn_w_up[l], ffn_conv[l], ffn_w_down[l]), norm_g[l, 3])
        if not last:
            z = z + mz[:, :, 2] * rmsnorm(oz, norm_g[l, 1])
            hz = modulate(rmsnorm(z, norm_g[l, 2]), mz[:, :, 3], mz[:, :, 4])
            z = z + mz[:, :, 5] * rmsnorm(conv_ffn(hz, ffn_w_up[l], ffn_conv[l], ffn_w_down[l]), norm_g[l, 3])
    return x
```

```python
import functools
import math

import jax
import jax.numpy as jnp
import numpy as np
from jax import lax
from jax.experimental import pallas as pl
from jax.experimental.pallas import tpu as pltpu

F32 = jnp.float32
BF16 = jnp.bfloat16

EPS = 1e-6
NEG_INF = -1e30
GRID_W = 64
HEAD_DIM = 64
ROPE_BASE = 10000.0
GDN_HEADS = 8
GDN_DIM = 64
GDN_CHUNK = 64
DIFF_HEADS = 4
DIFF_DIM = 64
SWA_HEADS = 8
SWA_KV_HEADS = 2
SWA_WINDOW = 128
SWA_BLOCK = 128
HY_BANDS = 16
HY_MIN_DECAY = math.log(1e-2) / 1.5
HY_MAX_DECAY = math.log(1e-2) / 0.3

LANES = 128
SUBLANES = 8
ROW_TILE = 256
VMEM_LIMIT = 56 * 1024 * 1024


def _cparams(sem):
    return pltpu.CompilerParams(dimension_semantics=sem, vmem_limit_bytes=VMEM_LIMIT)


def _resident(shape):
    zeros = (0,) * len(shape)
    return pl.BlockSpec(shape, lambda *_: zeros, pipeline_mode=pl.Buffered(1))


def _sigmoid(x):
    return 1.0 / (1.0 + jnp.exp(-x))


def _silu(x):
    return x * _sigmoid(x)


def _softplus(x):
    return jnp.maximum(x, 0.0) + jnp.log1p(jnp.exp(-jnp.abs(x)))


def _dot(a, b):
    return jnp.dot(a, b, preferred_element_type=F32)


def _dot_nt(a, b):
    return lax.dot_general(a, b, (((1,), (1,)), ((), ())), preferred_element_type=F32)


def _dot_tn(a, b):
    return lax.dot_general(a, b, (((0,), (0,)), ((), ())), preferred_element_type=F32)


def _dot_f32(a, b):
    return jnp.dot(a, b, preferred_element_type=F32, precision=lax.Precision.HIGHEST)


def _split2(x):
    hi = x.astype(BF16)
    lo = (x - hi.astype(F32)).astype(BF16)
    return hi, lo


def _split3(x):
    hi = x.astype(BF16)
    r = x - hi.astype(F32)
    mid = r.astype(BF16)
    lo = (r - mid.astype(F32)).astype(BF16)
    return hi, mid, lo


def _dot_sel(x, sel_bf16):
    hi, mid, lo = _split3(x)
    return _dot(hi, sel_bf16) + _dot(mid, sel_bf16) + _dot(lo, sel_bf16)


def _mm(a, b, passes):
    if passes == 1:
        return _dot(a.astype(BF16), b.astype(BF16))
    ah, al = _split2(a)
    bh, bl = _split2(b)
    return _dot(ah, bh) + _dot(al, bh) + _dot(ah, bl)


def _rms(y, g):
    return y * lax.rsqrt(jnp.mean(y * y, axis=-1, keepdims=True) + EPS) * g


def _modnorm(x, g, shift, scale):
    return _rms(x, g) * (1.0 + scale) + shift


def _mod_kernel(cc_ref, w_ref, b_ref, o_ref):
    s = _silu(cc_ref[...])
    o_ref[0] = _dot(s.astype(BF16), w_ref[0].astype(BF16)) + b_ref[0]


def _mod_call(cc, w_mod, b_mod):
    depth, d, nm = w_mod.shape
    rows = cc.shape[0]
    ct = 1536
    return pl.pallas_call(
        _mod_kernel,
        grid=(depth, nm // ct),
        in_specs=[
            pl.BlockSpec((rows, d), lambda l, j: (0, 0)),
            pl.BlockSpec((1, d, ct), lambda l, j: (l, 0, j)),
            pl.BlockSpec((1, 1, ct), lambda l, j: (l, 0, j)),
        ],
        out_specs=pl.BlockSpec((1, rows, ct), lambda l, j: (l, 0, j)),
        out_shape=jax.ShapeDtypeStruct((depth, rows, nm), F32),
        compiler_params=_cparams(("arbitrary", "arbitrary")),
        name="adaln_mod",
    )(cc, w_mod, b_mod.reshape(depth, 1, nm))


def _proj_kernel(x_ref, mod_ref, g_ref, w_ref, cos_ref, sin_ref, *out_refs, segs):
    m = mod_ref[0, 0]
    h = _modnorm(x_ref[0], g_ref[...], m[0:1], m[1:2]).astype(BF16)
    for o_ref, (start, width, rot_start) in zip(out_refs, segs):
        y = _dot(h, w_ref[:, start:start + width])
        if rot_start is not None:
            yr = _dot(h, w_ref[:, rot_start:rot_start + width])
            reps = width // LANES
            cos = jnp.concatenate([cos_ref[...]] * reps, axis=1)
            sin = jnp.concatenate([sin_ref[...]] * reps, axis=1)
            y = y * cos + yr * sin
        o_ref[0] = y


def _proj_call(xz, mod, g, w, cos_t, sin_t, segs, n_x_tiles, name):
    b, l, d = xz.shape
    tm = ROW_TILE
    nt = l // tm
    p = w.shape[1]
    return pl.pallas_call(
        functools.partial(_proj_kernel, segs=segs),
        grid=(nt, b),
        in_specs=[
            pl.BlockSpec((1, tm, d), lambda t, i: (i, t, 0)),
            pl.BlockSpec((1, 1, 6, d), lambda t, i: (i, t // n_x_tiles, 0, 0)),
            pl.BlockSpec((1, d), lambda t, i: (0, 0)),
            _resident((d, p)),
            pl.BlockSpec((tm, LANES), lambda t, i: (t, 0)),
            pl.BlockSpec((tm, LANES), lambda t, i: (t, 0)),
        ],
        out_specs=[pl.BlockSpec((1, tm, wd), lambda t, i: (i, t, 0)) for (_, wd, _) in segs],
        out_shape=[jax.ShapeDtypeStruct((b, l, wd), F32) for (_, wd, _) in segs],
        compiler_params=_cparams(("arbitrary", "arbitrary")),
        name=name,
    )(xz, mod, g.reshape(1, d), w, cos_t, sin_t)


def _outproj_kernel(o1_ref, o2_ref, w_ref, x_ref, mod_ref, g_ref, out_ref):
    k1 = o1_ref.shape[-1]
    y = _dot(o1_ref[0].astype(BF16), w_ref[0:k1, :]) + _dot(o2_ref[0].astype(BF16), w_ref[k1:, :])
    m = mod_ref[0, 0]
    out_ref[0] = x_ref[0] + m[2:3] * _rms(y, g_ref[...])


def _outproj_call(o1, o2, w, xz, mod, g, n_tiles, n_x_tiles):
    b, _, d = xz.shape
    tm = ROW_TILE
    k1, k2 = o1.shape[-1], o2.shape[-1]
    return pl.pallas_call(
        _outproj_kernel,
        grid=(n_tiles, b),
        in_specs=[
            pl.BlockSpec((1, tm, k1), lambda t, i: (i, t, 0)),
            pl.BlockSpec((1, tm, k2), lambda t, i: (i, t, 0)),
            _resident((k1 + k2, d)),
            pl.BlockSpec((1, tm, d), lambda t, i: (i, t, 0)),
            pl.BlockSpec((1, 1, 6, d), lambda t, i: (i, t // n_x_tiles, 0, 0)),
            pl.BlockSpec((1, d), lambda t, i: (0, 0)),
        ],
        out_specs=pl.BlockSpec((1, tm, d), lambda t, i: (i, t, 0)),
        out_shape=jax.ShapeDtypeStruct((b, n_tiles * tm, d), F32),
        compiler_params=_cparams(("arbitrary", "arbitrary")),
        name="mixer_out",
    )(o1, o2, w, xz, mod, g.reshape(1, d))


def _ffn_kernel(xp_ref, x_ref, xn_ref, mod_ref, g2_ref, g3_ref, wup_ref, cw_ref, wdn_ref, out_ref,
                ua_ref, ug_ref, *, tm, n_x_tiles, n_tiles, cf, dff):
    t = pl.program_id(0)
    first = jnp.logical_or(t == 0, t == n_x_tiles)
    last = jnp.logical_or(t == n_x_tiles - 1, t == n_tiles - 1)
    keep_top = jnp.where(first, 0.0, 1.0)
    keep_bot = jnp.where(last, 0.0, 1.0)
    m = mod_ref[0, 0]
    halo = SUBLANES
    xe = jnp.concatenate([xp_ref[0], x_ref[0], xn_ref[0]], axis=0)
    h = _modnorm(xe, g2_ref[...], m[3:4], m[4:5]).astype(BF16)
    acc = jnp.zeros((tm, x_ref.shape[-1]), F32)
    for j in range(dff // cf):
        halves = []
        for ref, base in ((ua_ref, j * cf), (ug_ref, dff + j * cf)):
            u = _dot(h, wup_ref[:, base:base + cf])
            ref[...] = u
            ref[0:halo, :] = u[0:halo] * keep_top
            ref[tm + halo:tm + 2 * halo, :] = u[tm + halo:] * keep_bot
            cw = cw_ref[:, base:base + cf]
            halves.append(cw[0:1] * ref[halo - 1:halo - 1 + tm, :] + cw[1:2] * ref[halo:halo + tm, :]
                          + cw[2:3] * ref[halo + 1:halo + 1 + tm, :])
        act = (_silu(halves[1]) * halves[0]).astype(BF16)
        acc = acc + _dot(act, wdn_ref[j * cf:(j + 1) * cf, :])
    out_ref[0] = x_ref[0] + m[5:6] * _rms(acc, g3_ref[...])


def _ffn_call(xz, mod, g2, g3, w_up, conv_w, w_down, n_tiles, n_x_tiles):
    b, l, d = xz.shape
    tm = ROW_TILE
    dff = w_down.shape[0]
    cf = dff // 2
    hb = tm // SUBLANES
    nb8 = l // SUBLANES
    kern = functools.partial(_ffn_kernel, tm=tm, n_x_tiles=n_x_tiles, n_tiles=n_tiles, cf=cf, dff=dff)
    return pl.pallas_call(
        kern,
        grid=(n_tiles, b),
        in_specs=[
            pl.BlockSpec((1, SUBLANES, d), lambda t, i: (i, jnp.maximum(t * hb - 1, 0), 0)),
            pl.BlockSpec((1, tm, d), lambda t, i: (i, t, 0)),
            pl.BlockSpec((1, SUBLANES, d), lambda t, i: (i, jnp.minimum((t + 1) * hb, nb8 - 1), 0)),
            pl.BlockSpec((1, 1, 6, d), lambda t, i: (i, t // n_x_tiles, 0, 0)),
            pl.BlockSpec((1, d), lambda t, i: (0, 0)),
            pl.BlockSpec((1, d), lambda t, i: (0, 0)),
            _resident((d, 2 * dff)),
            pl.BlockSpec((3, 2 * dff), lambda t, i: (0, 0)),
            _resident((dff, d)),
        ],
        out_specs=pl.BlockSpec((1, tm, d), lambda t, i: (i, t, 0)),
        out_shape=jax.ShapeDtypeStruct((b, n_tiles * tm, d), F32),
        scratch_shapes=[pltpu.VMEM((tm + 2 * SUBLANES, cf), F32), pltpu.VMEM((tm + 2 * SUBLANES, cf), F32)],
        compiler_params=_cparams(("arbitrary", "arbitrary")),
        name="conv_ffn",
    )(xz, xz, xz, mod, g2.reshape(1, d), g3.reshape(1, d), w_up, conv_w, w_down)


def _half_sums(x2, lane_lo):
    s0 = jnp.sum(jnp.where(lane_lo, x2, 0.0), axis=-1, keepdims=True)
    s1 = jnp.sum(jnp.where(lane_lo, 0.0, x2), axis=-1, keepdims=True)
    return jnp.where(lane_lo, s0, s1)


def _gdn_kernel(qkvg_ref, ba_ref, cw_ref, alog_ref, dtb_ref, ng_ref, out_ref,
                pad_ref, q_ref, k_ref, v_ref, bb_ref, gb_ref, o_ref, *, n, nc, solve_passes):
    l = n + nc
    c = GDN_CHUNK
    n_chunks = l // c
    pair = pl.program_id(1)
    halo = SUBLANES
    lane = lax.broadcasted_iota(jnp.int32, (1, LANES), 1)
    lane_lo = lane < GDN_DIM

    cw = cw_ref[:, 0:3 * LANES]
    zero_rows = jnp.zeros((halo, 3 * LANES), F32)
    for seq_start, seq_len in ((0, n), (n, nc)):
        base = halo + seq_start + (2 * halo if seq_start else 0)
        pad_ref[base - halo:base, :] = zero_rows
        pad_ref[base + seq_len:base + seq_len + halo, :] = zero_rows
        step = 256
        for r in range(0, seq_len, step):
            pad_ref[base + r:base + r + step, :] = qkvg_ref[0, seq_start + r:seq_start + r + step, 0:3 * LANES]
        for r in range(0, seq_len, step):
            y = (cw[0:1] * pad_ref[base + r - 1:base + r - 1 + step, :]
                 + cw[1:2] * pad_ref[base + r:base + r + step, :]
                 + cw[2:3] * pad_ref[base + r + 1:base + r + 1 + step, :])
            y = _silu(y)
            q = y[:, 0:LANES]
            k = y[:, LANES:2 * LANES]
            rows = slice(seq_start + r, seq_start + r + step)
            q_ref[rows, :] = q * lax.rsqrt(_half_sums(q * q, lane_lo) + EPS) * (GDN_DIM ** -0.5)
            k_ref[rows, :] = k * lax.rsqrt(_half_sums(k * k, lane_lo) + EPS)
            v_ref[rows, :] = y[:, 2 * LANES:3 * LANES]

    ii = lax.broadcasted_iota(jnp.int32, (c, c), 0)
    jj = lax.broadcasted_iota(jnp.int32, (c, c), 1)
    ltri = (ii >= jj).astype(BF16)
    bwd_lane = jnp.logical_and(lane >= 16 + GDN_HEADS, lane < 16 + 2 * GDN_HEADS)
    sel_row = lax.broadcasted_iota(jnp.int32, (LANES, LANES), 0)
    alog = alog_ref[...]
    dtb = dtb_ref[...]

    def gate_body(ci, carry):
        rows = pl.ds(pl.multiple_of(ci * c, c), c)
        ba = ba_ref[0, rows, :]
        beta_all = _sigmoid(ba)
        g_all = -jnp.exp(alog) * _softplus(ba + dtb)
        gc_l = _dot_sel_lhs(ltri, g_all)
        gc_u = gc_l[c - 1:c, :] - gc_l + g_all
        gc_all = jnp.where(bwd_lane, gc_u, gc_l)
        for d in range(2):
            for hh in range(2):
                col = d * GDN_HEADS + 2 * pair + hh
                bb_ref[2 * d + hh, rows, :] = _dot_sel(beta_all, (sel_row == col).astype(BF16))
                gb_ref[2 * d + hh, rows, :] = _dot_sel(gc_all, (sel_row == 16 + col).astype(BF16))
        return carry

    lax.fori_loop(0, n_chunks, gate_body, 0)

    r2 = lax.broadcasted_iota(jnp.int32, (2 * c, 2 * c), 0)
    c2 = lax.broadcasted_iota(jnp.int32, (2 * c, 2 * c), 1)
    same_head = (r2 >= c) == (c2 >= c)
    eye = (r2 == c2).astype(F32)
    masks = ((jnp.logical_and(same_head, r2 >= c2), jnp.logical_and(same_head, r2 > c2)),
             (jnp.logical_and(same_head, r2 <= c2), jnp.logical_and(same_head, r2 < c2)))
    m0 = lane_lo.astype(F32)
    m1 = 1.0 - m0

    def stack_heads(x2):
        return jnp.concatenate([x2 * m0, x2 * m1], axis=0)

    def fold_heads(x):
        return x[0:c] + x[c:2 * c]

    def chunk_step(d, chunk, s2):
        rows = pl.ds(pl.multiple_of(chunk * c, c), c)
        incl, strict = masks[d]
        q = q_ref[rows, :]
        k = k_ref[rows, :]
        v = v_ref[rows, :]
        gcb0 = gb_ref[2 * d, rows, :]
        gcb1 = gb_ref[2 * d + 1, rows, :]
        beta2 = jnp.where(lane_lo, bb_ref[2 * d, rows, :], bb_ref[2 * d + 1, rows, :])
        gc2 = jnp.where(lane_lo, gcb0, gcb1)
        g1 = jnp.concatenate([gcb0, gcb1], axis=0)
        diff = g1 - g1.T
        decay = jnp.where(incl, jnp.exp(jnp.where(incl, diff, 0.0)), 0.0)
        kb2 = k * beta2
        kst = stack_heads(k).astype(BF16)
        a = jnp.where(strict, _dot_nt(stack_heads(kb2).astype(BF16), kst) * decay, 0.0)
        qk = jnp.where(incl, _dot_nt(stack_heads(q).astype(BF16), kst) * decay, 0.0)
        p = -a
        tinv = eye + p
        for _ in range(int(math.log2(c)) - 1):
            p = _mm(p, p, solve_passes)
            tinv = tinv + _mm(tinv, p, solve_passes)
        egc = jnp.exp(gc2)
        rhs = jnp.concatenate([stack_heads(v * beta2), stack_heads(kb2 * egc)], axis=1)
        sol = _mm(tinv, rhs, solve_passes)
        u2 = fold_heads(sol[:, 0:LANES])
        w2 = fold_heads(sol[:, LANES:2 * LANES])
        gl2 = gc2[c - 1:c, :] if d == 0 else gc2[0:1, :]
        qdec2 = q * egc
        ktail2 = k * jnp.exp(gl2 - gc2)
        ws = _dot(jnp.concatenate([w2, qdec2], axis=0).astype(BF16), s2.astype(BF16))
        vnew = u2 - ws[0:c]
        o2 = ws[c:2 * c] + fold_heads(_dot(qk.astype(BF16), stack_heads(vnew).astype(BF16)))
        o_ref[d, rows, :] = o2
        kv = _dot_tn(ktail2.astype(BF16), vnew.astype(BF16))
        return s2 * jnp.exp(gl2) + jnp.where(same_head, kv, 0.0)

    ctx_chunks = nc // c

    def scan_body(s, carry):
        sf, sb = carry
        fwd_chunk = jnp.where(s < ctx_chunks, s + n // c, s - ctx_chunks)
        bwd_chunk = n_chunks - 1 - s
        return chunk_step(0, fwd_chunk, sf), chunk_step(1, bwd_chunk, sb)

    zero_state = jnp.zeros((2 * c, 2 * c), F32)
    lax.fori_loop(0, n_chunks, scan_body, (zero_state, zero_state))

    ng = ng_ref[...]
    step = 256
    for r in range(0, l, step):
        o = o_ref[0, r:r + step, :] + o_ref[1, r:r + step, :]
        ms = _half_sums(o * o, lane_lo) * (1.0 / GDN_DIM)
        gate = qkvg_ref[0, r:r + step, 3 * LANES:4 * LANES]
        out_ref[0, r:r + step, :] = o * lax.rsqrt(ms + EPS) * ng * _silu(gate)


def _dot_sel_lhs(sel_bf16, x):
    hi, mid, lo = _split3(x)
    return _dot(sel_bf16, hi) + _dot(sel_bf16, mid) + _dot(sel_bf16, lo)


def _gdn_call(qkvg, ba, conv_w, alog, dtb, ng, n, nc, solve_passes):
    b, l, _ = qkvg.shape
    pairs = GDN_HEADS // 2
    kern = functools.partial(_gdn_kernel, n=n, nc=nc, solve_passes=solve_passes)
    return pl.pallas_call(
        kern,
        grid=(b, pairs),
        in_specs=[
            pl.BlockSpec((1, l, 4 * LANES), lambda i, p: (i, 0, p)),
            pl.BlockSpec((1, l, LANES), lambda i, p: (i, 0, 0)),
            pl.BlockSpec((3, 4 * LANES), lambda i, p: (0, p)),
            pl.BlockSpec((1, LANES), lambda i, p: (0, 0)),
            pl.BlockSpec((1, LANES), lambda i, p: (0, 0)),
            pl.BlockSpec((1, LANES), lambda i, p: (0, 0)),
        ],
        out_specs=pl.BlockSpec((1, l, LANES), lambda i, p: (i, 0, p)),
        out_shape=jax.ShapeDtypeStruct((b, l, pairs * LANES), F32),
        scratch_shapes=[
            pltpu.VMEM((l + 5 * SUBLANES, 3 * LANES), F32),
            pltpu.VMEM((l, LANES), F32),
            pltpu.VMEM((l, LANES), F32),
            pltpu.VMEM((l, LANES), F32),
            pltpu.VMEM((4, l, LANES), F32),
            pltpu.VMEM((4, l, LANES), F32),
            pltpu.VMEM((2, l, LANES), F32),
        ],
        compiler_params=_cparams(("arbitrary", "arbitrary")),
        name="gated_deltanet",
    )(qkvg, ba, conv_w, alog, dtb, ng)


def _diff_kernel(q_ref, k_ref, v_ref, lam_ref, ng_ref, o_ref, *, n, lam_init, n_x_tiles):
    t = pl.program_id(2)
    lp = lam_ref[...]
    lam = (jnp.exp(jnp.sum(lp[0:1] * lp[1:2], axis=-1, keepdims=True))
           - jnp.exp(jnp.sum(lp[2:3] * lp[3:4], axis=-1, keepdims=True)) + lam_init)
    lane = lax.broadcasted_iota(jnp.int32, (1, LANES), 1)
    lane_lo = lane < DIFF_DIM
    q = q_ref[0] * (DIFF_DIM ** -0.5)
    ng = ng_ref[...]

    def attend(k, v):
        kb = k.astype(BF16)
        probs = []
        for half in range(2):
            qt = jnp.where(lane_lo if half == 0 else jnp.logical_not(lane_lo), q, 0.0).astype(BF16)
            s = _dot_nt(qt, kb)
            e = jnp.exp(s - jnp.max(s, axis=-1, keepdims=True))
            probs.append(e * (1.0 / jnp.sum(e, axis=-1, keepdims=True)))
        a = probs[0] - lam * probs[1]
        o = _dot(a.astype(BF16), v.astype(BF16))
        o_ref[0] = _rms(o, ng) * (1.0 - lam_init)

    @pl.when(t < n_x_tiles)
    def _():
        attend(k_ref[0], v_ref[0])

    @pl.when(t >= n_x_tiles)
    def _():
        attend(k_ref[0, n:, :], v_ref[0, n:, :])


def _diff_call(dq, dk, dv, lam_p, ng, n, lam_init):
    b, l, _ = dq.shape
    tq = ROW_TILE
    nt = l // tq
    kern = functools.partial(_diff_kernel, n=n, lam_init=lam_init, n_x_tiles=n // tq)
    return pl.pallas_call(
        kern,
        grid=(b, DIFF_HEADS, nt),
        in_specs=[
            pl.BlockSpec((1, tq, LANES), lambda i, h, t: (i, t, h)),
            pl.BlockSpec((1, l, LANES), lambda i, h, t: (i, 0, h)),
            pl.BlockSpec((1, l, LANES), lambda i, h, t: (i, 0, h)),
            pl.BlockSpec((4, DIFF_DIM), lambda i, h, t: (0, 0)),
            pl.BlockSpec((1, LANES), lambda i, h, t: (0, 0)),
        ],
        out_specs=pl.BlockSpec((1, tq, LANES), lambda i, h, t: (i, t, h)),
        out_shape=jax.ShapeDtypeStruct((b, l, DIFF_HEADS * LANES), F32),
        compiler_params=_cparams(("arbitrary", "arbitrary", "arbitrary")),
        name="diff_attention",
    )(dq, dk, dv, lam_p, ng.reshape(1, LANES))


def _swa_kernel(q_ref, k_ref, v_ref, sink_ref, o_ref, *, n, nc):
    t = pl.program_id(1)
    blk = SWA_BLOCK
    n_x = n // blk
    q = q_ref[0] * (HEAD_DIM ** -0.5)
    lane = lax.broadcasted_iota(jnp.int32, (1, LANES), 1)
    lane_lo = lane < HEAD_DIM
    sink = sink_ref[...]
    group = SWA_HEADS // SWA_KV_HEADS

    def run(keys, vals, valid):
        outs = []
        for pair in range(SWA_HEADS // 2):
            kvh = (2 * pair) // group
            qp = q[:, pair * LANES:(pair + 1) * LANES]
            kk = keys[:, kvh * LANES:(kvh + 1) * LANES].astype(BF16)
            vv = vals[:, kvh * LANES:(kvh + 1) * LANES].astype(BF16)
            res = []
            for j in range(2):
                h = 2 * pair + j
                qh = jnp.where(lane_lo if j == 0 else jnp.logical_not(lane_lo), qp, 0.0).astype(BF16)
                s = _dot_nt(qh, kk)
                if valid is not None:
                    s = jnp.where(valid, s, NEG_INF)
                sk = sink[:, h:h + 1]
                mx = jnp.maximum(jnp.max(s, axis=-1, keepdims=True), sk)
                e = jnp.exp(s - mx)
                den = jnp.sum(e, axis=-1, keepdims=True) + jnp.exp(sk - mx)
                res.append(_dot((e * (1.0 / den)).astype(BF16), vv))
            outs.append(jnp.where(lane_lo, res[0], res[1]))
        o_ref[0] = jnp.concatenate(outs, axis=1)

    @pl.when(t < n_x)
    def _():
        start = pl.multiple_of(jnp.clip((t - 1) * blk, 0, n - 3 * blk), blk)
        keys = jnp.concatenate([k_ref[0, pl.ds(start, 3 * blk), :], k_ref[0, n:n + nc, :]], axis=0)
        vals = jnp.concatenate([v_ref[0, pl.ds(start, 3 * blk), :], v_ref[0, n:n + nc, :]], axis=0)
        qpos = t * blk + lax.broadcasted_iota(jnp.int32, (blk, 3 * blk + nc), 0)
        col = lax.broadcasted_iota(jnp.int32, (blk, 3 * blk + nc), 1)
        dist = qpos - (start + col)
        in_window = jnp.logical_and(dist <= SWA_WINDOW, dist >= -SWA_WINDOW)
        valid = jnp.logical_or(col >= 3 * blk, in_window)
        run(keys, vals, valid)

    @pl.when(t >= n_x)
    def _():
        run(k_ref[0, n:n + nc, :], v_ref[0, n:n + nc, :], None)


def _swa_call(q, k, v, sink, n, nc, with_ctx):
    b, l, _ = q.shape
    blk = SWA_BLOCK
    nt = (l if with_ctx else n) // blk
    kern = functools.partial(_swa_kernel, n=n, nc=nc)
    return pl.pallas_call(
        kern,
        grid=(b, nt),
        in_specs=[
            pl.BlockSpec((1, blk, SWA_HEADS * HEAD_DIM), lambda i, t: (i, t, 0)),
            pl.BlockSpec((1, l, 2 * LANES), lambda i, t: (i, 0, 0)),
            pl.BlockSpec((1, l, 2 * LANES), lambda i, t: (i, 0, 0)),
            pl.BlockSpec((1, LANES), lambda i, t: (0, 0)),
        ],
        out_specs=pl.BlockSpec((1, blk, SWA_HEADS * HEAD_DIM), lambda i, t: (i, t, 0)),
        out_shape=jax.ShapeDtypeStruct((b, l, SWA_HEADS * HEAD_DIM), F32),
        compiler_params=_cparams(("arbitrary", "arbitrary")),
        name="window_attention",
    )(q, k, v, sink)


def _dft_mats(n):
    k = jnp.arange(n, dtype=jnp.int32)
    km = (k[:, None] * k[None, :]) % (2 * n)
    ang = km.astype(F32) * (math.pi / n)
    return jnp.cos(ang).astype(BF16), (-jnp.sin(ang)).astype(BF16)


def _hyena_filter_kernel(ff_ref, fb_ref, w1_ref, b1_ref, w2_ref, b2_ref, freq_ref, w3f_ref, w3b_ref,
                         dl_ref, c_ref, s_ref, kr_ref, ki_ref, kn_ref):
    n = ff_ref.shape[0]
    freq = freq_ref[...]

    def mlp(feat):
        h = jnp.sin(freq[0:1] * (_dot_f32(feat, w1_ref[...]) + b1_ref[...]))
        return jnp.sin(freq[1:2] * (_dot_f32(h, w2_ref[...]) + b2_ref[...]))

    ff = ff_ref[...]
    fb = fb_ref[...]
    dl = dl_ref[...]
    row = lax.broadcasted_iota(jnp.int32, (n, 1), 0)
    kf = _dot_f32(mlp(ff), w3f_ref[...]) * jnp.exp(-ff[:, 0:1] * dl)
    kb = _dot_f32(mlp(fb), w3b_ref[...]) * jnp.exp(-fb[:, 0:1] * dl)
    kb = jnp.where(row == 0, 0.0, kb)
    ss = jnp.sum(kf * kf, axis=0, keepdims=True) + jnp.sum(kb * kb, axis=0, keepdims=True)
    sc = lax.rsqrt(ss + EPS)
    kf = kf * sc
    kb = kb * sc
    sgn = jnp.where((row & 1) == 0, 1.0, -1.0)
    cm = c_ref[...]
    sm = s_ref[...]
    fh, fl = _split2(kf)
    bh, bl = _split2(kb)
    kr_ref[0] = _dot(cm, fh) + _dot(cm, fl) + sgn * (_dot(cm, bh) + _dot(cm, bl))
    ki_ref[0] = _dot(sm, fh) + _dot(sm, fl) + sgn * (_dot(sm, bh) + _dot(sm, bl))
    nyq = jnp.sum((kf + kb) * sgn, axis=0, keepdims=True)
    kn_ref[0] = jnp.broadcast_to(nyq, (SUBLANES, nyq.shape[-1]))


def _hyena_filter_call(featf, featb, w1, b1, w2, b2, freq, w3, deltas, cm, sm):
    n = featf.shape[0]
    hid = w2.shape[0]
    ch = deltas.shape[-1]
    tc = 2 * LANES
    nct = ch // tc
    return pl.pallas_call(
        _hyena_filter_kernel,
        grid=(2, nct),
        in_specs=[
            pl.BlockSpec((n, hid), lambda o, j: (0, 0)),
            pl.BlockSpec((n, hid), lambda o, j: (0, 0)),
            pl.BlockSpec((hid, hid), lambda o, j: (0, 0)),
            pl.BlockSpec((1, hid), lambda o, j: (0, 0)),
            pl.BlockSpec((hid, hid), lambda o, j: (0, 0)),
            pl.BlockSpec((1, hid), lambda o, j: (0, 0)),
            pl.BlockSpec((2, hid), lambda o, j: (0, 0)),
            pl.BlockSpec((hid, tc), lambda o, j: (0, (2 * o) * nct + j)),
            pl.BlockSpec((hid, tc), lambda o, j: (0, (2 * o + 1) * nct + j)),
            pl.BlockSpec((1, tc), lambda o, j: (0, j)),
            _resident((n, n)),
            _resident((n, n)),
        ],
        out_specs=[
            pl.BlockSpec((1, n, tc), lambda o, j: (o, 0, j)),
            pl.BlockSpec((1, n, tc), lambda o, j: (o, 0, j)),
            pl.BlockSpec((1, SUBLANES, tc), lambda o, j: (o, 0, j)),
        ],
        out_shape=[
            jax.ShapeDtypeStruct((2, n, ch), F32),
            jax.ShapeDtypeStruct((2, n, ch), F32),
            jax.ShapeDtypeStruct((2, SUBLANES, ch), F32),
        ],
        compiler_params=_cparams(("arbitrary", "arbitrary")),
        name="hyena_filters",
    )(featf, featb, w1, b1, w2, b2, freq, w3, w3, deltas, cm, sm)


def _hyena_kernel(*refs, n, aliased):
    if aliased:
        refs = refs[1:]
    (v_ref, x1_ref, x2_ref, cwv_ref, cw1_ref, cw2_ref, kr_ref, ki_ref, kn_ref, bias_ref, c_ref, s_ref,
     o_ref, pad_ref) = refs
    halo = SUBLANES
    tc = o_ref.shape[-1]
    zero_rows = jnp.zeros((halo, tc), F32)

    def conv3(ref, cw_ref):
        pad_ref[0:halo, :] = zero_rows
        pad_ref[halo + n:2 * halo + n, :] = zero_rows
        pad_ref[halo:halo + n, :] = ref[0]
        cw = cw_ref[...]
        return (cw[0:1] * pad_ref[halo - 1:halo - 1 + n, :] + cw[1:2] * pad_ref[halo:halo + n, :]
                + cw[2:3] * pad_ref[halo + 1:halo + 1 + n, :])

    z = conv3(v_ref, cwv_ref)
    gates = (conv3(x1_ref, cw1_ref), conv3(x2_ref, cw2_ref))
    row = lax.broadcasted_iota(jnp.int32, (n, 1), 0)
    sgn = jnp.where((row & 1) == 0, 1.0, -1.0)
    wgt = jnp.where(row == 0, 0.5 / n, 1.0 / n)
    cm = c_ref[...]
    sm = s_ref[...]
    for o in range(2):
        zb = z.astype(BF16)
        zr = _dot(cm, zb)
        zi = _dot(sm, zb)
        znyq = jnp.sum(z * sgn, axis=0, keepdims=True)
        kr = kr_ref[o]
        ki = ki_ref[o]
        pr = ((zr * kr - zi * ki) * wgt).astype(BF16)
        pi = ((zr * ki + zi * kr) * wgt).astype(BF16)
        y = _dot(cm, pr) + _dot(sm, pi) + sgn * (znyq * kn_ref[o, 0:1, :] * (0.5 / n))
        z = gates[o] * (y + z * bias_ref[o:o + 1, :])
    o_ref[0] = z


def _hyena_call(u, conv_w, kr, ki, kn, bias, cm, sm, n, row_block, prev_out):
    b, l, _ = u.shape
    ch = bias.shape[-1]
    tc = LANES
    nct = ch // tc
    aliased = prev_out is not None
    kern = functools.partial(_hyena_kernel, n=n, aliased=aliased)
    in_specs = [
        pl.BlockSpec((1, n, tc), lambda j, i: (i, row_block, j)),
        pl.BlockSpec((1, n, tc), lambda j, i: (i, row_block, nct + j)),
        pl.BlockSpec((1, n, tc), lambda j, i: (i, row_block, 2 * nct + j)),
        pl.BlockSpec((3, tc), lambda j, i: (0, j)),
        pl.BlockSpec((3, tc), lambda j, i: (0, nct + j)),
        pl.BlockSpec((3, tc), lambda j, i: (0, 2 * nct + j)),
        pl.BlockSpec((2, n, tc), lambda j, i: (0, 0, j)),
        pl.BlockSpec((2, n, tc), lambda j, i: (0, 0, j)),
        pl.BlockSpec((2, SUBLANES, tc), lambda j, i: (0, 0, j)),
        pl.BlockSpec((2, tc), lambda j, i: (0, j)),
        _resident((n, n)),
        _resident((n, n)),
    ]
    args = [u, u, u, conv_w, conv_w, conv_w, kr, ki, kn, bias, cm, sm]
    aliases = {}
    if aliased:
        in_specs = [pl.BlockSpec(memory_space=pl.ANY)] + in_specs
        args = [prev_out] + args
        aliases = {0: 0}
    return pl.pallas_call(
        kern,
        grid=(nct, b),
        in_specs=in_specs,
        out_specs=pl.BlockSpec((1, n, tc), lambda j, i: (i, row_block, j)),
        out_shape=jax.ShapeDtypeStruct((b, l, ch), F32),
        scratch_shapes=[pltpu.VMEM((n + 2 * SUBLANES, tc), F32)],
        input_output_aliases=aliases,
        compiler_params=_cparams(("arbitrary", "arbitrary")),
        name="hyena_conv_n%d" % n,
    )(*args)


def _rope_tables(n, nc):
    rows = n // GRID_W
    row = jnp.repeat(jnp.arange(rows, dtype=F32), GRID_W)
    col = jnp.tile(jnp.arange(GRID_W, dtype=F32), rows)
    half = HEAD_DIM // 2
    inv = ROPE_BASE ** (-jnp.arange(0, half, 2, dtype=F32) / half)
    ar = row[:, None] * inv
    ac = col[:, None] * inv
    cos = jnp.concatenate([jnp.cos(ar), jnp.cos(ar), jnp.cos(ac), jnp.cos(ac)], axis=-1)
    sin = jnp.concatenate([-jnp.sin(ar), jnp.sin(ar), -jnp.sin(ac), jnp.sin(ac)], axis=-1)
    cos = jnp.concatenate([cos, jnp.ones((nc, HEAD_DIM), F32)], axis=0)
    sin = jnp.concatenate([sin, jnp.zeros((nc, HEAD_DIM), F32)], axis=0)
    return jnp.tile(cos, (1, LANES // HEAD_DIM)), jnp.tile(sin, (1, LANES // HEAD_DIM))


def _rope_partner_cols(width):
    d = np.arange(width)
    quarter = HEAD_DIM // 4
    return np.where((d % (2 * quarter)) < quarter, d + quarter, d - quarter)


def _hyena_feats(n):
    pos = jnp.arange(n, dtype=F32)
    t = pos / max(n - 1, 1)
    ang = (2.0 * math.pi * pos / n)[:, None] * jnp.linspace(1e-4, HY_BANDS - 1, HY_BANDS, dtype=F32)[None, :]
    feats = jnp.concatenate([t[:, None], jnp.cos(ang), -jnp.sin(ang)], axis=-1)
    feats = jnp.pad(feats, ((0, 0), (0, 64 - feats.shape[-1])))
    back = jnp.concatenate([feats[0:1], jnp.flip(feats[1:], axis=0)], axis=0)
    return feats, back


def _pad_cols(w, width):
    return jnp.pad(w, ((0, 0), (0, width - w.shape[-1])))


def _layer_ab(xz, mod, norm_g0, norm_g1, w_in, w_out, conv_w, a_log, dt_bias, gdn_g, lam_p, diff_g, lam_init,
              rope, n, nc, solve_passes):
    hd = GDN_HEADS * GDN_DIM
    wq, wk, wv, wg = (w_in[:, i * hd:(i + 1) * hd] for i in range(4))
    o = 4 * hd
    w_beta, w_alpha = w_in[:, o:o + 16], w_in[:, o + 16:o + 32]
    o += 32
    dd = DIFF_HEADS * 2 * DIFF_DIM
    wdq, wdk, wdv = (w_in[:, o + i * dd:o + (i + 1) * dd] for i in range(3))
    pairs = GDN_HEADS // 2
    pair_cols = lambda w: [w[:, p * LANES:(p + 1) * LANES] for p in range(pairs)]
    w_qkvg = jnp.concatenate([blk for grp in zip(pair_cols(wq), pair_cols(wk), pair_cols(wv), pair_cols(wg))
                              for blk in grp], axis=1)
    perm = _rope_partner_cols(dd)
    w_all = jnp.concatenate([w_qkvg, _pad_cols(jnp.concatenate([w_beta, w_alpha], axis=1), LANES),
                             wdq, wdk, wdv, wdq[:, perm], wdk[:, perm]], axis=1).astype(BF16)
    c0 = 4 * hd
    segs = ((0, c0, None), (c0, LANES, None), (c0 + LANES, dd, c0 + LANES + 3 * dd),
            (c0 + LANES + dd, dd, c0 + LANES + 4 * dd), (c0 + LANES + 2 * dd, dd, None))
    qkvg, ba, dq, dk, dv = _proj_call(xz, mod, norm_g0, w_all, rope[0], rope[1], segs, n // ROW_TILE, "proj_ab")

    cq, ck, cv = (conv_w[:, i * hd:(i + 1) * hd] for i in range(3))
    zeros = jnp.zeros((3, LANES), F32)
    conv_l = jnp.concatenate([blk for p in range(pairs) for blk in
                              (cq[:, p * LANES:(p + 1) * LANES], ck[:, p * LANES:(p + 1) * LANES],
                               cv[:, p * LANES:(p + 1) * LANES], zeros)], axis=1)
    alog = jnp.pad(a_log.reshape(1, 16), ((0, 0), (16, LANES - 32)))
    dtb = jnp.pad(dt_bias.reshape(1, 16), ((0, 0), (16, LANES - 32)))
    ng = jnp.tile(gdn_g.reshape(1, GDN_DIM), (1, 2))
    oa = _gdn_call(qkvg, ba, conv_l, alog, dtb, ng, n, nc, solve_passes)
    ob = _diff_call(dq, dk, dv, lam_p, diff_g, n, lam_init)
    l = n + nc
    return _outproj_call(oa, ob, w_out.astype(BF16), xz, mod, norm_g1, l // ROW_TILE, n // ROW_TILE)


def _layer_cd(xz, mod, norm_g0, norm_g1, w_in, w_out, sink, hy_conv, hy_w1, hy_b1, hy_w2, hy_b2, hy_w3, hy_freq,
              hy_bias, rope, n, nc, last, dft_x, dft_c):
    qd = SWA_HEADS * HEAD_DIM
    kd = SWA_KV_HEADS * HEAD_DIM
    wq, wk, wv, wu = w_in[:, 0:qd], w_in[:, qd:qd + kd], w_in[:, qd + kd:qd + 2 * kd], w_in[:, qd + 2 * kd:]
    dup = lambda w: jnp.concatenate([w[:, 0:HEAD_DIM], w[:, 0:HEAD_DIM], w[:, HEAD_DIM:], w[:, HEAD_DIM:]], axis=1)
    wk2, wv2 = dup(wk), dup(wv)
    ud = wu.shape[1]
    w_all = jnp.concatenate([wq, wk2, wv2, wu, wq[:, _rope_partner_cols(qd)], wk2[:, _rope_partner_cols(2 * kd)]],
                            axis=1).astype(BF16)
    o_u = qd + 4 * kd
    segs = ((0, qd, o_u + ud), (qd, 2 * kd, o_u + ud + qd), (qd + 2 * kd, 2 * kd, None), (o_u, ud, None))
    q, k, v, u = _proj_call(xz, mod, norm_g0, w_all, rope[0], rope[1], segs, n // ROW_TILE, "proj_cd")
    oc = _swa_call(q, k, v, _pad_cols(sink.reshape(1, SWA_HEADS), LANES), n, nc, not last)

    ch = hy_bias.shape[-1]
    deltas = jnp.abs(jnp.linspace(HY_MIN_DECAY, HY_MAX_DECAY, ch, dtype=F32)).reshape(1, ch)
    hid = hy_w2.shape[0]
    w1p = jnp.pad(hy_w1, ((0, hid - hy_w1.shape[0]), (0, 0)))
    filt = lambda m, dft: _hyena_filter_call(*_hyena_feats(m), w1p, hy_b1.reshape(1, hid), hy_w2,
                                             hy_b2.reshape(1, hid), hy_freq, hy_w3, deltas, *dft)
    kr, ki, kn = filt(n, dft_x)
    od = _hyena_call(u, hy_conv, kr, ki, kn, hy_bias, *dft_x, n, 0, None)
    if not last:
        kr, ki, kn = filt(nc, dft_c)
        od = _hyena_call(u, hy_conv, kr, ki, kn, hy_bias, *dft_c, nc, n // nc, od)
    n_tiles = (n if last else n + nc) // ROW_TILE
    return _outproj_call(oc, od, w_out.astype(BF16), xz, mod, norm_g1, n_tiles, n // ROW_TILE)


def kernel(x, c, ctx, c_ctx, w_mod, b_mod, norm_g, ffn_w_up, ffn_conv, ffn_w_down, ab_w_in, ab_w_out, gdn_conv, gdn_a_log, gdn_dt_bias, gdn_norm_g, diff_lambda, diff_norm_g, cd_w_in, cd_w_out, swa_sink, hy_conv, hy_w1, hy_b1, hy_w2, hy_b2, hy_w3, hy_freq, hy_bias):
    b, n, d = x.shape
    nc = ctx.shape[1]
    depth = w_mod.shape[0]
    assert n % ROW_TILE == 0 and nc == ROW_TILE and n % GRID_W == 0
    xz = jnp.concatenate([x, ctx], axis=1)
    rows = -(-(b + 1) // SUBLANES) * SUBLANES
    cc = jnp.concatenate([c, c_ctx[None], jnp.zeros((rows - b - 1, d), F32)], axis=0)
    mods = _mod_call(cc, w_mod, b_mod)
    rope = _rope_tables(n, nc)
    dft_x = _dft_mats(n)
    dft_c = _dft_mats(nc)
    n_x_tiles = n // ROW_TILE
    for l in range(depth):
        last = l == depth - 1
        i = l // 2
        mx = mods[l, :b].reshape(b, 1, 6, d)
        mz = jnp.broadcast_to(mods[l, b].reshape(1, 1, 6, d), (b, 1, 6, d))
        mod = jnp.concatenate([mx, mz], axis=1)
        if l % 2 == 0:
            lam_init = 0.8 - 0.6 * math.exp(-0.3 * l)
            xz = _layer_ab(xz, mod, norm_g[l, 0], norm_g[l, 1], ab_w_in[i], ab_w_out[i], gdn_conv[i], gdn_a_log[i],
                           gdn_dt_bias[i], gdn_norm_g[i], diff_lambda[i], diff_norm_g[i], lam_init, rope, n, nc, 3)
        else:
            xz = _layer_cd(xz, mod, norm_g[l, 0], norm_g[l, 1], cd_w_in[i], cd_w_out[i], swa_sink[i], hy_conv[i],
                           hy_w1[i], hy_b1[i], hy_w2[i], hy_b2[i], hy_w3[i], hy_freq[i], hy_bias[i], rope, n, nc,
                           last, dft_x, dft_c)
        n_tiles = xz.shape[1] // ROW_TILE
        xz = _ffn_call(xz, mod, norm_g[l, 2], norm_g[l, 3], ffn_w_up[l].astype(BF16), ffn_conv[l],
                       ffn_w_down[l].astype(BF16), n_tiles, n_x_tiles)
    return xz[:, :n] if xz.shape[1] != n else xz
```

```python
import functools
import math

import jax
import jax.numpy as jnp
import numpy as np
from jax import lax
from jax.experimental import pallas as pl
from jax.experimental.pallas import tpu as pltpu

F32 = jnp.float32
BF16 = jnp.bfloat16

EPS = 1e-6
NEG_INF = -1e30
GRID_W = 64
HEAD_DIM = 64
ROPE_BASE = 10000.0
GDN_HEADS = 8
GDN_DIM = 64
GDN_CHUNK = 64
DIFF_HEADS = 4
DIFF_DIM = 64
SWA_HEADS = 8
SWA_KV_HEADS = 2
SWA_WINDOW = 128
SWA_BLOCK = 128
HY_BANDS = 16
HY_MIN_DECAY = math.log(1e-2) / 1.5
HY_MAX_DECAY = math.log(1e-2) / 0.3
HY_ROW_CHUNK = 512

GDN_SOLVE_PASSES = (3, 3, 3, 3, 3)

LANES = 128
SUBLANES = 8
ROW_TILE = 256
VMEM_LIMIT = 56 * 1024 * 1024


def _cparams(sem):
    return pltpu.CompilerParams(dimension_semantics=sem, vmem_limit_bytes=VMEM_LIMIT)


def _resident(shape):
    zeros = (0,) * len(shape)
    return pl.BlockSpec(shape, lambda *_: zeros, pipeline_mode=pl.Buffered(1))


def _log2(v):
    assert v & (v - 1) == 0
    return v.bit_length() - 1


def _sigmoid(x):
    return 1.0 / (1.0 + jnp.exp(-x))


def _silu(x):
    return x * _sigmoid(x)


def _softplus(x):
    return jnp.maximum(x, 0.0) + jnp.log1p(jnp.exp(-jnp.abs(x)))


def _dot(a, b):
    return jnp.dot(a, b, preferred_element_type=F32)


def _dot_nt(a, b):
    return lax.dot_general(a, b, (((1,), (1,)), ((), ())), preferred_element_type=F32)


def _dot_tn(a, b):
    return lax.dot_general(a, b, (((0,), (0,)), ((), ())), preferred_element_type=F32)


def _dot_f32(a, b):
    return jnp.dot(a, b, preferred_element_type=F32, precision=lax.Precision.HIGHEST)


def _split2(x):
    hi = x.astype(BF16)
    lo = (x - hi.astype(F32)).astype(BF16)
    return hi, lo


def _split3(x):
    hi = x.astype(BF16)
    r = x - hi.astype(F32)
    mid = r.astype(BF16)
    lo = (r - mid.astype(F32)).astype(BF16)
    return hi, mid, lo


def _dot_sel(x, sel_bf16):
    hi, mid, lo = _split3(x)
    return _dot(hi, sel_bf16) + _dot(mid, sel_bf16) + _dot(lo, sel_bf16)


def _mm(a, b, passes):
    if passes == 1:
        return _dot(a.astype(BF16), b.astype(BF16))
    ah, al = _split2(a)
    bh, bl = _split2(b)
    return _dot(ah, bh) + _dot(al, bh) + _dot(ah, bl)


def _rms(y, g):
    return y * lax.rsqrt(jnp.mean(y * y, axis=-1, keepdims=True) + EPS) * g


def _modnorm(x, g, shift, scale):
    return _rms(x, g) * (1.0 + scale) + shift


def _mod_kernel(cc_ref, w_ref, b_ref, o_ref):
    s = _silu(cc_ref[...])
    o_ref[0] = _dot(s.astype(BF16), w_ref[0].astype(BF16)) + b_ref[0]


def _mod_call(cc, w_mod, b_mod):
    depth, d, nm = w_mod.shape
    rows = cc.shape[0]
    ct = 1536
    return pl.pallas_call(
        _mod_kernel,
        grid=(depth, nm // ct),
        in_specs=[
            pl.BlockSpec((rows, d), lambda l, j: (0, 0)),
            pl.BlockSpec((1, d, ct), lambda l, j: (l, 0, j)),
            pl.BlockSpec((1, 1, ct), lambda l, j: (l, 0, j)),
        ],
        out_specs=pl.BlockSpec((1, rows, ct), lambda l, j: (l, 0, j)),
        out_shape=jax.ShapeDtypeStruct((depth, rows, nm), F32),
        compiler_params=_cparams(("arbitrary", "arbitrary")),
        name="adaln_mod",
    )(cc, w_mod, b_mod.reshape(depth, 1, nm))


def _proj_kernel(x_ref, mod_ref, g_ref, w_ref, cos_ref, sin_ref, *out_refs, segs):
    m = mod_ref[0, 0]
    h = _modnorm(x_ref[0], g_ref[...], m[0:1], m[1:2]).astype(BF16)
    for o_ref, (start, width, rot_start) in zip(out_refs, segs):
        y = _dot(h, w_ref[:, start:start + width])
        if rot_start is not None:
            yr = _dot(h, w_ref[:, rot_start:rot_start + width])
            reps = width // LANES
            cos = jnp.concatenate([cos_ref[...]] * reps, axis=1)
            sin = jnp.concatenate([sin_ref[...]] * reps, axis=1)
            y = y * cos + yr * sin
        o_ref[0] = y


def _proj_call(xz, mod, g, w, cos_t, sin_t, segs, n_x_tiles, name):
    b, l, d = xz.shape
    tm = ROW_TILE
    nt = l // tm
    p = w.shape[1]
    return pl.pallas_call(
        functools.partial(_proj_kernel, segs=segs),
        grid=(nt, b),
        in_specs=[
            pl.BlockSpec((1, tm, d), lambda t, i: (i, t, 0)),
            pl.BlockSpec((1, 1, 6, d), lambda t, i: (i, t // n_x_tiles, 0, 0)),
            pl.BlockSpec((1, d), lambda t, i: (0, 0)),
            _resident((d, p)),
            pl.BlockSpec((tm, LANES), lambda t, i: (t, 0)),
            pl.BlockSpec((tm, LANES), lambda t, i: (t, 0)),
        ],
        out_specs=[pl.BlockSpec((1, tm, wd), lambda t, i: (i, t, 0)) for (_, wd, _) in segs],
        out_shape=[jax.ShapeDtypeStruct((b, l, wd), F32) for (_, wd, _) in segs],
        compiler_params=_cparams(("arbitrary", "arbitrary")),
        name=name,
    )(xz, mod, g.reshape(1, d), w, cos_t, sin_t)


def _outproj_kernel(o1_ref, o2_ref, w_ref, x_ref, mod_ref, g_ref, out_ref):
    k1 = o1_ref.shape[-1]
    y = _dot(o1_ref[0].astype(BF16), w_ref[0:k1, :]) + _dot(o2_ref[0].astype(BF16), w_ref[k1:, :])
    m = mod_ref[0, 0]
    out_ref[0] = x_ref[0] + m[2:3] * _rms(y, g_ref[...])


def _outproj_call(o1, o2, w, xz, mod, g, n_tiles, n_x_tiles):
    b, _, d = xz.shape
    tm = ROW_TILE
    k1, k2 = o1.shape[-1], o2.shape[-1]
    return pl.pallas_call(
        _outproj_kernel,
        grid=(n_tiles, b),
        in_specs=[
            pl.BlockSpec((1, tm, k1), lambda t, i: (i, t, 0)),
            pl.BlockSpec((1, tm, k2), lambda t, i: (i, t, 0)),
            _resident((k1 + k2, d)),
            pl.BlockSpec((1, tm, d), lambda t, i: (i, t, 0)),
            pl.BlockSpec((1, 1, 6, d), lambda t, i: (i, t // n_x_tiles, 0, 0)),
            pl.BlockSpec((1, d), lambda t, i: (0, 0)),
        ],
        out_specs=pl.BlockSpec((1, tm, d), lambda t, i: (i, t, 0)),
        out_shape=jax.ShapeDtypeStruct((b, n_tiles * tm, d), F32),
        compiler_params=_cparams(("arbitrary", "arbitrary")),
        name="mixer_out",
    )(o1, o2, w, xz, mod, g.reshape(1, d))


def _ffn_kernel(xp_ref, x_ref, xn_ref, mod_ref, g2_ref, g3_ref, wup_ref, cw_ref, wdn_ref, out_ref,
                ua_ref, ug_ref, *, tm, n_x_tiles, n_tiles, cf, dff):
    t = pl.program_id(0)
    first = jnp.logical_or(t == 0, t == n_x_tiles)
    last = jnp.logical_or(t == n_x_tiles - 1, t == n_tiles - 1)
    keep_top = jnp.where(first, 0.0, 1.0)
    keep_bot = jnp.where(last, 0.0, 1.0)
    m = mod_ref[0, 0]
    halo = SUBLANES
    xe = jnp.concatenate([xp_ref[0], x_ref[0], xn_ref[0]], axis=0)
    h = _modnorm(xe, g2_ref[...], m[3:4], m[4:5]).astype(BF16)
    acc = jnp.zeros((tm, x_ref.shape[-1]), F32)
    for j in range(dff // cf):
        halves = []
        for ref, base in ((ua_ref, j * cf), (ug_ref, dff + j * cf)):
            u = _dot(h, wup_ref[:, base:base + cf])
            ref[...] = u
            ref[0:halo, :] = u[0:halo] * keep_top
            ref[tm + halo:tm + 2 * halo, :] = u[tm + halo:] * keep_bot
            cw = cw_ref[:, base:base + cf]
            halves.append(cw[0:1] * ref[halo - 1:halo - 1 + tm, :] + cw[1:2] * ref[halo:halo + tm, :]
                          + cw[2:3] * ref[halo + 1:halo + 1 + tm, :])
        act = (_silu(halves[1]) * halves[0]).astype(BF16)
        acc = acc + _dot(act, wdn_ref[j * cf:(j + 1) * cf, :])
    out_ref[0] = x_ref[0] + m[5:6] * _rms(acc, g3_ref[...])


def _ffn_call(xz, mod, g2, g3, w_up, conv_w, w_down, n_tiles, n_x_tiles):
    b, l, d = xz.shape
    tm = ROW_TILE
    dff = w_down.shape[0]
    cf = dff // 2
    hb = tm // SUBLANES
    nb8 = l // SUBLANES
    kern = functools.partial(_ffn_kernel, tm=tm, n_x_tiles=n_x_tiles, n_tiles=n_tiles, cf=cf, dff=dff)
    return pl.pallas_call(
        kern,
        grid=(n_tiles, b),
        in_specs=[
            pl.BlockSpec((1, SUBLANES, d), lambda t, i: (i, jnp.maximum(t * hb - 1, 0), 0)),
            pl.BlockSpec((1, tm, d), lambda t, i: (i, t, 0)),
            pl.BlockSpec((1, SUBLANES, d), lambda t, i: (i, jnp.minimum((t + 1) * hb, nb8 - 1), 0)),
            pl.BlockSpec((1, 1, 6, d), lambda t, i: (i, t // n_x_tiles, 0, 0)),
            pl.BlockSpec((1, d), lambda t, i: (0, 0)),
            pl.BlockSpec((1, d), lambda t, i: (0, 0)),
            _resident((d, 2 * dff)),
            pl.BlockSpec((3, 2 * dff), lambda t, i: (0, 0)),
            _resident((dff, d)),
        ],
        out_specs=pl.BlockSpec((1, tm, d), lambda t, i: (i, t, 0)),
        out_shape=jax.ShapeDtypeStruct((b, n_tiles * tm, d), F32),
        scratch_shapes=[pltpu.VMEM((tm + 2 * SUBLANES, cf), F32), pltpu.VMEM((tm + 2 * SUBLANES, cf), F32)],
        compiler_params=_cparams(("arbitrary", "arbitrary")),
        name="conv_ffn",
    )(xz, xz, xz, mod, g2.reshape(1, d), g3.reshape(1, d), w_up, conv_w, w_down)


def _half_sums(x2, lane_lo):
    s0 = jnp.sum(jnp.where(lane_lo, x2, 0.0), axis=-1, keepdims=True)
    s1 = jnp.sum(jnp.where(lane_lo, 0.0, x2), axis=-1, keepdims=True)
    return jnp.where(lane_lo, s0, s1)


def _gdn_kernel(qkvg_ref, ba_ref, cw_ref, gp_ref, ng_ref, out_ref,
                pad_ref, q_ref, k_ref, v_ref, bb_ref, gb_ref, qe_ref, mp_ref, ou_ref, nn_ref, egl_ref, o_ref,
                *, n, nc, solve_passes, chunks_per_iter):
    l = n + nc
    c = GDN_CHUNK
    n_chunks = l // c
    pair = pl.program_id(1)
    halo = SUBLANES
    lane = lax.broadcasted_iota(jnp.int32, (1, LANES), 1)
    lane_lo = lane < GDN_DIM

    cw = cw_ref[:, 0:3 * LANES]
    zero_rows = jnp.zeros((halo, 3 * LANES), F32)
    for seq_start, seq_len in ((0, n), (n, nc)):
        base = halo + seq_start + (2 * halo if seq_start else 0)
        pad_ref[base - halo:base, :] = zero_rows
        pad_ref[base + seq_len:base + seq_len + halo, :] = zero_rows
        step = 256
        for r in range(0, seq_len, step):
            pad_ref[base + r:base + r + step, :] = qkvg_ref[0, seq_start + r:seq_start + r + step, 0:3 * LANES]
        for r in range(0, seq_len, step):
            y = (cw[0:1] * pad_ref[base + r - 1:base + r - 1 + step, :]
                 + cw[1:2] * pad_ref[base + r:base + r + step, :]
                 + cw[2:3] * pad_ref[base + r + 1:base + r + 1 + step, :])
            y = _silu(y)
            q = y[:, 0:LANES]
            k = y[:, LANES:2 * LANES]
            rows = slice(seq_start + r, seq_start + r + step)
            q_ref[rows, :] = q * lax.rsqrt(_half_sums(q * q, lane_lo) + EPS) * (GDN_DIM ** -0.5)
            k_ref[rows, :] = k * lax.rsqrt(_half_sums(k * k, lane_lo) + EPS)
            v_ref[rows, :] = y[:, 2 * LANES:3 * LANES]

    sel_r = lax.broadcasted_iota(jnp.int32, (LANES, 4 * LANES), 0)
    sel_c = lax.broadcasted_iota(jnp.int32, (LANES, 4 * LANES), 1)
    quarter = sel_c >> _log2(LANES)
    src_lane = (quarter & 1) * 2 * GDN_HEADS + (quarter >> 1) * GDN_HEADS + 2 * pair + ((sel_c >> _log2(GDN_DIM)) & 1)
    sel = (sel_r == src_lane).astype(BF16)
    gblk = 256
    bi = lax.broadcasted_iota(jnp.int32, (gblk, gblk), 0)
    bj = lax.broadcasted_iota(jnp.int32, (gblk, gblk), 1)
    same_chunk = (bi >> _log2(c)) == (bj >> _log2(c))
    csum = (jnp.logical_and(same_chunk, bi >= bj).astype(BF16), jnp.logical_and(same_chunk, bi <= bj).astype(BF16))
    gp = gp_ref[0]
    for r in range(0, l, gblk):
        x = _dot_sel(ba_ref[0, r:r + gblk, :], sel)
        for d in range(2):
            g2 = -jnp.exp(gp[d:d + 1]) * _softplus(x[:, (2 * d + 1) * LANES:(2 * d + 2) * LANES] + gp[2 + d:3 + d])
            bb_ref[d, r:r + gblk, :] = _sigmoid(x[:, 2 * d * LANES:(2 * d + 1) * LANES])
            gb_ref[d, r:r + gblk, :] = _dot_sel_lhs(csum[d], g2)

    r2 = lax.broadcasted_iota(jnp.int32, (2 * c, 2 * c), 0)
    c2 = lax.broadcasted_iota(jnp.int32, (2 * c, 2 * c), 1)
    same_head = (r2 >= c) == (c2 >= c)
    eye = (r2 == c2).astype(F32)
    masks = ((jnp.logical_and(same_head, r2 >= c2), jnp.logical_and(same_head, r2 > c2)),
             (jnp.logical_and(same_head, r2 <= c2), jnp.logical_and(same_head, r2 < c2)))
    m0 = lane_lo.astype(F32)
    m1 = 1.0 - m0

    def stack_heads(x2):
        return jnp.concatenate([x2 * m0, x2 * m1], axis=0)

    def fold_heads(x):
        return x[0:c] + x[c:2 * c]

    def chunks_local(dirs, qs, ks, vs, betas, gcs):
        each = lambda f, *cols: [f(*args) for args in zip(*cols)]
        incl = [masks[d][0] for d in dirs]
        strict = [masks[d][1] for d in dirs]
        g1 = each(lambda gc2: jnp.concatenate([gc2, gc2], axis=0), gcs)
        decay = each(lambda g, m: jnp.where(m, jnp.exp(jnp.where(m, g - g.T, 0.0)), 0.0), g1, incl)
        kb = each(lambda k, b: k * b, ks, betas)
        kst = each(lambda k: stack_heads(k).astype(BF16), ks)
        a_raw = each(lambda x, y: _dot_nt(stack_heads(x).astype(BF16), y), kb, kst)
        qk_raw = each(lambda x, y: _dot_nt(stack_heads(x).astype(BF16), y), qs, kst)
        qk = each(lambda m, x, dc: jnp.where(m, x * dc, 0.0).astype(BF16), incl, qk_raw, decay)
        p = each(lambda m, x, dc: jnp.where(m, -(x * dc), 0.0), strict, a_raw, decay)
        tinv = each(lambda x: eye + x, p)
        assert len(solve_passes) == _log2(c) - 1
        for passes in solve_passes:
            p = each(lambda x: _mm(x, x, passes), p)
            tinv = each(lambda t, x: t + _mm(t, x, passes), tinv, p)
        egc = each(jnp.exp, gcs)
        rhs = each(lambda v, b, x, e: jnp.concatenate([stack_heads(v * b), stack_heads(x * e)], axis=1),
                   vs, betas, kb, egc)
        sol = each(lambda t, r: _mm(t, r, 3), tinv, rhs)
        u2 = each(lambda x: fold_heads(x[:, 0:LANES]), sol)
        w2 = each(lambda x: fold_heads(x[:, LANES:2 * LANES]), sol)
        gl = each(lambda d, gc2: gc2[c - 1:c, :] if d == 0 else gc2[0:1, :], dirs, gcs)
        ktail = each(lambda k, g, gc2: (k * jnp.exp(g - gc2)).astype(BF16), ks, gl, gcs)
        qwu = each(lambda x, w, u: _dot(x, jnp.concatenate([stack_heads(w), stack_heads(u)], axis=1).astype(BF16)),
                   qk, w2, u2)
        kwu = each(lambda x, w, u: _dot_tn(x, jnp.concatenate([w, u], axis=1).astype(BF16)), ktail, w2, u2)
        q_eff = each(lambda q, e, x: (q * e - fold_heads(x[:, 0:LANES])).astype(BF16), qs, egc, qwu)
        m_neg = each(lambda x: jnp.where(same_head, -x[:, 0:LANES], 0.0).astype(BF16), kwu)
        o_loc = each(lambda x: fold_heads(x[:, LANES:2 * LANES]), qwu)
        s_loc = each(lambda x: jnp.where(same_head, x[:, LANES:2 * LANES], 0.0), kwu)
        egl = each(lambda g: jnp.broadcast_to(jnp.exp(g), (SUBLANES, LANES)), gl)
        return q_eff, m_neg, o_loc, s_loc, egl

    def chunk_rows(chunk, rows_per_chunk):
        return pl.ds(pl.multiple_of(chunk * rows_per_chunk, rows_per_chunk), rows_per_chunk)

    def local_body(it, carry):
        dirs, chunks, qs, ks, vs, betas, gcs = [], [], [], [], [], [], []
        for g in range(chunks_per_iter):
            chunk = it + g * (n_chunks // chunks_per_iter)
            rows = chunk_rows(chunk, c)
            for d in range(2):
                dirs.append(d)
                chunks.append(chunk)
                qs.append(q_ref[rows, :])
                ks.append(k_ref[rows, :])
                vs.append(v_ref[rows, :])
                betas.append(bb_ref[d, rows, :])
                gcs.append(gb_ref[d, rows, :])
        results = chunks_local(dirs, qs, ks, vs, betas, gcs)
        for d, chunk, q_eff, m_neg, o_loc, s_loc, egl in zip(dirs, chunks, *results):
            qe_ref[d, chunk_rows(chunk, c), :] = q_eff
            mp_ref[d, chunk_rows(chunk, 2 * c), :] = m_neg
            ou_ref[d, chunk_rows(chunk, c), :] = o_loc
            nn_ref[d, chunk_rows(chunk, 2 * c), :] = s_loc
            egl_ref[d, chunk_rows(chunk, SUBLANES), :] = egl
        return carry

    lax.fori_loop(0, n_chunks // chunks_per_iter, local_body, 0)

    ctx_chunks = nc // c

    def scan_body(s, carry):
        chunks = (jnp.where(s < ctx_chunks, s + n // c, s - ctx_chunks), n_chunks - 1 - s)
        loaded = [(qe_ref[d, chunk_rows(ch, c), :], mp_ref[d, chunk_rows(ch, 2 * c), :],
                   ou_ref[d, chunk_rows(ch, c), :], nn_ref[d, chunk_rows(ch, 2 * c), :],
                   egl_ref[d, chunk_rows(ch, SUBLANES), :]) for d, ch in enumerate(chunks)]
        res = [_dot(jnp.concatenate([ld[0], ld[1]], axis=0), s2.astype(BF16)) for ld, s2 in zip(loaded, carry)]
        for d, ch in enumerate(chunks):
            o_ref[d, chunk_rows(ch, c), :] = res[d][0:c] + loaded[d][2]
        return tuple(s2 * ld[4][0:1] + r[c:3 * c] + ld[3] for s2, ld, r in zip(carry, loaded, res))

    zero_state = jnp.zeros((2 * c, 2 * c), F32)
    lax.fori_loop(0, n_chunks, scan_body, (zero_state, zero_state))

    ng = ng_ref[...]
    step = 256
    for r in range(0, l, step):
        o = o_ref[0, r:r + step, :] + o_ref[1, r:r + step, :]
        ms = _half_sums(o * o, lane_lo) * (1.0 / GDN_DIM)
        gate = qkvg_ref[0, r:r + step, 3 * LANES:4 * LANES]
        out_ref[0, r:r + step, :] = o * lax.rsqrt(ms + EPS) * ng * _silu(gate)


def _dot_sel_lhs(sel_bf16, x):
    hi, mid, lo = _split3(x)
    return _dot(sel_bf16, hi) + _dot(sel_bf16, mid) + _dot(sel_bf16, lo)


def _gdn_call(qkvg, ba, conv_w, gate_params, ng, n, nc, solve_passes):
    b, l, _ = qkvg.shape
    pairs = GDN_HEADS // 2
    n_chunks = l // GDN_CHUNK
    kern = functools.partial(_gdn_kernel, n=n, nc=nc, solve_passes=solve_passes, chunks_per_iter=4)
    return pl.pallas_call(
        kern,
        grid=(b, pairs),
        in_specs=[
            pl.BlockSpec((1, l, 4 * LANES), lambda i, p: (i, 0, p)),
            pl.BlockSpec((1, l, LANES), lambda i, p: (i, 0, 0)),
            pl.BlockSpec((3, 4 * LANES), lambda i, p: (0, p)),
            pl.BlockSpec((1, 4, LANES), lambda i, p: (p, 0, 0)),
            pl.BlockSpec((1, LANES), lambda i, p: (0, 0)),
        ],
        out_specs=pl.BlockSpec((1, l, LANES), lambda i, p: (i, 0, p)),
        out_shape=jax.ShapeDtypeStruct((b, l, pairs * LANES), F32),
        scratch_shapes=[
            pltpu.VMEM((l + 5 * SUBLANES, 3 * LANES), F32),
            pltpu.VMEM((l, LANES), F32),
            pltpu.VMEM((l, LANES), F32),
            pltpu.VMEM((l, LANES), F32),
            pltpu.VMEM((2, l, LANES), F32),
            pltpu.VMEM((2, l, LANES), F32),
            pltpu.VMEM((2, l, LANES), BF16),
            pltpu.VMEM((2, 2 * l, LANES), BF16),
            pltpu.VMEM((2, l, LANES), F32),
            pltpu.VMEM((2, 2 * l, LANES), F32),
            pltpu.VMEM((2, n_chunks * SUBLANES, LANES), F32),
            pltpu.VMEM((2, l, LANES), F32),
        ],
        compiler_params=_cparams(("arbitrary", "arbitrary")),
        name="gated_deltanet",
    )(qkvg, ba, conv_w, gate_params, ng)


def _diff_kernel(q_ref, k_ref, v_ref, lam_ref, ng_ref, o_ref, *, n, lam_init, n_x_tiles):
    t = pl.program_id(2)
    lp = lam_ref[...]
    lam = (jnp.exp(jnp.sum(lp[0:1] * lp[1:2], axis=-1, keepdims=True))
           - jnp.exp(jnp.sum(lp[2:3] * lp[3:4], axis=-1, keepdims=True)) + lam_init)
    lane = lax.broadcasted_iota(jnp.int32, (1, LANES), 1)
    lane_lo = lane < DIFF_DIM
    q = q_ref[0] * (DIFF_DIM ** -0.5)
    ng = ng_ref[...]

    def attend(k, v):
        kb = k.astype(BF16)
        vb = v.astype(BF16)
        qt = [jnp.where(m, q, 0.0).astype(BF16) for m in (lane_lo, jnp.logical_not(lane_lo))]
        s = [_dot_nt(x, kb) for x in qt]
        e = [jnp.exp(x - jnp.max(x, axis=-1, keepdims=True)) for x in s]
        pv = [_dot(x.astype(BF16), vb) for x in e]
        parts = [x * (1.0 / jnp.sum(y, axis=-1, keepdims=True)) for x, y in zip(pv, e)]
        o = parts[0] - lam * parts[1]
        o_ref[0] = _rms(o, ng) * (1.0 - lam_init)

    @pl.when(t < n_x_tiles)
    def _():
        attend(k_ref[0], v_ref[0])

    @pl.when(t >= n_x_tiles)
    def _():
        attend(k_ref[0, n:, :], v_ref[0, n:, :])


def _diff_call(dq, dk, dv, lam_p, ng, n, lam_init):
    b, l, _ = dq.shape
    tq = ROW_TILE
    nt = l // tq
    kern = functools.partial(_diff_kernel, n=n, lam_init=lam_init, n_x_tiles=n // tq)
    return pl.pallas_call(
        kern,
        grid=(b, DIFF_HEADS, nt),
        in_specs=[
            pl.BlockSpec((1, tq, LANES), lambda i, h, t: (i, t, h)),
            pl.BlockSpec((1, l, LANES), lambda i, h, t: (i, 0, h)),
            pl.BlockSpec((1, l, LANES), lambda i, h, t: (i, 0, h)),
            pl.BlockSpec((4, DIFF_DIM), lambda i, h, t: (0, 0)),
            pl.BlockSpec((1, LANES), lambda i, h, t: (0, 0)),
        ],
        out_specs=pl.BlockSpec((1, tq, LANES), lambda i, h, t: (i, t, h)),
        out_shape=jax.ShapeDtypeStruct((b, l, DIFF_HEADS * LANES), F32),
        compiler_params=_cparams(("arbitrary", "arbitrary", "arbitrary")),
        name="diff_attention",
    )(dq, dk, dv, lam_p, ng.reshape(1, LANES))


def _swa_kernel(q_ref, k_ref, v_ref, sink_ref, o_ref, *, n, nc):
    t = pl.program_id(1)
    blk = SWA_BLOCK
    n_x = n // blk
    q = q_ref[0] * (HEAD_DIM ** -0.5)
    lane = lax.broadcasted_iota(jnp.int32, (1, LANES), 1)
    lane_lo = lane < HEAD_DIM
    sink = sink_ref[...]
    group = SWA_HEADS // SWA_KV_HEADS

    def run(keys, vals, valid):
        head_of_row = lax.broadcasted_iota(jnp.int32, (group * blk, 1), 0) >> _log2(blk)
        kvs = range(SWA_KV_HEADS)
        kk = [keys[:, kvh * LANES:(kvh + 1) * LANES].astype(BF16) for kvh in kvs]
        vv = [vals[:, kvh * LANES:(kvh + 1) * LANES].astype(BF16) for kvh in kvs]
        qst, sk = [], []
        for kvh in kvs:
            q_rows = []
            sk_rows = jnp.zeros((group * blk, 1), F32)
            for g in range(group):
                h = kvh * group + g
                qp = q[:, (h // 2) * LANES:(h // 2 + 1) * LANES]
                q_rows.append(jnp.where(lane_lo if h % 2 == 0 else jnp.logical_not(lane_lo), qp, 0.0))
                sk_rows = jnp.where(head_of_row == g, sink[:, h:h + 1], sk_rows)
            qst.append(jnp.concatenate(q_rows, axis=0).astype(BF16))
            sk.append(sk_rows)
        s = [_dot_nt(x, y) for x, y in zip(qst, kk)]
        if valid is not None:
            s = [jnp.where(valid, x, NEG_INF) for x in s]
        mx = [jnp.maximum(jnp.max(x, axis=-1, keepdims=True), y) for x, y in zip(s, sk)]
        e = [jnp.exp(x - m) for x, m in zip(s, mx)]
        pv = [_dot(x.astype(BF16), y) for x, y in zip(e, vv)]
        den = [jnp.sum(x, axis=-1, keepdims=True) + jnp.exp(y - m) for x, y, m in zip(e, sk, mx)]
        outs = []
        for o, dn in zip(pv, den):
            o = o * (1.0 / dn)
            for g in range(0, group, 2):
                outs.append(jnp.where(lane_lo, o[g * blk:(g + 1) * blk], o[(g + 1) * blk:(g + 2) * blk]))
        o_ref[0] = jnp.concatenate(outs, axis=1)

    @pl.when(t < n_x)
    def _():
        start = pl.multiple_of(jnp.clip((t - 1) * blk, 0, n - 3 * blk), blk)
        keys = jnp.concatenate([k_ref[0, pl.ds(start, 3 * blk), :], k_ref[0, n:n + nc, :]], axis=0)
        vals = jnp.concatenate([v_ref[0, pl.ds(start, 3 * blk), :], v_ref[0, n:n + nc, :]], axis=0)
        shape = (group * blk, 3 * blk + nc)
        qpos = t * blk + (lax.broadcasted_iota(jnp.int32, shape, 0) & (blk - 1))
        col = lax.broadcasted_iota(jnp.int32, shape, 1)
        dist = qpos - (start + col)
        in_window = jnp.logical_and(dist <= SWA_WINDOW, dist >= -SWA_WINDOW)
        valid = jnp.logical_or(col >= 3 * blk, in_window)
        run(keys, vals, valid)

    @pl.when(t >= n_x)
    def _():
        run(k_ref[0, n:n + nc, :], v_ref[0, n:n + nc, :], None)


def _swa_call(q, k, v, sink, n, nc, with_ctx):
    b, l, _ = q.shape
    blk = SWA_BLOCK
    nt = (l if with_ctx else n) // blk
    kern = functools.partial(_swa_kernel, n=n, nc=nc)
    return pl.pallas_call(
        kern,
        grid=(b, nt),
        in_specs=[
            pl.BlockSpec((1, blk, SWA_HEADS * HEAD_DIM), lambda i, t: (i, t, 0)),
            pl.BlockSpec((1, l, 2 * LANES), lambda i, t: (i, 0, 0)),
            pl.BlockSpec((1, l, 2 * LANES), lambda i, t: (i, 0, 0)),
            pl.BlockSpec((1, LANES), lambda i, t: (0, 0)),
        ],
        out_specs=pl.BlockSpec((1, blk, SWA_HEADS * HEAD_DIM), lambda i, t: (i, t, 0)),
        out_shape=jax.ShapeDtypeStruct((b, l, SWA_HEADS * HEAD_DIM), F32),
        compiler_params=_cparams(("arbitrary", "arbitrary")),
        name="window_attention",
    )(q, k, v, sink)


def _dft_mats(n):
    k = jnp.arange(n, dtype=jnp.int32)
    km = (k[:, None] * k[None, :]) % (2 * n)
    ang = km.astype(F32) * (math.pi / n)
    return jnp.cos(ang).astype(BF16), (-jnp.sin(ang)).astype(BF16)


def _hyena_filter_kernel(ff_ref, fb_ref, w1_ref, b1_ref, w2_ref, b2_ref, freq_ref, w3f_ref, w3b_ref,
                         dl_ref, c_ref, s_ref, kr_ref, ki_ref, kn_ref):
    n = ff_ref.shape[0]
    freq = freq_ref[...]

    def mlp(feat):
        h = jnp.sin(freq[0:1] * (_dot_f32(feat, w1_ref[...]) + b1_ref[...]))
        return jnp.sin(freq[1:2] * (_dot_f32(h, w2_ref[...]) + b2_ref[...]))

    ff = ff_ref[...]
    fb = fb_ref[...]
    dl = dl_ref[...]
    row = lax.broadcasted_iota(jnp.int32, (n, 1), 0)
    kf = _dot_f32(mlp(ff), w3f_ref[...]) * jnp.exp(-ff[:, 0:1] * dl)
    kb = _dot_f32(mlp(fb), w3b_ref[...]) * jnp.exp(-fb[:, 0:1] * dl)
    kb = jnp.where(row == 0, 0.0, kb)
    ss = jnp.sum(kf * kf, axis=0, keepdims=True) + jnp.sum(kb * kb, axis=0, keepdims=True)
    sc = lax.rsqrt(ss + EPS)
    kf = kf * sc
    kb = kb * sc
    sgn = jnp.where((row & 1) == 0, 1.0, -1.0)
    cm = c_ref[...]
    sm = s_ref[...]
    fh, fl = _split2(kf)
    bh, bl = _split2(kb)
    kr_ref[0] = _dot(cm, fh) + _dot(cm, fl) + sgn * (_dot(cm, bh) + _dot(cm, bl))
    ki_ref[0] = _dot(sm, fh) + _dot(sm, fl) + sgn * (_dot(sm, bh) + _dot(sm, bl))
    nyq = jnp.sum((kf + kb) * sgn, axis=0, keepdims=True)
    kn_ref[0] = jnp.broadcast_to(nyq, (SUBLANES, nyq.shape[-1]))


def _hyena_filter_call(featf, featb, w1, b1, w2, b2, freq, w3, deltas, cm, sm):
    n = featf.shape[0]
    hid = w2.shape[0]
    ch = deltas.shape[-1]
    tc = 2 * LANES
    nct = ch // tc
    return pl.pallas_call(
        _hyena_filter_kernel,
        grid=(2, nct),
        in_specs=[
            pl.BlockSpec((n, hid), lambda o, j: (0, 0)),
            pl.BlockSpec((n, hid), lambda o, j: (0, 0)),
            pl.BlockSpec((hid, hid), lambda o, j: (0, 0)),
            pl.BlockSpec((1, hid), lambda o, j: (0, 0)),
            pl.BlockSpec((hid, hid), lambda o, j: (0, 0)),
            pl.BlockSpec((1, hid), lambda o, j: (0, 0)),
            pl.BlockSpec((2, hid), lambda o, j: (0, 0)),
            pl.BlockSpec((hid, tc), lambda o, j: (0, (2 * o) * nct + j)),
            pl.BlockSpec((hid, tc), lambda o, j: (0, (2 * o + 1) * nct + j)),
            pl.BlockSpec((1, tc), lambda o, j: (0, j)),
            _resident((n, n)),
            _resident((n, n)),
        ],
        out_specs=[
            pl.BlockSpec((1, n, tc), lambda o, j: (o, 0, j)),
            pl.BlockSpec((1, n, tc), lambda o, j: (o, 0, j)),
            pl.BlockSpec((1, SUBLANES, tc), lambda o, j: (o, 0, j)),
        ],
        out_shape=[
            jax.ShapeDtypeStruct((2, n, ch), F32),
            jax.ShapeDtypeStruct((2, n, ch), F32),
            jax.ShapeDtypeStruct((2, SUBLANES, ch), F32),
        ],
        compiler_params=_cparams(("arbitrary", "arbitrary")),
        name="hyena_filters",
    )(featf, featb, w1, b1, w2, b2, freq, w3, w3, deltas, cm, sm)


def _hyena_kernel(*refs, n, aliased):
    if aliased:
        refs = refs[1:]
    (v_ref, x1_ref, x2_ref, cwv_ref, cw1_ref, cw2_ref, kr_ref, ki_ref, kn_ref, bias_ref, c_ref, s_ref,
     o_ref, pad_ref, z_ref, zb_ref, p_ref) = refs
    halo = SUBLANES
    tc = o_ref.shape[-1]
    rc = min(n, HY_ROW_CHUNK)
    zero_rows = jnp.zeros((halo, tc), F32)
    pad_ref[0:halo, :] = zero_rows
    pad_ref[halo + n:2 * halo + n, :] = zero_rows

    def stage(ref):
        for r in range(0, n, rc):
            pad_ref[halo + r:halo + r + rc, :] = ref[0, r:r + rc, :]

    def conv_rows(cw, r):
        return (cw[0:1] * pad_ref[halo - 1 + r:halo - 1 + r + rc, :] + cw[1:2] * pad_ref[halo + r:halo + r + rc, :]
                + cw[2:3] * pad_ref[halo + 1 + r:halo + 1 + r + rc, :])

    def sign_rows(r):
        row = r + lax.broadcasted_iota(jnp.int32, (rc, 1), 0)
        return row, jnp.where((row & 1) == 0, 1.0, -1.0)

    stage(v_ref)
    cw = cwv_ref[...]
    for r in range(0, n, rc):
        z = conv_rows(cw, r)
        z_ref[r:r + rc, :] = z
        zb_ref[r:r + rc, :] = z.astype(BF16)

    for o, (gate_ref, gate_cw_ref) in enumerate(((x1_ref, cw1_ref), (x2_ref, cw2_ref))):
        znyq = jnp.zeros((1, tc), F32)
        for r in range(0, n, rc):
            row, sgn = sign_rows(r)
            znyq = znyq + jnp.sum(z_ref[r:r + rc, :] * sgn, axis=0, keepdims=True)
            zb = zb_ref[...]
            zr = _dot(c_ref[r:r + rc, :], zb)
            zi = _dot(s_ref[r:r + rc, :], zb)
            kr = kr_ref[o, r:r + rc, :]
            ki = ki_ref[o, r:r + rc, :]
            wgt = jnp.where(row == 0, 0.5 / n, 1.0 / n)
            p_ref[0, r:r + rc, :] = ((zr * kr - zi * ki) * wgt).astype(BF16)
            p_ref[1, r:r + rc, :] = ((zr * ki + zi * kr) * wgt).astype(BF16)
        nyq = znyq * kn_ref[o, 0:1, :] * (0.5 / n)
        stage(gate_ref)
        cw = gate_cw_ref[...]
        bias = bias_ref[o:o + 1, :]
        for r in range(0, n, rc):
            _, sgn = sign_rows(r)
            y = _dot(c_ref[r:r + rc, :], p_ref[0]) + _dot(s_ref[r:r + rc, :], p_ref[1]) + sgn * nyq
            z = conv_rows(cw, r) * (y + z_ref[r:r + rc, :] * bias)
            if o == 0:
                z_ref[r:r + rc, :] = z
                zb_ref[r:r + rc, :] = z.astype(BF16)
            else:
                o_ref[0, r:r + rc, :] = z


def _hyena_call(u, conv_w, kr, ki, kn, bias, cm, sm, n, row_block, prev_out):
    b, l, _ = u.shape
    ch = bias.shape[-1]
    tc = 2 * LANES
    nct = ch // tc
    aliased = prev_out is not None
    kern = functools.partial(_hyena_kernel, n=n, aliased=aliased)
    once = pl.Buffered(1)
    in_specs = [
        pl.BlockSpec((1, n, tc), lambda j, i: (i, row_block, j)),
        pl.BlockSpec((1, n, tc), lambda j, i: (i, row_block, nct + j)),
        pl.BlockSpec((1, n, tc), lambda j, i: (i, row_block, 2 * nct + j)),
        pl.BlockSpec((3, tc), lambda j, i: (0, j)),
        pl.BlockSpec((3, tc), lambda j, i: (0, nct + j)),
        pl.BlockSpec((3, tc), lambda j, i: (0, 2 * nct + j)),
        pl.BlockSpec((2, n, tc), lambda j, i: (0, 0, j), pipeline_mode=once),
        pl.BlockSpec((2, n, tc), lambda j, i: (0, 0, j), pipeline_mode=once),
        pl.BlockSpec((2, SUBLANES, tc), lambda j, i: (0, 0, j)),
        pl.BlockSpec((2, tc), lambda j, i: (0, j)),
        _resident((n, n)),
        _resident((n, n)),
    ]
    args = [u, u, u, conv_w, conv_w, conv_w, kr, ki, kn, bias, cm, sm]
    aliases = {}
    if aliased:
        in_specs = [pl.BlockSpec(memory_space=pl.ANY)] + in_specs
        args = [prev_out] + args
        aliases = {0: 0}
    return pl.pallas_call(
        kern,
        grid=(nct, b),
        in_specs=in_specs,
        out_specs=pl.BlockSpec((1, n, tc), lambda j, i: (i, row_block, j)),
        out_shape=jax.ShapeDtypeStruct((b, l, ch), F32),
        scratch_shapes=[
            pltpu.VMEM((n + 2 * SUBLANES, tc), F32),
            pltpu.VMEM((n, tc), F32),
            pltpu.VMEM((n, tc), BF16),
            pltpu.VMEM((2, n, tc), BF16),
        ],
        input_output_aliases=aliases,
        compiler_params=_cparams(("arbitrary", "arbitrary")),
        name="hyena_conv_n%d" % n,
    )(*args)


def _rope_tables(n, nc):
    rows = n // GRID_W
    row = jnp.repeat(jnp.arange(rows, dtype=F32), GRID_W)
    col = jnp.tile(jnp.arange(GRID_W, dtype=F32), rows)
    half = HEAD_DIM // 2
    inv = ROPE_BASE ** (-jnp.arange(0, half, 2, dtype=F32) / half)
    ar = row[:, None] * inv
    ac = col[:, None] * inv
    cos = jnp.concatenate([jnp.cos(ar), jnp.cos(ar), jnp.cos(ac), jnp.cos(ac)], axis=-1)
    sin = jnp.concatenate([-jnp.sin(ar), jnp.sin(ar), -jnp.sin(ac), jnp.sin(ac)], axis=-1)
    cos = jnp.concatenate([cos, jnp.ones((nc, HEAD_DIM), F32)], axis=0)
    sin = jnp.concatenate([sin, jnp.zeros((nc, HEAD_DIM), F32)], axis=0)
    return jnp.tile(cos, (1, LANES // HEAD_DIM)), jnp.tile(sin, (1, LANES // HEAD_DIM))


def _rope_partner_cols(width):
    d = np.arange(width)
    quarter = HEAD_DIM // 4
    return np.where((d % (2 * quarter)) < quarter, d + quarter, d - quarter)


def _hyena_feats(n):
    pos = jnp.arange(n, dtype=F32)
    t = pos / max(n - 1, 1)
    ang = (2.0 * math.pi * pos / n)[:, None] * jnp.linspace(1e-4, HY_BANDS - 1, HY_BANDS, dtype=F32)[None, :]
    feats = jnp.concatenate([t[:, None], jnp.cos(ang), -jnp.sin(ang)], axis=-1)
    feats = jnp.pad(feats, ((0, 0), (0, 64 - feats.shape[-1])))
    back = jnp.concatenate([feats[0:1], jnp.flip(feats[1:], axis=0)], axis=0)
    return feats, back


def _pad_cols(w, width):
    return jnp.pad(w, ((0, 0), (0, width - w.shape[-1])))


def _layer_ab(xz, mod, norm_g0, norm_g1, w_in, w_out, conv_w, a_log, dt_bias, gdn_g, lam_p, diff_g, lam_init,
              rope, n, nc, solve_passes):
    hd = GDN_HEADS * GDN_DIM
    wq, wk, wv, wg = (w_in[:, i * hd:(i + 1) * hd] for i in range(4))
    o = 4 * hd
    w_beta, w_alpha = w_in[:, o:o + 16], w_in[:, o + 16:o + 32]
    o += 32
    dd = DIFF_HEADS * 2 * DIFF_DIM
    wdq, wdk, wdv = (w_in[:, o + i * dd:o + (i + 1) * dd] for i in range(3))
    pairs = GDN_HEADS // 2
    pair_cols = lambda w: [w[:, p * LANES:(p + 1) * LANES] for p in range(pairs)]
    w_qkvg = jnp.concatenate([blk for grp in zip(pair_cols(wq), pair_cols(wk), pair_cols(wv), pair_cols(wg))
                              for blk in grp], axis=1)
    perm = _rope_partner_cols(dd)
    w_all = jnp.concatenate([w_qkvg, _pad_cols(jnp.concatenate([w_beta, w_alpha], axis=1), LANES),
                             wdq, wdk, wdv, wdq[:, perm], wdk[:, perm]], axis=1).astype(BF16)
    c0 = 4 * hd
    segs = ((0, c0, None), (c0, LANES, None), (c0 + LANES, dd, c0 + LANES + 3 * dd),
            (c0 + LANES + dd, dd, c0 + LANES + 4 * dd), (c0 + LANES + 2 * dd, dd, None))
    qkvg, ba, dq, dk, dv = _proj_call(xz, mod, norm_g0, w_all, rope[0], rope[1], segs, n // ROW_TILE, "proj_ab")

    cq, ck, cv = (conv_w[:, i * hd:(i + 1) * hd] for i in range(3))
    zeros = jnp.zeros((3, LANES), F32)
    conv_l = jnp.concatenate([blk for p in range(pairs) for blk in
                              (cq[:, p * LANES:(p + 1) * LANES], ck[:, p * LANES:(p + 1) * LANES],
                               cv[:, p * LANES:(p + 1) * LANES], zeros)], axis=1)
    pair_rows = lambda t: jnp.repeat(t.reshape(2, pairs, 2), GDN_DIM, axis=-1).transpose(1, 0, 2)
    gate_params = jnp.concatenate([pair_rows(a_log), pair_rows(dt_bias)], axis=1)
    ng = jnp.tile(gdn_g.reshape(1, GDN_DIM), (1, 2))
    oa = _gdn_call(qkvg, ba, conv_l, gate_params, ng, n, nc, solve_passes)
    ob = _diff_call(dq, dk, dv, lam_p, diff_g, n, lam_init)
    l = n + nc
    return _outproj_call(oa, ob, w_out.astype(BF16), xz, mod, norm_g1, l // ROW_TILE, n // ROW_TILE)


def _layer_cd(xz, mod, norm_g0, norm_g1, w_in, w_out, sink, hy_conv, hy_w1, hy_b1, hy_w2, hy_b2, hy_w3, hy_freq,
              hy_bias, rope, n, nc, last, dft_x, dft_c):
    qd = SWA_HEADS * HEAD_DIM
    kd = SWA_KV_HEADS * HEAD_DIM
    wq, wk, wv, wu = w_in[:, 0:qd], w_in[:, qd:qd + kd], w_in[:, qd + kd:qd + 2 * kd], w_in[:, qd + 2 * kd:]
    dup = lambda w: jnp.concatenate([w[:, 0:HEAD_DIM], w[:, 0:HEAD_DIM], w[:, HEAD_DIM:], w[:, HEAD_DIM:]], axis=1)
    wk2, wv2 = dup(wk), dup(wv)
    ud = wu.shape[1]
    w_all = jnp.concatenate([wq, wk2, wv2, wu, wq[:, _rope_partner_cols(qd)], wk2[:, _rope_partner_cols(2 * kd)]],
                            axis=1).astype(BF16)
    o_u = qd + 4 * kd
    segs = ((0, qd, o_u + ud), (qd, 2 * kd, o_u + ud + qd), (qd + 2 * kd, 2 * kd, None), (o_u, ud, None))
    q, k, v, u = _proj_call(xz, mod, norm_g0, w_all, rope[0], rope[1], segs, n // ROW_TILE, "proj_cd")
    oc = _swa_call(q, k, v, _pad_cols(sink.reshape(1, SWA_HEADS), LANES), n, nc, not last)

    ch = hy_bias.shape[-1]
    deltas = jnp.abs(jnp.linspace(HY_MIN_DECAY, HY_MAX_DECAY, ch, dtype=F32)).reshape(1, ch)
    hid = hy_w2.shape[0]
    w1p = jnp.pad(hy_w1, ((0, hid - hy_w1.shape[0]), (0, 0)))
    filt = lambda m, dft: _hyena_filter_call(*_hyena_feats(m), w1p, hy_b1.reshape(1, hid), hy_w2,
                                             hy_b2.reshape(1, hid), hy_freq, hy_w3, deltas, *dft)
    kr, ki, kn = filt(n, dft_x)
    od = _hyena_call(u, hy_conv, kr, ki, kn, hy_bias, *dft_x, n, 0, None)
    if not last:
        kr, ki, kn = filt(nc, dft_c)
        od = _hyena_call(u, hy_conv, kr, ki, kn, hy_bias, *dft_c, nc, n // nc, od)
    n_tiles = (n if last else n + nc) // ROW_TILE
    return _outproj_call(oc, od, w_out.astype(BF16), xz, mod, norm_g1, n_tiles, n // ROW_TILE)


def kernel(x, c, ctx, c_ctx, w_mod, b_mod, norm_g, ffn_w_up, ffn_conv, ffn_w_down, ab_w_in, ab_w_out, gdn_conv, gdn_a_log, gdn_dt_bias, gdn_norm_g, diff_lambda, diff_norm_g, cd_w_in, cd_w_out, swa_sink, hy_conv, hy_w1, hy_b1, hy_w2, hy_b2, hy_w3, hy_freq, hy_bias):
    b, n, d = x.shape
    nc = ctx.shape[1]
    depth = w_mod.shape[0]
    assert n % ROW_TILE == 0 and nc == ROW_TILE and n % GRID_W == 0
    xz = jnp.concatenate([x, ctx], axis=1)
    rows = -(-(b + 1) // SUBLANES) * SUBLANES
    cc = jnp.concatenate([c, c_ctx[None], jnp.zeros((rows - b - 1, d), F32)], axis=0)
    mods = _mod_call(cc, w_mod, b_mod)
    rope = _rope_tables(n, nc)
    dft_x = _dft_mats(n)
    dft_c = _dft_mats(nc)
    n_x_tiles = n // ROW_TILE
    for l in range(depth):
        last = l == depth - 1
        i = l // 2
        mx = mods[l, :b].reshape(b, 1, 6, d)
        mz = jnp.broadcast_to(mods[l, b].reshape(1, 1, 6, d), (b, 1, 6, d))
        mod = jnp.concatenate([mx, mz], axis=1)
        if l % 2 == 0:
            lam_init = 0.8 - 0.6 * math.exp(-0.3 * l)
            xz = _layer_ab(xz, mod, norm_g[l, 0], norm_g[l, 1], ab_w_in[i], ab_w_out[i], gdn_conv[i], gdn_a_log[i],
                           gdn_dt_bias[i], gdn_norm_g[i], diff_lambda[i], diff_norm_g[i], lam_init, rope, n, nc, GDN_SOLVE_PASSES)
        else:
            xz = _layer_cd(xz, mod, norm_g[l, 0], norm_g[l, 1], cd_w_in[i], cd_w_out[i], swa_sink[i], hy_conv[i],
                           hy_w1[i], hy_b1[i], hy_w2[i], hy_b2[i], hy_w3[i], hy_freq[i], hy_bias[i], rope, n, nc,
                           last, dft_x, dft_c)
        n_tiles = xz.shape[1] // ROW_TILE
        xz = _ffn_call(xz, mod, norm_g[l, 2], norm_g[l, 3], ffn_w_up[l].astype(BF16), ffn_conv[l],
                       ffn_w_down[l].astype(BF16), n_tiles, n_x_tiles)
    return xz[:, :n] if xz.shape[1] != n else xz
```

```python
import functools
import math
from typing import NamedTuple, Optional

import jax
import jax.numpy as jnp
import numpy as np
from jax import lax
from jax.experimental import pallas as pl
from jax.experimental.pallas import tpu as pltpu

F32 = jnp.float32
BF16 = jnp.bfloat16

EPS = 1e-6
NEG_INF = -1e30
GRID_W = 64
HEAD_DIM = 64
ROPE_BASE = 10000.0
GDN_HEADS = 8
GDN_DIM = 64
GDN_CHUNK = 64
DIFF_HEADS = 4
DIFF_DIM = 64
SWA_HEADS = 8
SWA_KV_HEADS = 2
SWA_WINDOW = 128
SWA_BLOCK = 128
HY_BANDS = 16
HY_MIN_DECAY = math.log(1e-2) / 1.5
HY_MAX_DECAY = math.log(1e-2) / 0.3
HY_ROW_CHUNK = 512

GDN_SOLVE_PASSES = (3, 3, 3, 3, 3)

LANES = 128
SUBLANES = 8
MXU_WIDTH = 256
ROW_TILE = 256
VMEM_LIMIT = 56 * 1024 * 1024


def _cparams(sem):
    return pltpu.CompilerParams(dimension_semantics=sem, vmem_limit_bytes=VMEM_LIMIT)


def _resident(shape):
    zeros = (0,) * len(shape)
    return pl.BlockSpec(shape, lambda *_: zeros, pipeline_mode=pl.Buffered(1))


def _log2(v):
    assert v & (v - 1) == 0
    return v.bit_length() - 1


def _sigmoid(x):
    return 1.0 / (1.0 + jnp.exp(-x))


def _silu(x):
    return x * _sigmoid(x)


def _softplus(x):
    return jnp.maximum(x, 0.0) + jnp.log1p(jnp.exp(-jnp.abs(x)))


def _dot(a, b):
    return jnp.dot(a, b, preferred_element_type=F32)


def _dot_nt(a, b):
    return lax.dot_general(a, b, (((1,), (1,)), ((), ())), preferred_element_type=F32)


def _dot_tn(a, b):
    return lax.dot_general(a, b, (((0,), (0,)), ((), ())), preferred_element_type=F32)


def _dot_f32(a, b):
    return jnp.dot(a, b, preferred_element_type=F32, precision=lax.Precision.HIGHEST)


def _split2(x):
    hi = x.astype(BF16)
    lo = (x - hi.astype(F32)).astype(BF16)
    return hi, lo


def _split3(x):
    hi = x.astype(BF16)
    r = x - hi.astype(F32)
    mid = r.astype(BF16)
    lo = (r - mid.astype(F32)).astype(BF16)
    return hi, mid, lo


def _dot_sel(x, sel_bf16):
    hi, mid, lo = _split3(x)
    return _dot(hi, sel_bf16) + _dot(mid, sel_bf16) + _dot(lo, sel_bf16)


def _mm(a, b, passes):
    if passes == 1:
        return _dot(a.astype(BF16), b.astype(BF16))
    ah, al = _split2(a)
    bh, bl = _split2(b)
    return _dot(ah, bh) + _dot(al, bh) + _dot(ah, bl)


def _rms(y, g):
    return y * lax.rsqrt(jnp.mean(y * y, axis=-1, keepdims=True) + EPS) * g


def _modnorm(x, g, shift, scale):
    return _rms(x, g) * (1.0 + scale) + shift


def _mod_kernel(cc_ref, w_ref, b_ref, o_ref):
    s = _silu(cc_ref[...])
    o_ref[0] = _dot(s.astype(BF16), w_ref[0].astype(BF16)) + b_ref[0]


def _mod_call(cc, w_mod, b_mod):
    depth, d, nm = w_mod.shape
    rows = cc.shape[0]
    ct = 1536
    return pl.pallas_call(
        _mod_kernel,
        grid=(depth, nm // ct),
        in_specs=[
            pl.BlockSpec((rows, d), lambda l, j: (0, 0)),
            pl.BlockSpec((1, d, ct), lambda l, j: (l, 0, j)),
            pl.BlockSpec((1, 1, ct), lambda l, j: (l, 0, j)),
        ],
        out_specs=pl.BlockSpec((1, rows, ct), lambda l, j: (l, 0, j)),
        out_shape=jax.ShapeDtypeStruct((depth, rows, nm), F32),
        compiler_params=_cparams(("arbitrary", "arbitrary")),
        name="adaln_mod",
    )(cc, w_mod, b_mod.reshape(depth, 1, nm))


class _Seg(NamedTuple):
    start: int
    width: int
    rot_start: Optional[int] = None
    scale: float = 1.0
    dtype: type = F32
    transposed: bool = False


def _proj_kernel(x_ref, mod_ref, g_ref, w_ref, cos_ref, sin_ref, *out_refs, segs):
    m = mod_ref[0, 0]
    h = _modnorm(x_ref[0], g_ref[...], m[0:1], m[1:2]).astype(BF16)
    for o_ref, seg in zip(out_refs, segs):
        y = _dot(h, w_ref[:, seg.start:seg.start + seg.width])
        if seg.rot_start is not None:
            yr = _dot(h, w_ref[:, seg.rot_start:seg.rot_start + seg.width])
            reps = seg.width // LANES
            cos = jnp.concatenate([cos_ref[...]] * reps, axis=1)
            sin = jnp.concatenate([sin_ref[...]] * reps, axis=1)
            y = y * cos + yr * sin
        if seg.scale != 1.0:
            y = y * seg.scale
        if seg.transposed:
            y = y.T
        o_ref[0] = y.astype(seg.dtype)


def _proj_call(xz, mod, g, w, cos_t, sin_t, segs, n_x_tiles, name):
    b, l, d = xz.shape
    tm = ROW_TILE
    nt = l // tm
    p = w.shape[1]
    return pl.pallas_call(
        functools.partial(_proj_kernel, segs=segs),
        grid=(nt, b),
        in_specs=[
            pl.BlockSpec((1, tm, d), lambda t, i: (i, t, 0)),
            pl.BlockSpec((1, 1, 6, d), lambda t, i: (i, t // n_x_tiles, 0, 0)),
            pl.BlockSpec((1, d), lambda t, i: (0, 0)),
            _resident((d, p)),
            pl.BlockSpec((tm, LANES), lambda t, i: (t, 0)),
            pl.BlockSpec((tm, LANES), lambda t, i: (t, 0)),
        ],
        out_specs=[pl.BlockSpec((1, s.width, tm), lambda t, i: (i, 0, t)) if s.transposed
                   else pl.BlockSpec((1, tm, s.width), lambda t, i: (i, t, 0)) for s in segs],
        out_shape=[jax.ShapeDtypeStruct((b, s.width, l) if s.transposed else (b, l, s.width), s.dtype)
                   for s in segs],
        compiler_params=_cparams(("arbitrary", "arbitrary")),
        name=name,
    )(xz, mod, g.reshape(1, d), w, cos_t, sin_t)


def _outproj_kernel(o1_ref, o2_ref, w_ref, x_ref, mod_ref, g_ref, out_ref):
    k1 = o1_ref.shape[-1]
    y = _dot(o1_ref[0].astype(BF16), w_ref[0:k1, :]) + _dot(o2_ref[0].astype(BF16), w_ref[k1:, :])
    m = mod_ref[0, 0]
    out_ref[0] = x_ref[0] + m[2:3] * _rms(y, g_ref[...])


def _outproj_call(o1, o2, w, xz, mod, g, n_tiles, n_x_tiles):
    b, _, d = xz.shape
    tm = ROW_TILE
    k1, k2 = o1.shape[-1], o2.shape[-1]
    return pl.pallas_call(
        _outproj_kernel,
        grid=(n_tiles, b),
        in_specs=[
            pl.BlockSpec((1, tm, k1), lambda t, i: (i, t, 0)),
            pl.BlockSpec((1, tm, k2), lambda t, i: (i, t, 0)),
            _resident((k1 + k2, d)),
            pl.BlockSpec((1, tm, d), lambda t, i: (i, t, 0)),
            pl.BlockSpec((1, 1, 6, d), lambda t, i: (i, t // n_x_tiles, 0, 0)),
            pl.BlockSpec((1, d), lambda t, i: (0, 0)),
        ],
        out_specs=pl.BlockSpec((1, tm, d), lambda t, i: (i, t, 0)),
        out_shape=jax.ShapeDtypeStruct((b, n_tiles * tm, d), F32),
        compiler_params=_cparams(("arbitrary", "arbitrary")),
        name="mixer_out",
    )(o1, o2, w, xz, mod, g.reshape(1, d))


def _ffn_kernel(xp_ref, x_ref, xn_ref, mod_ref, g2_ref, g3_ref, wup_ref, cw_ref, wdn_ref, out_ref,
                ua_ref, ug_ref, *, tm, n_x_tiles, n_tiles, cf, dff):
    t = pl.program_id(0)
    first = jnp.logical_or(t == 0, t == n_x_tiles)
    last = jnp.logical_or(t == n_x_tiles - 1, t == n_tiles - 1)
    keep_top = jnp.where(first, 0.0, 1.0)
    keep_bot = jnp.where(last, 0.0, 1.0)
    m = mod_ref[0, 0]
    halo = SUBLANES
    xe = jnp.concatenate([xp_ref[0], x_ref[0], xn_ref[0]], axis=0)
    h = _modnorm(xe, g2_ref[...], m[3:4], m[4:5]).astype(BF16)
    acc = jnp.zeros((tm, x_ref.shape[-1]), F32)
    for c0 in range(0, dff, cf):
        wd = min(cf, dff - c0)
        halves = []
        for ref, base in ((ua_ref, c0), (ug_ref, dff + c0)):
            u = _dot(h, wup_ref[:, base:base + wd])
            ref[:, 0:wd] = u
            ref[0:halo, 0:wd] = u[0:halo] * keep_top
            ref[tm + halo:tm + 2 * halo, 0:wd] = u[tm + halo:] * keep_bot
            cw = cw_ref[:, base:base + wd]
            halves.append(cw[0:1] * ref[halo - 1:halo - 1 + tm, 0:wd] + cw[1:2] * ref[halo:halo + tm, 0:wd]
                          + cw[2:3] * ref[halo + 1:halo + 1 + tm, 0:wd])
        act = (_silu(halves[1]) * halves[0]).astype(BF16)
        acc = acc + _dot(act, wdn_ref[c0:c0 + wd, :])
    out_ref[0] = x_ref[0] + m[5:6] * _rms(acc, g3_ref[...])


def _ffn_call(xz, mod, g2, g3, w_up, conv_w, w_down, n_tiles, n_x_tiles):
    b, l, d = xz.shape
    tm = ROW_TILE
    dff = w_down.shape[0]
    cf = -(-dff // (2 * MXU_WIDTH)) * MXU_WIDTH
    hb = tm // SUBLANES
    nb8 = l // SUBLANES
    kern = functools.partial(_ffn_kernel, tm=tm, n_x_tiles=n_x_tiles, n_tiles=n_tiles, cf=cf, dff=dff)
    return pl.pallas_call(
        kern,
        grid=(n_tiles, b),
        in_specs=[
            pl.BlockSpec((1, SUBLANES, d), lambda t, i: (i, jnp.maximum(t * hb - 1, 0), 0)),
            pl.BlockSpec((1, tm, d), lambda t, i: (i, t, 0)),
            pl.BlockSpec((1, SUBLANES, d), lambda t, i: (i, jnp.minimum((t + 1) * hb, nb8 - 1), 0)),
            pl.BlockSpec((1, 1, 6, d), lambda t, i: (i, t // n_x_tiles, 0, 0)),
            pl.BlockSpec((1, d), lambda t, i: (0, 0)),
            pl.BlockSpec((1, d), lambda t, i: (0, 0)),
            _resident((d, 2 * dff)),
            pl.BlockSpec((3, 2 * dff), lambda t, i: (0, 0)),
            _resident((dff, d)),
        ],
        out_specs=pl.BlockSpec((1, tm, d), lambda t, i: (i, t, 0)),
        out_shape=jax.ShapeDtypeStruct((b, n_tiles * tm, d), F32),
        scratch_shapes=[pltpu.VMEM((tm + 2 * SUBLANES, cf), F32), pltpu.VMEM((tm + 2 * SUBLANES, cf), F32)],
        compiler_params=_cparams(("arbitrary", "arbitrary")),
        name="conv_ffn",
    )(xz, xz, xz, mod, g2.reshape(1, d), g3.reshape(1, d), w_up, conv_w, w_down)


def _half_sums(x2, lane_lo):
    s0 = jnp.sum(jnp.where(lane_lo, x2, 0.0), axis=-1, keepdims=True)
    s1 = jnp.sum(jnp.where(lane_lo, 0.0, x2), axis=-1, keepdims=True)
    return jnp.where(lane_lo, s0, s1)


def _gdn_kernel(qkvg_ref, ba_ref, cw_ref, gp_ref, ng_ref, out_ref,
                pad_ref, q_ref, k_ref, v_ref, bb_ref, gb_ref, qe_ref, mp_ref, ou_ref, nn_ref, egl_ref, o_ref,
                *, n, nc, solve_passes, chunks_per_iter):
    l = n + nc
    c = GDN_CHUNK
    n_chunks = l // c
    pair = pl.program_id(1)
    halo = SUBLANES
    lane = lax.broadcasted_iota(jnp.int32, (1, LANES), 1)
    lane_lo = lane < GDN_DIM

    cw = cw_ref[:, 0:3 * LANES]
    zero_rows = jnp.zeros((halo, 3 * LANES), F32)
    for seq_start, seq_len in ((0, n), (n, nc)):
        base = halo + seq_start + (2 * halo if seq_start else 0)
        pad_ref[base - halo:base, :] = zero_rows
        pad_ref[base + seq_len:base + seq_len + halo, :] = zero_rows
        step = 256
        for r in range(0, seq_len, step):
            pad_ref[base + r:base + r + step, :] = qkvg_ref[0, seq_start + r:seq_start + r + step, 0:3 * LANES]
        for r in range(0, seq_len, step):
            y = (cw[0:1] * pad_ref[base + r - 1:base + r - 1 + step, :]
                 + cw[1:2] * pad_ref[base + r:base + r + step, :]
                 + cw[2:3] * pad_ref[base + r + 1:base + r + 1 + step, :])
            y = _silu(y)
            q = y[:, 0:LANES]
            k = y[:, LANES:2 * LANES]
            rows = slice(seq_start + r, seq_start + r + step)
            q_ref[rows, :] = q * lax.rsqrt(_half_sums(q * q, lane_lo) + EPS) * (GDN_DIM ** -0.5)
            k_ref[rows, :] = k * lax.rsqrt(_half_sums(k * k, lane_lo) + EPS)
            v_ref[rows, :] = y[:, 2 * LANES:3 * LANES]

    sel_r = lax.broadcasted_iota(jnp.int32, (LANES, 4 * LANES), 0)
    sel_c = lax.broadcasted_iota(jnp.int32, (LANES, 4 * LANES), 1)
    quarter = sel_c >> _log2(LANES)
    src_lane = (quarter & 1) * 2 * GDN_HEADS + (quarter >> 1) * GDN_HEADS + 2 * pair + ((sel_c >> _log2(GDN_DIM)) & 1)
    sel = (sel_r == src_lane).astype(BF16)
    gblk = 256
    bi = lax.broadcasted_iota(jnp.int32, (gblk, gblk), 0)
    bj = lax.broadcasted_iota(jnp.int32, (gblk, gblk), 1)
    same_chunk = (bi >> _log2(c)) == (bj >> _log2(c))
    csum = (jnp.logical_and(same_chunk, bi >= bj).astype(BF16), jnp.logical_and(same_chunk, bi <= bj).astype(BF16))
    neg_a = -jnp.exp(gp_ref[0:1, :])
    dt_bias = gp_ref[1:2, :]
    for r in range(0, l, gblk):
        ba = ba_ref[0, r:r + gblk, :]
        gates = jnp.where(lane < 2 * GDN_HEADS, _sigmoid(ba), neg_a * _softplus(ba + dt_bias))
        x = _dot_sel(gates, sel)
        for d in range(2):
            bb_ref[d, r:r + gblk, :] = x[:, 2 * d * LANES:(2 * d + 1) * LANES]
            gb_ref[d, r:r + gblk, :] = _dot_sel_lhs(csum[d], x[:, (2 * d + 1) * LANES:(2 * d + 2) * LANES])

    r2 = lax.broadcasted_iota(jnp.int32, (2 * c, 2 * c), 0)
    c2 = lax.broadcasted_iota(jnp.int32, (2 * c, 2 * c), 1)
    same_head = (r2 >= c) == (c2 >= c)
    eye = (r2 == c2).astype(F32)
    masks = ((jnp.logical_and(same_head, r2 >= c2), jnp.logical_and(same_head, r2 > c2)),
             (jnp.logical_and(same_head, r2 <= c2), jnp.logical_and(same_head, r2 < c2)))
    m0 = lane_lo.astype(F32)
    m1 = 1.0 - m0

    def stack_heads(x2):
        return jnp.concatenate([x2 * m0, x2 * m1], axis=0)

    def fold_heads(x):
        return x[0:c] + x[c:2 * c]

    def chunks_local(dirs, qs, ks, vs, betas, gcs):
        each = lambda f, *cols: [f(*args) for args in zip(*cols)]
        incl = [masks[d][0] for d in dirs]
        strict = [masks[d][1] for d in dirs]
        g1 = each(lambda gc2: jnp.concatenate([gc2, gc2], axis=0), gcs)
        decay = each(lambda g, m: jnp.where(m, jnp.exp(jnp.where(m, g - g.T, 0.0)), 0.0), g1, incl)
        kb = each(lambda k, b: k * b, ks, betas)
        kst = each(lambda k: stack_heads(k).astype(BF16), ks)
        a_raw = each(lambda x, y: _dot_nt(stack_heads(x).astype(BF16), y), kb, kst)
        qk_raw = each(lambda x, y: _dot_nt(stack_heads(x).astype(BF16), y), qs, kst)
        qk = each(lambda m, x, dc: jnp.where(m, x * dc, 0.0).astype(BF16), incl, qk_raw, decay)
        p = each(lambda m, x, dc: jnp.where(m, -(x * dc), 0.0), strict, a_raw, decay)
        tinv = each(lambda x: eye + x, p)
        assert len(solve_passes) == _log2(c) - 1
        for passes in solve_passes:
            p = each(lambda x: _mm(x, x, passes), p)
            tinv = each(lambda t, x: t + _mm(t, x, passes), tinv, p)
        egc = each(jnp.exp, gcs)
        rhs = each(lambda v, b, x, e: jnp.concatenate([stack_heads(v * b), stack_heads(x * e)], axis=1),
                   vs, betas, kb, egc)
        sol = each(lambda t, r: _mm(t, r, 3), tinv, rhs)
        u2 = each(lambda x: fold_heads(x[:, 0:LANES]), sol)
        w2 = each(lambda x: fold_heads(x[:, LANES:2 * LANES]), sol)
        gl = each(lambda d, gc2: gc2[c - 1:c, :] if d == 0 else gc2[0:1, :], dirs, gcs)
        ktail = each(lambda k, g, gc2: (k * jnp.exp(g - gc2)).astype(BF16), ks, gl, gcs)
        qwu = each(lambda x, w, u: _dot(x, jnp.concatenate([stack_heads(w), stack_heads(u)], axis=1).astype(BF16)),
                   qk, w2, u2)
        kwu = each(lambda x, w, u: _dot_tn(x, jnp.concatenate([w, u], axis=1).astype(BF16)), ktail, w2, u2)
        q_eff = each(lambda q, e, x: (q * e - fold_heads(x[:, 0:LANES])).astype(BF16), qs, egc, qwu)
        m_neg = each(lambda x: jnp.where(same_head, -x[:, 0:LANES], 0.0).astype(BF16), kwu)
        o_loc = each(lambda x: fold_heads(x[:, LANES:2 * LANES]), qwu)
        s_loc = each(lambda x: jnp.where(same_head, x[:, LANES:2 * LANES], 0.0), kwu)
        egl = each(lambda g: jnp.broadcast_to(jnp.exp(g), (SUBLANES, LANES)), gl)
        return q_eff, m_neg, o_loc, s_loc, egl

    def chunk_rows(chunk, rows_per_chunk):
        return pl.ds(pl.multiple_of(chunk * rows_per_chunk, rows_per_chunk), rows_per_chunk)

    def local_body(it, carry):
        dirs, chunks, qs, ks, vs, betas, gcs = [], [], [], [], [], [], []
        for g in range(chunks_per_iter):
            chunk = it + g * (n_chunks // chunks_per_iter)
            rows = chunk_rows(chunk, c)
            for d in range(2):
                dirs.append(d)
                chunks.append(chunk)
                qs.append(q_ref[rows, :])
                ks.append(k_ref[rows, :])
                vs.append(v_ref[rows, :])
                betas.append(bb_ref[d, rows, :])
                gcs.append(gb_ref[d, rows, :])
        results = chunks_local(dirs, qs, ks, vs, betas, gcs)
        for d, chunk, q_eff, m_neg, o_loc, s_loc, egl in zip(dirs, chunks, *results):
            qe_ref[d, chunk_rows(chunk, c), :] = q_eff
            mp_ref[d, chunk_rows(chunk, 2 * c), :] = m_neg
            ou_ref[d, chunk_rows(chunk, c), :] = o_loc
            nn_ref[d, chunk_rows(chunk, 2 * c), :] = s_loc
            egl_ref[d, chunk_rows(chunk, SUBLANES), :] = egl
        return carry

    lax.fori_loop(0, n_chunks // chunks_per_iter, local_body, 0)

    ctx_chunks = nc // c

    def scan_body(s, carry):
        chunks = (jnp.where(s < ctx_chunks, s + n // c, s - ctx_chunks), n_chunks - 1 - s)
        loaded = [(qe_ref[d, chunk_rows(ch, c), :], mp_ref[d, chunk_rows(ch, 2 * c), :],
                   ou_ref[d, chunk_rows(ch, c), :], nn_ref[d, chunk_rows(ch, 2 * c), :],
                   egl_ref[d, chunk_rows(ch, SUBLANES), :]) for d, ch in enumerate(chunks)]
        res = [_dot(jnp.concatenate([ld[0], ld[1]], axis=0), s2.astype(BF16)) for ld, s2 in zip(loaded, carry)]
        for d, ch in enumerate(chunks):
            o_ref[d, chunk_rows(ch, c), :] = res[d][0:c] + loaded[d][2]
        return tuple(s2 * ld[4][0:1] + r[c:3 * c] + ld[3] for s2, ld, r in zip(carry, loaded, res))

    zero_state = jnp.zeros((2 * c, 2 * c), F32)
    lax.fori_loop(0, n_chunks, scan_body, (zero_state, zero_state))

    ng = ng_ref[...]
    step = 256
    for r in range(0, l, step):
        o = o_ref[0, r:r + step, :] + o_ref[1, r:r + step, :]
        ms = _half_sums(o * o, lane_lo) * (1.0 / GDN_DIM)
        gate = qkvg_ref[0, r:r + step, 3 * LANES:4 * LANES]
        out_ref[0, r:r + step, :] = o * lax.rsqrt(ms + EPS) * ng * _silu(gate)


def _dot_sel_lhs(sel_bf16, x):
    hi, mid, lo = _split3(x)
    return _dot(sel_bf16, hi) + _dot(sel_bf16, mid) + _dot(sel_bf16, lo)


def _gdn_call(qkvg, ba, conv_w, gate_params, ng, n, nc, solve_passes):
    b, l, _ = qkvg.shape
    pairs = GDN_HEADS // 2
    n_chunks = l // GDN_CHUNK
    kern = functools.partial(_gdn_kernel, n=n, nc=nc, solve_passes=solve_passes, chunks_per_iter=4)
    return pl.pallas_call(
        kern,
        grid=(b, pairs),
        in_specs=[
            pl.BlockSpec((1, l, 4 * LANES), lambda i, p: (i, 0, p)),
            pl.BlockSpec((1, l, LANES), lambda i, p: (i, 0, 0)),
            pl.BlockSpec((3, 4 * LANES), lambda i, p: (0, p)),
            pl.BlockSpec((2, LANES), lambda i, p: (0, 0)),
            pl.BlockSpec((1, LANES), lambda i, p: (0, 0)),
        ],
        out_specs=pl.BlockSpec((1, l, LANES), lambda i, p: (i, 0, p)),
        out_shape=jax.ShapeDtypeStruct((b, l, pairs * LANES), F32),
        scratch_shapes=[
            pltpu.VMEM((l + 5 * SUBLANES, 3 * LANES), F32),
            pltpu.VMEM((l, LANES), F32),
            pltpu.VMEM((l, LANES), F32),
            pltpu.VMEM((l, LANES), F32),
            pltpu.VMEM((2, l, LANES), F32),
            pltpu.VMEM((2, l, LANES), F32),
            pltpu.VMEM((2, l, LANES), BF16),
            pltpu.VMEM((2, 2 * l, LANES), BF16),
            pltpu.VMEM((2, l, LANES), F32),
            pltpu.VMEM((2, 2 * l, LANES), F32),
            pltpu.VMEM((2, n_chunks * SUBLANES, LANES), F32),
            pltpu.VMEM((2, l, LANES), F32),
        ],
        compiler_params=_cparams(("arbitrary", "arbitrary")),
        name="gated_deltanet",
    )(qkvg, ba, conv_w, gate_params, ng)


def _diff_kernel(q_ref, k_ref, vt_ref, lam_ref, ng_ref, o_ref, *, n, lam_init, n_x_tiles):
    t = pl.program_id(2)
    lp = lam_ref[...]
    lam = (jnp.exp(jnp.sum(lp[0:1] * lp[1:2], axis=-1, keepdims=True))
           - jnp.exp(jnp.sum(lp[2:3] * lp[3:4], axis=-1, keepdims=True)) + lam_init)
    lane = lax.broadcasted_iota(jnp.int32, (1, LANES), 1)
    lane_lo = lane < DIFF_DIM
    q = q_ref[0]
    qt = [jnp.where(m, q, jnp.zeros_like(q)) for m in (lane_lo, jnp.logical_not(lane_lo))]
    ng = ng_ref[...]

    def attend(k, vt):
        s = [_dot_nt(k, x) for x in qt]
        e = [jnp.exp(x - jnp.max(x, axis=0, keepdims=True)) for x in s]
        pv = [_dot(vt, x.astype(BF16)) for x in e]
        parts = [x * (1.0 / jnp.sum(y, axis=0, keepdims=True)) for x, y in zip(pv, e)]
        ot = parts[0] - lam * parts[1]
        ot = ot * lax.rsqrt(jnp.mean(ot * ot, axis=0, keepdims=True) + EPS)
        o_ref[0] = ot.T * ng * (1.0 - lam_init)

    @pl.when(t < n_x_tiles)
    def _():
        attend(k_ref[0], vt_ref[0])

    @pl.when(t >= n_x_tiles)
    def _():
        attend(k_ref[0, n:, :], vt_ref[0, :, n:])


def _diff_call(dq, dk, dvt, lam_p, ng, n, lam_init):
    b, l, _ = dq.shape
    tq = ROW_TILE
    nt = l // tq
    kern = functools.partial(_diff_kernel, n=n, lam_init=lam_init, n_x_tiles=n // tq)
    return pl.pallas_call(
        kern,
        grid=(b, DIFF_HEADS, nt),
        in_specs=[
            pl.BlockSpec((1, tq, LANES), lambda i, h, t: (i, t, h)),
            pl.BlockSpec((1, l, LANES), lambda i, h, t: (i, 0, h)),
            pl.BlockSpec((1, LANES, l), lambda i, h, t: (i, h, 0)),
            pl.BlockSpec((4, DIFF_DIM), lambda i, h, t: (0, 0)),
            pl.BlockSpec((1, LANES), lambda i, h, t: (0, 0)),
        ],
        out_specs=pl.BlockSpec((1, tq, LANES), lambda i, h, t: (i, t, h)),
        out_shape=jax.ShapeDtypeStruct((b, l, DIFF_HEADS * LANES), F32),
        compiler_params=_cparams(("arbitrary", "arbitrary", "arbitrary")),
        name="diff_attention",
    )(dq, dk, dvt, lam_p, ng.reshape(1, LANES))


def _swa_kernel(q_ref, k_ref, v_ref, sink_ref, o_ref, *, n, nc):
    t = pl.program_id(1)
    blk = SWA_BLOCK
    n_x = n // blk
    q = q_ref[0]
    lane = lax.broadcasted_iota(jnp.int32, (1, LANES), 1)
    lane_lo = lane < HEAD_DIM
    sink = sink_ref[...]
    group = SWA_HEADS // SWA_KV_HEADS

    def run(keys, vals, valid):
        head_of_row = lax.broadcasted_iota(jnp.int32, (group * blk, 1), 0) >> _log2(blk)
        kvs = range(SWA_KV_HEADS)
        kk = [keys[:, kvh * LANES:(kvh + 1) * LANES] for kvh in kvs]
        vv = [vals[:, kvh * LANES:(kvh + 1) * LANES] for kvh in kvs]
        qst, sk = [], []
        for kvh in kvs:
            q_rows = []
            sk_rows = jnp.zeros((group * blk, 1), F32)
            for g in range(group):
                h = kvh * group + g
                qp = q[:, (h // 2) * LANES:(h // 2 + 1) * LANES]
                q_rows.append(jnp.where(lane_lo if h % 2 == 0 else jnp.logical_not(lane_lo), qp, jnp.zeros_like(qp)))
                sk_rows = jnp.where(head_of_row == g, sink[:, h:h + 1], sk_rows)
            qst.append(jnp.concatenate(q_rows, axis=0))
            sk.append(sk_rows)
        s = [_dot_nt(x, y) for x, y in zip(qst, kk)]
        if valid is not None:
            s = [jnp.where(valid, x, NEG_INF) for x in s]
        mx = [jnp.maximum(jnp.max(x, axis=-1, keepdims=True), y) for x, y in zip(s, sk)]
        e = [jnp.exp(x - m) for x, m in zip(s, mx)]
        pv = [_dot(x.astype(BF16), y) for x, y in zip(e, vv)]
        den = [jnp.sum(x, axis=-1, keepdims=True) + jnp.exp(y - m) for x, y, m in zip(e, sk, mx)]
        outs = []
        for o, dn in zip(pv, den):
            o = o * (1.0 / dn)
            for g in range(0, group, 2):
                outs.append(jnp.where(lane_lo, o[g * blk:(g + 1) * blk], o[(g + 1) * blk:(g + 2) * blk]))
        o_ref[0] = jnp.concatenate(outs, axis=1)

    @pl.when(t < n_x)
    def _():
        start = pl.multiple_of(jnp.clip((t - 1) * blk, 0, n - 3 * blk), blk)
        keys = jnp.concatenate([k_ref[0, pl.ds(start, 3 * blk), :], k_ref[0, n:n + nc, :]], axis=0)
        vals = jnp.concatenate([v_ref[0, pl.ds(start, 3 * blk), :], v_ref[0, n:n + nc, :]], axis=0)
        shape = (group * blk, 3 * blk + nc)
        qpos = t * blk + (lax.broadcasted_iota(jnp.int32, shape, 0) & (blk - 1))
        col = lax.broadcasted_iota(jnp.int32, shape, 1)
        dist = qpos - (start + col)
        in_window = jnp.logical_and(dist <= SWA_WINDOW, dist >= -SWA_WINDOW)
        valid = jnp.logical_or(col >= 3 * blk, in_window)
        run(keys, vals, valid)

    @pl.when(t >= n_x)
    def _():
        run(k_ref[0, n:n + nc, :], v_ref[0, n:n + nc, :], None)


def _swa_call(q, k, v, sink, n, nc, with_ctx):
    b, l, _ = q.shape
    blk = SWA_BLOCK
    nt = (l if with_ctx else n) // blk
    kern = functools.partial(_swa_kernel, n=n, nc=nc)
    return pl.pallas_call(
        kern,
        grid=(b, nt),
        in_specs=[
            pl.BlockSpec((1, blk, SWA_HEADS * HEAD_DIM), lambda i, t: (i, t, 0)),
            pl.BlockSpec((1, l, 2 * LANES), lambda i, t: (i, 0, 0)),
            pl.BlockSpec((1, l, 2 * LANES), lambda i, t: (i, 0, 0)),
            pl.BlockSpec((1, LANES), lambda i, t: (0, 0)),
        ],
        out_specs=pl.BlockSpec((1, blk, SWA_HEADS * HEAD_DIM), lambda i, t: (i, t, 0)),
        out_shape=jax.ShapeDtypeStruct((b, l, SWA_HEADS * HEAD_DIM), F32),
        compiler_params=_cparams(("arbitrary", "arbitrary")),
        name="window_attention",
    )(q, k, v, sink)


def _dft_mats(n):
    k = jnp.arange(n, dtype=jnp.int32)
    km = (k[:, None] * k[None, :]) % (2 * n)
    ang = km.astype(F32) * (math.pi / n)
    return jnp.cos(ang).astype(BF16), (-jnp.sin(ang)).astype(BF16)


def _hyena_filter_kernel(ff_ref, fb_ref, w1_ref, b1_ref, w2_ref, b2_ref, freq_ref, w3f_ref, w3b_ref,
                         dl_ref, c_ref, s_ref, kr_ref, ki_ref, kn_ref):
    n = ff_ref.shape[0]
    freq = freq_ref[...]

    def mlp(feat):
        h = jnp.sin(freq[0:1] * (_dot_f32(feat, w1_ref[...]) + b1_ref[...]))
        return jnp.sin(freq[1:2] * (_dot_f32(h, w2_ref[...]) + b2_ref[...]))

    ff = ff_ref[...]
    fb = fb_ref[...]
    dl = dl_ref[...]
    row = lax.broadcasted_iota(jnp.int32, (n, 1), 0)
    kf = _dot_f32(mlp(ff), w3f_ref[...]) * jnp.exp(-ff[:, 0:1] * dl)
    kb = _dot_f32(mlp(fb), w3b_ref[...]) * jnp.exp(-fb[:, 0:1] * dl)
    kb = jnp.where(row == 0, 0.0, kb)
    ss = jnp.sum(kf * kf, axis=0, keepdims=True) + jnp.sum(kb * kb, axis=0, keepdims=True)
    sc = lax.rsqrt(ss + EPS)
    kf = kf * sc
    kb = kb * sc
    sgn = jnp.where((row & 1) == 0, 1.0, -1.0)
    cm = c_ref[...]
    sm = s_ref[...]
    fh, fl = _split2(kf)
    bh, bl = _split2(kb)
    kr_ref[0] = _dot(cm, fh) + _dot(cm, fl) + sgn * (_dot(cm, bh) + _dot(cm, bl))
    ki_ref[0] = _dot(sm, fh) + _dot(sm, fl) + sgn * (_dot(sm, bh) + _dot(sm, bl))
    nyq = jnp.sum((kf + kb) * sgn, axis=0, keepdims=True)
    kn_ref[0] = jnp.broadcast_to(nyq, (SUBLANES, nyq.shape[-1]))


def _hyena_filter_call(featf, featb, w1, b1, w2, b2, freq, w3, deltas, cm, sm):
    n = featf.shape[0]
    hid = w2.shape[0]
    ch = deltas.shape[-1]
    tc = 2 * LANES
    nct = ch // tc
    return pl.pallas_call(
        _hyena_filter_kernel,
        grid=(2, nct),
        in_specs=[
            pl.BlockSpec((n, hid), lambda o, j: (0, 0)),
            pl.BlockSpec((n, hid), lambda o, j: (0, 0)),
            pl.BlockSpec((hid, hid), lambda o, j: (0, 0)),
            pl.BlockSpec((1, hid), lambda o, j: (0, 0)),
            pl.BlockSpec((hid, hid), lambda o, j: (0, 0)),
            pl.BlockSpec((1, hid), lambda o, j: (0, 0)),
            pl.BlockSpec((2, hid), lambda o, j: (0, 0)),
            pl.BlockSpec((hid, tc), lambda o, j: (0, (2 * o) * nct + j)),
            pl.BlockSpec((hid, tc), lambda o, j: (0, (2 * o + 1) * nct + j)),
            pl.BlockSpec((1, tc), lambda o, j: (0, j)),
            _resident((n, n)),
            _resident((n, n)),
        ],
        out_specs=[
            pl.BlockSpec((1, n, tc), lambda o, j: (o, 0, j)),
            pl.BlockSpec((1, n, tc), lambda o, j: (o, 0, j)),
            pl.BlockSpec((1, SUBLANES, tc), lambda o, j: (o, 0, j)),
        ],
        out_shape=[
            jax.ShapeDtypeStruct((2, n, ch), F32),
            jax.ShapeDtypeStruct((2, n, ch), F32),
            jax.ShapeDtypeStruct((2, SUBLANES, ch), F32),
        ],
        compiler_params=_cparams(("arbitrary", "arbitrary")),
        name="hyena_filters",
    )(featf, featb, w1, b1, w2, b2, freq, w3, w3, deltas, cm, sm)


def _hyena_kernel(*refs, n, aliased):
    if aliased:
        refs = refs[1:]
    (v_ref, x1_ref, x2_ref, cwv_ref, cw1_ref, cw2_ref, kr_ref, ki_ref, kn_ref, bias_ref, c_ref, s_ref,
     o_ref, pad_ref, z_ref, zb_ref, p_ref) = refs
    halo = SUBLANES
    tc = o_ref.shape[-1]
    rc = min(n, HY_ROW_CHUNK)
    zero_rows = jnp.zeros((halo, tc), F32)
    pad_ref[0:halo, :] = zero_rows
    pad_ref[halo + n:2 * halo + n, :] = zero_rows

    def stage(ref):
        for r in range(0, n, rc):
            pad_ref[halo + r:halo + r + rc, :] = ref[0, r:r + rc, :]

    def conv_rows(cw, r):
        return (cw[0:1] * pad_ref[halo - 1 + r:halo - 1 + r + rc, :] + cw[1:2] * pad_ref[halo + r:halo + r + rc, :]
                + cw[2:3] * pad_ref[halo + 1 + r:halo + 1 + r + rc, :])

    def sign_rows(r):
        row = r + lax.broadcasted_iota(jnp.int32, (rc, 1), 0)
        return row, jnp.where((row & 1) == 0, 1.0, -1.0)

    stage(v_ref)
    cw = cwv_ref[...]
    for r in range(0, n, rc):
        z = conv_rows(cw, r)
        z_ref[r:r + rc, :] = z
        zb_ref[r:r + rc, :] = z.astype(BF16)

    for o, (gate_ref, gate_cw_ref) in enumerate(((x1_ref, cw1_ref), (x2_ref, cw2_ref))):
        znyq = jnp.zeros((1, tc), F32)
        for r in range(0, n, rc):
            row, sgn = sign_rows(r)
            znyq = znyq + jnp.sum(z_ref[r:r + rc, :] * sgn, axis=0, keepdims=True)
            zb = zb_ref[...]
            zr = _dot(c_ref[r:r + rc, :], zb)
            zi = _dot(s_ref[r:r + rc, :], zb)
            kr = kr_ref[o, r:r + rc, :]
            ki = ki_ref[o, r:r + rc, :]
            wgt = jnp.where(row == 0, 0.5 / n, 1.0 / n)
            p_ref[0, r:r + rc, :] = ((zr * kr - zi * ki) * wgt).astype(BF16)
            p_ref[1, r:r + rc, :] = ((zr * ki + zi * kr) * wgt).astype(BF16)
        nyq = znyq * kn_ref[o, 0:1, :] * (0.5 / n)
        stage(gate_ref)
        cw = gate_cw_ref[...]
        bias = bias_ref[o:o + 1, :]
        for r in range(0, n, rc):
            _, sgn = sign_rows(r)
            y = _dot(c_ref[r:r + rc, :], p_ref[0]) + _dot(s_ref[r:r + rc, :], p_ref[1]) + sgn * nyq
            z = conv_rows(cw, r) * (y + z_ref[r:r + rc, :] * bias)
            if o == 0:
                z_ref[r:r + rc, :] = z
                zb_ref[r:r + rc, :] = z.astype(BF16)
            else:
                o_ref[0, r:r + rc, :] = z


def _hyena_call(u, conv_w, kr, ki, kn, bias, cm, sm, n, row_block, prev_out):
    b, l, _ = u.shape
    ch = bias.shape[-1]
    tc = 2 * LANES
    nct = ch // tc
    aliased = prev_out is not None
    kern = functools.partial(_hyena_kernel, n=n, aliased=aliased)
    once = pl.Buffered(1)
    in_specs = [
        pl.BlockSpec((1, n, tc), lambda j, i: (i, row_block, j)),
        pl.BlockSpec((1, n, tc), lambda j, i: (i, row_block, nct + j)),
        pl.BlockSpec((1, n, tc), lambda j, i: (i, row_block, 2 * nct + j)),
        pl.BlockSpec((3, tc), lambda j, i: (0, j)),
        pl.BlockSpec((3, tc), lambda j, i: (0, nct + j)),
        pl.BlockSpec((3, tc), lambda j, i: (0, 2 * nct + j)),
        pl.BlockSpec((2, n, tc), lambda j, i: (0, 0, j), pipeline_mode=once),
        pl.BlockSpec((2, n, tc), lambda j, i: (0, 0, j), pipeline_mode=once),
        pl.BlockSpec((2, SUBLANES, tc), lambda j, i: (0, 0, j)),
        pl.BlockSpec((2, tc), lambda j, i: (0, j)),
        _resident((n, n)),
        _resident((n, n)),
    ]
    args = [u, u, u, conv_w, conv_w, conv_w, kr, ki, kn, bias, cm, sm]
    aliases = {}
    if aliased:
        in_specs = [pl.BlockSpec(memory_space=pl.ANY)] + in_specs
        args = [prev_out] + args
        aliases = {0: 0}
    return pl.pallas_call(
        kern,
        grid=(nct, b),
        in_specs=in_specs,
        out_specs=pl.BlockSpec((1, n, tc), lambda j, i: (i, row_block, j)),
        out_shape=jax.ShapeDtypeStruct((b, l, ch), F32),
        scratch_shapes=[
            pltpu.VMEM((n + 2 * SUBLANES, tc), F32),
            pltpu.VMEM((n, tc), F32),
            pltpu.VMEM((n, tc), BF16),
            pltpu.VMEM((2, n, tc), BF16),
        ],
        input_output_aliases=aliases,
        compiler_params=_cparams(("arbitrary", "arbitrary")),
        name="hyena_conv_n%d" % n,
    )(*args)


def _rope_tables(n, nc):
    rows = n // GRID_W
    row = jnp.repeat(jnp.arange(rows, dtype=F32), GRID_W)
    col = jnp.tile(jnp.arange(GRID_W, dtype=F32), rows)
    half = HEAD_DIM // 2
    inv = ROPE_BASE ** (-jnp.arange(0, half, 2, dtype=F32) / half)
    ar = row[:, None] * inv
    ac = col[:, None] * inv
    cos = jnp.concatenate([jnp.cos(ar), jnp.cos(ar), jnp.cos(ac), jnp.cos(ac)], axis=-1)
    sin = jnp.concatenate([-jnp.sin(ar), jnp.sin(ar), -jnp.sin(ac), jnp.sin(ac)], axis=-1)
    cos = jnp.concatenate([cos, jnp.ones((nc, HEAD_DIM), F32)], axis=0)
    sin = jnp.concatenate([sin, jnp.zeros((nc, HEAD_DIM), F32)], axis=0)
    return jnp.tile(cos, (1, LANES // HEAD_DIM)), jnp.tile(sin, (1, LANES // HEAD_DIM))


def _rope_partner_cols(width):
    d = np.arange(width)
    quarter = HEAD_DIM // 4
    return np.where((d % (2 * quarter)) < quarter, d + quarter, d - quarter)


def _hyena_feats(n):
    pos = jnp.arange(n, dtype=F32)
    t = pos / max(n - 1, 1)
    ang = (2.0 * math.pi * pos / n)[:, None] * jnp.linspace(1e-4, HY_BANDS - 1, HY_BANDS, dtype=F32)[None, :]
    feats = jnp.concatenate([t[:, None], jnp.cos(ang), -jnp.sin(ang)], axis=-1)
    feats = jnp.pad(feats, ((0, 0), (0, 64 - feats.shape[-1])))
    back = jnp.concatenate([feats[0:1], jnp.flip(feats[1:], axis=0)], axis=0)
    return feats, back


def _pad_cols(w, width):
    return jnp.pad(w, ((0, 0), (0, width - w.shape[-1])))


def _layer_ab(xz, mod, norm_g0, norm_g1, w_in, w_out, conv_w, a_log, dt_bias, gdn_g, lam_p, diff_g, lam_init,
              rope, n, nc, solve_passes):
    hd = GDN_HEADS * GDN_DIM
    wq, wk, wv, wg = (w_in[:, i * hd:(i + 1) * hd] for i in range(4))
    o = 4 * hd
    w_beta, w_alpha = w_in[:, o:o + 16], w_in[:, o + 16:o + 32]
    o += 32
    dd = DIFF_HEADS * 2 * DIFF_DIM
    wdq, wdk, wdv = (w_in[:, o + i * dd:o + (i + 1) * dd] for i in range(3))
    pairs = GDN_HEADS // 2
    pair_cols = lambda w: [w[:, p * LANES:(p + 1) * LANES] for p in range(pairs)]
    w_qkvg = jnp.concatenate([blk for grp in zip(pair_cols(wq), pair_cols(wk), pair_cols(wv), pair_cols(wg))
                              for blk in grp], axis=1)
    perm = _rope_partner_cols(dd)
    w_all = jnp.concatenate([w_qkvg, _pad_cols(jnp.concatenate([w_beta, w_alpha], axis=1), LANES),
                             wdq, wdk, wdv, wdq[:, perm], wdk[:, perm]], axis=1).astype(BF16)
    c0 = 4 * hd
    c1 = c0 + LANES
    segs = (_Seg(0, c0), _Seg(c0, LANES),
            _Seg(c1, dd, rot_start=c1 + 3 * dd, scale=DIFF_DIM ** -0.5, dtype=BF16),
            _Seg(c1 + dd, dd, rot_start=c1 + 4 * dd, dtype=BF16),
            _Seg(c1 + 2 * dd, dd, dtype=BF16, transposed=True))
    qkvg, ba, dq, dk, dvt = _proj_call(xz, mod, norm_g0, w_all, rope[0], rope[1], segs, n // ROW_TILE, "proj_ab")

    cq, ck, cv = (conv_w[:, i * hd:(i + 1) * hd] for i in range(3))
    zeros = jnp.zeros((3, LANES), F32)
    conv_l = jnp.concatenate([blk for p in range(pairs) for blk in
                              (cq[:, p * LANES:(p + 1) * LANES], ck[:, p * LANES:(p + 1) * LANES],
                               cv[:, p * LANES:(p + 1) * LANES], zeros)], axis=1)
    n_gate = 2 * GDN_HEADS
    on_decay_lanes = lambda t: jnp.pad(t.reshape(1, n_gate), ((0, 0), (n_gate, LANES - 2 * n_gate)))
    gate_params = jnp.concatenate([on_decay_lanes(a_log), on_decay_lanes(dt_bias)], axis=0)
    ng = jnp.tile(gdn_g.reshape(1, GDN_DIM), (1, 2))
    oa = _gdn_call(qkvg, ba, conv_l, gate_params, ng, n, nc, solve_passes)
    ob = _diff_call(dq, dk, dvt, lam_p, diff_g, n, lam_init)
    l = n + nc
    return _outproj_call(oa, ob, w_out.astype(BF16), xz, mod, norm_g1, l // ROW_TILE, n // ROW_TILE)


def _layer_cd(xz, mod, norm_g0, norm_g1, w_in, w_out, sink, hy_conv, hy_w1, hy_b1, hy_w2, hy_b2, hy_w3, hy_freq,
              hy_bias, rope, n, nc, last, dft_x, dft_c):
    qd = SWA_HEADS * HEAD_DIM
    kd = SWA_KV_HEADS * HEAD_DIM
    wq, wk, wv, wu = w_in[:, 0:qd], w_in[:, qd:qd + kd], w_in[:, qd + kd:qd + 2 * kd], w_in[:, qd + 2 * kd:]
    dup = lambda w: jnp.concatenate([w[:, 0:HEAD_DIM], w[:, 0:HEAD_DIM], w[:, HEAD_DIM:], w[:, HEAD_DIM:]], axis=1)
    wk2, wv2 = dup(wk), dup(wv)
    ud = wu.shape[1]
    w_all = jnp.concatenate([wq, wk2, wv2, wu, wq[:, _rope_partner_cols(qd)], wk2[:, _rope_partner_cols(2 * kd)]],
                            axis=1).astype(BF16)
    o_u = qd + 4 * kd
    segs = (_Seg(0, qd, rot_start=o_u + ud, scale=HEAD_DIM ** -0.5, dtype=BF16),
            _Seg(qd, 2 * kd, rot_start=o_u + ud + qd, dtype=BF16),
            _Seg(qd + 2 * kd, 2 * kd, dtype=BF16), _Seg(o_u, ud))
    q, k, v, u = _proj_call(xz, mod, norm_g0, w_all, rope[0], rope[1], segs, n // ROW_TILE, "proj_cd")
    oc = _swa_call(q, k, v, _pad_cols(sink.reshape(1, SWA_HEADS), LANES), n, nc, not last)

    ch = hy_bias.shape[-1]
    deltas = jnp.abs(jnp.linspace(HY_MIN_DECAY, HY_MAX_DECAY, ch, dtype=F32)).reshape(1, ch)
    hid = hy_w2.shape[0]
    w1p = jnp.pad(hy_w1, ((0, hid - hy_w1.shape[0]), (0, 0)))
    filt = lambda m, dft: _hyena_filter_call(*_hyena_feats(m), w1p, hy_b1.reshape(1, hid), hy_w2,
                                             hy_b2.reshape(1, hid), hy_freq, hy_w3, deltas, *dft)
    kr, ki, kn = filt(n, dft_x)
    od = _hyena_call(u, hy_conv, kr, ki, kn, hy_bias, *dft_x, n, 0, None)
    if not last:
        kr, ki, kn = filt(nc, dft_c)
        od = _hyena_call(u, hy_conv, kr, ki, kn, hy_bias, *dft_c, nc, n // nc, od)
    n_tiles = (n if last else n + nc) // ROW_TILE
    return _outproj_call(oc, od, w_out.astype(BF16), xz, mod, norm_g1, n_tiles, n // ROW_TILE)


def kernel(x, c, ctx, c_ctx, w_mod, b_mod, norm_g, ffn_w_up, ffn_conv, ffn_w_down, ab_w_in, ab_w_out, gdn_conv, gdn_a_log, gdn_dt_bias, gdn_norm_g, diff_lambda, diff_norm_g, cd_w_in, cd_w_out, swa_sink, hy_conv, hy_w1, hy_b1, hy_w2, hy_b2, hy_w3, hy_freq, hy_bias):
    b, n, d = x.shape
    nc = ctx.shape[1]
    depth = w_mod.shape[0]
    assert n % ROW_TILE == 0 and nc == ROW_TILE and n % GRID_W == 0
    xz = jnp.concatenate([x, ctx], axis=1)
    rows = -(-(b + 1) // SUBLANES) * SUBLANES
    cc = jnp.concatenate([c, c_ctx[None], jnp.zeros((rows - b - 1, d), F32)], axis=0)
    mods = _mod_call(cc, w_mod, b_mod)
    rope = _rope_tables(n, nc)
    dft_x = _dft_mats(n)
    dft_c = _dft_mats(nc)
    n_x_tiles = n // ROW_TILE
    for l in range(depth):
        last = l == depth - 1
        i = l // 2
        mx = mods[l, :b].reshape(b, 1, 6, d)
        mz = jnp.broadcast_to(mods[l, b].reshape(1, 1, 6, d), (b, 1, 6, d))
        mod = jnp.concatenate([mx, mz], axis=1)
        if l % 2 == 0:
            lam_init = 0.8 - 0.6 * math.exp(-0.3 * l)
            xz = _layer_ab(xz, mod, norm_g[l, 0], norm_g[l, 1], ab_w_in[i], ab_w_out[i], gdn_conv[i], gdn_a_log[i],
                           gdn_dt_bias[i], gdn_norm_g[i], diff_lambda[i], diff_norm_g[i], lam_init, rope, n, nc, GDN_SOLVE_PASSES)
        else:
            xz = _layer_cd(xz, mod, norm_g[l, 0], norm_g[l, 1], cd_w_in[i], cd_w_out[i], swa_sink[i], hy_conv[i],
                           hy_w1[i], hy_b1[i], hy_w2[i], hy_b2[i], hy_w3[i], hy_freq[i], hy_bias[i], rope, n, nc,
                           last, dft_x, dft_c)
        n_tiles = xz.shape[1] // ROW_TILE
        xz = _ffn_call(xz, mod, norm_g[l, 2], norm_g[l, 3], ffn_w_up[l].astype(BF16), ffn_conv[l],
                       ffn_w_down[l].astype(BF16), n_tiles, n_x_tiles)
    return xz[:, :n] if xz.shape[1] != n else xz
```

```python
import functools
import math
from typing import NamedTuple, Optional

import jax
import jax.numpy as jnp
import numpy as np
from jax import lax
from jax.experimental import pallas as pl
from jax.experimental.pallas import tpu as pltpu

F32 = jnp.float32
BF16 = jnp.bfloat16

EPS = 1e-6
NEG_INF = -1e30
GRID_W = 64
HEAD_DIM = 64
ROPE_BASE = 10000.0
GDN_HEADS = 8
GDN_DIM = 64
GDN_CHUNK = 64
DIFF_HEADS = 4
DIFF_DIM = 64
SWA_HEADS = 8
SWA_KV_HEADS = 2
SWA_WINDOW = 128
SWA_BLOCK = 128
HY_BANDS = 16
HY_MIN_DECAY = math.log(1e-2) / 1.5
HY_MAX_DECAY = math.log(1e-2) / 0.3
HY_ROW_CHUNK = 512

GDN_SOLVE_PASSES = 3

LANES = 128
SUBLANES = 8
MXU_WIDTH = 256
FFN_COL_CHUNK = 3 * MXU_WIDTH
ROW_TILE = 256
VMEM_LIMIT = 56 * 1024 * 1024


def _cparams(sem):
    return pltpu.CompilerParams(dimension_semantics=sem, vmem_limit_bytes=VMEM_LIMIT)


def _resident(shape):
    zeros = (0,) * len(shape)
    return pl.BlockSpec(shape, lambda *_: zeros, pipeline_mode=pl.Buffered(1))


def _log2(v):
    assert v & (v - 1) == 0
    return v.bit_length() - 1


def _sigmoid(x):
    return 1.0 / (1.0 + jnp.exp(-x))


def _silu(x):
    return x * _sigmoid(x)


def _softplus(x):
    return jnp.maximum(x, 0.0) + jnp.log1p(jnp.exp(-jnp.abs(x)))


def _dot(a, b):
    return jnp.dot(a, b, preferred_element_type=F32)


def _dot_nt(a, b):
    return lax.dot_general(a, b, (((1,), (1,)), ((), ())), preferred_element_type=F32)


def _dot_tn(a, b):
    return lax.dot_general(a, b, (((0,), (0,)), ((), ())), preferred_element_type=F32)


def _dot_f32(a, b):
    return jnp.dot(a, b, preferred_element_type=F32, precision=lax.Precision.HIGHEST)


def _split2(x):
    hi = x.astype(BF16)
    lo = (x - hi.astype(F32)).astype(BF16)
    return hi, lo


def _split3(x):
    hi = x.astype(BF16)
    r = x - hi.astype(F32)
    mid = r.astype(BF16)
    lo = (r - mid.astype(F32)).astype(BF16)
    return hi, mid, lo


def _dot_sel(x, sel_bf16):
    hi, mid, lo = _split3(x)
    return _dot(hi, sel_bf16) + _dot(mid, sel_bf16) + _dot(lo, sel_bf16)


def _mm(a, b, passes):
    if passes == 1:
        return _dot(a.astype(BF16), b.astype(BF16))
    ah, al = _split2(a)
    bh, bl = _split2(b)
    return _dot(ah, bh) + _dot(al, bh) + _dot(ah, bl)


def _rms(y, g):
    return y * lax.rsqrt(jnp.mean(y * y, axis=-1, keepdims=True) + EPS) * g


def _modnorm(x, g, shift, scale):
    return _rms(x, g) * (1.0 + scale) + shift


def _mod_kernel(cc_ref, w_ref, b_ref, o_ref):
    s = _silu(cc_ref[...])
    o_ref[0] = _dot(s.astype(BF16), w_ref[0].astype(BF16)) + b_ref[0]


def _mod_call(cc, w_mod, b_mod):
    depth, d, nm = w_mod.shape
    rows = cc.shape[0]
    ct = 1536
    return pl.pallas_call(
        _mod_kernel,
        grid=(depth, nm // ct),
        in_specs=[
            pl.BlockSpec((rows, d), lambda l, j: (0, 0)),
            pl.BlockSpec((1, d, ct), lambda l, j: (l, 0, j)),
            pl.BlockSpec((1, 1, ct), lambda l, j: (l, 0, j)),
        ],
        out_specs=pl.BlockSpec((1, rows, ct), lambda l, j: (l, 0, j)),
        out_shape=jax.ShapeDtypeStruct((depth, rows, nm), F32),
        compiler_params=_cparams(("arbitrary", "arbitrary")),
        name="adaln_mod",
    )(cc, w_mod, b_mod.reshape(depth, 1, nm))


class _Seg(NamedTuple):
    start: int
    width: int
    rot_start: Optional[int] = None
    scale: float = 1.0
    dtype: type = F32
    transposed: bool = False


def _proj_kernel(x_ref, mod_ref, g_ref, w_ref, cos_ref, sin_ref, *out_refs, segs):
    m = mod_ref[0, 0]
    h = _modnorm(x_ref[0], g_ref[...], m[0:1], m[1:2]).astype(BF16)
    for o_ref, seg in zip(out_refs, segs):
        y = _dot(h, w_ref[:, seg.start:seg.start + seg.width])
        if seg.rot_start is not None:
            yr = _dot(h, w_ref[:, seg.rot_start:seg.rot_start + seg.width])
            reps = seg.width // LANES
            cos = jnp.concatenate([cos_ref[...]] * reps, axis=1)
            sin = jnp.concatenate([sin_ref[...]] * reps, axis=1)
            y = y * cos + yr * sin
        if seg.scale != 1.0:
            y = y * seg.scale
        if seg.transposed:
            y = y.T
        o_ref[0] = y.astype(seg.dtype)


def _proj_call(xz, mod, g, w, cos_t, sin_t, segs, n_x_tiles, name):
    b, l, d = xz.shape
    tm = ROW_TILE
    nt = l // tm
    p = w.shape[1]
    return pl.pallas_call(
        functools.partial(_proj_kernel, segs=segs),
        grid=(nt, b),
        in_specs=[
            pl.BlockSpec((1, tm, d), lambda t, i: (i, t, 0)),
            pl.BlockSpec((1, 1, 6, d), lambda t, i: (i, t // n_x_tiles, 0, 0)),
            pl.BlockSpec((1, d), lambda t, i: (0, 0)),
            _resident((d, p)),
            pl.BlockSpec((tm, LANES), lambda t, i: (t, 0)),
            pl.BlockSpec((tm, LANES), lambda t, i: (t, 0)),
        ],
        out_specs=[pl.BlockSpec((1, s.width, tm), lambda t, i: (i, 0, t)) if s.transposed
                   else pl.BlockSpec((1, tm, s.width), lambda t, i: (i, t, 0)) for s in segs],
        out_shape=[jax.ShapeDtypeStruct((b, s.width, l) if s.transposed else (b, l, s.width), s.dtype)
                   for s in segs],
        compiler_params=_cparams(("arbitrary", "arbitrary")),
        name=name,
    )(xz, mod, g.reshape(1, d), w, cos_t, sin_t)


def _outproj_kernel(o1_ref, o2_ref, w_ref, x_ref, mod_ref, g_ref, out_ref):
    k1 = o1_ref.shape[-1]
    y = _dot(o1_ref[0].astype(BF16), w_ref[0:k1, :]) + _dot(o2_ref[0].astype(BF16), w_ref[k1:, :])
    m = mod_ref[0, 0]
    out_ref[0] = x_ref[0] + m[2:3] * _rms(y, g_ref[...])


def _outproj_call(o1, o2, w, xz, mod, g, n_tiles, n_x_tiles):
    b, _, d = xz.shape
    tm = ROW_TILE
    k1, k2 = o1.shape[-1], o2.shape[-1]
    return pl.pallas_call(
        _outproj_kernel,
        grid=(n_tiles, b),
        in_specs=[
            pl.BlockSpec((1, tm, k1), lambda t, i: (i, t, 0)),
            pl.BlockSpec((1, tm, k2), lambda t, i: (i, t, 0)),
            _resident((k1 + k2, d)),
            pl.BlockSpec((1, tm, d), lambda t, i: (i, t, 0)),
            pl.BlockSpec((1, 1, 6, d), lambda t, i: (i, t // n_x_tiles, 0, 0)),
            pl.BlockSpec((1, d), lambda t, i: (0, 0)),
        ],
        out_specs=pl.BlockSpec((1, tm, d), lambda t, i: (i, t, 0)),
        out_shape=jax.ShapeDtypeStruct((b, n_tiles * tm, d), F32),
        compiler_params=_cparams(("arbitrary", "arbitrary")),
        name="mixer_out",
    )(o1, o2, w, xz, mod, g.reshape(1, d))


def _ffn_kernel(xp_ref, x_ref, xn_ref, mod_ref, g2_ref, g3_ref, wup_ref, cw_ref, wdn_ref, out_ref,
                up_ref, *, tm, n_x_tiles, n_tiles, cf, dff):
    t = pl.program_id(0)
    first = jnp.logical_or(t == 0, t == n_x_tiles)
    last = jnp.logical_or(t == n_x_tiles - 1, t == n_tiles - 1)
    keep_top = jnp.where(first, 0.0, 1.0)
    keep_bot = jnp.where(last, 0.0, 1.0)
    m = mod_ref[0, 0]
    halo = SUBLANES
    xe = jnp.concatenate([xp_ref[0], x_ref[0], xn_ref[0]], axis=0)
    h = _modnorm(xe, g2_ref[...], m[3:4], m[4:5]).astype(BF16)
    chunks = [(c0, min(cf, dff - c0)) for c0 in range(0, dff, cf)]
    for j, (c0, wd) in enumerate(chunks):
        for half, base in enumerate((c0, dff + c0)):
            u = _dot(h, wup_ref[:, base:base + wd])
            up_ref[j, half, :, 0:wd] = u
            up_ref[j, half, 0:halo, 0:wd] = u[0:halo] * keep_top
            up_ref[j, half, tm + halo:tm + 2 * halo, 0:wd] = u[tm + halo:] * keep_bot
    acc = jnp.zeros((tm, x_ref.shape[-1]), F32)
    for j, (c0, wd) in enumerate(chunks):
        halves = []
        for half, base in enumerate((c0, dff + c0)):
            cw = cw_ref[:, base:base + wd]
            halves.append(cw[0:1] * up_ref[j, half, halo - 1:halo - 1 + tm, 0:wd]
                          + cw[1:2] * up_ref[j, half, halo:halo + tm, 0:wd]
                          + cw[2:3] * up_ref[j, half, halo + 1:halo + 1 + tm, 0:wd])
        act = (_silu(halves[1]) * halves[0]).astype(BF16)
        acc = acc + _dot(act, wdn_ref[c0:c0 + wd, :])
    out_ref[0] = x_ref[0] + m[5:6] * _rms(acc, g3_ref[...])


def _ffn_call(xz, mod, g2, g3, w_up, conv_w, w_down, n_tiles, n_x_tiles):
    b, l, d = xz.shape
    tm = ROW_TILE
    dff = w_down.shape[0]
    cf = FFN_COL_CHUNK
    hb = tm // SUBLANES
    nb8 = l // SUBLANES
    kern = functools.partial(_ffn_kernel, tm=tm, n_x_tiles=n_x_tiles, n_tiles=n_tiles, cf=cf, dff=dff)
    return pl.pallas_call(
        kern,
        grid=(n_tiles, b),
        in_specs=[
            pl.BlockSpec((1, SUBLANES, d), lambda t, i: (i, jnp.maximum(t * hb - 1, 0), 0)),
            pl.BlockSpec((1, tm, d), lambda t, i: (i, t, 0)),
            pl.BlockSpec((1, SUBLANES, d), lambda t, i: (i, jnp.minimum((t + 1) * hb, nb8 - 1), 0)),
            pl.BlockSpec((1, 1, 6, d), lambda t, i: (i, t // n_x_tiles, 0, 0)),
            pl.BlockSpec((1, d), lambda t, i: (0, 0)),
            pl.BlockSpec((1, d), lambda t, i: (0, 0)),
            _resident((d, 2 * dff)),
            pl.BlockSpec((3, 2 * dff), lambda t, i: (0, 0)),
            _resident((dff, d)),
        ],
        out_specs=pl.BlockSpec((1, tm, d), lambda t, i: (i, t, 0)),
        out_shape=jax.ShapeDtypeStruct((b, n_tiles * tm, d), F32),
        scratch_shapes=[pltpu.VMEM((-(-dff // cf), 2, tm + 2 * SUBLANES, cf), F32)],
        compiler_params=_cparams(("arbitrary", "arbitrary")),
        name="conv_ffn",
    )(xz, xz, xz, mod, g2.reshape(1, d), g3.reshape(1, d), w_up, conv_w, w_down)


def _half_sums(x2, lane_lo):
    s0 = jnp.sum(jnp.where(lane_lo, x2, 0.0), axis=-1, keepdims=True)
    s1 = jnp.sum(jnp.where(lane_lo, 0.0, x2), axis=-1, keepdims=True)
    return jnp.where(lane_lo, s0, s1)


def _gdn_kernel(qkvg_ref, ba_ref, cw_ref, gp_ref, ng_ref, out_ref,
                pad_ref, q_ref, k_ref, v_ref, bb_ref, gb_ref, qe_ref, mp_ref, ou_ref, nn_ref, egl_ref, o_ref,
                *, n, nc, solve_passes, chunks_per_iter):
    l = n + nc
    c = GDN_CHUNK
    n_chunks = l // c
    pair = pl.program_id(1)
    halo = SUBLANES
    lane = lax.broadcasted_iota(jnp.int32, (1, LANES), 1)
    lane_lo = lane < GDN_DIM

    cw = cw_ref[:, 0:3 * LANES]
    zero_rows = jnp.zeros((halo, 3 * LANES), F32)
    for seq_start, seq_len in ((0, n), (n, nc)):
        base = halo + seq_start + (2 * halo if seq_start else 0)
        pad_ref[base - halo:base, :] = zero_rows
        pad_ref[base + seq_len:base + seq_len + halo, :] = zero_rows
        step = 256
        for r in range(0, seq_len, step):
            pad_ref[base + r:base + r + step, :] = qkvg_ref[0, seq_start + r:seq_start + r + step, 0:3 * LANES]
        for r in range(0, seq_len, step):
            y = (cw[0:1] * pad_ref[base + r - 1:base + r - 1 + step, :]
                 + cw[1:2] * pad_ref[base + r:base + r + step, :]
                 + cw[2:3] * pad_ref[base + r + 1:base + r + 1 + step, :])
            y = _silu(y)
            q = y[:, 0:LANES]
            k = y[:, LANES:2 * LANES]
            rows = slice(seq_start + r, seq_start + r + step)
            q_ref[rows, :] = q * lax.rsqrt(_half_sums(q * q, lane_lo) + EPS) * (GDN_DIM ** -0.5)
            k_ref[rows, :] = k * lax.rsqrt(_half_sums(k * k, lane_lo) + EPS)
            v_ref[rows, :] = y[:, 2 * LANES:3 * LANES]

    sel_r = lax.broadcasted_iota(jnp.int32, (LANES, 4 * LANES), 0)
    sel_c = lax.broadcasted_iota(jnp.int32, (LANES, 4 * LANES), 1)
    quarter = sel_c >> _log2(LANES)
    src_lane = (quarter & 1) * 2 * GDN_HEADS + (quarter >> 1) * GDN_HEADS + 2 * pair + ((sel_c >> _log2(GDN_DIM)) & 1)
    sel = (sel_r == src_lane).astype(BF16)
    gblk = 256
    bi = lax.broadcasted_iota(jnp.int32, (gblk, gblk), 0)
    bj = lax.broadcasted_iota(jnp.int32, (gblk, gblk), 1)
    same_chunk = (bi >> _log2(c)) == (bj >> _log2(c))
    csum = (jnp.logical_and(same_chunk, bi >= bj).astype(BF16), jnp.logical_and(same_chunk, bi <= bj).astype(BF16))
    neg_a = -jnp.exp(gp_ref[0:1, :])
    dt_bias = gp_ref[1:2, :]
    for r in range(0, l, gblk):
        ba = ba_ref[0, r:r + gblk, :]
        gates = jnp.where(lane < 2 * GDN_HEADS, _sigmoid(ba), neg_a * _softplus(ba + dt_bias))
        x = _dot_sel(gates, sel)
        for d in range(2):
            bb_ref[d, r:r + gblk, :] = x[:, 2 * d * LANES:(2 * d + 1) * LANES]
            gb_ref[d, r:r + gblk, :] = _dot_sel_lhs(csum[d], x[:, (2 * d + 1) * LANES:(2 * d + 2) * LANES])

    r2 = lax.broadcasted_iota(jnp.int32, (2 * c, 2 * c), 0)
    c2 = lax.broadcasted_iota(jnp.int32, (2 * c, 2 * c), 1)
    same_head = (r2 >= c) == (c2 >= c)
    eye = (r2 == c2).astype(F32)
    masks = ((jnp.logical_and(same_head, r2 >= c2), jnp.logical_and(same_head, r2 > c2)),
             (jnp.logical_and(same_head, r2 <= c2), jnp.logical_and(same_head, r2 < c2)))
    m0 = lane_lo.astype(F32)
    m1 = 1.0 - m0

    def stack_heads(x2):
        return jnp.concatenate([x2 * m0, x2 * m1], axis=0)

    def fold_heads(x):
        return x[0:c] + x[c:2 * c]

    def chunks_local(dirs, qs, ks, vs, betas, gcs):
        each = lambda f, *cols: [f(*args) for args in zip(*cols)]
        incl = [masks[d][0] for d in dirs]
        strict = [masks[d][1] for d in dirs]
        g1 = each(lambda gc2: jnp.concatenate([gc2, gc2], axis=0), gcs)
        decay = each(lambda g, m: jnp.where(m, jnp.exp(jnp.where(m, g - g.T, 0.0)), 0.0), g1, incl)
        kb = each(lambda k, b: k * b, ks, betas)
        kst = each(lambda k: stack_heads(k).astype(BF16), ks)
        a_raw = each(lambda x, y: _dot_nt(stack_heads(x).astype(BF16), y), kb, kst)
        qk_raw = each(lambda x, y: _dot_nt(stack_heads(x).astype(BF16), y), qs, kst)
        qk = each(lambda m, x, dc: jnp.where(m, x * dc, 0.0).astype(BF16), incl, qk_raw, decay)
        p = each(lambda m, x, dc: jnp.where(m, -(x * dc), 0.0), strict, a_raw, decay)
        tinv = each(lambda x: eye + x, p)
        p = each(lambda x: _mm(x, x, solve_passes), p)
        for _ in range(_log2(c) - 2):
            r = each(lambda x, t: _mm(x, jnp.concatenate([x, t], axis=1), solve_passes), p, tinv)
            p = each(lambda y: y[:, 0:LANES], r)
            tinv = each(lambda t, y: t + y[:, LANES:2 * LANES], tinv, r)
        tinv = each(lambda t, x: t + _mm(x, t, solve_passes), tinv, p)
        egc = each(jnp.exp, gcs)
        rhs = each(lambda v, b, x, e: jnp.concatenate([stack_heads(v * b), stack_heads(x * e)], axis=1),
                   vs, betas, kb, egc)
        sol = each(lambda t, r: _mm(t, r, solve_passes), tinv, rhs)
        u2 = each(lambda x: fold_heads(x[:, 0:LANES]), sol)
        w2 = each(lambda x: fold_heads(x[:, LANES:2 * LANES]), sol)
        gl = each(lambda d, gc2: gc2[c - 1:c, :] if d == 0 else gc2[0:1, :], dirs, gcs)
        ktail = each(lambda k, g, gc2: (k * jnp.exp(g - gc2)).astype(BF16), ks, gl, gcs)
        qwu = each(lambda x, w, u: _dot(x, jnp.concatenate([stack_heads(w), stack_heads(u)], axis=1).astype(BF16)),
                   qk, w2, u2)
        kwu = each(lambda x, w, u: _dot_tn(x, jnp.concatenate([w, u], axis=1).astype(BF16)), ktail, w2, u2)
        q_eff = each(lambda q, e, x: (q * e - fold_heads(x[:, 0:LANES])).astype(BF16), qs, egc, qwu)
        m_neg = each(lambda x: jnp.where(same_head, -x[:, 0:LANES], 0.0).astype(BF16), kwu)
        o_loc = each(lambda x: fold_heads(x[:, LANES:2 * LANES]), qwu)
        s_loc = each(lambda x: jnp.where(same_head, x[:, LANES:2 * LANES], 0.0), kwu)
        egl = each(lambda g: jnp.broadcast_to(jnp.exp(g), (SUBLANES, LANES)), gl)
        return q_eff, m_neg, o_loc, s_loc, egl

    def chunk_rows(chunk, rows_per_chunk):
        return pl.ds(pl.multiple_of(chunk * rows_per_chunk, rows_per_chunk), rows_per_chunk)

    def local_body(it, carry):
        dirs, chunks, qs, ks, vs, betas, gcs = [], [], [], [], [], [], []
        for g in range(chunks_per_iter):
            chunk = it + g * (n_chunks // chunks_per_iter)
            rows = chunk_rows(chunk, c)
            for d in range(2):
                dirs.append(d)
                chunks.append(chunk)
                qs.append(q_ref[rows, :])
                ks.append(k_ref[rows, :])
                vs.append(v_ref[rows, :])
                betas.append(bb_ref[d, rows, :])
                gcs.append(gb_ref[d, rows, :])
        results = chunks_local(dirs, qs, ks, vs, betas, gcs)
        for d, chunk, q_eff, m_neg, o_loc, s_loc, egl in zip(dirs, chunks, *results):
            qe_ref[d, chunk_rows(chunk, c), :] = q_eff
            mp_ref[d, chunk_rows(chunk, 2 * c), :] = m_neg
            ou_ref[d, chunk_rows(chunk, c), :] = o_loc
            nn_ref[d, chunk_rows(chunk, 2 * c), :] = s_loc
            egl_ref[d, chunk_rows(chunk, SUBLANES), :] = egl
        return carry

    lax.fori_loop(0, n_chunks // chunks_per_iter, local_body, 0)

    ctx_chunks = nc // c

    def scan_body(s, carry):
        chunks = (jnp.where(s < ctx_chunks, s + n // c, s - ctx_chunks), n_chunks - 1 - s)
        loaded = [(qe_ref[d, chunk_rows(ch, c), :], mp_ref[d, chunk_rows(ch, 2 * c), :],
                   ou_ref[d, chunk_rows(ch, c), :], nn_ref[d, chunk_rows(ch, 2 * c), :],
                   egl_ref[d, chunk_rows(ch, SUBLANES), :]) for d, ch in enumerate(chunks)]
        res = [_dot(jnp.concatenate([ld[0], ld[1]], axis=0), s2.astype(BF16)) for ld, s2 in zip(loaded, carry)]
        for d, ch in enumerate(chunks):
            o_ref[d, chunk_rows(ch, c), :] = res[d][0:c] + loaded[d][2]
        return tuple(s2 * ld[4][0:1] + r[c:3 * c] + ld[3] for s2, ld, r in zip(carry, loaded, res))

    zero_state = jnp.zeros((2 * c, 2 * c), F32)
    lax.fori_loop(0, n_chunks, scan_body, (zero_state, zero_state))

    ng = ng_ref[...]
    step = 256
    for r in range(0, l, step):
        o = o_ref[0, r:r + step, :] + o_ref[1, r:r + step, :]
        ms = _half_sums(o * o, lane_lo) * (1.0 / GDN_DIM)
        gate = qkvg_ref[0, r:r + step, 3 * LANES:4 * LANES]
        out_ref[0, r:r + step, :] = o * lax.rsqrt(ms + EPS) * ng * _silu(gate)


def _dot_sel_lhs(sel_bf16, x):
    hi, mid, lo = _split3(x)
    return _dot(sel_bf16, hi) + _dot(sel_bf16, mid) + _dot(sel_bf16, lo)


def _gdn_call(qkvg, ba, conv_w, gate_params, ng, n, nc, solve_passes):
    b, l, _ = qkvg.shape
    pairs = GDN_HEADS // 2
    n_chunks = l // GDN_CHUNK
    kern = functools.partial(_gdn_kernel, n=n, nc=nc, solve_passes=solve_passes, chunks_per_iter=4)
    return pl.pallas_call(
        kern,
        grid=(b, pairs),
        in_specs=[
            pl.BlockSpec((1, l, 4 * LANES), lambda i, p: (i, 0, p)),
            pl.BlockSpec((1, l, LANES), lambda i, p: (i, 0, 0)),
            pl.BlockSpec((3, 4 * LANES), lambda i, p: (0, p)),
            pl.BlockSpec((2, LANES), lambda i, p: (0, 0)),
            pl.BlockSpec((1, LANES), lambda i, p: (0, 0)),
        ],
        out_specs=pl.BlockSpec((1, l, LANES), lambda i, p: (i, 0, p)),
        out_shape=jax.ShapeDtypeStruct((b, l, pairs * LANES), F32),
        scratch_shapes=[
            pltpu.VMEM((l + 5 * SUBLANES, 3 * LANES), F32),
            pltpu.VMEM((l, LANES), F32),
            pltpu.VMEM((l, LANES), F32),
            pltpu.VMEM((l, LANES), F32),
            pltpu.VMEM((2, l, LANES), F32),
            pltpu.VMEM((2, l, LANES), F32),
            pltpu.VMEM((2, l, LANES), BF16),
            pltpu.VMEM((2, 2 * l, LANES), BF16),
            pltpu.VMEM((2, l, LANES), F32),
            pltpu.VMEM((2, 2 * l, LANES), F32),
            pltpu.VMEM((2, n_chunks * SUBLANES, LANES), F32),
            pltpu.VMEM((2, l, LANES), F32),
        ],
        compiler_params=_cparams(("arbitrary", "arbitrary")),
        name="gated_deltanet",
    )(qkvg, ba, conv_w, gate_params, ng)


def _diff_kernel(*refs, key_start, n_sub, lam_init, aliased):
    if aliased:
        refs = refs[1:]
    q_ref, k_ref, vt_ref, lam_ref, ng_ref, o_ref = refs
    lp = lam_ref[...]
    lam = (jnp.exp(jnp.sum(lp[0:1] * lp[1:2], axis=-1, keepdims=True))
           - jnp.exp(jnp.sum(lp[2:3] * lp[3:4], axis=-1, keepdims=True)) + lam_init)
    lane = lax.broadcasted_iota(jnp.int32, (1, LANES), 1)
    halves = (lane < DIFF_DIM, lane >= DIFF_DIM)
    ng = ng_ref[...]
    k = k_ref[0, key_start:, :]
    vt = vt_ref[0, :, key_start:]
    tq = q_ref.shape[1] // n_sub
    scores = []
    for i in range(n_sub):
        q = q_ref[0, i * tq:(i + 1) * tq, :]
        scores.append([_dot_nt(k, jnp.where(m, q, jnp.zeros_like(q))) for m in halves])
    for i, s in enumerate(scores):
        e = [jnp.exp(x - jnp.max(x, axis=0, keepdims=True)) for x in s]
        pv = [_dot(vt, x.astype(BF16)) for x in e]
        parts = [x * (1.0 / jnp.sum(y, axis=0, keepdims=True)) for x, y in zip(pv, e)]
        ot = parts[0] - lam * parts[1]
        ot = ot * lax.rsqrt(jnp.mean(ot * ot, axis=0, keepdims=True) + EPS)
        o_ref[0, i * tq:(i + 1) * tq, :] = ot.T * ng * (1.0 - lam_init)


def _diff_call(dq, dk, dvt, lam_p, ng, lam_init, q_rows, first_block, n_q_blocks, key_start, n_sub, prev_out):
    b, l, _ = dq.shape
    aliased = prev_out is not None
    kern = functools.partial(_diff_kernel, key_start=key_start, n_sub=n_sub, lam_init=lam_init, aliased=aliased)
    row_of = lambda t: first_block + t
    in_specs = [
        pl.BlockSpec((1, q_rows, LANES), lambda i, h, t: (i, row_of(t), h)),
        pl.BlockSpec((1, l, LANES), lambda i, h, t: (i, 0, h)),
        pl.BlockSpec((1, LANES, l), lambda i, h, t: (i, h, 0)),
        pl.BlockSpec((4, DIFF_DIM), lambda i, h, t: (0, 0)),
        pl.BlockSpec((1, LANES), lambda i, h, t: (0, 0)),
    ]
    args = [dq, dk, dvt, lam_p, ng.reshape(1, LANES)]
    aliases = {}
    if aliased:
        in_specs = [pl.BlockSpec(memory_space=pl.ANY)] + in_specs
        args = [prev_out] + args
        aliases = {0: 0}
    return pl.pallas_call(
        kern,
        grid=(b, DIFF_HEADS, n_q_blocks),
        in_specs=in_specs,
        out_specs=pl.BlockSpec((1, q_rows, LANES), lambda i, h, t: (i, row_of(t), h)),
        out_shape=jax.ShapeDtypeStruct((b, l, DIFF_HEADS * LANES), F32),
        input_output_aliases=aliases,
        compiler_params=_cparams(("arbitrary", "arbitrary", "arbitrary")),
        name="diff_attention_ctx" if aliased else "diff_attention",
    )(*args)


def _swa_kernel(q_ref, k_ref, v_ref, sink_ref, o_ref, *, n, nc):
    t = pl.program_id(1)
    blk = SWA_BLOCK
    n_x = n // blk
    q = q_ref[0]
    lane = lax.broadcasted_iota(jnp.int32, (1, LANES), 1)
    lane_lo = lane < HEAD_DIM
    sink = sink_ref[...]
    group = SWA_HEADS // SWA_KV_HEADS

    def run(keys, vals, valid):
        head_of_row = lax.broadcasted_iota(jnp.int32, (group * blk, 1), 0) >> _log2(blk)
        kvs = range(SWA_KV_HEADS)
        kk = [keys[:, kvh * LANES:(kvh + 1) * LANES] for kvh in kvs]
        vv = [vals[:, kvh * LANES:(kvh + 1) * LANES] for kvh in kvs]
        qst, sk = [], []
        for kvh in kvs:
            q_rows = []
            sk_rows = jnp.zeros((group * blk, 1), F32)
            for g in range(group):
                h = kvh * group + g
                qp = q[:, (h // 2) * LANES:(h // 2 + 1) * LANES]
                q_rows.append(jnp.where(lane_lo if h % 2 == 0 else jnp.logical_not(lane_lo), qp, jnp.zeros_like(qp)))
                sk_rows = jnp.where(head_of_row == g, sink[:, h:h + 1], sk_rows)
            qst.append(jnp.concatenate(q_rows, axis=0))
            sk.append(sk_rows)
        s = [_dot_nt(x, y) for x, y in zip(qst, kk)]
        if valid is not None:
            s = [jnp.where(valid, x, NEG_INF) for x in s]
        mx = [jnp.maximum(jnp.max(x, axis=-1, keepdims=True), y) for x, y in zip(s, sk)]
        e = [jnp.exp(x - m) for x, m in zip(s, mx)]
        pv = [_dot(x.astype(BF16), y) for x, y in zip(e, vv)]
        den = [jnp.sum(x, axis=-1, keepdims=True) + jnp.exp(y - m) for x, y, m in zip(e, sk, mx)]
        outs = []
        for o, dn in zip(pv, den):
            o = o * (1.0 / dn)
            for g in range(0, group, 2):
                outs.append(jnp.where(lane_lo, o[g * blk:(g + 1) * blk], o[(g + 1) * blk:(g + 2) * blk]))
        o_ref[0] = jnp.concatenate(outs, axis=1)

    @pl.when(t < n_x)
    def _():
        start = pl.multiple_of(jnp.clip((t - 1) * blk, 0, n - 3 * blk), blk)
        keys = jnp.concatenate([k_ref[0, pl.ds(start, 3 * blk), :], k_ref[0, n:n + nc, :]], axis=0)
        vals = jnp.concatenate([v_ref[0, pl.ds(start, 3 * blk), :], v_ref[0, n:n + nc, :]], axis=0)
        shape = (group * blk, 3 * blk + nc)
        qpos = t * blk + (lax.broadcasted_iota(jnp.int32, shape, 0) & (blk - 1))
        col = lax.broadcasted_iota(jnp.int32, shape, 1)
        dist = qpos - (start + col)
        in_window = jnp.logical_and(dist <= SWA_WINDOW, dist >= -SWA_WINDOW)
        valid = jnp.logical_or(col >= 3 * blk, in_window)
        run(keys, vals, valid)

    @pl.when(t >= n_x)
    def _():
        run(k_ref[0, n:n + nc, :], v_ref[0, n:n + nc, :], None)


def _swa_call(q, k, v, sink, n, nc, with_ctx):
    b, l, _ = q.shape
    blk = SWA_BLOCK
    nt = (l if with_ctx else n) // blk
    kern = functools.partial(_swa_kernel, n=n, nc=nc)
    return pl.pallas_call(
        kern,
        grid=(b, nt),
        in_specs=[
            pl.BlockSpec((1, blk, SWA_HEADS * HEAD_DIM), lambda i, t: (i, t, 0)),
            pl.BlockSpec((1, l, 2 * LANES), lambda i, t: (i, 0, 0)),
            pl.BlockSpec((1, l, 2 * LANES), lambda i, t: (i, 0, 0)),
            pl.BlockSpec((1, LANES), lambda i, t: (0, 0)),
        ],
        out_specs=pl.BlockSpec((1, blk, SWA_HEADS * HEAD_DIM), lambda i, t: (i, t, 0)),
        out_shape=jax.ShapeDtypeStruct((b, l, SWA_HEADS * HEAD_DIM), F32),
        compiler_params=_cparams(("arbitrary", "arbitrary")),
        name="window_attention",
    )(q, k, v, sink)


def _dft_mats(n):
    k = jnp.arange(n, dtype=jnp.int32)
    km = (k[:, None] * k[None, :]) % (2 * n)
    ang = km.astype(F32) * (math.pi / n)
    return jnp.cos(ang).astype(BF16), (-jnp.sin(ang)).astype(BF16)


def _hyena_filter_kernel(ff_ref, fb_ref, w1_ref, b1_ref, w2_ref, b2_ref, freq_ref, w3f_ref, w3b_ref,
                         dl_ref, c_ref, s_ref, kr_ref, ki_ref, kn_ref):
    n = ff_ref.shape[0]
    freq = freq_ref[...]

    def mlp(feat):
        h = jnp.sin(freq[0:1] * (_dot_f32(feat, w1_ref[...]) + b1_ref[...]))
        return jnp.sin(freq[1:2] * (_dot_f32(h, w2_ref[...]) + b2_ref[...]))

    ff = ff_ref[...]
    fb = fb_ref[...]
    dl = dl_ref[...]
    row = lax.broadcasted_iota(jnp.int32, (n, 1), 0)
    kf = _dot_f32(mlp(ff), w3f_ref[...]) * jnp.exp(-ff[:, 0:1] * dl)
    kb = _dot_f32(mlp(fb), w3b_ref[...]) * jnp.exp(-fb[:, 0:1] * dl)
    kb = jnp.where(row == 0, 0.0, kb)
    ss = jnp.sum(kf * kf, axis=0, keepdims=True) + jnp.sum(kb * kb, axis=0, keepdims=True)
    sc = lax.rsqrt(ss + EPS)
    kf = kf * sc
    kb = kb * sc
    sgn = jnp.where((row & 1) == 0, 1.0, -1.0)
    cm = c_ref[...]
    sm = s_ref[...]
    fh, fl = _split2(kf)
    bh, bl = _split2(kb)
    kr_ref[0] = _dot(cm, fh) + _dot(cm, fl) + sgn * (_dot(cm, bh) + _dot(cm, bl))
    ki_ref[0] = _dot(sm, fh) + _dot(sm, fl) + sgn * (_dot(sm, bh) + _dot(sm, bl))
    nyq = jnp.sum((kf + kb) * sgn, axis=0, keepdims=True)
    kn_ref[0] = jnp.broadcast_to(nyq, (SUBLANES, nyq.shape[-1]))


def _hyena_filter_call(featf, featb, w1, b1, w2, b2, freq, w3, deltas, cm, sm):
    n = featf.shape[0]
    hid = w2.shape[0]
    ch = deltas.shape[-1]
    tc = 2 * LANES
    nct = ch // tc
    return pl.pallas_call(
        _hyena_filter_kernel,
        grid=(2, nct),
        in_specs=[
            pl.BlockSpec((n, hid), lambda o, j: (0, 0)),
            pl.BlockSpec((n, hid), lambda o, j: (0, 0)),
            pl.BlockSpec((hid, hid), lambda o, j: (0, 0)),
            pl.BlockSpec((1, hid), lambda o, j: (0, 0)),
            pl.BlockSpec((hid, hid), lambda o, j: (0, 0)),
            pl.BlockSpec((1, hid), lambda o, j: (0, 0)),
            pl.BlockSpec((2, hid), lambda o, j: (0, 0)),
            pl.BlockSpec((hid, tc), lambda o, j: (0, (2 * o) * nct + j)),
            pl.BlockSpec((hid, tc), lambda o, j: (0, (2 * o + 1) * nct + j)),
            pl.BlockSpec((1, tc), lambda o, j: (0, j)),
            _resident((n, n)),
            _resident((n, n)),
        ],
        out_specs=[
            pl.BlockSpec((1, n, tc), lambda o, j: (o, 0, j)),
            pl.BlockSpec((1, n, tc), lambda o, j: (o, 0, j)),
            pl.BlockSpec((1, SUBLANES, tc), lambda o, j: (o, 0, j)),
        ],
        out_shape=[
            jax.ShapeDtypeStruct((2, n, ch), F32),
            jax.ShapeDtypeStruct((2, n, ch), F32),
            jax.ShapeDtypeStruct((2, SUBLANES, ch), F32),
        ],
        compiler_params=_cparams(("arbitrary", "arbitrary")),
        name="hyena_filters",
    )(featf, featb, w1, b1, w2, b2, freq, w3, w3, deltas, cm, sm)


def _hyena_kernel(*refs, n, aliased):
    if aliased:
        refs = refs[1:]
    (v_ref, x1_ref, x2_ref, cwv_ref, cw1_ref, cw2_ref, kr_ref, ki_ref, kn_ref, bias_ref, c_ref, s_ref,
     o_ref, pad_ref, z_ref, zb_ref, p_ref) = refs
    halo = SUBLANES
    tc = o_ref.shape[-1]
    rc = min(n, HY_ROW_CHUNK)
    zero_rows = jnp.zeros((halo, tc), F32)
    pad_ref[0:halo, :] = zero_rows
    pad_ref[halo + n:2 * halo + n, :] = zero_rows

    def stage(ref):
        for r in range(0, n, rc):
            pad_ref[halo + r:halo + r + rc, :] = ref[0, r:r + rc, :]

    def conv_rows(cw, r):
        return (cw[0:1] * pad_ref[halo - 1 + r:halo - 1 + r + rc, :] + cw[1:2] * pad_ref[halo + r:halo + r + rc, :]
                + cw[2:3] * pad_ref[halo + 1 + r:halo + 1 + r + rc, :])

    def sign_rows(r):
        row = r + lax.broadcasted_iota(jnp.int32, (rc, 1), 0)
        return row, jnp.where((row & 1) == 0, 1.0, -1.0)

    stage(v_ref)
    cw = cwv_ref[...]
    for r in range(0, n, rc):
        z = conv_rows(cw, r)
        z_ref[r:r + rc, :] = z
        zb_ref[r:r + rc, :] = z.astype(BF16)

    for o, (gate_ref, gate_cw_ref) in enumerate(((x1_ref, cw1_ref), (x2_ref, cw2_ref))):
        znyq = jnp.zeros((1, tc), F32)
        for r in range(0, n, rc):
            row, sgn = sign_rows(r)
            znyq = znyq + jnp.sum(z_ref[r:r + rc, :] * sgn, axis=0, keepdims=True)
            zb = zb_ref[...]
            zr = _dot(c_ref[r:r + rc, :], zb)
            zi = _dot(s_ref[r:r + rc, :], zb)
            kr = kr_ref[o, r:r + rc, :]
            ki = ki_ref[o, r:r + rc, :]
            wgt = jnp.where(row == 0, 0.5 / n, 1.0 / n)
            p_ref[0, r:r + rc, :] = ((zr * kr - zi * ki) * wgt).astype(BF16)
            p_ref[1, r:r + rc, :] = ((zr * ki + zi * kr) * wgt).astype(BF16)
        nyq = znyq * kn_ref[o, 0:1, :] * (0.5 / n)
        stage(gate_ref)
        cw = gate_cw_ref[...]
        bias = bias_ref[o:o + 1, :]
        for r in range(0, n, rc):
            _, sgn = sign_rows(r)
            y = _dot(c_ref[r:r + rc, :], p_ref[0]) + _dot(s_ref[r:r + rc, :], p_ref[1]) + sgn * nyq
            z = conv_rows(cw, r) * (y + z_ref[r:r + rc, :] * bias)
            if o == 0:
                z_ref[r:r + rc, :] = z
                zb_ref[r:r + rc, :] = z.astype(BF16)
            else:
                o_ref[0, r:r + rc, :] = z


def _hyena_call(u, conv_w, kr, ki, kn, bias, cm, sm, n, row_block, prev_out):
    b, l, _ = u.shape
    ch = bias.shape[-1]
    tc = 2 * LANES
    nct = ch // tc
    aliased = prev_out is not None
    kern = functools.partial(_hyena_kernel, n=n, aliased=aliased)
    once = pl.Buffered(1)
    in_specs = [
        pl.BlockSpec((1, n, tc), lambda j, i: (i, row_block, j)),
        pl.BlockSpec((1, n, tc), lambda j, i: (i, row_block, nct + j)),
        pl.BlockSpec((1, n, tc), lambda j, i: (i, row_block, 2 * nct + j)),
        pl.BlockSpec((3, tc), lambda j, i: (0, j)),
        pl.BlockSpec((3, tc), lambda j, i: (0, nct + j)),
        pl.BlockSpec((3, tc), lambda j, i: (0, 2 * nct + j)),
        pl.BlockSpec((2, n, tc), lambda j, i: (0, 0, j), pipeline_mode=once),
        pl.BlockSpec((2, n, tc), lambda j, i: (0, 0, j), pipeline_mode=once),
        pl.BlockSpec((2, SUBLANES, tc), lambda j, i: (0, 0, j)),
        pl.BlockSpec((2, tc), lambda j, i: (0, j)),
        _resident((n, n)),
        _resident((n, n)),
    ]
    args = [u, u, u, conv_w, conv_w, conv_w, kr, ki, kn, bias, cm, sm]
    aliases = {}
    if aliased:
        in_specs = [pl.BlockSpec(memory_space=pl.ANY)] + in_specs
        args = [prev_out] + args
        aliases = {0: 0}
    return pl.pallas_call(
        kern,
        grid=(nct, b),
        in_specs=in_specs,
        out_specs=pl.BlockSpec((1, n, tc), lambda j, i: (i, row_block, j)),
        out_shape=jax.ShapeDtypeStruct((b, l, ch), F32),
        scratch_shapes=[
            pltpu.VMEM((n + 2 * SUBLANES, tc), F32),
            pltpu.VMEM((n, tc), F32),
            pltpu.VMEM((n, tc), BF16),
            pltpu.VMEM((2, n, tc), BF16),
        ],
        input_output_aliases=aliases,
        compiler_params=_cparams(("arbitrary", "arbitrary")),
        name="hyena_conv_n%d" % n,
    )(*args)


def _rope_tables(n, nc):
    rows = n // GRID_W
    row = jnp.repeat(jnp.arange(rows, dtype=F32), GRID_W)
    col = jnp.tile(jnp.arange(GRID_W, dtype=F32), rows)
    half = HEAD_DIM // 2
    inv = ROPE_BASE ** (-jnp.arange(0, half, 2, dtype=F32) / half)
    ar = row[:, None] * inv
    ac = col[:, None] * inv
    cos = jnp.concatenate([jnp.cos(ar), jnp.cos(ar), jnp.cos(ac), jnp.cos(ac)], axis=-1)
    sin = jnp.concatenate([-jnp.sin(ar), jnp.sin(ar), -jnp.sin(ac), jnp.sin(ac)], axis=-1)
    cos = jnp.concatenate([cos, jnp.ones((nc, HEAD_DIM), F32)], axis=0)
    sin = jnp.concatenate([sin, jnp.zeros((nc, HEAD_DIM), F32)], axis=0)
    return jnp.tile(cos, (1, LANES // HEAD_DIM)), jnp.tile(sin, (1, LANES // HEAD_DIM))


def _rope_partner_cols(width):
    d = np.arange(width)
    quarter = HEAD_DIM // 4
    return np.where((d % (2 * quarter)) < quarter, d + quarter, d - quarter)


def _hyena_feats(n):
    pos = jnp.arange(n, dtype=F32)
    t = pos / max(n - 1, 1)
    ang = (2.0 * math.pi * pos / n)[:, None] * jnp.linspace(1e-4, HY_BANDS - 1, HY_BANDS, dtype=F32)[None, :]
    feats = jnp.concatenate([t[:, None], jnp.cos(ang), -jnp.sin(ang)], axis=-1)
    feats = jnp.pad(feats, ((0, 0), (0, 64 - feats.shape[-1])))
    back = jnp.concatenate([feats[0:1], jnp.flip(feats[1:], axis=0)], axis=0)
    return feats, back


def _pad_cols(w, width):
    return jnp.pad(w, ((0, 0), (0, width - w.shape[-1])))


def _layer_ab(xz, mod, norm_g0, norm_g1, w_in, w_out, conv_w, a_log, dt_bias, gdn_g, lam_p, diff_g, lam_init,
              rope, n, nc, solve_passes):
    hd = GDN_HEADS * GDN_DIM
    wq, wk, wv, wg = (w_in[:, i * hd:(i + 1) * hd] for i in range(4))
    o = 4 * hd
    w_beta, w_alpha = w_in[:, o:o + 16], w_in[:, o + 16:o + 32]
    o += 32
    dd = DIFF_HEADS * 2 * DIFF_DIM
    wdq, wdk, wdv = (w_in[:, o + i * dd:o + (i + 1) * dd] for i in range(3))
    pairs = GDN_HEADS // 2
    pair_cols = lambda w: [w[:, p * LANES:(p + 1) * LANES] for p in range(pairs)]
    w_qkvg = jnp.concatenate([blk for grp in zip(pair_cols(wq), pair_cols(wk), pair_cols(wv), pair_cols(wg))
                              for blk in grp], axis=1)
    perm = _rope_partner_cols(dd)
    w_all = jnp.concatenate([w_qkvg, _pad_cols(jnp.concatenate([w_beta, w_alpha], axis=1), LANES),
                             wdq, wdk, wdv, wdq[:, perm], wdk[:, perm]], axis=1).astype(BF16)
    c0 = 4 * hd
    c1 = c0 + LANES
    segs = (_Seg(0, c0), _Seg(c0, LANES),
            _Seg(c1, dd, rot_start=c1 + 3 * dd, scale=DIFF_DIM ** -0.5, dtype=BF16),
            _Seg(c1 + dd, dd, rot_start=c1 + 4 * dd, dtype=BF16),
            _Seg(c1 + 2 * dd, dd, dtype=BF16, transposed=True))
    qkvg, ba, dq, dk, dvt = _proj_call(xz, mod, norm_g0, w_all, rope[0], rope[1], segs, n // ROW_TILE, "proj_ab")

    cq, ck, cv = (conv_w[:, i * hd:(i + 1) * hd] for i in range(3))
    zeros = jnp.zeros((3, LANES), F32)
    conv_l = jnp.concatenate([blk for p in range(pairs) for blk in
                              (cq[:, p * LANES:(p + 1) * LANES], ck[:, p * LANES:(p + 1) * LANES],
                               cv[:, p * LANES:(p + 1) * LANES], zeros)], axis=1)
    n_gate = 2 * GDN_HEADS
    on_decay_lanes = lambda t: jnp.pad(t.reshape(1, n_gate), ((0, 0), (n_gate, LANES - 2 * n_gate)))
    gate_params = jnp.concatenate([on_decay_lanes(a_log), on_decay_lanes(dt_bias)], axis=0)
    ng = jnp.tile(gdn_g.reshape(1, GDN_DIM), (1, 2))
    oa = _gdn_call(qkvg, ba, conv_l, gate_params, ng, n, nc, solve_passes)
    ob = _diff_call(dq, dk, dvt, lam_p, diff_g, lam_init, 2 * ROW_TILE, 0, n // (2 * ROW_TILE), 0, 2, None)
    ob = _diff_call(dq, dk, dvt, lam_p, diff_g, lam_init, nc, n // nc, 1, n, 1, ob)
    l = n + nc
    return _outproj_call(oa, ob, w_out.astype(BF16), xz, mod, norm_g1, l // ROW_TILE, n // ROW_TILE)


def _layer_cd(xz, mod, norm_g0, norm_g1, w_in, w_out, sink, hy_conv, hy_w1, hy_b1, hy_w2, hy_b2, hy_w3, hy_freq,
              hy_bias, rope, n, nc, last, dft_x, dft_c):
    qd = SWA_HEADS * HEAD_DIM
    kd = SWA_KV_HEADS * HEAD_DIM
    wq, wk, wv, wu = w_in[:, 0:qd], w_in[:, qd:qd + kd], w_in[:, qd + kd:qd + 2 * kd], w_in[:, qd + 2 * kd:]
    dup = lambda w: jnp.concatenate([w[:, 0:HEAD_DIM], w[:, 0:HEAD_DIM], w[:, HEAD_DIM:], w[:, HEAD_DIM:]], axis=1)
    wk2, wv2 = dup(wk), dup(wv)
    ud = wu.shape[1]
    w_all = jnp.concatenate([wq, wk2, wv2, wu, wq[:, _rope_partner_cols(qd)], wk2[:, _rope_partner_cols(2 * kd)]],
                            axis=1).astype(BF16)
    o_u = qd + 4 * kd
    segs = (_Seg(0, qd, rot_start=o_u + ud, scale=HEAD_DIM ** -0.5, dtype=BF16),
            _Seg(qd, 2 * kd, rot_start=o_u + ud + qd, dtype=BF16),
            _Seg(qd + 2 * kd, 2 * kd, dtype=BF16), _Seg(o_u, ud))
    q, k, v, u = _proj_call(xz, mod, norm_g0, w_all, rope[0], rope[1], segs, n // ROW_TILE, "proj_cd")
    oc = _swa_call(q, k, v, _pad_cols(sink.reshape(1, SWA_HEADS), LANES), n, nc, not last)

    ch = hy_bias.shape[-1]
    deltas = jnp.abs(jnp.linspace(HY_MIN_DECAY, HY_MAX_DECAY, ch, dtype=F32)).reshape(1, ch)
    hid = hy_w2.shape[0]
    w1p = jnp.pad(hy_w1, ((0, hid - hy_w1.shape[0]), (0, 0)))
    filt = lambda m, dft: _hyena_filter_call(*_hyena_feats(m), w1p, hy_b1.reshape(1, hid), hy_w2,
                                             hy_b2.reshape(1, hid), hy_freq, hy_w3, deltas, *dft)
    kr, ki, kn = filt(n, dft_x)
    od = _hyena_call(u, hy_conv, kr, ki, kn, hy_bias, *dft_x, n, 0, None)
    if not last:
        kr, ki, kn = filt(nc, dft_c)
        od = _hyena_call(u, hy_conv, kr, ki, kn, hy_bias, *dft_c, nc, n // nc, od)
    n_tiles = (n if last else n + nc) // ROW_TILE
    return _outproj_call(oc, od, w_out.astype(BF16), xz, mod, norm_g1, n_tiles, n // ROW_TILE)


def kernel(x, c, ctx, c_ctx, w_mod, b_mod, norm_g, ffn_w_up, ffn_conv, ffn_w_down, ab_w_in, ab_w_out, gdn_conv, gdn_a_log, gdn_dt_bias, gdn_norm_g, diff_lambda, diff_norm_g, cd_w_in, cd_w_out, swa_sink, hy_conv, hy_w1, hy_b1, hy_w2, hy_b2, hy_w3, hy_freq, hy_bias):
    b, n, d = x.shape
    nc = ctx.shape[1]
    depth = w_mod.shape[0]
    assert n % ROW_TILE == 0 and nc == ROW_TILE and n % GRID_W == 0
    xz = jnp.concatenate([x, ctx], axis=1)
    rows = -(-(b + 1) // SUBLANES) * SUBLANES
    cc = jnp.concatenate([c, c_ctx[None], jnp.zeros((rows - b - 1, d), F32)], axis=0)
    mods = _mod_call(cc, w_mod, b_mod)
    rope = _rope_tables(n, nc)
    dft_x = _dft_mats(n)
    dft_c = _dft_mats(nc)
    n_x_tiles = n // ROW_TILE
    for l in range(depth):
        last = l == depth - 1
        i = l // 2
        mx = mods[l, :b].reshape(b, 1, 6, d)
        mz = jnp.broadcast_to(mods[l, b].reshape(1, 1, 6, d), (b, 1, 6, d))
        mod = jnp.concatenate([mx, mz], axis=1)
        if l % 2 == 0:
            lam_init = 0.8 - 0.6 * math.exp(-0.3 * l)
            xz = _layer_ab(xz, mod, norm_g[l, 0], norm_g[l, 1], ab_w_in[i], ab_w_out[i], gdn_conv[i], gdn_a_log[i],
                           gdn_dt_bias[i], gdn_norm_g[i], diff_lambda[i], diff_norm_g[i], lam_init, rope, n, nc, GDN_SOLVE_PASSES)
        else:
            xz = _layer_cd(xz, mod, norm_g[l, 0], norm_g[l, 1], cd_w_in[i], cd_w_out[i], swa_sink[i], hy_conv[i],
                           hy_w1[i], hy_b1[i], hy_w2[i], hy_b2[i], hy_w3[i], hy_freq[i], hy_bias[i], rope, n, nc,
                           last, dft_x, dft_c)
        n_tiles = xz.shape[1] // ROW_TILE
        xz = _ffn_call(xz, mod, norm_g[l, 2], norm_g[l, 3], ffn_w_up[l].astype(BF16), ffn_conv[l],
                       ffn_w_down[l].astype(BF16), n_tiles, n_x_tiles)
    return xz[:, :n] if xz.shape[1] != n else xz
```

```python
import functools
import math
from typing import NamedTuple, Optional

import jax
import jax.numpy as jnp
import numpy as np
from jax import lax
from jax.experimental import pallas as pl
from jax.experimental.pallas import tpu as pltpu

F32 = jnp.float32
BF16 = jnp.bfloat16

EPS = 1e-6
NEG_INF = -1e30
GRID_W = 64
HEAD_DIM = 64
ROPE_BASE = 10000.0
GDN_HEADS = 8
GDN_DIM = 64
GDN_CHUNK = 64
DIFF_HEADS = 4
DIFF_DIM = 64
SWA_HEADS = 8
SWA_KV_HEADS = 2
SWA_WINDOW = 128
SWA_BLOCK = 128
HY_BANDS = 16
HY_MIN_DECAY = math.log(1e-2) / 1.5
HY_MAX_DECAY = math.log(1e-2) / 0.3
HY_ROW_CHUNK = 512

LANES = 128
SUBLANES = 8
MXU_WIDTH = 256
FFN_COL_CHUNK = 6 * MXU_WIDTH
ROW_TILE = 256
VMEM_LIMIT = 56 * 1024 * 1024


def _cparams(sem):
    return pltpu.CompilerParams(dimension_semantics=sem, vmem_limit_bytes=VMEM_LIMIT)


def _resident(shape):
    zeros = (0,) * len(shape)
    return pl.BlockSpec(shape, lambda *_: zeros, pipeline_mode=pl.Buffered(1))


def _log2(v):
    assert v & (v - 1) == 0
    return v.bit_length() - 1


def _sigmoid(x):
    return 1.0 / (1.0 + jnp.exp(-x))


def _silu(x):
    return x * _sigmoid(x)


def _softplus(x):
    return jnp.maximum(x, 0.0) + jnp.log1p(jnp.exp(-jnp.abs(x)))


def _dot(a, b):
    return jnp.dot(a, b, preferred_element_type=F32)


def _dot_nt(a, b):
    return lax.dot_general(a, b, (((1,), (1,)), ((), ())), preferred_element_type=F32)


def _dot_tn(a, b):
    return lax.dot_general(a, b, (((0,), (0,)), ((), ())), preferred_element_type=F32)


def _dot_f32(a, b):
    return jnp.dot(a, b, preferred_element_type=F32, precision=lax.Precision.HIGHEST)


def _split2(x):
    hi = x.astype(BF16)
    lo = (x - hi.astype(F32)).astype(BF16)
    return hi, lo


def _split3(x):
    hi = x.astype(BF16)
    r = x - hi.astype(F32)
    mid = r.astype(BF16)
    lo = (r - mid.astype(F32)).astype(BF16)
    return hi, mid, lo


def _dot_sel(x, sel_bf16):
    hi, mid, lo = _split3(x)
    return _dot(hi, sel_bf16) + _dot(mid, sel_bf16) + _dot(lo, sel_bf16)


def _mm(a, b):
    return _dot(a.astype(BF16), b.astype(BF16))


def _rms(y, g):
    return y * lax.rsqrt(jnp.mean(y * y, axis=-1, keepdims=True) + EPS) * g


def _modnorm(x, g, shift, scale):
    return _rms(x, g) * (1.0 + scale) + shift


def _mod_kernel(cc_ref, w_ref, b_ref, o_ref):
    s = _silu(cc_ref[...])
    o_ref[0] = _dot(s.astype(BF16), w_ref[0].astype(BF16)) + b_ref[0]


def _mod_call(cc, w_mod, b_mod):
    depth, d, nm = w_mod.shape
    rows = cc.shape[0]
    ct = 1536
    return pl.pallas_call(
        _mod_kernel,
        grid=(depth, nm // ct),
        in_specs=[
            pl.BlockSpec((rows, d), lambda l, j: (0, 0)),
            pl.BlockSpec((1, d, ct), lambda l, j: (l, 0, j)),
            pl.BlockSpec((1, 1, ct), lambda l, j: (l, 0, j)),
        ],
        out_specs=pl.BlockSpec((1, rows, ct), lambda l, j: (l, 0, j)),
        out_shape=jax.ShapeDtypeStruct((depth, rows, nm), F32),
        compiler_params=_cparams(("arbitrary", "arbitrary")),
        name="adaln_mod",
    )(cc, w_mod, b_mod.reshape(depth, 1, nm))


class _Seg(NamedTuple):
    start: int
    width: int
    rot_start: Optional[int] = None
    scale: float = 1.0
    dtype: type = F32
    transposed: bool = False


def _proj_kernel(x_ref, mod_ref, g_ref, w_ref, cos_ref, sin_ref, *out_refs, segs):
    m = mod_ref[0, 0]
    h = _modnorm(x_ref[0], g_ref[...], m[0:1], m[1:2]).astype(BF16)
    for o_ref, seg in zip(out_refs, segs):
        y = _dot(h, w_ref[:, seg.start:seg.start + seg.width])
        if seg.rot_start is not None:
            yr = _dot(h, w_ref[:, seg.rot_start:seg.rot_start + seg.width])
            reps = seg.width // LANES
            cos = jnp.concatenate([cos_ref[...]] * reps, axis=1)
            sin = jnp.concatenate([sin_ref[...]] * reps, axis=1)
            y = y * cos + yr * sin
        if seg.scale != 1.0:
            y = y * seg.scale
        if seg.transposed:
            y = y.T
        o_ref[0] = y.astype(seg.dtype)


def _proj_call(xz, mod, g, w, cos_t, sin_t, segs, n_x_tiles, name):
    b, l, d = xz.shape
    tm = ROW_TILE
    nt = l // tm
    p = w.shape[1]
    return pl.pallas_call(
        functools.partial(_proj_kernel, segs=segs),
        grid=(nt, b),
        in_specs=[
            pl.BlockSpec((1, tm, d), lambda t, i: (i, t, 0)),
            pl.BlockSpec((1, 1, 6, d), lambda t, i: (i, t // n_x_tiles, 0, 0)),
            pl.BlockSpec((1, d), lambda t, i: (0, 0)),
            _resident((d, p)),
            pl.BlockSpec((tm, LANES), lambda t, i: (t, 0)),
            pl.BlockSpec((tm, LANES), lambda t, i: (t, 0)),
        ],
        out_specs=[pl.BlockSpec((1, s.width, tm), lambda t, i: (i, 0, t)) if s.transposed
                   else pl.BlockSpec((1, tm, s.width), lambda t, i: (i, t, 0)) for s in segs],
        out_shape=[jax.ShapeDtypeStruct((b, s.width, l) if s.transposed else (b, l, s.width), s.dtype)
                   for s in segs],
        compiler_params=_cparams(("arbitrary", "arbitrary")),
        name=name,
    )(xz, mod, g.reshape(1, d), w, cos_t, sin_t)


def _outproj_kernel(o1_ref, o2_ref, w_ref, x_ref, mod_ref, g_ref, out_ref):
    k1 = o1_ref.shape[-1]
    y = _dot(o1_ref[0].astype(BF16), w_ref[0:k1, :]) + _dot(o2_ref[0].astype(BF16), w_ref[k1:, :])
    m = mod_ref[0, 0]
    out_ref[0] = x_ref[0] + m[2:3] * _rms(y, g_ref[...])


def _outproj_call(o1, o2, w, xz, mod, g, n_tiles, n_x_tiles):
    b, _, d = xz.shape
    tm = ROW_TILE
    k1, k2 = o1.shape[-1], o2.shape[-1]
    return pl.pallas_call(
        _outproj_kernel,
        grid=(n_tiles, b),
        in_specs=[
            pl.BlockSpec((1, tm, k1), lambda t, i: (i, t, 0)),
            pl.BlockSpec((1, tm, k2), lambda t, i: (i, t, 0)),
            _resident((k1 + k2, d)),
            pl.BlockSpec((1, tm, d), lambda t, i: (i, t, 0)),
            pl.BlockSpec((1, 1, 6, d), lambda t, i: (i, t // n_x_tiles, 0, 0)),
            pl.BlockSpec((1, d), lambda t, i: (0, 0)),
        ],
        out_specs=pl.BlockSpec((1, tm, d), lambda t, i: (i, t, 0)),
        out_shape=jax.ShapeDtypeStruct((b, n_tiles * tm, d), F32),
        compiler_params=_cparams(("arbitrary", "arbitrary")),
        name="mixer_out",
    )(o1, o2, w, xz, mod, g.reshape(1, d))


def _ffn_kernel(xp_ref, x_ref, xn_ref, mod_ref, g2_ref, g3_ref, wup_ref, cw_ref, wdn_ref, out_ref,
                up_ref, *, tm, n_x_tiles, n_tiles, cf, dff):
    t = pl.program_id(0)
    first = jnp.logical_or(t == 0, t == n_x_tiles)
    last = jnp.logical_or(t == n_x_tiles - 1, t == n_tiles - 1)
    keep_top = jnp.where(first, 0.0, 1.0)
    keep_bot = jnp.where(last, 0.0, 1.0)
    m = mod_ref[0, 0]
    halo = SUBLANES
    xe = jnp.concatenate([xp_ref[0], x_ref[0], xn_ref[0]], axis=0)
    h = _modnorm(xe, g2_ref[...], m[3:4], m[4:5]).astype(BF16)
    acc = jnp.zeros((tm, x_ref.shape[-1]), F32)
    for c0 in range(0, dff, cf):
        wd = min(cf, dff - c0)
        halves = []
        for half, base in enumerate((c0, dff + c0)):
            u = _dot(h, wup_ref[:, base:base + wd])
            up_ref[half, :, 0:wd] = u
            up_ref[half, 0:halo, 0:wd] = u[0:halo] * keep_top
            up_ref[half, tm + halo:tm + 2 * halo, 0:wd] = u[tm + halo:] * keep_bot
            cw = cw_ref[:, base:base + wd]
            halves.append(cw[0:1] * up_ref[half, halo - 1:halo - 1 + tm, 0:wd]
                          + cw[1:2] * up_ref[half, halo:halo + tm, 0:wd]
                          + cw[2:3] * up_ref[half, halo + 1:halo + 1 + tm, 0:wd])
        act = (_silu(halves[1]) * halves[0]).astype(BF16)
        acc = acc + _dot(act, wdn_ref[c0:c0 + wd, :])
    out_ref[0] = x_ref[0] + m[5:6] * _rms(acc, g3_ref[...])


def _ffn_call(xz, mod, g2, g3, w_up, conv_w, w_down, n_tiles, n_x_tiles):
    b, l, d = xz.shape
    tm = ROW_TILE
    dff = w_down.shape[0]
    cf = FFN_COL_CHUNK
    hb = tm // SUBLANES
    nb8 = l // SUBLANES
    kern = functools.partial(_ffn_kernel, tm=tm, n_x_tiles=n_x_tiles, n_tiles=n_tiles, cf=cf, dff=dff)
    return pl.pallas_call(
        kern,
        grid=(n_tiles, b),
        in_specs=[
            pl.BlockSpec((1, SUBLANES, d), lambda t, i: (i, jnp.maximum(t * hb - 1, 0), 0)),
            pl.BlockSpec((1, tm, d), lambda t, i: (i, t, 0)),
            pl.BlockSpec((1, SUBLANES, d), lambda t, i: (i, jnp.minimum((t + 1) * hb, nb8 - 1), 0)),
            pl.BlockSpec((1, 1, 6, d), lambda t, i: (i, t // n_x_tiles, 0, 0)),
            pl.BlockSpec((1, d), lambda t, i: (0, 0)),
            pl.BlockSpec((1, d), lambda t, i: (0, 0)),
            _resident((d, 2 * dff)),
            pl.BlockSpec((3, 2 * dff), lambda t, i: (0, 0)),
            _resident((dff, d)),
        ],
        out_specs=pl.BlockSpec((1, tm, d), lambda t, i: (i, t, 0)),
        out_shape=jax.ShapeDtypeStruct((b, n_tiles * tm, d), F32),
        scratch_shapes=[pltpu.VMEM((2, tm + 2 * SUBLANES, cf), F32)],
        compiler_params=_cparams(("arbitrary", "arbitrary")),
        name="conv_ffn",
    )(xz, xz, xz, mod, g2.reshape(1, d), g3.reshape(1, d), w_up, conv_w, w_down)


def _half_sums(x2, lane_lo):
    s0 = jnp.sum(jnp.where(lane_lo, x2, 0.0), axis=-1, keepdims=True)
    s1 = jnp.sum(jnp.where(lane_lo, 0.0, x2), axis=-1, keepdims=True)
    return jnp.where(lane_lo, s0, s1)


def _gdn_kernel(qkvg_ref, ba_ref, cw_ref, gp_ref, ng_ref, out_ref,
                pad_ref, q_ref, k_ref, v_ref, bb_ref, gb_ref, qe_ref, mp_ref, ou_ref, nn_ref, egl_ref, o_ref,
                *, n, nc, chunks_per_iter):
    l = n + nc
    c = GDN_CHUNK
    n_chunks = l // c
    pair = pl.program_id(1)
    halo = SUBLANES
    lane = lax.broadcasted_iota(jnp.int32, (1, LANES), 1)
    lane_lo = lane < GDN_DIM

    cw = cw_ref[:, 0:3 * LANES]
    zero_rows = jnp.zeros((halo, 3 * LANES), F32)
    for seq_start, seq_len in ((0, n), (n, nc)):
        base = halo + seq_start + (2 * halo if seq_start else 0)
        pad_ref[base - halo:base, :] = zero_rows
        pad_ref[base + seq_len:base + seq_len + halo, :] = zero_rows
        step = 256
        for r in range(0, seq_len, step):
            pad_ref[base + r:base + r + step, :] = qkvg_ref[0, seq_start + r:seq_start + r + step, 0:3 * LANES]
        for r in range(0, seq_len, step):
            y = (cw[0:1] * pad_ref[base + r - 1:base + r - 1 + step, :]
                 + cw[1:2] * pad_ref[base + r:base + r + step, :]
                 + cw[2:3] * pad_ref[base + r + 1:base + r + 1 + step, :])
            y = _silu(y)
            q = y[:, 0:LANES]
            k = y[:, LANES:2 * LANES]
            rows = slice(seq_start + r, seq_start + r + step)
            q_ref[rows, :] = q * lax.rsqrt(_half_sums(q * q, lane_lo) + EPS) * (GDN_DIM ** -0.5)
            k_ref[rows, :] = k * lax.rsqrt(_half_sums(k * k, lane_lo) + EPS)
            v_ref[rows, :] = y[:, 2 * LANES:3 * LANES]

    sel_r = lax.broadcasted_iota(jnp.int32, (LANES, 4 * LANES), 0)
    sel_c = lax.broadcasted_iota(jnp.int32, (LANES, 4 * LANES), 1)
    quarter = sel_c >> _log2(LANES)
    src_lane = (quarter & 1) * 2 * GDN_HEADS + (quarter >> 1) * GDN_HEADS + 2 * pair + ((sel_c >> _log2(GDN_DIM)) & 1)
    sel = (sel_r == src_lane).astype(BF16)
    gblk = 256
    bi = lax.broadcasted_iota(jnp.int32, (gblk, gblk), 0)
    bj = lax.broadcasted_iota(jnp.int32, (gblk, gblk), 1)
    same_chunk = (bi >> _log2(c)) == (bj >> _log2(c))
    csum = (jnp.logical_and(same_chunk, bi >= bj).astype(BF16), jnp.logical_and(same_chunk, bi <= bj).astype(BF16))
    neg_a = -jnp.exp(gp_ref[0:1, :])
    dt_bias = gp_ref[1:2, :]
    for r in range(0, l, gblk):
        ba = ba_ref[0, r:r + gblk, :]
        gates = jnp.where(lane < 2 * GDN_HEADS, _sigmoid(ba), neg_a * _softplus(ba + dt_bias))
        x = _dot_sel(gates, sel)
        for d in range(2):
            bb_ref[d, r:r + gblk, :] = x[:, 2 * d * LANES:(2 * d + 1) * LANES]
            gb_ref[d, r:r + gblk, :] = _dot_sel_lhs(csum[d], x[:, (2 * d + 1) * LANES:(2 * d + 2) * LANES])

    r2 = lax.broadcasted_iota(jnp.int32, (2 * c, 2 * c), 0)
    c2 = lax.broadcasted_iota(jnp.int32, (2 * c, 2 * c), 1)
    same_head = (r2 >= c) == (c2 >= c)
    eye = (r2 == c2).astype(F32)
    masks = ((jnp.logical_and(same_head, r2 >= c2), jnp.logical_and(same_head, r2 > c2)),
             (jnp.logical_and(same_head, r2 <= c2), jnp.logical_and(same_head, r2 < c2)))
    m0 = lane_lo.astype(F32)
    m1 = 1.0 - m0

    def pair_mask(lv, lower):
        same_block = (r2 >> (lv + 1)) == (c2 >> (lv + 1))
        r_hi = ((r2 >> lv) & 1) == 1
        c_hi = ((c2 >> lv) & 1) == 1
        off = jnp.logical_and(r_hi, jnp.logical_not(c_hi)) if lower else jnp.logical_and(c_hi, jnp.logical_not(r_hi))
        return jnp.logical_and(same_block, off)

    pair_masks = tuple(tuple(pair_mask(lv, lower) for lv in range(_log2(c))) for lower in (True, False))

    def stack_heads(x2):
        return jnp.concatenate([x2 * m0, x2 * m1], axis=0)

    def fold_heads(x):
        return x[0:c] + x[c:2 * c]

    def chunks_local(dirs, qs, ks, vs, betas, gcs):
        each = lambda f, *cols: [f(*args) for args in zip(*cols)]
        incl = [masks[d][0] for d in dirs]
        strict = [masks[d][1] for d in dirs]
        g1 = each(lambda gc2: jnp.concatenate([gc2, gc2], axis=0), gcs)
        decay = each(lambda g, m: jnp.where(m, jnp.exp(jnp.where(m, g - g.T, 0.0)), 0.0), g1, incl)
        kb = each(lambda k, b: k * b, ks, betas)
        kst = each(lambda k: stack_heads(k).astype(BF16), ks)
        a_raw = each(lambda x, y: _dot_nt(stack_heads(x).astype(BF16), y), kb, kst)
        qk_raw = each(lambda x, y: _dot_nt(stack_heads(x).astype(BF16), y), qs, kst)
        qk = each(lambda m, x, dc: jnp.where(m, x * dc, 0.0).astype(BF16), incl, qk_raw, decay)
        a = each(lambda m, x, dc: jnp.where(m, x * dc, 0.0), strict, a_raw, decay)
        tinv = each(lambda d, x: eye - jnp.where(pair_masks[d][0], x, 0.0), dirs, a)
        for lv in range(1, _log2(c)):
            ta = each(lambda d, t, x: _mm(t, jnp.where(pair_masks[d][lv], x, 0.0)), dirs, tinv, a)
            tat = each(_mm, ta, tinv)
            tinv = each(lambda t, x: t - x, tinv, tat)
        egc = each(jnp.exp, gcs)
        rhs = each(lambda v, b, x, e: jnp.concatenate([stack_heads(v * b), stack_heads(x * e)], axis=1),
                   vs, betas, kb, egc)
        sol = each(_mm, tinv, rhs)
        u2 = each(lambda x: fold_heads(x[:, 0:LANES]), sol)
        w2 = each(lambda x: fold_heads(x[:, LANES:2 * LANES]), sol)
        gl = each(lambda d, gc2: gc2[c - 1:c, :] if d == 0 else gc2[0:1, :], dirs, gcs)
        ktail = each(lambda k, g, gc2: (k * jnp.exp(g - gc2)).astype(BF16), ks, gl, gcs)
        qwu = each(lambda x, w, u: _dot(x, jnp.concatenate([stack_heads(w), stack_heads(u)], axis=1).astype(BF16)),
                   qk, w2, u2)
        kwu = each(lambda x, w, u: _dot_tn(x, jnp.concatenate([w, u], axis=1).astype(BF16)), ktail, w2, u2)
        q_eff = each(lambda q, e, x: (q * e - fold_heads(x[:, 0:LANES])).astype(BF16), qs, egc, qwu)
        m_neg = each(lambda x: jnp.where(same_head, -x[:, 0:LANES], 0.0).astype(BF16), kwu)
        o_loc = each(lambda x: fold_heads(x[:, LANES:2 * LANES]), qwu)
        s_loc = each(lambda x: jnp.where(same_head, x[:, LANES:2 * LANES], 0.0), kwu)
        egl = each(lambda g: jnp.broadcast_to(jnp.exp(g), (SUBLANES, LANES)), gl)
        return q_eff, m_neg, o_loc, s_loc, egl

    def chunk_rows(chunk, rows_per_chunk):
        return pl.ds(pl.multiple_of(chunk * rows_per_chunk, rows_per_chunk), rows_per_chunk)

    def local_body(it, carry):
        dirs, chunks, qs, ks, vs, betas, gcs = [], [], [], [], [], [], []
        for g in range(chunks_per_iter):
            chunk = it + g * (n_chunks // chunks_per_iter)
            rows = chunk_rows(chunk, c)
            for d in range(2):
                dirs.append(d)
                chunks.append(chunk)
                qs.append(q_ref[rows, :])
                ks.append(k_ref[rows, :])
                vs.append(v_ref[rows, :])
                betas.append(bb_ref[d, rows, :])
                gcs.append(gb_ref[d, rows, :])
        results = chunks_local(dirs, qs, ks, vs, betas, gcs)
        for d, chunk, q_eff, m_neg, o_loc, s_loc, egl in zip(dirs, chunks, *results):
            qe_ref[d, chunk_rows(chunk, c), :] = q_eff
            mp_ref[d, chunk_rows(chunk, 2 * c), :] = m_neg
            ou_ref[d, chunk_rows(chunk, c), :] = o_loc
            nn_ref[d, chunk_rows(chunk, 2 * c), :] = s_loc
            egl_ref[d, chunk_rows(chunk, SUBLANES), :] = egl
        return carry

    lax.fori_loop(0, n_chunks // chunks_per_iter, local_body, 0)

    ctx_chunks = nc // c

    def scan_body(s, carry):
        chunks = (jnp.where(s < ctx_chunks, s + n // c, s - ctx_chunks), n_chunks - 1 - s)
        loaded = [(qe_ref[d, chunk_rows(ch, c), :], mp_ref[d, chunk_rows(ch, 2 * c), :],
                   ou_ref[d, chunk_rows(ch, c), :], nn_ref[d, chunk_rows(ch, 2 * c), :],
                   egl_ref[d, chunk_rows(ch, SUBLANES), :]) for d, ch in enumerate(chunks)]
        res = [_dot(jnp.concatenate([ld[0], ld[1]], axis=0), s2.astype(BF16)) for ld, s2 in zip(loaded, carry)]
        for d, ch in enumerate(chunks):
            o_ref[d, chunk_rows(ch, c), :] = res[d][0:c] + loaded[d][2]
        return tuple(s2 * ld[4][0:1] + r[c:3 * c] + ld[3] for s2, ld, r in zip(carry, loaded, res))

    zero_state = jnp.zeros((2 * c, 2 * c), F32)
    lax.fori_loop(0, n_chunks, scan_body, (zero_state, zero_state))

    ng = ng_ref[...]
    step = 256
    for r in range(0, l, step):
        o = o_ref[0, r:r + step, :] + o_ref[1, r:r + step, :]
        ms = _half_sums(o * o, lane_lo) * (1.0 / GDN_DIM)
        gate = qkvg_ref[0, r:r + step, 3 * LANES:4 * LANES]
        out_ref[0, r:r + step, :] = o * lax.rsqrt(ms + EPS) * ng * _silu(gate)


def _dot_sel_lhs(sel_bf16, x):
    hi, mid, lo = _split3(x)
    return _dot(sel_bf16, hi) + _dot(sel_bf16, mid) + _dot(sel_bf16, lo)


def _gdn_call(qkvg, ba, conv_w, gate_params, ng, n, nc):
    b, l, _ = qkvg.shape
    pairs = GDN_HEADS // 2
    n_chunks = l // GDN_CHUNK
    kern = functools.partial(_gdn_kernel, n=n, nc=nc, chunks_per_iter=4)
    return pl.pallas_call(
        kern,
        grid=(b, pairs),
        in_specs=[
            pl.BlockSpec((1, l, 4 * LANES), lambda i, p: (i, 0, p)),
            pl.BlockSpec((1, l, LANES), lambda i, p: (i, 0, 0)),
            pl.BlockSpec((3, 4 * LANES), lambda i, p: (0, p)),
            pl.BlockSpec((2, LANES), lambda i, p: (0, 0)),
            pl.BlockSpec((1, LANES), lambda i, p: (0, 0)),
        ],
        out_specs=pl.BlockSpec((1, l, LANES), lambda i, p: (i, 0, p)),
        out_shape=jax.ShapeDtypeStruct((b, l, pairs * LANES), F32),
        scratch_shapes=[
            pltpu.VMEM((l + 5 * SUBLANES, 3 * LANES), F32),
            pltpu.VMEM((l, LANES), F32),
            pltpu.VMEM((l, LANES), F32),
            pltpu.VMEM((l, LANES), F32),
            pltpu.VMEM((2, l, LANES), F32),
            pltpu.VMEM((2, l, LANES), F32),
            pltpu.VMEM((2, l, LANES), BF16),
            pltpu.VMEM((2, 2 * l, LANES), BF16),
            pltpu.VMEM((2, l, LANES), F32),
            pltpu.VMEM((2, 2 * l, LANES), F32),
            pltpu.VMEM((2, n_chunks * SUBLANES, LANES), F32),
            pltpu.VMEM((2, l, LANES), F32),
        ],
        compiler_params=_cparams(("arbitrary", "arbitrary")),
        name="gated_deltanet",
    )(qkvg, ba, conv_w, gate_params, ng)


def _diff_kernel(*refs, key_start, n_sub, lam_init, aliased):
    if aliased:
        refs = refs[1:]
    q_ref, k_ref, vt_ref, lam_ref, ng_ref, o_ref = refs
    lp = lam_ref[...]
    lam = (jnp.exp(jnp.sum(lp[0:1] * lp[1:2], axis=-1, keepdims=True))
           - jnp.exp(jnp.sum(lp[2:3] * lp[3:4], axis=-1, keepdims=True)) + lam_init)
    lane = lax.broadcasted_iota(jnp.int32, (1, LANES), 1)
    halves = (lane < DIFF_DIM, lane >= DIFF_DIM)
    ng = ng_ref[...]
    k = k_ref[0, key_start:, :]
    vt = vt_ref[0, :, key_start:]
    tq = q_ref.shape[1] // n_sub
    scores = []
    for i in range(n_sub):
        q = q_ref[0, i * tq:(i + 1) * tq, :]
        scores.append([_dot_nt(k, jnp.where(m, q, jnp.zeros_like(q))) for m in halves])
    for i, s in enumerate(scores):
        e = [jnp.exp(x - jnp.max(x, axis=0, keepdims=True)) for x in s]
        pv = [_dot(vt, x.astype(BF16)) for x in e]
        parts = [x * (1.0 / jnp.sum(y, axis=0, keepdims=True)) for x, y in zip(pv, e)]
        ot = parts[0] - lam * parts[1]
        ot = ot * lax.rsqrt(jnp.mean(ot * ot, axis=0, keepdims=True) + EPS)
        o_ref[0, i * tq:(i + 1) * tq, :] = ot.T * ng * (1.0 - lam_init)


def _diff_call(dq, dk, dvt, lam_p, ng, lam_init, q_rows, first_block, n_q_blocks, key_start, n_sub, prev_out):
    b, l, _ = dq.shape
    aliased = prev_out is not None
    kern = functools.partial(_diff_kernel, key_start=key_start, n_sub=n_sub, lam_init=lam_init, aliased=aliased)
    row_of = lambda t: first_block + t
    in_specs = [
        pl.BlockSpec((1, q_rows, LANES), lambda i, h, t: (i, row_of(t), h)),
        pl.BlockSpec((1, l, LANES), lambda i, h, t: (i, 0, h)),
        pl.BlockSpec((1, LANES, l), lambda i, h, t: (i, h, 0)),
        pl.BlockSpec((4, DIFF_DIM), lambda i, h, t: (0, 0)),
        pl.BlockSpec((1, LANES), lambda i, h, t: (0, 0)),
    ]
    args = [dq, dk, dvt, lam_p, ng.reshape(1, LANES)]
    aliases = {}
    if aliased:
        in_specs = [pl.BlockSpec(memory_space=pl.ANY)] + in_specs
        args = [prev_out] + args
        aliases = {0: 0}
    return pl.pallas_call(
        kern,
        grid=(b, DIFF_HEADS, n_q_blocks),
        in_specs=in_specs,
        out_specs=pl.BlockSpec((1, q_rows, LANES), lambda i, h, t: (i, row_of(t), h)),
        out_shape=jax.ShapeDtypeStruct((b, l, DIFF_HEADS * LANES), F32),
        input_output_aliases=aliases,
        compiler_params=_cparams(("arbitrary", "arbitrary", "arbitrary")),
        name="diff_attention_ctx" if aliased else "diff_attention",
    )(*args)


def _swa_kernel(q_ref, k_ref, v_ref, sink_ref, o_ref, *, n, nc):
    t = pl.program_id(1)
    blk = SWA_BLOCK
    n_x = n // blk
    q = q_ref[0]
    lane = lax.broadcasted_iota(jnp.int32, (1, LANES), 1)
    lane_lo = lane < HEAD_DIM
    sink = sink_ref[...]
    group = SWA_HEADS // SWA_KV_HEADS

    def run(keys, vals, valid):
        head_of_row = lax.broadcasted_iota(jnp.int32, (group * blk, 1), 0) >> _log2(blk)
        kvs = range(SWA_KV_HEADS)
        kk = [keys[:, kvh * LANES:(kvh + 1) * LANES] for kvh in kvs]
        vv = [vals[:, kvh * LANES:(kvh + 1) * LANES] for kvh in kvs]
        qst, sk = [], []
        for kvh in kvs:
            q_rows = []
            sk_rows = jnp.zeros((group * blk, 1), F32)
            for g in range(group):
                h = kvh * group + g
                qp = q[:, (h // 2) * LANES:(h // 2 + 1) * LANES]
                q_rows.append(jnp.where(lane_lo if h % 2 == 0 else jnp.logical_not(lane_lo), qp, jnp.zeros_like(qp)))
                sk_rows = jnp.where(head_of_row == g, sink[:, h:h + 1], sk_rows)
            qst.append(jnp.concatenate(q_rows, axis=0))
            sk.append(sk_rows)
        s = [_dot_nt(x, y) for x, y in zip(qst, kk)]
        if valid is not None:
            s = [jnp.where(valid, x, NEG_INF) for x in s]
        mx = [jnp.maximum(jnp.max(x, axis=-1, keepdims=True), y) for x, y in zip(s, sk)]
        e = [jnp.exp(x - m) for x, m in zip(s, mx)]
        pv = [_dot(x.astype(BF16), y) for x, y in zip(e, vv)]
        den = [jnp.sum(x, axis=-1, keepdims=True) + jnp.exp(y - m) for x, y, m in zip(e, sk, mx)]
        outs = []
        for o, dn in zip(pv, den):
            o = o * (1.0 / dn)
            for g in range(0, group, 2):
                outs.append(jnp.where(lane_lo, o[g * blk:(g + 1) * blk], o[(g + 1) * blk:(g + 2) * blk]))
        o_ref[0] = jnp.concatenate(outs, axis=1)

    @pl.when(t < n_x)
    def _():
        start = pl.multiple_of(jnp.clip((t - 1) * blk, 0, n - 3 * blk), blk)
        keys = jnp.concatenate([k_ref[0, pl.ds(start, 3 * blk), :], k_ref[0, n:n + nc, :]], axis=0)
        vals = jnp.concatenate([v_ref[0, pl.ds(start, 3 * blk), :], v_ref[0, n:n + nc, :]], axis=0)
        shape = (group * blk, 3 * blk + nc)
        qpos = t * blk + (lax.broadcasted_iota(jnp.int32, shape, 0) & (blk - 1))
        col = lax.broadcasted_iota(jnp.int32, shape, 1)
        dist = qpos - (start + col)
        in_window = jnp.logical_and(dist <= SWA_WINDOW, dist >= -SWA_WINDOW)
        valid = jnp.logical_or(col >= 3 * blk, in_window)
        run(keys, vals, valid)

    @pl.when(t >= n_x)
    def _():
        run(k_ref[0, n:n + nc, :], v_ref[0, n:n + nc, :], None)


def _swa_call(q, k, v, sink, n, nc, with_ctx):
    b, l, _ = q.shape
    blk = SWA_BLOCK
    nt = (l if with_ctx else n) // blk
    kern = functools.partial(_swa_kernel, n=n, nc=nc)
    return pl.pallas_call(
        kern,
        grid=(b, nt),
        in_specs=[
            pl.BlockSpec((1, blk, SWA_HEADS * HEAD_DIM), lambda i, t: (i, t, 0)),
            pl.BlockSpec((1, l, 2 * LANES), lambda i, t: (i, 0, 0)),
            pl.BlockSpec((1, l, 2 * LANES), lambda i, t: (i, 0, 0)),
            pl.BlockSpec((1, LANES), lambda i, t: (0, 0)),
        ],
        out_specs=pl.BlockSpec((1, blk, SWA_HEADS * HEAD_DIM), lambda i, t: (i, t, 0)),
        out_shape=jax.ShapeDtypeStruct((b, l, SWA_HEADS * HEAD_DIM), F32),
        compiler_params=_cparams(("arbitrary", "arbitrary")),
        name="window_attention",
    )(q, k, v, sink)


def _dft_mats(n):
    r = 1 << (_log2(n) // 2)
    m = jnp.arange(n, dtype=jnp.int32)
    thin = lambda k: ((k[:, None] * m[None, :]) % (2 * n)).astype(F32) * (math.pi / n)
    a = thin(r * jnp.arange(n // r, dtype=jnp.int32))[:, None, :]
    b = thin(jnp.arange(r, dtype=jnp.int32))[None, :, :]
    cos = jnp.cos(a) * jnp.cos(b) - jnp.sin(a) * jnp.sin(b)
    sin = jnp.sin(a) * jnp.cos(b) + jnp.cos(a) * jnp.sin(b)
    return cos.reshape(n, n).astype(BF16), (-sin).reshape(n, n).astype(BF16)


def _hyena_filter_kernel(ff_ref, fb_ref, w1_ref, b1_ref, w2_ref, b2_ref, freq_ref, w3f_ref, w3b_ref,
                         dl_ref, c_ref, s_ref, kr_ref, ki_ref, kn_ref):
    n = ff_ref.shape[0]
    freq = freq_ref[...]

    def mlp(feat):
        h = jnp.sin(freq[0:1] * (_dot_f32(feat, w1_ref[...]) + b1_ref[...]))
        return jnp.sin(freq[1:2] * (_dot_f32(h, w2_ref[...]) + b2_ref[...]))

    ff = ff_ref[...]
    fb = fb_ref[...]
    dl = dl_ref[...]
    row = lax.broadcasted_iota(jnp.int32, (n, 1), 0)
    kf = _dot_f32(mlp(ff), w3f_ref[...]) * jnp.exp(-ff[:, 0:1] * dl)
    kb = _dot_f32(mlp(fb), w3b_ref[...]) * jnp.exp(-fb[:, 0:1] * dl)
    kb = jnp.where(row == 0, 0.0, kb)
    ss = jnp.sum(kf * kf, axis=0, keepdims=True) + jnp.sum(kb * kb, axis=0, keepdims=True)
    sc = lax.rsqrt(ss + EPS)
    kf = kf * sc
    kb = kb * sc
    sgn = jnp.where((row & 1) == 0, 1.0, -1.0)
    cm = c_ref[...]
    sm = s_ref[...]
    fh, fl = _split2(kf)
    bh, bl = _split2(kb)
    kr_ref[0] = _dot(cm, fh) + _dot(cm, fl) + sgn * (_dot(cm, bh) + _dot(cm, bl))
    ki_ref[0] = _dot(sm, fh) + _dot(sm, fl) + sgn * (_dot(sm, bh) + _dot(sm, bl))
    nyq = jnp.sum((kf + kb) * sgn, axis=0, keepdims=True)
    kn_ref[0] = jnp.broadcast_to(nyq, (SUBLANES, nyq.shape[-1]))


def _hyena_filter_call(featf, featb, w1, b1, w2, b2, freq, w3, deltas, cm, sm):
    n = featf.shape[0]
    hid = w2.shape[0]
    ch = deltas.shape[-1]
    tc = 2 * LANES
    nct = ch // tc
    return pl.pallas_call(
        _hyena_filter_kernel,
        grid=(2, nct),
        in_specs=[
            pl.BlockSpec((n, hid), lambda o, j: (0, 0)),
            pl.BlockSpec((n, hid), lambda o, j: (0, 0)),
            pl.BlockSpec((hid, hid), lambda o, j: (0, 0)),
            pl.BlockSpec((1, hid), lambda o, j: (0, 0)),
            pl.BlockSpec((hid, hid), lambda o, j: (0, 0)),
            pl.BlockSpec((1, hid), lambda o, j: (0, 0)),
            pl.BlockSpec((2, hid), lambda o, j: (0, 0)),
            pl.BlockSpec((hid, tc), lambda o, j: (0, (2 * o) * nct + j)),
            pl.BlockSpec((hid, tc), lambda o, j: (0, (2 * o + 1) * nct + j)),
            pl.BlockSpec((1, tc), lambda o, j: (0, j)),
            _resident((n, n)),
            _resident((n, n)),
        ],
        out_specs=[
            pl.BlockSpec((1, n, tc), lambda o, j: (o, 0, j)),
            pl.BlockSpec((1, n, tc), lambda o, j: (o, 0, j)),
            pl.BlockSpec((1, SUBLANES, tc), lambda o, j: (o, 0, j)),
        ],
        out_shape=[
            jax.ShapeDtypeStruct((2, n, ch), F32),
            jax.ShapeDtypeStruct((2, n, ch), F32),
            jax.ShapeDtypeStruct((2, SUBLANES, ch), F32),
        ],
        compiler_params=_cparams(("arbitrary", "arbitrary")),
        name="hyena_filters",
    )(featf, featb, w1, b1, w2, b2, freq, w3, w3, deltas, cm, sm)


def _hyena_kernel(*refs, n, aliased):
    if aliased:
        refs = refs[1:]
    (v_ref, x1_ref, x2_ref, cwv_ref, cw1_ref, cw2_ref, kr_ref, ki_ref, kn_ref, bias_ref, c_ref, s_ref,
     o_ref, pad_ref, z_ref, zb_ref, p_ref) = refs
    halo = SUBLANES
    tc = o_ref.shape[-1]
    rc = min(n, HY_ROW_CHUNK)
    zero_rows = jnp.zeros((halo, tc), F32)
    pad_ref[0:halo, :] = zero_rows
    pad_ref[halo + n:2 * halo + n, :] = zero_rows

    def stage(ref):
        for r in range(0, n, rc):
            pad_ref[halo + r:halo + r + rc, :] = ref[0, r:r + rc, :]

    def conv_rows(cw, r):
        return (cw[0:1] * pad_ref[halo - 1 + r:halo - 1 + r + rc, :] + cw[1:2] * pad_ref[halo + r:halo + r + rc, :]
                + cw[2:3] * pad_ref[halo + 1 + r:halo + 1 + r + rc, :])

    def sign_rows(r):
        row = r + lax.broadcasted_iota(jnp.int32, (rc, 1), 0)
        return row, jnp.where((row & 1) == 0, 1.0, -1.0)

    stage(v_ref)
    cw = cwv_ref[...]
    for r in range(0, n, rc):
        z = conv_rows(cw, r)
        z_ref[r:r + rc, :] = z
        zb_ref[r:r + rc, :] = z.astype(BF16)

    for o, (gate_ref, gate_cw_ref) in enumerate(((x1_ref, cw1_ref), (x2_ref, cw2_ref))):
        znyq = jnp.zeros((1, tc), F32)
        for r in range(0, n, rc):
            row, sgn = sign_rows(r)
            znyq = znyq + jnp.sum(z_ref[r:r + rc, :] * sgn, axis=0, keepdims=True)
            zb = zb_ref[...]
            zr = _dot(c_ref[r:r + rc, :], zb)
            zi = _dot(s_ref[r:r + rc, :], zb)
            kr = kr_ref[o, r:r + rc, :]
            ki = ki_ref[o, r:r + rc, :]
            wgt = jnp.where(row == 0, 0.5 / n, 1.0 / n)
            p_ref[0, r:r + rc, :] = ((zr * kr - zi * ki) * wgt).astype(BF16)
            p_ref[1, r:r + rc, :] = ((zr * ki + zi * kr) * wgt).astype(BF16)
        nyq = znyq * kn_ref[o, 0:1, :] * (0.5 / n)
        stage(gate_ref)
        cw = gate_cw_ref[...]
        bias = bias_ref[o:o + 1, :]
        for r in range(0, n, rc):
            _, sgn = sign_rows(r)
            y = _dot(c_ref[r:r + rc, :], p_ref[0]) + _dot(s_ref[r:r + rc, :], p_ref[1]) + sgn * nyq
            z = conv_rows(cw, r) * (y + z_ref[r:r + rc, :] * bias)
            if o == 0:
                z_ref[r:r + rc, :] = z
                zb_ref[r:r + rc, :] = z.astype(BF16)
            else:
                o_ref[0, r:r + rc, :] = z


def _hyena_call(u, conv_w, kr, ki, kn, bias, cm, sm, n, row_block, prev_out):
    b, l, _ = u.shape
    ch = bias.shape[-1]
    tc = 2 * LANES
    nct = ch // tc
    aliased = prev_out is not None
    kern = functools.partial(_hyena_kernel, n=n, aliased=aliased)
    once = pl.Buffered(1)
    in_specs = [
        pl.BlockSpec((1, n, tc), lambda j, i: (i, row_block, j)),
        pl.BlockSpec((1, n, tc), lambda j, i: (i, row_block, nct + j)),
        pl.BlockSpec((1, n, tc), lambda j, i: (i, row_block, 2 * nct + j)),
        pl.BlockSpec((3, tc), lambda j, i: (0, j)),
        pl.BlockSpec((3, tc), lambda j, i: (0, nct + j)),
        pl.BlockSpec((3, tc), lambda j, i: (0, 2 * nct + j)),
        pl.BlockSpec((2, n, tc), lambda j, i: (0, 0, j), pipeline_mode=once),
        pl.BlockSpec((2, n, tc), lambda j, i: (0, 0, j), pipeline_mode=once),
        pl.BlockSpec((2, SUBLANES, tc), lambda j, i: (0, 0, j)),
        pl.BlockSpec((2, tc), lambda j, i: (0, j)),
        _resident((n, n)),
        _resident((n, n)),
    ]
    args = [u, u, u, conv_w, conv_w, conv_w, kr, ki, kn, bias, cm, sm]
    aliases = {}
    if aliased:
        in_specs = [pl.BlockSpec(memory_space=pl.ANY)] + in_specs
        args = [prev_out] + args
        aliases = {0: 0}
    return pl.pallas_call(
        kern,
        grid=(nct, b),
        in_specs=in_specs,
        out_specs=pl.BlockSpec((1, n, tc), lambda j, i: (i, row_block, j)),
        out_shape=jax.ShapeDtypeStruct((b, l, ch), F32),
        scratch_shapes=[
            pltpu.VMEM((n + 2 * SUBLANES, tc), F32),
            pltpu.VMEM((n, tc), F32),
            pltpu.VMEM((n, tc), BF16),
            pltpu.VMEM((2, n, tc), BF16),
        ],
        input_output_aliases=aliases,
        compiler_params=_cparams(("arbitrary", "arbitrary")),
        name="hyena_conv_n%d" % n,
    )(*args)


def _rope_tables(n, nc):
    rows = n // GRID_W
    row = jnp.repeat(jnp.arange(rows, dtype=F32), GRID_W)
    col = jnp.tile(jnp.arange(GRID_W, dtype=F32), rows)
    half = HEAD_DIM // 2
    inv = ROPE_BASE ** (-jnp.arange(0, half, 2, dtype=F32) / half)
    ar = row[:, None] * inv
    ac = col[:, None] * inv
    cos = jnp.concatenate([jnp.cos(ar), jnp.cos(ar), jnp.cos(ac), jnp.cos(ac)], axis=-1)
    sin = jnp.concatenate([-jnp.sin(ar), jnp.sin(ar), -jnp.sin(ac), jnp.sin(ac)], axis=-1)
    cos = jnp.concatenate([cos, jnp.ones((nc, HEAD_DIM), F32)], axis=0)
    sin = jnp.concatenate([sin, jnp.zeros((nc, HEAD_DIM), F32)], axis=0)
    return jnp.tile(cos, (1, LANES // HEAD_DIM)), jnp.tile(sin, (1, LANES // HEAD_DIM))


def _rope_partner_cols(width):
    d = np.arange(width)
    quarter = HEAD_DIM // 4
    return np.where((d % (2 * quarter)) < quarter, d + quarter, d - quarter)


def _hyena_feats(n):
    pos = jnp.arange(n, dtype=F32)
    t = pos / max(n - 1, 1)
    ang = (2.0 * math.pi * pos / n)[:, None] * jnp.linspace(1e-4, HY_BANDS - 1, HY_BANDS, dtype=F32)[None, :]
    feats = jnp.concatenate([t[:, None], jnp.cos(ang), -jnp.sin(ang)], axis=-1)
    feats = jnp.pad(feats, ((0, 0), (0, 64 - feats.shape[-1])))
    back = jnp.concatenate([feats[0:1], jnp.flip(feats[1:], axis=0)], axis=0)
    return feats, back


def _pad_cols(w, width):
    return jnp.pad(w, ((0, 0), (0, width - w.shape[-1])))


def _layer_ab(xz, mod, norm_g0, norm_g1, w_in, w_out, conv_w, a_log, dt_bias, gdn_g, lam_p, diff_g, lam_init,
              rope, n, nc):
    hd = GDN_HEADS * GDN_DIM
    wq, wk, wv, wg = (w_in[:, i * hd:(i + 1) * hd] for i in range(4))
    o = 4 * hd
    w_beta, w_alpha = w_in[:, o:o + 16], w_in[:, o + 16:o + 32]
    o += 32
    dd = DIFF_HEADS * 2 * DIFF_DIM
    wdq, wdk, wdv = (w_in[:, o + i * dd:o + (i + 1) * dd] for i in range(3))
    pairs = GDN_HEADS // 2
    pair_cols = lambda w: [w[:, p * LANES:(p + 1) * LANES] for p in range(pairs)]
    w_qkvg = jnp.concatenate([blk for grp in zip(pair_cols(wq), pair_cols(wk), pair_cols(wv), pair_cols(wg))
                              for blk in grp], axis=1)
    perm = _rope_partner_cols(dd)
    w_all = jnp.concatenate([w_qkvg, _pad_cols(jnp.concatenate([w_beta, w_alpha], axis=1), LANES),
                             wdq, wdk, wdv, wdq[:, perm], wdk[:, perm]], axis=1).astype(BF16)
    c0 = 4 * hd
    c1 = c0 + LANES
    segs = (_Seg(0, c0), _Seg(c0, LANES),
            _Seg(c1, dd, rot_start=c1 + 3 * dd, scale=DIFF_DIM ** -0.5, dtype=BF16),
            _Seg(c1 + dd, dd, rot_start=c1 + 4 * dd, dtype=BF16),
            _Seg(c1 + 2 * dd, dd, dtype=BF16, transposed=True))
    qkvg, ba, dq, dk, dvt = _proj_call(xz, mod, norm_g0, w_all, rope[0], rope[1], segs, n // ROW_TILE, "proj_ab")

    cq, ck, cv = (conv_w[:, i * hd:(i + 1) * hd] for i in range(3))
    zeros = jnp.zeros((3, LANES), F32)
    conv_l = jnp.concatenate([blk for p in range(pairs) for blk in
                              (cq[:, p * LANES:(p + 1) * LANES], ck[:, p * LANES:(p + 1) * LANES],
                               cv[:, p * LANES:(p + 1) * LANES], zeros)], axis=1)
    n_gate = 2 * GDN_HEADS
    on_decay_lanes = lambda t: jnp.pad(t.reshape(1, n_gate), ((0, 0), (n_gate, LANES - 2 * n_gate)))
    gate_params = jnp.concatenate([on_decay_lanes(a_log), on_decay_lanes(dt_bias)], axis=0)
    ng = jnp.tile(gdn_g.reshape(1, GDN_DIM), (1, 2))
    oa = _gdn_call(qkvg, ba, conv_l, gate_params, ng, n, nc)
    ob = _diff_call(dq, dk, dvt, lam_p, diff_g, lam_init, 2 * ROW_TILE, 0, n // (2 * ROW_TILE), 0, 2, None)
    ob = _diff_call(dq, dk, dvt, lam_p, diff_g, lam_init, nc, n // nc, 1, n, 1, ob)
    l = n + nc
    return _outproj_call(oa, ob, w_out.astype(BF16), xz, mod, norm_g1, l // ROW_TILE, n // ROW_TILE)


def _layer_cd(xz, mod, norm_g0, norm_g1, w_in, w_out, sink, hy_conv, hy_w1, hy_b1, hy_w2, hy_b2, hy_w3, hy_freq,
              hy_bias, rope, n, nc, last, dft_x, dft_c):
    qd = SWA_HEADS * HEAD_DIM
    kd = SWA_KV_HEADS * HEAD_DIM
    wq, wk, wv, wu = w_in[:, 0:qd], w_in[:, qd:qd + kd], w_in[:, qd + kd:qd + 2 * kd], w_in[:, qd + 2 * kd:]
    dup = lambda w: jnp.concatenate([w[:, 0:HEAD_DIM], w[:, 0:HEAD_DIM], w[:, HEAD_DIM:], w[:, HEAD_DIM:]], axis=1)
    wk2, wv2 = dup(wk), dup(wv)
    ud = wu.shape[1]
    w_all = jnp.concatenate([wq, wk2, wv2, wu, wq[:, _rope_partner_cols(qd)], wk2[:, _rope_partner_cols(2 * kd)]],
                            axis=1).astype(BF16)
    o_u = qd + 4 * kd
    segs = (_Seg(0, qd, rot_start=o_u + ud, scale=HEAD_DIM ** -0.5, dtype=BF16),
            _Seg(qd, 2 * kd, rot_start=o_u + ud + qd, dtype=BF16),
            _Seg(qd + 2 * kd, 2 * kd, dtype=BF16), _Seg(o_u, ud))
    q, k, v, u = _proj_call(xz, mod, norm_g0, w_all, rope[0], rope[1], segs, n // ROW_TILE, "proj_cd")
    oc = _swa_call(q, k, v, _pad_cols(sink.reshape(1, SWA_HEADS), LANES), n, nc, not last)

    ch = hy_bias.shape[-1]
    deltas = jnp.abs(jnp.linspace(HY_MIN_DECAY, HY_MAX_DECAY, ch, dtype=F32)).reshape(1, ch)
    hid = hy_w2.shape[0]
    w1p = jnp.pad(hy_w1, ((0, hid - hy_w1.shape[0]), (0, 0)))
    filt = lambda m, dft: _hyena_filter_call(*_hyena_feats(m), w1p, hy_b1.reshape(1, hid), hy_w2,
                                             hy_b2.reshape(1, hid), hy_freq, hy_w3, deltas, *dft)
    kr, ki, kn = filt(n, dft_x)
    od = _hyena_call(u, hy_conv, kr, ki, kn, hy_bias, *dft_x, n, 0, None)
    if not last:
        kr, ki, kn = filt(nc, dft_c)
        od = _hyena_call(u, hy_conv, kr, ki, kn, hy_bias, *dft_c, nc, n // nc, od)
    n_tiles = (n if last else n + nc) // ROW_TILE
    return _outproj_call(oc, od, w_out.astype(BF16), xz, mod, norm_g1, n_tiles, n // ROW_TILE)


def kernel(x, c, ctx, c_ctx, w_mod, b_mod, norm_g, ffn_w_up, ffn_conv, ffn_w_down, ab_w_in, ab_w_out, gdn_conv, gdn_a_log, gdn_dt_bias, gdn_norm_g, diff_lambda, diff_norm_g, cd_w_in, cd_w_out, swa_sink, hy_conv, hy_w1, hy_b1, hy_w2, hy_b2, hy_w3, hy_freq, hy_bias):
    b, n, d = x.shape
    nc = ctx.shape[1]
    depth = w_mod.shape[0]
    assert n % ROW_TILE == 0 and nc == ROW_TILE and n % GRID_W == 0
    xz = jnp.concatenate([x, ctx], axis=1)
    rows = -(-(b + 1) // SUBLANES) * SUBLANES
    cc = jnp.concatenate([c, c_ctx[None], jnp.zeros((rows - b - 1, d), F32)], axis=0)
    mods = _mod_call(cc, w_mod, b_mod)
    rope = _rope_tables(n, nc)
    dft_x = _dft_mats(n)
    dft_c = _dft_mats(nc)
    n_x_tiles = n // ROW_TILE
    for l in range(depth):
        last = l == depth - 1
        i = l // 2
        mx = mods[l, :b].reshape(b, 1, 6, d)
        mz = jnp.broadcast_to(mods[l, b].reshape(1, 1, 6, d), (b, 1, 6, d))
        mod = jnp.concatenate([mx, mz], axis=1)
        if l % 2 == 0:
            lam_init = 0.8 - 0.6 * math.exp(-0.3 * l)
            xz = _layer_ab(xz, mod, norm_g[l, 0], norm_g[l, 1], ab_w_in[i], ab_w_out[i], gdn_conv[i], gdn_a_log[i],
                           gdn_dt_bias[i], gdn_norm_g[i], diff_lambda[i], diff_norm_g[i], lam_init, rope, n, nc)
        else:
            xz = _layer_cd(xz, mod, norm_g[l, 0], norm_g[l, 1], cd_w_in[i], cd_w_out[i], swa_sink[i], hy_conv[i],
                           hy_w1[i], hy_b1[i], hy_w2[i], hy_b2[i], hy_w3[i], hy_freq[i], hy_bias[i], rope, n, nc,
                           last, dft_x, dft_c)
        n_tiles = xz.shape[1] // ROW_TILE
        xz = _ffn_call(xz, mod, norm_g[l, 2], norm_g[l, 3], ffn_w_up[l].astype(BF16), ffn_conv[l],
                       ffn_w_down[l].astype(BF16), n_tiles, n_x_tiles)
    return xz[:, :n] if xz.shape[1] != n else xz
```

```python
import functools
import math
from typing import NamedTuple, Optional

import jax
import jax.numpy as jnp
import numpy as np
from jax import lax
from jax.experimental import pallas as pl
from jax.experimental.pallas import tpu as pltpu

F32 = jnp.float32
BF16 = jnp.bfloat16
MIXER_OUT_DTYPE = BF16

EPS = 1e-6
NEG_INF = -1e30
GRID_W = 64
HEAD_DIM = 64
ROPE_BASE = 10000.0
GDN_HEADS = 8
GDN_DIM = 64
GDN_CHUNK = 64
DIFF_HEADS = 4
DIFF_DIM = 64
SWA_HEADS = 8
SWA_KV_HEADS = 2
SWA_WINDOW = 128
SWA_BLOCK = 128
HY_BANDS = 16
HY_MIN_DECAY = math.log(1e-2) / 1.5
HY_MAX_DECAY = math.log(1e-2) / 0.3
HY_ROW_CHUNK = 512

LANES = 128
SUBLANES = 8
MXU_WIDTH = 256
FFN_COL_CHUNK = 6 * MXU_WIDTH
ROW_TILE = 256
VMEM_LIMIT = 56 * 1024 * 1024


def _cparams(sem):
    return pltpu.CompilerParams(dimension_semantics=sem, vmem_limit_bytes=VMEM_LIMIT)


def _resident(shape):
    zeros = (0,) * len(shape)
    return pl.BlockSpec(shape, lambda *_: zeros, pipeline_mode=pl.Buffered(1))


def _log2(v):
    assert v & (v - 1) == 0
    return v.bit_length() - 1


def _sigmoid(x):
    return 1.0 / (1.0 + jnp.exp(-x))


def _silu(x):
    return x * _sigmoid(x)


def _softplus(x):
    return jnp.maximum(x, 0.0) + jnp.log1p(jnp.exp(-jnp.abs(x)))


def _dot(a, b):
    return jnp.dot(a, b, preferred_element_type=F32)


def _dot_nt(a, b):
    return lax.dot_general(a, b, (((1,), (1,)), ((), ())), preferred_element_type=F32)


def _dot_tn(a, b):
    return lax.dot_general(a, b, (((0,), (0,)), ((), ())), preferred_element_type=F32)


def _dot_f32(a, b):
    return jnp.dot(a, b, preferred_element_type=F32, precision=lax.Precision.HIGHEST)


def _split2(x):
    hi = x.astype(BF16)
    lo = (x - hi.astype(F32)).astype(BF16)
    return hi, lo


def _split3(x):
    hi = x.astype(BF16)
    r = x - hi.astype(F32)
    mid = r.astype(BF16)
    lo = (r - mid.astype(F32)).astype(BF16)
    return hi, mid, lo


def _dot_sel(x, sel_bf16):
    hi, mid, lo = _split3(x)
    return _dot(hi, sel_bf16) + _dot(mid, sel_bf16) + _dot(lo, sel_bf16)


def _mm(a, b):
    return _dot(a.astype(BF16), b.astype(BF16))


def _rms(y, g):
    return y * lax.rsqrt(jnp.mean(y * y, axis=-1, keepdims=True) + EPS) * g


def _modnorm(x, g, shift, scale):
    return _rms(x, g) * (1.0 + scale) + shift


def _mod_kernel(cc_ref, w_ref, b_ref, o_ref):
    s = _silu(cc_ref[...])
    o_ref[0] = _dot(s.astype(BF16), w_ref[0].astype(BF16)) + b_ref[0]


def _mod_call(cc, w_mod, b_mod):
    depth, d, nm = w_mod.shape
    rows = cc.shape[0]
    ct = 1536
    return pl.pallas_call(
        _mod_kernel,
        grid=(depth, nm // ct),
        in_specs=[
            pl.BlockSpec((rows, d), lambda l, j: (0, 0)),
            pl.BlockSpec((1, d, ct), lambda l, j: (l, 0, j)),
            pl.BlockSpec((1, 1, ct), lambda l, j: (l, 0, j)),
        ],
        out_specs=pl.BlockSpec((1, rows, ct), lambda l, j: (l, 0, j)),
        out_shape=jax.ShapeDtypeStruct((depth, rows, nm), F32),
        compiler_params=_cparams(("arbitrary", "arbitrary")),
        name="adaln_mod",
    )(cc, w_mod, b_mod.reshape(depth, 1, nm))


class _Seg(NamedTuple):
    start: int
    width: int
    rot_start: Optional[int] = None
    scale: float = 1.0
    dtype: type = F32
    transposed: bool = False


def _proj_kernel(x_ref, mod_ref, g_ref, w_ref, cos_ref, sin_ref, *out_refs, segs):
    m = mod_ref[0, 0]
    h = _modnorm(x_ref[0], g_ref[...], m[0:1], m[1:2]).astype(BF16)
    for o_ref, seg in zip(out_refs, segs):
        y = _dot(h, w_ref[:, seg.start:seg.start + seg.width])
        if seg.rot_start is not None:
            yr = _dot(h, w_ref[:, seg.rot_start:seg.rot_start + seg.width])
            reps = seg.width // LANES
            cos = jnp.concatenate([cos_ref[...]] * reps, axis=1)
            sin = jnp.concatenate([sin_ref[...]] * reps, axis=1)
            y = y * cos + yr * sin
        if seg.scale != 1.0:
            y = y * seg.scale
        if seg.transposed:
            y = y.T
        o_ref[0] = y.astype(seg.dtype)


def _proj_call(xz, mod, g, w, cos_t, sin_t, segs, n_x_tiles, name):
    b, l, d = xz.shape
    tm = ROW_TILE
    nt = l // tm
    p = w.shape[1]
    return pl.pallas_call(
        functools.partial(_proj_kernel, segs=segs),
        grid=(nt, b),
        in_specs=[
            pl.BlockSpec((1, tm, d), lambda t, i: (i, t, 0)),
            pl.BlockSpec((1, 1, 6, d), lambda t, i: (i, t // n_x_tiles, 0, 0)),
            pl.BlockSpec((1, d), lambda t, i: (0, 0)),
            _resident((d, p)),
            pl.BlockSpec((tm, LANES), lambda t, i: (t, 0)),
            pl.BlockSpec((tm, LANES), lambda t, i: (t, 0)),
        ],
        out_specs=[pl.BlockSpec((1, s.width, tm), lambda t, i: (i, 0, t)) if s.transposed
                   else pl.BlockSpec((1, tm, s.width), lambda t, i: (i, t, 0)) for s in segs],
        out_shape=[jax.ShapeDtypeStruct((b, s.width, l) if s.transposed else (b, l, s.width), s.dtype)
                   for s in segs],
        compiler_params=_cparams(("arbitrary", "arbitrary")),
        name=name,
    )(xz, mod, g.reshape(1, d), w, cos_t, sin_t)


def _post_kernel(o1p_ref, o1_ref, o1n_ref, o2p_ref, o2_ref, o2n_ref, xp_ref, x_ref, xn_ref, mod_ref,
                 g1_ref, g2_ref, g3_ref, wout_ref, wup_ref, cw_ref, wdn_ref, out_ref, up_ref,
                 *, tm, n_x_tiles, n_tiles, cf, dff):
    t = pl.program_id(0)
    first = jnp.logical_or(t == 0, t == n_x_tiles)
    last = jnp.logical_or(t == n_x_tiles - 1, t == n_tiles - 1)
    m = mod_ref[0, 0]
    halo = SUBLANES
    ohalo = o1p_ref.shape[1]
    k1 = o1_ref.shape[-1]
    o1e = jnp.concatenate([o1p_ref[0], o1_ref[0], o1n_ref[0]], axis=0)
    o2e = jnp.concatenate([o2p_ref[0], o2_ref[0], o2n_ref[0]], axis=0)
    y = _dot(o1e, wout_ref[0:k1, :]) + _dot(o2e, wout_ref[k1:, :])
    y = y[ohalo - halo:ohalo + tm + halo]
    xe = jnp.concatenate([xp_ref[0], x_ref[0], xn_ref[0]], axis=0)
    x1 = xe + m[2:3] * _rms(y, g1_ref[...])
    h = _modnorm(x1, g2_ref[...], m[3:4], m[4:5]).astype(BF16)
    acc = jnp.zeros((tm, x_ref.shape[-1]), F32)
    for c0 in range(0, dff, cf):
        wd = min(cf, dff - c0)
        halves = []
        for half, base in enumerate((c0, dff + c0)):
            u = _dot(h, wup_ref[:, base:base + wd])
            up_ref[half, :, 0:wd] = u
            up_ref[half, 0:halo, 0:wd] = jnp.where(first, 0.0, u[0:halo])
            up_ref[half, tm + halo:tm + 2 * halo, 0:wd] = jnp.where(last, 0.0, u[tm + halo:])
            cw = cw_ref[:, base:base + wd]
            halves.append(cw[0:1] * up_ref[half, halo - 1:halo - 1 + tm, 0:wd]
                          + cw[1:2] * up_ref[half, halo:halo + tm, 0:wd]
                          + cw[2:3] * up_ref[half, halo + 1:halo + 1 + tm, 0:wd])
        act = (_silu(halves[1]) * halves[0]).astype(BF16)
        acc = acc + _dot(act, wdn_ref[c0:c0 + wd, :])
    out_ref[0] = x1[halo:halo + tm] + m[5:6] * _rms(acc, g3_ref[...])


def _post_call(o1, o2, xz, mod, g1, g2, g3, w_out, w_up, conv_w, w_down, n_tiles, n_x_tiles):
    b, _, d = xz.shape
    tm = ROW_TILE
    rows = n_tiles * tm
    dff = w_down.shape[0]
    cf = FFN_COL_CHUNK
    k1, k2 = o1.shape[-1], o2.shape[-1]
    ohalo = 2 * SUBLANES
    kern = functools.partial(_post_kernel, tm=tm, n_x_tiles=n_x_tiles, n_tiles=n_tiles, cf=cf, dff=dff)

    def with_halos(width, halo_rows):
        per_tile = tm // halo_rows
        n_blocks = rows // halo_rows
        return [
            pl.BlockSpec((1, halo_rows, width), lambda t, i: (i, jnp.maximum(t * per_tile - 1, 0), 0)),
            pl.BlockSpec((1, tm, width), lambda t, i: (i, t, 0)),
            pl.BlockSpec((1, halo_rows, width), lambda t, i: (i, jnp.minimum((t + 1) * per_tile, n_blocks - 1), 0)),
        ]

    row_vec = pl.BlockSpec((1, d), lambda t, i: (0, 0))
    return pl.pallas_call(
        kern,
        grid=(n_tiles, b),
        in_specs=with_halos(k1, ohalo) + with_halos(k2, ohalo) + with_halos(d, SUBLANES) + [
            pl.BlockSpec((1, 1, 6, d), lambda t, i: (i, t // n_x_tiles, 0, 0)),
            row_vec, row_vec, row_vec,
            _resident((k1 + k2, d)),
            _resident((d, 2 * dff)),
            pl.BlockSpec((3, 2 * dff), lambda t, i: (0, 0)),
            _resident((dff, d)),
        ],
        out_specs=pl.BlockSpec((1, tm, d), lambda t, i: (i, t, 0)),
        out_shape=jax.ShapeDtypeStruct((b, rows, d), F32),
        scratch_shapes=[pltpu.VMEM((2, tm + 2 * SUBLANES, cf), F32)],
        compiler_params=_cparams(("arbitrary", "arbitrary")),
        name="mixer_out_conv_ffn",
    )(o1, o1, o1, o2, o2, o2, xz, xz, xz, mod, g1.reshape(1, d), g2.reshape(1, d), g3.reshape(1, d),
      w_out, w_up, conv_w, w_down)


def _half_sums(x2, lane_lo):
    s0 = jnp.sum(jnp.where(lane_lo, x2, 0.0), axis=-1, keepdims=True)
    s1 = jnp.sum(jnp.where(lane_lo, 0.0, x2), axis=-1, keepdims=True)
    return jnp.where(lane_lo, s0, s1)


def _gdn_kernel(qkvg_ref, ba_ref, cw_ref, gp_ref, ng_ref, out_ref,
                pad_ref, q_ref, k_ref, v_ref, bb_ref, gb_ref, qe_ref, mp_ref, ou_ref, nn_ref, egl_ref, o_ref,
                *, n, nc, chunks_per_iter):
    l = n + nc
    c = GDN_CHUNK
    n_chunks = l // c
    pair = pl.program_id(1)
    halo = SUBLANES
    lane = lax.broadcasted_iota(jnp.int32, (1, LANES), 1)
    lane_lo = lane < GDN_DIM

    cw = cw_ref[:, 0:3 * LANES]
    zero_rows = jnp.zeros((halo, 3 * LANES), F32)
    for seq_start, seq_len in ((0, n), (n, nc)):
        base = halo + seq_start + (2 * halo if seq_start else 0)
        pad_ref[base - halo:base, :] = zero_rows
        pad_ref[base + seq_len:base + seq_len + halo, :] = zero_rows
        step = 256
        for r in range(0, seq_len, step):
            pad_ref[base + r:base + r + step, :] = qkvg_ref[0, seq_start + r:seq_start + r + step, 0:3 * LANES]
        for r in range(0, seq_len, step):
            y = (cw[0:1] * pad_ref[base + r - 1:base + r - 1 + step, :]
                 + cw[1:2] * pad_ref[base + r:base + r + step, :]
                 + cw[2:3] * pad_ref[base + r + 1:base + r + 1 + step, :])
            y = _silu(y)
            q = y[:, 0:LANES]
            k = y[:, LANES:2 * LANES]
            rows = slice(seq_start + r, seq_start + r + step)
            q_ref[rows, :] = q * lax.rsqrt(_half_sums(q * q, lane_lo) + EPS) * (GDN_DIM ** -0.5)
            k_ref[rows, :] = k * lax.rsqrt(_half_sums(k * k, lane_lo) + EPS)
            v_ref[rows, :] = y[:, 2 * LANES:3 * LANES]

    sel_r = lax.broadcasted_iota(jnp.int32, (LANES, 4 * LANES), 0)
    sel_c = lax.broadcasted_iota(jnp.int32, (LANES, 4 * LANES), 1)
    quarter = sel_c >> _log2(LANES)
    src_lane = (quarter & 1) * 2 * GDN_HEADS + (quarter >> 1) * GDN_HEADS + 2 * pair + ((sel_c >> _log2(GDN_DIM)) & 1)
    sel = (sel_r == src_lane).astype(BF16)
    gblk = 256
    bi = lax.broadcasted_iota(jnp.int32, (gblk, gblk), 0)
    bj = lax.broadcasted_iota(jnp.int32, (gblk, gblk), 1)
    same_chunk = (bi >> _log2(c)) == (bj >> _log2(c))
    csum = (jnp.logical_and(same_chunk, bi >= bj).astype(BF16), jnp.logical_and(same_chunk, bi <= bj).astype(BF16))
    neg_a = -jnp.exp(gp_ref[0:1, :])
    dt_bias = gp_ref[1:2, :]
    for r in range(0, l, gblk):
        ba = ba_ref[0, r:r + gblk, :]
        gates = jnp.where(lane < 2 * GDN_HEADS, _sigmoid(ba), neg_a * _softplus(ba + dt_bias))
        x = _dot_sel(gates, sel)
        for d in range(2):
            bb_ref[d, r:r + gblk, :] = x[:, 2 * d * LANES:(2 * d + 1) * LANES]
            gb_ref[d, r:r + gblk, :] = _dot_sel_lhs(csum[d], x[:, (2 * d + 1) * LANES:(2 * d + 2) * LANES])

    r2 = lax.broadcasted_iota(jnp.int32, (2 * c, 2 * c), 0)
    c2 = lax.broadcasted_iota(jnp.int32, (2 * c, 2 * c), 1)
    same_head = (r2 >= c) == (c2 >= c)
    eye = (r2 == c2).astype(F32)
    masks = ((jnp.logical_and(same_head, r2 >= c2), jnp.logical_and(same_head, r2 > c2)),
             (jnp.logical_and(same_head, r2 <= c2), jnp.logical_and(same_head, r2 < c2)))
    m0 = lane_lo.astype(F32)
    m1 = 1.0 - m0

    def pair_mask(lv, lower):
        same_block = (r2 >> (lv + 1)) == (c2 >> (lv + 1))
        r_hi = ((r2 >> lv) & 1) == 1
        c_hi = ((c2 >> lv) & 1) == 1
        off = jnp.logical_and(r_hi, jnp.logical_not(c_hi)) if lower else jnp.logical_and(c_hi, jnp.logical_not(r_hi))
        return jnp.logical_and(same_block, off)

    pair_masks = tuple(tuple(pair_mask(lv, lower) for lv in range(_log2(c))) for lower in (True, False))

    def stack_heads(x2):
        return jnp.concatenate([x2 * m0, x2 * m1], axis=0)

    def fold_heads(x):
        return x[0:c] + x[c:2 * c]

    def chunks_local(dirs, qs, ks, vs, betas, gcs):
        each = lambda f, *cols: [f(*args) for args in zip(*cols)]
        incl = [masks[d][0] for d in dirs]
        strict = [masks[d][1] for d in dirs]
        g1 = each(lambda gc2: jnp.concatenate([gc2, gc2], axis=0), gcs)
        decay = each(lambda g, m: jnp.where(m, jnp.exp(jnp.where(m, g - g.T, 0.0)), 0.0), g1, incl)
        kb = each(lambda k, b: k * b, ks, betas)
        kst = each(lambda k: stack_heads(k).astype(BF16), ks)
        a_raw = each(lambda x, y: _dot_nt(stack_heads(x).astype(BF16), y), kb, kst)
        qk_raw = each(lambda x, y: _dot_nt(stack_heads(x).astype(BF16), y), qs, kst)
        qk = each(lambda m, x, dc: jnp.where(m, x * dc, 0.0).astype(BF16), incl, qk_raw, decay)
        a = each(lambda m, x, dc: jnp.where(m, x * dc, 0.0), strict, a_raw, decay)
        tinv = each(lambda d, x: eye - jnp.where(pair_masks[d][0], x, 0.0), dirs, a)
        for lv in range(1, _log2(c)):
            ta = each(lambda d, t, x: _mm(t, jnp.where(pair_masks[d][lv], x, 0.0)), dirs, tinv, a)
            tat = each(_mm, ta, tinv)
            tinv = each(lambda t, x: t - x, tinv, tat)
        egc = each(jnp.exp, gcs)
        rhs = each(lambda v, b, x, e: jnp.concatenate([stack_heads(v * b), stack_heads(x * e)], axis=1),
                   vs, betas, kb, egc)
        sol = each(_mm, tinv, rhs)
        u2 = each(lambda x: fold_heads(x[:, 0:LANES]), sol)
        w2 = each(lambda x: fold_heads(x[:, LANES:2 * LANES]), sol)
        gl = each(lambda d, gc2: gc2[c - 1:c, :] if d == 0 else gc2[0:1, :], dirs, gcs)
        ktail = each(lambda k, g, gc2: (k * jnp.exp(g - gc2)).astype(BF16), ks, gl, gcs)
        qwu = each(lambda x, w, u: _dot(x, jnp.concatenate([stack_heads(w), stack_heads(u)], axis=1).astype(BF16)),
                   qk, w2, u2)
        kwu = each(lambda x, w, u: _dot_tn(x, jnp.concatenate([w, u], axis=1).astype(BF16)), ktail, w2, u2)
        q_eff = each(lambda q, e, x: (q * e - fold_heads(x[:, 0:LANES])).astype(BF16), qs, egc, qwu)
        m_neg = each(lambda x: jnp.where(same_head, -x[:, 0:LANES], 0.0).astype(BF16), kwu)
        o_loc = each(lambda x: fold_heads(x[:, LANES:2 * LANES]), qwu)
        s_loc = each(lambda x: jnp.where(same_head, x[:, LANES:2 * LANES], 0.0), kwu)
        egl = each(lambda g: jnp.broadcast_to(jnp.exp(g), (SUBLANES, LANES)), gl)
        return q_eff, m_neg, o_loc, s_loc, egl

    def chunk_rows(chunk, rows_per_chunk):
        return pl.ds(pl.multiple_of(chunk * rows_per_chunk, rows_per_chunk), rows_per_chunk)

    def local_body(it, carry):
        dirs, chunks, qs, ks, vs, betas, gcs = [], [], [], [], [], [], []
        for g in range(chunks_per_iter):
            chunk = it + g * (n_chunks // chunks_per_iter)
            rows = chunk_rows(chunk, c)
            for d in range(2):
                dirs.append(d)
                chunks.append(chunk)
                qs.append(q_ref[rows, :])
                ks.append(k_ref[rows, :])
                vs.append(v_ref[rows, :])
                betas.append(bb_ref[d, rows, :])
                gcs.append(gb_ref[d, rows, :])
        results = chunks_local(dirs, qs, ks, vs, betas, gcs)
        for d, chunk, q_eff, m_neg, o_loc, s_loc, egl in zip(dirs, chunks, *results):
            qe_ref[d, chunk_rows(chunk, c), :] = q_eff
            mp_ref[d, chunk_rows(chunk, 2 * c), :] = m_neg
            ou_ref[d, chunk_rows(chunk, c), :] = o_loc
            nn_ref[d, chunk_rows(chunk, 2 * c), :] = s_loc
            egl_ref[d, chunk_rows(chunk, SUBLANES), :] = egl
        return carry

    lax.fori_loop(0, n_chunks // chunks_per_iter, local_body, 0)

    ctx_chunks = nc // c

    def scan_body(s, carry):
        chunks = (jnp.where(s < ctx_chunks, s + n // c, s - ctx_chunks), n_chunks - 1 - s)
        loaded = [(qe_ref[d, chunk_rows(ch, c), :], mp_ref[d, chunk_rows(ch, 2 * c), :],
                   ou_ref[d, chunk_rows(ch, c), :], nn_ref[d, chunk_rows(ch, 2 * c), :],
                   egl_ref[d, chunk_rows(ch, SUBLANES), :]) for d, ch in enumerate(chunks)]
        res = [_dot(jnp.concatenate([ld[0], ld[1]], axis=0), s2.astype(BF16)) for ld, s2 in zip(loaded, carry)]
        for d, ch in enumerate(chunks):
            o_ref[d, chunk_rows(ch, c), :] = res[d][0:c] + loaded[d][2]
        return tuple(s2 * ld[4][0:1] + r[c:3 * c] + ld[3] for s2, ld, r in zip(carry, loaded, res))

    zero_state = jnp.zeros((2 * c, 2 * c), F32)
    lax.fori_loop(0, n_chunks, scan_body, (zero_state, zero_state))

    ng = ng_ref[...]
    step = 256
    for r in range(0, l, step):
        o = o_ref[0, r:r + step, :] + o_ref[1, r:r + step, :]
        ms = _half_sums(o * o, lane_lo) * (1.0 / GDN_DIM)
        gate = qkvg_ref[0, r:r + step, 3 * LANES:4 * LANES]
        out_ref[0, r:r + step, :] = (o * lax.rsqrt(ms + EPS) * ng * _silu(gate)).astype(out_ref.dtype)


def _dot_sel_lhs(sel_bf16, x):
    hi, mid, lo = _split3(x)
    return _dot(sel_bf16, hi) + _dot(sel_bf16, mid) + _dot(sel_bf16, lo)


def _gdn_call(qkvg, ba, conv_w, gate_params, ng, n, nc):
    b, l, _ = qkvg.shape
    pairs = GDN_HEADS // 2
    n_chunks = l // GDN_CHUNK
    kern = functools.partial(_gdn_kernel, n=n, nc=nc, chunks_per_iter=4)
    return pl.pallas_call(
        kern,
        grid=(b, pairs),
        in_specs=[
            pl.BlockSpec((1, l, 4 * LANES), lambda i, p: (i, 0, p)),
            pl.BlockSpec((1, l, LANES), lambda i, p: (i, 0, 0)),
            pl.BlockSpec((3, 4 * LANES), lambda i, p: (0, p)),
            pl.BlockSpec((2, LANES), lambda i, p: (0, 0)),
            pl.BlockSpec((1, LANES), lambda i, p: (0, 0)),
        ],
        out_specs=pl.BlockSpec((1, l, LANES), lambda i, p: (i, 0, p)),
        out_shape=jax.ShapeDtypeStruct((b, l, pairs * LANES), MIXER_OUT_DTYPE),
        scratch_shapes=[
            pltpu.VMEM((l + 5 * SUBLANES, 3 * LANES), F32),
            pltpu.VMEM((l, LANES), F32),
            pltpu.VMEM((l, LANES), F32),
            pltpu.VMEM((l, LANES), F32),
            pltpu.VMEM((2, l, LANES), F32),
            pltpu.VMEM((2, l, LANES), F32),
            pltpu.VMEM((2, l, LANES), BF16),
            pltpu.VMEM((2, 2 * l, LANES), BF16),
            pltpu.VMEM((2, l, LANES), F32),
            pltpu.VMEM((2, 2 * l, LANES), F32),
            pltpu.VMEM((2, n_chunks * SUBLANES, LANES), F32),
            pltpu.VMEM((2, l, LANES), F32),
        ],
        compiler_params=_cparams(("arbitrary", "arbitrary")),
        name="gated_deltanet",
    )(qkvg, ba, conv_w, gate_params, ng)


def _diff_kernel(*refs, key_start, n_sub, lam_init, aliased):
    if aliased:
        refs = refs[1:]
    q_ref, k_ref, vt_ref, lam_ref, ng_ref, o_ref = refs
    lp = lam_ref[...]
    lam = (jnp.exp(jnp.sum(lp[0:1] * lp[1:2], axis=-1, keepdims=True))
           - jnp.exp(jnp.sum(lp[2:3] * lp[3:4], axis=-1, keepdims=True)) + lam_init)
    lane = lax.broadcasted_iota(jnp.int32, (1, LANES), 1)
    halves = (lane < DIFF_DIM, lane >= DIFF_DIM)
    ng = ng_ref[...]
    k = k_ref[0, key_start:, :]
    vt = vt_ref[0, :, key_start:]
    tq = q_ref.shape[1] // n_sub
    scores = []
    for i in range(n_sub):
        q = q_ref[0, i * tq:(i + 1) * tq, :]
        scores.append([_dot_nt(k, jnp.where(m, q, jnp.zeros_like(q))) for m in halves])
    for i, s in enumerate(scores):
        e = [jnp.exp(x - jnp.max(x, axis=0, keepdims=True)) for x in s]
        pv = [_dot(vt, x.astype(BF16)) for x in e]
        parts = [x * (1.0 / jnp.sum(y, axis=0, keepdims=True)) for x, y in zip(pv, e)]
        ot = parts[0] - lam * parts[1]
        ot = ot * lax.rsqrt(jnp.mean(ot * ot, axis=0, keepdims=True) + EPS)
        o_ref[0, i * tq:(i + 1) * tq, :] = (ot.T * ng * (1.0 - lam_init)).astype(o_ref.dtype)


def _diff_call(dq, dk, dvt, lam_p, ng, lam_init, q_rows, first_block, n_q_blocks, key_start, n_sub, prev_out):
    b, l, _ = dq.shape
    aliased = prev_out is not None
    kern = functools.partial(_diff_kernel, key_start=key_start, n_sub=n_sub, lam_init=lam_init, aliased=aliased)
    row_of = lambda t: first_block + t
    in_specs = [
        pl.BlockSpec((1, q_rows, LANES), lambda i, h, t: (i, row_of(t), h)),
        pl.BlockSpec((1, l, LANES), lambda i, h, t: (i, 0, h)),
        pl.BlockSpec((1, LANES, l), lambda i, h, t: (i, h, 0)),
        pl.BlockSpec((4, DIFF_DIM), lambda i, h, t: (0, 0)),
        pl.BlockSpec((1, LANES), lambda i, h, t: (0, 0)),
    ]
    args = [dq, dk, dvt, lam_p, ng.reshape(1, LANES)]
    aliases = {}
    if aliased:
        in_specs = [pl.BlockSpec(memory_space=pl.ANY)] + in_specs
        args = [prev_out] + args
        aliases = {0: 0}
    return pl.pallas_call(
        kern,
        grid=(b, DIFF_HEADS, n_q_blocks),
        in_specs=in_specs,
        out_specs=pl.BlockSpec((1, q_rows, LANES), lambda i, h, t: (i, row_of(t), h)),
        out_shape=jax.ShapeDtypeStruct((b, l, DIFF_HEADS * LANES), MIXER_OUT_DTYPE),
        input_output_aliases=aliases,
        compiler_params=_cparams(("arbitrary", "arbitrary", "arbitrary")),
        name="diff_attention_ctx" if aliased else "diff_attention",
    )(*args)


def _swa_kernel(q_ref, k_ref, v_ref, sink_ref, o_ref, *, n, nc):
    t = pl.program_id(1)
    blk = SWA_BLOCK
    n_x = n // blk
    q = q_ref[0]
    lane = lax.broadcasted_iota(jnp.int32, (1, LANES), 1)
    lane_lo = lane < HEAD_DIM
    sink = sink_ref[...]
    group = SWA_HEADS // SWA_KV_HEADS

    def run(keys, vals, valid):
        head_of_row = lax.broadcasted_iota(jnp.int32, (group * blk, 1), 0) >> _log2(blk)
        kvs = range(SWA_KV_HEADS)
        kk = [keys[:, kvh * LANES:(kvh + 1) * LANES] for kvh in kvs]
        vv = [vals[:, kvh * LANES:(kvh + 1) * LANES] for kvh in kvs]
        qst, sk = [], []
        for kvh in kvs:
            q_rows = []
            sk_rows = jnp.zeros((group * blk, 1), F32)
            for g in range(group):
                h = kvh * group + g
                qp = q[:, (h // 2) * LANES:(h // 2 + 1) * LANES]
                q_rows.append(jnp.where(lane_lo if h % 2 == 0 else jnp.logical_not(lane_lo), qp, jnp.zeros_like(qp)))
                sk_rows = jnp.where(head_of_row == g, sink[:, h:h + 1], sk_rows)
            qst.append(jnp.concatenate(q_rows, axis=0))
            sk.append(sk_rows)
        s = [_dot_nt(x, y) for x, y in zip(qst, kk)]
        if valid is not None:
            s = [jnp.where(valid, x, NEG_INF) for x in s]
        mx = [jnp.maximum(jnp.max(x, axis=-1, keepdims=True), y) for x, y in zip(s, sk)]
        e = [jnp.exp(x - m) for x, m in zip(s, mx)]
        pv = [_dot(x.astype(BF16), y) for x, y in zip(e, vv)]
        den = [jnp.sum(x, axis=-1, keepdims=True) + jnp.exp(y - m) for x, y, m in zip(e, sk, mx)]
        outs = []
        for o, dn in zip(pv, den):
            o = o * (1.0 / dn)
            for g in range(0, group, 2):
                outs.append(jnp.where(lane_lo, o[g * blk:(g + 1) * blk], o[(g + 1) * blk:(g + 2) * blk]))
        o_ref[0] = jnp.concatenate(outs, axis=1).astype(o_ref.dtype)

    @pl.when(t < n_x)
    def _():
        start = pl.multiple_of(jnp.clip((t - 1) * blk, 0, n - 3 * blk), blk)
        keys = jnp.concatenate([k_ref[0, pl.ds(start, 3 * blk), :], k_ref[0, n:n + nc, :]], axis=0)
        vals = jnp.concatenate([v_ref[0, pl.ds(start, 3 * blk), :], v_ref[0, n:n + nc, :]], axis=0)
        shape = (group * blk, 3 * blk + nc)
        qpos = t * blk + (lax.broadcasted_iota(jnp.int32, shape, 0) & (blk - 1))
        col = lax.broadcasted_iota(jnp.int32, shape, 1)
        dist = qpos - (start + col)
        in_window = jnp.logical_and(dist <= SWA_WINDOW, dist >= -SWA_WINDOW)
        valid = jnp.logical_or(col >= 3 * blk, in_window)
        run(keys, vals, valid)

    @pl.when(t >= n_x)
    def _():
        run(k_ref[0, n:n + nc, :], v_ref[0, n:n + nc, :], None)


def _swa_call(q, k, v, sink, n, nc, with_ctx):
    b, l, _ = q.shape
    blk = SWA_BLOCK
    nt = (l if with_ctx else n) // blk
    kern = functools.partial(_swa_kernel, n=n, nc=nc)
    return pl.pallas_call(
        kern,
        grid=(b, nt),
        in_specs=[
            pl.BlockSpec((1, blk, SWA_HEADS * HEAD_DIM), lambda i, t: (i, t, 0)),
            pl.BlockSpec((1, l, 2 * LANES), lambda i, t: (i, 0, 0)),
            pl.BlockSpec((1, l, 2 * LANES), lambda i, t: (i, 0, 0)),
            pl.BlockSpec((1, LANES), lambda i, t: (0, 0)),
        ],
        out_specs=pl.BlockSpec((1, blk, SWA_HEADS * HEAD_DIM), lambda i, t: (i, t, 0)),
        out_shape=jax.ShapeDtypeStruct((b, l, SWA_HEADS * HEAD_DIM), MIXER_OUT_DTYPE),
        compiler_params=_cparams(("arbitrary", "arbitrary")),
        name="window_attention",
    )(q, k, v, sink)


def _dft_mats(n):
    r = 1 << (_log2(n) // 2)
    m = jnp.arange(n, dtype=jnp.int32)
    thin = lambda k: ((k[:, None] * m[None, :]) % (2 * n)).astype(F32) * (math.pi / n)
    a = thin(r * jnp.arange(n // r, dtype=jnp.int32))[:, None, :]
    b = thin(jnp.arange(r, dtype=jnp.int32))[None, :, :]
    cos = jnp.cos(a) * jnp.cos(b) - jnp.sin(a) * jnp.sin(b)
    sin = jnp.sin(a) * jnp.cos(b) + jnp.cos(a) * jnp.sin(b)
    return cos.reshape(n, n).astype(BF16), (-sin).reshape(n, n).astype(BF16)


def _hyena_filter_kernel(ff_ref, fb_ref, w1_ref, b1_ref, w2_ref, b2_ref, freq_ref, w3f_ref, w3b_ref,
                         dl_ref, c_ref, s_ref, kr_ref, ki_ref, kn_ref):
    n = ff_ref.shape[0]
    freq = freq_ref[...]

    def mlp(feat):
        h = jnp.sin(freq[0:1] * (_dot_f32(feat, w1_ref[...]) + b1_ref[...]))
        return jnp.sin(freq[1:2] * (_dot_f32(h, w2_ref[...]) + b2_ref[...]))

    ff = ff_ref[...]
    fb = fb_ref[...]
    dl = dl_ref[...]
    row = lax.broadcasted_iota(jnp.int32, (n, 1), 0)
    kf = _dot_f32(mlp(ff), w3f_ref[...]) * jnp.exp(-ff[:, 0:1] * dl)
    kb = _dot_f32(mlp(fb), w3b_ref[...]) * jnp.exp(-fb[:, 0:1] * dl)
    kb = jnp.where(row == 0, 0.0, kb)
    ss = jnp.sum(kf * kf, axis=0, keepdims=True) + jnp.sum(kb * kb, axis=0, keepdims=True)
    sc = lax.rsqrt(ss + EPS)
    kf = kf * sc
    kb = kb * sc
    sgn = jnp.where((row & 1) == 0, 1.0, -1.0)
    cm = c_ref[...]
    sm = s_ref[...]
    fh, fl = _split2(kf)
    bh, bl = _split2(kb)
    kr_ref[0] = _dot(cm, fh) + _dot(cm, fl) + sgn * (_dot(cm, bh) + _dot(cm, bl))
    ki_ref[0] = _dot(sm, fh) + _dot(sm, fl) + sgn * (_dot(sm, bh) + _dot(sm, bl))
    nyq = jnp.sum((kf + kb) * sgn, axis=0, keepdims=True)
    kn_ref[0] = jnp.broadcast_to(nyq, (SUBLANES, nyq.shape[-1]))


def _hyena_filter_call(featf, featb, w1, b1, w2, b2, freq, w3, deltas, cm, sm):
    n = featf.shape[0]
    hid = w2.shape[0]
    ch = deltas.shape[-1]
    tc = 2 * LANES
    nct = ch // tc
    return pl.pallas_call(
        _hyena_filter_kernel,
        grid=(2, nct),
        in_specs=[
            pl.BlockSpec((n, hid), lambda o, j: (0, 0)),
            pl.BlockSpec((n, hid), lambda o, j: (0, 0)),
            pl.BlockSpec((hid, hid), lambda o, j: (0, 0)),
            pl.BlockSpec((1, hid), lambda o, j: (0, 0)),
            pl.BlockSpec((hid, hid), lambda o, j: (0, 0)),
            pl.BlockSpec((1, hid), lambda o, j: (0, 0)),
            pl.BlockSpec((2, hid), lambda o, j: (0, 0)),
            pl.BlockSpec((hid, tc), lambda o, j: (0, (2 * o) * nct + j)),
            pl.BlockSpec((hid, tc), lambda o, j: (0, (2 * o + 1) * nct + j)),
            pl.BlockSpec((1, tc), lambda o, j: (0, j)),
            _resident((n, n)),
            _resident((n, n)),
        ],
        out_specs=[
            pl.BlockSpec((1, n, tc), lambda o, j: (o, 0, j)),
            pl.BlockSpec((1, n, tc), lambda o, j: (o, 0, j)),
            pl.BlockSpec((1, SUBLANES, tc), lambda o, j: (o, 0, j)),
        ],
        out_shape=[
            jax.ShapeDtypeStruct((2, n, ch), F32),
            jax.ShapeDtypeStruct((2, n, ch), F32),
            jax.ShapeDtypeStruct((2, SUBLANES, ch), F32),
        ],
        compiler_params=_cparams(("arbitrary", "arbitrary")),
        name="hyena_filters",
    )(featf, featb, w1, b1, w2, b2, freq, w3, w3, deltas, cm, sm)


def _hyena_kernel(*refs, n, aliased):
    if aliased:
        refs = refs[1:]
    (v_ref, x1_ref, x2_ref, cwv_ref, cw1_ref, cw2_ref, kr_ref, ki_ref, kn_ref, bias_ref, c_ref, s_ref,
     o_ref, pad_ref, z_ref, zb_ref, p_ref) = refs
    halo = SUBLANES
    tc = o_ref.shape[-1]
    rc = min(n, HY_ROW_CHUNK)
    zero_rows = jnp.zeros((halo, tc), F32)
    pad_ref[0:halo, :] = zero_rows
    pad_ref[halo + n:2 * halo + n, :] = zero_rows

    def stage(ref):
        for r in range(0, n, rc):
            pad_ref[halo + r:halo + r + rc, :] = ref[0, r:r + rc, :]

    def conv_rows(cw, r):
        return (cw[0:1] * pad_ref[halo - 1 + r:halo - 1 + r + rc, :] + cw[1:2] * pad_ref[halo + r:halo + r + rc, :]
                + cw[2:3] * pad_ref[halo + 1 + r:halo + 1 + r + rc, :])

    def sign_rows(r):
        row = r + lax.broadcasted_iota(jnp.int32, (rc, 1), 0)
        return row, jnp.where((row & 1) == 0, 1.0, -1.0)

    stage(v_ref)
    cw = cwv_ref[...]
    for r in range(0, n, rc):
        z = conv_rows(cw, r)
        z_ref[r:r + rc, :] = z
        zb_ref[r:r + rc, :] = z.astype(BF16)

    for o, (gate_ref, gate_cw_ref) in enumerate(((x1_ref, cw1_ref), (x2_ref, cw2_ref))):
        znyq = jnp.zeros((1, tc), F32)
        for r in range(0, n, rc):
            row, sgn = sign_rows(r)
            znyq = znyq + jnp.sum(z_ref[r:r + rc, :] * sgn, axis=0, keepdims=True)
            zb = zb_ref[...]
            zr = _dot(c_ref[r:r + rc, :], zb)
            zi = _dot(s_ref[r:r + rc, :], zb)
            kr = kr_ref[o, r:r + rc, :]
            ki = ki_ref[o, r:r + rc, :]
            wgt = jnp.where(row == 0, 0.5 / n, 1.0 / n)
            p_ref[0, r:r + rc, :] = ((zr * kr - zi * ki) * wgt).astype(BF16)
            p_ref[1, r:r + rc, :] = ((zr * ki + zi * kr) * wgt).astype(BF16)
        nyq = znyq * kn_ref[o, 0:1, :] * (0.5 / n)
        stage(gate_ref)
        cw = gate_cw_ref[...]
        bias = bias_ref[o:o + 1, :]
        for r in range(0, n, rc):
            _, sgn = sign_rows(r)
            y = _dot(c_ref[r:r + rc, :], p_ref[0]) + _dot(s_ref[r:r + rc, :], p_ref[1]) + sgn * nyq
            z = conv_rows(cw, r) * (y + z_ref[r:r + rc, :] * bias)
            if o == 0:
                z_ref[r:r + rc, :] = z
                zb_ref[r:r + rc, :] = z.astype(BF16)
            else:
                o_ref[0, r:r + rc, :] = z.astype(o_ref.dtype)


def _hyena_call(u, conv_w, kr, ki, kn, bias, cm, sm, n, row_block, prev_out):
    b, l, _ = u.shape
    ch = bias.shape[-1]
    tc = 2 * LANES
    nct = ch // tc
    aliased = prev_out is not None
    kern = functools.partial(_hyena_kernel, n=n, aliased=aliased)
    once = pl.Buffered(1)
    in_specs = [
        pl.BlockSpec((1, n, tc), lambda j, i: (i, row_block, j)),
        pl.BlockSpec((1, n, tc), lambda j, i: (i, row_block, nct + j)),
        pl.BlockSpec((1, n, tc), lambda j, i: (i, row_block, 2 * nct + j)),
        pl.BlockSpec((3, tc), lambda j, i: (0, j)),
        pl.BlockSpec((3, tc), lambda j, i: (0, nct + j)),
        pl.BlockSpec((3, tc), lambda j, i: (0, 2 * nct + j)),
        pl.BlockSpec((2, n, tc), lambda j, i: (0, 0, j), pipeline_mode=once),
        pl.BlockSpec((2, n, tc), lambda j, i: (0, 0, j), pipeline_mode=once),
        pl.BlockSpec((2, SUBLANES, tc), lambda j, i: (0, 0, j)),
        pl.BlockSpec((2, tc), lambda j, i: (0, j)),
        _resident((n, n)),
        _resident((n, n)),
    ]
    args = [u, u, u, conv_w, conv_w, conv_w, kr, ki, kn, bias, cm, sm]
    aliases = {}
    if aliased:
        in_specs = [pl.BlockSpec(memory_space=pl.ANY)] + in_specs
        args = [prev_out] + args
        aliases = {0: 0}
    return pl.pallas_call(
        kern,
        grid=(nct, b),
        in_specs=in_specs,
        out_specs=pl.BlockSpec((1, n, tc), lambda j, i: (i, row_block, j)),
        out_shape=jax.ShapeDtypeStruct((b, l, ch), MIXER_OUT_DTYPE),
        scratch_shapes=[
            pltpu.VMEM((n + 2 * SUBLANES, tc), F32),
            pltpu.VMEM((n, tc), F32),
            pltpu.VMEM((n, tc), BF16),
            pltpu.VMEM((2, n, tc), BF16),
        ],
        input_output_aliases=aliases,
        compiler_params=_cparams(("arbitrary", "arbitrary")),
        name="hyena_conv_n%d" % n,
    )(*args)


def _rope_tables(n, nc):
    rows = n // GRID_W
    row = jnp.repeat(jnp.arange(rows, dtype=F32), GRID_W)
    col = jnp.tile(jnp.arange(GRID_W, dtype=F32), rows)
    half = HEAD_DIM // 2
    inv = ROPE_BASE ** (-jnp.arange(0, half, 2, dtype=F32) / half)
    ar = row[:, None] * inv
    ac = col[:, None] * inv
    cos = jnp.concatenate([jnp.cos(ar), jnp.cos(ar), jnp.cos(ac), jnp.cos(ac)], axis=-1)
    sin = jnp.concatenate([-jnp.sin(ar), jnp.sin(ar), -jnp.sin(ac), jnp.sin(ac)], axis=-1)
    cos = jnp.concatenate([cos, jnp.ones((nc, HEAD_DIM), F32)], axis=0)
    sin = jnp.concatenate([sin, jnp.zeros((nc, HEAD_DIM), F32)], axis=0)
    return jnp.tile(cos, (1, LANES // HEAD_DIM)), jnp.tile(sin, (1, LANES // HEAD_DIM))


def _rope_partner_cols(width):
    d = np.arange(width)
    quarter = HEAD_DIM // 4
    return np.where((d % (2 * quarter)) < quarter, d + quarter, d - quarter)


def _hyena_feats(n):
    pos = jnp.arange(n, dtype=F32)
    t = pos / max(n - 1, 1)
    ang = (2.0 * math.pi * pos / n)[:, None] * jnp.linspace(1e-4, HY_BANDS - 1, HY_BANDS, dtype=F32)[None, :]
    feats = jnp.concatenate([t[:, None], jnp.cos(ang), -jnp.sin(ang)], axis=-1)
    feats = jnp.pad(feats, ((0, 0), (0, 64 - feats.shape[-1])))
    back = jnp.concatenate([feats[0:1], jnp.flip(feats[1:], axis=0)], axis=0)
    return feats, back


def _pad_cols(w, width):
    return jnp.pad(w, ((0, 0), (0, width - w.shape[-1])))


def _layer_ab(xz, mod, norm_g0, w_in, conv_w, a_log, dt_bias, gdn_g, lam_p, diff_g, lam_init, rope, n, nc):
    hd = GDN_HEADS * GDN_DIM
    wq, wk, wv, wg = (w_in[:, i * hd:(i + 1) * hd] for i in range(4))
    o = 4 * hd
    w_beta, w_alpha = w_in[:, o:o + 16], w_in[:, o + 16:o + 32]
    o += 32
    dd = DIFF_HEADS * 2 * DIFF_DIM
    wdq, wdk, wdv = (w_in[:, o + i * dd:o + (i + 1) * dd] for i in range(3))
    pairs = GDN_HEADS // 2
    pair_cols = lambda w: [w[:, p * LANES:(p + 1) * LANES] for p in range(pairs)]
    w_qkvg = jnp.concatenate([blk for grp in zip(pair_cols(wq), pair_cols(wk), pair_cols(wv), pair_cols(wg))
                              for blk in grp], axis=1)
    perm = _rope_partner_cols(dd)
    w_all = jnp.concatenate([w_qkvg, _pad_cols(jnp.concatenate([w_beta, w_alpha], axis=1), LANES),
                             wdq, wdk, wdv, wdq[:, perm], wdk[:, perm]], axis=1).astype(BF16)
    c0 = 4 * hd
    c1 = c0 + LANES
    segs = (_Seg(0, c0), _Seg(c0, LANES),
            _Seg(c1, dd, rot_start=c1 + 3 * dd, scale=DIFF_DIM ** -0.5, dtype=BF16),
            _Seg(c1 + dd, dd, rot_start=c1 + 4 * dd, dtype=BF16),
            _Seg(c1 + 2 * dd, dd, dtype=BF16, transposed=True))
    qkvg, ba, dq, dk, dvt = _proj_call(xz, mod, norm_g0, w_all, rope[0], rope[1], segs, n // ROW_TILE, "proj_ab")

    cq, ck, cv = (conv_w[:, i * hd:(i + 1) * hd] for i in range(3))
    zeros = jnp.zeros((3, LANES), F32)
    conv_l = jnp.concatenate([blk for p in range(pairs) for blk in
                              (cq[:, p * LANES:(p + 1) * LANES], ck[:, p * LANES:(p + 1) * LANES],
                               cv[:, p * LANES:(p + 1) * LANES], zeros)], axis=1)
    n_gate = 2 * GDN_HEADS
    on_decay_lanes = lambda t: jnp.pad(t.reshape(1, n_gate), ((0, 0), (n_gate, LANES - 2 * n_gate)))
    gate_params = jnp.concatenate([on_decay_lanes(a_log), on_decay_lanes(dt_bias)], axis=0)
    ng = jnp.tile(gdn_g.reshape(1, GDN_DIM), (1, 2))
    oa = _gdn_call(qkvg, ba, conv_l, gate_params, ng, n, nc)
    ob = _diff_call(dq, dk, dvt, lam_p, diff_g, lam_init, 2 * ROW_TILE, 0, n // (2 * ROW_TILE), 0, 2, None)
    ob = _diff_call(dq, dk, dvt, lam_p, diff_g, lam_init, nc, n // nc, 1, n, 1, ob)
    return oa, ob


def _layer_cd(xz, mod, norm_g0, w_in, sink, hy_conv, hy_w1, hy_b1, hy_w2, hy_b2, hy_w3, hy_freq, hy_bias,
              rope, n, nc, last, dft_x, dft_c):
    qd = SWA_HEADS * HEAD_DIM
    kd = SWA_KV_HEADS * HEAD_DIM
    wq, wk, wv, wu = w_in[:, 0:qd], w_in[:, qd:qd + kd], w_in[:, qd + kd:qd + 2 * kd], w_in[:, qd + 2 * kd:]
    dup = lambda w: jnp.concatenate([w[:, 0:HEAD_DIM], w[:, 0:HEAD_DIM], w[:, HEAD_DIM:], w[:, HEAD_DIM:]], axis=1)
    wk2, wv2 = dup(wk), dup(wv)
    ud = wu.shape[1]
    w_all = jnp.concatenate([wq, wk2, wv2, wu, wq[:, _rope_partner_cols(qd)], wk2[:, _rope_partner_cols(2 * kd)]],
                            axis=1).astype(BF16)
    o_u = qd + 4 * kd
    segs = (_Seg(0, qd, rot_start=o_u + ud, scale=HEAD_DIM ** -0.5, dtype=BF16),
            _Seg(qd, 2 * kd, rot_start=o_u + ud + qd, dtype=BF16),
            _Seg(qd + 2 * kd, 2 * kd, dtype=BF16), _Seg(o_u, ud))
    q, k, v, u = _proj_call(xz, mod, norm_g0, w_all, rope[0], rope[1], segs, n // ROW_TILE, "proj_cd")
    oc = _swa_call(q, k, v, _pad_cols(sink.reshape(1, SWA_HEADS), LANES), n, nc, not last)

    ch = hy_bias.shape[-1]
    deltas = jnp.abs(jnp.linspace(HY_MIN_DECAY, HY_MAX_DECAY, ch, dtype=F32)).reshape(1, ch)
    hid = hy_w2.shape[0]
    w1p = jnp.pad(hy_w1, ((0, hid - hy_w1.shape[0]), (0, 0)))
    filt = lambda m, dft: _hyena_filter_call(*_hyena_feats(m), w1p, hy_b1.reshape(1, hid), hy_w2,
                                             hy_b2.reshape(1, hid), hy_freq, hy_w3, deltas, *dft)
    kr, ki, kn = filt(n, dft_x)
    od = _hyena_call(u, hy_conv, kr, ki, kn, hy_bias, *dft_x, n, 0, None)
    if not last:
        kr, ki, kn = filt(nc, dft_c)
        od = _hyena_call(u, hy_conv, kr, ki, kn, hy_bias, *dft_c, nc, n // nc, od)
    return oc, od


def kernel(x, c, ctx, c_ctx, w_mod, b_mod, norm_g, ffn_w_up, ffn_conv, ffn_w_down, ab_w_in, ab_w_out, gdn_conv, gdn_a_log, gdn_dt_bias, gdn_norm_g, diff_lambda, diff_norm_g, cd_w_in, cd_w_out, swa_sink, hy_conv, hy_w1, hy_b1, hy_w2, hy_b2, hy_w3, hy_freq, hy_bias):
    b, n, d = x.shape
    nc = ctx.shape[1]
    depth = w_mod.shape[0]
    assert n % ROW_TILE == 0 and nc == ROW_TILE and n % GRID_W == 0
    xz = jnp.concatenate([x, ctx], axis=1)
    rows = -(-(b + 1) // SUBLANES) * SUBLANES
    cc = jnp.concatenate([c, c_ctx[None], jnp.zeros((rows - b - 1, d), F32)], axis=0)
    mods = _mod_call(cc, w_mod, b_mod)
    rope = _rope_tables(n, nc)
    dft_x = _dft_mats(n)
    dft_c = _dft_mats(nc)
    n_x_tiles = n // ROW_TILE
    for l in range(depth):
        last = l == depth - 1
        i = l // 2
        mx = mods[l, :b].reshape(b, 1, 6, d)
        mz = jnp.broadcast_to(mods[l, b].reshape(1, 1, 6, d), (b, 1, 6, d))
        mod = jnp.concatenate([mx, mz], axis=1)
        if l % 2 == 0:
            lam_init = 0.8 - 0.6 * math.exp(-0.3 * l)
            o1, o2 = _layer_ab(xz, mod, norm_g[l, 0], ab_w_in[i], gdn_conv[i], gdn_a_log[i], gdn_dt_bias[i],
                               gdn_norm_g[i], diff_lambda[i], diff_norm_g[i], lam_init, rope, n, nc)
            w_out = ab_w_out[i]
        else:
            o1, o2 = _layer_cd(xz, mod, norm_g[l, 0], cd_w_in[i], swa_sink[i], hy_conv[i], hy_w1[i], hy_b1[i],
                               hy_w2[i], hy_b2[i], hy_w3[i], hy_freq[i], hy_bias[i], rope, n, nc, last, dft_x, dft_c)
            w_out = cd_w_out[i]
        n_tiles = (n if last else n + nc) // ROW_TILE
        xz = _post_call(o1, o2, xz, mod, norm_g[l, 1], norm_g[l, 2], norm_g[l, 3], w_out.astype(BF16),
                        ffn_w_up[l].astype(BF16), ffn_conv[l], ffn_w_down[l].astype(BF16), n_tiles, n_x_tiles)
    return xz
```

```python
import functools
import math
from typing import NamedTuple, Optional

import jax
import jax.numpy as jnp
import numpy as np
from jax import lax
from jax.experimental import pallas as pl
from jax.experimental.pallas import tpu as pltpu

F32 = jnp.float32
BF16 = jnp.bfloat16
MIXER_OUT_DTYPE = BF16

EPS = 1e-6
NEG_INF = -1e30
GRID_W = 64
HEAD_DIM = 64
ROPE_BASE = 10000.0
GDN_HEADS = 8
GDN_DIM = 64
GDN_CHUNK = 64
DIFF_HEADS = 4
DIFF_DIM = 64
DIFF_SUB_TILES = 4
SWA_HEADS = 8
SWA_KV_HEADS = 2
SWA_WINDOW = 128
SWA_BLOCK = 128
HY_BANDS = 16
HY_MIN_DECAY = math.log(1e-2) / 1.5
HY_MAX_DECAY = math.log(1e-2) / 0.3
HY_ROW_CHUNK = 512

LANES = 128
SUBLANES = 8
MXU_WIDTH = 256
FFN_COL_CHUNK = 6 * MXU_WIDTH
ROW_TILE = 256
VMEM_LIMIT = 56 * 1024 * 1024


def _cparams(sem):
    return pltpu.CompilerParams(dimension_semantics=sem, vmem_limit_bytes=VMEM_LIMIT)


def _resident(shape):
    zeros = (0,) * len(shape)
    return pl.BlockSpec(shape, lambda *_: zeros, pipeline_mode=pl.Buffered(1))


def _log2(v):
    assert v & (v - 1) == 0
    return v.bit_length() - 1


def _sigmoid(x):
    return 1.0 / (1.0 + jnp.exp(-x))


def _silu(x):
    return x * _sigmoid(x)


def _softplus(x):
    return jnp.maximum(x, 0.0) + jnp.log1p(jnp.exp(-jnp.abs(x)))


def _dot(a, b):
    return jnp.dot(a, b, preferred_element_type=F32)


def _dot_nt(a, b):
    return lax.dot_general(a, b, (((1,), (1,)), ((), ())), preferred_element_type=F32)


def _dot_tn(a, b):
    return lax.dot_general(a, b, (((0,), (0,)), ((), ())), preferred_element_type=F32)


def _dot_f32(a, b):
    return jnp.dot(a, b, preferred_element_type=F32, precision=lax.Precision.HIGHEST)


def _split2(x):
    hi = x.astype(BF16)
    lo = (x - hi.astype(F32)).astype(BF16)
    return hi, lo


def _split3(x):
    hi = x.astype(BF16)
    r = x - hi.astype(F32)
    mid = r.astype(BF16)
    lo = (r - mid.astype(F32)).astype(BF16)
    return hi, mid, lo


def _dot_sel(x, sel_bf16):
    hi, mid, lo = _split3(x)
    return _dot(hi, sel_bf16) + _dot(mid, sel_bf16) + _dot(lo, sel_bf16)


def _mm(a, b):
    return _dot(a.astype(BF16), b.astype(BF16))


def _rms(y, g):
    return y * lax.rsqrt(jnp.mean(y * y, axis=-1, keepdims=True) + EPS) * g


def _modnorm(x, g, shift, scale):
    return _rms(x, g) * (1.0 + scale) + shift


def _mod_kernel(cc_ref, w_ref, b_ref, o_ref):
    s = _silu(cc_ref[...])
    o_ref[0] = _dot(s.astype(BF16), w_ref[0].astype(BF16)) + b_ref[0]


def _mod_call(cc, w_mod, b_mod):
    depth, d, nm = w_mod.shape
    rows = cc.shape[0]
    ct = 1536
    return pl.pallas_call(
        _mod_kernel,
        grid=(depth, nm // ct),
        in_specs=[
            pl.BlockSpec((rows, d), lambda l, j: (0, 0)),
            pl.BlockSpec((1, d, ct), lambda l, j: (l, 0, j)),
            pl.BlockSpec((1, 1, ct), lambda l, j: (l, 0, j)),
        ],
        out_specs=pl.BlockSpec((1, rows, ct), lambda l, j: (l, 0, j)),
        out_shape=jax.ShapeDtypeStruct((depth, rows, nm), F32),
        compiler_params=_cparams(("arbitrary", "arbitrary")),
        name="adaln_mod",
    )(cc, w_mod, b_mod.reshape(depth, 1, nm))


class _Seg(NamedTuple):
    start: int
    width: int
    rot_start: Optional[int] = None
    scale: float = 1.0
    dtype: type = F32
    transposed: bool = False


def _proj_kernel(x_ref, mod_ref, g_ref, w_ref, cos_ref, sin_ref, *out_refs, segs):
    m = mod_ref[0, 0]
    h = _modnorm(x_ref[0], g_ref[...], m[0:1], m[1:2]).astype(BF16)
    for o_ref, seg in zip(out_refs, segs):
        y = _dot(h, w_ref[:, seg.start:seg.start + seg.width])
        if seg.rot_start is not None:
            yr = _dot(h, w_ref[:, seg.rot_start:seg.rot_start + seg.width])
            reps = seg.width // LANES
            cos = jnp.concatenate([cos_ref[...]] * reps, axis=1)
            sin = jnp.concatenate([sin_ref[...]] * reps, axis=1)
            y = y * cos + yr * sin
        if seg.scale != 1.0:
            y = y * seg.scale
        if seg.transposed:
            y = y.T
        o_ref[0] = y.astype(seg.dtype)


def _proj_call(xz, mod, g, w, cos_t, sin_t, segs, n_x_tiles, name):
    b, l, d = xz.shape
    tm = ROW_TILE
    nt = l // tm
    p = w.shape[1]
    return pl.pallas_call(
        functools.partial(_proj_kernel, segs=segs),
        grid=(nt, b),
        in_specs=[
            pl.BlockSpec((1, tm, d), lambda t, i: (i, t, 0)),
            pl.BlockSpec((1, 1, 6, d), lambda t, i: (i, t // n_x_tiles, 0, 0)),
            pl.BlockSpec((1, d), lambda t, i: (0, 0)),
            _resident((d, p)),
            pl.BlockSpec((tm, LANES), lambda t, i: (t, 0)),
            pl.BlockSpec((tm, LANES), lambda t, i: (t, 0)),
        ],
        out_specs=[pl.BlockSpec((1, s.width, tm), lambda t, i: (i, 0, t)) if s.transposed
                   else pl.BlockSpec((1, tm, s.width), lambda t, i: (i, t, 0)) for s in segs],
        out_shape=[jax.ShapeDtypeStruct((b, s.width, l) if s.transposed else (b, l, s.width), s.dtype)
                   for s in segs],
        compiler_params=_cparams(("arbitrary", "arbitrary")),
        name=name,
    )(xz, mod, g.reshape(1, d), w, cos_t, sin_t)


def _post_kernel(o1p_ref, o1_ref, o1n_ref, o2p_ref, o2_ref, o2n_ref, xp_ref, x_ref, xn_ref, mod_ref,
                 g1_ref, g2_ref, g3_ref, wout_ref, wup_ref, cw_ref, wdn_ref, out_ref, up_ref,
                 *, tm, n_x_tiles, n_tiles, cf, dff):
    t = pl.program_id(0)
    first = jnp.logical_or(t == 0, t == n_x_tiles)
    last = jnp.logical_or(t == n_x_tiles - 1, t == n_tiles - 1)
    m = mod_ref[0, 0]
    halo = SUBLANES
    ohalo = o1p_ref.shape[1]
    k1 = o1_ref.shape[-1]
    o1e = jnp.concatenate([o1p_ref[0], o1_ref[0], o1n_ref[0]], axis=0)
    o2e = jnp.concatenate([o2p_ref[0], o2_ref[0], o2n_ref[0]], axis=0)
    y = _dot(o1e, wout_ref[0:k1, :]) + _dot(o2e, wout_ref[k1:, :])
    y = y[ohalo - halo:ohalo + tm + halo]
    xe = jnp.concatenate([xp_ref[0], x_ref[0], xn_ref[0]], axis=0)
    x1 = xe + m[2:3] * _rms(y, g1_ref[...])
    h = _modnorm(x1, g2_ref[...], m[3:4], m[4:5]).astype(BF16)
    acc = jnp.zeros((tm, x_ref.shape[-1]), F32)
    for c0 in range(0, dff, cf):
        wd = min(cf, dff - c0)
        halves = []
        for half, base in enumerate((c0, dff + c0)):
            u = _dot(h, wup_ref[:, base:base + wd])
            up_ref[half, :, 0:wd] = u
            up_ref[half, 0:halo, 0:wd] = jnp.where(first, 0.0, u[0:halo])
            up_ref[half, tm + halo:tm + 2 * halo, 0:wd] = jnp.where(last, 0.0, u[tm + halo:])
            cw = cw_ref[:, base:base + wd]
            halves.append(cw[0:1] * up_ref[half, halo - 1:halo - 1 + tm, 0:wd]
                          + cw[1:2] * up_ref[half, halo:halo + tm, 0:wd]
                          + cw[2:3] * up_ref[half, halo + 1:halo + 1 + tm, 0:wd])
        act = (_silu(halves[1]) * halves[0]).astype(BF16)
        acc = acc + _dot(act, wdn_ref[c0:c0 + wd, :])
    out_ref[0] = x1[halo:halo + tm] + m[5:6] * _rms(acc, g3_ref[...])


def _post_call(o1, o2, xz, mod, g1, g2, g3, w_out, w_up, conv_w, w_down, n_tiles, n_x_tiles):
    b, _, d = xz.shape
    tm = ROW_TILE
    rows = n_tiles * tm
    dff = w_down.shape[0]
    cf = FFN_COL_CHUNK
    k1, k2 = o1.shape[-1], o2.shape[-1]
    ohalo = 2 * SUBLANES
    kern = functools.partial(_post_kernel, tm=tm, n_x_tiles=n_x_tiles, n_tiles=n_tiles, cf=cf, dff=dff)

    def with_halos(width, halo_rows):
        per_tile = tm // halo_rows
        n_blocks = rows // halo_rows
        return [
            pl.BlockSpec((1, halo_rows, width), lambda t, i: (i, jnp.maximum(t * per_tile - 1, 0), 0)),
            pl.BlockSpec((1, tm, width), lambda t, i: (i, t, 0)),
            pl.BlockSpec((1, halo_rows, width), lambda t, i: (i, jnp.minimum((t + 1) * per_tile, n_blocks - 1), 0)),
        ]

    row_vec = pl.BlockSpec((1, d), lambda t, i: (0, 0))
    return pl.pallas_call(
        kern,
        grid=(n_tiles, b),
        in_specs=with_halos(k1, ohalo) + with_halos(k2, ohalo) + with_halos(d, SUBLANES) + [
            pl.BlockSpec((1, 1, 6, d), lambda t, i: (i, t // n_x_tiles, 0, 0)),
            row_vec, row_vec, row_vec,
            _resident((k1 + k2, d)),
            _resident((d, 2 * dff)),
            pl.BlockSpec((3, 2 * dff), lambda t, i: (0, 0)),
            _resident((dff, d)),
        ],
        out_specs=pl.BlockSpec((1, tm, d), lambda t, i: (i, t, 0)),
        out_shape=jax.ShapeDtypeStruct((b, rows, d), F32),
        scratch_shapes=[pltpu.VMEM((2, tm + 2 * SUBLANES, cf), F32)],
        compiler_params=_cparams(("arbitrary", "arbitrary")),
        name="mixer_out_conv_ffn",
    )(o1, o1, o1, o2, o2, o2, xz, xz, xz, mod, g1.reshape(1, d), g2.reshape(1, d), g3.reshape(1, d),
      w_out, w_up, conv_w, w_down)


def _half_sums(x2, lane_lo):
    s0 = jnp.sum(jnp.where(lane_lo, x2, 0.0), axis=-1, keepdims=True)
    s1 = jnp.sum(jnp.where(lane_lo, 0.0, x2), axis=-1, keepdims=True)
    return jnp.where(lane_lo, s0, s1)


def _gdn_kernel(qkvg_ref, ba_ref, cw_ref, gp_ref, ng_ref, out_ref,
                pad_ref, q_ref, k_ref, v_ref, bb_ref, gb_ref, qe_ref, mp_ref, ou_ref, nn_ref, egl_ref, o_ref,
                *, n, nc, chunks_per_iter):
    l = n + nc
    c = GDN_CHUNK
    n_chunks = l // c
    pair = pl.program_id(1)
    halo = SUBLANES
    lane = lax.broadcasted_iota(jnp.int32, (1, LANES), 1)
    lane_lo = lane < GDN_DIM

    cw = cw_ref[:, 0:3 * LANES]
    zero_rows = jnp.zeros((halo, 3 * LANES), F32)
    for seq_start, seq_len in ((0, n), (n, nc)):
        base = halo + seq_start + (2 * halo if seq_start else 0)
        pad_ref[base - halo:base, :] = zero_rows
        pad_ref[base + seq_len:base + seq_len + halo, :] = zero_rows
        step = 256
        for r in range(0, seq_len, step):
            pad_ref[base + r:base + r + step, :] = qkvg_ref[0, seq_start + r:seq_start + r + step, 0:3 * LANES]
        for r in range(0, seq_len, step):
            y = (cw[0:1] * pad_ref[base + r - 1:base + r - 1 + step, :]
                 + cw[1:2] * pad_ref[base + r:base + r + step, :]
                 + cw[2:3] * pad_ref[base + r + 1:base + r + 1 + step, :])
            y = _silu(y)
            q = y[:, 0:LANES]
            k = y[:, LANES:2 * LANES]
            rows = slice(seq_start + r, seq_start + r + step)
            q_ref[rows, :] = q * lax.rsqrt(_half_sums(q * q, lane_lo) + EPS) * (GDN_DIM ** -0.5)
            k_ref[rows, :] = k * lax.rsqrt(_half_sums(k * k, lane_lo) + EPS)
            v_ref[rows, :] = y[:, 2 * LANES:3 * LANES]

    sel_r = lax.broadcasted_iota(jnp.int32, (LANES, 4 * LANES), 0)
    sel_c = lax.broadcasted_iota(jnp.int32, (LANES, 4 * LANES), 1)
    quarter = sel_c >> _log2(LANES)
    src_lane = (quarter & 1) * 2 * GDN_HEADS + (quarter >> 1) * GDN_HEADS + 2 * pair + ((sel_c >> _log2(GDN_DIM)) & 1)
    sel = (sel_r == src_lane).astype(BF16)
    gblk = 256
    bi = lax.broadcasted_iota(jnp.int32, (gblk, gblk), 0)
    bj = lax.broadcasted_iota(jnp.int32, (gblk, gblk), 1)
    same_chunk = (bi >> _log2(c)) == (bj >> _log2(c))
    csum = (jnp.logical_and(same_chunk, bi >= bj).astype(BF16), jnp.logical_and(same_chunk, bi <= bj).astype(BF16))
    neg_a = -jnp.exp(gp_ref[0:1, :])
    dt_bias = gp_ref[1:2, :]
    for r in range(0, l, gblk):
        ba = ba_ref[0, r:r + gblk, :]
        gates = jnp.where(lane < 2 * GDN_HEADS, _sigmoid(ba), neg_a * _softplus(ba + dt_bias))
        x = _dot_sel(gates, sel)
        for d in range(2):
            bb_ref[d, r:r + gblk, :] = x[:, 2 * d * LANES:(2 * d + 1) * LANES]
            gb_ref[d, r:r + gblk, :] = _dot_sel_lhs(csum[d], x[:, (2 * d + 1) * LANES:(2 * d + 2) * LANES])

    r2 = lax.broadcasted_iota(jnp.int32, (2 * c, 2 * c), 0)
    c2 = lax.broadcasted_iota(jnp.int32, (2 * c, 2 * c), 1)
    same_head = (r2 >= c) == (c2 >= c)
    eye = (r2 == c2).astype(F32)
    masks = ((jnp.logical_and(same_head, r2 >= c2), jnp.logical_and(same_head, r2 > c2)),
             (jnp.logical_and(same_head, r2 <= c2), jnp.logical_and(same_head, r2 < c2)))
    m0 = lane_lo.astype(F32)
    m1 = 1.0 - m0

    def pair_mask(lv, lower):
        same_block = (r2 >> (lv + 1)) == (c2 >> (lv + 1))
        r_hi = ((r2 >> lv) & 1) == 1
        c_hi = ((c2 >> lv) & 1) == 1
        off = jnp.logical_and(r_hi, jnp.logical_not(c_hi)) if lower else jnp.logical_and(c_hi, jnp.logical_not(r_hi))
        return jnp.logical_and(same_block, off)

    pair_masks = tuple(tuple(pair_mask(lv, lower) for lv in range(_log2(c))) for lower in (True, False))

    def stack_heads(x2):
        return jnp.concatenate([x2 * m0, x2 * m1], axis=0)

    def fold_heads(x):
        return x[0:c] + x[c:2 * c]

    def local_stages(dirs, qs, ks, vs, betas, gcs, out):
        each = lambda f, *cols: [f(*args) for args in zip(*cols)]
        incl = [masks[d][0] for d in dirs]
        strict = [masks[d][1] for d in dirs]
        g1 = each(lambda gc2: jnp.concatenate([gc2, gc2], axis=0), gcs)
        decay = each(lambda g, m: jnp.where(m, jnp.exp(jnp.where(m, g - g.T, 0.0)), 0.0), g1, incl)
        kb = each(lambda k, b: k * b, ks, betas)
        kst = each(lambda k: stack_heads(k).astype(BF16), ks)
        a_raw = each(lambda x, y: _dot_nt(stack_heads(x).astype(BF16), y), kb, kst)
        qk_raw = each(lambda x, y: _dot_nt(stack_heads(x).astype(BF16), y), qs, kst)
        yield
        qk = each(lambda m, x, dc: jnp.where(m, x * dc, 0.0).astype(BF16), incl, qk_raw, decay)
        a = each(lambda m, x, dc: jnp.where(m, x * dc, 0.0), strict, a_raw, decay)
        tinv = each(lambda d, x: eye - jnp.where(pair_masks[d][0], x, 0.0), dirs, a)
        for lv in range(1, _log2(c)):
            ta = each(lambda d, t, x: _mm(t, jnp.where(pair_masks[d][lv], x, 0.0)), dirs, tinv, a)
            yield
            tat = each(_mm, ta, tinv)
            yield
            tinv = each(lambda t, x: t - x, tinv, tat)
        egc = each(jnp.exp, gcs)
        rhs = each(lambda v, b, x, e: jnp.concatenate([stack_heads(v * b), stack_heads(x * e)], axis=1),
                   vs, betas, kb, egc)
        sol = each(_mm, tinv, rhs)
        yield
        u2 = each(lambda x: fold_heads(x[:, 0:LANES]), sol)
        w2 = each(lambda x: fold_heads(x[:, LANES:2 * LANES]), sol)
        gl = each(lambda d, gc2: gc2[c - 1:c, :] if d == 0 else gc2[0:1, :], dirs, gcs)
        ktail = each(lambda k, g, gc2: (k * jnp.exp(g - gc2)).astype(BF16), ks, gl, gcs)
        qwu = each(lambda x, w, u: _dot(x, jnp.concatenate([stack_heads(w), stack_heads(u)], axis=1).astype(BF16)),
                   qk, w2, u2)
        kwu = each(lambda x, w, u: _dot_tn(x, jnp.concatenate([w, u], axis=1).astype(BF16)), ktail, w2, u2)
        yield
        q_eff = each(lambda q, e, x: (q * e - fold_heads(x[:, 0:LANES])).astype(BF16), qs, egc, qwu)
        m_neg = each(lambda x: jnp.where(same_head, -x[:, 0:LANES], 0.0).astype(BF16), kwu)
        o_loc = each(lambda x: fold_heads(x[:, LANES:2 * LANES]), qwu)
        s_loc = each(lambda x: jnp.where(same_head, x[:, LANES:2 * LANES], 0.0), kwu)
        egl = each(lambda g: jnp.broadcast_to(jnp.exp(g), (SUBLANES, LANES)), gl)
        out.extend(zip(q_eff, m_neg, o_loc, s_loc, egl))

    def chunk_rows(chunk, rows_per_chunk):
        return pl.ds(pl.multiple_of(chunk * rows_per_chunk, rows_per_chunk), rows_per_chunk)

    ctx_chunks = nc // c
    per_group = chunks_per_iter
    n_groups = n_chunks // per_group

    def chunks_at(step):
        return jnp.where(step < ctx_chunks, step + n // c, step - ctx_chunks), n_chunks - 1 - step

    def run_group(local_group, scan_group, states):
        dirs, chunks, qs, ks, vs, betas, gcs = [], [], [], [], [], [], []
        if local_group is not None:
            for g in range(per_group):
                for d, chunk in enumerate(chunks_at(per_group * local_group + g)):
                    rows = chunk_rows(chunk, c)
                    dirs.append(d)
                    chunks.append(chunk)
                    qs.append(q_ref[rows, :])
                    ks.append(k_ref[rows, :])
                    vs.append(v_ref[rows, :])
                    betas.append(bb_ref[d, rows, :])
                    gcs.append(gb_ref[d, rows, :])
        scan_chunks, scan_in = [], []
        if scan_group is not None:
            for g in range(per_group):
                step_chunks = chunks_at(per_group * scan_group + g)
                scan_chunks.append(step_chunks)
                scan_in.append([(qe_ref[d, chunk_rows(ch, c), :], mp_ref[d, chunk_rows(ch, 2 * c), :],
                                 ou_ref[d, chunk_rows(ch, c), :], nn_ref[d, chunk_rows(ch, 2 * c), :],
                                 egl_ref[d, chunk_rows(ch, SUBLANES), :]) for d, ch in enumerate(step_chunks)])
        scan_out = []

        def scan_step(states):
            loaded = scan_in[len(scan_out)]
            res = [_dot(jnp.concatenate([ld[0], ld[1]], axis=0), s2.astype(BF16)) for ld, s2 in zip(loaded, states)]
            scan_out.append([r[0:c] + ld[2] for r, ld in zip(res, loaded)])
            return tuple(s2 * ld[4][0:1] + r[c:3 * c] + ld[3] for s2, ld, r in zip(states, loaded, res))

        local_out = []
        stages = local_stages(dirs, qs, ks, vs, betas, gcs, local_out) if local_group is not None else iter(())
        for stage, _ in enumerate(stages):
            if scan_group is not None and stage % 3 == 0 and len(scan_out) < per_group:
                states = scan_step(states)
        while scan_group is not None and len(scan_out) < per_group:
            states = scan_step(states)
        for d, chunk, (q_eff, m_neg, o_loc, s_loc, egl) in zip(dirs, chunks, local_out):
            qe_ref[d, chunk_rows(chunk, c), :] = q_eff
            mp_ref[d, chunk_rows(chunk, 2 * c), :] = m_neg
            ou_ref[d, chunk_rows(chunk, c), :] = o_loc
            nn_ref[d, chunk_rows(chunk, 2 * c), :] = s_loc
            egl_ref[d, chunk_rows(chunk, SUBLANES), :] = egl
        for step_chunks, outs in zip(scan_chunks, scan_out):
            for d, ch in enumerate(step_chunks):
                o_ref[d, chunk_rows(ch, c), :] = outs[d]
        return states

    zero_state = jnp.zeros((2 * c, 2 * c), F32)
    states = run_group(0, None, (zero_state, zero_state))
    states = lax.fori_loop(1, n_groups, lambda j, st: run_group(j, j - 1, st), states)
    run_group(None, n_groups - 1, states)

    ng = ng_ref[...]
    step = 256
    for r in range(0, l, step):
        o = o_ref[0, r:r + step, :] + o_ref[1, r:r + step, :]
        ms = _half_sums(o * o, lane_lo) * (1.0 / GDN_DIM)
        gate = qkvg_ref[0, r:r + step, 3 * LANES:4 * LANES]
        out_ref[0, r:r + step, :] = (o * lax.rsqrt(ms + EPS) * ng * _silu(gate)).astype(out_ref.dtype)


def _dot_sel_lhs(sel_bf16, x):
    hi, mid, lo = _split3(x)
    return _dot(sel_bf16, hi) + _dot(sel_bf16, mid) + _dot(sel_bf16, lo)


def _gdn_call(qkvg, ba, conv_w, gate_params, ng, n, nc):
    b, l, _ = qkvg.shape
    pairs = GDN_HEADS // 2
    n_chunks = l // GDN_CHUNK
    kern = functools.partial(_gdn_kernel, n=n, nc=nc, chunks_per_iter=4)
    return pl.pallas_call(
        kern,
        grid=(b, pairs),
        in_specs=[
            pl.BlockSpec((1, l, 4 * LANES), lambda i, p: (i, 0, p)),
            pl.BlockSpec((1, l, LANES), lambda i, p: (i, 0, 0)),
            pl.BlockSpec((3, 4 * LANES), lambda i, p: (0, p)),
            pl.BlockSpec((2, LANES), lambda i, p: (0, 0)),
            pl.BlockSpec((1, LANES), lambda i, p: (0, 0)),
        ],
        out_specs=pl.BlockSpec((1, l, LANES), lambda i, p: (i, 0, p)),
        out_shape=jax.ShapeDtypeStruct((b, l, pairs * LANES), MIXER_OUT_DTYPE),
        scratch_shapes=[
            pltpu.VMEM((l + 5 * SUBLANES, 3 * LANES), F32),
            pltpu.VMEM((l, LANES), F32),
            pltpu.VMEM((l, LANES), F32),
            pltpu.VMEM((l, LANES), F32),
            pltpu.VMEM((2, l, LANES), F32),
            pltpu.VMEM((2, l, LANES), F32),
            pltpu.VMEM((2, l, LANES), BF16),
            pltpu.VMEM((2, 2 * l, LANES), BF16),
            pltpu.VMEM((2, l, LANES), F32),
            pltpu.VMEM((2, 2 * l, LANES), F32),
            pltpu.VMEM((2, n_chunks * SUBLANES, LANES), F32),
            pltpu.VMEM((2, l, LANES), F32),
        ],
        compiler_params=_cparams(("arbitrary", "arbitrary")),
        name="gated_deltanet",
    )(qkvg, ba, conv_w, gate_params, ng)


def _diff_kernel(*refs, key_start, n_sub, lam_init, aliased):
    if aliased:
        refs = refs[1:]
    q_ref, k_ref, vt_ref, lam_ref, ng_ref, o_ref = refs
    lp = lam_ref[...]
    lam = (jnp.exp(jnp.sum(lp[0:1] * lp[1:2], axis=-1, keepdims=True))
           - jnp.exp(jnp.sum(lp[2:3] * lp[3:4], axis=-1, keepdims=True)) + lam_init)
    lane = lax.broadcasted_iota(jnp.int32, (1, LANES), 1)
    halves = (lane < DIFF_DIM, lane >= DIFF_DIM)
    ng = ng_ref[...]
    k = k_ref[0, key_start:, :]
    vt = vt_ref[0, :, key_start:]
    tq = q_ref.shape[1] // n_sub

    def scores_of(i):
        q = q_ref[0, i * tq:(i + 1) * tq, :]
        return [_dot_nt(k, jnp.where(m, q, jnp.zeros_like(q))) for m in halves]

    ahead = scores_of(0)
    for i in range(n_sub):
        s = ahead
        if i + 1 < n_sub:
            ahead = scores_of(i + 1)
        e = [jnp.exp(x - jnp.max(x, axis=0, keepdims=True)) for x in s]
        pv = [_dot(vt, x.astype(BF16)) for x in e]
        parts = [x * (1.0 / jnp.sum(y, axis=0, keepdims=True)) for x, y in zip(pv, e)]
        ot = parts[0] - lam * parts[1]
        ot = ot * lax.rsqrt(jnp.mean(ot * ot, axis=0, keepdims=True) + EPS)
        o_ref[0, i * tq:(i + 1) * tq, :] = (ot.T * ng * (1.0 - lam_init)).astype(o_ref.dtype)


def _diff_call(dq, dk, dvt, lam_p, ng, lam_init, q_rows, first_block, n_q_blocks, key_start, n_sub, prev_out):
    b, l, _ = dq.shape
    aliased = prev_out is not None
    kern = functools.partial(_diff_kernel, key_start=key_start, n_sub=n_sub, lam_init=lam_init, aliased=aliased)
    row_of = lambda t: first_block + t
    in_specs = [
        pl.BlockSpec((1, q_rows, LANES), lambda i, h, t: (i, row_of(t), h)),
        pl.BlockSpec((1, l, LANES), lambda i, h, t: (i, 0, h)),
        pl.BlockSpec((1, LANES, l), lambda i, h, t: (i, h, 0)),
        pl.BlockSpec((4, DIFF_DIM), lambda i, h, t: (0, 0)),
        pl.BlockSpec((1, LANES), lambda i, h, t: (0, 0)),
    ]
    args = [dq, dk, dvt, lam_p, ng.reshape(1, LANES)]
    aliases = {}
    if aliased:
        in_specs = [pl.BlockSpec(memory_space=pl.ANY)] + in_specs
        args = [prev_out] + args
        aliases = {0: 0}
    return pl.pallas_call(
        kern,
        grid=(b, DIFF_HEADS, n_q_blocks),
        in_specs=in_specs,
        out_specs=pl.BlockSpec((1, q_rows, LANES), lambda i, h, t: (i, row_of(t), h)),
        out_shape=jax.ShapeDtypeStruct((b, l, DIFF_HEADS * LANES), MIXER_OUT_DTYPE),
        input_output_aliases=aliases,
        compiler_params=_cparams(("arbitrary", "arbitrary", "arbitrary")),
        name="diff_attention_ctx" if aliased else "diff_attention",
    )(*args)


def _swa_kernel(q_ref, k_ref, v_ref, sink_ref, o_ref, *, n, nc):
    t = pl.program_id(1)
    blk = SWA_BLOCK
    n_x = n // blk
    q = q_ref[0]
    lane = lax.broadcasted_iota(jnp.int32, (1, LANES), 1)
    lane_lo = lane < HEAD_DIM
    sink = sink_ref[...]
    group = SWA_HEADS // SWA_KV_HEADS

    def run(keys, vals, valid):
        head_of_row = lax.broadcasted_iota(jnp.int32, (group * blk, 1), 0) >> _log2(blk)
        kvs = range(SWA_KV_HEADS)
        kk = [keys[:, kvh * LANES:(kvh + 1) * LANES] for kvh in kvs]
        vv = [vals[:, kvh * LANES:(kvh + 1) * LANES] for kvh in kvs]
        qst, sk = [], []
        for kvh in kvs:
            q_rows = []
            sk_rows = jnp.zeros((group * blk, 1), F32)
            for g in range(group):
                h = kvh * group + g
                qp = q[:, (h // 2) * LANES:(h // 2 + 1) * LANES]
                q_rows.append(jnp.where(lane_lo if h % 2 == 0 else jnp.logical_not(lane_lo), qp, jnp.zeros_like(qp)))
                sk_rows = jnp.where(head_of_row == g, sink[:, h:h + 1], sk_rows)
            qst.append(jnp.concatenate(q_rows, axis=0))
            sk.append(sk_rows)
        s = [_dot_nt(x, y) for x, y in zip(qst, kk)]
        if valid is not None:
            s = [jnp.where(valid, x, NEG_INF) for x in s]
        mx = [jnp.maximum(jnp.max(x, axis=-1, keepdims=True), y) for x, y in zip(s, sk)]
        e = [jnp.exp(x - m) for x, m in zip(s, mx)]
        pv = [_dot(x.astype(BF16), y) for x, y in zip(e, vv)]
        den = [jnp.sum(x, axis=-1, keepdims=True) + jnp.exp(y - m) for x, y, m in zip(e, sk, mx)]
        outs = []
        for o, dn in zip(pv, den):
            o = o * (1.0 / dn)
            for g in range(0, group, 2):
                outs.append(jnp.where(lane_lo, o[g * blk:(g + 1) * blk], o[(g + 1) * blk:(g + 2) * blk]))
        o_ref[0] = jnp.concatenate(outs, axis=1).astype(o_ref.dtype)

    @pl.when(t < n_x)
    def _():
        start = pl.multiple_of(jnp.clip((t - 1) * blk, 0, n - 3 * blk), blk)
        keys = jnp.concatenate([k_ref[0, pl.ds(start, 3 * blk), :], k_ref[0, n:n + nc, :]], axis=0)
        vals = jnp.concatenate([v_ref[0, pl.ds(start, 3 * blk), :], v_ref[0, n:n + nc, :]], axis=0)
        shape = (group * blk, 3 * blk + nc)
        qpos = t * blk + (lax.broadcasted_iota(jnp.int32, shape, 0) & (blk - 1))
        col = lax.broadcasted_iota(jnp.int32, shape, 1)
        dist = qpos - (start + col)
        in_window = jnp.logical_and(dist <= SWA_WINDOW, dist >= -SWA_WINDOW)
        valid = jnp.logical_or(col >= 3 * blk, in_window)
        run(keys, vals, valid)

    @pl.when(t >= n_x)
    def _():
        run(k_ref[0, n:n + nc, :], v_ref[0, n:n + nc, :], None)


def _swa_call(q, k, v, sink, n, nc, with_ctx):
    b, l, _ = q.shape
    blk = SWA_BLOCK
    nt = (l if with_ctx else n) // blk
    kern = functools.partial(_swa_kernel, n=n, nc=nc)
    return pl.pallas_call(
        kern,
        grid=(b, nt),
        in_specs=[
            pl.BlockSpec((1, blk, SWA_HEADS * HEAD_DIM), lambda i, t: (i, t, 0)),
            pl.BlockSpec((1, l, 2 * LANES), lambda i, t: (i, 0, 0)),
            pl.BlockSpec((1, l, 2 * LANES), lambda i, t: (i, 0, 0)),
            pl.BlockSpec((1, LANES), lambda i, t: (0, 0)),
        ],
        out_specs=pl.BlockSpec((1, blk, SWA_HEADS * HEAD_DIM), lambda i, t: (i, t, 0)),
        out_shape=jax.ShapeDtypeStruct((b, l, SWA_HEADS * HEAD_DIM), MIXER_OUT_DTYPE),
        compiler_params=_cparams(("arbitrary", "arbitrary")),
        name="window_attention",
    )(q, k, v, sink)


def _dft_mats(n):
    r = 1 << (_log2(n) // 2)
    m = jnp.arange(n, dtype=jnp.int32)
    thin = lambda k: ((k[:, None] * m[None, :]) % (2 * n)).astype(F32) * (math.pi / n)
    a = thin(r * jnp.arange(n // r, dtype=jnp.int32))[:, None, :]
    b = thin(jnp.arange(r, dtype=jnp.int32))[None, :, :]
    cos = jnp.cos(a) * jnp.cos(b) - jnp.sin(a) * jnp.sin(b)
    sin = jnp.sin(a) * jnp.cos(b) + jnp.cos(a) * jnp.sin(b)
    return cos.reshape(n, n).astype(BF16), (-sin).reshape(n, n).astype(BF16)


def _hyena_filter_kernel(ff_ref, fb_ref, w1_ref, b1_ref, w2_ref, b2_ref, freq_ref, w3f_ref, w3b_ref,
                         dl_ref, c_ref, s_ref, kr_ref, ki_ref, kn_ref):
    n = ff_ref.shape[0]
    freq = freq_ref[...]

    def mlp(feat):
        h = jnp.sin(freq[0:1] * (_dot_f32(feat, w1_ref[...]) + b1_ref[...]))
        return jnp.sin(freq[1:2] * (_dot_f32(h, w2_ref[...]) + b2_ref[...]))

    ff = ff_ref[...]
    fb = fb_ref[...]
    dl = dl_ref[...]
    row = lax.broadcasted_iota(jnp.int32, (n, 1), 0)
    kf = _dot_f32(mlp(ff), w3f_ref[...]) * jnp.exp(-ff[:, 0:1] * dl)
    kb = _dot_f32(mlp(fb), w3b_ref[...]) * jnp.exp(-fb[:, 0:1] * dl)
    kb = jnp.where(row == 0, 0.0, kb)
    ss = jnp.sum(kf * kf, axis=0, keepdims=True) + jnp.sum(kb * kb, axis=0, keepdims=True)
    sc = lax.rsqrt(ss + EPS)
    kf = kf * sc
    kb = kb * sc
    sgn = jnp.where((row & 1) == 0, 1.0, -1.0)
    cm = c_ref[...]
    sm = s_ref[...]
    fh, fl = _split2(kf)
    bh, bl = _split2(kb)
    kr_ref[0] = _dot(cm, fh) + _dot(cm, fl) + sgn * (_dot(cm, bh) + _dot(cm, bl))
    ki_ref[0] = _dot(sm, fh) + _dot(sm, fl) + sgn * (_dot(sm, bh) + _dot(sm, bl))
    nyq = jnp.sum((kf + kb) * sgn, axis=0, keepdims=True)
    kn_ref[0] = jnp.broadcast_to(nyq, (SUBLANES, nyq.shape[-1]))


def _hyena_filter_call(featf, featb, w1, b1, w2, b2, freq, w3, deltas, cm, sm):
    n = featf.shape[0]
    hid = w2.shape[0]
    ch = deltas.shape[-1]
    tc = 2 * LANES
    nct = ch // tc
    return pl.pallas_call(
        _hyena_filter_kernel,
        grid=(2, nct),
        in_specs=[
            pl.BlockSpec((n, hid), lambda o, j: (0, 0)),
            pl.BlockSpec((n, hid), lambda o, j: (0, 0)),
            pl.BlockSpec((hid, hid), lambda o, j: (0, 0)),
            pl.BlockSpec((1, hid), lambda o, j: (0, 0)),
            pl.BlockSpec((hid, hid), lambda o, j: (0, 0)),
            pl.BlockSpec((1, hid), lambda o, j: (0, 0)),
            pl.BlockSpec((2, hid), lambda o, j: (0, 0)),
            pl.BlockSpec((hid, tc), lambda o, j: (0, (2 * o) * nct + j)),
            pl.BlockSpec((hid, tc), lambda o, j: (0, (2 * o + 1) * nct + j)),
            pl.BlockSpec((1, tc), lambda o, j: (0, j)),
            _resident((n, n)),
            _resident((n, n)),
        ],
        out_specs=[
            pl.BlockSpec((1, n, tc), lambda o, j: (o, 0, j)),
            pl.BlockSpec((1, n, tc), lambda o, j: (o, 0, j)),
            pl.BlockSpec((1, SUBLANES, tc), lambda o, j: (o, 0, j)),
        ],
        out_shape=[
            jax.ShapeDtypeStruct((2, n, ch), F32),
            jax.ShapeDtypeStruct((2, n, ch), F32),
            jax.ShapeDtypeStruct((2, SUBLANES, ch), F32),
        ],
        compiler_params=_cparams(("arbitrary", "arbitrary")),
        name="hyena_filters",
    )(featf, featb, w1, b1, w2, b2, freq, w3, w3, deltas, cm, sm)


def _hyena_kernel(*refs, n, aliased):
    if aliased:
        refs = refs[1:]
    (v_ref, x1_ref, x2_ref, cwv_ref, cw1_ref, cw2_ref, kr_ref, ki_ref, kn_ref, bias_ref, c_ref, s_ref,
     o_ref, pad_ref, z_ref, zb_ref, p_ref) = refs
    halo = SUBLANES
    tc = o_ref.shape[-1]
    rc = min(n, HY_ROW_CHUNK)
    zero_rows = jnp.zeros((halo, tc), F32)
    pad_ref[0:halo, :] = zero_rows
    pad_ref[halo + n:2 * halo + n, :] = zero_rows

    def stage(ref):
        for r in range(0, n, rc):
            pad_ref[halo + r:halo + r + rc, :] = ref[0, r:r + rc, :]

    def conv_rows(cw, r):
        return (cw[0:1] * pad_ref[halo - 1 + r:halo - 1 + r + rc, :] + cw[1:2] * pad_ref[halo + r:halo + r + rc, :]
                + cw[2:3] * pad_ref[halo + 1 + r:halo + 1 + r + rc, :])

    def sign_rows(r):
        row = r + lax.broadcasted_iota(jnp.int32, (rc, 1), 0)
        return row, jnp.where((row & 1) == 0, 1.0, -1.0)

    stage(v_ref)
    cw = cwv_ref[...]
    for r in range(0, n, rc):
        z = conv_rows(cw, r)
        z_ref[r:r + rc, :] = z
        zb_ref[r:r + rc, :] = z.astype(BF16)

    for o, (gate_ref, gate_cw_ref) in enumerate(((x1_ref, cw1_ref), (x2_ref, cw2_ref))):
        znyq = jnp.zeros((1, tc), F32)
        for r in range(0, n, rc):
            row, sgn = sign_rows(r)
            znyq = znyq + jnp.sum(z_ref[r:r + rc, :] * sgn, axis=0, keepdims=True)
            zb = zb_ref[...]
            zr = _dot(c_ref[r:r + rc, :], zb)
            zi = _dot(s_ref[r:r + rc, :], zb)
            kr = kr_ref[o, r:r + rc, :]
            ki = ki_ref[o, r:r + rc, :]
            wgt = jnp.where(row == 0, 0.5 / n, 1.0 / n)
            p_ref[0, r:r + rc, :] = ((zr * kr - zi * ki) * wgt).astype(BF16)
            p_ref[1, r:r + rc, :] = ((zr * ki + zi * kr) * wgt).astype(BF16)
        nyq = znyq * kn_ref[o, 0:1, :] * (0.5 / n)
        stage(gate_ref)
        cw = gate_cw_ref[...]
        bias = bias_ref[o:o + 1, :]
        for r in range(0, n, rc):
            _, sgn = sign_rows(r)
            y = _dot(c_ref[r:r + rc, :], p_ref[0]) + _dot(s_ref[r:r + rc, :], p_ref[1]) + sgn * nyq
            z = conv_rows(cw, r) * (y + z_ref[r:r + rc, :] * bias)
            if o == 0:
                z_ref[r:r + rc, :] = z
                zb_ref[r:r + rc, :] = z.astype(BF16)
            else:
                o_ref[0, r:r + rc, :] = z.astype(o_ref.dtype)


def _hyena_call(u, conv_w, kr, ki, kn, bias, cm, sm, n, row_block, prev_out):
    b, l, _ = u.shape
    ch = bias.shape[-1]
    tc = 2 * LANES
    nct = ch // tc
    aliased = prev_out is not None
    kern = functools.partial(_hyena_kernel, n=n, aliased=aliased)
    once = pl.Buffered(1)
    in_specs = [
        pl.BlockSpec((1, n, tc), lambda j, i: (i, row_block, j)),
        pl.BlockSpec((1, n, tc), lambda j, i: (i, row_block, nct + j)),
        pl.BlockSpec((1, n, tc), lambda j, i: (i, row_block, 2 * nct + j)),
        pl.BlockSpec((3, tc), lambda j, i: (0, j)),
        pl.BlockSpec((3, tc), lambda j, i: (0, nct + j)),
        pl.BlockSpec((3, tc), lambda j, i: (0, 2 * nct + j)),
        pl.BlockSpec((2, n, tc), lambda j, i: (0, 0, j), pipeline_mode=once),
        pl.BlockSpec((2, n, tc), lambda j, i: (0, 0, j), pipeline_mode=once),
        pl.BlockSpec((2, SUBLANES, tc), lambda j, i: (0, 0, j)),
        pl.BlockSpec((2, tc), lambda j, i: (0, j)),
        _resident((n, n)),
        _resident((n, n)),
    ]
    args = [u, u, u, conv_w, conv_w, conv_w, kr, ki, kn, bias, cm, sm]
    aliases = {}
    if aliased:
        in_specs = [pl.BlockSpec(memory_space=pl.ANY)] + in_specs
        args = [prev_out] + args
        aliases = {0: 0}
    return pl.pallas_call(
        kern,
        grid=(nct, b),
        in_specs=in_specs,
        out_specs=pl.BlockSpec((1, n, tc), lambda j, i: (i, row_block, j)),
        out_shape=jax.ShapeDtypeStruct((b, l, ch), MIXER_OUT_DTYPE),
        scratch_shapes=[
            pltpu.VMEM((n + 2 * SUBLANES, tc), F32),
            pltpu.VMEM((n, tc), F32),
            pltpu.VMEM((n, tc), BF16),
            pltpu.VMEM((2, n, tc), BF16),
        ],
        input_output_aliases=aliases,
        compiler_params=_cparams(("arbitrary", "arbitrary")),
        name="hyena_conv_n%d" % n,
    )(*args)


def _rope_tables(n, nc):
    rows = n // GRID_W
    row = jnp.repeat(jnp.arange(rows, dtype=F32), GRID_W)
    col = jnp.tile(jnp.arange(GRID_W, dtype=F32), rows)
    half = HEAD_DIM // 2
    inv = ROPE_BASE ** (-jnp.arange(0, half, 2, dtype=F32) / half)
    ar = row[:, None] * inv
    ac = col[:, None] * inv
    cos = jnp.concatenate([jnp.cos(ar), jnp.cos(ar), jnp.cos(ac), jnp.cos(ac)], axis=-1)
    sin = jnp.concatenate([-jnp.sin(ar), jnp.sin(ar), -jnp.sin(ac), jnp.sin(ac)], axis=-1)
    cos = jnp.concatenate([cos, jnp.ones((nc, HEAD_DIM), F32)], axis=0)
    sin = jnp.concatenate([sin, jnp.zeros((nc, HEAD_DIM), F32)], axis=0)
    return jnp.tile(cos, (1, LANES // HEAD_DIM)), jnp.tile(sin, (1, LANES // HEAD_DIM))


def _rope_partner_cols(width):
    d = np.arange(width)
    quarter = HEAD_DIM // 4
    return np.where((d % (2 * quarter)) < quarter, d + quarter, d - quarter)


def _hyena_feats(n):
    pos = jnp.arange(n, dtype=F32)
    t = pos / max(n - 1, 1)
    ang = (2.0 * math.pi * pos / n)[:, None] * jnp.linspace(1e-4, HY_BANDS - 1, HY_BANDS, dtype=F32)[None, :]
    feats = jnp.concatenate([t[:, None], jnp.cos(ang), -jnp.sin(ang)], axis=-1)
    feats = jnp.pad(feats, ((0, 0), (0, 64 - feats.shape[-1])))
    back = jnp.concatenate([feats[0:1], jnp.flip(feats[1:], axis=0)], axis=0)
    return feats, back


def _pad_cols(w, width):
    return jnp.pad(w, ((0, 0), (0, width - w.shape[-1])))


def _layer_ab(xz, mod, norm_g0, w_in, conv_w, a_log, dt_bias, gdn_g, lam_p, diff_g, lam_init, rope, n, nc):
    hd = GDN_HEADS * GDN_DIM
    wq, wk, wv, wg = (w_in[:, i * hd:(i + 1) * hd] for i in range(4))
    o = 4 * hd
    w_beta, w_alpha = w_in[:, o:o + 16], w_in[:, o + 16:o + 32]
    o += 32
    dd = DIFF_HEADS * 2 * DIFF_DIM
    wdq, wdk, wdv = (w_in[:, o + i * dd:o + (i + 1) * dd] for i in range(3))
    pairs = GDN_HEADS // 2
    pair_cols = lambda w: [w[:, p * LANES:(p + 1) * LANES] for p in range(pairs)]
    w_qkvg = jnp.concatenate([blk for grp in zip(pair_cols(wq), pair_cols(wk), pair_cols(wv), pair_cols(wg))
                              for blk in grp], axis=1)
    perm = _rope_partner_cols(dd)
    w_all = jnp.concatenate([w_qkvg, _pad_cols(jnp.concatenate([w_beta, w_alpha], axis=1), LANES),
                             wdq, wdk, wdv, wdq[:, perm], wdk[:, perm]], axis=1).astype(BF16)
    c0 = 4 * hd
    c1 = c0 + LANES
    segs = (_Seg(0, c0), _Seg(c0, LANES),
            _Seg(c1, dd, rot_start=c1 + 3 * dd, scale=DIFF_DIM ** -0.5, dtype=BF16),
            _Seg(c1 + dd, dd, rot_start=c1 + 4 * dd, dtype=BF16),
            _Seg(c1 + 2 * dd, dd, dtype=BF16, transposed=True))
    qkvg, ba, dq, dk, dvt = _proj_call(xz, mod, norm_g0, w_all, rope[0], rope[1], segs, n // ROW_TILE, "proj_ab")

    cq, ck, cv = (conv_w[:, i * hd:(i + 1) * hd] for i in range(3))
    zeros = jnp.zeros((3, LANES), F32)
    conv_l = jnp.concatenate([blk for p in range(pairs) for blk in
                              (cq[:, p * LANES:(p + 1) * LANES], ck[:, p * LANES:(p + 1) * LANES],
                               cv[:, p * LANES:(p + 1) * LANES], zeros)], axis=1)
    n_gate = 2 * GDN_HEADS
    on_decay_lanes = lambda t: jnp.pad(t.reshape(1, n_gate), ((0, 0), (n_gate, LANES - 2 * n_gate)))
    gate_params = jnp.concatenate([on_decay_lanes(a_log), on_decay_lanes(dt_bias)], axis=0)
    ng = jnp.tile(gdn_g.reshape(1, GDN_DIM), (1, 2))
    oa = _gdn_call(qkvg, ba, conv_l, gate_params, ng, n, nc)
    q_rows = DIFF_SUB_TILES * ROW_TILE
    ob = _diff_call(dq, dk, dvt, lam_p, diff_g, lam_init, q_rows, 0, n // q_rows, 0, DIFF_SUB_TILES, None)
    ob = _diff_call(dq, dk, dvt, lam_p, diff_g, lam_init, nc, n // nc, 1, n, 1, ob)
    return oa, ob


def _layer_cd(xz, mod, norm_g0, w_in, sink, hy_conv, hy_w1, hy_b1, hy_w2, hy_b2, hy_w3, hy_freq, hy_bias,
              rope, n, nc, last, dft_x, dft_c):
    qd = SWA_HEADS * HEAD_DIM
    kd = SWA_KV_HEADS * HEAD_DIM
    wq, wk, wv, wu = w_in[:, 0:qd], w_in[:, qd:qd + kd], w_in[:, qd + kd:qd + 2 * kd], w_in[:, qd + 2 * kd:]
    dup = lambda w: jnp.concatenate([w[:, 0:HEAD_DIM], w[:, 0:HEAD_DIM], w[:, HEAD_DIM:], w[:, HEAD_DIM:]], axis=1)
    wk2, wv2 = dup(wk), dup(wv)
    ud = wu.shape[1]
    w_all = jnp.concatenate([wq, wk2, wv2, wu, wq[:, _rope_partner_cols(qd)], wk2[:, _rope_partner_cols(2 * kd)]],
                            axis=1).astype(BF16)
    o_u = qd + 4 * kd
    segs = (_Seg(0, qd, rot_start=o_u + ud, scale=HEAD_DIM ** -0.5, dtype=BF16),
            _Seg(qd, 2 * kd, rot_start=o_u + ud + qd, dtype=BF16),
            _Seg(qd + 2 * kd, 2 * kd, dtype=BF16), _Seg(o_u, ud))
    q, k, v, u = _proj_call(xz, mod, norm_g0, w_all, rope[0], rope[1], segs, n // ROW_TILE, "proj_cd")
    oc = _swa_call(q, k, v, _pad_cols(sink.reshape(1, SWA_HEADS), LANES), n, nc, not last)

    ch = hy_bias.shape[-1]
    deltas = jnp.abs(jnp.linspace(HY_MIN_DECAY, HY_MAX_DECAY, ch, dtype=F32)).reshape(1, ch)
    hid = hy_w2.shape[0]
    w1p = jnp.pad(hy_w1, ((0, hid - hy_w1.shape[0]), (0, 0)))
    filt = lambda m, dft: _hyena_filter_call(*_hyena_feats(m), w1p, hy_b1.reshape(1, hid), hy_w2,
                                             hy_b2.reshape(1, hid), hy_freq, hy_w3, deltas, *dft)
    kr, ki, kn = filt(n, dft_x)
    od = _hyena_call(u, hy_conv, kr, ki, kn, hy_bias, *dft_x, n, 0, None)
    if not last:
        kr, ki, kn = filt(nc, dft_c)
        od = _hyena_call(u, hy_conv, kr, ki, kn, hy_bias, *dft_c, nc, n // nc, od)
    return oc, od


def kernel(x, c, ctx, c_ctx, w_mod, b_mod, norm_g, ffn_w_up, ffn_conv, ffn_w_down, ab_w_in, ab_w_out, gdn_conv, gdn_a_log, gdn_dt_bias, gdn_norm_g, diff_lambda, diff_norm_g, cd_w_in, cd_w_out, swa_sink, hy_conv, hy_w1, hy_b1, hy_w2, hy_b2, hy_w3, hy_freq, hy_bias):
    b, n, d = x.shape
    nc = ctx.shape[1]
    depth = w_mod.shape[0]
    assert n % ROW_TILE == 0 and nc == ROW_TILE and n % GRID_W == 0
    xz = jnp.concatenate([x, ctx], axis=1)
    rows = -(-(b + 1) // SUBLANES) * SUBLANES
    cc = jnp.concatenate([c, c_ctx[None], jnp.zeros((rows - b - 1, d), F32)], axis=0)
    mods = _mod_call(cc, w_mod, b_mod)
    rope = _rope_tables(n, nc)
    dft_x = _dft_mats(n)
    dft_c = _dft_mats(nc)
    n_x_tiles = n // ROW_TILE
    for l in range(depth):
        last = l == depth - 1
        i = l // 2
        mx = mods[l, :b].reshape(b, 1, 6, d)
        mz = jnp.broadcast_to(mods[l, b].reshape(1, 1, 6, d), (b, 1, 6, d))
        mod = jnp.concatenate([mx, mz], axis=1)
        if l % 2 == 0:
            lam_init = 0.8 - 0.6 * math.exp(-0.3 * l)
            o1, o2 = _layer_ab(xz, mod, norm_g[l, 0], ab_w_in[i], gdn_conv[i], gdn_a_log[i], gdn_dt_bias[i],
                               gdn_norm_g[i], diff_lambda[i], diff_norm_g[i], lam_init, rope, n, nc)
            w_out = ab_w_out[i]
        else:
            o1, o2 = _layer_cd(xz, mod, norm_g[l, 0], cd_w_in[i], swa_sink[i], hy_conv[i], hy_w1[i], hy_b1[i],
                               hy_w2[i], hy_b2[i], hy_w3[i], hy_freq[i], hy_bias[i], rope, n, nc, last, dft_x, dft_c)
            w_out = cd_w_out[i]
        n_tiles = (n if last else n + nc) // ROW_TILE
        xz = _post_call(o1, o2, xz, mod, norm_g[l, 1], norm_g[l, 2], norm_g[l, 3], w_out.astype(BF16),
                        ffn_w_up[l].astype(BF16), ffn_conv[l], ffn_w_down[l].astype(BF16), n_tiles, n_x_tiles)
    return xz
```

```python
import functools
import math
from typing import NamedTuple, Optional

import jax
import jax.numpy as jnp
import numpy as np
from jax import lax
from jax.experimental import pallas as pl
from jax.experimental.pallas import tpu as pltpu

F32 = jnp.float32
BF16 = jnp.bfloat16
MIXER_OUT_DTYPE = BF16

EPS = 1e-6
NEG_INF = -1e30
GRID_W = 64
HEAD_DIM = 64
ROPE_BASE = 10000.0
GDN_HEADS = 8
GDN_DIM = 64
GDN_CHUNK = 64
DIFF_HEADS = 4
DIFF_DIM = 64
DIFF_SUB_TILES = 4
SWA_HEADS = 8
SWA_KV_HEADS = 2
SWA_WINDOW = 128
SWA_BLOCK = 128
HY_BANDS = 16
HY_MIN_DECAY = math.log(1e-2) / 1.5
HY_MAX_DECAY = math.log(1e-2) / 0.3
HY_ROW_CHUNK = 512

LANES = 128
SUBLANES = 8
MXU_WIDTH = 256
FFN_COL_CHUNK = 6 * MXU_WIDTH
ROW_TILE = 256
VMEM_LIMIT = 56 * 1024 * 1024


def _cparams(sem):
    return pltpu.CompilerParams(dimension_semantics=sem, vmem_limit_bytes=VMEM_LIMIT)


def _resident(shape):
    zeros = (0,) * len(shape)
    return pl.BlockSpec(shape, lambda *_: zeros, pipeline_mode=pl.Buffered(1))


def _log2(v):
    assert v & (v - 1) == 0
    return v.bit_length() - 1


def _sigmoid(x):
    return 1.0 / (1.0 + jnp.exp(-x))


def _silu(x):
    return x * _sigmoid(x)


def _softplus(x):
    return jnp.maximum(x, 0.0) + jnp.log1p(jnp.exp(-jnp.abs(x)))


def _dot(a, b):
    return jnp.dot(a, b, preferred_element_type=F32)


def _dot_nt(a, b):
    return lax.dot_general(a, b, (((1,), (1,)), ((), ())), preferred_element_type=F32)


def _dot_tn(a, b):
    return lax.dot_general(a, b, (((0,), (0,)), ((), ())), preferred_element_type=F32)


def _dot_f32(a, b):
    return jnp.dot(a, b, preferred_element_type=F32, precision=lax.Precision.HIGHEST)


def _split2(x):
    hi = x.astype(BF16)
    lo = (x - hi.astype(F32)).astype(BF16)
    return hi, lo


def _split3(x):
    hi = x.astype(BF16)
    r = x - hi.astype(F32)
    mid = r.astype(BF16)
    lo = (r - mid.astype(F32)).astype(BF16)
    return hi, mid, lo


def _dot_sel(x, sel_bf16):
    hi, mid, lo = _split3(x)
    return _dot(hi, sel_bf16) + _dot(mid, sel_bf16) + _dot(lo, sel_bf16)


def _mm(a, b):
    return _dot(a.astype(BF16), b.astype(BF16))


def _rms(y, g):
    return y * lax.rsqrt(jnp.mean(y * y, axis=-1, keepdims=True) + EPS) * g


def _modnorm(x, g, shift, scale):
    return _rms(x, g) * (1.0 + scale) + shift


def _mod_kernel(cc_ref, w_ref, b_ref, o_ref):
    s = _silu(cc_ref[...])
    o_ref[0] = _dot(s.astype(BF16), w_ref[0].astype(BF16)) + b_ref[0]


def _mod_call(cc, w_mod, b_mod):
    depth, d, nm = w_mod.shape
    rows = cc.shape[0]
    ct = 1536
    return pl.pallas_call(
        _mod_kernel,
        grid=(depth, nm // ct),
        in_specs=[
            pl.BlockSpec((rows, d), lambda l, j: (0, 0)),
            pl.BlockSpec((1, d, ct), lambda l, j: (l, 0, j)),
            pl.BlockSpec((1, 1, ct), lambda l, j: (l, 0, j)),
        ],
        out_specs=pl.BlockSpec((1, rows, ct), lambda l, j: (l, 0, j)),
        out_shape=jax.ShapeDtypeStruct((depth, rows, nm), F32),
        compiler_params=_cparams(("arbitrary", "arbitrary")),
        name="adaln_mod",
    )(cc, w_mod, b_mod.reshape(depth, 1, nm))


class _Seg(NamedTuple):
    start: int
    width: int
    rot_start: Optional[int] = None
    scale: float = 1.0
    dtype: type = F32
    transposed: bool = False


def _proj_kernel(x_ref, mod_ref, g_ref, w_ref, cos_ref, sin_ref, *out_refs, segs):
    m = mod_ref[0, 0]
    h = _modnorm(x_ref[0], g_ref[...], m[0:1], m[1:2]).astype(BF16)
    for o_ref, seg in zip(out_refs, segs):
        y = _dot(h, w_ref[:, seg.start:seg.start + seg.width])
        if seg.rot_start is not None:
            yr = _dot(h, w_ref[:, seg.rot_start:seg.rot_start + seg.width])
            reps = seg.width // LANES
            cos = jnp.concatenate([cos_ref[...]] * reps, axis=1)
            sin = jnp.concatenate([sin_ref[...]] * reps, axis=1)
            y = y * cos + yr * sin
        if seg.scale != 1.0:
            y = y * seg.scale
        if seg.transposed:
            y = y.T
        o_ref[0] = y.astype(seg.dtype)


def _proj_call(xz, mod, g, w, cos_t, sin_t, segs, n_x_tiles, name):
    b, l, d = xz.shape
    tm = ROW_TILE
    nt = l // tm
    p = w.shape[1]
    return pl.pallas_call(
        functools.partial(_proj_kernel, segs=segs),
        grid=(nt, b),
        in_specs=[
            pl.BlockSpec((1, tm, d), lambda t, i: (i, t, 0)),
            pl.BlockSpec((1, 1, 6, d), lambda t, i: (i, t // n_x_tiles, 0, 0)),
            pl.BlockSpec((1, d), lambda t, i: (0, 0)),
            _resident((d, p)),
            pl.BlockSpec((tm, LANES), lambda t, i: (t, 0)),
            pl.BlockSpec((tm, LANES), lambda t, i: (t, 0)),
        ],
        out_specs=[pl.BlockSpec((1, s.width, tm), lambda t, i: (i, 0, t)) if s.transposed
                   else pl.BlockSpec((1, tm, s.width), lambda t, i: (i, t, 0)) for s in segs],
        out_shape=[jax.ShapeDtypeStruct((b, s.width, l) if s.transposed else (b, l, s.width), s.dtype)
                   for s in segs],
        compiler_params=_cparams(("arbitrary", "arbitrary")),
        name=name,
    )(xz, mod, g.reshape(1, d), w, cos_t, sin_t)


def _post_kernel(o1p_ref, o1_ref, o1n_ref, o2p_ref, o2_ref, o2n_ref, xp_ref, x_ref, xn_ref, mod_ref,
                 g1_ref, g2_ref, g3_ref, wout_ref, wup_ref, cw_ref, wdn_ref, out_ref, up_ref,
                 *, tm, n_x_tiles, n_tiles, cf, dff):
    t = pl.program_id(0)
    first = jnp.logical_or(t == 0, t == n_x_tiles)
    last = jnp.logical_or(t == n_x_tiles - 1, t == n_tiles - 1)
    m = mod_ref[0, 0]
    halo = SUBLANES
    ohalo = o1p_ref.shape[1]
    k1 = o1_ref.shape[-1]
    o1e = jnp.concatenate([o1p_ref[0], o1_ref[0], o1n_ref[0]], axis=0)
    o2e = jnp.concatenate([o2p_ref[0], o2_ref[0], o2n_ref[0]], axis=0)
    y = _dot(o1e, wout_ref[0:k1, :]) + _dot(o2e, wout_ref[k1:, :])
    y = y[ohalo - halo:ohalo + tm + halo]
    xe = jnp.concatenate([xp_ref[0], x_ref[0], xn_ref[0]], axis=0)
    x1 = xe + m[2:3] * _rms(y, g1_ref[...])
    h = _modnorm(x1, g2_ref[...], m[3:4], m[4:5]).astype(BF16)
    acc = jnp.zeros((tm, x_ref.shape[-1]), F32)
    for c0 in range(0, dff, cf):
        wd = min(cf, dff - c0)
        halves = []
        for half, base in enumerate((c0, dff + c0)):
            u = _dot(h, wup_ref[:, base:base + wd])
            up_ref[half, :, 0:wd] = u
            up_ref[half, 0:halo, 0:wd] = jnp.where(first, 0.0, u[0:halo])
            up_ref[half, tm + halo:tm + 2 * halo, 0:wd] = jnp.where(last, 0.0, u[tm + halo:])
            cw = cw_ref[:, base:base + wd]
            halves.append(cw[0:1] * up_ref[half, halo - 1:halo - 1 + tm, 0:wd]
                          + cw[1:2] * up_ref[half, halo:halo + tm, 0:wd]
                          + cw[2:3] * up_ref[half, halo + 1:halo + 1 + tm, 0:wd])
        act = (_silu(halves[1]) * halves[0]).astype(BF16)
        acc = acc + _dot(act, wdn_ref[c0:c0 + wd, :])
    out_ref[0] = x1[halo:halo + tm] + m[5:6] * _rms(acc, g3_ref[...])


def _post_call(o1, o2, xz, mod, g1, g2, g3, w_out, w_up, conv_w, w_down, n_tiles, n_x_tiles):
    b, _, d = xz.shape
    tm = ROW_TILE
    rows = n_tiles * tm
    dff = w_down.shape[0]
    cf = FFN_COL_CHUNK
    k1, k2 = o1.shape[-1], o2.shape[-1]
    ohalo = 2 * SUBLANES
    kern = functools.partial(_post_kernel, tm=tm, n_x_tiles=n_x_tiles, n_tiles=n_tiles, cf=cf, dff=dff)

    def with_halos(width, halo_rows):
        per_tile = tm // halo_rows
        n_blocks = rows // halo_rows
        return [
            pl.BlockSpec((1, halo_rows, width), lambda t, i: (i, jnp.maximum(t * per_tile - 1, 0), 0)),
            pl.BlockSpec((1, tm, width), lambda t, i: (i, t, 0)),
            pl.BlockSpec((1, halo_rows, width), lambda t, i: (i, jnp.minimum((t + 1) * per_tile, n_blocks - 1), 0)),
        ]

    row_vec = pl.BlockSpec((1, d), lambda t, i: (0, 0))
    return pl.pallas_call(
        kern,
        grid=(n_tiles, b),
        in_specs=with_halos(k1, ohalo) + with_halos(k2, ohalo) + with_halos(d, SUBLANES) + [
            pl.BlockSpec((1, 1, 6, d), lambda t, i: (i, t // n_x_tiles, 0, 0)),
            row_vec, row_vec, row_vec,
            _resident((k1 + k2, d)),
            _resident((d, 2 * dff)),
            pl.BlockSpec((3, 2 * dff), lambda t, i: (0, 0)),
            _resident((dff, d)),
        ],
        out_specs=pl.BlockSpec((1, tm, d), lambda t, i: (i, t, 0)),
        out_shape=jax.ShapeDtypeStruct((b, rows, d), F32),
        scratch_shapes=[pltpu.VMEM((2, tm + 2 * SUBLANES, cf), F32)],
        compiler_params=_cparams(("arbitrary", "arbitrary")),
        name="mixer_out_conv_ffn",
    )(o1, o1, o1, o2, o2, o2, xz, xz, xz, mod, g1.reshape(1, d), g2.reshape(1, d), g3.reshape(1, d),
      w_out, w_up, conv_w, w_down)


def _half_sums(x2, lane_lo):
    s0 = jnp.sum(jnp.where(lane_lo, x2, 0.0), axis=-1, keepdims=True)
    s1 = jnp.sum(jnp.where(lane_lo, 0.0, x2), axis=-1, keepdims=True)
    return jnp.where(lane_lo, s0, s1)


def _gdn_kernel(qkvg_ref, ba_ref, cw_ref, gp_ref, ng_ref, out_ref,
                pad_ref, q_ref, k_ref, v_ref, bb_ref, gb_ref, qe_ref, mp_ref, ou_ref, nn_ref, egl_ref, o_ref,
                *, n, nc, chunks_per_iter):
    l = n + nc
    c = GDN_CHUNK
    n_chunks = l // c
    pair = pl.program_id(1)
    halo = SUBLANES
    lane = lax.broadcasted_iota(jnp.int32, (1, LANES), 1)
    lane_lo = lane < GDN_DIM

    cw = cw_ref[:, 0:3 * LANES]
    zero_rows = jnp.zeros((halo, 3 * LANES), F32)
    for seq_start, seq_len in ((0, n), (n, nc)):
        base = halo + seq_start + (2 * halo if seq_start else 0)
        pad_ref[base - halo:base, :] = zero_rows
        pad_ref[base + seq_len:base + seq_len + halo, :] = zero_rows
        step = 256
        for r in range(0, seq_len, step):
            pad_ref[base + r:base + r + step, :] = qkvg_ref[0, seq_start + r:seq_start + r + step, 0:3 * LANES]
        for r in range(0, seq_len, step):
            y = (cw[0:1] * pad_ref[base + r - 1:base + r - 1 + step, :]
                 + cw[1:2] * pad_ref[base + r:base + r + step, :]
                 + cw[2:3] * pad_ref[base + r + 1:base + r + 1 + step, :])
            y = _silu(y)
            q = y[:, 0:LANES]
            k = y[:, LANES:2 * LANES]
            rows = slice(seq_start + r, seq_start + r + step)
            q_ref[rows, :] = q * lax.rsqrt(_half_sums(q * q, lane_lo) + EPS) * (GDN_DIM ** -0.5)
            k_ref[rows, :] = k * lax.rsqrt(_half_sums(k * k, lane_lo) + EPS)
            v_ref[rows, :] = y[:, 2 * LANES:3 * LANES]

    sel_r = lax.broadcasted_iota(jnp.int32, (LANES, 4 * LANES), 0)
    sel_c = lax.broadcasted_iota(jnp.int32, (LANES, 4 * LANES), 1)
    quarter = sel_c >> _log2(LANES)
    src_lane = (quarter & 1) * 2 * GDN_HEADS + (quarter >> 1) * GDN_HEADS + 2 * pair + ((sel_c >> _log2(GDN_DIM)) & 1)
    sel = (sel_r == src_lane).astype(BF16)
    gblk = 256
    bi = lax.broadcasted_iota(jnp.int32, (gblk, gblk), 0)
    bj = lax.broadcasted_iota(jnp.int32, (gblk, gblk), 1)
    same_chunk = (bi >> _log2(c)) == (bj >> _log2(c))
    csum = (jnp.logical_and(same_chunk, bi >= bj).astype(BF16), jnp.logical_and(same_chunk, bi <= bj).astype(BF16))
    neg_a = -jnp.exp(gp_ref[0:1, :])
    dt_bias = gp_ref[1:2, :]
    for r in range(0, l, gblk):
        ba = ba_ref[0, r:r + gblk, :]
        gates = jnp.where(lane < 2 * GDN_HEADS, _sigmoid(ba), neg_a * _softplus(ba + dt_bias))
        x = _dot_sel(gates, sel)
        for d in range(2):
            bb_ref[d, r:r + gblk, :] = x[:, 2 * d * LANES:(2 * d + 1) * LANES]
            gb_ref[d, r:r + gblk, :] = _dot_sel_lhs(csum[d], x[:, (2 * d + 1) * LANES:(2 * d + 2) * LANES])

    r2 = lax.broadcasted_iota(jnp.int32, (2 * c, 2 * c), 0)
    c2 = lax.broadcasted_iota(jnp.int32, (2 * c, 2 * c), 1)
    same_head = (r2 >= c) == (c2 >= c)
    eye = (r2 == c2).astype(F32)
    masks = ((jnp.logical_and(same_head, r2 >= c2), jnp.logical_and(same_head, r2 > c2)),
             (jnp.logical_and(same_head, r2 <= c2), jnp.logical_and(same_head, r2 < c2)))
    m0 = lane_lo.astype(F32)
    m1 = 1.0 - m0

    def pair_mask(lv, lower):
        same_block = (r2 >> (lv + 1)) == (c2 >> (lv + 1))
        r_hi = ((r2 >> lv) & 1) == 1
        c_hi = ((c2 >> lv) & 1) == 1
        off = jnp.logical_and(r_hi, jnp.logical_not(c_hi)) if lower else jnp.logical_and(c_hi, jnp.logical_not(r_hi))
        return jnp.logical_and(same_block, off)

    pair_masks = tuple(tuple(pair_mask(lv, lower) for lv in range(_log2(c))) for lower in (True, False))

    def stack_heads(x2):
        return jnp.concatenate([x2 * m0, x2 * m1], axis=0)

    def fold_heads(x):
        return x[0:c] + x[c:2 * c]

    def local_stages(dirs, qs, ks, vs, betas, gcs, out):
        each = lambda f, *cols: [f(*args) for args in zip(*cols)]
        incl = [masks[d][0] for d in dirs]
        strict = [masks[d][1] for d in dirs]
        g1 = each(lambda gc2: jnp.concatenate([gc2, gc2], axis=0), gcs)
        decay = each(lambda g, m: jnp.where(m, jnp.exp(jnp.where(m, g - g.T, 0.0)), 0.0), g1, incl)
        kb = each(lambda k, b: k * b, ks, betas)
        kst = each(lambda k: stack_heads(k).astype(BF16), ks)
        a_raw = each(lambda x, y: _dot_nt(stack_heads(x).astype(BF16), y), kb, kst)
        qk_raw = each(lambda x, y: _dot_nt(stack_heads(x).astype(BF16), y), qs, kst)
        yield
        qk = each(lambda m, x, dc: jnp.where(m, x * dc, 0.0).astype(BF16), incl, qk_raw, decay)
        a = each(lambda m, x, dc: jnp.where(m, x * dc, 0.0), strict, a_raw, decay)
        tinv = each(lambda d, x: eye - jnp.where(pair_masks[d][0], x, 0.0), dirs, a)
        for lv in range(1, _log2(c)):
            ta = each(lambda d, t, x: _mm(t, jnp.where(pair_masks[d][lv], x, 0.0)), dirs, tinv, a)
            yield
            tat = each(_mm, ta, tinv)
            yield
            tinv = each(lambda t, x: t - x, tinv, tat)
        egc = each(jnp.exp, gcs)
        rhs = each(lambda v, b, x, e: jnp.concatenate([stack_heads(v * b), stack_heads(x * e)], axis=1),
                   vs, betas, kb, egc)
        sol = each(_mm, tinv, rhs)
        yield
        u2 = each(lambda x: fold_heads(x[:, 0:LANES]), sol)
        w2 = each(lambda x: fold_heads(x[:, LANES:2 * LANES]), sol)
        gl = each(lambda d, gc2: gc2[c - 1:c, :] if d == 0 else gc2[0:1, :], dirs, gcs)
        ktail = each(lambda k, g, gc2: (k * jnp.exp(g - gc2)).astype(BF16), ks, gl, gcs)
        qwu = each(lambda x, w, u: _dot(x, jnp.concatenate([stack_heads(w), stack_heads(u)], axis=1).astype(BF16)),
                   qk, w2, u2)
        kwu = each(lambda x, w, u: _dot_tn(x, jnp.concatenate([w, u], axis=1).astype(BF16)), ktail, w2, u2)
        yield
        q_eff = each(lambda q, e, x: (q * e - fold_heads(x[:, 0:LANES])).astype(BF16), qs, egc, qwu)
        m_neg = each(lambda x: jnp.where(same_head, -x[:, 0:LANES], 0.0).astype(BF16), kwu)
        o_loc = each(lambda x: fold_heads(x[:, LANES:2 * LANES]), qwu)
        s_loc = each(lambda x: jnp.where(same_head, x[:, LANES:2 * LANES], 0.0), kwu)
        egl = each(lambda g: jnp.broadcast_to(jnp.exp(g), (SUBLANES, LANES)), gl)
        out.extend(zip(q_eff, m_neg, o_loc, s_loc, egl))

    def chunk_rows(chunk, rows_per_chunk):
        return pl.ds(pl.multiple_of(chunk * rows_per_chunk, rows_per_chunk), rows_per_chunk)

    ctx_chunks = nc // c
    per_group = chunks_per_iter
    n_groups = n_chunks // per_group

    def chunks_at(step):
        return jnp.where(step < ctx_chunks, step + n // c, step - ctx_chunks), n_chunks - 1 - step

    def run_group(local_group, scan_group, states):
        dirs, chunks, qs, ks, vs, betas, gcs = [], [], [], [], [], [], []
        if local_group is not None:
            for g in range(per_group):
                for d, chunk in enumerate(chunks_at(per_group * local_group + g)):
                    rows = chunk_rows(chunk, c)
                    dirs.append(d)
                    chunks.append(chunk)
                    qs.append(q_ref[rows, :])
                    ks.append(k_ref[rows, :])
                    vs.append(v_ref[rows, :])
                    betas.append(bb_ref[d, rows, :])
                    gcs.append(gb_ref[d, rows, :])
        scan_chunks, scan_in = [], []
        if scan_group is not None:
            for g in range(per_group):
                step_chunks = chunks_at(per_group * scan_group + g)
                scan_chunks.append(step_chunks)
                scan_in.append([(qe_ref[d, chunk_rows(ch, c), :], mp_ref[d, chunk_rows(ch, 2 * c), :],
                                 ou_ref[d, chunk_rows(ch, c), :], nn_ref[d, chunk_rows(ch, 2 * c), :],
                                 egl_ref[d, chunk_rows(ch, SUBLANES), :]) for d, ch in enumerate(step_chunks)])
        scan_out = []

        def scan_step(states):
            loaded = scan_in[len(scan_out)]
            res = [_dot(jnp.concatenate([ld[0], ld[1]], axis=0), s2.astype(BF16)) for ld, s2 in zip(loaded, states)]
            scan_out.append([r[0:c] + ld[2] for r, ld in zip(res, loaded)])
            return tuple(s2 * ld[4][0:1] + r[c:3 * c] + ld[3] for s2, ld, r in zip(states, loaded, res))

        local_out = []
        stages = local_stages(dirs, qs, ks, vs, betas, gcs, local_out) if local_group is not None else iter(())
        for stage, _ in enumerate(stages):
            if scan_group is not None and stage % 3 == 0 and len(scan_out) < per_group:
                states = scan_step(states)
        while scan_group is not None and len(scan_out) < per_group:
            states = scan_step(states)
        for d, chunk, (q_eff, m_neg, o_loc, s_loc, egl) in zip(dirs, chunks, local_out):
            qe_ref[d, chunk_rows(chunk, c), :] = q_eff
            mp_ref[d, chunk_rows(chunk, 2 * c), :] = m_neg
            ou_ref[d, chunk_rows(chunk, c), :] = o_loc
            nn_ref[d, chunk_rows(chunk, 2 * c), :] = s_loc
            egl_ref[d, chunk_rows(chunk, SUBLANES), :] = egl
        for step_chunks, outs in zip(scan_chunks, scan_out):
            for d, ch in enumerate(step_chunks):
                o_ref[d, chunk_rows(ch, c), :] = outs[d]
        return states

    zero_state = jnp.zeros((2 * c, 2 * c), F32)
    states = run_group(0, None, (zero_state, zero_state))
    states = lax.fori_loop(1, n_groups, lambda j, st: run_group(j, j - 1, st), states)
    run_group(None, n_groups - 1, states)

    ng = ng_ref[...]
    step = 256
    for r in range(0, l, step):
        o = o_ref[0, r:r + step, :] + o_ref[1, r:r + step, :]
        ms = _half_sums(o * o, lane_lo) * (1.0 / GDN_DIM)
        gate = qkvg_ref[0, r:r + step, 3 * LANES:4 * LANES]
        out_ref[0, r:r + step, :] = (o * lax.rsqrt(ms + EPS) * ng * _silu(gate)).astype(out_ref.dtype)


def _dot_sel_lhs(sel_bf16, x):
    hi, mid, lo = _split3(x)
    return _dot(sel_bf16, hi) + _dot(sel_bf16, mid) + _dot(sel_bf16, lo)


def _gdn_call(qkvg, ba, conv_w, gate_params, ng, n, nc):
    b, l, _ = qkvg.shape
    pairs = GDN_HEADS // 2
    n_chunks = l // GDN_CHUNK
    kern = functools.partial(_gdn_kernel, n=n, nc=nc, chunks_per_iter=4)
    return pl.pallas_call(
        kern,
        grid=(b, pairs),
        in_specs=[
            pl.BlockSpec((1, l, 4 * LANES), lambda i, p: (i, 0, p)),
            pl.BlockSpec((1, l, LANES), lambda i, p: (i, 0, 0)),
            pl.BlockSpec((3, 4 * LANES), lambda i, p: (0, p)),
            pl.BlockSpec((2, LANES), lambda i, p: (0, 0)),
            pl.BlockSpec((1, LANES), lambda i, p: (0, 0)),
        ],
        out_specs=pl.BlockSpec((1, l, LANES), lambda i, p: (i, 0, p)),
        out_shape=jax.ShapeDtypeStruct((b, l, pairs * LANES), MIXER_OUT_DTYPE),
        scratch_shapes=[
            pltpu.VMEM((l + 5 * SUBLANES, 3 * LANES), F32),
            pltpu.VMEM((l, LANES), F32),
            pltpu.VMEM((l, LANES), F32),
            pltpu.VMEM((l, LANES), F32),
            pltpu.VMEM((2, l, LANES), F32),
            pltpu.VMEM((2, l, LANES), F32),
            pltpu.VMEM((2, l, LANES), BF16),
            pltpu.VMEM((2, 2 * l, LANES), BF16),
            pltpu.VMEM((2, l, LANES), F32),
            pltpu.VMEM((2, 2 * l, LANES), F32),
            pltpu.VMEM((2, n_chunks * SUBLANES, LANES), F32),
            pltpu.VMEM((2, l, LANES), F32),
        ],
        compiler_params=_cparams(("arbitrary", "arbitrary")),
        name="gated_deltanet",
    )(qkvg, ba, conv_w, gate_params, ng)


def _diff_kernel(*refs, key_start, n_sub, lam_init, aliased):
    if aliased:
        refs = refs[1:]
    q_ref, k_ref, vt_ref, lam_ref, ng_ref, o_ref = refs
    lp = lam_ref[...]
    lam = (jnp.exp(jnp.sum(lp[0:1] * lp[1:2], axis=-1, keepdims=True))
           - jnp.exp(jnp.sum(lp[2:3] * lp[3:4], axis=-1, keepdims=True)) + lam_init)
    lane = lax.broadcasted_iota(jnp.int32, (1, LANES), 1)
    halves = (lane < DIFF_DIM, lane >= DIFF_DIM)
    ng = ng_ref[...]
    k = k_ref[0, key_start:, :]
    vt = vt_ref[0, :, key_start:]
    tq = q_ref.shape[1] // n_sub

    def scores_of(i):
        q = q_ref[0, i * tq:(i + 1) * tq, :]
        return [_dot_nt(k, jnp.where(m, q, jnp.zeros_like(q))) for m in halves]

    ahead = scores_of(0)
    for i in range(n_sub):
        s = ahead
        if i + 1 < n_sub:
            ahead = scores_of(i + 1)
        e = [jnp.exp(x - jnp.max(x, axis=0, keepdims=True)) for x in s]
        pv = [_dot(vt, x.astype(BF16)) for x in e]
        parts = [x * (1.0 / jnp.sum(y, axis=0, keepdims=True)) for x, y in zip(pv, e)]
        ot = parts[0] - lam * parts[1]
        ot = ot * lax.rsqrt(jnp.mean(ot * ot, axis=0, keepdims=True) + EPS)
        o_ref[0, i * tq:(i + 1) * tq, :] = (ot.T * ng * (1.0 - lam_init)).astype(o_ref.dtype)


def _diff_call(dq, dk, dvt, lam_p, ng, lam_init, q_rows, first_block, n_q_blocks, key_start, n_sub, prev_out):
    b, l, _ = dq.shape
    aliased = prev_out is not None
    kern = functools.partial(_diff_kernel, key_start=key_start, n_sub=n_sub, lam_init=lam_init, aliased=aliased)
    row_of = lambda t: first_block + t
    in_specs = [
        pl.BlockSpec((1, q_rows, LANES), lambda i, h, t: (i, row_of(t), h)),
        pl.BlockSpec((1, l, LANES), lambda i, h, t: (i, 0, h)),
        pl.BlockSpec((1, LANES, l), lambda i, h, t: (i, h, 0)),
        pl.BlockSpec((4, DIFF_DIM), lambda i, h, t: (0, 0)),
        pl.BlockSpec((1, LANES), lambda i, h, t: (0, 0)),
    ]
    args = [dq, dk, dvt, lam_p, ng.reshape(1, LANES)]
    aliases = {}
    if aliased:
        in_specs = [pl.BlockSpec(memory_space=pl.ANY)] + in_specs
        args = [prev_out] + args
        aliases = {0: 0}
    return pl.pallas_call(
        kern,
        grid=(b, DIFF_HEADS, n_q_blocks),
        in_specs=in_specs,
        out_specs=pl.BlockSpec((1, q_rows, LANES), lambda i, h, t: (i, row_of(t), h)),
        out_shape=jax.ShapeDtypeStruct((b, l, DIFF_HEADS * LANES), MIXER_OUT_DTYPE),
        input_output_aliases=aliases,
        compiler_params=_cparams(("arbitrary", "arbitrary", "arbitrary")),
        name="diff_attention_ctx" if aliased else "diff_attention",
    )(*args)


def _swa_kernel(q_ref, k_ref, v_ref, sink_ref, o_ref, *, n, nc):
    t = pl.program_id(1)
    blk = SWA_BLOCK
    n_x = n // blk
    q = q_ref[0]
    lane = lax.broadcasted_iota(jnp.int32, (1, LANES), 1)
    lane_lo = lane < HEAD_DIM
    sink = sink_ref[...]
    group = SWA_HEADS // SWA_KV_HEADS

    def run(keys, vals, valid):
        head_of_row = lax.broadcasted_iota(jnp.int32, (group * blk, 1), 0) >> _log2(blk)
        kvs = range(SWA_KV_HEADS)
        kk = [keys[:, kvh * LANES:(kvh + 1) * LANES] for kvh in kvs]
        vv = [vals[:, kvh * LANES:(kvh + 1) * LANES] for kvh in kvs]
        qst, sk = [], []
        for kvh in kvs:
            q_rows = []
            sk_rows = jnp.zeros((group * blk, 1), F32)
            for g in range(group):
                h = kvh * group + g
                qp = q[:, (h // 2) * LANES:(h // 2 + 1) * LANES]
                q_rows.append(jnp.where(lane_lo if h % 2 == 0 else jnp.logical_not(lane_lo), qp, jnp.zeros_like(qp)))
                sk_rows = jnp.where(head_of_row == g, sink[:, h:h + 1], sk_rows)
            qst.append(jnp.concatenate(q_rows, axis=0))
            sk.append(sk_rows)
        s = [_dot_nt(x, y) for x, y in zip(qst, kk)]
        if valid is not None:
            s = [jnp.where(valid, x, NEG_INF) for x in s]
        mx = [jnp.maximum(jnp.max(x, axis=-1, keepdims=True), y) for x, y in zip(s, sk)]
        e = [jnp.exp(x - m) for x, m in zip(s, mx)]
        pv = [_dot(x.astype(BF16), y) for x, y in zip(e, vv)]
        den = [jnp.sum(x, axis=-1, keepdims=True) + jnp.exp(y - m) for x, y, m in zip(e, sk, mx)]
        outs = []
        for o, dn in zip(pv, den):
            o = o * (1.0 / dn)
            for g in range(0, group, 2):
                outs.append(jnp.where(lane_lo, o[g * blk:(g + 1) * blk], o[(g + 1) * blk:(g + 2) * blk]))
        o_ref[0] = jnp.concatenate(outs, axis=1).astype(o_ref.dtype)

    @pl.when(t < n_x)
    def _():
        start = pl.multiple_of(jnp.clip((t - 1) * blk, 0, n - 3 * blk), blk)
        keys = jnp.concatenate([k_ref[0, pl.ds(start, 3 * blk), :], k_ref[0, n:n + nc, :]], axis=0)
        vals = jnp.concatenate([v_ref[0, pl.ds(start, 3 * blk), :], v_ref[0, n:n + nc, :]], axis=0)
        shape = (group * blk, 3 * blk + nc)
        qpos = t * blk + (lax.broadcasted_iota(jnp.int32, shape, 0) & (blk - 1))
        col = lax.broadcasted_iota(jnp.int32, shape, 1)
        dist = qpos - (start + col)
        in_window = jnp.logical_and(dist <= SWA_WINDOW, dist >= -SWA_WINDOW)
        valid = jnp.logical_or(col >= 3 * blk, in_window)
        run(keys, vals, valid)

    @pl.when(t >= n_x)
    def _():
        run(k_ref[0, n:n + nc, :], v_ref[0, n:n + nc, :], None)


def _swa_call(q, k, v, sink, n, nc, with_ctx):
    b, l, _ = q.shape
    blk = SWA_BLOCK
    nt = (l if with_ctx else n) // blk
    kern = functools.partial(_swa_kernel, n=n, nc=nc)
    return pl.pallas_call(
        kern,
        grid=(b, nt),
        in_specs=[
            pl.BlockSpec((1, blk, SWA_HEADS * HEAD_DIM), lambda i, t: (i, t, 0)),
            pl.BlockSpec((1, l, 2 * LANES), lambda i, t: (i, 0, 0)),
            pl.BlockSpec((1, l, 2 * LANES), lambda i, t: (i, 0, 0)),
            pl.BlockSpec((1, LANES), lambda i, t: (0, 0)),
        ],
        out_specs=pl.BlockSpec((1, blk, SWA_HEADS * HEAD_DIM), lambda i, t: (i, t, 0)),
        out_shape=jax.ShapeDtypeStruct((b, l, SWA_HEADS * HEAD_DIM), MIXER_OUT_DTYPE),
        compiler_params=_cparams(("arbitrary", "arbitrary")),
        name="window_attention",
    )(q, k, v, sink)


def _dft_mats(n):
    r = 1 << (_log2(n) // 2)
    m = jnp.arange(n, dtype=jnp.int32)
    thin = lambda k: ((k[:, None] * m[None, :]) % (2 * n)).astype(F32) * (math.pi / n)
    a = thin(r * jnp.arange(n // r, dtype=jnp.int32))[:, None, :]
    b = thin(jnp.arange(r, dtype=jnp.int32))[None, :, :]
    cos = jnp.cos(a) * jnp.cos(b) - jnp.sin(a) * jnp.sin(b)
    sin = jnp.sin(a) * jnp.cos(b) + jnp.cos(a) * jnp.sin(b)
    return cos.reshape(n, n).astype(BF16), (-sin).reshape(n, n).astype(BF16)


def _hyena_filter_kernel(ff_ref, fb_ref, w1_ref, b1_ref, w2_ref, b2_ref, freq_ref, w3f_ref, w3b_ref,
                         dl_ref, c_ref, s_ref, kr_ref, ki_ref, kn_ref):
    n = ff_ref.shape[0]
    freq = freq_ref[...]

    def mlp(feat):
        h = jnp.sin(freq[0:1] * (_dot_f32(feat, w1_ref[...]) + b1_ref[...]))
        return jnp.sin(freq[1:2] * (_dot_f32(h, w2_ref[...]) + b2_ref[...]))

    ff = ff_ref[...]
    fb = fb_ref[...]
    dl = dl_ref[...]
    row = lax.broadcasted_iota(jnp.int32, (n, 1), 0)
    kf = _dot_f32(mlp(ff), w3f_ref[...]) * jnp.exp(-ff[:, 0:1] * dl)
    kb = _dot_f32(mlp(fb), w3b_ref[...]) * jnp.exp(-fb[:, 0:1] * dl)
    kb = jnp.where(row == 0, 0.0, kb)
    ss = jnp.sum(kf * kf, axis=0, keepdims=True) + jnp.sum(kb * kb, axis=0, keepdims=True)
    sc = lax.rsqrt(ss + EPS)
    kf = kf * sc
    kb = kb * sc
    sgn = jnp.where((row & 1) == 0, 1.0, -1.0)
    cm = c_ref[...]
    sm = s_ref[...]
    fh, fl = _split2(kf)
    bh, bl = _split2(kb)
    kr_ref[0] = _dot(cm, fh) + _dot(cm, fl) + sgn * (_dot(cm, bh) + _dot(cm, bl))
    ki_ref[0] = _dot(sm, fh) + _dot(sm, fl) + sgn * (_dot(sm, bh) + _dot(sm, bl))
    nyq = jnp.sum((kf + kb) * sgn, axis=0, keepdims=True)
    kn_ref[0] = jnp.broadcast_to(nyq, (SUBLANES, nyq.shape[-1]))


def _hyena_filter_call(featf, featb, w1, b1, w2, b2, freq, w3, deltas, cm, sm):
    n = featf.shape[0]
    hid = w2.shape[0]
    ch = deltas.shape[-1]
    tc = 2 * LANES
    nct = ch // tc
    return pl.pallas_call(
        _hyena_filter_kernel,
        grid=(2, nct),
        in_specs=[
            pl.BlockSpec((n, hid), lambda o, j: (0, 0)),
            pl.BlockSpec((n, hid), lambda o, j: (0, 0)),
            pl.BlockSpec((hid, hid), lambda o, j: (0, 0)),
            pl.BlockSpec((1, hid), lambda o, j: (0, 0)),
            pl.BlockSpec((hid, hid), lambda o, j: (0, 0)),
            pl.BlockSpec((1, hid), lambda o, j: (0, 0)),
            pl.BlockSpec((2, hid), lambda o, j: (0, 0)),
            pl.BlockSpec((hid, tc), lambda o, j: (0, (2 * o) * nct + j)),
            pl.BlockSpec((hid, tc), lambda o, j: (0, (2 * o + 1) * nct + j)),
            pl.BlockSpec((1, tc), lambda o, j: (0, j)),
            _resident((n, n)),
            _resident((n, n)),
        ],
        out_specs=[
            pl.BlockSpec((1, n, tc), lambda o, j: (o, 0, j)),
            pl.BlockSpec((1, n, tc), lambda o, j: (o, 0, j)),
            pl.BlockSpec((1, SUBLANES, tc), lambda o, j: (o, 0, j)),
        ],
        out_shape=[
            jax.ShapeDtypeStruct((2, n, ch), F32),
            jax.ShapeDtypeStruct((2, n, ch), F32),
            jax.ShapeDtypeStruct((2, SUBLANES, ch), F32),
        ],
        compiler_params=_cparams(("arbitrary", "arbitrary")),
        name="hyena_filters",
    )(featf, featb, w1, b1, w2, b2, freq, w3, w3, deltas, cm, sm)


def _hyena_kernel(*refs, n, aliased):
    if aliased:
        refs = refs[1:]
    (v_ref, x1_ref, x2_ref, cwv_ref, cw1_ref, cw2_ref, ka_ref, kb_ref, km_ref, bias_ref,
     ce_ref, se_ref, co_ref, so_ref, cot_ref, sot_ref, o_ref, pad_ref, z_ref, zb_ref, p_ref, y_ref) = refs
    halo = SUBLANES
    tc = o_ref.shape[-1]
    h = n // 2
    rc = min(h, HY_ROW_CHUNK)
    lane_groups = tc // LANES
    zero_rows = jnp.zeros((halo, LANES), F32)
    for g in range(lane_groups):
        pad_ref[g, 0:halo, :] = zero_rows
        pad_ref[g, halo + n:2 * halo + n, :] = zero_rows

    def stage(ref):
        for r in range(0, n, 2 * rc):
            for g in range(lane_groups):
                pad_ref[g, halo + r:halo + r + 2 * rc, :] = ref[0, r:r + 2 * rc, g * LANES:(g + 1) * LANES]

    def conv_rows(cw, parity, r):
        first = halo + 2 * r + parity - 1
        taps = [jnp.concatenate([pad_ref[g, pl.ds(first + i, rc, stride=2), :] for g in range(lane_groups)], axis=1)
                for i in range(3)]
        return cw[0:1] * taps[0] + cw[1:2] * taps[1] + cw[2:3] * taps[2]

    def alt_sign(r):
        j = r + lax.broadcasted_iota(jnp.int32, (rc, 1), 0)
        return j, jnp.where((j & 1) == 0, 1.0, -1.0)

    stage(v_ref)
    cw = cwv_ref[...]
    for parity in range(2):
        for r in range(0, h, rc):
            z = conv_rows(cw, parity, r)
            z_ref[parity, r:r + rc, :] = z
            zb_ref[parity, r:r + rc, :] = z.astype(BF16)

    for o, (gate_ref, gate_cw_ref) in enumerate(((x1_ref, cw1_ref), (x2_ref, cw2_ref))):
        mid_r = jnp.zeros((1, tc), F32)
        mid_i = jnp.zeros((1, tc), F32)
        for r in range(0, h, rc):
            _, sgn = alt_sign(r)
            mid_r = mid_r + jnp.sum(z_ref[0, r:r + rc, :] * sgn, axis=0, keepdims=True)
            mid_i = mid_i - jnp.sum(z_ref[1, r:r + rc, :] * sgn, axis=0, keepdims=True)
        km_r = km_ref[o, 0:1, :]
        km_i = km_ref[o, 1:2, :]
        pm_r = (mid_r * km_r - mid_i * km_i) * (1.0 / n)
        pm_i = (mid_r * km_i + mid_i * km_r) * (1.0 / n)
        ze = zb_ref[0]
        zo = zb_ref[1]
        for r in range(0, h, rc):
            k, _ = alt_sign(r)
            rows = slice(r, r + rc)
            ce, co = _dot(ce_ref[rows, :], ze), _dot(co_ref[rows, :], zo)
            se, so = _dot(se_ref[rows, :], ze), _dot(so_ref[rows, :], zo)
            xa_r, xb_r, xa_i, xb_i = ce + co, ce - co, se + so, so - se
            wgt = jnp.where(k == 0, 0.5 / n, 1.0 / n)
            ka_r, ka_i = ka_ref[o, 0, rows, :], ka_ref[o, 1, rows, :]
            kb_r, kb_i = kb_ref[o, 0, rows, :], kb_ref[o, 1, rows, :]
            pa_r = (xa_r * ka_r - xa_i * ka_i) * wgt
            pa_i = (xa_r * ka_i + xa_i * ka_r) * wgt
            pb_r = (xb_r * kb_r - xb_i * kb_i) * wgt
            pb_i = (xb_r * kb_i + xb_i * kb_r) * wgt
            p_ref[0, rows, :] = (pa_r + pb_r).astype(BF16)
            p_ref[1, rows, :] = (pa_i - pb_i).astype(BF16)
            p_ref[2, rows, :] = (pa_r - pb_r).astype(BF16)
            p_ref[3, rows, :] = (pa_i + pb_i).astype(BF16)
        stage(gate_ref)
        cw = gate_cw_ref[...]
        bias = bias_ref[o:o + 1, :]
        for parity, (c_ref, s_ref, mid) in enumerate(((ce_ref, se_ref, pm_r), (cot_ref, sot_ref, -pm_i))):
            for r in range(0, h, rc):
                _, sgn = alt_sign(r)
                rows = slice(r, r + rc)
                y = (_dot(c_ref[rows, :], p_ref[2 * parity]) + _dot(s_ref[rows, :], p_ref[2 * parity + 1])
                     + sgn * mid)
                z = conv_rows(cw, parity, r) * (y + z_ref[parity, rows, :] * bias)
                if o == 0:
                    z_ref[parity, rows, :] = z
                    zb_ref[parity, rows, :] = z.astype(BF16)
                else:
                    for g in range(lane_groups):
                        y_ref[g, pl.ds(2 * r + parity, rc, stride=2), :] = z[:, g * LANES:(g + 1) * LANES]
    for r in range(0, n, 2 * rc):
        rows = slice(r, r + 2 * rc)
        o_ref[0, rows, :] = jnp.concatenate([y_ref[g, rows, :] for g in range(lane_groups)],
                                            axis=1).astype(o_ref.dtype)


def _hyena_call(u, conv_w, kr, ki, kn, bias, cm, sm, n, row_block, prev_out):
    b, l, _ = u.shape
    ch = bias.shape[-1]
    tc = MXU_WIDTH
    nct = ch // tc
    h = n // 2
    aliased = prev_out is not None
    kern = functools.partial(_hyena_kernel, n=n, aliased=aliased)
    ce, co, se, so = cm[:h, 0::2], cm[:h, 1::2], sm[:h, 0::2], sm[:h, 1::2]
    ka = jnp.stack([kr[:, :h], ki[:, :h]], axis=1)
    mirror = lambda t, first: jnp.concatenate([first, jnp.flip(t[:, h + 1:], axis=1)], axis=1)
    kb = jnp.stack([mirror(kr, kn[:, 0:1]), mirror(ki, jnp.zeros_like(kn[:, 0:1]))], axis=1)
    km = jnp.concatenate([kr[:, h:h + 1], ki[:, h:h + 1], jnp.zeros((2, SUBLANES - 2, ch), F32)], axis=1)
    once = pl.Buffered(1)
    in_specs = [
        pl.BlockSpec((1, n, tc), lambda j, i: (i, row_block, j)),
        pl.BlockSpec((1, n, tc), lambda j, i: (i, row_block, nct + j)),
        pl.BlockSpec((1, n, tc), lambda j, i: (i, row_block, 2 * nct + j)),
        pl.BlockSpec((3, tc), lambda j, i: (0, j)),
        pl.BlockSpec((3, tc), lambda j, i: (0, nct + j)),
        pl.BlockSpec((3, tc), lambda j, i: (0, 2 * nct + j)),
        pl.BlockSpec((2, 2, h, tc), lambda j, i: (0, 0, 0, j), pipeline_mode=once),
        pl.BlockSpec((2, 2, h, tc), lambda j, i: (0, 0, 0, j), pipeline_mode=once),
        pl.BlockSpec((2, SUBLANES, tc), lambda j, i: (0, 0, j)),
        pl.BlockSpec((2, tc), lambda j, i: (0, j)),
    ] + [_resident((h, h))] * 6
    args = [u, u, u, conv_w, conv_w, conv_w, ka, kb, km, bias, ce, se, co, so, co.T, so.T]
    aliases = {}
    if aliased:
        in_specs = [pl.BlockSpec(memory_space=pl.ANY)] + in_specs
        args = [prev_out] + args
        aliases = {0: 0}
    return pl.pallas_call(
        kern,
        grid=(nct, b),
        in_specs=in_specs,
        out_specs=pl.BlockSpec((1, n, tc), lambda j, i: (i, row_block, j)),
        out_shape=jax.ShapeDtypeStruct((b, l, ch), MIXER_OUT_DTYPE),
        scratch_shapes=[
            pltpu.VMEM((tc // LANES, n + 2 * SUBLANES, LANES), F32),
            pltpu.VMEM((2, h, tc), F32),
            pltpu.VMEM((2, h, tc), BF16),
            pltpu.VMEM((4, h, tc), BF16),
            pltpu.VMEM((tc // LANES, n, LANES), F32),
        ],
        input_output_aliases=aliases,
        compiler_params=_cparams(("arbitrary", "arbitrary")),
        name="hyena_conv_n%d" % n,
    )(*args)


def _rope_tables(n, nc):
    rows = n // GRID_W
    row = jnp.repeat(jnp.arange(rows, dtype=F32), GRID_W)
    col = jnp.tile(jnp.arange(GRID_W, dtype=F32), rows)
    half = HEAD_DIM // 2
    inv = ROPE_BASE ** (-jnp.arange(0, half, 2, dtype=F32) / half)
    ar = row[:, None] * inv
    ac = col[:, None] * inv
    cos = jnp.concatenate([jnp.cos(ar), jnp.cos(ar), jnp.cos(ac), jnp.cos(ac)], axis=-1)
    sin = jnp.concatenate([-jnp.sin(ar), jnp.sin(ar), -jnp.sin(ac), jnp.sin(ac)], axis=-1)
    cos = jnp.concatenate([cos, jnp.ones((nc, HEAD_DIM), F32)], axis=0)
    sin = jnp.concatenate([sin, jnp.zeros((nc, HEAD_DIM), F32)], axis=0)
    return jnp.tile(cos, (1, LANES // HEAD_DIM)), jnp.tile(sin, (1, LANES // HEAD_DIM))


def _rope_partner_cols(width):
    d = np.arange(width)
    quarter = HEAD_DIM // 4
    return np.where((d % (2 * quarter)) < quarter, d + quarter, d - quarter)


def _hyena_feats(n):
    pos = jnp.arange(n, dtype=F32)
    t = pos / max(n - 1, 1)
    ang = (2.0 * math.pi * pos / n)[:, None] * jnp.linspace(1e-4, HY_BANDS - 1, HY_BANDS, dtype=F32)[None, :]
    feats = jnp.concatenate([t[:, None], jnp.cos(ang), -jnp.sin(ang)], axis=-1)
    feats = jnp.pad(feats, ((0, 0), (0, 64 - feats.shape[-1])))
    back = jnp.concatenate([feats[0:1], jnp.flip(feats[1:], axis=0)], axis=0)
    return feats, back


def _pad_cols(w, width):
    return jnp.pad(w, ((0, 0), (0, width - w.shape[-1])))


def _layer_ab(xz, mod, norm_g0, w_in, conv_w, a_log, dt_bias, gdn_g, lam_p, diff_g, lam_init, rope, n, nc):
    hd = GDN_HEADS * GDN_DIM
    wq, wk, wv, wg = (w_in[:, i * hd:(i + 1) * hd] for i in range(4))
    o = 4 * hd
    w_beta, w_alpha = w_in[:, o:o + 16], w_in[:, o + 16:o + 32]
    o += 32
    dd = DIFF_HEADS * 2 * DIFF_DIM
    wdq, wdk, wdv = (w_in[:, o + i * dd:o + (i + 1) * dd] for i in range(3))
    pairs = GDN_HEADS // 2
    pair_cols = lambda w: [w[:, p * LANES:(p + 1) * LANES] for p in range(pairs)]
    w_qkvg = jnp.concatenate([blk for grp in zip(pair_cols(wq), pair_cols(wk), pair_cols(wv), pair_cols(wg))
                              for blk in grp], axis=1)
    perm = _rope_partner_cols(dd)
    w_all = jnp.concatenate([w_qkvg, _pad_cols(jnp.concatenate([w_beta, w_alpha], axis=1), LANES),
                             wdq, wdk, wdv, wdq[:, perm], wdk[:, perm]], axis=1).astype(BF16)
    c0 = 4 * hd
    c1 = c0 + LANES
    segs = (_Seg(0, c0), _Seg(c0, LANES),
            _Seg(c1, dd, rot_start=c1 + 3 * dd, scale=DIFF_DIM ** -0.5, dtype=BF16),
            _Seg(c1 + dd, dd, rot_start=c1 + 4 * dd, dtype=BF16),
            _Seg(c1 + 2 * dd, dd, dtype=BF16, transposed=True))
    qkvg, ba, dq, dk, dvt = _proj_call(xz, mod, norm_g0, w_all, rope[0], rope[1], segs, n // ROW_TILE, "proj_ab")

    cq, ck, cv = (conv_w[:, i * hd:(i + 1) * hd] for i in range(3))
    zeros = jnp.zeros((3, LANES), F32)
    conv_l = jnp.concatenate([blk for p in range(pairs) for blk in
                              (cq[:, p * LANES:(p + 1) * LANES], ck[:, p * LANES:(p + 1) * LANES],
                               cv[:, p * LANES:(p + 1) * LANES], zeros)], axis=1)
    n_gate = 2 * GDN_HEADS
    on_decay_lanes = lambda t: jnp.pad(t.reshape(1, n_gate), ((0, 0), (n_gate, LANES - 2 * n_gate)))
    gate_params = jnp.concatenate([on_decay_lanes(a_log), on_decay_lanes(dt_bias)], axis=0)
    ng = jnp.tile(gdn_g.reshape(1, GDN_DIM), (1, 2))
    oa = _gdn_call(qkvg, ba, conv_l, gate_params, ng, n, nc)
    q_rows = DIFF_SUB_TILES * ROW_TILE
    ob = _diff_call(dq, dk, dvt, lam_p, diff_g, lam_init, q_rows, 0, n // q_rows, 0, DIFF_SUB_TILES, None)
    ob = _diff_call(dq, dk, dvt, lam_p, diff_g, lam_init, nc, n // nc, 1, n, 1, ob)
    return oa, ob


def _layer_cd(xz, mod, norm_g0, w_in, sink, hy_conv, hy_w1, hy_b1, hy_w2, hy_b2, hy_w3, hy_freq, hy_bias,
              rope, n, nc, last, dft_x, dft_c):
    qd = SWA_HEADS * HEAD_DIM
    kd = SWA_KV_HEADS * HEAD_DIM
    wq, wk, wv, wu = w_in[:, 0:qd], w_in[:, qd:qd + kd], w_in[:, qd + kd:qd + 2 * kd], w_in[:, qd + 2 * kd:]
    dup = lambda w: jnp.concatenate([w[:, 0:HEAD_DIM], w[:, 0:HEAD_DIM], w[:, HEAD_DIM:], w[:, HEAD_DIM:]], axis=1)
    wk2, wv2 = dup(wk), dup(wv)
    ud = wu.shape[1]
    w_all = jnp.concatenate([wq, wk2, wv2, wu, wq[:, _rope_partner_cols(qd)], wk2[:, _rope_partner_cols(2 * kd)]],
                            axis=1).astype(BF16)
    o_u = qd + 4 * kd
    segs = (_Seg(0, qd, rot_start=o_u + ud, scale=HEAD_DIM ** -0.5, dtype=BF16),
            _Seg(qd, 2 * kd, rot_start=o_u + ud + qd, dtype=BF16),
            _Seg(qd + 2 * kd, 2 * kd, dtype=BF16), _Seg(o_u, ud))
    q, k, v, u = _proj_call(xz, mod, norm_g0, w_all, rope[0], rope[1], segs, n // ROW_TILE, "proj_cd")
    oc = _swa_call(q, k, v, _pad_cols(sink.reshape(1, SWA_HEADS), LANES), n, nc, not last)

    ch = hy_bias.shape[-1]
    deltas = jnp.abs(jnp.linspace(HY_MIN_DECAY, HY_MAX_DECAY, ch, dtype=F32)).reshape(1, ch)
    hid = hy_w2.shape[0]
    w1p = jnp.pad(hy_w1, ((0, hid - hy_w1.shape[0]), (0, 0)))
    filt = lambda m, dft: _hyena_filter_call(*_hyena_feats(m), w1p, hy_b1.reshape(1, hid), hy_w2,
                                             hy_b2.reshape(1, hid), hy_freq, hy_w3, deltas, *dft)
    kr, ki, kn = filt(n, dft_x)
    od = _hyena_call(u, hy_conv, kr, ki, kn, hy_bias, *dft_x, n, 0, None)
    if not last:
        kr, ki, kn = filt(nc, dft_c)
        od = _hyena_call(u, hy_conv, kr, ki, kn, hy_bias, *dft_c, nc, n // nc, od)
    return oc, od


def kernel(x, c, ctx, c_ctx, w_mod, b_mod, norm_g, ffn_w_up, ffn_conv, ffn_w_down, ab_w_in, ab_w_out, gdn_conv, gdn_a_log, gdn_dt_bias, gdn_norm_g, diff_lambda, diff_norm_g, cd_w_in, cd_w_out, swa_sink, hy_conv, hy_w1, hy_b1, hy_w2, hy_b2, hy_w3, hy_freq, hy_bias):
    b, n, d = x.shape
    nc = ctx.shape[1]
    depth = w_mod.shape[0]
    assert n % ROW_TILE == 0 and nc == ROW_TILE and n % GRID_W == 0
    xz = jnp.concatenate([x, ctx], axis=1)
    rows = -(-(b + 1) // SUBLANES) * SUBLANES
    cc = jnp.concatenate([c, c_ctx[None], jnp.zeros((rows - b - 1, d), F32)], axis=0)
    mods = _mod_call(cc, w_mod, b_mod)
    rope = _rope_tables(n, nc)
    dft_x = _dft_mats(n)
    dft_c = _dft_mats(nc)
    n_x_tiles = n // ROW_TILE
    for l in range(depth):
        last = l == depth - 1
        i = l // 2
        mx = mods[l, :b].reshape(b, 1, 6, d)
        mz = jnp.broadcast_to(mods[l, b].reshape(1, 1, 6, d), (b, 1, 6, d))
        mod = jnp.concatenate([mx, mz], axis=1)
        if l % 2 == 0:
            lam_init = 0.8 - 0.6 * math.exp(-0.3 * l)
            o1, o2 = _layer_ab(xz, mod, norm_g[l, 0], ab_w_in[i], gdn_conv[i], gdn_a_log[i], gdn_dt_bias[i],
                               gdn_norm_g[i], diff_lambda[i], diff_norm_g[i], lam_init, rope, n, nc)
            w_out = ab_w_out[i]
        else:
            o1, o2 = _layer_cd(xz, mod, norm_g[l, 0], cd_w_in[i], swa_sink[i], hy_conv[i], hy_w1[i], hy_b1[i],
                               hy_w2[i], hy_b2[i], hy_w3[i], hy_freq[i], hy_bias[i], rope, n, nc, last, dft_x, dft_c)
            w_out = cd_w_out[i]
        n_tiles = (n if last else n + nc) // ROW_TILE
        xz = _post_call(o1, o2, xz, mod, norm_g[l, 1], norm_g[l, 2], norm_g[l, 3], w_out.astype(BF16),
                        ffn_w_up[l].astype(BF16), ffn_conv[l], ffn_w_down[l].astype(BF16), n_tiles, n_x_tiles)
    return xz
```

```python
import functools
import math
from typing import NamedTuple, Optional

import jax
import jax.numpy as jnp
import numpy as np
from jax import lax
from jax.experimental import pallas as pl
from jax.experimental.pallas import tpu as pltpu

F32 = jnp.float32
BF16 = jnp.bfloat16
MIXER_OUT_DTYPE = BF16

EPS = 1e-6
NEG_INF = -1e30
GRID_W = 64
HEAD_DIM = 64
ROPE_BASE = 10000.0
GDN_HEADS = 8
GDN_DIM = 64
GDN_CHUNK = 64
DIFF_HEADS = 4
DIFF_DIM = 64
DIFF_SUB_TILES = 4
SWA_HEADS = 8
SWA_KV_HEADS = 2
SWA_WINDOW = 128
SWA_BLOCK = 128
HY_BANDS = 16
HY_MIN_DECAY = math.log(1e-2) / 1.5
HY_MAX_DECAY = math.log(1e-2) / 0.3
HY_ROW_CHUNK = 512

LANES = 128
SUBLANES = 8
MXU_WIDTH = 256
FFN_COL_CHUNK = 6 * MXU_WIDTH
ROW_TILE = 256
VMEM_LIMIT = 56 * 1024 * 1024


def _cparams(sem):
    return pltpu.CompilerParams(dimension_semantics=sem, vmem_limit_bytes=VMEM_LIMIT)


def _resident(shape):
    zeros = (0,) * len(shape)
    return pl.BlockSpec(shape, lambda *_: zeros, pipeline_mode=pl.Buffered(1))


def _log2(v):
    assert v & (v - 1) == 0
    return v.bit_length() - 1


def _sigmoid(x):
    return 1.0 / (1.0 + jnp.exp(-x))


def _silu(x):
    return x * _sigmoid(x)


def _softplus(x):
    return jnp.maximum(x, 0.0) + jnp.log1p(jnp.exp(-jnp.abs(x)))


def _dot(a, b):
    return jnp.dot(a, b, preferred_element_type=F32)


def _dot_nt(a, b):
    return lax.dot_general(a, b, (((1,), (1,)), ((), ())), preferred_element_type=F32)


def _dot_tn(a, b):
    return lax.dot_general(a, b, (((0,), (0,)), ((), ())), preferred_element_type=F32)


def _dot_f32(a, b):
    return jnp.dot(a, b, preferred_element_type=F32, precision=lax.Precision.HIGHEST)


def _split2(x):
    hi = x.astype(BF16)
    lo = (x - hi.astype(F32)).astype(BF16)
    return hi, lo


def _split3(x):
    hi = x.astype(BF16)
    r = x - hi.astype(F32)
    mid = r.astype(BF16)
    lo = (r - mid.astype(F32)).astype(BF16)
    return hi, mid, lo


def _dot_sel(x, sel_bf16):
    hi, mid, lo = _split3(x)
    return _dot(hi, sel_bf16) + _dot(mid, sel_bf16) + _dot(lo, sel_bf16)


def _mm(a, b):
    return _dot(a.astype(BF16), b.astype(BF16))


def _rms(y, g):
    return y * lax.rsqrt(jnp.mean(y * y, axis=-1, keepdims=True) + EPS) * g


def _modnorm(x, g, shift, scale):
    return _rms(x, g) * (1.0 + scale) + shift


def _mod_kernel(cc_ref, w_ref, b_ref, o_ref):
    s = _silu(cc_ref[...])
    o_ref[0] = _dot(s.astype(BF16), w_ref[0].astype(BF16)) + b_ref[0]


def _mod_call(cc, w_mod, b_mod):
    depth, d, nm = w_mod.shape
    rows = cc.shape[0]
    ct = 1536
    return pl.pallas_call(
        _mod_kernel,
        grid=(depth, nm // ct),
        in_specs=[
            pl.BlockSpec((rows, d), lambda l, j: (0, 0)),
            pl.BlockSpec((1, d, ct), lambda l, j: (l, 0, j)),
            pl.BlockSpec((1, 1, ct), lambda l, j: (l, 0, j)),
        ],
        out_specs=pl.BlockSpec((1, rows, ct), lambda l, j: (l, 0, j)),
        out_shape=jax.ShapeDtypeStruct((depth, rows, nm), F32),
        compiler_params=_cparams(("arbitrary", "arbitrary")),
        name="adaln_mod",
    )(cc, w_mod, b_mod.reshape(depth, 1, nm))


class _Seg(NamedTuple):
    start: int
    width: int
    rot_start: Optional[int] = None
    scale: float = 1.0
    dtype: type = F32
    transposed: bool = False


def _proj_kernel(x_ref, mod_ref, g_ref, w_ref, cos_ref, sin_ref, *out_refs, segs):
    m = mod_ref[0, 0]
    h = _modnorm(x_ref[0], g_ref[...], m[0:1], m[1:2]).astype(BF16)
    for o_ref, seg in zip(out_refs, segs):
        y = _dot(h, w_ref[:, seg.start:seg.start + seg.width])
        if seg.rot_start is not None:
            yr = _dot(h, w_ref[:, seg.rot_start:seg.rot_start + seg.width])
            reps = seg.width // LANES
            cos = jnp.concatenate([cos_ref[...]] * reps, axis=1)
            sin = jnp.concatenate([sin_ref[...]] * reps, axis=1)
            y = y * cos + yr * sin
        if seg.scale != 1.0:
            y = y * seg.scale
        if seg.transposed:
            y = y.T
        o_ref[0] = y.astype(seg.dtype)


def _proj_call(xz, mod, g, w, cos_t, sin_t, segs, n_x_tiles, name):
    b, l, d = xz.shape
    tm = ROW_TILE
    nt = l // tm
    p = w.shape[1]
    return pl.pallas_call(
        functools.partial(_proj_kernel, segs=segs),
        grid=(nt, b),
        in_specs=[
            pl.BlockSpec((1, tm, d), lambda t, i: (i, t, 0)),
            pl.BlockSpec((1, 1, 6, d), lambda t, i: (i, t // n_x_tiles, 0, 0)),
            pl.BlockSpec((1, d), lambda t, i: (0, 0)),
            _resident((d, p)),
            pl.BlockSpec((tm, LANES), lambda t, i: (t, 0)),
            pl.BlockSpec((tm, LANES), lambda t, i: (t, 0)),
        ],
        out_specs=[pl.BlockSpec((1, s.width, tm), lambda t, i: (i, 0, t)) if s.transposed
                   else pl.BlockSpec((1, tm, s.width), lambda t, i: (i, t, 0)) for s in segs],
        out_shape=[jax.ShapeDtypeStruct((b, s.width, l) if s.transposed else (b, l, s.width), s.dtype)
                   for s in segs],
        compiler_params=_cparams(("arbitrary", "arbitrary")),
        name=name,
    )(xz, mod, g.reshape(1, d), w, cos_t, sin_t)


def _post_kernel(o1p_ref, o1_ref, o1n_ref, o2p_ref, o2_ref, o2n_ref, xp_ref, x_ref, xn_ref, mod_ref,
                 g1_ref, g2_ref, g3_ref, wout_ref, wup_ref, cw_ref, wdn_ref, out_ref, up_ref,
                 *, tm, n_x_tiles, n_tiles, cf, dff):
    t = pl.program_id(0)
    first = jnp.logical_or(t == 0, t == n_x_tiles)
    last = jnp.logical_or(t == n_x_tiles - 1, t == n_tiles - 1)
    m = mod_ref[0, 0]
    halo = SUBLANES
    ohalo = o1p_ref.shape[1]
    k1 = o1_ref.shape[-1]
    o1e = jnp.concatenate([o1p_ref[0], o1_ref[0], o1n_ref[0]], axis=0)
    o2e = jnp.concatenate([o2p_ref[0], o2_ref[0], o2n_ref[0]], axis=0)
    y = _dot(o1e, wout_ref[0:k1, :]) + _dot(o2e, wout_ref[k1:, :])
    y = y[ohalo - halo:ohalo + tm + halo]
    xe = jnp.concatenate([xp_ref[0], x_ref[0], xn_ref[0]], axis=0)
    x1 = xe + m[2:3] * _rms(y, g1_ref[...])
    h = _modnorm(x1, g2_ref[...], m[3:4], m[4:5]).astype(BF16)
    acc = jnp.zeros((tm, x_ref.shape[-1]), F32)
    for c0 in range(0, dff, cf):
        wd = min(cf, dff - c0)
        halves = []
        for half, base in enumerate((c0, dff + c0)):
            u = _dot(h, wup_ref[:, base:base + wd])
            up_ref[half, :, 0:wd] = u
            up_ref[half, 0:halo, 0:wd] = jnp.where(first, 0.0, u[0:halo])
            up_ref[half, tm + halo:tm + 2 * halo, 0:wd] = jnp.where(last, 0.0, u[tm + halo:])
            cw = cw_ref[:, base:base + wd]
            halves.append(cw[0:1] * up_ref[half, halo - 1:halo - 1 + tm, 0:wd]
                          + cw[1:2] * up_ref[half, halo:halo + tm, 0:wd]
                          + cw[2:3] * up_ref[half, halo + 1:halo + 1 + tm, 0:wd])
        act = (_silu(halves[1]) * halves[0]).astype(BF16)
        acc = acc + _dot(act, wdn_ref[c0:c0 + wd, :])
    out_ref[0] = x1[halo:halo + tm] + m[5:6] * _rms(acc, g3_ref[...])


def _post_call(o1, o2, xz, mod, g1, g2, g3, w_out, w_up, conv_w, w_down, n_tiles, n_x_tiles):
    b, _, d = xz.shape
    tm = ROW_TILE
    rows = n_tiles * tm
    dff = w_down.shape[0]
    cf = FFN_COL_CHUNK
    k1, k2 = o1.shape[-1], o2.shape[-1]
    ohalo = 2 * SUBLANES
    kern = functools.partial(_post_kernel, tm=tm, n_x_tiles=n_x_tiles, n_tiles=n_tiles, cf=cf, dff=dff)

    def with_halos(width, halo_rows):
        per_tile = tm // halo_rows
        n_blocks = rows // halo_rows
        return [
            pl.BlockSpec((1, halo_rows, width), lambda t, i: (i, jnp.maximum(t * per_tile - 1, 0), 0)),
            pl.BlockSpec((1, tm, width), lambda t, i: (i, t, 0)),
            pl.BlockSpec((1, halo_rows, width), lambda t, i: (i, jnp.minimum((t + 1) * per_tile, n_blocks - 1), 0)),
        ]

    row_vec = pl.BlockSpec((1, d), lambda t, i: (0, 0))
    return pl.pallas_call(
        kern,
        grid=(n_tiles, b),
        in_specs=with_halos(k1, ohalo) + with_halos(k2, ohalo) + with_halos(d, SUBLANES) + [
            pl.BlockSpec((1, 1, 6, d), lambda t, i: (i, t // n_x_tiles, 0, 0)),
            row_vec, row_vec, row_vec,
            _resident((k1 + k2, d)),
            _resident((d, 2 * dff)),
            pl.BlockSpec((3, 2 * dff), lambda t, i: (0, 0)),
            _resident((dff, d)),
        ],
        out_specs=pl.BlockSpec((1, tm, d), lambda t, i: (i, t, 0)),
        out_shape=jax.ShapeDtypeStruct((b, rows, d), F32),
        scratch_shapes=[pltpu.VMEM((2, tm + 2 * SUBLANES, cf), F32)],
        compiler_params=_cparams(("arbitrary", "arbitrary")),
        name="mixer_out_conv_ffn",
    )(o1, o1, o1, o2, o2, o2, xz, xz, xz, mod, g1.reshape(1, d), g2.reshape(1, d), g3.reshape(1, d),
      w_out, w_up, conv_w, w_down)


def _half_sums(x2, lane_lo):
    s0 = jnp.sum(jnp.where(lane_lo, x2, 0.0), axis=-1, keepdims=True)
    s1 = jnp.sum(jnp.where(lane_lo, 0.0, x2), axis=-1, keepdims=True)
    return jnp.where(lane_lo, s0, s1)


def _gdn_kernel(qkvg_ref, ba_ref, cw_ref, gp_ref, ng_ref, out_ref,
                pad_ref, q_ref, k_ref, v_ref, bb_ref, gb_ref, qe_ref, mp_ref, ou_ref, nn_ref, egl_ref, o_ref,
                *, n, nc, chunks_per_iter):
    l = n + nc
    c = GDN_CHUNK
    n_chunks = l // c
    pair = pl.program_id(1)
    halo = SUBLANES
    lane = lax.broadcasted_iota(jnp.int32, (1, LANES), 1)
    lane_lo = lane < GDN_DIM

    cw = cw_ref[:, 0:3 * LANES]
    zero_rows = jnp.zeros((halo, 3 * LANES), F32)
    for seq_start, seq_len in ((0, n), (n, nc)):
        base = halo + seq_start + (2 * halo if seq_start else 0)
        pad_ref[base - halo:base, :] = zero_rows
        pad_ref[base + seq_len:base + seq_len + halo, :] = zero_rows
        step = 256
        for r in range(0, seq_len, step):
            pad_ref[base + r:base + r + step, :] = qkvg_ref[0, seq_start + r:seq_start + r + step, 0:3 * LANES]
        for r in range(0, seq_len, step):
            y = (cw[0:1] * pad_ref[base + r - 1:base + r - 1 + step, :]
                 + cw[1:2] * pad_ref[base + r:base + r + step, :]
                 + cw[2:3] * pad_ref[base + r + 1:base + r + 1 + step, :])
            y = _silu(y)
            q = y[:, 0:LANES]
            k = y[:, LANES:2 * LANES]
            rows = slice(seq_start + r, seq_start + r + step)
            q_ref[rows, :] = q * lax.rsqrt(_half_sums(q * q, lane_lo) + EPS) * (GDN_DIM ** -0.5)
            k_ref[rows, :] = k * lax.rsqrt(_half_sums(k * k, lane_lo) + EPS)
            v_ref[rows, :] = y[:, 2 * LANES:3 * LANES]

    sel_r = lax.broadcasted_iota(jnp.int32, (LANES, 4 * LANES), 0)
    sel_c = lax.broadcasted_iota(jnp.int32, (LANES, 4 * LANES), 1)
    quarter = sel_c >> _log2(LANES)
    src_lane = (quarter & 1) * 2 * GDN_HEADS + (quarter >> 1) * GDN_HEADS + 2 * pair + ((sel_c >> _log2(GDN_DIM)) & 1)
    sel = (sel_r == src_lane).astype(BF16)
    gblk = 256
    bi = lax.broadcasted_iota(jnp.int32, (gblk, gblk), 0)
    bj = lax.broadcasted_iota(jnp.int32, (gblk, gblk), 1)
    same_chunk = (bi >> _log2(c)) == (bj >> _log2(c))
    csum = (jnp.logical_and(same_chunk, bi >= bj).astype(BF16), jnp.logical_and(same_chunk, bi <= bj).astype(BF16))
    neg_a = -jnp.exp(gp_ref[0:1, :])
    dt_bias = gp_ref[1:2, :]
    for r in range(0, l, gblk):
        ba = ba_ref[0, r:r + gblk, :]
        gates = jnp.where(lane < 2 * GDN_HEADS, _sigmoid(ba), neg_a * _softplus(ba + dt_bias))
        x = _dot_sel(gates, sel)
        for d in range(2):
            bb_ref[d, r:r + gblk, :] = x[:, 2 * d * LANES:(2 * d + 1) * LANES]
            gb_ref[d, r:r + gblk, :] = _dot_sel_lhs(csum[d], x[:, (2 * d + 1) * LANES:(2 * d + 2) * LANES])

    r2 = lax.broadcasted_iota(jnp.int32, (2 * c, 2 * c), 0)
    c2 = lax.broadcasted_iota(jnp.int32, (2 * c, 2 * c), 1)
    same_head = (r2 >= c) == (c2 >= c)
    eye = (r2 == c2).astype(F32)
    masks = ((jnp.logical_and(same_head, r2 >= c2), jnp.logical_and(same_head, r2 > c2)),
             (jnp.logical_and(same_head, r2 <= c2), jnp.logical_and(same_head, r2 < c2)))
    m0 = lane_lo.astype(F32)
    m1 = 1.0 - m0

    def pair_mask(lv, lower):
        same_block = (r2 >> (lv + 1)) == (c2 >> (lv + 1))
        r_hi = ((r2 >> lv) & 1) == 1
        c_hi = ((c2 >> lv) & 1) == 1
        off = jnp.logical_and(r_hi, jnp.logical_not(c_hi)) if lower else jnp.logical_and(c_hi, jnp.logical_not(r_hi))
        return jnp.logical_and(same_block, off)

    pair_masks = tuple(tuple(pair_mask(lv, lower) for lv in range(_log2(c))) for lower in (True, False))

    def stack_heads(x2):
        return jnp.concatenate([x2 * m0, x2 * m1], axis=0)

    def fold_heads(x):
        return x[0:c] + x[c:2 * c]

    def local_stages(dirs, qs, ks, vs, betas, gcs, out):
        each = lambda f, *cols: [f(*args) for args in zip(*cols)]
        incl = [masks[d][0] for d in dirs]
        strict = [masks[d][1] for d in dirs]
        g1 = each(lambda gc2: jnp.concatenate([gc2, gc2], axis=0), gcs)
        decay = each(lambda g, m: jnp.where(m, jnp.exp(jnp.where(m, g - g.T, 0.0)), 0.0), g1, incl)
        kb = each(lambda k, b: k * b, ks, betas)
        kst = each(lambda k: stack_heads(k).astype(BF16), ks)
        a_raw = each(lambda x, y: _dot_nt(stack_heads(x).astype(BF16), y), kb, kst)
        qk_raw = each(lambda x, y: _dot_nt(stack_heads(x).astype(BF16), y), qs, kst)
        yield
        qk = each(lambda m, x, dc: jnp.where(m, x * dc, 0.0).astype(BF16), incl, qk_raw, decay)
        a = each(lambda m, x, dc: jnp.where(m, x * dc, 0.0), strict, a_raw, decay)
        tinv = each(lambda d, x: eye - jnp.where(pair_masks[d][0], x, 0.0), dirs, a)
        for lv in range(1, _log2(c)):
            ta = each(lambda d, t, x: _mm(t, jnp.where(pair_masks[d][lv], x, 0.0)), dirs, tinv, a)
            yield
            tat = each(_mm, ta, tinv)
            yield
            tinv = each(lambda t, x: t - x, tinv, tat)
        egc = each(jnp.exp, gcs)
        rhs = each(lambda v, b, x, e: jnp.concatenate([stack_heads(v * b), stack_heads(x * e)], axis=1),
                   vs, betas, kb, egc)
        sol = each(_mm, tinv, rhs)
        yield
        u2 = each(lambda x: fold_heads(x[:, 0:LANES]), sol)
        w2 = each(lambda x: fold_heads(x[:, LANES:2 * LANES]), sol)
        gl = each(lambda d, gc2: gc2[c - 1:c, :] if d == 0 else gc2[0:1, :], dirs, gcs)
        ktail = each(lambda k, g, gc2: (k * jnp.exp(g - gc2)).astype(BF16), ks, gl, gcs)
        qwu = each(lambda x, w, u: _dot(x, jnp.concatenate([stack_heads(w), stack_heads(u)], axis=1).astype(BF16)),
                   qk, w2, u2)
        kwu = each(lambda x, w, u: _dot_tn(x, jnp.concatenate([w, u], axis=1).astype(BF16)), ktail, w2, u2)
        yield
        q_eff = each(lambda q, e, x: (q * e - fold_heads(x[:, 0:LANES])).astype(BF16), qs, egc, qwu)
        m_neg = each(lambda x: jnp.where(same_head, -x[:, 0:LANES], 0.0).astype(BF16), kwu)
        o_loc = each(lambda x: fold_heads(x[:, LANES:2 * LANES]), qwu)
        s_loc = each(lambda x: jnp.where(same_head, x[:, LANES:2 * LANES], 0.0), kwu)
        egl = each(lambda g: jnp.broadcast_to(jnp.exp(g), (SUBLANES, LANES)), gl)
        out.extend(zip(q_eff, m_neg, o_loc, s_loc, egl))

    def chunk_rows(chunk, rows_per_chunk):
        return pl.ds(pl.multiple_of(chunk * rows_per_chunk, rows_per_chunk), rows_per_chunk)

    ctx_chunks = nc // c
    per_group = chunks_per_iter
    n_groups = n_chunks // per_group

    def chunks_at(step):
        return jnp.where(step < ctx_chunks, step + n // c, step - ctx_chunks), n_chunks - 1 - step

    def run_group(local_group, scan_group, states):
        dirs, chunks, qs, ks, vs, betas, gcs = [], [], [], [], [], [], []
        if local_group is not None:
            for g in range(per_group):
                for d, chunk in enumerate(chunks_at(per_group * local_group + g)):
                    rows = chunk_rows(chunk, c)
                    dirs.append(d)
                    chunks.append(chunk)
                    qs.append(q_ref[rows, :])
                    ks.append(k_ref[rows, :])
                    vs.append(v_ref[rows, :])
                    betas.append(bb_ref[d, rows, :])
                    gcs.append(gb_ref[d, rows, :])
        scan_chunks, scan_in = [], []
        if scan_group is not None:
            for g in range(per_group):
                step_chunks = chunks_at(per_group * scan_group + g)
                scan_chunks.append(step_chunks)
                scan_in.append([(qe_ref[d, chunk_rows(ch, c), :], mp_ref[d, chunk_rows(ch, 2 * c), :],
                                 ou_ref[d, chunk_rows(ch, c), :], nn_ref[d, chunk_rows(ch, 2 * c), :],
                                 egl_ref[d, chunk_rows(ch, SUBLANES), :]) for d, ch in enumerate(step_chunks)])
        scan_out = []

        def scan_step(states):
            loaded = scan_in[len(scan_out)]
            res = [_dot(jnp.concatenate([ld[0], ld[1]], axis=0), s2.astype(BF16)) for ld, s2 in zip(loaded, states)]
            scan_out.append([r[0:c] + ld[2] for r, ld in zip(res, loaded)])
            return tuple(s2 * ld[4][0:1] + r[c:3 * c] + ld[3] for s2, ld, r in zip(states, loaded, res))

        local_out = []
        stages = local_stages(dirs, qs, ks, vs, betas, gcs, local_out) if local_group is not None else iter(())
        for stage, _ in enumerate(stages):
            if scan_group is not None and stage % 3 == 0 and len(scan_out) < per_group:
                states = scan_step(states)
        while scan_group is not None and len(scan_out) < per_group:
            states = scan_step(states)
        for d, chunk, (q_eff, m_neg, o_loc, s_loc, egl) in zip(dirs, chunks, local_out):
            qe_ref[d, chunk_rows(chunk, c), :] = q_eff
            mp_ref[d, chunk_rows(chunk, 2 * c), :] = m_neg
            ou_ref[d, chunk_rows(chunk, c), :] = o_loc
            nn_ref[d, chunk_rows(chunk, 2 * c), :] = s_loc
            egl_ref[d, chunk_rows(chunk, SUBLANES), :] = egl
        for step_chunks, outs in zip(scan_chunks, scan_out):
            for d, ch in enumerate(step_chunks):
                o_ref[d, chunk_rows(ch, c), :] = outs[d]
        return states

    zero_state = jnp.zeros((2 * c, 2 * c), F32)
    states = run_group(0, None, (zero_state, zero_state))
    states = lax.fori_loop(1, n_groups, lambda j, st: run_group(j, j - 1, st), states)
    run_group(None, n_groups - 1, states)

    ng = ng_ref[...]
    step = 256
    for r in range(0, l, step):
        o = o_ref[0, r:r + step, :] + o_ref[1, r:r + step, :]
        ms = _half_sums(o * o, lane_lo) * (1.0 / GDN_DIM)
        gate = qkvg_ref[0, r:r + step, 3 * LANES:4 * LANES]
        out_ref[0, r:r + step, :] = (o * lax.rsqrt(ms + EPS) * ng * _silu(gate)).astype(out_ref.dtype)


def _dot_sel_lhs(sel_bf16, x):
    hi, mid, lo = _split3(x)
    return _dot(sel_bf16, hi) + _dot(sel_bf16, mid) + _dot(sel_bf16, lo)


def _gdn_call(qkvg, ba, conv_w, gate_params, ng, n, nc):
    b, l, _ = qkvg.shape
    pairs = GDN_HEADS // 2
    n_chunks = l // GDN_CHUNK
    kern = functools.partial(_gdn_kernel, n=n, nc=nc, chunks_per_iter=4)
    return pl.pallas_call(
        kern,
        grid=(b, pairs),
        in_specs=[
            pl.BlockSpec((1, l, 4 * LANES), lambda i, p: (i, 0, p)),
            pl.BlockSpec((1, l, LANES), lambda i, p: (i, 0, 0)),
            pl.BlockSpec((3, 4 * LANES), lambda i, p: (0, p)),
            pl.BlockSpec((2, LANES), lambda i, p: (0, 0)),
            pl.BlockSpec((1, LANES), lambda i, p: (0, 0)),
        ],
        out_specs=pl.BlockSpec((1, l, LANES), lambda i, p: (i, 0, p)),
        out_shape=jax.ShapeDtypeStruct((b, l, pairs * LANES), MIXER_OUT_DTYPE),
        scratch_shapes=[
            pltpu.VMEM((l + 5 * SUBLANES, 3 * LANES), F32),
            pltpu.VMEM((l, LANES), F32),
            pltpu.VMEM((l, LANES), F32),
            pltpu.VMEM((l, LANES), F32),
            pltpu.VMEM((2, l, LANES), F32),
            pltpu.VMEM((2, l, LANES), F32),
            pltpu.VMEM((2, l, LANES), BF16),
            pltpu.VMEM((2, 2 * l, LANES), BF16),
            pltpu.VMEM((2, l, LANES), F32),
            pltpu.VMEM((2, 2 * l, LANES), F32),
            pltpu.VMEM((2, n_chunks * SUBLANES, LANES), F32),
            pltpu.VMEM((2, l, LANES), F32),
        ],
        compiler_params=_cparams(("arbitrary", "arbitrary")),
        name="gated_deltanet",
    )(qkvg, ba, conv_w, gate_params, ng)


def _diff_kernel(*refs, key_start, n_sub, lam_init, aliased):
    if aliased:
        refs = refs[1:]
    q_ref, k_ref, vt_ref, lam_ref, ng_ref, o_ref = refs
    lp = lam_ref[...]
    lam = (jnp.exp(jnp.sum(lp[0:1] * lp[1:2], axis=-1, keepdims=True))
           - jnp.exp(jnp.sum(lp[2:3] * lp[3:4], axis=-1, keepdims=True)) + lam_init)
    lane = lax.broadcasted_iota(jnp.int32, (1, LANES), 1)
    halves = (lane < DIFF_DIM, lane >= DIFF_DIM)
    ng = ng_ref[...]
    k = k_ref[0, key_start:, :]
    vt = vt_ref[0, :, key_start:]
    tq = q_ref.shape[1] // n_sub

    def scores_of(i):
        q = q_ref[0, i * tq:(i + 1) * tq, :]
        return [_dot_nt(k, jnp.where(m, q, jnp.zeros_like(q))) for m in halves]

    ahead = scores_of(0)
    for i in range(n_sub):
        s = ahead
        if i + 1 < n_sub:
            ahead = scores_of(i + 1)
        e = [jnp.exp(x - jnp.max(x, axis=0, keepdims=True)) for x in s]
        pv = [_dot(vt, x.astype(BF16)) for x in e]
        parts = [x * (1.0 / jnp.sum(y, axis=0, keepdims=True)) for x, y in zip(pv, e)]
        ot = parts[0] - lam * parts[1]
        ot = ot * lax.rsqrt(jnp.mean(ot * ot, axis=0, keepdims=True) + EPS)
        o_ref[0, i * tq:(i + 1) * tq, :] = (ot.T * ng * (1.0 - lam_init)).astype(o_ref.dtype)


def _diff_call(dq, dk, dvt, lam_p, ng, lam_init, q_rows, first_block, n_q_blocks, key_start, n_sub, prev_out):
    b, l, _ = dq.shape
    aliased = prev_out is not None
    kern = functools.partial(_diff_kernel, key_start=key_start, n_sub=n_sub, lam_init=lam_init, aliased=aliased)
    row_of = lambda t: first_block + t
    in_specs = [
        pl.BlockSpec((1, q_rows, LANES), lambda i, h, t: (i, row_of(t), h)),
        pl.BlockSpec((1, l, LANES), lambda i, h, t: (i, 0, h)),
        pl.BlockSpec((1, LANES, l), lambda i, h, t: (i, h, 0)),
        pl.BlockSpec((4, DIFF_DIM), lambda i, h, t: (0, 0)),
        pl.BlockSpec((1, LANES), lambda i, h, t: (0, 0)),
    ]
    args = [dq, dk, dvt, lam_p, ng.reshape(1, LANES)]
    aliases = {}
    if aliased:
        in_specs = [pl.BlockSpec(memory_space=pl.ANY)] + in_specs
        args = [prev_out] + args
        aliases = {0: 0}
    return pl.pallas_call(
        kern,
        grid=(b, DIFF_HEADS, n_q_blocks),
        in_specs=in_specs,
        out_specs=pl.BlockSpec((1, q_rows, LANES), lambda i, h, t: (i, row_of(t), h)),
        out_shape=jax.ShapeDtypeStruct((b, l, DIFF_HEADS * LANES), MIXER_OUT_DTYPE),
        input_output_aliases=aliases,
        compiler_params=_cparams(("arbitrary", "arbitrary", "arbitrary")),
        name="diff_attention_ctx" if aliased else "diff_attention",
    )(*args)


def _swa_kernel(q_ref, k_ref, v_ref, sink_ref, o_ref, *, n, nc):
    t = pl.program_id(1)
    blk = SWA_BLOCK
    n_x = n // blk
    q = q_ref[0]
    lane = lax.broadcasted_iota(jnp.int32, (1, LANES), 1)
    lane_lo = lane < HEAD_DIM
    sink = sink_ref[...]
    group = SWA_HEADS // SWA_KV_HEADS

    def run(keys, vals, valid):
        head_of_row = lax.broadcasted_iota(jnp.int32, (group * blk, 1), 0) >> _log2(blk)
        kvs = range(SWA_KV_HEADS)
        kk = [keys[:, kvh * LANES:(kvh + 1) * LANES] for kvh in kvs]
        vv = [vals[:, kvh * LANES:(kvh + 1) * LANES] for kvh in kvs]
        qst, sk = [], []
        for kvh in kvs:
            q_rows = []
            sk_rows = jnp.zeros((group * blk, 1), F32)
            for g in range(group):
                h = kvh * group + g
                qp = q[:, (h // 2) * LANES:(h // 2 + 1) * LANES]
                q_rows.append(jnp.where(lane_lo if h % 2 == 0 else jnp.logical_not(lane_lo), qp, jnp.zeros_like(qp)))
                sk_rows = jnp.where(head_of_row == g, sink[:, h:h + 1], sk_rows)
            qst.append(jnp.concatenate(q_rows, axis=0))
            sk.append(sk_rows)
        s = [_dot_nt(x, y) for x, y in zip(qst, kk)]
        if valid is not None:
            s = [jnp.where(valid, x, NEG_INF) for x in s]
        mx = [jnp.maximum(jnp.max(x, axis=-1, keepdims=True), y) for x, y in zip(s, sk)]
        e = [jnp.exp(x - m) for x, m in zip(s, mx)]
        pv = [_dot(x.astype(BF16), y) for x, y in zip(e, vv)]
        den = [jnp.sum(x, axis=-1, keepdims=True) + jnp.exp(y - m) for x, y, m in zip(e, sk, mx)]
        outs = []
        for o, dn in zip(pv, den):
            o = o * (1.0 / dn)
            for g in range(0, group, 2):
                outs.append(jnp.where(lane_lo, o[g * blk:(g + 1) * blk], o[(g + 1) * blk:(g + 2) * blk]))
        o_ref[0] = jnp.concatenate(outs, axis=1).astype(o_ref.dtype)

    @pl.when(t < n_x)
    def _():
        start = pl.multiple_of(jnp.clip((t - 1) * blk, 0, n - 3 * blk), blk)
        keys = jnp.concatenate([k_ref[0, pl.ds(start, 3 * blk), :], k_ref[0, n:n + nc, :]], axis=0)
        vals = jnp.concatenate([v_ref[0, pl.ds(start, 3 * blk), :], v_ref[0, n:n + nc, :]], axis=0)
        shape = (group * blk, 3 * blk + nc)
        qpos = t * blk + (lax.broadcasted_iota(jnp.int32, shape, 0) & (blk - 1))
        col = lax.broadcasted_iota(jnp.int32, shape, 1)
        dist = qpos - (start + col)
        in_window = jnp.logical_and(dist <= SWA_WINDOW, dist >= -SWA_WINDOW)
        valid = jnp.logical_or(col >= 3 * blk, in_window)
        run(keys, vals, valid)

    @pl.when(t >= n_x)
    def _():
        run(k_ref[0, n:n + nc, :], v_ref[0, n:n + nc, :], None)


def _swa_call(q, k, v, sink, n, nc, with_ctx):
    b, l, _ = q.shape
    blk = SWA_BLOCK
    nt = (l if with_ctx else n) // blk
    kern = functools.partial(_swa_kernel, n=n, nc=nc)
    return pl.pallas_call(
        kern,
        grid=(b, nt),
        in_specs=[
            pl.BlockSpec((1, blk, SWA_HEADS * HEAD_DIM), lambda i, t: (i, t, 0)),
            pl.BlockSpec((1, l, 2 * LANES), lambda i, t: (i, 0, 0)),
            pl.BlockSpec((1, l, 2 * LANES), lambda i, t: (i, 0, 0)),
            pl.BlockSpec((1, LANES), lambda i, t: (0, 0)),
        ],
        out_specs=pl.BlockSpec((1, blk, SWA_HEADS * HEAD_DIM), lambda i, t: (i, t, 0)),
        out_shape=jax.ShapeDtypeStruct((b, l, SWA_HEADS * HEAD_DIM), MIXER_OUT_DTYPE),
        compiler_params=_cparams(("arbitrary", "arbitrary")),
        name="window_attention",
    )(q, k, v, sink)


def _dft_tables(n):
    h = n // 2
    r = 1 << (_log2(h) // 2)
    j = jnp.arange(h, dtype=jnp.int32)

    def tables(m):
        thin = lambda k: ((k[:, None] * m[None, :]) % (2 * n)).astype(F32) * (math.pi / n)
        a = thin(r * jnp.arange(h // r, dtype=jnp.int32))[:, None, :]
        b = thin(jnp.arange(r, dtype=jnp.int32))[None, :, :]
        cos = jnp.cos(a) * jnp.cos(b) - jnp.sin(a) * jnp.sin(b)
        sin = jnp.sin(a) * jnp.cos(b) + jnp.cos(a) * jnp.sin(b)
        return cos.reshape(h, h).astype(BF16), (-sin).reshape(h, h).astype(BF16)

    ce, se = tables(2 * j)
    co, so = tables(2 * j + 1)
    return ce, se, co, so, co.T, so.T


def _hyena_filter_kernel(feat_ref, w1_ref, b1_ref, w2_ref, b2_ref, freq_ref, w3f_ref, w3b_ref, dl_ref,
                         ce_ref, se_ref, co_ref, so_ref, ka_ref, kb_ref, km_ref):
    h = feat_ref.shape[1]
    assert h % 2 == 0
    freq = freq_ref[...]
    dl = dl_ref[...]
    row = lax.broadcasted_iota(jnp.int32, (h, 1), 0)

    def taps(part, w3_ref):
        feat = feat_ref[part]
        x = jnp.sin(freq[0:1] * (_dot_f32(feat, w1_ref[...]) + b1_ref[...]))
        x = jnp.sin(freq[1:2] * (_dot_f32(x, w2_ref[...]) + b2_ref[...]))
        return _dot_f32(x, w3_ref[...]) * jnp.exp(-feat[:, 0:1] * dl)

    fe, fo = taps(0, w3f_ref), taps(1, w3f_ref)
    be, bo = jnp.where(row == 0, 0.0, taps(2, w3b_ref)), taps(3, w3b_ref)
    ss = sum(jnp.sum(x * x, axis=0, keepdims=True) for x in (fe, fo, be, bo))
    sc = lax.rsqrt(ss + EPS)
    fe, fo, be, bo = (x * sc for x in (fe, fo, be, bo))
    sgn = jnp.where((row & 1) == 0, 1.0, -1.0)

    def dot2(t_ref, x):
        hi, lo = _split2(x)
        return _dot(t_ref[...], hi) + _dot(t_ref[...], lo)

    def bins(x_even, x_odd):
        ce, co = dot2(ce_ref, x_even), dot2(co_ref, x_odd)
        se, so = dot2(se_ref, x_even), dot2(so_ref, x_odd)
        return ce + co, se + so, ce - co, so - se

    f = bins(fe, fo)
    g = bins(be, bo)
    ka_ref[0, 0], ka_ref[0, 1], kb_ref[0, 0], kb_ref[0, 1] = (x + sgn * y for x, y in zip(f, g))
    mid_r = jnp.sum((fe + be) * sgn, axis=0, keepdims=True)
    mid_i = -jnp.sum((fo + bo) * sgn, axis=0, keepdims=True)
    km_ref[0] = jnp.concatenate([mid_r, mid_i, jnp.zeros((SUBLANES - 2, mid_r.shape[-1]), F32)], axis=0)


def _hyena_filter_call(feats, w1, b1, w2, b2, freq, w3, deltas, tables):
    h = feats.shape[1]
    hid = w2.shape[0]
    ch = deltas.shape[-1]
    tc = MXU_WIDTH
    nct = ch // tc
    const = lambda shape: pl.BlockSpec(shape, lambda o, j: (0,) * len(shape))
    spectrum = pl.BlockSpec((1, 2, h, tc), lambda o, j: (o, 0, 0, j))
    return pl.pallas_call(
        _hyena_filter_kernel,
        grid=(2, nct),
        in_specs=[
            const((4, h, hid)), const((hid, hid)), const((1, hid)), const((hid, hid)), const((1, hid)),
            const((2, hid)),
            pl.BlockSpec((hid, tc), lambda o, j: (0, (2 * o) * nct + j)),
            pl.BlockSpec((hid, tc), lambda o, j: (0, (2 * o + 1) * nct + j)),
            pl.BlockSpec((1, tc), lambda o, j: (0, j)),
        ] + [_resident((h, h))] * 4,
        out_specs=[spectrum, spectrum, pl.BlockSpec((1, SUBLANES, tc), lambda o, j: (o, 0, j))],
        out_shape=[
            jax.ShapeDtypeStruct((2, 2, h, ch), F32),
            jax.ShapeDtypeStruct((2, 2, h, ch), F32),
            jax.ShapeDtypeStruct((2, SUBLANES, ch), F32),
        ],
        compiler_params=_cparams(("arbitrary", "arbitrary")),
        name="hyena_filters",
    )(feats, w1, b1, w2, b2, freq, w3, w3, deltas, *tables[:4])


def _hyena_kernel(*refs, n, aliased):
    if aliased:
        refs = refs[1:]
    (v_ref, x1_ref, x2_ref, cwv_ref, cw1_ref, cw2_ref, ka_ref, kb_ref, km_ref, bias_ref,
     ce_ref, se_ref, co_ref, so_ref, cot_ref, sot_ref, o_ref, pad_ref, z_ref, zb_ref, p_ref, y_ref) = refs
    halo = SUBLANES
    tc = o_ref.shape[-1]
    h = n // 2
    rc = min(h, HY_ROW_CHUNK)
    lane_groups = tc // LANES
    zero_rows = jnp.zeros((halo, LANES), F32)
    for g in range(lane_groups):
        pad_ref[g, 0:halo, :] = zero_rows
        pad_ref[g, halo + n:2 * halo + n, :] = zero_rows

    def stage(ref):
        for r in range(0, n, 2 * rc):
            for g in range(lane_groups):
                pad_ref[g, halo + r:halo + r + 2 * rc, :] = ref[0, r:r + 2 * rc, g * LANES:(g + 1) * LANES]

    def conv_rows(cw, parity, r):
        first = halo + 2 * r + parity - 1
        taps = [jnp.concatenate([pad_ref[g, pl.ds(first + i, rc, stride=2), :] for g in range(lane_groups)], axis=1)
                for i in range(3)]
        return cw[0:1] * taps[0] + cw[1:2] * taps[1] + cw[2:3] * taps[2]

    def alt_sign(r):
        j = r + lax.broadcasted_iota(jnp.int32, (rc, 1), 0)
        return j, jnp.where((j & 1) == 0, 1.0, -1.0)

    stage(v_ref)
    cw = cwv_ref[...]
    for parity in range(2):
        for r in range(0, h, rc):
            z = conv_rows(cw, parity, r)
            z_ref[parity, r:r + rc, :] = z
            zb_ref[parity, r:r + rc, :] = z.astype(BF16)

    for o, (gate_ref, gate_cw_ref) in enumerate(((x1_ref, cw1_ref), (x2_ref, cw2_ref))):
        mid_r = jnp.zeros((1, tc), F32)
        mid_i = jnp.zeros((1, tc), F32)
        for r in range(0, h, rc):
            _, sgn = alt_sign(r)
            mid_r = mid_r + jnp.sum(z_ref[0, r:r + rc, :] * sgn, axis=0, keepdims=True)
            mid_i = mid_i - jnp.sum(z_ref[1, r:r + rc, :] * sgn, axis=0, keepdims=True)
        km_r = km_ref[o, 0:1, :]
        km_i = km_ref[o, 1:2, :]
        pm_r = (mid_r * km_r - mid_i * km_i) * (1.0 / n)
        pm_i = (mid_r * km_i + mid_i * km_r) * (1.0 / n)
        ze = zb_ref[0]
        zo = zb_ref[1]
        for r in range(0, h, rc):
            k, _ = alt_sign(r)
            rows = slice(r, r + rc)
            ce, co = _dot(ce_ref[rows, :], ze), _dot(co_ref[rows, :], zo)
            se, so = _dot(se_ref[rows, :], ze), _dot(so_ref[rows, :], zo)
            xa_r, xb_r, xa_i, xb_i = ce + co, ce - co, se + so, so - se
            wgt = jnp.where(k == 0, 0.5 / n, 1.0 / n)
            ka_r, ka_i = ka_ref[o, 0, rows, :], ka_ref[o, 1, rows, :]
            kb_r, kb_i = kb_ref[o, 0, rows, :], kb_ref[o, 1, rows, :]
            pa_r = (xa_r * ka_r - xa_i * ka_i) * wgt
            pa_i = (xa_r * ka_i + xa_i * ka_r) * wgt
            pb_r = (xb_r * kb_r - xb_i * kb_i) * wgt
            pb_i = (xb_r * kb_i + xb_i * kb_r) * wgt
            p_ref[0, rows, :] = (pa_r + pb_r).astype(BF16)
            p_ref[1, rows, :] = (pa_i - pb_i).astype(BF16)
            p_ref[2, rows, :] = (pa_r - pb_r).astype(BF16)
            p_ref[3, rows, :] = (pa_i + pb_i).astype(BF16)
        stage(gate_ref)
        cw = gate_cw_ref[...]
        bias = bias_ref[o:o + 1, :]
        for parity, (c_ref, s_ref, mid) in enumerate(((ce_ref, se_ref, pm_r), (cot_ref, sot_ref, -pm_i))):
            for r in range(0, h, rc):
                _, sgn = alt_sign(r)
                rows = slice(r, r + rc)
                y = (_dot(c_ref[rows, :], p_ref[2 * parity]) + _dot(s_ref[rows, :], p_ref[2 * parity + 1])
                     + sgn * mid)
                z = conv_rows(cw, parity, r) * (y + z_ref[parity, rows, :] * bias)
                if o == 0:
                    z_ref[parity, rows, :] = z
                    zb_ref[parity, rows, :] = z.astype(BF16)
                else:
                    for g in range(lane_groups):
                        y_ref[g, pl.ds(2 * r + parity, rc, stride=2), :] = z[:, g * LANES:(g + 1) * LANES]
    for r in range(0, n, 2 * rc):
        rows = slice(r, r + 2 * rc)
        o_ref[0, rows, :] = jnp.concatenate([y_ref[g, rows, :] for g in range(lane_groups)],
                                            axis=1).astype(o_ref.dtype)


def _hyena_call(u, conv_w, ka, kb, km, bias, tables, n, row_block, prev_out):
    b, l, _ = u.shape
    ch = bias.shape[-1]
    tc = MXU_WIDTH
    nct = ch // tc
    h = n // 2
    aliased = prev_out is not None
    kern = functools.partial(_hyena_kernel, n=n, aliased=aliased)
    once = pl.Buffered(1)
    in_specs = [
        pl.BlockSpec((1, n, tc), lambda j, i: (i, row_block, j)),
        pl.BlockSpec((1, n, tc), lambda j, i: (i, row_block, nct + j)),
        pl.BlockSpec((1, n, tc), lambda j, i: (i, row_block, 2 * nct + j)),
        pl.BlockSpec((3, tc), lambda j, i: (0, j)),
        pl.BlockSpec((3, tc), lambda j, i: (0, nct + j)),
        pl.BlockSpec((3, tc), lambda j, i: (0, 2 * nct + j)),
        pl.BlockSpec((2, 2, h, tc), lambda j, i: (0, 0, 0, j), pipeline_mode=once),
        pl.BlockSpec((2, 2, h, tc), lambda j, i: (0, 0, 0, j), pipeline_mode=once),
        pl.BlockSpec((2, SUBLANES, tc), lambda j, i: (0, 0, j)),
        pl.BlockSpec((2, tc), lambda j, i: (0, j)),
    ] + [_resident((h, h))] * 6
    args = [u, u, u, conv_w, conv_w, conv_w, ka, kb, km, bias, *tables]
    aliases = {}
    if aliased:
        in_specs = [pl.BlockSpec(memory_space=pl.ANY)] + in_specs
        args = [prev_out] + args
        aliases = {0: 0}
    return pl.pallas_call(
        kern,
        grid=(nct, b),
        in_specs=in_specs,
        out_specs=pl.BlockSpec((1, n, tc), lambda j, i: (i, row_block, j)),
        out_shape=jax.ShapeDtypeStruct((b, l, ch), MIXER_OUT_DTYPE),
        scratch_shapes=[
            pltpu.VMEM((tc // LANES, n + 2 * SUBLANES, LANES), F32),
            pltpu.VMEM((2, h, tc), F32),
            pltpu.VMEM((2, h, tc), BF16),
            pltpu.VMEM((4, h, tc), BF16),
            pltpu.VMEM((tc // LANES, n, LANES), F32),
        ],
        input_output_aliases=aliases,
        compiler_params=_cparams(("arbitrary", "arbitrary")),
        name="hyena_conv_n%d" % n,
    )(*args)


def _rope_tables(n, nc):
    rows = n // GRID_W
    row = jnp.repeat(jnp.arange(rows, dtype=F32), GRID_W)
    col = jnp.tile(jnp.arange(GRID_W, dtype=F32), rows)
    half = HEAD_DIM // 2
    inv = ROPE_BASE ** (-jnp.arange(0, half, 2, dtype=F32) / half)
    ar = row[:, None] * inv
    ac = col[:, None] * inv
    cos = jnp.concatenate([jnp.cos(ar), jnp.cos(ar), jnp.cos(ac), jnp.cos(ac)], axis=-1)
    sin = jnp.concatenate([-jnp.sin(ar), jnp.sin(ar), -jnp.sin(ac), jnp.sin(ac)], axis=-1)
    cos = jnp.concatenate([cos, jnp.ones((nc, HEAD_DIM), F32)], axis=0)
    sin = jnp.concatenate([sin, jnp.zeros((nc, HEAD_DIM), F32)], axis=0)
    return jnp.tile(cos, (1, LANES // HEAD_DIM)), jnp.tile(sin, (1, LANES // HEAD_DIM))


def _rope_partner_cols(width):
    d = np.arange(width)
    quarter = HEAD_DIM // 4
    return np.where((d % (2 * quarter)) < quarter, d + quarter, d - quarter)


def _hyena_feats(n):
    pos = jnp.arange(n, dtype=F32)
    t = pos / max(n - 1, 1)
    ang = (2.0 * math.pi * pos / n)[:, None] * jnp.linspace(1e-4, HY_BANDS - 1, HY_BANDS, dtype=F32)[None, :]
    feats = jnp.concatenate([t[:, None], jnp.cos(ang), -jnp.sin(ang)], axis=-1)
    feats = jnp.pad(feats, ((0, 0), (0, 64 - feats.shape[-1])))
    back = jnp.concatenate([feats[0:1], jnp.flip(feats[1:], axis=0)], axis=0)
    return jnp.stack([feats[0::2], feats[1::2], back[0::2], back[1::2]])


def _pad_cols(w, width):
    return jnp.pad(w, ((0, 0), (0, width - w.shape[-1])))


def _layer_ab(xz, mod, norm_g0, w_in, conv_w, a_log, dt_bias, gdn_g, lam_p, diff_g, lam_init, rope, n, nc):
    hd = GDN_HEADS * GDN_DIM
    wq, wk, wv, wg = (w_in[:, i * hd:(i + 1) * hd] for i in range(4))
    o = 4 * hd
    w_beta, w_alpha = w_in[:, o:o + 16], w_in[:, o + 16:o + 32]
    o += 32
    dd = DIFF_HEADS * 2 * DIFF_DIM
    wdq, wdk, wdv = (w_in[:, o + i * dd:o + (i + 1) * dd] for i in range(3))
    pairs = GDN_HEADS // 2
    pair_cols = lambda w: [w[:, p * LANES:(p + 1) * LANES] for p in range(pairs)]
    w_qkvg = jnp.concatenate([blk for grp in zip(pair_cols(wq), pair_cols(wk), pair_cols(wv), pair_cols(wg))
                              for blk in grp], axis=1)
    perm = _rope_partner_cols(dd)
    w_all = jnp.concatenate([w_qkvg, _pad_cols(jnp.concatenate([w_beta, w_alpha], axis=1), LANES),
                             wdq, wdk, wdv, wdq[:, perm], wdk[:, perm]], axis=1).astype(BF16)
    c0 = 4 * hd
    c1 = c0 + LANES
    segs = (_Seg(0, c0), _Seg(c0, LANES),
            _Seg(c1, dd, rot_start=c1 + 3 * dd, scale=DIFF_DIM ** -0.5, dtype=BF16),
            _Seg(c1 + dd, dd, rot_start=c1 + 4 * dd, dtype=BF16),
            _Seg(c1 + 2 * dd, dd, dtype=BF16, transposed=True))
    qkvg, ba, dq, dk, dvt = _proj_call(xz, mod, norm_g0, w_all, rope[0], rope[1], segs, n // ROW_TILE, "proj_ab")

    cq, ck, cv = (conv_w[:, i * hd:(i + 1) * hd] for i in range(3))
    zeros = jnp.zeros((3, LANES), F32)
    conv_l = jnp.concatenate([blk for p in range(pairs) for blk in
                              (cq[:, p * LANES:(p + 1) * LANES], ck[:, p * LANES:(p + 1) * LANES],
                               cv[:, p * LANES:(p + 1) * LANES], zeros)], axis=1)
    n_gate = 2 * GDN_HEADS
    on_decay_lanes = lambda t: jnp.pad(t.reshape(1, n_gate), ((0, 0), (n_gate, LANES - 2 * n_gate)))
    gate_params = jnp.concatenate([on_decay_lanes(a_log), on_decay_lanes(dt_bias)], axis=0)
    ng = jnp.tile(gdn_g.reshape(1, GDN_DIM), (1, 2))
    oa = _gdn_call(qkvg, ba, conv_l, gate_params, ng, n, nc)
    q_rows = DIFF_SUB_TILES * ROW_TILE
    ob = _diff_call(dq, dk, dvt, lam_p, diff_g, lam_init, q_rows, 0, n // q_rows, 0, DIFF_SUB_TILES, None)
    ob = _diff_call(dq, dk, dvt, lam_p, diff_g, lam_init, nc, n // nc, 1, n, 1, ob)
    return oa, ob


def _layer_cd(xz, mod, norm_g0, w_in, sink, hy_conv, hy_w1, hy_b1, hy_w2, hy_b2, hy_w3, hy_freq, hy_bias,
              rope, n, nc, last, dft_x, dft_c):
    qd = SWA_HEADS * HEAD_DIM
    kd = SWA_KV_HEADS * HEAD_DIM
    wq, wk, wv, wu = w_in[:, 0:qd], w_in[:, qd:qd + kd], w_in[:, qd + kd:qd + 2 * kd], w_in[:, qd + 2 * kd:]
    dup = lambda w: jnp.concatenate([w[:, 0:HEAD_DIM], w[:, 0:HEAD_DIM], w[:, HEAD_DIM:], w[:, HEAD_DIM:]], axis=1)
    wk2, wv2 = dup(wk), dup(wv)
    ud = wu.shape[1]
    w_all = jnp.concatenate([wq, wk2, wv2, wu, wq[:, _rope_partner_cols(qd)], wk2[:, _rope_partner_cols(2 * kd)]],
                            axis=1).astype(BF16)
    o_u = qd + 4 * kd
    segs = (_Seg(0, qd, rot_start=o_u + ud, scale=HEAD_DIM ** -0.5, dtype=BF16),
            _Seg(qd, 2 * kd, rot_start=o_u + ud + qd, dtype=BF16),
            _Seg(qd + 2 * kd, 2 * kd, dtype=BF16), _Seg(o_u, ud))
    q, k, v, u = _proj_call(xz, mod, norm_g0, w_all, rope[0], rope[1], segs, n // ROW_TILE, "proj_cd")
    oc = _swa_call(q, k, v, _pad_cols(sink.reshape(1, SWA_HEADS), LANES), n, nc, not last)

    ch = hy_bias.shape[-1]
    deltas = jnp.abs(jnp.linspace(HY_MIN_DECAY, HY_MAX_DECAY, ch, dtype=F32)).reshape(1, ch)
    hid = hy_w2.shape[0]
    w1p = jnp.pad(hy_w1, ((0, hid - hy_w1.shape[0]), (0, 0)))
    filt = lambda m, dft: _hyena_filter_call(_hyena_feats(m), w1p, hy_b1.reshape(1, hid), hy_w2,
                                             hy_b2.reshape(1, hid), hy_freq, hy_w3, deltas, dft)
    od = _hyena_call(u, hy_conv, *filt(n, dft_x), hy_bias, dft_x, n, 0, None)
    if not last:
        od = _hyena_call(u, hy_conv, *filt(nc, dft_c), hy_bias, dft_c, nc, n // nc, od)
    return oc, od


def kernel(x, c, ctx, c_ctx, w_mod, b_mod, norm_g, ffn_w_up, ffn_conv, ffn_w_down, ab_w_in, ab_w_out, gdn_conv, gdn_a_log, gdn_dt_bias, gdn_norm_g, diff_lambda, diff_norm_g, cd_w_in, cd_w_out, swa_sink, hy_conv, hy_w1, hy_b1, hy_w2, hy_b2, hy_w3, hy_freq, hy_bias):
    b, n, d = x.shape
    nc = ctx.shape[1]
    depth = w_mod.shape[0]
    assert n % ROW_TILE == 0 and nc == ROW_TILE and n % GRID_W == 0
    xz = jnp.concatenate([x, ctx], axis=1)
    rows = -(-(b + 1) // SUBLANES) * SUBLANES
    cc = jnp.concatenate([c, c_ctx[None], jnp.zeros((rows - b - 1, d), F32)], axis=0)
    mods = _mod_call(cc, w_mod, b_mod)
    mod_all = jnp.concatenate([mods[:, :b].reshape(depth, b, 1, 6, d),
                               jnp.broadcast_to(mods[:, b].reshape(depth, 1, 1, 6, d), (depth, b, 1, 6, d))], axis=2)
    rope = _rope_tables(n, nc)
    dft_x = _dft_tables(n)
    dft_c = _dft_tables(nc)
    n_x_tiles = n // ROW_TILE
    for l in range(depth):
        last = l == depth - 1
        i = l // 2
        mod = mod_all[l]
        if l % 2 == 0:
            lam_init = 0.8 - 0.6 * math.exp(-0.3 * l)
            o1, o2 = _layer_ab(xz, mod, norm_g[l, 0], ab_w_in[i], gdn_conv[i], gdn_a_log[i], gdn_dt_bias[i],
                               gdn_norm_g[i], diff_lambda[i], diff_norm_g[i], lam_init, rope, n, nc)
            w_out = ab_w_out[i]
        else:
            o1, o2 = _layer_cd(xz, mod, norm_g[l, 0], cd_w_in[i], swa_sink[i], hy_conv[i], hy_w1[i], hy_b1[i],
                               hy_w2[i], hy_b2[i], hy_w3[i], hy_freq[i], hy_bias[i], rope, n, nc, last, dft_x, dft_c)
            w_out = cd_w_out[i]
        n_tiles = (n if last else n + nc) // ROW_TILE
        xz = _post_call(o1, o2, xz, mod, norm_g[l, 1], norm_g[l, 2], norm_g[l, 3], w_out.astype(BF16),
                        ffn_w_up[l].astype(BF16), ffn_conv[l], ffn_w_down[l].astype(BF16), n_tiles, n_x_tiles)
    return xz
```

```python
import functools
import math
from typing import NamedTuple, Optional

import jax
import jax.numpy as jnp
import numpy as np
from jax import lax
from jax.experimental import pallas as pl
from jax.experimental.pallas import tpu as pltpu

F32 = jnp.float32
BF16 = jnp.bfloat16
MIXER_OUT_DTYPE = BF16

EPS = 1e-6
NEG_INF = -1e30
GRID_W = 64
HEAD_DIM = 64
ROPE_BASE = 10000.0
GDN_HEADS = 8
GDN_DIM = 64
GDN_CHUNK = 64
DIFF_HEADS = 4
DIFF_DIM = 64
DIFF_SUB_TILES = 4
SWA_HEADS = 8
SWA_KV_HEADS = 2
SWA_WINDOW = 128
SWA_BLOCK = 128
HY_BANDS = 16
HY_MIN_DECAY = math.log(1e-2) / 1.5
HY_MAX_DECAY = math.log(1e-2) / 0.3
HY_ROW_CHUNK = 512

LANES = 128
SUBLANES = 8
MXU_WIDTH = 256
FFN_COL_CHUNK = 6 * MXU_WIDTH
ROW_TILE = 256
VMEM_LIMIT = 56 * 1024 * 1024


def _cparams(sem):
    return pltpu.CompilerParams(dimension_semantics=sem, vmem_limit_bytes=VMEM_LIMIT)


def _resident(shape):
    zeros = (0,) * len(shape)
    return pl.BlockSpec(shape, lambda *_: zeros, pipeline_mode=pl.Buffered(1))


def _log2(v):
    assert v & (v - 1) == 0
    return v.bit_length() - 1


def _sigmoid(x):
    return 1.0 / (1.0 + jnp.exp(-x))


def _silu(x):
    return x * _sigmoid(x)


def _softplus(x):
    return jnp.maximum(x, 0.0) + jnp.log1p(jnp.exp(-jnp.abs(x)))


def _dot(a, b):
    return jnp.dot(a, b, preferred_element_type=F32)


def _dot_nt(a, b):
    return lax.dot_general(a, b, (((1,), (1,)), ((), ())), preferred_element_type=F32)


def _dot_tn(a, b):
    return lax.dot_general(a, b, (((0,), (0,)), ((), ())), preferred_element_type=F32)


def _dot_f32(a, b):
    return jnp.dot(a, b, preferred_element_type=F32, precision=lax.Precision.HIGHEST)


def _split2(x):
    hi = x.astype(BF16)
    lo = (x - hi.astype(F32)).astype(BF16)
    return hi, lo


def _dot_sel(x, sel_bf16):
    hi, lo = _split2(x)
    return _dot(hi, sel_bf16) + _dot(lo, sel_bf16)


def _mm(a, b):
    return _dot(a.astype(BF16), b.astype(BF16))


def _rms(y, g):
    return y * lax.rsqrt(jnp.mean(y * y, axis=-1, keepdims=True) + EPS) * g


def _modnorm(x, g, shift, scale):
    return _rms(x, g) * (1.0 + scale) + shift


def _mod_kernel(cc_ref, w_ref, b_ref, o_ref):
    s = _silu(cc_ref[...])
    o_ref[0] = _dot(s.astype(BF16), w_ref[0].astype(BF16)) + b_ref[0]


def _mod_call(cc, w_mod, b_mod):
    depth, d, nm = w_mod.shape
    rows = cc.shape[0]
    ct = 1536
    return pl.pallas_call(
        _mod_kernel,
        grid=(depth, nm // ct),
        in_specs=[
            pl.BlockSpec((rows, d), lambda l, j: (0, 0)),
            pl.BlockSpec((1, d, ct), lambda l, j: (l, 0, j)),
            pl.BlockSpec((1, 1, ct), lambda l, j: (l, 0, j)),
        ],
        out_specs=pl.BlockSpec((1, rows, ct), lambda l, j: (l, 0, j)),
        out_shape=jax.ShapeDtypeStruct((depth, rows, nm), F32),
        compiler_params=_cparams(("arbitrary", "arbitrary")),
        name="adaln_mod",
    )(cc, w_mod, b_mod.reshape(depth, 1, nm))


class _Seg(NamedTuple):
    start: int
    width: int
    rot_start: Optional[int] = None
    scale: float = 1.0
    dtype: type = F32
    transposed: bool = False


def _proj_kernel(x_ref, mod_ref, g_ref, w_ref, cos_ref, sin_ref, *out_refs, segs):
    m = mod_ref[0, 0]
    h = _modnorm(x_ref[0], g_ref[...], m[0:1], m[1:2]).astype(BF16)
    for o_ref, seg in zip(out_refs, segs):
        y = _dot(h, w_ref[:, seg.start:seg.start + seg.width])
        if seg.rot_start is not None:
            yr = _dot(h, w_ref[:, seg.rot_start:seg.rot_start + seg.width])
            reps = seg.width // LANES
            cos = jnp.concatenate([cos_ref[...]] * reps, axis=1)
            sin = jnp.concatenate([sin_ref[...]] * reps, axis=1)
            y = y * cos + yr * sin
        if seg.scale != 1.0:
            y = y * seg.scale
        if seg.transposed:
            y = y.T
        o_ref[0] = y.astype(seg.dtype)


def _proj_call(xz, mod, g, w, cos_t, sin_t, segs, n_x_tiles, name):
    b, l, d = xz.shape
    tm = ROW_TILE
    nt = l // tm
    p = w.shape[1]
    return pl.pallas_call(
        functools.partial(_proj_kernel, segs=segs),
        grid=(nt, b),
        in_specs=[
            pl.BlockSpec((1, tm, d), lambda t, i: (i, t, 0)),
            pl.BlockSpec((1, 1, 6, d), lambda t, i: (i, t // n_x_tiles, 0, 0)),
            pl.BlockSpec((1, d), lambda t, i: (0, 0)),
            _resident((d, p)),
            pl.BlockSpec((tm, LANES), lambda t, i: (t, 0)),
            pl.BlockSpec((tm, LANES), lambda t, i: (t, 0)),
        ],
        out_specs=[pl.BlockSpec((1, s.width, tm), lambda t, i: (i, 0, t)) if s.transposed
                   else pl.BlockSpec((1, tm, s.width), lambda t, i: (i, t, 0)) for s in segs],
        out_shape=[jax.ShapeDtypeStruct((b, s.width, l) if s.transposed else (b, l, s.width), s.dtype)
                   for s in segs],
        compiler_params=_cparams(("arbitrary", "arbitrary")),
        name=name,
    )(xz, mod, g.reshape(1, d), w, cos_t, sin_t)


def _post_kernel(o1p_ref, o1_ref, o1n_ref, o2p_ref, o2_ref, o2n_ref, xp_ref, x_ref, xn_ref, mod_ref,
                 g1_ref, g2_ref, g3_ref, wout_ref, wup_ref, cw_ref, wdn_ref, out_ref, up_ref,
                 *, tm, n_x_tiles, n_tiles, cf, dff):
    t = pl.program_id(0)
    first = jnp.logical_or(t == 0, t == n_x_tiles)
    last = jnp.logical_or(t == n_x_tiles - 1, t == n_tiles - 1)
    m = mod_ref[0, 0]
    halo = SUBLANES
    ohalo = o1p_ref.shape[1]
    k1 = o1_ref.shape[-1]
    o1e = jnp.concatenate([o1p_ref[0], o1_ref[0], o1n_ref[0]], axis=0)
    o2e = jnp.concatenate([o2p_ref[0], o2_ref[0], o2n_ref[0]], axis=0)
    y = _dot(o1e, wout_ref[0:k1, :]) + _dot(o2e, wout_ref[k1:, :])
    y = y[ohalo - halo:ohalo + tm + halo]
    xe = jnp.concatenate([xp_ref[0], x_ref[0], xn_ref[0]], axis=0)
    x1 = xe + m[2:3] * _rms(y, g1_ref[...])
    h = _modnorm(x1, g2_ref[...], m[3:4], m[4:5]).astype(BF16)
    acc = jnp.zeros((tm, x_ref.shape[-1]), F32)
    for c0 in range(0, dff, cf):
        wd = min(cf, dff - c0)
        halves = []
        for half, base in enumerate((c0, dff + c0)):
            u = _dot(h, wup_ref[:, base:base + wd])
            up_ref[half, :, 0:wd] = u
            up_ref[half, 0:halo, 0:wd] = jnp.where(first, 0.0, u[0:halo])
            up_ref[half, tm + halo:tm + 2 * halo, 0:wd] = jnp.where(last, 0.0, u[tm + halo:])
            cw = cw_ref[:, base:base + wd]
            halves.append(cw[0:1] * up_ref[half, halo - 1:halo - 1 + tm, 0:wd]
                          + cw[1:2] * up_ref[half, halo:halo + tm, 0:wd]
                          + cw[2:3] * up_ref[half, halo + 1:halo + 1 + tm, 0:wd])
        act = (_silu(halves[1]) * halves[0]).astype(BF16)
        acc = acc + _dot(act, wdn_ref[c0:c0 + wd, :])
    out_ref[0] = x1[halo:halo + tm] + m[5:6] * _rms(acc, g3_ref[...])


def _post_call(o1, o2, xz, mod, g1, g2, g3, w_out, w_up, conv_w, w_down, n_tiles, n_x_tiles):
    b, _, d = xz.shape
    tm = ROW_TILE
    rows = n_tiles * tm
    dff = w_down.shape[0]
    cf = FFN_COL_CHUNK
    k1, k2 = o1.shape[-1], o2.shape[-1]
    ohalo = 2 * SUBLANES
    kern = functools.partial(_post_kernel, tm=tm, n_x_tiles=n_x_tiles, n_tiles=n_tiles, cf=cf, dff=dff)

    def with_halos(width, halo_rows):
        per_tile = tm // halo_rows
        n_blocks = rows // halo_rows
        return [
            pl.BlockSpec((1, halo_rows, width), lambda t, i: (i, jnp.maximum(t * per_tile - 1, 0), 0)),
            pl.BlockSpec((1, tm, width), lambda t, i: (i, t, 0)),
            pl.BlockSpec((1, halo_rows, width), lambda t, i: (i, jnp.minimum((t + 1) * per_tile, n_blocks - 1), 0)),
        ]

    row_vec = pl.BlockSpec((1, d), lambda t, i: (0, 0))
    return pl.pallas_call(
        kern,
        grid=(n_tiles, b),
        in_specs=with_halos(k1, ohalo) + with_halos(k2, ohalo) + with_halos(d, SUBLANES) + [
            pl.BlockSpec((1, 1, 6, d), lambda t, i: (i, t // n_x_tiles, 0, 0)),
            row_vec, row_vec, row_vec,
            _resident((k1 + k2, d)),
            _resident((d, 2 * dff)),
            pl.BlockSpec((3, 2 * dff), lambda t, i: (0, 0)),
            _resident((dff, d)),
        ],
        out_specs=pl.BlockSpec((1, tm, d), lambda t, i: (i, t, 0)),
        out_shape=jax.ShapeDtypeStruct((b, rows, d), F32),
        scratch_shapes=[pltpu.VMEM((2, tm + 2 * SUBLANES, cf), F32)],
        compiler_params=_cparams(("arbitrary", "arbitrary")),
        name="mixer_out_conv_ffn",
    )(o1, o1, o1, o2, o2, o2, xz, xz, xz, mod, g1.reshape(1, d), g2.reshape(1, d), g3.reshape(1, d),
      w_out, w_up, conv_w, w_down)


def _half_sums(x2, lane_lo):
    s0 = jnp.sum(jnp.where(lane_lo, x2, 0.0), axis=-1, keepdims=True)
    s1 = jnp.sum(jnp.where(lane_lo, 0.0, x2), axis=-1, keepdims=True)
    return jnp.where(lane_lo, s0, s1)


def _gdn_kernel(qkvg_ref, ba_ref, cw_ref, gp_ref, ng_ref, out_ref,
                pad_ref, q_ref, k_ref, v_ref, bb_ref, gb_ref, qe_ref, mp_ref, ou_ref, nn_ref, egl_ref, o_ref,
                *, n, nc, chunks_per_iter):
    l = n + nc
    c = GDN_CHUNK
    n_chunks = l // c
    pair = pl.program_id(1)
    halo = SUBLANES
    lane = lax.broadcasted_iota(jnp.int32, (1, LANES), 1)
    lane_lo = lane < GDN_DIM

    cw = cw_ref[:, 0:3 * LANES]
    zero_rows = jnp.zeros((halo, 3 * LANES), F32)
    for seq_start, seq_len in ((0, n), (n, nc)):
        base = halo + seq_start + (2 * halo if seq_start else 0)
        pad_ref[base - halo:base, :] = zero_rows
        pad_ref[base + seq_len:base + seq_len + halo, :] = zero_rows
        step = 256
        for r in range(0, seq_len, step):
            pad_ref[base + r:base + r + step, :] = qkvg_ref[0, seq_start + r:seq_start + r + step, 0:3 * LANES]
        for r in range(0, seq_len, step):
            y = (cw[0:1] * pad_ref[base + r - 1:base + r - 1 + step, :]
                 + cw[1:2] * pad_ref[base + r:base + r + step, :]
                 + cw[2:3] * pad_ref[base + r + 1:base + r + 1 + step, :])
            y = _silu(y)
            q = y[:, 0:LANES]
            k = y[:, LANES:2 * LANES]
            rows = slice(seq_start + r, seq_start + r + step)
            q_ref[rows, :] = q * lax.rsqrt(_half_sums(q * q, lane_lo) + EPS) * (GDN_DIM ** -0.5)
            k_ref[rows, :] = k * lax.rsqrt(_half_sums(k * k, lane_lo) + EPS)
            v_ref[rows, :] = y[:, 2 * LANES:3 * LANES]

    sel_r = lax.broadcasted_iota(jnp.int32, (LANES, 4 * LANES), 0)
    sel_c = lax.broadcasted_iota(jnp.int32, (LANES, 4 * LANES), 1)
    quarter = sel_c >> _log2(LANES)
    src_lane = (quarter & 1) * 2 * GDN_HEADS + (quarter >> 1) * GDN_HEADS + 2 * pair + ((sel_c >> _log2(GDN_DIM)) & 1)
    sel = (sel_r == src_lane).astype(BF16)
    gblk = 256
    bi = lax.broadcasted_iota(jnp.int32, (gblk, gblk), 0)
    bj = lax.broadcasted_iota(jnp.int32, (gblk, gblk), 1)
    same_chunk = (bi >> _log2(c)) == (bj >> _log2(c))
    csum = (jnp.logical_and(same_chunk, bi >= bj).astype(BF16), jnp.logical_and(same_chunk, bi <= bj).astype(BF16))
    neg_a = -jnp.exp(gp_ref[0:1, :])
    dt_bias = gp_ref[1:2, :]
    for r in range(0, l, gblk):
        ba = ba_ref[0, r:r + gblk, :]
        gates = jnp.where(lane < 2 * GDN_HEADS, _sigmoid(ba), neg_a * _softplus(ba + dt_bias))
        x = _dot_sel(gates, sel)
        for d in range(2):
            bb_ref[d, r:r + gblk, :] = x[:, 2 * d * LANES:(2 * d + 1) * LANES]
            gb_ref[d, r:r + gblk, :] = _dot_sel_lhs(csum[d], x[:, (2 * d + 1) * LANES:(2 * d + 2) * LANES])

    r2 = lax.broadcasted_iota(jnp.int32, (2 * c, 2 * c), 0)
    c2 = lax.broadcasted_iota(jnp.int32, (2 * c, 2 * c), 1)
    same_head = (r2 >= c) == (c2 >= c)
    eye = (r2 == c2).astype(F32)
    masks = ((jnp.logical_and(same_head, r2 >= c2), jnp.logical_and(same_head, r2 > c2)),
             (jnp.logical_and(same_head, r2 <= c2), jnp.logical_and(same_head, r2 < c2)))
    m0 = lane_lo.astype(F32)
    m1 = 1.0 - m0

    def pair_mask(lv, lower):
        same_block = (r2 >> (lv + 1)) == (c2 >> (lv + 1))
        r_hi = ((r2 >> lv) & 1) == 1
        c_hi = ((c2 >> lv) & 1) == 1
        off = jnp.logical_and(r_hi, jnp.logical_not(c_hi)) if lower else jnp.logical_and(c_hi, jnp.logical_not(r_hi))
        return jnp.logical_and(same_block, off)

    pair_masks = tuple(tuple(pair_mask(lv, lower) for lv in range(_log2(c))) for lower in (True, False))

    def stack_heads(x2):
        return jnp.concatenate([x2 * m0, x2 * m1], axis=0)

    def fold_heads(x):
        return x[0:c] + x[c:2 * c]

    def local_stages(dirs, qs, ks, vs, betas, gcs, out):
        each = lambda f, *cols: [f(*args) for args in zip(*cols)]
        incl = [masks[d][0] for d in dirs]
        strict = [masks[d][1] for d in dirs]
        g1 = each(lambda gc2: jnp.concatenate([gc2, gc2], axis=0), gcs)
        decay = each(lambda g, m: jnp.where(m, jnp.exp(jnp.where(m, g - g.T, 0.0)), 0.0), g1, incl)
        kb = each(lambda k, b: k * b, ks, betas)
        kst = each(lambda k: stack_heads(k).astype(BF16), ks)
        a_raw = each(lambda x, y: _dot_nt(stack_heads(x).astype(BF16), y), kb, kst)
        qk_raw = each(lambda x, y: _dot_nt(stack_heads(x).astype(BF16), y), qs, kst)
        yield
        qk = each(lambda m, x, dc: jnp.where(m, x * dc, 0.0).astype(BF16), incl, qk_raw, decay)
        a = each(lambda m, x, dc: jnp.where(m, x * dc, 0.0), strict, a_raw, decay)
        tinv = each(lambda d, x: eye - jnp.where(pair_masks[d][0], x, 0.0), dirs, a)
        for lv in range(1, _log2(c)):
            ta = each(lambda d, t, x: _mm(t, jnp.where(pair_masks[d][lv], x, 0.0)), dirs, tinv, a)
            yield
            tat = each(_mm, ta, tinv)
            yield
            tinv = each(lambda t, x: t - x, tinv, tat)
        egc = each(jnp.exp, gcs)
        rhs = each(lambda v, b, x, e: jnp.concatenate([stack_heads(v * b), stack_heads(x * e)], axis=1),
                   vs, betas, kb, egc)
        sol = each(_mm, tinv, rhs)
        yield
        u2 = each(lambda x: fold_heads(x[:, 0:LANES]), sol)
        w2 = each(lambda x: fold_heads(x[:, LANES:2 * LANES]), sol)
        gl = each(lambda d, gc2: gc2[c - 1:c, :] if d == 0 else gc2[0:1, :], dirs, gcs)
        ktail = each(lambda k, g, gc2: (k * jnp.exp(g - gc2)).astype(BF16), ks, gl, gcs)
        qwu = each(lambda x, w, u: _dot(x, jnp.concatenate([stack_heads(w), stack_heads(u)], axis=1).astype(BF16)),
                   qk, w2, u2)
        kwu = each(lambda x, w, u: _dot_tn(x, jnp.concatenate([w, u], axis=1).astype(BF16)), ktail, w2, u2)
        yield
        q_eff = each(lambda q, e, x: (q * e - fold_heads(x[:, 0:LANES])).astype(BF16), qs, egc, qwu)
        m_neg = each(lambda x: jnp.where(same_head, -x[:, 0:LANES], 0.0).astype(BF16), kwu)
        o_loc = each(lambda x: fold_heads(x[:, LANES:2 * LANES]), qwu)
        s_loc = each(lambda x: jnp.where(same_head, x[:, LANES:2 * LANES], 0.0), kwu)
        egl = each(lambda g: jnp.broadcast_to(jnp.exp(g), (SUBLANES, LANES)), gl)
        out.extend(zip(q_eff, m_neg, o_loc, s_loc, egl))

    def chunk_rows(chunk, rows_per_chunk):
        return pl.ds(pl.multiple_of(chunk * rows_per_chunk, rows_per_chunk), rows_per_chunk)

    ctx_chunks = nc // c
    per_group = chunks_per_iter
    n_groups = n_chunks // per_group

    def chunks_at(step):
        return jnp.where(step < ctx_chunks, step + n // c, step - ctx_chunks), n_chunks - 1 - step

    def run_group(local_group, scan_group, states):
        dirs, chunks, qs, ks, vs, betas, gcs = [], [], [], [], [], [], []
        if local_group is not None:
            for g in range(per_group):
                for d, chunk in enumerate(chunks_at(per_group * local_group + g)):
                    rows = chunk_rows(chunk, c)
                    dirs.append(d)
                    chunks.append(chunk)
                    qs.append(q_ref[rows, :])
                    ks.append(k_ref[rows, :])
                    vs.append(v_ref[rows, :])
                    betas.append(bb_ref[d, rows, :])
                    gcs.append(gb_ref[d, rows, :])
        scan_chunks, scan_in = [], []
        if scan_group is not None:
            for g in range(per_group):
                step_chunks = chunks_at(per_group * scan_group + g)
                scan_chunks.append(step_chunks)
                scan_in.append([(qe_ref[d, chunk_rows(ch, c), :], mp_ref[d, chunk_rows(ch, 2 * c), :],
                                 ou_ref[d, chunk_rows(ch, c), :], nn_ref[d, chunk_rows(ch, 2 * c), :],
                                 egl_ref[d, chunk_rows(ch, SUBLANES), :]) for d, ch in enumerate(step_chunks)])
        scan_out = []

        def scan_step(states):
            loaded = scan_in[len(scan_out)]
            res = [_dot(jnp.concatenate([ld[0], ld[1]], axis=0), s2.astype(BF16)) for ld, s2 in zip(loaded, states)]
            scan_out.append([r[0:c] + ld[2] for r, ld in zip(res, loaded)])
            return tuple(s2 * ld[4][0:1] + r[c:3 * c] + ld[3] for s2, ld, r in zip(states, loaded, res))

        local_out = []
        stages = local_stages(dirs, qs, ks, vs, betas, gcs, local_out) if local_group is not None else iter(())
        for stage, _ in enumerate(stages):
            if scan_group is not None and stage % 3 == 0 and len(scan_out) < per_group:
                states = scan_step(states)
        while scan_group is not None and len(scan_out) < per_group:
            states = scan_step(states)
        for d, chunk, (q_eff, m_neg, o_loc, s_loc, egl) in zip(dirs, chunks, local_out):
            qe_ref[d, chunk_rows(chunk, c), :] = q_eff
            mp_ref[d, chunk_rows(chunk, 2 * c), :] = m_neg
            ou_ref[d, chunk_rows(chunk, c), :] = o_loc
            nn_ref[d, chunk_rows(chunk, 2 * c), :] = s_loc
            egl_ref[d, chunk_rows(chunk, SUBLANES), :] = egl
        for step_chunks, outs in zip(scan_chunks, scan_out):
            for d, ch in enumerate(step_chunks):
                o_ref[d, chunk_rows(ch, c), :] = outs[d]
        return states

    zero_state = jnp.zeros((2 * c, 2 * c), F32)
    states = run_group(0, None, (zero_state, zero_state))
    states = lax.fori_loop(1, n_groups, lambda j, st: run_group(j, j - 1, st), states)
    run_group(None, n_groups - 1, states)

    ng = ng_ref[...]
    step = 256
    for r in range(0, l, step):
        o = o_ref[0, r:r + step, :] + o_ref[1, r:r + step, :]
        ms = _half_sums(o * o, lane_lo) * (1.0 / GDN_DIM)
        gate = qkvg_ref[0, r:r + step, 3 * LANES:4 * LANES]
        out_ref[0, r:r + step, :] = (o * lax.rsqrt(ms + EPS) * ng * _silu(gate)).astype(out_ref.dtype)


def _dot_sel_lhs(sel_bf16, x):
    hi, lo = _split2(x)
    return _dot(sel_bf16, hi) + _dot(sel_bf16, lo)


def _gdn_call(qkvg, ba, conv_w, gate_params, ng, n, nc):
    b, l, _ = qkvg.shape
    pairs = GDN_HEADS // 2
    n_chunks = l // GDN_CHUNK
    kern = functools.partial(_gdn_kernel, n=n, nc=nc, chunks_per_iter=4)
    return pl.pallas_call(
        kern,
        grid=(b, pairs),
        in_specs=[
            pl.BlockSpec((1, l, 4 * LANES), lambda i, p: (i, 0, p)),
            pl.BlockSpec((1, l, LANES), lambda i, p: (i, 0, 0)),
            pl.BlockSpec((3, 4 * LANES), lambda i, p: (0, p)),
            pl.BlockSpec((2, LANES), lambda i, p: (0, 0)),
            pl.BlockSpec((1, LANES), lambda i, p: (0, 0)),
        ],
        out_specs=pl.BlockSpec((1, l, LANES), lambda i, p: (i, 0, p)),
        out_shape=jax.ShapeDtypeStruct((b, l, pairs * LANES), MIXER_OUT_DTYPE),
        scratch_shapes=[
            pltpu.VMEM((l + 5 * SUBLANES, 3 * LANES), F32),
            pltpu.VMEM((l, LANES), F32),
            pltpu.VMEM((l, LANES), F32),
            pltpu.VMEM((l, LANES), F32),
            pltpu.VMEM((2, l, LANES), F32),
            pltpu.VMEM((2, l, LANES), F32),
            pltpu.VMEM((2, l, LANES), BF16),
            pltpu.VMEM((2, 2 * l, LANES), BF16),
            pltpu.VMEM((2, l, LANES), F32),
            pltpu.VMEM((2, 2 * l, LANES), F32),
            pltpu.VMEM((2, n_chunks * SUBLANES, LANES), F32),
            pltpu.VMEM((2, l, LANES), F32),
        ],
        compiler_params=_cparams(("arbitrary", "arbitrary")),
        name="gated_deltanet",
    )(qkvg, ba, conv_w, gate_params, ng)


def _diff_kernel(*refs, key_start, n_sub, lam_init, aliased):
    if aliased:
        refs = refs[1:]
    q_ref, k_ref, vt_ref, lam_ref, ng_ref, o_ref = refs
    lp = lam_ref[...]
    lam = (jnp.exp(jnp.sum(lp[0:1] * lp[1:2], axis=-1, keepdims=True))
           - jnp.exp(jnp.sum(lp[2:3] * lp[3:4], axis=-1, keepdims=True)) + lam_init)
    lane = lax.broadcasted_iota(jnp.int32, (1, LANES), 1)
    halves = (lane < DIFF_DIM, lane >= DIFF_DIM)
    ng = ng_ref[...]
    k = k_ref[0, key_start:, :]
    vt = vt_ref[0, :, key_start:]
    tq = q_ref.shape[1] // n_sub

    def scores_of(i):
        q = q_ref[0, i * tq:(i + 1) * tq, :]
        return [_dot_nt(k, jnp.where(m, q, jnp.zeros_like(q))) for m in halves]

    ahead = scores_of(0)
    for i in range(n_sub):
        s = ahead
        if i + 1 < n_sub:
            ahead = scores_of(i + 1)
        e = [jnp.exp(x - jnp.max(x, axis=0, keepdims=True)) for x in s]
        pv = [_dot(vt, x.astype(BF16)) for x in e]
        parts = [x * (1.0 / jnp.sum(y, axis=0, keepdims=True)) for x, y in zip(pv, e)]
        ot = parts[0] - lam * parts[1]
        ot = ot * lax.rsqrt(jnp.mean(ot * ot, axis=0, keepdims=True) + EPS)
        o_ref[0, i * tq:(i + 1) * tq, :] = (ot.T * ng * (1.0 - lam_init)).astype(o_ref.dtype)


def _diff_call(dq, dk, dvt, lam_p, ng, lam_init, q_rows, first_block, n_q_blocks, key_start, n_sub, prev_out):
    b, l, _ = dq.shape
    aliased = prev_out is not None
    kern = functools.partial(_diff_kernel, key_start=key_start, n_sub=n_sub, lam_init=lam_init, aliased=aliased)
    row_of = lambda t: first_block + t
    in_specs = [
        pl.BlockSpec((1, q_rows, LANES), lambda i, h, t: (i, row_of(t), h)),
        pl.BlockSpec((1, l, LANES), lambda i, h, t: (i, 0, h)),
        pl.BlockSpec((1, LANES, l), lambda i, h, t: (i, h, 0)),
        pl.BlockSpec((4, DIFF_DIM), lambda i, h, t: (0, 0)),
        pl.BlockSpec((1, LANES), lambda i, h, t: (0, 0)),
    ]
    args = [dq, dk, dvt, lam_p, ng.reshape(1, LANES)]
    aliases = {}
    if aliased:
        in_specs = [pl.BlockSpec(memory_space=pl.ANY)] + in_specs
        args = [prev_out] + args
        aliases = {0: 0}
    return pl.pallas_call(
        kern,
        grid=(b, DIFF_HEADS, n_q_blocks),
        in_specs=in_specs,
        out_specs=pl.BlockSpec((1, q_rows, LANES), lambda i, h, t: (i, row_of(t), h)),
        out_shape=jax.ShapeDtypeStruct((b, l, DIFF_HEADS * LANES), MIXER_OUT_DTYPE),
        input_output_aliases=aliases,
        compiler_params=_cparams(("arbitrary", "arbitrary", "arbitrary")),
        name="diff_attention_ctx" if aliased else "diff_attention",
    )(*args)


def _swa_kernel(q_ref, k_ref, vt_ref, sink_ref, o_ref, *, n, nc):
    t = pl.program_id(1)
    blk = SWA_BLOCK
    n_x = n // blk
    q = q_ref[0]
    lane = lax.broadcasted_iota(jnp.int32, (1, LANES), 1)
    lane_lo = lane < HEAD_DIM
    sink = sink_ref[...]
    group = SWA_HEADS // SWA_KV_HEADS
    cols = group * blk

    def run(keys, vts, valid):
        head_of_col = lax.broadcasted_iota(jnp.int32, (1, cols), 1) >> _log2(blk)
        kvs = range(SWA_KV_HEADS)
        kk = [keys[:, kvh * LANES:(kvh + 1) * LANES] for kvh in kvs]
        vv = [vts[kvh * LANES:(kvh + 1) * LANES, :] for kvh in kvs]
        qst, sk = [], []
        for kvh in kvs:
            q_rows = []
            sk_cols = jnp.zeros((1, cols), F32)
            for g in range(group):
                h = kvh * group + g
                qp = q[:, (h // 2) * LANES:(h // 2 + 1) * LANES]
                q_rows.append(jnp.where(lane_lo if h % 2 == 0 else jnp.logical_not(lane_lo), qp, jnp.zeros_like(qp)))
                sk_cols = jnp.where(head_of_col == g, sink[:, h:h + 1], sk_cols)
            qst.append(jnp.concatenate(q_rows, axis=0))
            sk.append(sk_cols)
        s = [_dot_nt(y, x) for x, y in zip(qst, kk)]
        if valid is not None:
            s = [jnp.where(valid, x, NEG_INF) for x in s]
        mx = [jnp.maximum(jnp.max(x, axis=0, keepdims=True), y) for x, y in zip(s, sk)]
        e = [jnp.exp(x - m) for x, m in zip(s, mx)]
        pv = [_dot(y, x.astype(BF16)) for x, y in zip(e, vv)]
        den = [jnp.sum(x, axis=0, keepdims=True) + jnp.exp(y - m) for x, y, m in zip(e, sk, mx)]
        outs = []
        for ot, dn in zip(pv, den):
            o = (ot * (1.0 / dn)).T
            for g in range(0, group, 2):
                outs.append(jnp.where(lane_lo, o[g * blk:(g + 1) * blk], o[(g + 1) * blk:(g + 2) * blk]))
        o_ref[0] = jnp.concatenate(outs, axis=1).astype(o_ref.dtype)

    @pl.when(t < n_x)
    def _():
        start = pl.multiple_of(jnp.clip((t - 1) * blk, 0, n - 3 * blk), blk)
        keys = jnp.concatenate([k_ref[0, pl.ds(start, 3 * blk), :], k_ref[0, n:n + nc, :]], axis=0)
        vts = jnp.concatenate([vt_ref[0, :, pl.ds(start, 3 * blk)], vt_ref[0, :, n:n + nc]], axis=1)
        shape = (3 * blk + nc, cols)
        key = lax.broadcasted_iota(jnp.int32, shape, 0)
        qpos = t * blk + (lax.broadcasted_iota(jnp.int32, shape, 1) & (blk - 1))
        dist = qpos - (start + key)
        in_window = jnp.logical_and(dist <= SWA_WINDOW, dist >= -SWA_WINDOW)
        valid = jnp.logical_or(key >= 3 * blk, in_window)
        run(keys, vts, valid)

    @pl.when(t >= n_x)
    def _():
        run(k_ref[0, n:n + nc, :], vt_ref[0, :, n:n + nc], None)


def _swa_call(q, k, vt, sink, n, nc, with_ctx):
    b, l, _ = q.shape
    blk = SWA_BLOCK
    nt = (l if with_ctx else n) // blk
    kern = functools.partial(_swa_kernel, n=n, nc=nc)
    return pl.pallas_call(
        kern,
        grid=(b, nt),
        in_specs=[
            pl.BlockSpec((1, blk, SWA_HEADS * HEAD_DIM), lambda i, t: (i, t, 0)),
            pl.BlockSpec((1, l, 2 * LANES), lambda i, t: (i, 0, 0)),
            pl.BlockSpec((1, 2 * LANES, l), lambda i, t: (i, 0, 0)),
            pl.BlockSpec((1, LANES), lambda i, t: (0, 0)),
        ],
        out_specs=pl.BlockSpec((1, blk, SWA_HEADS * HEAD_DIM), lambda i, t: (i, t, 0)),
        out_shape=jax.ShapeDtypeStruct((b, l, SWA_HEADS * HEAD_DIM), MIXER_OUT_DTYPE),
        compiler_params=_cparams(("arbitrary", "arbitrary")),
        name="window_attention",
    )(q, k, vt, sink)


def _dft_tables(n):
    h = n // 2
    r = 1 << (_log2(h) // 2)
    j = jnp.arange(h, dtype=jnp.int32)

    def tables(m):
        thin = lambda k: ((k[:, None] * m[None, :]) % (2 * n)).astype(F32) * (math.pi / n)
        a = thin(r * jnp.arange(h // r, dtype=jnp.int32))[:, None, :]
        b = thin(jnp.arange(r, dtype=jnp.int32))[None, :, :]
        cos = jnp.cos(a) * jnp.cos(b) - jnp.sin(a) * jnp.sin(b)
        sin = jnp.sin(a) * jnp.cos(b) + jnp.cos(a) * jnp.sin(b)
        return cos.reshape(h, h).astype(BF16), (-sin).reshape(h, h).astype(BF16)

    ce, se = tables(2 * j)
    co, so = tables(2 * j + 1)
    return ce, se, co, so, co.T, so.T


def _hyena_filter_kernel(feat_ref, w1_ref, b1_ref, w2_ref, b2_ref, freq_ref, w3f_ref, w3b_ref, dl_ref,
                         ce_ref, se_ref, co_ref, so_ref, ka_ref, kb_ref, km_ref):
    h = feat_ref.shape[1]
    assert h % 2 == 0
    freq = freq_ref[...]
    dl = dl_ref[...]
    row = lax.broadcasted_iota(jnp.int32, (h, 1), 0)

    def taps(part, w3_ref):
        feat = feat_ref[part]
        x = jnp.sin(freq[0:1] * (_dot_f32(feat, w1_ref[...]) + b1_ref[...]))
        x = jnp.sin(freq[1:2] * (_dot_f32(x, w2_ref[...]) + b2_ref[...]))
        return _dot_f32(x, w3_ref[...]) * jnp.exp(-feat[:, 0:1] * dl)

    fe, fo = taps(0, w3f_ref), taps(1, w3f_ref)
    be, bo = jnp.where(row == 0, 0.0, taps(2, w3b_ref)), taps(3, w3b_ref)
    ss = sum(jnp.sum(x * x, axis=0, keepdims=True) for x in (fe, fo, be, bo))
    sc = lax.rsqrt(ss + EPS)
    fe, fo, be, bo = (x * sc for x in (fe, fo, be, bo))
    sgn = jnp.where((row & 1) == 0, 1.0, -1.0)

    def dot2(t_ref, x):
        hi, lo = _split2(x)
        return _dot(t_ref[...], hi) + _dot(t_ref[...], lo)

    def bins(x_even, x_odd):
        ce, co = dot2(ce_ref, x_even), dot2(co_ref, x_odd)
        se, so = dot2(se_ref, x_even), dot2(so_ref, x_odd)
        return ce + co, se + so, ce - co, so - se

    f = bins(fe, fo)
    g = bins(be, bo)
    ka_ref[0, 0], ka_ref[0, 1], kb_ref[0, 0], kb_ref[0, 1] = (x + sgn * y for x, y in zip(f, g))
    mid_r = jnp.sum((fe + be) * sgn, axis=0, keepdims=True)
    mid_i = -jnp.sum((fo + bo) * sgn, axis=0, keepdims=True)
    km_ref[0] = jnp.concatenate([mid_r, mid_i, jnp.zeros((SUBLANES - 2, mid_r.shape[-1]), F32)], axis=0)


def _hyena_filter_call(feats, w1, b1, w2, b2, freq, w3, deltas, tables):
    h = feats.shape[1]
    hid = w2.shape[0]
    ch = deltas.shape[-1]
    tc = MXU_WIDTH
    nct = ch // tc
    const = lambda shape: pl.BlockSpec(shape, lambda o, j: (0,) * len(shape))
    spectrum = pl.BlockSpec((1, 2, h, tc), lambda o, j: (o, 0, 0, j))
    return pl.pallas_call(
        _hyena_filter_kernel,
        grid=(2, nct),
        in_specs=[
            const((4, h, hid)), const((hid, hid)), const((1, hid)), const((hid, hid)), const((1, hid)),
            const((2, hid)),
            pl.BlockSpec((hid, tc), lambda o, j: (0, (2 * o) * nct + j)),
            pl.BlockSpec((hid, tc), lambda o, j: (0, (2 * o + 1) * nct + j)),
            pl.BlockSpec((1, tc), lambda o, j: (0, j)),
        ] + [_resident((h, h))] * 4,
        out_specs=[spectrum, spectrum, pl.BlockSpec((1, SUBLANES, tc), lambda o, j: (o, 0, j))],
        out_shape=[
            jax.ShapeDtypeStruct((2, 2, h, ch), F32),
            jax.ShapeDtypeStruct((2, 2, h, ch), F32),
            jax.ShapeDtypeStruct((2, SUBLANES, ch), F32),
        ],
        compiler_params=_cparams(("arbitrary", "arbitrary")),
        name="hyena_filters",
    )(feats, w1, b1, w2, b2, freq, w3, w3, deltas, *tables[:4])


def _hyena_kernel(*refs, n, aliased):
    if aliased:
        refs = refs[1:]
    (v_ref, x1_ref, x2_ref, cwv_ref, cw1_ref, cw2_ref, ka_ref, kb_ref, km_ref, bias_ref,
     ce_ref, se_ref, co_ref, so_ref, cot_ref, sot_ref, o_ref, pad_ref, z_ref, zb_ref, p_ref, y_ref) = refs
    halo = SUBLANES
    tc = o_ref.shape[-1]
    h = n // 2
    rc = min(h, HY_ROW_CHUNK)
    lane_groups = tc // LANES
    zero_rows = jnp.zeros((halo, LANES), F32)
    for g in range(lane_groups):
        pad_ref[g, 0:halo, :] = zero_rows
        pad_ref[g, halo + n:2 * halo + n, :] = zero_rows

    def stage(ref):
        for r in range(0, n, 2 * rc):
            for g in range(lane_groups):
                pad_ref[g, halo + r:halo + r + 2 * rc, :] = ref[0, r:r + 2 * rc, g * LANES:(g + 1) * LANES]

    def conv_rows(cw, parity, r):
        first = halo + 2 * r + parity - 1
        taps = [jnp.concatenate([pad_ref[g, pl.ds(first + i, rc, stride=2), :] for g in range(lane_groups)], axis=1)
                for i in range(3)]
        return cw[0:1] * taps[0] + cw[1:2] * taps[1] + cw[2:3] * taps[2]

    def alt_sign(r):
        j = r + lax.broadcasted_iota(jnp.int32, (rc, 1), 0)
        return j, jnp.where((j & 1) == 0, 1.0, -1.0)

    stage(v_ref)
    cw = cwv_ref[...]
    for parity in range(2):
        for r in range(0, h, rc):
            z = conv_rows(cw, parity, r)
            z_ref[parity, r:r + rc, :] = z
            zb_ref[parity, r:r + rc, :] = z.astype(BF16)

    for o, (gate_ref, gate_cw_ref) in enumerate(((x1_ref, cw1_ref), (x2_ref, cw2_ref))):
        mid_r = jnp.zeros((1, tc), F32)
        mid_i = jnp.zeros((1, tc), F32)
        for r in range(0, h, rc):
            _, sgn = alt_sign(r)
            mid_r = mid_r + jnp.sum(z_ref[0, r:r + rc, :] * sgn, axis=0, keepdims=True)
            mid_i = mid_i - jnp.sum(z_ref[1, r:r + rc, :] * sgn, axis=0, keepdims=True)
        km_r = km_ref[o, 0:1, :]
        km_i = km_ref[o, 1:2, :]
        pm_r = (mid_r * km_r - mid_i * km_i) * (1.0 / n)
        pm_i = (mid_r * km_i + mid_i * km_r) * (1.0 / n)
        ze = zb_ref[0]
        zo = zb_ref[1]
        for r in range(0, h, rc):
            k, _ = alt_sign(r)
            rows = slice(r, r + rc)
            ce, co = _dot(ce_ref[rows, :], ze), _dot(co_ref[rows, :], zo)
            se, so = _dot(se_ref[rows, :], ze), _dot(so_ref[rows, :], zo)
            xa_r, xb_r, xa_i, xb_i = ce + co, ce - co, se + so, so - se
            wgt = jnp.where(k == 0, 0.5 / n, 1.0 / n)
            ka_r, ka_i = ka_ref[o, 0, rows, :], ka_ref[o, 1, rows, :]
            kb_r, kb_i = kb_ref[o, 0, rows, :], kb_ref[o, 1, rows, :]
            pa_r = (xa_r * ka_r - xa_i * ka_i) * wgt
            pa_i = (xa_r * ka_i + xa_i * ka_r) * wgt
            pb_r = (xb_r * kb_r - xb_i * kb_i) * wgt
            pb_i = (xb_r * kb_i + xb_i * kb_r) * wgt
            p_ref[0, rows, :] = (pa_r + pb_r).astype(BF16)
            p_ref[1, rows, :] = (pa_i - pb_i).astype(BF16)
            p_ref[2, rows, :] = (pa_r - pb_r).astype(BF16)
            p_ref[3, rows, :] = (pa_i + pb_i).astype(BF16)
        stage(gate_ref)
        cw = gate_cw_ref[...]
        bias = bias_ref[o:o + 1, :]
        for parity, (c_ref, s_ref, mid) in enumerate(((ce_ref, se_ref, pm_r), (cot_ref, sot_ref, -pm_i))):
            for r in range(0, h, rc):
                _, sgn = alt_sign(r)
                rows = slice(r, r + rc)
                y = (_dot(c_ref[rows, :], p_ref[2 * parity]) + _dot(s_ref[rows, :], p_ref[2 * parity + 1])
                     + sgn * mid)
                z = conv_rows(cw, parity, r) * (y + z_ref[parity, rows, :] * bias)
                if o == 0:
                    z_ref[parity, rows, :] = z
                    zb_ref[parity, rows, :] = z.astype(BF16)
                else:
                    for g in range(lane_groups):
                        y_ref[g, pl.ds(2 * r + parity, rc, stride=2), :] = z[:, g * LANES:(g + 1) * LANES]
    for r in range(0, n, 2 * rc):
        rows = slice(r, r + 2 * rc)
        o_ref[0, rows, :] = jnp.concatenate([y_ref[g, rows, :] for g in range(lane_groups)],
                                            axis=1).astype(o_ref.dtype)


def _hyena_call(u, conv_w, ka, kb, km, bias, tables, n, row_block, prev_out):
    b, l, _ = u.shape
    ch = bias.shape[-1]
    tc = MXU_WIDTH
    nct = ch // tc
    h = n // 2
    aliased = prev_out is not None
    kern = functools.partial(_hyena_kernel, n=n, aliased=aliased)
    once = pl.Buffered(1)
    in_specs = [
        pl.BlockSpec((1, n, tc), lambda j, i: (i, row_block, j)),
        pl.BlockSpec((1, n, tc), lambda j, i: (i, row_block, nct + j)),
        pl.BlockSpec((1, n, tc), lambda j, i: (i, row_block, 2 * nct + j)),
        pl.BlockSpec((3, tc), lambda j, i: (0, j)),
        pl.BlockSpec((3, tc), lambda j, i: (0, nct + j)),
        pl.BlockSpec((3, tc), lambda j, i: (0, 2 * nct + j)),
        pl.BlockSpec((2, 2, h, tc), lambda j, i: (0, 0, 0, j), pipeline_mode=once),
        pl.BlockSpec((2, 2, h, tc), lambda j, i: (0, 0, 0, j), pipeline_mode=once),
        pl.BlockSpec((2, SUBLANES, tc), lambda j, i: (0, 0, j)),
        pl.BlockSpec((2, tc), lambda j, i: (0, j)),
    ] + [_resident((h, h))] * 6
    args = [u, u, u, conv_w, conv_w, conv_w, ka, kb, km, bias, *tables]
    aliases = {}
    if aliased:
        in_specs = [pl.BlockSpec(memory_space=pl.ANY)] + in_specs
        args = [prev_out] + args
        aliases = {0: 0}
    return pl.pallas_call(
        kern,
        grid=(nct, b),
        in_specs=in_specs,
        out_specs=pl.BlockSpec((1, n, tc), lambda j, i: (i, row_block, j)),
        out_shape=jax.ShapeDtypeStruct((b, l, ch), MIXER_OUT_DTYPE),
        scratch_shapes=[
            pltpu.VMEM((tc // LANES, n + 2 * SUBLANES, LANES), F32),
            pltpu.VMEM((2, h, tc), F32),
            pltpu.VMEM((2, h, tc), BF16),
            pltpu.VMEM((4, h, tc), BF16),
            pltpu.VMEM((tc // LANES, n, LANES), F32),
        ],
        input_output_aliases=aliases,
        compiler_params=_cparams(("arbitrary", "arbitrary")),
        name="hyena_conv_n%d" % n,
    )(*args)


def _rope_tables(n, nc):
    rows = n // GRID_W
    row = jnp.repeat(jnp.arange(rows, dtype=F32), GRID_W)
    col = jnp.tile(jnp.arange(GRID_W, dtype=F32), rows)
    half = HEAD_DIM // 2
    inv = ROPE_BASE ** (-jnp.arange(0, half, 2, dtype=F32) / half)
    ar = row[:, None] * inv
    ac = col[:, None] * inv
    cos = jnp.concatenate([jnp.cos(ar), jnp.cos(ar), jnp.cos(ac), jnp.cos(ac)], axis=-1)
    sin = jnp.concatenate([-jnp.sin(ar), jnp.sin(ar), -jnp.sin(ac), jnp.sin(ac)], axis=-1)
    cos = jnp.concatenate([cos, jnp.ones((nc, HEAD_DIM), F32)], axis=0)
    sin = jnp.concatenate([sin, jnp.zeros((nc, HEAD_DIM), F32)], axis=0)
    return jnp.tile(cos, (1, LANES // HEAD_DIM)), jnp.tile(sin, (1, LANES // HEAD_DIM))


def _rope_partner_cols(width):
    d = np.arange(width)
    quarter = HEAD_DIM // 4
    return np.where((d % (2 * quarter)) < quarter, d + quarter, d - quarter)


def _hyena_feats(n):
    pos = jnp.arange(n, dtype=F32)
    t = pos / max(n - 1, 1)
    ang = (2.0 * math.pi * pos / n)[:, None] * jnp.linspace(1e-4, HY_BANDS - 1, HY_BANDS, dtype=F32)[None, :]
    feats = jnp.concatenate([t[:, None], jnp.cos(ang), -jnp.sin(ang)], axis=-1)
    feats = jnp.pad(feats, ((0, 0), (0, 64 - feats.shape[-1])))
    back = jnp.concatenate([feats[0:1], jnp.flip(feats[1:], axis=0)], axis=0)
    return jnp.stack([feats[0::2], feats[1::2], back[0::2], back[1::2]])


def _pad_cols(w, width):
    return jnp.pad(w, ((0, 0), (0, width - w.shape[-1])))


def _layer_ab(xz, mod, norm_g0, w_in, conv_w, a_log, dt_bias, gdn_g, lam_p, diff_g, lam_init, rope, n, nc):
    hd = GDN_HEADS * GDN_DIM
    wq, wk, wv, wg = (w_in[:, i * hd:(i + 1) * hd] for i in range(4))
    o = 4 * hd
    w_beta, w_alpha = w_in[:, o:o + 16], w_in[:, o + 16:o + 32]
    o += 32
    dd = DIFF_HEADS * 2 * DIFF_DIM
    wdq, wdk, wdv = (w_in[:, o + i * dd:o + (i + 1) * dd] for i in range(3))
    pairs = GDN_HEADS // 2
    pair_cols = lambda w: [w[:, p * LANES:(p + 1) * LANES] for p in range(pairs)]
    w_qkvg = jnp.concatenate([blk for grp in zip(pair_cols(wq), pair_cols(wk), pair_cols(wv), pair_cols(wg))
                              for blk in grp], axis=1)
    perm = _rope_partner_cols(dd)
    w_all = jnp.concatenate([w_qkvg, _pad_cols(jnp.concatenate([w_beta, w_alpha], axis=1), LANES),
                             wdq, wdk, wdv, wdq[:, perm], wdk[:, perm]], axis=1).astype(BF16)
    c0 = 4 * hd
    c1 = c0 + LANES
    segs = (_Seg(0, c0), _Seg(c0, LANES),
            _Seg(c1, dd, rot_start=c1 + 3 * dd, scale=DIFF_DIM ** -0.5, dtype=BF16),
            _Seg(c1 + dd, dd, rot_start=c1 + 4 * dd, dtype=BF16),
            _Seg(c1 + 2 * dd, dd, dtype=BF16, transposed=True))
    qkvg, ba, dq, dk, dvt = _proj_call(xz, mod, norm_g0, w_all, rope[0], rope[1], segs, n // ROW_TILE, "proj_ab")

    cq, ck, cv = (conv_w[:, i * hd:(i + 1) * hd] for i in range(3))
    zeros = jnp.zeros((3, LANES), F32)
    conv_l = jnp.concatenate([blk for p in range(pairs) for blk in
                              (cq[:, p * LANES:(p + 1) * LANES], ck[:, p * LANES:(p + 1) * LANES],
                               cv[:, p * LANES:(p + 1) * LANES], zeros)], axis=1)
    n_gate = 2 * GDN_HEADS
    on_decay_lanes = lambda t: jnp.pad(t.reshape(1, n_gate), ((0, 0), (n_gate, LANES - 2 * n_gate)))
    gate_params = jnp.concatenate([on_decay_lanes(a_log), on_decay_lanes(dt_bias)], axis=0)
    ng = jnp.tile(gdn_g.reshape(1, GDN_DIM), (1, 2))
    oa = _gdn_call(qkvg, ba, conv_l, gate_params, ng, n, nc)
    q_rows = DIFF_SUB_TILES * ROW_TILE
    ob = _diff_call(dq, dk, dvt, lam_p, diff_g, lam_init, q_rows, 0, n // q_rows, 0, DIFF_SUB_TILES, None)
    ob = _diff_call(dq, dk, dvt, lam_p, diff_g, lam_init, nc, n // nc, 1, n, 1, ob)
    return oa, ob


def _layer_cd(xz, mod, norm_g0, w_in, sink, hy_conv, hy_w1, hy_b1, hy_w2, hy_b2, hy_w3, hy_freq, hy_bias,
              rope, n, nc, last, dft_x, dft_c):
    qd = SWA_HEADS * HEAD_DIM
    kd = SWA_KV_HEADS * HEAD_DIM
    wq, wk, wv, wu = w_in[:, 0:qd], w_in[:, qd:qd + kd], w_in[:, qd + kd:qd + 2 * kd], w_in[:, qd + 2 * kd:]
    dup = lambda w: jnp.concatenate([w[:, 0:HEAD_DIM], w[:, 0:HEAD_DIM], w[:, HEAD_DIM:], w[:, HEAD_DIM:]], axis=1)
    wk2, wv2 = dup(wk), dup(wv)
    ud = wu.shape[1]
    w_all = jnp.concatenate([wq, wk2, wv2, wu, wq[:, _rope_partner_cols(qd)], wk2[:, _rope_partner_cols(2 * kd)]],
                            axis=1).astype(BF16)
    o_u = qd + 4 * kd
    segs = (_Seg(0, qd, rot_start=o_u + ud, scale=HEAD_DIM ** -0.5, dtype=BF16),
            _Seg(qd, 2 * kd, rot_start=o_u + ud + qd, dtype=BF16),
            _Seg(qd + 2 * kd, 2 * kd, dtype=BF16, transposed=True), _Seg(o_u, ud))
    q, k, vt, u = _proj_call(xz, mod, norm_g0, w_all, rope[0], rope[1], segs, n // ROW_TILE, "proj_cd")
    oc = _swa_call(q, k, vt, _pad_cols(sink.reshape(1, SWA_HEADS), LANES), n, nc, not last)

    ch = hy_bias.shape[-1]
    deltas = jnp.abs(jnp.linspace(HY_MIN_DECAY, HY_MAX_DECAY, ch, dtype=F32)).reshape(1, ch)
    hid = hy_w2.shape[0]
    w1p = jnp.pad(hy_w1, ((0, hid - hy_w1.shape[0]), (0, 0)))
    filt = lambda m, dft: _hyena_filter_call(_hyena_feats(m), w1p, hy_b1.reshape(1, hid), hy_w2,
                                             hy_b2.reshape(1, hid), hy_freq, hy_w3, deltas, dft)
    od = _hyena_call(u, hy_conv, *filt(n, dft_x), hy_bias, dft_x, n, 0, None)
    if not last:
        od = _hyena_call(u, hy_conv, *filt(nc, dft_c), hy_bias, dft_c, nc, n // nc, od)
    return oc, od


def kernel(x, c, ctx, c_ctx, w_mod, b_mod, norm_g, ffn_w_up, ffn_conv, ffn_w_down, ab_w_in, ab_w_out, gdn_conv, gdn_a_log, gdn_dt_bias, gdn_norm_g, diff_lambda, diff_norm_g, cd_w_in, cd_w_out, swa_sink, hy_conv, hy_w1, hy_b1, hy_w2, hy_b2, hy_w3, hy_freq, hy_bias):
    b, n, d = x.shape
    nc = ctx.shape[1]
    depth = w_mod.shape[0]
    assert n % ROW_TILE == 0 and nc == ROW_TILE and n % GRID_W == 0
    xz = jnp.concatenate([x, ctx], axis=1)
    rows = -(-(b + 1) // SUBLANES) * SUBLANES
    cc = jnp.concatenate([c, c_ctx[None], jnp.zeros((rows - b - 1, d), F32)], axis=0)
    mods = _mod_call(cc, w_mod, b_mod)
    mod_all = jnp.concatenate([mods[:, :b].reshape(depth, b, 1, 6, d),
                               jnp.broadcast_to(mods[:, b].reshape(depth, 1, 1, 6, d), (depth, b, 1, 6, d))], axis=2)
    rope = _rope_tables(n, nc)
    dft_x = _dft_tables(n)
    dft_c = _dft_tables(nc)
    n_x_tiles = n // ROW_TILE
    for l in range(depth):
        last = l == depth - 1
        i = l // 2
        mod = mod_all[l]
        if l % 2 == 0:
            lam_init = 0.8 - 0.6 * math.exp(-0.3 * l)
            o1, o2 = _layer_ab(xz, mod, norm_g[l, 0], ab_w_in[i], gdn_conv[i], gdn_a_log[i], gdn_dt_bias[i],
                               gdn_norm_g[i], diff_lambda[i], diff_norm_g[i], lam_init, rope, n, nc)
            w_out = ab_w_out[i]
        else:
            o1, o2 = _layer_cd(xz, mod, norm_g[l, 0], cd_w_in[i], swa_sink[i], hy_conv[i], hy_w1[i], hy_b1[i],
                               hy_w2[i], hy_b2[i], hy_w3[i], hy_freq[i], hy_bias[i], rope, n, nc, last, dft_x, dft_c)
            w_out = cd_w_out[i]
        n_tiles = (n if last else n + nc) // ROW_TILE
        xz = _post_call(o1, o2, xz, mod, norm_g[l, 1], norm_g[l, 2], norm_g[l, 3], w_out.astype(BF16),
                        ffn_w_up[l].astype(BF16), ffn_conv[l], ffn_w_down[l].astype(BF16), n_tiles, n_x_tiles)
    return xz
```

```python
import functools
import math
from typing import NamedTuple, Optional

import jax
import jax.numpy as jnp
import numpy as np
from jax import lax
from jax.experimental import pallas as pl
from jax.experimental.pallas import tpu as pltpu

F32 = jnp.float32
BF16 = jnp.bfloat16
MIXER_OUT_DTYPE = BF16

EPS = 1e-6
NEG_INF = -1e30
GRID_W = 64
HEAD_DIM = 64
ROPE_BASE = 10000.0
GDN_HEADS = 8
GDN_DIM = 64
GDN_CHUNK = 64
GDN_CHUNKS_PER_GROUP = 6
DIFF_HEADS = 4
DIFF_DIM = 64
DIFF_SUB_TILES = 8
SWA_HEADS = 8
SWA_KV_HEADS = 2
SWA_WINDOW = 128
SWA_BLOCK = 128
HY_BANDS = 16
HY_MIN_DECAY = math.log(1e-2) / 1.5
HY_MAX_DECAY = math.log(1e-2) / 0.3
HY_ROW_CHUNK = 512

LANES = 128
SUBLANES = 8
MXU_WIDTH = 256
FFN_COL_CHUNK = 6 * MXU_WIDTH
ROW_TILE = 256
VMEM_LIMIT = 56 * 1024 * 1024


def _cparams(sem):
    return pltpu.CompilerParams(dimension_semantics=sem, vmem_limit_bytes=VMEM_LIMIT)


def _resident(shape):
    zeros = (0,) * len(shape)
    return pl.BlockSpec(shape, lambda *_: zeros, pipeline_mode=pl.Buffered(1))


def _log2(v):
    assert v & (v - 1) == 0
    return v.bit_length() - 1


def _sigmoid(x):
    return 1.0 / (1.0 + jnp.exp(-x))


def _silu(x):
    return x * _sigmoid(x)


def _softplus(x):
    return jnp.maximum(x, 0.0) + jnp.log1p(jnp.exp(-jnp.abs(x)))


def _dot(a, b):
    return jnp.dot(a, b, preferred_element_type=F32)


def _dot_nt(a, b):
    return lax.dot_general(a, b, (((1,), (1,)), ((), ())), preferred_element_type=F32)


def _dot_tn(a, b):
    return lax.dot_general(a, b, (((0,), (0,)), ((), ())), preferred_element_type=F32)


def _dot_f32(a, b):
    return jnp.dot(a, b, preferred_element_type=F32, precision=lax.Precision.HIGHEST)


def _split2(x):
    hi = x.astype(BF16)
    lo = (x - hi.astype(F32)).astype(BF16)
    return hi, lo


def _dot_sel(x, sel_bf16):
    hi, lo = _split2(x)
    return _dot(hi, sel_bf16) + _dot(lo, sel_bf16)


def _mm(a, b):
    return _dot(a.astype(BF16), b.astype(BF16))


def _rms(y, g):
    return y * lax.rsqrt(jnp.mean(y * y, axis=-1, keepdims=True) + EPS) * g


def _modnorm(x, g, shift, scale):
    return _rms(x, g) * (1.0 + scale) + shift


def _mod_kernel(cc_ref, w_ref, b_ref, o_ref):
    s = _silu(cc_ref[...])
    o_ref[0] = _dot(s.astype(BF16), w_ref[0].astype(BF16)) + b_ref[0]


def _mod_call(cc, w_mod, b_mod):
    depth, d, nm = w_mod.shape
    rows = cc.shape[0]
    ct = 1536
    return pl.pallas_call(
        _mod_kernel,
        grid=(depth, nm // ct),
        in_specs=[
            pl.BlockSpec((rows, d), lambda l, j: (0, 0)),
            pl.BlockSpec((1, d, ct), lambda l, j: (l, 0, j)),
            pl.BlockSpec((1, 1, ct), lambda l, j: (l, 0, j)),
        ],
        out_specs=pl.BlockSpec((1, rows, ct), lambda l, j: (l, 0, j)),
        out_shape=jax.ShapeDtypeStruct((depth, rows, nm), F32),
        compiler_params=_cparams(("arbitrary", "arbitrary")),
        name="adaln_mod",
    )(cc, w_mod, b_mod.reshape(depth, 1, nm))


class _Seg(NamedTuple):
    start: int
    width: int
    rot_start: Optional[int] = None
    scale: float = 1.0
    dtype: type = F32
    transposed: bool = False


def _proj_kernel(x_ref, mod_ref, g_ref, w_ref, cos_ref, sin_ref, *out_refs, segs):
    m = mod_ref[0, 0]
    h = _modnorm(x_ref[0], g_ref[...], m[0:1], m[1:2]).astype(BF16)
    for o_ref, seg in zip(out_refs, segs):
        y = _dot(h, w_ref[:, seg.start:seg.start + seg.width])
        if seg.rot_start is not None:
            yr = _dot(h, w_ref[:, seg.rot_start:seg.rot_start + seg.width])
            reps = seg.width // LANES
            cos = jnp.concatenate([cos_ref[...]] * reps, axis=1)
            sin = jnp.concatenate([sin_ref[...]] * reps, axis=1)
            y = y * cos + yr * sin
        if seg.scale != 1.0:
            y = y * seg.scale
        if seg.transposed:
            y = y.T
        o_ref[0] = y.astype(seg.dtype)


def _proj_call(xz, mod, g, w, cos_t, sin_t, segs, n_x_tiles, name):
    b, l, d = xz.shape
    tm = ROW_TILE
    nt = l // tm
    p = w.shape[1]
    return pl.pallas_call(
        functools.partial(_proj_kernel, segs=segs),
        grid=(nt, b),
        in_specs=[
            pl.BlockSpec((1, tm, d), lambda t, i: (i, t, 0)),
            pl.BlockSpec((1, 1, 6, d), lambda t, i: (i, t // n_x_tiles, 0, 0)),
            pl.BlockSpec((1, d), lambda t, i: (0, 0)),
            _resident((d, p)),
            pl.BlockSpec((tm, LANES), lambda t, i: (t, 0)),
            pl.BlockSpec((tm, LANES), lambda t, i: (t, 0)),
        ],
        out_specs=[pl.BlockSpec((1, s.width, tm), lambda t, i: (i, 0, t)) if s.transposed
                   else pl.BlockSpec((1, tm, s.width), lambda t, i: (i, t, 0)) for s in segs],
        out_shape=[jax.ShapeDtypeStruct((b, s.width, l) if s.transposed else (b, l, s.width), s.dtype)
                   for s in segs],
        compiler_params=_cparams(("arbitrary", "arbitrary")),
        name=name,
    )(xz, mod, g.reshape(1, d), w, cos_t, sin_t)


def _post_kernel(o1p_ref, o1_ref, o1n_ref, o2p_ref, o2_ref, o2n_ref, xp_ref, x_ref, xn_ref, mod_ref,
                 g1_ref, g2_ref, g3_ref, wout_ref, wup_ref, cw_ref, wdn_ref, out_ref, up_ref,
                 *, tm, n_x_tiles, n_tiles, cf, dff):
    t = pl.program_id(0)
    first = jnp.logical_or(t == 0, t == n_x_tiles)
    last = jnp.logical_or(t == n_x_tiles - 1, t == n_tiles - 1)
    m = mod_ref[0, 0]
    halo = SUBLANES
    ohalo = o1p_ref.shape[1]
    k1 = o1_ref.shape[-1]
    o1e = jnp.concatenate([o1p_ref[0], o1_ref[0], o1n_ref[0]], axis=0)
    o2e = jnp.concatenate([o2p_ref[0], o2_ref[0], o2n_ref[0]], axis=0)
    y = _dot(o1e, wout_ref[0:k1, :]) + _dot(o2e, wout_ref[k1:, :])
    y = y[ohalo - halo:ohalo + tm + halo]
    xe = jnp.concatenate([xp_ref[0], x_ref[0], xn_ref[0]], axis=0)
    x1 = xe + m[2:3] * _rms(y, g1_ref[...])
    h = _modnorm(x1, g2_ref[...], m[3:4], m[4:5]).astype(BF16)
    acc = jnp.zeros((tm, x_ref.shape[-1]), F32)
    for c0 in range(0, dff, cf):
        wd = min(cf, dff - c0)
        halves = []
        for half, base in enumerate((c0, dff + c0)):
            u = _dot(h, wup_ref[:, base:base + wd])
            up_ref[half, :, 0:wd] = u
            up_ref[half, 0:halo, 0:wd] = jnp.where(first, 0.0, u[0:halo])
            up_ref[half, tm + halo:tm + 2 * halo, 0:wd] = jnp.where(last, 0.0, u[tm + halo:])
            cw = cw_ref[:, base:base + wd]
            halves.append(cw[0:1] * up_ref[half, halo - 1:halo - 1 + tm, 0:wd]
                          + cw[1:2] * up_ref[half, halo:halo + tm, 0:wd]
                          + cw[2:3] * up_ref[half, halo + 1:halo + 1 + tm, 0:wd])
        act = (_silu(halves[1]) * halves[0]).astype(BF16)
        acc = acc + _dot(act, wdn_ref[c0:c0 + wd, :])
    out_ref[0] = x1[halo:halo + tm] + m[5:6] * _rms(acc, g3_ref[...])


def _post_call(o1, o2, xz, mod, g1, g2, g3, w_out, w_up, conv_w, w_down, n_tiles, n_x_tiles):
    b, _, d = xz.shape
    tm = ROW_TILE
    rows = n_tiles * tm
    dff = w_down.shape[0]
    cf = FFN_COL_CHUNK
    k1, k2 = o1.shape[-1], o2.shape[-1]
    ohalo = 2 * SUBLANES
    kern = functools.partial(_post_kernel, tm=tm, n_x_tiles=n_x_tiles, n_tiles=n_tiles, cf=cf, dff=dff)

    def with_halos(width, halo_rows):
        per_tile = tm // halo_rows
        n_blocks = rows // halo_rows
        return [
            pl.BlockSpec((1, halo_rows, width), lambda t, i: (i, jnp.maximum(t * per_tile - 1, 0), 0)),
            pl.BlockSpec((1, tm, width), lambda t, i: (i, t, 0)),
            pl.BlockSpec((1, halo_rows, width), lambda t, i: (i, jnp.minimum((t + 1) * per_tile, n_blocks - 1), 0)),
        ]

    row_vec = pl.BlockSpec((1, d), lambda t, i: (0, 0))
    return pl.pallas_call(
        kern,
        grid=(n_tiles, b),
        in_specs=with_halos(k1, ohalo) + with_halos(k2, ohalo) + with_halos(d, SUBLANES) + [
            pl.BlockSpec((1, 1, 6, d), lambda t, i: (i, t // n_x_tiles, 0, 0)),
            row_vec, row_vec, row_vec,
            _resident((k1 + k2, d)),
            _resident((d, 2 * dff)),
            pl.BlockSpec((3, 2 * dff), lambda t, i: (0, 0)),
            _resident((dff, d)),
        ],
        out_specs=pl.BlockSpec((1, tm, d), lambda t, i: (i, t, 0)),
        out_shape=jax.ShapeDtypeStruct((b, rows, d), F32),
        scratch_shapes=[pltpu.VMEM((2, tm + 2 * SUBLANES, cf), F32)],
        compiler_params=_cparams(("arbitrary", "arbitrary")),
        name="mixer_out_conv_ffn",
    )(o1, o1, o1, o2, o2, o2, xz, xz, xz, mod, g1.reshape(1, d), g2.reshape(1, d), g3.reshape(1, d),
      w_out, w_up, conv_w, w_down)


def _half_sums(x2, lane_lo):
    s0 = jnp.sum(jnp.where(lane_lo, x2, 0.0), axis=-1, keepdims=True)
    s1 = jnp.sum(jnp.where(lane_lo, 0.0, x2), axis=-1, keepdims=True)
    return jnp.where(lane_lo, s0, s1)


def _gdn_kernel(qkvg_ref, ba_ref, cw_ref, gp_ref, ng_ref, out_ref,
                pad_ref, q_ref, k_ref, v_ref, bb_ref, gb_ref, qe_ref, mp_ref, ou_ref, nn_ref, egl_ref, o_ref,
                *, n, nc, chunks_per_iter):
    l = n + nc
    c = GDN_CHUNK
    n_chunks = l // c
    pair = pl.program_id(1)
    halo = SUBLANES
    lane = lax.broadcasted_iota(jnp.int32, (1, LANES), 1)
    lane_lo = lane < GDN_DIM

    cw = cw_ref[:, 0:3 * LANES]
    zero_rows = jnp.zeros((halo, 3 * LANES), F32)
    for seq_start, seq_len in ((0, n), (n, nc)):
        base = halo + seq_start + (2 * halo if seq_start else 0)
        pad_ref[base - halo:base, :] = zero_rows
        pad_ref[base + seq_len:base + seq_len + halo, :] = zero_rows
        step = 256
        for r in range(0, seq_len, step):
            pad_ref[base + r:base + r + step, :] = qkvg_ref[0, seq_start + r:seq_start + r + step, 0:3 * LANES]
        for r in range(0, seq_len, step):
            y = (cw[0:1] * pad_ref[base + r - 1:base + r - 1 + step, :]
                 + cw[1:2] * pad_ref[base + r:base + r + step, :]
                 + cw[2:3] * pad_ref[base + r + 1:base + r + 1 + step, :])
            y = _silu(y)
            q = y[:, 0:LANES]
            k = y[:, LANES:2 * LANES]
            rows = slice(seq_start + r, seq_start + r + step)
            q_ref[rows, :] = q * lax.rsqrt(_half_sums(q * q, lane_lo) + EPS) * (GDN_DIM ** -0.5)
            k_ref[rows, :] = k * lax.rsqrt(_half_sums(k * k, lane_lo) + EPS)
            v_ref[rows, :] = y[:, 2 * LANES:3 * LANES]

    sel_r = lax.broadcasted_iota(jnp.int32, (LANES, 4 * LANES), 0)
    sel_c = lax.broadcasted_iota(jnp.int32, (LANES, 4 * LANES), 1)
    quarter = sel_c >> _log2(LANES)
    src_lane = (quarter & 1) * 2 * GDN_HEADS + (quarter >> 1) * GDN_HEADS + 2 * pair + ((sel_c >> _log2(GDN_DIM)) & 1)
    sel = (sel_r == src_lane).astype(BF16)
    gblk = 256
    bi = lax.broadcasted_iota(jnp.int32, (gblk, gblk), 0)
    bj = lax.broadcasted_iota(jnp.int32, (gblk, gblk), 1)
    same_chunk = (bi >> _log2(c)) == (bj >> _log2(c))
    csum = (jnp.logical_and(same_chunk, bi >= bj).astype(BF16), jnp.logical_and(same_chunk, bi <= bj).astype(BF16))
    neg_a = -jnp.exp(gp_ref[0:1, :])
    dt_bias = gp_ref[1:2, :]
    for r in range(0, l, gblk):
        ba = ba_ref[0, r:r + gblk, :]
        gates = jnp.where(lane < 2 * GDN_HEADS, _sigmoid(ba), neg_a * _softplus(ba + dt_bias))
        x = _dot_sel(gates, sel)
        for d in range(2):
            bb_ref[d, r:r + gblk, :] = x[:, 2 * d * LANES:(2 * d + 1) * LANES]
            gb_ref[d, r:r + gblk, :] = _dot_sel_lhs(csum[d], x[:, (2 * d + 1) * LANES:(2 * d + 2) * LANES])

    r2 = lax.broadcasted_iota(jnp.int32, (2 * c, 2 * c), 0)
    c2 = lax.broadcasted_iota(jnp.int32, (2 * c, 2 * c), 1)
    same_head = (r2 >= c) == (c2 >= c)
    eye = (r2 == c2).astype(F32)
    masks = ((jnp.logical_and(same_head, r2 >= c2), jnp.logical_and(same_head, r2 > c2)),
             (jnp.logical_and(same_head, r2 <= c2), jnp.logical_and(same_head, r2 < c2)))
    m0 = lane_lo.astype(F32)
    m1 = 1.0 - m0

    def pair_mask(lv, lower):
        same_block = (r2 >> (lv + 1)) == (c2 >> (lv + 1))
        r_hi = ((r2 >> lv) & 1) == 1
        c_hi = ((c2 >> lv) & 1) == 1
        off = jnp.logical_and(r_hi, jnp.logical_not(c_hi)) if lower else jnp.logical_and(c_hi, jnp.logical_not(r_hi))
        return jnp.logical_and(same_block, off)

    pair_masks = tuple(tuple(pair_mask(lv, lower) for lv in range(_log2(c))) for lower in (True, False))

    def stack_heads(x2):
        return jnp.concatenate([x2 * m0, x2 * m1], axis=0)

    def fold_heads(x):
        return x[0:c] + x[c:2 * c]

    def local_stages(dirs, qs, ks, vs, betas, gcs, out):
        each = lambda f, *cols: [f(*args) for args in zip(*cols)]
        incl = [masks[d][0] for d in dirs]
        strict = [masks[d][1] for d in dirs]
        g1 = each(lambda gc2: jnp.concatenate([gc2, gc2], axis=0), gcs)
        decay = each(lambda g, m: jnp.where(m, jnp.exp(jnp.where(m, g - g.T, 0.0)), 0.0), g1, incl)
        kb = each(lambda k, b: k * b, ks, betas)
        kst = each(lambda k: stack_heads(k).astype(BF16), ks)
        a_raw = each(lambda x, y: _dot_nt(stack_heads(x).astype(BF16), y), kb, kst)
        qk_raw = each(lambda x, y: _dot_nt(stack_heads(x).astype(BF16), y), qs, kst)
        yield
        qk = each(lambda m, x, dc: jnp.where(m, x * dc, 0.0).astype(BF16), incl, qk_raw, decay)
        a = each(lambda m, x, dc: jnp.where(m, x * dc, 0.0), strict, a_raw, decay)
        tinv = each(lambda d, x: eye - jnp.where(pair_masks[d][0], x, 0.0), dirs, a)
        for lv in range(1, _log2(c)):
            ta = each(lambda d, t, x: _mm(t, jnp.where(pair_masks[d][lv], x, 0.0)), dirs, tinv, a)
            yield
            tat = each(_mm, ta, tinv)
            yield
            tinv = each(lambda t, x: t - x, tinv, tat)
        egc = each(jnp.exp, gcs)
        rhs = each(lambda v, b, x, e: jnp.concatenate([stack_heads(v * b), stack_heads(x * e)], axis=1),
                   vs, betas, kb, egc)
        sol = each(_mm, tinv, rhs)
        yield
        u2 = each(lambda x: fold_heads(x[:, 0:LANES]), sol)
        w2 = each(lambda x: fold_heads(x[:, LANES:2 * LANES]), sol)
        gl = each(lambda d, gc2: gc2[c - 1:c, :] if d == 0 else gc2[0:1, :], dirs, gcs)
        ktail = each(lambda k, g, gc2: (k * jnp.exp(g - gc2)).astype(BF16), ks, gl, gcs)
        qwu = each(lambda x, w, u: _dot(x, jnp.concatenate([stack_heads(w), stack_heads(u)], axis=1).astype(BF16)),
                   qk, w2, u2)
        kwu = each(lambda x, w, u: _dot_tn(x, jnp.concatenate([w, u], axis=1).astype(BF16)), ktail, w2, u2)
        yield
        q_eff = each(lambda q, e, x: (q * e - fold_heads(x[:, 0:LANES])).astype(BF16), qs, egc, qwu)
        m_neg = each(lambda x: jnp.where(same_head, -x[:, 0:LANES], 0.0).astype(BF16), kwu)
        o_loc = each(lambda x: fold_heads(x[:, LANES:2 * LANES]), qwu)
        s_loc = each(lambda x: jnp.where(same_head, x[:, LANES:2 * LANES], 0.0), kwu)
        egl = each(lambda g: jnp.broadcast_to(jnp.exp(g), (SUBLANES, LANES)), gl)
        out.extend(zip(q_eff, m_neg, o_loc, s_loc, egl))

    def chunk_rows(chunk, rows_per_chunk):
        return pl.ds(pl.multiple_of(chunk * rows_per_chunk, rows_per_chunk), rows_per_chunk)

    ctx_chunks = nc // c
    per_group = chunks_per_iter
    n_groups = n_chunks // per_group

    def chunks_at(step):
        return jnp.where(step < ctx_chunks, step + n // c, step - ctx_chunks), n_chunks - 1 - step

    def run_group(local_group, scan_group, states):
        dirs, chunks, qs, ks, vs, betas, gcs = [], [], [], [], [], [], []
        if local_group is not None:
            for g in range(per_group):
                for d, chunk in enumerate(chunks_at(per_group * local_group + g)):
                    rows = chunk_rows(chunk, c)
                    dirs.append(d)
                    chunks.append(chunk)
                    qs.append(q_ref[rows, :])
                    ks.append(k_ref[rows, :])
                    vs.append(v_ref[rows, :])
                    betas.append(bb_ref[d, rows, :])
                    gcs.append(gb_ref[d, rows, :])
        scan_chunks, scan_in = [], []
        if scan_group is not None:
            for g in range(per_group):
                step_chunks = chunks_at(per_group * scan_group + g)
                scan_chunks.append(step_chunks)
                scan_in.append([(qe_ref[d, chunk_rows(ch, c), :], mp_ref[d, chunk_rows(ch, 2 * c), :],
                                 ou_ref[d, chunk_rows(ch, c), :], nn_ref[d, chunk_rows(ch, 2 * c), :],
                                 egl_ref[d, chunk_rows(ch, SUBLANES), :]) for d, ch in enumerate(step_chunks)])
        scan_out = []

        def scan_step(states):
            loaded = scan_in[len(scan_out)]
            res = [_dot(jnp.concatenate([ld[0], ld[1]], axis=0), s2.astype(BF16)) for ld, s2 in zip(loaded, states)]
            scan_out.append([r[0:c] + ld[2] for r, ld in zip(res, loaded)])
            return tuple(s2 * ld[4][0:1] + r[c:3 * c] + ld[3] for s2, ld, r in zip(states, loaded, res))

        local_out = []
        stages = local_stages(dirs, qs, ks, vs, betas, gcs, local_out) if local_group is not None else iter(())
        for stage, _ in enumerate(stages):
            if scan_group is not None and stage % 3 == 0 and len(scan_out) < per_group:
                states = scan_step(states)
        while scan_group is not None and len(scan_out) < per_group:
            states = scan_step(states)
        for d, chunk, (q_eff, m_neg, o_loc, s_loc, egl) in zip(dirs, chunks, local_out):
            qe_ref[d, chunk_rows(chunk, c), :] = q_eff
            mp_ref[d, chunk_rows(chunk, 2 * c), :] = m_neg
            ou_ref[d, chunk_rows(chunk, c), :] = o_loc
            nn_ref[d, chunk_rows(chunk, 2 * c), :] = s_loc
            egl_ref[d, chunk_rows(chunk, SUBLANES), :] = egl
        for step_chunks, outs in zip(scan_chunks, scan_out):
            for d, ch in enumerate(step_chunks):
                o_ref[d, chunk_rows(ch, c), :] = outs[d]
        return states

    zero_state = jnp.zeros((2 * c, 2 * c), F32)
    states = run_group(0, None, (zero_state, zero_state))
    states = lax.fori_loop(1, n_groups, lambda j, st: run_group(j, j - 1, st), states)
    run_group(None, n_groups - 1, states)

    ng = ng_ref[...]
    step = 256
    for r in range(0, l, step):
        o = o_ref[0, r:r + step, :] + o_ref[1, r:r + step, :]
        ms = _half_sums(o * o, lane_lo) * (1.0 / GDN_DIM)
        gate = qkvg_ref[0, r:r + step, 3 * LANES:4 * LANES]
        out_ref[0, r:r + step, :] = (o * lax.rsqrt(ms + EPS) * ng * _silu(gate)).astype(out_ref.dtype)


def _dot_sel_lhs(sel_bf16, x):
    hi, lo = _split2(x)
    return _dot(sel_bf16, hi) + _dot(sel_bf16, lo)


def _gdn_call(qkvg, ba, conv_w, gate_params, ng, n, nc):
    b, l, _ = qkvg.shape
    pairs = GDN_HEADS // 2
    n_chunks = l // GDN_CHUNK
    kern = functools.partial(_gdn_kernel, n=n, nc=nc, chunks_per_iter=GDN_CHUNKS_PER_GROUP)
    return pl.pallas_call(
        kern,
        grid=(b, pairs),
        in_specs=[
            pl.BlockSpec((1, l, 4 * LANES), lambda i, p: (i, 0, p)),
            pl.BlockSpec((1, l, LANES), lambda i, p: (i, 0, 0)),
            pl.BlockSpec((3, 4 * LANES), lambda i, p: (0, p)),
            pl.BlockSpec((2, LANES), lambda i, p: (0, 0)),
            pl.BlockSpec((1, LANES), lambda i, p: (0, 0)),
        ],
        out_specs=pl.BlockSpec((1, l, LANES), lambda i, p: (i, 0, p)),
        out_shape=jax.ShapeDtypeStruct((b, l, pairs * LANES), MIXER_OUT_DTYPE),
        scratch_shapes=[
            pltpu.VMEM((l + 5 * SUBLANES, 3 * LANES), F32),
            pltpu.VMEM((l, LANES), F32),
            pltpu.VMEM((l, LANES), F32),
            pltpu.VMEM((l, LANES), F32),
            pltpu.VMEM((2, l, LANES), F32),
            pltpu.VMEM((2, l, LANES), F32),
            pltpu.VMEM((2, l, LANES), BF16),
            pltpu.VMEM((2, 2 * l, LANES), BF16),
            pltpu.VMEM((2, l, LANES), F32),
            pltpu.VMEM((2, 2 * l, LANES), F32),
            pltpu.VMEM((2, n_chunks * SUBLANES, LANES), F32),
            pltpu.VMEM((2, l, LANES), F32),
        ],
        compiler_params=_cparams(("arbitrary", "arbitrary")),
        name="gated_deltanet",
    )(qkvg, ba, conv_w, gate_params, ng)


def _diff_kernel(*refs, key_start, n_sub, lam_init, aliased):
    if aliased:
        refs = refs[1:]
    q_ref, k_ref, vt_ref, lam_ref, ng_ref, o_ref = refs
    lp = lam_ref[...]
    lam = (jnp.exp(jnp.sum(lp[0:1] * lp[1:2], axis=-1, keepdims=True))
           - jnp.exp(jnp.sum(lp[2:3] * lp[3:4], axis=-1, keepdims=True)) + lam_init)
    lane = lax.broadcasted_iota(jnp.int32, (1, LANES), 1)
    halves = (lane < DIFF_DIM, lane >= DIFF_DIM)
    ng = ng_ref[...]
    k = k_ref[0, key_start:, :]
    vt = vt_ref[0, :, key_start:]
    tq = q_ref.shape[1] // n_sub

    def scores_of(i):
        q = q_ref[0, i * tq:(i + 1) * tq, :]
        return [_dot_nt(k, jnp.where(m, q, jnp.zeros_like(q))) for m in halves]

    ahead = scores_of(0)
    for i in range(n_sub):
        s = ahead
        if i + 1 < n_sub:
            ahead = scores_of(i + 1)
        e = [jnp.exp(x - jnp.max(x, axis=0, keepdims=True)) for x in s]
        pv = [_dot(vt, x.astype(BF16)) for x in e]
        parts = [x * (1.0 / jnp.sum(y, axis=0, keepdims=True)) for x, y in zip(pv, e)]
        ot = parts[0] - lam * parts[1]
        ot = ot * lax.rsqrt(jnp.mean(ot * ot, axis=0, keepdims=True) + EPS)
        o_ref[0, i * tq:(i + 1) * tq, :] = (ot.T * ng * (1.0 - lam_init)).astype(o_ref.dtype)


def _diff_call(dq, dk, dvt, lam_p, ng, lam_init, q_rows, first_block, n_q_blocks, key_start, n_sub, prev_out):
    b, l, _ = dq.shape
    aliased = prev_out is not None
    kern = functools.partial(_diff_kernel, key_start=key_start, n_sub=n_sub, lam_init=lam_init, aliased=aliased)
    row_of = lambda t: first_block + t
    in_specs = [
        pl.BlockSpec((1, q_rows, LANES), lambda i, h, t: (i, row_of(t), h)),
        pl.BlockSpec((1, l, LANES), lambda i, h, t: (i, 0, h)),
        pl.BlockSpec((1, LANES, l), lambda i, h, t: (i, h, 0)),
        pl.BlockSpec((4, DIFF_DIM), lambda i, h, t: (0, 0)),
        pl.BlockSpec((1, LANES), lambda i, h, t: (0, 0)),
    ]
    args = [dq, dk, dvt, lam_p, ng.reshape(1, LANES)]
    aliases = {}
    if aliased:
        in_specs = [pl.BlockSpec(memory_space=pl.ANY)] + in_specs
        args = [prev_out] + args
        aliases = {0: 0}
    return pl.pallas_call(
        kern,
        grid=(b, DIFF_HEADS, n_q_blocks),
        in_specs=in_specs,
        out_specs=pl.BlockSpec((1, q_rows, LANES), lambda i, h, t: (i, row_of(t), h)),
        out_shape=jax.ShapeDtypeStruct((b, l, DIFF_HEADS * LANES), MIXER_OUT_DTYPE),
        input_output_aliases=aliases,
        compiler_params=_cparams(("arbitrary", "arbitrary", "arbitrary")),
        name="diff_attention_ctx" if aliased else "diff_attention",
    )(*args)


def _swa_kernel(q_ref, k_ref, v_ref, sink_ref, o_ref, *, n, nc):
    t = pl.program_id(1)
    blk = SWA_BLOCK
    n_x = n // blk
    q = q_ref[0]
    lane = lax.broadcasted_iota(jnp.int32, (1, LANES), 1)
    lane_lo = lane < HEAD_DIM
    sink = sink_ref[...]
    group = SWA_HEADS // SWA_KV_HEADS

    def run(keys, vals, valid):
        head_of_row = lax.broadcasted_iota(jnp.int32, (group * blk, 1), 0) >> _log2(blk)
        kvs = range(SWA_KV_HEADS)
        kk = [keys[:, kvh * LANES:(kvh + 1) * LANES] for kvh in kvs]
        vv = [vals[:, kvh * LANES:(kvh + 1) * LANES] for kvh in kvs]
        qst, sk = [], []
        for kvh in kvs:
            q_rows = []
            sk_rows = jnp.zeros((group * blk, 1), F32)
            for g in range(group):
                h = kvh * group + g
                qp = q[:, (h // 2) * LANES:(h // 2 + 1) * LANES]
                q_rows.append(jnp.where(lane_lo if h % 2 == 0 else jnp.logical_not(lane_lo), qp, jnp.zeros_like(qp)))
                sk_rows = jnp.where(head_of_row == g, sink[:, h:h + 1], sk_rows)
            qst.append(jnp.concatenate(q_rows, axis=0))
            sk.append(sk_rows)
        s = [_dot_nt(x, y) for x, y in zip(qst, kk)]
        if valid is not None:
            s = [jnp.where(valid, x, NEG_INF) for x in s]
        mx = [jnp.maximum(jnp.max(x, axis=-1, keepdims=True), y) for x, y in zip(s, sk)]
        e = [jnp.exp(x - m) for x, m in zip(s, mx)]
        pv = [_dot(x.astype(BF16), y) for x, y in zip(e, vv)]
        den = [jnp.sum(x, axis=-1, keepdims=True) + jnp.exp(y - m) for x, y, m in zip(e, sk, mx)]
        outs = []
        for o, dn in zip(pv, den):
            o = o * (1.0 / dn)
            for g in range(0, group, 2):
                outs.append(jnp.where(lane_lo, o[g * blk:(g + 1) * blk], o[(g + 1) * blk:(g + 2) * blk]))
        o_ref[0] = jnp.concatenate(outs, axis=1).astype(o_ref.dtype)

    @pl.when(t < n_x)
    def _():
        start = pl.multiple_of(jnp.clip((t - 1) * blk, 0, n - 3 * blk), blk)
        keys = jnp.concatenate([k_ref[0, pl.ds(start, 3 * blk), :], k_ref[0, n:n + nc, :]], axis=0)
        vals = jnp.concatenate([v_ref[0, pl.ds(start, 3 * blk), :], v_ref[0, n:n + nc, :]], axis=0)
        shape = (group * blk, 3 * blk + nc)
        qpos = t * blk + (lax.broadcasted_iota(jnp.int32, shape, 0) & (blk - 1))
        col = lax.broadcasted_iota(jnp.int32, shape, 1)
        dist = qpos - (start + col)
        in_window = jnp.logical_and(dist <= SWA_WINDOW, dist >= -SWA_WINDOW)
        valid = jnp.logical_or(col >= 3 * blk, in_window)
        run(keys, vals, valid)

    @pl.when(t >= n_x)
    def _():
        run(k_ref[0, n:n + nc, :], v_ref[0, n:n + nc, :], None)


def _swa_call(q, k, v, sink, n, nc, with_ctx):
    b, l, _ = q.shape
    blk = SWA_BLOCK
    nt = (l if with_ctx else n) // blk
    kern = functools.partial(_swa_kernel, n=n, nc=nc)
    return pl.pallas_call(
        kern,
        grid=(b, nt),
        in_specs=[
            pl.BlockSpec((1, blk, SWA_HEADS * HEAD_DIM), lambda i, t: (i, t, 0)),
            pl.BlockSpec((1, l, 2 * LANES), lambda i, t: (i, 0, 0)),
            pl.BlockSpec((1, l, 2 * LANES), lambda i, t: (i, 0, 0)),
            pl.BlockSpec((1, LANES), lambda i, t: (0, 0)),
        ],
        out_specs=pl.BlockSpec((1, blk, SWA_HEADS * HEAD_DIM), lambda i, t: (i, t, 0)),
        out_shape=jax.ShapeDtypeStruct((b, l, SWA_HEADS * HEAD_DIM), MIXER_OUT_DTYPE),
        compiler_params=_cparams(("arbitrary", "arbitrary")),
        name="window_attention",
    )(q, k, v, sink)


def _dft_tables(n):
    h = n // 2
    r = 1 << (_log2(h) // 2)
    j = jnp.arange(h, dtype=jnp.int32)

    def tables(m):
        thin = lambda k: ((k[:, None] * m[None, :]) % (2 * n)).astype(F32) * (math.pi / n)
        a = thin(r * jnp.arange(h // r, dtype=jnp.int32))[:, None, :]
        b = thin(jnp.arange(r, dtype=jnp.int32))[None, :, :]
        cos = jnp.cos(a) * jnp.cos(b) - jnp.sin(a) * jnp.sin(b)
        sin = jnp.sin(a) * jnp.cos(b) + jnp.cos(a) * jnp.sin(b)
        return cos.reshape(h, h).astype(BF16), (-sin).reshape(h, h).astype(BF16)

    ce, se = tables(2 * j)
    co, so = tables(2 * j + 1)
    return ce, se, co, so, co.T, so.T


def _hyena_filter_kernel(feat_ref, w1_ref, b1_ref, w2_ref, b2_ref, freq_ref, w3f_ref, w3b_ref, dl_ref,
                         ce_ref, se_ref, co_ref, so_ref, ka_ref, kb_ref, km_ref):
    h = feat_ref.shape[1]
    assert h % 2 == 0
    freq = freq_ref[...]
    dl = dl_ref[...]
    row = lax.broadcasted_iota(jnp.int32, (h, 1), 0)

    def taps(part, w3_ref):
        feat = feat_ref[part]
        x = jnp.sin(freq[0:1] * (_dot_f32(feat, w1_ref[...]) + b1_ref[...]))
        x = jnp.sin(freq[1:2] * (_dot_f32(x, w2_ref[...]) + b2_ref[...]))
        return _dot_f32(x, w3_ref[...]) * jnp.exp(-feat[:, 0:1] * dl)

    fe, fo = taps(0, w3f_ref), taps(1, w3f_ref)
    be, bo = jnp.where(row == 0, 0.0, taps(2, w3b_ref)), taps(3, w3b_ref)
    ss = sum(jnp.sum(x * x, axis=0, keepdims=True) for x in (fe, fo, be, bo))
    sc = lax.rsqrt(ss + EPS)
    fe, fo, be, bo = (x * sc for x in (fe, fo, be, bo))
    sgn = jnp.where((row & 1) == 0, 1.0, -1.0)

    def dot2(t_ref, x):
        hi, lo = _split2(x)
        return _dot(t_ref[...], hi) + _dot(t_ref[...], lo)

    def bins(x_even, x_odd):
        ce, co = dot2(ce_ref, x_even), dot2(co_ref, x_odd)
        se, so = dot2(se_ref, x_even), dot2(so_ref, x_odd)
        return ce + co, se + so, ce - co, so - se

    f = bins(fe, fo)
    g = bins(be, bo)
    ka_ref[0, 0], ka_ref[0, 1], kb_ref[0, 0], kb_ref[0, 1] = (x + sgn * y for x, y in zip(f, g))
    mid_r = jnp.sum((fe + be) * sgn, axis=0, keepdims=True)
    mid_i = -jnp.sum((fo + bo) * sgn, axis=0, keepdims=True)
    km_ref[0] = jnp.concatenate([mid_r, mid_i, jnp.zeros((SUBLANES - 2, mid_r.shape[-1]), F32)], axis=0)


def _hyena_filter_call(feats, w1, b1, w2, b2, freq, w3, deltas, tables):
    h = feats.shape[1]
    hid = w2.shape[0]
    ch = deltas.shape[-1]
    tc = MXU_WIDTH
    nct = ch // tc
    const = lambda shape: pl.BlockSpec(shape, lambda o, j: (0,) * len(shape))
    spectrum = pl.BlockSpec((1, 2, h, tc), lambda o, j: (o, 0, 0, j))
    return pl.pallas_call(
        _hyena_filter_kernel,
        grid=(2, nct),
        in_specs=[
            const((4, h, hid)), const((hid, hid)), const((1, hid)), const((hid, hid)), const((1, hid)),
            const((2, hid)),
            pl.BlockSpec((hid, tc), lambda o, j: (0, (2 * o) * nct + j)),
            pl.BlockSpec((hid, tc), lambda o, j: (0, (2 * o + 1) * nct + j)),
            pl.BlockSpec((1, tc), lambda o, j: (0, j)),
        ] + [_resident((h, h))] * 4,
        out_specs=[spectrum, spectrum, pl.BlockSpec((1, SUBLANES, tc), lambda o, j: (o, 0, j))],
        out_shape=[
            jax.ShapeDtypeStruct((2, 2, h, ch), F32),
            jax.ShapeDtypeStruct((2, 2, h, ch), F32),
            jax.ShapeDtypeStruct((2, SUBLANES, ch), F32),
        ],
        compiler_params=_cparams(("arbitrary", "arbitrary")),
        name="hyena_filters",
    )(feats, w1, b1, w2, b2, freq, w3, w3, deltas, *tables[:4])


def _hyena_kernel(*refs, n, aliased):
    if aliased:
        refs = refs[1:]
    (v_ref, x1_ref, x2_ref, cwv_ref, cw1_ref, cw2_ref, ka_ref, kb_ref, km_ref, bias_ref,
     ce_ref, se_ref, co_ref, so_ref, cot_ref, sot_ref, o_ref, pad_ref, z_ref, zb_ref, p_ref, y_ref) = refs
    halo = SUBLANES
    tc = o_ref.shape[-1]
    h = n // 2
    rc = min(h, HY_ROW_CHUNK)
    lane_groups = tc // LANES
    zero_rows = jnp.zeros((halo, LANES), F32)
    for g in range(lane_groups):
        pad_ref[g, 0:halo, :] = zero_rows
        pad_ref[g, halo + n:2 * halo + n, :] = zero_rows

    def stage(ref):
        for r in range(0, n, 2 * rc):
            for g in range(lane_groups):
                pad_ref[g, halo + r:halo + r + 2 * rc, :] = ref[0, r:r + 2 * rc, g * LANES:(g + 1) * LANES]

    def conv_rows(cw, parity, r):
        first = halo + 2 * r + parity - 1
        taps = [jnp.concatenate([pad_ref[g, pl.ds(first + i, rc, stride=2), :] for g in range(lane_groups)], axis=1)
                for i in range(3)]
        return cw[0:1] * taps[0] + cw[1:2] * taps[1] + cw[2:3] * taps[2]

    def alt_sign(r):
        j = r + lax.broadcasted_iota(jnp.int32, (rc, 1), 0)
        return j, jnp.where((j & 1) == 0, 1.0, -1.0)

    stage(v_ref)
    cw = cwv_ref[...]
    for parity in range(2):
        for r in range(0, h, rc):
            z = conv_rows(cw, parity, r)
            z_ref[parity, r:r + rc, :] = z
            zb_ref[parity, r:r + rc, :] = z.astype(BF16)

    for o, (gate_ref, gate_cw_ref) in enumerate(((x1_ref, cw1_ref), (x2_ref, cw2_ref))):
        mid_r = jnp.zeros((1, tc), F32)
        mid_i = jnp.zeros((1, tc), F32)
        for r in range(0, h, rc):
            _, sgn = alt_sign(r)
            mid_r = mid_r + jnp.sum(z_ref[0, r:r + rc, :] * sgn, axis=0, keepdims=True)
            mid_i = mid_i - jnp.sum(z_ref[1, r:r + rc, :] * sgn, axis=0, keepdims=True)
        km_r = km_ref[o, 0:1, :]
        km_i = km_ref[o, 1:2, :]
        pm_r = (mid_r * km_r - mid_i * km_i) * (1.0 / n)
        pm_i = (mid_r * km_i + mid_i * km_r) * (1.0 / n)
        ze = zb_ref[0]
        zo = zb_ref[1]
        for r in range(0, h, rc):
            k, _ = alt_sign(r)
            rows = slice(r, r + rc)
            ce, co = _dot(ce_ref[rows, :], ze), _dot(co_ref[rows, :], zo)
            se, so = _dot(se_ref[rows, :], ze), _dot(so_ref[rows, :], zo)
            xa_r, xb_r, xa_i, xb_i = ce + co, ce - co, se + so, so - se
            wgt = jnp.where(k == 0, 0.5 / n, 1.0 / n)
            ka_r, ka_i = ka_ref[o, 0, rows, :], ka_ref[o, 1, rows, :]
            kb_r, kb_i = kb_ref[o, 0, rows, :], kb_ref[o, 1, rows, :]
            pa_r = (xa_r * ka_r - xa_i * ka_i) * wgt
            pa_i = (xa_r * ka_i + xa_i * ka_r) * wgt
            pb_r = (xb_r * kb_r - xb_i * kb_i) * wgt
            pb_i = (xb_r * kb_i + xb_i * kb_r) * wgt
            p_ref[0, rows, :] = (pa_r + pb_r).astype(BF16)
            p_ref[1, rows, :] = (pa_i - pb_i).astype(BF16)
            p_ref[2, rows, :] = (pa_r - pb_r).astype(BF16)
            p_ref[3, rows, :] = (pa_i + pb_i).astype(BF16)
        stage(gate_ref)
        cw = gate_cw_ref[...]
        bias = bias_ref[o:o + 1, :]
        for parity, (c_ref, s_ref, mid) in enumerate(((ce_ref, se_ref, pm_r), (cot_ref, sot_ref, -pm_i))):
            for r in range(0, h, rc):
                _, sgn = alt_sign(r)
                rows = slice(r, r + rc)
                y = (_dot(c_ref[rows, :], p_ref[2 * parity]) + _dot(s_ref[rows, :], p_ref[2 * parity + 1])
                     + sgn * mid)
                z = conv_rows(cw, parity, r) * (y + z_ref[parity, rows, :] * bias)
                if o == 0:
                    z_ref[parity, rows, :] = z
                    zb_ref[parity, rows, :] = z.astype(BF16)
                else:
                    for g in range(lane_groups):
                        y_ref[g, pl.ds(2 * r + parity, rc, stride=2), :] = z[:, g * LANES:(g + 1) * LANES]
    for r in range(0, n, 2 * rc):
        rows = slice(r, r + 2 * rc)
        o_ref[0, rows, :] = jnp.concatenate([y_ref[g, rows, :] for g in range(lane_groups)],
                                            axis=1).astype(o_ref.dtype)


def _hyena_call(u, conv_w, ka, kb, km, bias, tables, n, row_block, prev_out):
    b, l, _ = u.shape
    ch = bias.shape[-1]
    tc = MXU_WIDTH
    nct = ch // tc
    h = n // 2
    aliased = prev_out is not None
    kern = functools.partial(_hyena_kernel, n=n, aliased=aliased)
    once = pl.Buffered(1)
    in_specs = [
        pl.BlockSpec((1, n, tc), lambda j, i: (i, row_block, j)),
        pl.BlockSpec((1, n, tc), lambda j, i: (i, row_block, nct + j)),
        pl.BlockSpec((1, n, tc), lambda j, i: (i, row_block, 2 * nct + j)),
        pl.BlockSpec((3, tc), lambda j, i: (0, j)),
        pl.BlockSpec((3, tc), lambda j, i: (0, nct + j)),
        pl.BlockSpec((3, tc), lambda j, i: (0, 2 * nct + j)),
        pl.BlockSpec((2, 2, h, tc), lambda j, i: (0, 0, 0, j), pipeline_mode=once),
        pl.BlockSpec((2, 2, h, tc), lambda j, i: (0, 0, 0, j), pipeline_mode=once),
        pl.BlockSpec((2, SUBLANES, tc), lambda j, i: (0, 0, j)),
        pl.BlockSpec((2, tc), lambda j, i: (0, j)),
    ] + [_resident((h, h))] * 6
    args = [u, u, u, conv_w, conv_w, conv_w, ka, kb, km, bias, *tables]
    aliases = {}
    if aliased:
        in_specs = [pl.BlockSpec(memory_space=pl.ANY)] + in_specs
        args = [prev_out] + args
        aliases = {0: 0}
    return pl.pallas_call(
        kern,
        grid=(nct, b),
        in_specs=in_specs,
        out_specs=pl.BlockSpec((1, n, tc), lambda j, i: (i, row_block, j)),
        out_shape=jax.ShapeDtypeStruct((b, l, ch), MIXER_OUT_DTYPE),
        scratch_shapes=[
            pltpu.VMEM((tc // LANES, n + 2 * SUBLANES, LANES), F32),
            pltpu.VMEM((2, h, tc), F32),
            pltpu.VMEM((2, h, tc), BF16),
            pltpu.VMEM((4, h, tc), BF16),
            pltpu.VMEM((tc // LANES, n, LANES), F32),
        ],
        input_output_aliases=aliases,
        compiler_params=_cparams(("arbitrary", "arbitrary")),
        name="hyena_conv_n%d" % n,
    )(*args)


def _rope_tables(n, nc):
    rows = n // GRID_W
    row = jnp.repeat(jnp.arange(rows, dtype=F32), GRID_W)
    col = jnp.tile(jnp.arange(GRID_W, dtype=F32), rows)
    half = HEAD_DIM // 2
    inv = ROPE_BASE ** (-jnp.arange(0, half, 2, dtype=F32) / half)
    ar = row[:, None] * inv
    ac = col[:, None] * inv
    cos = jnp.concatenate([jnp.cos(ar), jnp.cos(ar), jnp.cos(ac), jnp.cos(ac)], axis=-1)
    sin = jnp.concatenate([-jnp.sin(ar), jnp.sin(ar), -jnp.sin(ac), jnp.sin(ac)], axis=-1)
    cos = jnp.concatenate([cos, jnp.ones((nc, HEAD_DIM), F32)], axis=0)
    sin = jnp.concatenate([sin, jnp.zeros((nc, HEAD_DIM), F32)], axis=0)
    return jnp.tile(cos, (1, LANES // HEAD_DIM)), jnp.tile(sin, (1, LANES // HEAD_DIM))


def _rope_partner_cols(width):
    d = np.arange(width)
    quarter = HEAD_DIM // 4
    return np.where((d % (2 * quarter)) < quarter, d + quarter, d - quarter)


def _hyena_feats(n):
    pos = jnp.arange(n, dtype=F32)
    t = pos / max(n - 1, 1)
    ang = (2.0 * math.pi * pos / n)[:, None] * jnp.linspace(1e-4, HY_BANDS - 1, HY_BANDS, dtype=F32)[None, :]
    feats = jnp.concatenate([t[:, None], jnp.cos(ang), -jnp.sin(ang)], axis=-1)
    feats = jnp.pad(feats, ((0, 0), (0, 64 - feats.shape[-1])))
    back = jnp.concatenate([feats[0:1], jnp.flip(feats[1:], axis=0)], axis=0)
    return jnp.stack([feats[0::2], feats[1::2], back[0::2], back[1::2]])


def _pad_cols(w, width):
    return jnp.pad(w, ((0, 0), (0, width - w.shape[-1])))


def _layer_ab(xz, mod, norm_g0, w_in, conv_w, a_log, dt_bias, gdn_g, lam_p, diff_g, lam_init, rope, n, nc):
    hd = GDN_HEADS * GDN_DIM
    wq, wk, wv, wg = (w_in[:, i * hd:(i + 1) * hd] for i in range(4))
    o = 4 * hd
    w_beta, w_alpha = w_in[:, o:o + 16], w_in[:, o + 16:o + 32]
    o += 32
    dd = DIFF_HEADS * 2 * DIFF_DIM
    wdq, wdk, wdv = (w_in[:, o + i * dd:o + (i + 1) * dd] for i in range(3))
    pairs = GDN_HEADS // 2
    pair_cols = lambda w: [w[:, p * LANES:(p + 1) * LANES] for p in range(pairs)]
    w_qkvg = jnp.concatenate([blk for grp in zip(pair_cols(wq), pair_cols(wk), pair_cols(wv), pair_cols(wg))
                              for blk in grp], axis=1)
    perm = _rope_partner_cols(dd)
    w_all = jnp.concatenate([w_qkvg, _pad_cols(jnp.concatenate([w_beta, w_alpha], axis=1), LANES),
                             wdq, wdk, wdv, wdq[:, perm], wdk[:, perm]], axis=1).astype(BF16)
    c0 = 4 * hd
    c1 = c0 + LANES
    segs = (_Seg(0, c0), _Seg(c0, LANES),
            _Seg(c1, dd, rot_start=c1 + 3 * dd, scale=DIFF_DIM ** -0.5, dtype=BF16),
            _Seg(c1 + dd, dd, rot_start=c1 + 4 * dd, dtype=BF16),
            _Seg(c1 + 2 * dd, dd, dtype=BF16, transposed=True))
    qkvg, ba, dq, dk, dvt = _proj_call(xz, mod, norm_g0, w_all, rope[0], rope[1], segs, n // ROW_TILE, "proj_ab")

    cq, ck, cv = (conv_w[:, i * hd:(i + 1) * hd] for i in range(3))
    zeros = jnp.zeros((3, LANES), F32)
    conv_l = jnp.concatenate([blk for p in range(pairs) for blk in
                              (cq[:, p * LANES:(p + 1) * LANES], ck[:, p * LANES:(p + 1) * LANES],
                               cv[:, p * LANES:(p + 1) * LANES], zeros)], axis=1)
    n_gate = 2 * GDN_HEADS
    on_decay_lanes = lambda t: jnp.pad(t.reshape(1, n_gate), ((0, 0), (n_gate, LANES - 2 * n_gate)))
    gate_params = jnp.concatenate([on_decay_lanes(a_log), on_decay_lanes(dt_bias)], axis=0)
    ng = jnp.tile(gdn_g.reshape(1, GDN_DIM), (1, 2))
    oa = _gdn_call(qkvg, ba, conv_l, gate_params, ng, n, nc)
    q_rows = DIFF_SUB_TILES * ROW_TILE
    ob = _diff_call(dq, dk, dvt, lam_p, diff_g, lam_init, q_rows, 0, n // q_rows, 0, DIFF_SUB_TILES, None)
    ob = _diff_call(dq, dk, dvt, lam_p, diff_g, lam_init, nc, n // nc, 1, n, 1, ob)
    return oa, ob


def _layer_cd(xz, mod, norm_g0, w_in, sink, hy_conv, hy_w1, hy_b1, hy_w2, hy_b2, hy_w3, hy_freq, hy_bias,
              rope, n, nc, last, dft_x, dft_c):
    qd = SWA_HEADS * HEAD_DIM
    kd = SWA_KV_HEADS * HEAD_DIM
    wq, wk, wv, wu = w_in[:, 0:qd], w_in[:, qd:qd + kd], w_in[:, qd + kd:qd + 2 * kd], w_in[:, qd + 2 * kd:]
    dup = lambda w: jnp.concatenate([w[:, 0:HEAD_DIM], w[:, 0:HEAD_DIM], w[:, HEAD_DIM:], w[:, HEAD_DIM:]], axis=1)
    wk2, wv2 = dup(wk), dup(wv)
    ud = wu.shape[1]
    w_all = jnp.concatenate([wq, wk2, wv2, wu, wq[:, _rope_partner_cols(qd)], wk2[:, _rope_partner_cols(2 * kd)]],
                            axis=1).astype(BF16)
    o_u = qd + 4 * kd
    segs = (_Seg(0, qd, rot_start=o_u + ud, scale=HEAD_DIM ** -0.5, dtype=BF16),
            _Seg(qd, 2 * kd, rot_start=o_u + ud + qd, dtype=BF16),
            _Seg(qd + 2 * kd, 2 * kd, dtype=BF16), _Seg(o_u, ud))
    q, k, v, u = _proj_call(xz, mod, norm_g0, w_all, rope[0], rope[1], segs, n // ROW_TILE, "proj_cd")
    oc = _swa_call(q, k, v, _pad_cols(sink.reshape(1, SWA_HEADS), LANES), n, nc, not last)

    ch = hy_bias.shape[-1]
    deltas = jnp.abs(jnp.linspace(HY_MIN_DECAY, HY_MAX_DECAY, ch, dtype=F32)).reshape(1, ch)
    hid = hy_w2.shape[0]
    w1p = jnp.pad(hy_w1, ((0, hid - hy_w1.shape[0]), (0, 0)))
    filt = lambda m, dft: _hyena_filter_call(_hyena_feats(m), w1p, hy_b1.reshape(1, hid), hy_w2,
                                             hy_b2.reshape(1, hid), hy_freq, hy_w3, deltas, dft)
    od = _hyena_call(u, hy_conv, *filt(n, dft_x), hy_bias, dft_x, n, 0, None)
    if not last:
        od = _hyena_call(u, hy_conv, *filt(nc, dft_c), hy_bias, dft_c, nc, n // nc, od)
    return oc, od


def kernel(x, c, ctx, c_ctx, w_mod, b_mod, norm_g, ffn_w_up, ffn_conv, ffn_w_down, ab_w_in, ab_w_out, gdn_conv, gdn_a_log, gdn_dt_bias, gdn_norm_g, diff_lambda, diff_norm_g, cd_w_in, cd_w_out, swa_sink, hy_conv, hy_w1, hy_b1, hy_w2, hy_b2, hy_w3, hy_freq, hy_bias):
    b, n, d = x.shape
    nc = ctx.shape[1]
    depth = w_mod.shape[0]
    assert n % ROW_TILE == 0 and nc == ROW_TILE and n % GRID_W == 0
    xz = jnp.concatenate([x, ctx], axis=1)
    rows = -(-(b + 1) // SUBLANES) * SUBLANES
    cc = jnp.concatenate([c, c_ctx[None], jnp.zeros((rows - b - 1, d), F32)], axis=0)
    mods = _mod_call(cc, w_mod, b_mod)
    mod_all = jnp.concatenate([mods[:, :b].reshape(depth, b, 1, 6, d),
                               jnp.broadcast_to(mods[:, b].reshape(depth, 1, 1, 6, d), (depth, b, 1, 6, d))], axis=2)
    rope = _rope_tables(n, nc)
    dft_x = _dft_tables(n)
    dft_c = _dft_tables(nc)
    n_x_tiles = n // ROW_TILE
    for l in range(depth):
        last = l == depth - 1
        i = l // 2
        mod = mod_all[l]
        if l % 2 == 0:
            lam_init = 0.8 - 0.6 * math.exp(-0.3 * l)
            o1, o2 = _layer_ab(xz, mod, norm_g[l, 0], ab_w_in[i], gdn_conv[i], gdn_a_log[i], gdn_dt_bias[i],
                               gdn_norm_g[i], diff_lambda[i], diff_norm_g[i], lam_init, rope, n, nc)
            w_out = ab_w_out[i]
        else:
            o1, o2 = _layer_cd(xz, mod, norm_g[l, 0], cd_w_in[i], swa_sink[i], hy_conv[i], hy_w1[i], hy_b1[i],
                               hy_w2[i], hy_b2[i], hy_w3[i], hy_freq[i], hy_bias[i], rope, n, nc, last, dft_x, dft_c)
            w_out = cd_w_out[i]
        n_tiles = (n if last else n + nc) // ROW_TILE
        xz = _post_call(o1, o2, xz, mod, norm_g[l, 1], norm_g[l, 2], norm_g[l, 3], w_out.astype(BF16),
                        ffn_w_up[l].astype(BF16), ffn_conv[l], ffn_w_down[l].astype(BF16), n_tiles, n_x_tiles)
    return xz
```

```python
import functools
import math
from typing import NamedTuple, Optional

import jax
import jax.numpy as jnp
import numpy as np
from jax import lax
from jax.experimental import pallas as pl
from jax.experimental.pallas import tpu as pltpu

F32 = jnp.float32
BF16 = jnp.bfloat16
MIXER_OUT_DTYPE = BF16

EPS = 1e-6
NEG_INF = -1e30
GRID_W = 64
HEAD_DIM = 64
ROPE_BASE = 10000.0
GDN_HEADS = 8
GDN_DIM = 64
GDN_CHUNK = 64
GDN_CHUNKS_PER_GROUP = 9
DIFF_HEADS = 4
DIFF_DIM = 64
DIFF_SUB_TILES = 8
SWA_HEADS = 8
SWA_KV_HEADS = 2
SWA_WINDOW = 128
SWA_BLOCK = 128
HY_BANDS = 16
HY_MIN_DECAY = math.log(1e-2) / 1.5
HY_MAX_DECAY = math.log(1e-2) / 0.3
HY_ROW_CHUNK = 512

LANES = 128
SUBLANES = 8
MXU_WIDTH = 256
FFN_COL_CHUNK = 3 * MXU_WIDTH
ROW_TILE = 256
VMEM_LIMIT = 56 * 1024 * 1024


def _cparams(sem):
    return pltpu.CompilerParams(dimension_semantics=sem, vmem_limit_bytes=VMEM_LIMIT)


def _resident(shape):
    zeros = (0,) * len(shape)
    return pl.BlockSpec(shape, lambda *_: zeros, pipeline_mode=pl.Buffered(1))


def _log2(v):
    assert v & (v - 1) == 0
    return v.bit_length() - 1


def _sigmoid(x):
    return 1.0 / (1.0 + jnp.exp(-x))


def _silu(x):
    return x * _sigmoid(x)


def _softplus(x):
    return jnp.maximum(x, 0.0) + jnp.log1p(jnp.exp(-jnp.abs(x)))


def _dot(a, b):
    return jnp.dot(a, b, preferred_element_type=F32)


def _dot_nt(a, b):
    return lax.dot_general(a, b, (((1,), (1,)), ((), ())), preferred_element_type=F32)


def _dot_tn(a, b):
    return lax.dot_general(a, b, (((0,), (0,)), ((), ())), preferred_element_type=F32)


def _dot_f32(a, b):
    return jnp.dot(a, b, preferred_element_type=F32, precision=lax.Precision.HIGHEST)


def _split2(x):
    hi = x.astype(BF16)
    lo = (x - hi.astype(F32)).astype(BF16)
    return hi, lo


def _dot_sel(x, sel_bf16):
    hi, lo = _split2(x)
    return _dot(hi, sel_bf16) + _dot(lo, sel_bf16)


def _mm(a, b):
    return _dot(a.astype(BF16), b.astype(BF16))


def _rms(y, g):
    return y * lax.rsqrt(jnp.mean(y * y, axis=-1, keepdims=True) + EPS) * g


def _modnorm(x, g, shift, scale):
    return _rms(x, g) * (1.0 + scale) + shift


def _mod_kernel(cc_ref, w_ref, b_ref, o_ref):
    s = _silu(cc_ref[...])
    o_ref[0] = _dot(s.astype(BF16), w_ref[0].astype(BF16)) + b_ref[0]


def _mod_call(cc, w_mod, b_mod):
    depth, d, nm = w_mod.shape
    rows = cc.shape[0]
    ct = 1536
    return pl.pallas_call(
        _mod_kernel,
        grid=(depth, nm // ct),
        in_specs=[
            pl.BlockSpec((rows, d), lambda l, j: (0, 0)),
            pl.BlockSpec((1, d, ct), lambda l, j: (l, 0, j)),
            pl.BlockSpec((1, 1, ct), lambda l, j: (l, 0, j)),
        ],
        out_specs=pl.BlockSpec((1, rows, ct), lambda l, j: (l, 0, j)),
        out_shape=jax.ShapeDtypeStruct((depth, rows, nm), F32),
        compiler_params=_cparams(("arbitrary", "arbitrary")),
        name="adaln_mod",
    )(cc, w_mod, b_mod.reshape(depth, 1, nm))


class _Seg(NamedTuple):
    start: int
    width: int
    rot_start: Optional[int] = None
    scale: float = 1.0
    dtype: type = F32
    transposed: bool = False


def _proj_kernel(x_ref, mod_ref, g_ref, w_ref, cos_ref, sin_ref, *out_refs, segs):
    m = mod_ref[0, 0]
    h = _modnorm(x_ref[0], g_ref[...], m[0:1], m[1:2]).astype(BF16)
    for o_ref, seg in zip(out_refs, segs):
        y = _dot(h, w_ref[:, seg.start:seg.start + seg.width])
        if seg.rot_start is not None:
            yr = _dot(h, w_ref[:, seg.rot_start:seg.rot_start + seg.width])
            reps = seg.width // LANES
            cos = jnp.concatenate([cos_ref[...]] * reps, axis=1)
            sin = jnp.concatenate([sin_ref[...]] * reps, axis=1)
            y = y * cos + yr * sin
        if seg.scale != 1.0:
            y = y * seg.scale
        if seg.transposed:
            y = y.T
        o_ref[0] = y.astype(seg.dtype)


def _proj_call(xz, mod, g, w, cos_t, sin_t, segs, n_x_tiles, name):
    b, l, d = xz.shape
    tm = ROW_TILE
    nt = l // tm
    p = w.shape[1]
    return pl.pallas_call(
        functools.partial(_proj_kernel, segs=segs),
        grid=(nt, b),
        in_specs=[
            pl.BlockSpec((1, tm, d), lambda t, i: (i, t, 0)),
            pl.BlockSpec((1, 1, 6, d), lambda t, i: (i, t // n_x_tiles, 0, 0)),
            pl.BlockSpec((1, d), lambda t, i: (0, 0)),
            _resident((d, p)),
            pl.BlockSpec((tm, LANES), lambda t, i: (t, 0)),
            pl.BlockSpec((tm, LANES), lambda t, i: (t, 0)),
        ],
        out_specs=[pl.BlockSpec((1, s.width, tm), lambda t, i: (i, 0, t)) if s.transposed
                   else pl.BlockSpec((1, tm, s.width), lambda t, i: (i, t, 0)) for s in segs],
        out_shape=[jax.ShapeDtypeStruct((b, s.width, l) if s.transposed else (b, l, s.width), s.dtype)
                   for s in segs],
        compiler_params=_cparams(("arbitrary", "arbitrary")),
        name=name,
    )(xz, mod, g.reshape(1, d), w, cos_t, sin_t)


def _post_kernel(o1p_ref, o1_ref, o1n_ref, o2p_ref, o2_ref, o2n_ref, xp_ref, x_ref, xn_ref, mod_ref,
                 g1_ref, g2_ref, g3_ref, wout_ref, wup_ref, cw_ref, wdn_ref, out_ref, up_ref,
                 *, tm, n_x_tiles, n_tiles, cf, dff):
    t = pl.program_id(0)
    first = jnp.logical_or(t == 0, t == n_x_tiles)
    last = jnp.logical_or(t == n_x_tiles - 1, t == n_tiles - 1)
    m = mod_ref[0, 0]
    halo = SUBLANES
    ohalo = o1p_ref.shape[1]
    k1 = o1_ref.shape[-1]
    o1e = jnp.concatenate([o1p_ref[0], o1_ref[0], o1n_ref[0]], axis=0)
    o2e = jnp.concatenate([o2p_ref[0], o2_ref[0], o2n_ref[0]], axis=0)
    y = _dot(o1e, wout_ref[0:k1, :]) + _dot(o2e, wout_ref[k1:, :])
    y = y[ohalo - halo:ohalo + tm + halo]
    xe = jnp.concatenate([xp_ref[0], x_ref[0], xn_ref[0]], axis=0)
    x1 = xe + m[2:3] * _rms(y, g1_ref[...])
    h = _modnorm(x1, g2_ref[...], m[3:4], m[4:5]).astype(BF16)
    acc = jnp.zeros((tm, x_ref.shape[-1]), F32)
    for c0 in range(0, dff, cf):
        wd = min(cf, dff - c0)
        halves = []
        for half, base in enumerate((c0, dff + c0)):
            u = _dot(h, wup_ref[:, base:base + wd])
            up_ref[half, :, 0:wd] = u
            up_ref[half, 0:halo, 0:wd] = jnp.where(first, 0.0, u[0:halo])
            up_ref[half, tm + halo:tm + 2 * halo, 0:wd] = jnp.where(last, 0.0, u[tm + halo:])
            cw = cw_ref[:, base:base + wd]
            halves.append(cw[0:1] * up_ref[half, halo - 1:halo - 1 + tm, 0:wd]
                          + cw[1:2] * up_ref[half, halo:halo + tm, 0:wd]
                          + cw[2:3] * up_ref[half, halo + 1:halo + 1 + tm, 0:wd])
        act = (_silu(halves[1]) * halves[0]).astype(BF16)
        acc = acc + _dot(act, wdn_ref[c0:c0 + wd, :])
    out_ref[0] = x1[halo:halo + tm] + m[5:6] * _rms(acc, g3_ref[...])


def _post_call(o1, o2, xz, mod, g1, g2, g3, w_out, w_up, conv_w, w_down, n_tiles, n_x_tiles):
    b, _, d = xz.shape
    tm = ROW_TILE
    rows = n_tiles * tm
    dff = w_down.shape[0]
    cf = FFN_COL_CHUNK
    k1, k2 = o1.shape[-1], o2.shape[-1]
    ohalo = 2 * SUBLANES
    kern = functools.partial(_post_kernel, tm=tm, n_x_tiles=n_x_tiles, n_tiles=n_tiles, cf=cf, dff=dff)

    def with_halos(width, halo_rows):
        per_tile = tm // halo_rows
        n_blocks = rows // halo_rows
        return [
            pl.BlockSpec((1, halo_rows, width), lambda t, i: (i, jnp.maximum(t * per_tile - 1, 0), 0)),
            pl.BlockSpec((1, tm, width), lambda t, i: (i, t, 0)),
            pl.BlockSpec((1, halo_rows, width), lambda t, i: (i, jnp.minimum((t + 1) * per_tile, n_blocks - 1), 0)),
        ]

    row_vec = pl.BlockSpec((1, d), lambda t, i: (0, 0))
    return pl.pallas_call(
        kern,
        grid=(n_tiles, b),
        in_specs=with_halos(k1, ohalo) + with_halos(k2, ohalo) + with_halos(d, SUBLANES) + [
            pl.BlockSpec((1, 1, 6, d), lambda t, i: (i, t // n_x_tiles, 0, 0)),
            row_vec, row_vec, row_vec,
            _resident((k1 + k2, d)),
            _resident((d, 2 * dff)),
            pl.BlockSpec((3, 2 * dff), lambda t, i: (0, 0)),
            _resident((dff, d)),
        ],
        out_specs=pl.BlockSpec((1, tm, d), lambda t, i: (i, t, 0)),
        out_shape=jax.ShapeDtypeStruct((b, rows, d), F32),
        scratch_shapes=[pltpu.VMEM((2, tm + 2 * SUBLANES, cf), F32)],
        compiler_params=_cparams(("arbitrary", "arbitrary")),
        name="mixer_out_conv_ffn",
    )(o1, o1, o1, o2, o2, o2, xz, xz, xz, mod, g1.reshape(1, d), g2.reshape(1, d), g3.reshape(1, d),
      w_out, w_up, conv_w, w_down)


def _half_sums(x2, lane_lo):
    s0 = jnp.sum(jnp.where(lane_lo, x2, 0.0), axis=-1, keepdims=True)
    s1 = jnp.sum(jnp.where(lane_lo, 0.0, x2), axis=-1, keepdims=True)
    return jnp.where(lane_lo, s0, s1)


def _gdn_kernel(qkvg_ref, ba_ref, cw_ref, gp_ref, ng_ref, out_ref,
                pad_ref, q_ref, k_ref, v_ref, bb_ref, gb_ref, qe_ref, mp_ref, ou_ref, nn_ref, egl_ref, o_ref,
                *, n, nc, chunks_per_iter):
    l = n + nc
    c = GDN_CHUNK
    n_chunks = l // c
    pair = pl.program_id(1)
    halo = SUBLANES
    lane = lax.broadcasted_iota(jnp.int32, (1, LANES), 1)
    lane_lo = lane < GDN_DIM

    cw = cw_ref[:, 0:3 * LANES]
    zero_rows = jnp.zeros((halo, 3 * LANES), F32)
    for seq_start, seq_len in ((0, n), (n, nc)):
        base = halo + seq_start + (2 * halo if seq_start else 0)
        pad_ref[base - halo:base, :] = zero_rows
        pad_ref[base + seq_len:base + seq_len + halo, :] = zero_rows
        step = 256
        for r in range(0, seq_len, step):
            pad_ref[base + r:base + r + step, :] = qkvg_ref[0, seq_start + r:seq_start + r + step, 0:3 * LANES]
        for r in range(0, seq_len, step):
            y = (cw[0:1] * pad_ref[base + r - 1:base + r - 1 + step, :]
                 + cw[1:2] * pad_ref[base + r:base + r + step, :]
                 + cw[2:3] * pad_ref[base + r + 1:base + r + 1 + step, :])
            y = _silu(y)
            q = y[:, 0:LANES]
            k = y[:, LANES:2 * LANES]
            rows = slice(seq_start + r, seq_start + r + step)
            q_ref[rows, :] = q * lax.rsqrt(_half_sums(q * q, lane_lo) + EPS) * (GDN_DIM ** -0.5)
            k_ref[rows, :] = k * lax.rsqrt(_half_sums(k * k, lane_lo) + EPS)
            v_ref[rows, :] = y[:, 2 * LANES:3 * LANES]

    sel_r = lax.broadcasted_iota(jnp.int32, (LANES, 4 * LANES), 0)
    sel_c = lax.broadcasted_iota(jnp.int32, (LANES, 4 * LANES), 1)
    quarter = sel_c >> _log2(LANES)
    src_lane = (quarter & 1) * 2 * GDN_HEADS + (quarter >> 1) * GDN_HEADS + 2 * pair + ((sel_c >> _log2(GDN_DIM)) & 1)
    sel = (sel_r == src_lane).astype(BF16)
    gblk = 256
    bi = lax.broadcasted_iota(jnp.int32, (gblk, gblk), 0)
    bj = lax.broadcasted_iota(jnp.int32, (gblk, gblk), 1)
    same_chunk = (bi >> _log2(c)) == (bj >> _log2(c))
    csum = (jnp.logical_and(same_chunk, bi >= bj).astype(BF16), jnp.logical_and(same_chunk, bi <= bj).astype(BF16))
    neg_a = -jnp.exp(gp_ref[0:1, :])
    dt_bias = gp_ref[1:2, :]
    for r in range(0, l, gblk):
        ba = ba_ref[0, r:r + gblk, :]
        gates = jnp.where(lane < 2 * GDN_HEADS, _sigmoid(ba), neg_a * _softplus(ba + dt_bias))
        x = _dot_sel(gates, sel)
        for d in range(2):
            bb_ref[d, r:r + gblk, :] = x[:, 2 * d * LANES:(2 * d + 1) * LANES]
            gb_ref[d, r:r + gblk, :] = _dot_sel_lhs(csum[d], x[:, (2 * d + 1) * LANES:(2 * d + 2) * LANES])

    r2 = lax.broadcasted_iota(jnp.int32, (2 * c, 2 * c), 0)
    c2 = lax.broadcasted_iota(jnp.int32, (2 * c, 2 * c), 1)
    same_head = (r2 >= c) == (c2 >= c)
    eye = (r2 == c2).astype(F32)
    masks = ((jnp.logical_and(same_head, r2 >= c2), jnp.logical_and(same_head, r2 > c2)),
             (jnp.logical_and(same_head, r2 <= c2), jnp.logical_and(same_head, r2 < c2)))
    m0 = lane_lo.astype(F32)
    m1 = 1.0 - m0

    def pair_mask(lv, lower):
        same_block = (r2 >> (lv + 1)) == (c2 >> (lv + 1))
        r_hi = ((r2 >> lv) & 1) == 1
        c_hi = ((c2 >> lv) & 1) == 1
        off = jnp.logical_and(r_hi, jnp.logical_not(c_hi)) if lower else jnp.logical_and(c_hi, jnp.logical_not(r_hi))
        return jnp.logical_and(same_block, off)

    pair_masks = tuple(tuple(pair_mask(lv, lower) for lv in range(_log2(c))) for lower in (True, False))

    def stack_heads(x2):
        return jnp.concatenate([x2 * m0, x2 * m1], axis=0)

    def fold_heads(x):
        return x[0:c] + x[c:2 * c]

    def local_stages(dirs, qs, ks, vs, betas, gcs, out):
        each = lambda f, *cols: [f(*args) for args in zip(*cols)]
        incl = [masks[d][0] for d in dirs]
        strict = [masks[d][1] for d in dirs]
        g1 = each(lambda gc2: jnp.concatenate([gc2, gc2], axis=0), gcs)
        decay = each(lambda g, m: jnp.where(m, jnp.exp(jnp.where(m, g - g.T, 0.0)), 0.0), g1, incl)
        kb = each(lambda k, b: k * b, ks, betas)
        kst = each(lambda k: stack_heads(k).astype(BF16), ks)
        a_raw = each(lambda x, y: _dot_nt(stack_heads(x).astype(BF16), y), kb, kst)
        qk_raw = each(lambda x, y: _dot_nt(stack_heads(x).astype(BF16), y), qs, kst)
        yield
        qk = each(lambda m, x, dc: jnp.where(m, x * dc, 0.0).astype(BF16), incl, qk_raw, decay)
        a = each(lambda m, x, dc: jnp.where(m, x * dc, 0.0), strict, a_raw, decay)
        tinv = each(lambda d, x: eye - jnp.where(pair_masks[d][0], x, 0.0), dirs, a)
        for lv in range(1, _log2(c)):
            ta = each(lambda d, t, x: _mm(t, jnp.where(pair_masks[d][lv], x, 0.0)), dirs, tinv, a)
            yield
            tat = each(_mm, ta, tinv)
            yield
            tinv = each(lambda t, x: t - x, tinv, tat)
        egc = each(jnp.exp, gcs)
        rhs = each(lambda v, b, x, e: jnp.concatenate([stack_heads(v * b), stack_heads(x * e)], axis=1),
                   vs, betas, kb, egc)
        sol = each(_mm, tinv, rhs)
        yield
        u2 = each(lambda x: fold_heads(x[:, 0:LANES]), sol)
        w2 = each(lambda x: fold_heads(x[:, LANES:2 * LANES]), sol)
        gl = each(lambda d, gc2: gc2[c - 1:c, :] if d == 0 else gc2[0:1, :], dirs, gcs)
        ktail = each(lambda k, g, gc2: (k * jnp.exp(g - gc2)).astype(BF16), ks, gl, gcs)
        qwu = each(lambda x, w, u: _dot(x, jnp.concatenate([stack_heads(w), stack_heads(u)], axis=1).astype(BF16)),
                   qk, w2, u2)
        kwu = each(lambda x, w, u: _dot_tn(x, jnp.concatenate([w, u], axis=1).astype(BF16)), ktail, w2, u2)
        yield
        q_eff = each(lambda q, e, x: (q * e - fold_heads(x[:, 0:LANES])).astype(BF16), qs, egc, qwu)
        m_neg = each(lambda x: jnp.where(same_head, -x[:, 0:LANES], 0.0).astype(BF16), kwu)
        o_loc = each(lambda x: fold_heads(x[:, LANES:2 * LANES]), qwu)
        s_loc = each(lambda x: jnp.where(same_head, x[:, LANES:2 * LANES], 0.0), kwu)
        egl = each(lambda g: jnp.broadcast_to(jnp.exp(g), (SUBLANES, LANES)), gl)
        out.extend(zip(q_eff, m_neg, o_loc, s_loc, egl))

    def chunk_rows(chunk, rows_per_chunk):
        return pl.ds(pl.multiple_of(chunk * rows_per_chunk, rows_per_chunk), rows_per_chunk)

    ctx_chunks = nc // c
    per_group = chunks_per_iter
    n_groups = n_chunks // per_group

    def chunks_at(step):
        return jnp.where(step < ctx_chunks, step + n // c, step - ctx_chunks), n_chunks - 1 - step

    def run_group(local_group, scan_group, states):
        dirs, chunks, qs, ks, vs, betas, gcs = [], [], [], [], [], [], []
        if local_group is not None:
            for g in range(per_group):
                for d, chunk in enumerate(chunks_at(per_group * local_group + g)):
                    rows = chunk_rows(chunk, c)
                    dirs.append(d)
                    chunks.append(chunk)
                    qs.append(q_ref[rows, :])
                    ks.append(k_ref[rows, :])
                    vs.append(v_ref[rows, :])
                    betas.append(bb_ref[d, rows, :])
                    gcs.append(gb_ref[d, rows, :])
        scan_chunks, scan_in = [], []
        if scan_group is not None:
            for g in range(per_group):
                step_chunks = chunks_at(per_group * scan_group + g)
                scan_chunks.append(step_chunks)
                scan_in.append([(qe_ref[d, chunk_rows(ch, c), :], mp_ref[d, chunk_rows(ch, 2 * c), :],
                                 ou_ref[d, chunk_rows(ch, c), :], nn_ref[d, chunk_rows(ch, 2 * c), :],
                                 egl_ref[d, chunk_rows(ch, SUBLANES), :]) for d, ch in enumerate(step_chunks)])
        scan_out = []

        def scan_step(states):
            loaded = scan_in[len(scan_out)]
            res = [_dot(jnp.concatenate([ld[0], ld[1]], axis=0), s2.astype(BF16)) for ld, s2 in zip(loaded, states)]
            scan_out.append([r[0:c] + ld[2] for r, ld in zip(res, loaded)])
            return tuple(s2 * ld[4][0:1] + r[c:3 * c] + ld[3] for s2, ld, r in zip(states, loaded, res))

        local_out = []
        stages = local_stages(dirs, qs, ks, vs, betas, gcs, local_out) if local_group is not None else iter(())
        n_stages = 2 * _log2(c) + 1
        every = max(1, n_stages // per_group)
        for stage, _ in enumerate(stages):
            if scan_group is not None and stage % every == 0 and len(scan_out) < per_group:
                states = scan_step(states)
        while scan_group is not None and len(scan_out) < per_group:
            states = scan_step(states)
        for d, chunk, (q_eff, m_neg, o_loc, s_loc, egl) in zip(dirs, chunks, local_out):
            qe_ref[d, chunk_rows(chunk, c), :] = q_eff
            mp_ref[d, chunk_rows(chunk, 2 * c), :] = m_neg
            ou_ref[d, chunk_rows(chunk, c), :] = o_loc
            nn_ref[d, chunk_rows(chunk, 2 * c), :] = s_loc
            egl_ref[d, chunk_rows(chunk, SUBLANES), :] = egl
        for step_chunks, outs in zip(scan_chunks, scan_out):
            for d, ch in enumerate(step_chunks):
                o_ref[d, chunk_rows(ch, c), :] = outs[d]
        return states

    zero_state = jnp.zeros((2 * c, 2 * c), F32)
    states = run_group(0, None, (zero_state, zero_state))
    states = lax.fori_loop(1, n_groups, lambda j, st: run_group(j, j - 1, st), states)
    run_group(None, n_groups - 1, states)

    ng = ng_ref[...]
    step = 256
    for r in range(0, l, step):
        o = o_ref[0, r:r + step, :] + o_ref[1, r:r + step, :]
        ms = _half_sums(o * o, lane_lo) * (1.0 / GDN_DIM)
        gate = qkvg_ref[0, r:r + step, 3 * LANES:4 * LANES]
        out_ref[0, r:r + step, :] = (o * lax.rsqrt(ms + EPS) * ng * _silu(gate)).astype(out_ref.dtype)


def _dot_sel_lhs(sel_bf16, x):
    hi, lo = _split2(x)
    return _dot(sel_bf16, hi) + _dot(sel_bf16, lo)


def _gdn_call(qkvg, ba, conv_w, gate_params, ng, n, nc):
    b, l, _ = qkvg.shape
    pairs = GDN_HEADS // 2
    n_chunks = l // GDN_CHUNK
    kern = functools.partial(_gdn_kernel, n=n, nc=nc, chunks_per_iter=GDN_CHUNKS_PER_GROUP)
    return pl.pallas_call(
        kern,
        grid=(b, pairs),
        in_specs=[
            pl.BlockSpec((1, l, 4 * LANES), lambda i, p: (i, 0, p)),
            pl.BlockSpec((1, l, LANES), lambda i, p: (i, 0, 0)),
            pl.BlockSpec((3, 4 * LANES), lambda i, p: (0, p)),
            pl.BlockSpec((2, LANES), lambda i, p: (0, 0)),
            pl.BlockSpec((1, LANES), lambda i, p: (0, 0)),
        ],
        out_specs=pl.BlockSpec((1, l, LANES), lambda i, p: (i, 0, p)),
        out_shape=jax.ShapeDtypeStruct((b, l, pairs * LANES), MIXER_OUT_DTYPE),
        scratch_shapes=[
            pltpu.VMEM((l + 5 * SUBLANES, 3 * LANES), F32),
            pltpu.VMEM((l, LANES), F32),
            pltpu.VMEM((l, LANES), F32),
            pltpu.VMEM((l, LANES), F32),
            pltpu.VMEM((2, l, LANES), F32),
            pltpu.VMEM((2, l, LANES), F32),
            pltpu.VMEM((2, l, LANES), BF16),
            pltpu.VMEM((2, 2 * l, LANES), BF16),
            pltpu.VMEM((2, l, LANES), F32),
            pltpu.VMEM((2, 2 * l, LANES), F32),
            pltpu.VMEM((2, n_chunks * SUBLANES, LANES), F32),
            pltpu.VMEM((2, l, LANES), F32),
        ],
        compiler_params=_cparams(("arbitrary", "arbitrary")),
        name="gated_deltanet",
    )(qkvg, ba, conv_w, gate_params, ng)


def _diff_kernel(*refs, key_start, n_sub, lam_init, aliased):
    if aliased:
        refs = refs[1:]
    q_ref, k_ref, vt_ref, lam_ref, ng_ref, o_ref = refs
    lp = lam_ref[...]
    lam = (jnp.exp(jnp.sum(lp[0:1] * lp[1:2], axis=-1, keepdims=True))
           - jnp.exp(jnp.sum(lp[2:3] * lp[3:4], axis=-1, keepdims=True)) + lam_init)
    lane = lax.broadcasted_iota(jnp.int32, (1, LANES), 1)
    halves = (lane < DIFF_DIM, lane >= DIFF_DIM)
    ng = ng_ref[...]
    k = k_ref[0, key_start:, :]
    vt = vt_ref[0, :, key_start:]
    tq = q_ref.shape[1] // n_sub

    def scores_of(i):
        q = q_ref[0, i * tq:(i + 1) * tq, :]
        return [_dot_nt(k, jnp.where(m, q, jnp.zeros_like(q))) for m in halves]

    ahead = scores_of(0)
    for i in range(n_sub):
        s = ahead
        if i + 1 < n_sub:
            ahead = scores_of(i + 1)
        e = [jnp.exp(x - jnp.max(x, axis=0, keepdims=True)) for x in s]
        pv = [_dot(vt, x.astype(BF16)) for x in e]
        parts = [x * (1.0 / jnp.sum(y, axis=0, keepdims=True)) for x, y in zip(pv, e)]
        ot = parts[0] - lam * parts[1]
        ot = ot * lax.rsqrt(jnp.mean(ot * ot, axis=0, keepdims=True) + EPS)
        o_ref[0, i * tq:(i + 1) * tq, :] = (ot.T * ng * (1.0 - lam_init)).astype(o_ref.dtype)


def _diff_call(dq, dk, dvt, lam_p, ng, lam_init, q_rows, first_block, n_q_blocks, key_start, n_sub, prev_out):
    b, l, _ = dq.shape
    aliased = prev_out is not None
    kern = functools.partial(_diff_kernel, key_start=key_start, n_sub=n_sub, lam_init=lam_init, aliased=aliased)
    row_of = lambda t: first_block + t
    in_specs = [
        pl.BlockSpec((1, q_rows, LANES), lambda i, h, t: (i, row_of(t), h)),
        pl.BlockSpec((1, l, LANES), lambda i, h, t: (i, 0, h)),
        pl.BlockSpec((1, LANES, l), lambda i, h, t: (i, h, 0)),
        pl.BlockSpec((4, DIFF_DIM), lambda i, h, t: (0, 0)),
        pl.BlockSpec((1, LANES), lambda i, h, t: (0, 0)),
    ]
    args = [dq, dk, dvt, lam_p, ng.reshape(1, LANES)]
    aliases = {}
    if aliased:
        in_specs = [pl.BlockSpec(memory_space=pl.ANY)] + in_specs
        args = [prev_out] + args
        aliases = {0: 0}
    return pl.pallas_call(
        kern,
        grid=(b, DIFF_HEADS, n_q_blocks),
        in_specs=in_specs,
        out_specs=pl.BlockSpec((1, q_rows, LANES), lambda i, h, t: (i, row_of(t), h)),
        out_shape=jax.ShapeDtypeStruct((b, l, DIFF_HEADS * LANES), MIXER_OUT_DTYPE),
        input_output_aliases=aliases,
        compiler_params=_cparams(("arbitrary", "arbitrary", "arbitrary")),
        name="diff_attention_ctx" if aliased else "diff_attention",
    )(*args)


def _swa_kernel(q_ref, k_ref, v_ref, sink_ref, o_ref, *, n, nc):
    t = pl.program_id(1)
    blk = SWA_BLOCK
    n_x = n // blk
    q = q_ref[0]
    lane = lax.broadcasted_iota(jnp.int32, (1, LANES), 1)
    lane_lo = lane < HEAD_DIM
    sink = sink_ref[...]
    group = SWA_HEADS // SWA_KV_HEADS

    def run(keys, vals, valid):
        head_of_row = lax.broadcasted_iota(jnp.int32, (group * blk, 1), 0) >> _log2(blk)
        kvs = range(SWA_KV_HEADS)
        kk = [keys[:, kvh * LANES:(kvh + 1) * LANES] for kvh in kvs]
        vv = [vals[:, kvh * LANES:(kvh + 1) * LANES] for kvh in kvs]
        qst, sk = [], []
        for kvh in kvs:
            q_rows = []
            sk_rows = jnp.zeros((group * blk, 1), F32)
            for g in range(group):
                h = kvh * group + g
                qp = q[:, (h // 2) * LANES:(h // 2 + 1) * LANES]
                q_rows.append(jnp.where(lane_lo if h % 2 == 0 else jnp.logical_not(lane_lo), qp, jnp.zeros_like(qp)))
                sk_rows = jnp.where(head_of_row == g, sink[:, h:h + 1], sk_rows)
            qst.append(jnp.concatenate(q_rows, axis=0))
            sk.append(sk_rows)
        s = [_dot_nt(x, y) for x, y in zip(qst, kk)]
        if valid is not None:
            s = [jnp.where(valid, x, NEG_INF) for x in s]
        mx = [jnp.maximum(jnp.max(x, axis=-1, keepdims=True), y) for x, y in zip(s, sk)]
        e = [jnp.exp(x - m) for x, m in zip(s, mx)]
        pv = [_dot(x.astype(BF16), y) for x, y in zip(e, vv)]
        den = [jnp.sum(x, axis=-1, keepdims=True) + jnp.exp(y - m) for x, y, m in zip(e, sk, mx)]
        outs = []
        for o, dn in zip(pv, den):
            o = o * (1.0 / dn)
            for g in range(0, group, 2):
                outs.append(jnp.where(lane_lo, o[g * blk:(g + 1) * blk], o[(g + 1) * blk:(g + 2) * blk]))
        o_ref[0] = jnp.concatenate(outs, axis=1).astype(o_ref.dtype)

    @pl.when(t < n_x)
    def _():
        start = pl.multiple_of(jnp.clip((t - 1) * blk, 0, n - 3 * blk), blk)
        keys = jnp.concatenate([k_ref[0, pl.ds(start, 3 * blk), :], k_ref[0, n:n + nc, :]], axis=0)
        vals = jnp.concatenate([v_ref[0, pl.ds(start, 3 * blk), :], v_ref[0, n:n + nc, :]], axis=0)
        shape = (group * blk, 3 * blk + nc)
        qpos = t * blk + (lax.broadcasted_iota(jnp.int32, shape, 0) & (blk - 1))
        col = lax.broadcasted_iota(jnp.int32, shape, 1)
        dist = qpos - (start + col)
        in_window = jnp.logical_and(dist <= SWA_WINDOW, dist >= -SWA_WINDOW)
        valid = jnp.logical_or(col >= 3 * blk, in_window)
        run(keys, vals, valid)

    @pl.when(t >= n_x)
    def _():
        run(k_ref[0, n:n + nc, :], v_ref[0, n:n + nc, :], None)


def _swa_call(q, k, v, sink, n, nc, with_ctx):
    b, l, _ = q.shape
    blk = SWA_BLOCK
    nt = (l if with_ctx else n) // blk
    kern = functools.partial(_swa_kernel, n=n, nc=nc)
    return pl.pallas_call(
        kern,
        grid=(b, nt),
        in_specs=[
            pl.BlockSpec((1, blk, SWA_HEADS * HEAD_DIM), lambda i, t: (i, t, 0)),
            pl.BlockSpec((1, l, 2 * LANES), lambda i, t: (i, 0, 0)),
            pl.BlockSpec((1, l, 2 * LANES), lambda i, t: (i, 0, 0)),
            pl.BlockSpec((1, LANES), lambda i, t: (0, 0)),
        ],
        out_specs=pl.BlockSpec((1, blk, SWA_HEADS * HEAD_DIM), lambda i, t: (i, t, 0)),
        out_shape=jax.ShapeDtypeStruct((b, l, SWA_HEADS * HEAD_DIM), MIXER_OUT_DTYPE),
        compiler_params=_cparams(("arbitrary", "arbitrary")),
        name="window_attention",
    )(q, k, v, sink)


def _dft_tables(n):
    h = n // 2
    r = 1 << (_log2(h) // 2)
    j = jnp.arange(h, dtype=jnp.int32)

    def tables(m):
        thin = lambda k: ((k[:, None] * m[None, :]) % (2 * n)).astype(F32) * (math.pi / n)
        a = thin(r * jnp.arange(h // r, dtype=jnp.int32))[:, None, :]
        b = thin(jnp.arange(r, dtype=jnp.int32))[None, :, :]
        cos = jnp.cos(a) * jnp.cos(b) - jnp.sin(a) * jnp.sin(b)
        sin = jnp.sin(a) * jnp.cos(b) + jnp.cos(a) * jnp.sin(b)
        return cos.reshape(h, h).astype(BF16), (-sin).reshape(h, h).astype(BF16)

    ce, se = tables(2 * j)
    co, so = tables(2 * j + 1)
    return ce, se, co, so, co.T, so.T


def _hyena_filter_kernel(feat_ref, w1_ref, b1_ref, w2_ref, b2_ref, freq_ref, w3f_ref, w3b_ref, dl_ref,
                         ce_ref, se_ref, co_ref, so_ref, ka_ref, kb_ref, km_ref):
    h = feat_ref.shape[1]
    assert h % 2 == 0
    freq = freq_ref[...]
    dl = dl_ref[...]
    row = lax.broadcasted_iota(jnp.int32, (h, 1), 0)

    def taps(part, w3_ref):
        feat = feat_ref[part]
        x = jnp.sin(freq[0:1] * (_dot_f32(feat, w1_ref[...]) + b1_ref[...]))
        x = jnp.sin(freq[1:2] * (_dot_f32(x, w2_ref[...]) + b2_ref[...]))
        return _dot_f32(x, w3_ref[...]) * jnp.exp(-feat[:, 0:1] * dl)

    fe, fo = taps(0, w3f_ref), taps(1, w3f_ref)
    be, bo = jnp.where(row == 0, 0.0, taps(2, w3b_ref)), taps(3, w3b_ref)
    ss = sum(jnp.sum(x * x, axis=0, keepdims=True) for x in (fe, fo, be, bo))
    sc = lax.rsqrt(ss + EPS)
    fe, fo, be, bo = (x * sc for x in (fe, fo, be, bo))
    sgn = jnp.where((row & 1) == 0, 1.0, -1.0)

    def dot2(t_ref, x):
        hi, lo = _split2(x)
        return _dot(t_ref[...], hi) + _dot(t_ref[...], lo)

    def bins(x_even, x_odd):
        ce, co = dot2(ce_ref, x_even), dot2(co_ref, x_odd)
        se, so = dot2(se_ref, x_even), dot2(so_ref, x_odd)
        return ce + co, se + so, ce - co, so - se

    f = bins(fe, fo)
    g = bins(be, bo)
    ka_ref[0, 0], ka_ref[0, 1], kb_ref[0, 0], kb_ref[0, 1] = (x + sgn * y for x, y in zip(f, g))
    mid_r = jnp.sum((fe + be) * sgn, axis=0, keepdims=True)
    mid_i = -jnp.sum((fo + bo) * sgn, axis=0, keepdims=True)
    km_ref[0] = jnp.concatenate([mid_r, mid_i, jnp.zeros((SUBLANES - 2, mid_r.shape[-1]), F32)], axis=0)


def _hyena_filter_call(feats, w1, b1, w2, b2, freq, w3, deltas, tables):
    h = feats.shape[1]
    hid = w2.shape[0]
    ch = deltas.shape[-1]
    tc = MXU_WIDTH
    nct = ch // tc
    const = lambda shape: pl.BlockSpec(shape, lambda o, j: (0,) * len(shape))
    spectrum = pl.BlockSpec((1, 2, h, tc), lambda o, j: (o, 0, 0, j))
    return pl.pallas_call(
        _hyena_filter_kernel,
        grid=(2, nct),
        in_specs=[
            const((4, h, hid)), const((hid, hid)), const((1, hid)), const((hid, hid)), const((1, hid)),
            const((2, hid)),
            pl.BlockSpec((hid, tc), lambda o, j: (0, (2 * o) * nct + j)),
            pl.BlockSpec((hid, tc), lambda o, j: (0, (2 * o + 1) * nct + j)),
            pl.BlockSpec((1, tc), lambda o, j: (0, j)),
        ] + [_resident((h, h))] * 4,
        out_specs=[spectrum, spectrum, pl.BlockSpec((1, SUBLANES, tc), lambda o, j: (o, 0, j))],
        out_shape=[
            jax.ShapeDtypeStruct((2, 2, h, ch), F32),
            jax.ShapeDtypeStruct((2, 2, h, ch), F32),
            jax.ShapeDtypeStruct((2, SUBLANES, ch), F32),
        ],
        compiler_params=_cparams(("arbitrary", "arbitrary")),
        name="hyena_filters",
    )(feats, w1, b1, w2, b2, freq, w3, w3, deltas, *tables[:4])


def _hyena_kernel(*refs, n, aliased):
    if aliased:
        refs = refs[1:]
    (v_ref, x1_ref, x2_ref, cwv_ref, cw1_ref, cw2_ref, ka_ref, kb_ref, km_ref, bias_ref,
     ce_ref, se_ref, co_ref, so_ref, cot_ref, sot_ref, o_ref, pad_ref, z_ref, zb_ref, p_ref, y_ref) = refs
    halo = SUBLANES
    tc = o_ref.shape[-1]
    h = n // 2
    rc = min(h, HY_ROW_CHUNK)
    lane_groups = tc // LANES
    zero_rows = jnp.zeros((halo, LANES), F32)
    for g in range(lane_groups):
        pad_ref[g, 0:halo, :] = zero_rows
        pad_ref[g, halo + n:2 * halo + n, :] = zero_rows

    def stage(ref):
        for r in range(0, n, 2 * rc):
            for g in range(lane_groups):
                pad_ref[g, halo + r:halo + r + 2 * rc, :] = ref[0, r:r + 2 * rc, g * LANES:(g + 1) * LANES]

    def conv_rows(cw, parity, r):
        first = halo + 2 * r + parity - 1
        taps = [jnp.concatenate([pad_ref[g, pl.ds(first + i, rc, stride=2), :] for g in range(lane_groups)], axis=1)
                for i in range(3)]
        return cw[0:1] * taps[0] + cw[1:2] * taps[1] + cw[2:3] * taps[2]

    def alt_sign(r):
        j = r + lax.broadcasted_iota(jnp.int32, (rc, 1), 0)
        return j, jnp.where((j & 1) == 0, 1.0, -1.0)

    stage(v_ref)
    cw = cwv_ref[...]
    for parity in range(2):
        for r in range(0, h, rc):
            z = conv_rows(cw, parity, r)
            z_ref[parity, r:r + rc, :] = z
            zb_ref[parity, r:r + rc, :] = z.astype(BF16)

    for o, (gate_ref, gate_cw_ref) in enumerate(((x1_ref, cw1_ref), (x2_ref, cw2_ref))):
        mid_r = jnp.zeros((1, tc), F32)
        mid_i = jnp.zeros((1, tc), F32)
        for r in range(0, h, rc):
            _, sgn = alt_sign(r)
            mid_r = mid_r + jnp.sum(z_ref[0, r:r + rc, :] * sgn, axis=0, keepdims=True)
            mid_i = mid_i - jnp.sum(z_ref[1, r:r + rc, :] * sgn, axis=0, keepdims=True)
        km_r = km_ref[o, 0:1, :]
        km_i = km_ref[o, 1:2, :]
        pm_r = (mid_r * km_r - mid_i * km_i) * (1.0 / n)
        pm_i = (mid_r * km_i + mid_i * km_r) * (1.0 / n)
        ze = zb_ref[0]
        zo = zb_ref[1]
        for r in range(0, h, rc):
            k, _ = alt_sign(r)
            rows = slice(r, r + rc)
            ce, co = _dot(ce_ref[rows, :], ze), _dot(co_ref[rows, :], zo)
            se, so = _dot(se_ref[rows, :], ze), _dot(so_ref[rows, :], zo)
            xa_r, xb_r, xa_i, xb_i = ce + co, ce - co, se + so, so - se
            wgt = jnp.where(k == 0, 0.5 / n, 1.0 / n)
            ka_r, ka_i = ka_ref[o, 0, rows, :], ka_ref[o, 1, rows, :]
            kb_r, kb_i = kb_ref[o, 0, rows, :], kb_ref[o, 1, rows, :]
            pa_r = (xa_r * ka_r - xa_i * ka_i) * wgt
            pa_i = (xa_r * ka_i + xa_i * ka_r) * wgt
            pb_r = (xb_r * kb_r - xb_i * kb_i) * wgt
            pb_i = (xb_r * kb_i + xb_i * kb_r) * wgt
            p_ref[0, rows, :] = (pa_r + pb_r).astype(BF16)
            p_ref[1, rows, :] = (pa_i - pb_i).astype(BF16)
            p_ref[2, rows, :] = (pa_r - pb_r).astype(BF16)
            p_ref[3, rows, :] = (pa_i + pb_i).astype(BF16)
        stage(gate_ref)
        cw = gate_cw_ref[...]
        bias = bias_ref[o:o + 1, :]
        for parity, (c_ref, s_ref, mid) in enumerate(((ce_ref, se_ref, pm_r), (cot_ref, sot_ref, -pm_i))):
            for r in range(0, h, rc):
                _, sgn = alt_sign(r)
                rows = slice(r, r + rc)
                y = (_dot(c_ref[rows, :], p_ref[2 * parity]) + _dot(s_ref[rows, :], p_ref[2 * parity + 1])
                     + sgn * mid)
                z = conv_rows(cw, parity, r) * (y + z_ref[parity, rows, :] * bias)
                if o == 0:
                    z_ref[parity, rows, :] = z
                    zb_ref[parity, rows, :] = z.astype(BF16)
                else:
                    for g in range(lane_groups):
                        y_ref[g, pl.ds(2 * r + parity, rc, stride=2), :] = z[:, g * LANES:(g + 1) * LANES]
    for r in range(0, n, 2 * rc):
        rows = slice(r, r + 2 * rc)
        o_ref[0, rows, :] = jnp.concatenate([y_ref[g, rows, :] for g in range(lane_groups)],
                                            axis=1).astype(o_ref.dtype)


def _hyena_call(u, conv_w, ka, kb, km, bias, tables, n, row_block, prev_out):
    b, l, _ = u.shape
    ch = bias.shape[-1]
    tc = MXU_WIDTH
    nct = ch // tc
    h = n // 2
    aliased = prev_out is not None
    kern = functools.partial(_hyena_kernel, n=n, aliased=aliased)
    once = pl.Buffered(1)
    in_specs = [
        pl.BlockSpec((1, n, tc), lambda j, i: (i, row_block, j)),
        pl.BlockSpec((1, n, tc), lambda j, i: (i, row_block, nct + j)),
        pl.BlockSpec((1, n, tc), lambda j, i: (i, row_block, 2 * nct + j)),
        pl.BlockSpec((3, tc), lambda j, i: (0, j)),
        pl.BlockSpec((3, tc), lambda j, i: (0, nct + j)),
        pl.BlockSpec((3, tc), lambda j, i: (0, 2 * nct + j)),
        pl.BlockSpec((2, 2, h, tc), lambda j, i: (0, 0, 0, j), pipeline_mode=once),
        pl.BlockSpec((2, 2, h, tc), lambda j, i: (0, 0, 0, j), pipeline_mode=once),
        pl.BlockSpec((2, SUBLANES, tc), lambda j, i: (0, 0, j)),
        pl.BlockSpec((2, tc), lambda j, i: (0, j)),
    ] + [_resident((h, h))] * 6
    args = [u, u, u, conv_w, conv_w, conv_w, ka, kb, km, bias, *tables]
    aliases = {}
    if aliased:
        in_specs = [pl.BlockSpec(memory_space=pl.ANY)] + in_specs
        args = [prev_out] + args
        aliases = {0: 0}
    return pl.pallas_call(
        kern,
        grid=(nct, b),
        in_specs=in_specs,
        out_specs=pl.BlockSpec((1, n, tc), lambda j, i: (i, row_block, j)),
        out_shape=jax.ShapeDtypeStruct((b, l, ch), MIXER_OUT_DTYPE),
        scratch_shapes=[
            pltpu.VMEM((tc // LANES, n + 2 * SUBLANES, LANES), F32),
            pltpu.VMEM((2, h, tc), F32),
            pltpu.VMEM((2, h, tc), BF16),
            pltpu.VMEM((4, h, tc), BF16),
            pltpu.VMEM((tc // LANES, n, LANES), F32),
        ],
        input_output_aliases=aliases,
        compiler_params=_cparams(("arbitrary", "arbitrary")),
        name="hyena_conv_n%d" % n,
    )(*args)


def _rope_tables(n, nc):
    rows = n // GRID_W
    row = jnp.repeat(jnp.arange(rows, dtype=F32), GRID_W)
    col = jnp.tile(jnp.arange(GRID_W, dtype=F32), rows)
    half = HEAD_DIM // 2
    inv = ROPE_BASE ** (-jnp.arange(0, half, 2, dtype=F32) / half)
    ar = row[:, None] * inv
    ac = col[:, None] * inv
    cos = jnp.concatenate([jnp.cos(ar), jnp.cos(ar), jnp.cos(ac), jnp.cos(ac)], axis=-1)
    sin = jnp.concatenate([-jnp.sin(ar), jnp.sin(ar), -jnp.sin(ac), jnp.sin(ac)], axis=-1)
    cos = jnp.concatenate([cos, jnp.ones((nc, HEAD_DIM), F32)], axis=0)
    sin = jnp.concatenate([sin, jnp.zeros((nc, HEAD_DIM), F32)], axis=0)
    return jnp.tile(cos, (1, LANES // HEAD_DIM)), jnp.tile(sin, (1, LANES // HEAD_DIM))


def _rope_partner_cols(width):
    d = np.arange(width)
    quarter = HEAD_DIM // 4
    return np.where((d % (2 * quarter)) < quarter, d + quarter, d - quarter)


def _hyena_feats(n):
    pos = jnp.arange(n, dtype=F32)
    t = pos / max(n - 1, 1)
    ang = (2.0 * math.pi * pos / n)[:, None] * jnp.linspace(1e-4, HY_BANDS - 1, HY_BANDS, dtype=F32)[None, :]
    feats = jnp.concatenate([t[:, None], jnp.cos(ang), -jnp.sin(ang)], axis=-1)
    feats = jnp.pad(feats, ((0, 0), (0, 64 - feats.shape[-1])))
    back = jnp.concatenate([feats[0:1], jnp.flip(feats[1:], axis=0)], axis=0)
    return jnp.stack([feats[0::2], feats[1::2], back[0::2], back[1::2]])


def _pad_cols(w, width):
    return jnp.pad(w, ((0, 0), (0, width - w.shape[-1])))


def _layer_ab(xz, mod, norm_g0, w_in, conv_w, a_log, dt_bias, gdn_g, lam_p, diff_g, lam_init, rope, n, nc):
    hd = GDN_HEADS * GDN_DIM
    wq, wk, wv, wg = (w_in[:, i * hd:(i + 1) * hd] for i in range(4))
    o = 4 * hd
    w_beta, w_alpha = w_in[:, o:o + 16], w_in[:, o + 16:o + 32]
    o += 32
    dd = DIFF_HEADS * 2 * DIFF_DIM
    wdq, wdk, wdv = (w_in[:, o + i * dd:o + (i + 1) * dd] for i in range(3))
    pairs = GDN_HEADS // 2
    pair_cols = lambda w: [w[:, p * LANES:(p + 1) * LANES] for p in range(pairs)]
    w_qkvg = jnp.concatenate([blk for grp in zip(pair_cols(wq), pair_cols(wk), pair_cols(wv), pair_cols(wg))
                              for blk in grp], axis=1)
    perm = _rope_partner_cols(dd)
    w_all = jnp.concatenate([w_qkvg, _pad_cols(jnp.concatenate([w_beta, w_alpha], axis=1), LANES),
                             wdq, wdk, wdv, wdq[:, perm], wdk[:, perm]], axis=1).astype(BF16)
    c0 = 4 * hd
    c1 = c0 + LANES
    segs = (_Seg(0, c0), _Seg(c0, LANES),
            _Seg(c1, dd, rot_start=c1 + 3 * dd, scale=DIFF_DIM ** -0.5, dtype=BF16),
            _Seg(c1 + dd, dd, rot_start=c1 + 4 * dd, dtype=BF16),
            _Seg(c1 + 2 * dd, dd, dtype=BF16, transposed=True))
    qkvg, ba, dq, dk, dvt = _proj_call(xz, mod, norm_g0, w_all, rope[0], rope[1], segs, n // ROW_TILE, "proj_ab")

    cq, ck, cv = (conv_w[:, i * hd:(i + 1) * hd] for i in range(3))
    zeros = jnp.zeros((3, LANES), F32)
    conv_l = jnp.concatenate([blk for p in range(pairs) for blk in
                              (cq[:, p * LANES:(p + 1) * LANES], ck[:, p * LANES:(p + 1) * LANES],
                               cv[:, p * LANES:(p + 1) * LANES], zeros)], axis=1)
    n_gate = 2 * GDN_HEADS
    on_decay_lanes = lambda t: jnp.pad(t.reshape(1, n_gate), ((0, 0), (n_gate, LANES - 2 * n_gate)))
    gate_params = jnp.concatenate([on_decay_lanes(a_log), on_decay_lanes(dt_bias)], axis=0)
    ng = jnp.tile(gdn_g.reshape(1, GDN_DIM), (1, 2))
    oa = _gdn_call(qkvg, ba, conv_l, gate_params, ng, n, nc)
    q_rows = DIFF_SUB_TILES * ROW_TILE
    ob = _diff_call(dq, dk, dvt, lam_p, diff_g, lam_init, q_rows, 0, n // q_rows, 0, DIFF_SUB_TILES, None)
    ob = _diff_call(dq, dk, dvt, lam_p, diff_g, lam_init, nc, n // nc, 1, n, 1, ob)
    return oa, ob


def _layer_cd(xz, mod, norm_g0, w_in, sink, hy_conv, hy_w1, hy_b1, hy_w2, hy_b2, hy_w3, hy_freq, hy_bias,
              rope, n, nc, last, dft_x, dft_c):
    qd = SWA_HEADS * HEAD_DIM
    kd = SWA_KV_HEADS * HEAD_DIM
    wq, wk, wv, wu = w_in[:, 0:qd], w_in[:, qd:qd + kd], w_in[:, qd + kd:qd + 2 * kd], w_in[:, qd + 2 * kd:]
    dup = lambda w: jnp.concatenate([w[:, 0:HEAD_DIM], w[:, 0:HEAD_DIM], w[:, HEAD_DIM:], w[:, HEAD_DIM:]], axis=1)
    wk2, wv2 = dup(wk), dup(wv)
    ud = wu.shape[1]
    w_all = jnp.concatenate([wq, wk2, wv2, wu, wq[:, _rope_partner_cols(qd)], wk2[:, _rope_partner_cols(2 * kd)]],
                            axis=1).astype(BF16)
    o_u = qd + 4 * kd
    segs = (_Seg(0, qd, rot_start=o_u + ud, scale=HEAD_DIM ** -0.5, dtype=BF16),
            _Seg(qd, 2 * kd, rot_start=o_u + ud + qd, dtype=BF16),
            _Seg(qd + 2 * kd, 2 * kd, dtype=BF16), _Seg(o_u, ud))
    q, k, v, u = _proj_call(xz, mod, norm_g0, w_all, rope[0], rope[1], segs, n // ROW_TILE, "proj_cd")
    oc = _swa_call(q, k, v, _pad_cols(sink.reshape(1, SWA_HEADS), LANES), n, nc, not last)

    ch = hy_bias.shape[-1]
    deltas = jnp.abs(jnp.linspace(HY_MIN_DECAY, HY_MAX_DECAY, ch, dtype=F32)).reshape(1, ch)
    hid = hy_w2.shape[0]
    w1p = jnp.pad(hy_w1, ((0, hid - hy_w1.shape[0]), (0, 0)))
    filt = lambda m, dft: _hyena_filter_call(_hyena_feats(m), w1p, hy_b1.reshape(1, hid), hy_w2,
                                             hy_b2.reshape(1, hid), hy_freq, hy_w3, deltas, dft)
    od = _hyena_call(u, hy_conv, *filt(n, dft_x), hy_bias, dft_x, n, 0, None)
    if not last:
        od = _hyena_call(u, hy_conv, *filt(nc, dft_c), hy_bias, dft_c, nc, n // nc, od)
    return oc, od


def kernel(x, c, ctx, c_ctx, w_mod, b_mod, norm_g, ffn_w_up, ffn_conv, ffn_w_down, ab_w_in, ab_w_out, gdn_conv, gdn_a_log, gdn_dt_bias, gdn_norm_g, diff_lambda, diff_norm_g, cd_w_in, cd_w_out, swa_sink, hy_conv, hy_w1, hy_b1, hy_w2, hy_b2, hy_w3, hy_freq, hy_bias):
    b, n, d = x.shape
    nc = ctx.shape[1]
    depth = w_mod.shape[0]
    assert n % ROW_TILE == 0 and nc == ROW_TILE and n % GRID_W == 0
    xz = jnp.concatenate([x, ctx], axis=1)
    rows = -(-(b + 1) // SUBLANES) * SUBLANES
    cc = jnp.concatenate([c, c_ctx[None], jnp.zeros((rows - b - 1, d), F32)], axis=0)
    mods = _mod_call(cc, w_mod, b_mod)
    mod_all = jnp.concatenate([mods[:, :b].reshape(depth, b, 1, 6, d),
                               jnp.broadcast_to(mods[:, b].reshape(depth, 1, 1, 6, d), (depth, b, 1, 6, d))], axis=2)
    rope = _rope_tables(n, nc)
    dft_x = _dft_tables(n)
    dft_c = _dft_tables(nc)
    n_x_tiles = n // ROW_TILE
    for l in range(depth):
        last = l == depth - 1
        i = l // 2
        mod = mod_all[l]
        if l % 2 == 0:
            lam_init = 0.8 - 0.6 * math.exp(-0.3 * l)
            o1, o2 = _layer_ab(xz, mod, norm_g[l, 0], ab_w_in[i], gdn_conv[i], gdn_a_log[i], gdn_dt_bias[i],
                               gdn_norm_g[i], diff_lambda[i], diff_norm_g[i], lam_init, rope, n, nc)
            w_out = ab_w_out[i]
        else:
            o1, o2 = _layer_cd(xz, mod, norm_g[l, 0], cd_w_in[i], swa_sink[i], hy_conv[i], hy_w1[i], hy_b1[i],
                               hy_w2[i], hy_b2[i], hy_w3[i], hy_freq[i], hy_bias[i], rope, n, nc, last, dft_x, dft_c)
            w_out = cd_w_out[i]
        n_tiles = (n if last else n + nc) // ROW_TILE
        xz = _post_call(o1, o2, xz, mod, norm_g[l, 1], norm_g[l, 2], norm_g[l, 3], w_out.astype(BF16),
                        ffn_w_up[l].astype(BF16), ffn_conv[l], ffn_w_down[l].astype(BF16), n_tiles, n_x_tiles)
    return xz
```

```python
import functools
import math
from typing import NamedTuple, Optional

import jax
import jax.numpy as jnp
import numpy as np
from jax import lax
from jax.experimental import pallas as pl
from jax.experimental.pallas import tpu as pltpu

F32 = jnp.float32
BF16 = jnp.bfloat16
MIXER_OUT_DTYPE = BF16

EPS = 1e-6
NEG_INF = -1e30
GRID_W = 64
HEAD_DIM = 64
ROPE_BASE = 10000.0
GDN_HEADS = 8
GDN_DIM = 64
GDN_CHUNK = 64
GDN_CHUNKS_PER_GROUP = 9
DIFF_HEADS = 4
DIFF_DIM = 64
DIFF_SUB_TILES = 8
SWA_HEADS = 8
SWA_KV_HEADS = 2
SWA_WINDOW = 128
SWA_BLOCK = 128
HY_BANDS = 16
HY_MIN_DECAY = math.log(1e-2) / 1.5
HY_MAX_DECAY = math.log(1e-2) / 0.3
HY_ROW_CHUNK = 512

LANES = 128
SUBLANES = 8
MXU_WIDTH = 256
FFN_COL_CHUNK = 6 * MXU_WIDTH
ROW_TILE = 256
VMEM_LIMIT = 56 * 1024 * 1024


def _cparams(sem):
    return pltpu.CompilerParams(dimension_semantics=sem, vmem_limit_bytes=VMEM_LIMIT)


def _resident(shape):
    zeros = (0,) * len(shape)
    return pl.BlockSpec(shape, lambda *_: zeros, pipeline_mode=pl.Buffered(1))


def _log2(v):
    assert v & (v - 1) == 0
    return v.bit_length() - 1


def _sigmoid(x):
    return 1.0 / (1.0 + jnp.exp(-x))


def _silu(x):
    return x * _sigmoid(x)


def _softplus(x):
    return jnp.maximum(x, 0.0) + jnp.log1p(jnp.exp(-jnp.abs(x)))


def _dot(a, b):
    return jnp.dot(a, b, preferred_element_type=F32)


def _dot_nt(a, b):
    return lax.dot_general(a, b, (((1,), (1,)), ((), ())), preferred_element_type=F32)


def _dot_tn(a, b):
    return lax.dot_general(a, b, (((0,), (0,)), ((), ())), preferred_element_type=F32)


def _dot_f32(a, b):
    return jnp.dot(a, b, preferred_element_type=F32, precision=lax.Precision.HIGHEST)


def _split2(x):
    hi = x.astype(BF16)
    lo = (x - hi.astype(F32)).astype(BF16)
    return hi, lo


def _dot_sel(x, sel_bf16):
    hi, lo = _split2(x)
    return _dot(hi, sel_bf16) + _dot(lo, sel_bf16)


def _mm(a, b):
    return _dot(a.astype(BF16), b.astype(BF16))


def _rms(y, g):
    return y * lax.rsqrt(jnp.mean(y * y, axis=-1, keepdims=True) + EPS) * g


def _modnorm(x, g, shift, scale):
    return _rms(x, g) * (1.0 + scale) + shift


def _mod_kernel(cc_ref, w_ref, b_ref, o_ref):
    s = _silu(cc_ref[...])
    o_ref[0] = _dot(s.astype(BF16), w_ref[0].astype(BF16)) + b_ref[0]


def _mod_call(cc, w_mod, b_mod):
    depth, d, nm = w_mod.shape
    rows = cc.shape[0]
    ct = 1536
    return pl.pallas_call(
        _mod_kernel,
        grid=(depth, nm // ct),
        in_specs=[
            pl.BlockSpec((rows, d), lambda l, j: (0, 0)),
            pl.BlockSpec((1, d, ct), lambda l, j: (l, 0, j)),
            pl.BlockSpec((1, 1, ct), lambda l, j: (l, 0, j)),
        ],
        out_specs=pl.BlockSpec((1, rows, ct), lambda l, j: (l, 0, j)),
        out_shape=jax.ShapeDtypeStruct((depth, rows, nm), F32),
        compiler_params=_cparams(("arbitrary", "arbitrary")),
        name="adaln_mod",
    )(cc, w_mod, b_mod.reshape(depth, 1, nm))


class _Seg(NamedTuple):
    start: int
    width: int
    rot_start: Optional[int] = None
    scale: float = 1.0
    dtype: type = F32
    transposed: bool = False


def _proj_kernel(x_ref, mod_ref, g_ref, w_ref, cos_ref, sin_ref, *out_refs, segs):
    m = mod_ref[0, 0]
    h = _modnorm(x_ref[0], g_ref[...], m[0:1], m[1:2]).astype(BF16)
    for o_ref, seg in zip(out_refs, segs):
        y = _dot(h, w_ref[:, seg.start:seg.start + seg.width])
        if seg.rot_start is not None:
            yr = _dot(h, w_ref[:, seg.rot_start:seg.rot_start + seg.width])
            reps = seg.width // LANES
            cos = jnp.concatenate([cos_ref[...]] * reps, axis=1)
            sin = jnp.concatenate([sin_ref[...]] * reps, axis=1)
            y = y * cos + yr * sin
        if seg.scale != 1.0:
            y = y * seg.scale
        if seg.transposed:
            y = y.T
        o_ref[0] = y.astype(seg.dtype)


def _proj_call(xz, mod, g, w, cos_t, sin_t, segs, n_x_tiles, name):
    b, l, d = xz.shape
    tm = ROW_TILE
    nt = l // tm
    p = w.shape[1]
    return pl.pallas_call(
        functools.partial(_proj_kernel, segs=segs),
        grid=(nt, b),
        in_specs=[
            pl.BlockSpec((1, tm, d), lambda t, i: (i, t, 0)),
            pl.BlockSpec((1, 1, 6, d), lambda t, i: (i, t // n_x_tiles, 0, 0)),
            pl.BlockSpec((1, d), lambda t, i: (0, 0)),
            _resident((d, p)),
            pl.BlockSpec((tm, LANES), lambda t, i: (t, 0)),
            pl.BlockSpec((tm, LANES), lambda t, i: (t, 0)),
        ],
        out_specs=[pl.BlockSpec((1, s.width, tm), lambda t, i: (i, 0, t)) if s.transposed
                   else pl.BlockSpec((1, tm, s.width), lambda t, i: (i, t, 0)) for s in segs],
        out_shape=[jax.ShapeDtypeStruct((b, s.width, l) if s.transposed else (b, l, s.width), s.dtype)
                   for s in segs],
        compiler_params=_cparams(("arbitrary", "arbitrary")),
        name=name,
    )(xz, mod, g.reshape(1, d), w, cos_t, sin_t)


def _post_kernel(o1p_ref, o1_ref, o1n_ref, o2p_ref, o2_ref, o2n_ref, xp_ref, x_ref, xn_ref, mod_ref,
                 g1_ref, g2_ref, g3_ref, wout_ref, wup_ref, cw_ref, wdn_ref, out_ref, up_ref,
                 *, tm, n_x_tiles, n_tiles, cf, dff):
    t = pl.program_id(0)
    first = jnp.logical_or(t == 0, t == n_x_tiles)
    last = jnp.logical_or(t == n_x_tiles - 1, t == n_tiles - 1)
    m = mod_ref[0, 0]
    halo = SUBLANES
    ohalo = o1p_ref.shape[1]
    k1 = o1_ref.shape[-1]
    o1e = jnp.concatenate([o1p_ref[0], o1_ref[0], o1n_ref[0]], axis=0)
    o2e = jnp.concatenate([o2p_ref[0], o2_ref[0], o2n_ref[0]], axis=0)
    y = _dot(o1e, wout_ref[0:k1, :]) + _dot(o2e, wout_ref[k1:, :])
    y = y[ohalo - halo:ohalo + tm + halo]
    xe = jnp.concatenate([xp_ref[0], x_ref[0], xn_ref[0]], axis=0)
    x1 = xe + m[2:3] * _rms(y, g1_ref[...])
    h = _modnorm(x1, g2_ref[...], m[3:4], m[4:5]).astype(BF16)
    acc = jnp.zeros((tm, x_ref.shape[-1]), F32)
    for c0 in range(0, dff, cf):
        wd = min(cf, dff - c0)
        halves = []
        for half, base in enumerate((c0, dff + c0)):
            u = _dot(h, wup_ref[:, base:base + wd])
            up_ref[half, :, 0:wd] = u
            up_ref[half, 0:halo, 0:wd] = jnp.where(first, 0.0, u[0:halo])
            up_ref[half, tm + halo:tm + 2 * halo, 0:wd] = jnp.where(last, 0.0, u[tm + halo:])
            cw = cw_ref[:, base:base + wd]
            halves.append(cw[0:1] * up_ref[half, halo - 1:halo - 1 + tm, 0:wd]
                          + cw[1:2] * up_ref[half, halo:halo + tm, 0:wd]
                          + cw[2:3] * up_ref[half, halo + 1:halo + 1 + tm, 0:wd])
        act = (_silu(halves[1]) * halves[0]).astype(BF16)
        acc = acc + _dot(act, wdn_ref[c0:c0 + wd, :])
    out_ref[0] = x1[halo:halo + tm] + m[5:6] * _rms(acc, g3_ref[...])


def _post_call(o1, o2, xz, mod, g1, g2, g3, w_out, w_up, conv_w, w_down, n_tiles, n_x_tiles):
    b, _, d = xz.shape
    tm = ROW_TILE
    rows = n_tiles * tm
    dff = w_down.shape[0]
    cf = FFN_COL_CHUNK
    k1, k2 = o1.shape[-1], o2.shape[-1]
    ohalo = 2 * SUBLANES
    kern = functools.partial(_post_kernel, tm=tm, n_x_tiles=n_x_tiles, n_tiles=n_tiles, cf=cf, dff=dff)

    def with_halos(width, halo_rows):
        per_tile = tm // halo_rows
        n_blocks = rows // halo_rows
        return [
            pl.BlockSpec((1, halo_rows, width), lambda t, i: (i, jnp.maximum(t * per_tile - 1, 0), 0)),
            pl.BlockSpec((1, tm, width), lambda t, i: (i, t, 0)),
            pl.BlockSpec((1, halo_rows, width), lambda t, i: (i, jnp.minimum((t + 1) * per_tile, n_blocks - 1), 0)),
        ]

    row_vec = pl.BlockSpec((1, d), lambda t, i: (0, 0))
    return pl.pallas_call(
        kern,
        grid=(n_tiles, b),
        in_specs=with_halos(k1, ohalo) + with_halos(k2, ohalo) + with_halos(d, SUBLANES) + [
            pl.BlockSpec((1, 1, 6, d), lambda t, i: (i, t // n_x_tiles, 0, 0)),
            row_vec, row_vec, row_vec,
            _resident((k1 + k2, d)),
            _resident((d, 2 * dff)),
            pl.BlockSpec((3, 2 * dff), lambda t, i: (0, 0)),
            _resident((dff, d)),
        ],
        out_specs=pl.BlockSpec((1, tm, d), lambda t, i: (i, t, 0)),
        out_shape=jax.ShapeDtypeStruct((b, rows, d), F32),
        scratch_shapes=[pltpu.VMEM((2, tm + 2 * SUBLANES, cf), F32)],
        compiler_params=_cparams(("arbitrary", "arbitrary")),
        name="mixer_out_conv_ffn",
    )(o1, o1, o1, o2, o2, o2, xz, xz, xz, mod, g1.reshape(1, d), g2.reshape(1, d), g3.reshape(1, d),
      w_out, w_up, conv_w, w_down)


def _half_sums(x2, lane_lo):
    s0 = jnp.sum(jnp.where(lane_lo, x2, 0.0), axis=-1, keepdims=True)
    s1 = jnp.sum(jnp.where(lane_lo, 0.0, x2), axis=-1, keepdims=True)
    return jnp.where(lane_lo, s0, s1)


def _gdn_kernel(qkvg_ref, ba_ref, cw_ref, gp_ref, ng_ref, out_ref,
                pad_ref, q_ref, k_ref, v_ref, bb_ref, gb_ref, qe_ref, mp_ref, ou_ref, nn_ref, egl_ref, o_ref,
                *, n, nc, chunks_per_iter):
    l = n + nc
    c = GDN_CHUNK
    n_chunks = l // c
    pair = pl.program_id(1)
    halo = SUBLANES
    lane = lax.broadcasted_iota(jnp.int32, (1, LANES), 1)
    lane_lo = lane < GDN_DIM

    cw = cw_ref[:, 0:3 * LANES]
    zero_rows = jnp.zeros((halo, 3 * LANES), F32)
    for seq_start, seq_len in ((0, n), (n, nc)):
        base = halo + seq_start + (2 * halo if seq_start else 0)
        pad_ref[base - halo:base, :] = zero_rows
        pad_ref[base + seq_len:base + seq_len + halo, :] = zero_rows
        step = 256
        for r in range(0, seq_len, step):
            pad_ref[base + r:base + r + step, :] = qkvg_ref[0, seq_start + r:seq_start + r + step, 0:3 * LANES]
        for r in range(0, seq_len, step):
            y = (cw[0:1] * pad_ref[base + r - 1:base + r - 1 + step, :]
                 + cw[1:2] * pad_ref[base + r:base + r + step, :]
                 + cw[2:3] * pad_ref[base + r + 1:base + r + 1 + step, :])
            y = _silu(y)
            q = y[:, 0:LANES]
            k = y[:, LANES:2 * LANES]
            rows = slice(seq_start + r, seq_start + r + step)
            q_ref[rows, :] = q * lax.rsqrt(_half_sums(q * q, lane_lo) + EPS) * (GDN_DIM ** -0.5)
            k_ref[rows, :] = k * lax.rsqrt(_half_sums(k * k, lane_lo) + EPS)
            v_ref[rows, :] = y[:, 2 * LANES:3 * LANES]

    sel_r = lax.broadcasted_iota(jnp.int32, (LANES, 4 * LANES), 0)
    sel_c = lax.broadcasted_iota(jnp.int32, (LANES, 4 * LANES), 1)
    quarter = sel_c >> _log2(LANES)
    src_lane = (quarter & 1) * 2 * GDN_HEADS + (quarter >> 1) * GDN_HEADS + 2 * pair + ((sel_c >> _log2(GDN_DIM)) & 1)
    sel = (sel_r == src_lane).astype(BF16)
    gblk = 256
    bi = lax.broadcasted_iota(jnp.int32, (gblk, gblk), 0)
    bj = lax.broadcasted_iota(jnp.int32, (gblk, gblk), 1)
    same_chunk = (bi >> _log2(c)) == (bj >> _log2(c))
    csum = (jnp.logical_and(same_chunk, bi >= bj).astype(BF16), jnp.logical_and(same_chunk, bi <= bj).astype(BF16))
    neg_a = -jnp.exp(gp_ref[0:1, :])
    dt_bias = gp_ref[1:2, :]
    for r in range(0, l, gblk):
        ba = ba_ref[0, r:r + gblk, :]
        gates = jnp.where(lane < 2 * GDN_HEADS, _sigmoid(ba), neg_a * _softplus(ba + dt_bias))
        x = _dot_sel(gates, sel)
        for d in range(2):
            bb_ref[d, r:r + gblk, :] = x[:, 2 * d * LANES:(2 * d + 1) * LANES]
            gb_ref[d, r:r + gblk, :] = _dot_sel_lhs(csum[d], x[:, (2 * d + 1) * LANES:(2 * d + 2) * LANES])

    r2 = lax.broadcasted_iota(jnp.int32, (2 * c, 2 * c), 0)
    c2 = lax.broadcasted_iota(jnp.int32, (2 * c, 2 * c), 1)
    same_head = (r2 >= c) == (c2 >= c)
    eye = (r2 == c2).astype(F32)
    masks = ((jnp.logical_and(same_head, r2 >= c2), jnp.logical_and(same_head, r2 > c2)),
             (jnp.logical_and(same_head, r2 <= c2), jnp.logical_and(same_head, r2 < c2)))
    m0 = lane_lo.astype(F32)
    m1 = 1.0 - m0

    def pair_mask(lv, lower):
        same_block = (r2 >> (lv + 1)) == (c2 >> (lv + 1))
        r_hi = ((r2 >> lv) & 1) == 1
        c_hi = ((c2 >> lv) & 1) == 1
        off = jnp.logical_and(r_hi, jnp.logical_not(c_hi)) if lower else jnp.logical_and(c_hi, jnp.logical_not(r_hi))
        return jnp.logical_and(same_block, off)

    pair_masks = tuple(tuple(pair_mask(lv, lower) for lv in range(_log2(c))) for lower in (True, False))

    def stack_heads(x2):
        return jnp.concatenate([x2 * m0, x2 * m1], axis=0)

    def fold_heads(x):
        return x[0:c] + x[c:2 * c]

    def local_stages(dirs, qs, ks, vs, betas, gcs, out):
        each = lambda f, *cols: [f(*args) for args in zip(*cols)]
        incl = [masks[d][0] for d in dirs]
        strict = [masks[d][1] for d in dirs]
        g1 = each(lambda gc2: jnp.concatenate([gc2, gc2], axis=0), gcs)
        decay = each(lambda g, m: jnp.where(m, jnp.exp(jnp.where(m, g - g.T, 0.0)), 0.0), g1, incl)
        kb = each(lambda k, b: k * b, ks, betas)
        kst = each(lambda k: stack_heads(k).astype(BF16), ks)
        a_raw = each(lambda x, y: _dot_nt(stack_heads(x).astype(BF16), y), kb, kst)
        qk_raw = each(lambda x, y: _dot_nt(stack_heads(x).astype(BF16), y), qs, kst)
        yield
        qk = each(lambda m, x, dc: jnp.where(m, x * dc, 0.0).astype(BF16), incl, qk_raw, decay)
        a = each(lambda m, x, dc: jnp.where(m, x * dc, 0.0), strict, a_raw, decay)
        tinv = each(lambda d, x: eye - jnp.where(pair_masks[d][0], x, 0.0), dirs, a)
        for lv in range(1, _log2(c)):
            ta = each(lambda d, t, x: _mm(t, jnp.where(pair_masks[d][lv], x, 0.0)), dirs, tinv, a)
            yield
            tat = each(_mm, ta, tinv)
            yield
            tinv = each(lambda t, x: t - x, tinv, tat)
        egc = each(jnp.exp, gcs)
        rhs = each(lambda v, b, x, e: jnp.concatenate([stack_heads(v * b), stack_heads(x * e)], axis=1),
                   vs, betas, kb, egc)
        sol = each(_mm, tinv, rhs)
        yield
        u2 = each(lambda x: fold_heads(x[:, 0:LANES]), sol)
        w2 = each(lambda x: fold_heads(x[:, LANES:2 * LANES]), sol)
        gl = each(lambda d, gc2: gc2[c - 1:c, :] if d == 0 else gc2[0:1, :], dirs, gcs)
        ktail = each(lambda k, g, gc2: (k * jnp.exp(g - gc2)).astype(BF16), ks, gl, gcs)
        qwu = each(lambda x, w, u: _dot(x, jnp.concatenate([stack_heads(w), stack_heads(u)], axis=1).astype(BF16)),
                   qk, w2, u2)
        kwu = each(lambda x, w, u: _dot_tn(x, jnp.concatenate([w, u], axis=1).astype(BF16)), ktail, w2, u2)
        yield
        q_eff = each(lambda q, e, x: (q * e - fold_heads(x[:, 0:LANES])).astype(BF16), qs, egc, qwu)
        m_neg = each(lambda x: jnp.where(same_head, -x[:, 0:LANES], 0.0).astype(BF16), kwu)
        o_loc = each(lambda x: fold_heads(x[:, LANES:2 * LANES]), qwu)
        s_loc = each(lambda x: jnp.where(same_head, x[:, LANES:2 * LANES], 0.0), kwu)
        egl = each(lambda g: jnp.broadcast_to(jnp.exp(g), (SUBLANES, LANES)), gl)
        out.extend(zip(q_eff, m_neg, o_loc, s_loc, egl))

    def chunk_rows(chunk, rows_per_chunk):
        return pl.ds(pl.multiple_of(chunk * rows_per_chunk, rows_per_chunk), rows_per_chunk)

    ctx_chunks = nc // c
    per_group = chunks_per_iter
    n_groups = n_chunks // per_group

    def chunks_at(step):
        return jnp.where(step < ctx_chunks, step + n // c, step - ctx_chunks), n_chunks - 1 - step

    def run_group(local_group, scan_group, states):
        dirs, chunks, qs, ks, vs, betas, gcs = [], [], [], [], [], [], []
        if local_group is not None:
            for g in range(per_group):
                for d, chunk in enumerate(chunks_at(per_group * local_group + g)):
                    rows = chunk_rows(chunk, c)
                    dirs.append(d)
                    chunks.append(chunk)
                    qs.append(q_ref[rows, :])
                    ks.append(k_ref[rows, :])
                    vs.append(v_ref[rows, :])
                    betas.append(bb_ref[d, rows, :])
                    gcs.append(gb_ref[d, rows, :])
        scan_chunks, scan_in = [], []
        if scan_group is not None:
            for g in range(per_group):
                step_chunks = chunks_at(per_group * scan_group + g)
                scan_chunks.append(step_chunks)
                scan_in.append([(qe_ref[d, chunk_rows(ch, c), :], mp_ref[d, chunk_rows(ch, 2 * c), :],
                                 ou_ref[d, chunk_rows(ch, c), :], nn_ref[d, chunk_rows(ch, 2 * c), :],
                                 egl_ref[d, chunk_rows(ch, SUBLANES), :]) for d, ch in enumerate(step_chunks)])
        scan_out = []

        def scan_step(states):
            loaded = scan_in[len(scan_out)]
            res = [_dot(jnp.concatenate([ld[0], ld[1]], axis=0), s2.astype(BF16)) for ld, s2 in zip(loaded, states)]
            scan_out.append([r[0:c] + ld[2] for r, ld in zip(res, loaded)])
            return tuple(s2 * ld[4][0:1] + r[c:3 * c] + ld[3] for s2, ld, r in zip(states, loaded, res))

        local_out = []
        stages = local_stages(dirs, qs, ks, vs, betas, gcs, local_out) if local_group is not None else iter(())
        n_stages = 2 * _log2(c) + 1
        every = max(1, n_stages // per_group)
        for stage, _ in enumerate(stages):
            if scan_group is not None and stage % every == 0 and len(scan_out) < per_group:
                states = scan_step(states)
        while scan_group is not None and len(scan_out) < per_group:
            states = scan_step(states)
        for d, chunk, (q_eff, m_neg, o_loc, s_loc, egl) in zip(dirs, chunks, local_out):
            qe_ref[d, chunk_rows(chunk, c), :] = q_eff
            mp_ref[d, chunk_rows(chunk, 2 * c), :] = m_neg
            ou_ref[d, chunk_rows(chunk, c), :] = o_loc
            nn_ref[d, chunk_rows(chunk, 2 * c), :] = s_loc
            egl_ref[d, chunk_rows(chunk, SUBLANES), :] = egl
        for step_chunks, outs in zip(scan_chunks, scan_out):
            for d, ch in enumerate(step_chunks):
                o_ref[d, chunk_rows(ch, c), :] = outs[d]
        return states

    zero_state = jnp.zeros((2 * c, 2 * c), F32)
    states = run_group(0, None, (zero_state, zero_state))
    states = lax.fori_loop(1, n_groups, lambda j, st: run_group(j, j - 1, st), states)
    run_group(None, n_groups - 1, states)

    ng = ng_ref[...]
    step = 256
    for r in range(0, l, step):
        o = o_ref[0, r:r + step, :] + o_ref[1, r:r + step, :]
        ms = _half_sums(o * o, lane_lo) * (1.0 / GDN_DIM)
        gate = qkvg_ref[0, r:r + step, 3 * LANES:4 * LANES]
        out_ref[0, r:r + step, :] = (o * lax.rsqrt(ms + EPS) * ng * _silu(gate)).astype(out_ref.dtype)


def _dot_sel_lhs(sel_bf16, x):
    hi, lo = _split2(x)
    return _dot(sel_bf16, hi) + _dot(sel_bf16, lo)


def _gdn_call(qkvg, ba, conv_w, gate_params, ng, n, nc):
    b, l, _ = qkvg.shape
    pairs = GDN_HEADS // 2
    n_chunks = l // GDN_CHUNK
    kern = functools.partial(_gdn_kernel, n=n, nc=nc, chunks_per_iter=GDN_CHUNKS_PER_GROUP)
    return pl.pallas_call(
        kern,
        grid=(b, pairs),
        in_specs=[
            pl.BlockSpec((1, l, 4 * LANES), lambda i, p: (i, 0, p)),
            pl.BlockSpec((1, l, LANES), lambda i, p: (i, 0, 0)),
            pl.BlockSpec((3, 4 * LANES), lambda i, p: (0, p)),
            pl.BlockSpec((2, LANES), lambda i, p: (0, 0)),
            pl.BlockSpec((1, LANES), lambda i, p: (0, 0)),
        ],
        out_specs=pl.BlockSpec((1, l, LANES), lambda i, p: (i, 0, p)),
        out_shape=jax.ShapeDtypeStruct((b, l, pairs * LANES), MIXER_OUT_DTYPE),
        scratch_shapes=[
            pltpu.VMEM((l + 5 * SUBLANES, 3 * LANES), F32),
            pltpu.VMEM((l, LANES), F32),
            pltpu.VMEM((l, LANES), F32),
            pltpu.VMEM((l, LANES), F32),
            pltpu.VMEM((2, l, LANES), F32),
            pltpu.VMEM((2, l, LANES), F32),
            pltpu.VMEM((2, l, LANES), BF16),
            pltpu.VMEM((2, 2 * l, LANES), BF16),
            pltpu.VMEM((2, l, LANES), F32),
            pltpu.VMEM((2, 2 * l, LANES), F32),
            pltpu.VMEM((2, n_chunks * SUBLANES, LANES), F32),
            pltpu.VMEM((2, l, LANES), F32),
        ],
        compiler_params=_cparams(("arbitrary", "arbitrary")),
        name="gated_deltanet",
    )(qkvg, ba, conv_w, gate_params, ng)


def _diff_kernel(*refs, key_start, n_sub, lam_init, aliased):
    if aliased:
        refs = refs[1:]
    q_ref, k_ref, vt_ref, lam_ref, ng_ref, o_ref = refs
    lp = lam_ref[...]
    lam = (jnp.exp(jnp.sum(lp[0:1] * lp[1:2], axis=-1, keepdims=True))
           - jnp.exp(jnp.sum(lp[2:3] * lp[3:4], axis=-1, keepdims=True)) + lam_init)
    lane = lax.broadcasted_iota(jnp.int32, (1, LANES), 1)
    halves = (lane < DIFF_DIM, lane >= DIFF_DIM)
    ng = ng_ref[...]
    k = k_ref[0, key_start:, :]
    vt = vt_ref[0, :, key_start:]
    tq = q_ref.shape[1] // n_sub

    def scores_of(i):
        q = q_ref[0, i * tq:(i + 1) * tq, :]
        return [_dot_nt(k, jnp.where(m, q, jnp.zeros_like(q))) for m in halves]

    ahead = scores_of(0)
    for i in range(n_sub):
        s = ahead
        if i + 1 < n_sub:
            ahead = scores_of(i + 1)
        e = [jnp.exp(x - jnp.max(x, axis=0, keepdims=True)) for x in s]
        pv = [_dot(vt, x.astype(BF16)) for x in e]
        parts = [x * (1.0 / jnp.sum(y, axis=0, keepdims=True)) for x, y in zip(pv, e)]
        ot = parts[0] - lam * parts[1]
        ot = ot * lax.rsqrt(jnp.mean(ot * ot, axis=0, keepdims=True) + EPS)
        o_ref[0, i * tq:(i + 1) * tq, :] = (ot.T * ng * (1.0 - lam_init)).astype(o_ref.dtype)


def _diff_call(dq, dk, dvt, lam_p, ng, lam_init, q_rows, first_block, n_q_blocks, key_start, n_sub, prev_out):
    b, l, _ = dq.shape
    aliased = prev_out is not None
    kern = functools.partial(_diff_kernel, key_start=key_start, n_sub=n_sub, lam_init=lam_init, aliased=aliased)
    row_of = lambda t: first_block + t
    in_specs = [
        pl.BlockSpec((1, q_rows, LANES), lambda i, h, t: (i, row_of(t), h)),
        pl.BlockSpec((1, l, LANES), lambda i, h, t: (i, 0, h)),
        pl.BlockSpec((1, LANES, l), lambda i, h, t: (i, h, 0)),
        pl.BlockSpec((4, DIFF_DIM), lambda i, h, t: (0, 0)),
        pl.BlockSpec((1, LANES), lambda i, h, t: (0, 0)),
    ]
    args = [dq, dk, dvt, lam_p, ng.reshape(1, LANES)]
    aliases = {}
    if aliased:
        in_specs = [pl.BlockSpec(memory_space=pl.ANY)] + in_specs
        args = [prev_out] + args
        aliases = {0: 0}
    return pl.pallas_call(
        kern,
        grid=(b, DIFF_HEADS, n_q_blocks),
        in_specs=in_specs,
        out_specs=pl.BlockSpec((1, q_rows, LANES), lambda i, h, t: (i, row_of(t), h)),
        out_shape=jax.ShapeDtypeStruct((b, l, DIFF_HEADS * LANES), MIXER_OUT_DTYPE),
        input_output_aliases=aliases,
        compiler_params=_cparams(("arbitrary", "arbitrary", "arbitrary")),
        name="diff_attention_ctx" if aliased else "diff_attention",
    )(*args)


def _swa_kernel(q_ref, k_ref, v_ref, sink_ref, o_ref, *, n, nc):
    t = pl.program_id(1)
    blk = SWA_BLOCK
    n_x = n // blk
    q = q_ref[0]
    lane = lax.broadcasted_iota(jnp.int32, (1, LANES), 1)
    lane_lo = lane < HEAD_DIM
    sink = sink_ref[...]
    group = SWA_HEADS // SWA_KV_HEADS

    def run(keys, vals, valid):
        head_of_row = lax.broadcasted_iota(jnp.int32, (group * blk, 1), 0) >> _log2(blk)
        kvs = range(SWA_KV_HEADS)
        kk = [keys[:, kvh * LANES:(kvh + 1) * LANES] for kvh in kvs]
        vv = [vals[:, kvh * LANES:(kvh + 1) * LANES] for kvh in kvs]
        qst, sk = [], []
        for kvh in kvs:
            q_rows = []
            sk_rows = jnp.zeros((group * blk, 1), F32)
            for g in range(group):
                h = kvh * group + g
                qp = q[:, (h // 2) * LANES:(h // 2 + 1) * LANES]
                q_rows.append(jnp.where(lane_lo if h % 2 == 0 else jnp.logical_not(lane_lo), qp, jnp.zeros_like(qp)))
                sk_rows = jnp.where(head_of_row == g, sink[:, h:h + 1], sk_rows)
            qst.append(jnp.concatenate(q_rows, axis=0))
            sk.append(sk_rows)
        s = [_dot_nt(x, y) for x, y in zip(qst, kk)]
        if valid is not None:
            s = [jnp.where(valid, x, NEG_INF) for x in s]
        mx = [jnp.maximum(jnp.max(x, axis=-1, keepdims=True), y) for x, y in zip(s, sk)]
        e = [jnp.exp(x - m) for x, m in zip(s, mx)]
        pv = [_dot(x.astype(BF16), y) for x, y in zip(e, vv)]
        den = [jnp.sum(x, axis=-1, keepdims=True) + jnp.exp(y - m) for x, y, m in zip(e, sk, mx)]
        outs = []
        for o, dn in zip(pv, den):
            o = o * (1.0 / dn)
            for g in range(0, group, 2):
                outs.append(jnp.where(lane_lo, o[g * blk:(g + 1) * blk], o[(g + 1) * blk:(g + 2) * blk]))
        o_ref[0] = jnp.concatenate(outs, axis=1).astype(o_ref.dtype)

    @pl.when(t < n_x)
    def _():
        start = pl.multiple_of(jnp.clip((t - 1) * blk, 0, n - 3 * blk), blk)
        keys = jnp.concatenate([k_ref[0, pl.ds(start, 3 * blk), :], k_ref[0, n:n + nc, :]], axis=0)
        vals = jnp.concatenate([v_ref[0, pl.ds(start, 3 * blk), :], v_ref[0, n:n + nc, :]], axis=0)
        shape = (group * blk, 3 * blk + nc)
        qpos = t * blk + (lax.broadcasted_iota(jnp.int32, shape, 0) & (blk - 1))
        col = lax.broadcasted_iota(jnp.int32, shape, 1)
        dist = qpos - (start + col)
        in_window = jnp.logical_and(dist <= SWA_WINDOW, dist >= -SWA_WINDOW)
        valid = jnp.logical_or(col >= 3 * blk, in_window)
        run(keys, vals, valid)

    @pl.when(t >= n_x)
    def _():
        run(k_ref[0, n:n + nc, :], v_ref[0, n:n + nc, :], None)


def _swa_call(q, k, v, sink, n, nc, with_ctx):
    b, l, _ = q.shape
    blk = SWA_BLOCK
    nt = (l if with_ctx else n) // blk
    kern = functools.partial(_swa_kernel, n=n, nc=nc)
    return pl.pallas_call(
        kern,
        grid=(b, nt),
        in_specs=[
            pl.BlockSpec((1, blk, SWA_HEADS * HEAD_DIM), lambda i, t: (i, t, 0)),
            pl.BlockSpec((1, l, 2 * LANES), lambda i, t: (i, 0, 0)),
            pl.BlockSpec((1, l, 2 * LANES), lambda i, t: (i, 0, 0)),
            pl.BlockSpec((1, LANES), lambda i, t: (0, 0)),
        ],
        out_specs=pl.BlockSpec((1, blk, SWA_HEADS * HEAD_DIM), lambda i, t: (i, t, 0)),
        out_shape=jax.ShapeDtypeStruct((b, l, SWA_HEADS * HEAD_DIM), MIXER_OUT_DTYPE),
        compiler_params=_cparams(("arbitrary", "arbitrary")),
        name="window_attention",
    )(q, k, v, sink)


def _dft_tables(n):
    h = n // 2
    r = 1 << (_log2(h) // 2)
    j = jnp.arange(h, dtype=jnp.int32)

    def tables(m):
        thin = lambda k: ((k[:, None] * m[None, :]) % (2 * n)).astype(F32) * (math.pi / n)
        a = thin(r * jnp.arange(h // r, dtype=jnp.int32))[:, None, :]
        b = thin(jnp.arange(r, dtype=jnp.int32))[None, :, :]
        cos = jnp.cos(a) * jnp.cos(b) - jnp.sin(a) * jnp.sin(b)
        sin = jnp.sin(a) * jnp.cos(b) + jnp.cos(a) * jnp.sin(b)
        return cos.reshape(h, h).astype(BF16), (-sin).reshape(h, h).astype(BF16)

    ce, se = tables(2 * j)
    co, so = tables(2 * j + 1)
    return ce, se, co, so, co.T, so.T


def _hyena_filter_kernel(feat_ref, w1_ref, b1_ref, w2_ref, b2_ref, freq_ref, w3f_ref, w3b_ref, dl_ref,
                         ce_ref, se_ref, co_ref, so_ref, ka_ref, kb_ref, km_ref):
    h = feat_ref.shape[1]
    assert h % 2 == 0
    freq = freq_ref[...]
    dl = dl_ref[...]
    row = lax.broadcasted_iota(jnp.int32, (h, 1), 0)

    def taps(part, w3_ref):
        feat = feat_ref[part]
        x = jnp.sin(freq[0:1] * (_dot_f32(feat, w1_ref[...]) + b1_ref[...]))
        x = jnp.sin(freq[1:2] * (_dot_f32(x, w2_ref[...]) + b2_ref[...]))
        return _dot_f32(x, w3_ref[...]) * jnp.exp(-feat[:, 0:1] * dl)

    fe, fo = taps(0, w3f_ref), taps(1, w3f_ref)
    be, bo = jnp.where(row == 0, 0.0, taps(2, w3b_ref)), taps(3, w3b_ref)
    ss = sum(jnp.sum(x * x, axis=0, keepdims=True) for x in (fe, fo, be, bo))
    sc = lax.rsqrt(ss + EPS)
    fe, fo, be, bo = (x * sc for x in (fe, fo, be, bo))
    sgn = jnp.where((row & 1) == 0, 1.0, -1.0)

    def dot2(t_ref, x):
        hi, lo = _split2(x)
        return _dot(t_ref[...], hi) + _dot(t_ref[...], lo)

    def bins(x_even, x_odd):
        ce, co = dot2(ce_ref, x_even), dot2(co_ref, x_odd)
        se, so = dot2(se_ref, x_even), dot2(so_ref, x_odd)
        return ce + co, se + so, ce - co, so - se

    f = bins(fe, fo)
    g = bins(be, bo)
    ka_ref[0, 0], ka_ref[0, 1], kb_ref[0, 0], kb_ref[0, 1] = (x + sgn * y for x, y in zip(f, g))
    mid_r = jnp.sum((fe + be) * sgn, axis=0, keepdims=True)
    mid_i = -jnp.sum((fo + bo) * sgn, axis=0, keepdims=True)
    km_ref[0] = jnp.concatenate([mid_r, mid_i, jnp.zeros((SUBLANES - 2, mid_r.shape[-1]), F32)], axis=0)


def _hyena_filter_call(feats, w1, b1, w2, b2, freq, w3, deltas, tables):
    h = feats.shape[1]
    hid = w2.shape[0]
    ch = deltas.shape[-1]
    tc = MXU_WIDTH
    nct = ch // tc
    const = lambda shape: pl.BlockSpec(shape, lambda o, j: (0,) * len(shape))
    spectrum = pl.BlockSpec((1, 2, h, tc), lambda o, j: (o, 0, 0, j))
    return pl.pallas_call(
        _hyena_filter_kernel,
        grid=(2, nct),
        in_specs=[
            const((4, h, hid)), const((hid, hid)), const((1, hid)), const((hid, hid)), const((1, hid)),
            const((2, hid)),
            pl.BlockSpec((hid, tc), lambda o, j: (0, (2 * o) * nct + j)),
            pl.BlockSpec((hid, tc), lambda o, j: (0, (2 * o + 1) * nct + j)),
            pl.BlockSpec((1, tc), lambda o, j: (0, j)),
        ] + [_resident((h, h))] * 4,
        out_specs=[spectrum, spectrum, pl.BlockSpec((1, SUBLANES, tc), lambda o, j: (o, 0, j))],
        out_shape=[
            jax.ShapeDtypeStruct((2, 2, h, ch), F32),
            jax.ShapeDtypeStruct((2, 2, h, ch), F32),
            jax.ShapeDtypeStruct((2, SUBLANES, ch), F32),
        ],
        compiler_params=_cparams(("arbitrary", "arbitrary")),
        name="hyena_filters",
    )(feats, w1, b1, w2, b2, freq, w3, w3, deltas, *tables[:4])


def _hyena_kernel(*refs, n, aliased):
    if aliased:
        refs = refs[1:]
    (v_ref, x1_ref, x2_ref, cwv_ref, cw1_ref, cw2_ref, ka_ref, kb_ref, km_ref, bias_ref,
     ce_ref, se_ref, co_ref, so_ref, cot_ref, sot_ref, o_ref, pad_ref, z_ref, zb_ref, p_ref, y_ref) = refs
    halo = SUBLANES
    tc = o_ref.shape[-1]
    h = n // 2
    rc = min(h, HY_ROW_CHUNK)
    lane_groups = tc // LANES
    zero_rows = jnp.zeros((halo, LANES), F32)
    for g in range(lane_groups):
        pad_ref[g, 0:halo, :] = zero_rows
        pad_ref[g, halo + n:2 * halo + n, :] = zero_rows

    def stage(ref):
        for r in range(0, n, 2 * rc):
            for g in range(lane_groups):
                pad_ref[g, halo + r:halo + r + 2 * rc, :] = ref[0, r:r + 2 * rc, g * LANES:(g + 1) * LANES]

    def conv_rows(cw, parity, r):
        first = halo + 2 * r + parity - 1
        taps = [jnp.concatenate([pad_ref[g, pl.ds(first + i, rc, stride=2), :] for g in range(lane_groups)], axis=1)
                for i in range(3)]
        return cw[0:1] * taps[0] + cw[1:2] * taps[1] + cw[2:3] * taps[2]

    def alt_sign(r):
        j = r + lax.broadcasted_iota(jnp.int32, (rc, 1), 0)
        return j, jnp.where((j & 1) == 0, 1.0, -1.0)

    stage(v_ref)
    cw = cwv_ref[...]
    for parity in range(2):
        for r in range(0, h, rc):
            z = conv_rows(cw, parity, r)
            z_ref[parity, r:r + rc, :] = z
            zb_ref[parity, r:r + rc, :] = z.astype(BF16)

    for o, (gate_ref, gate_cw_ref) in enumerate(((x1_ref, cw1_ref), (x2_ref, cw2_ref))):
        mid_r = jnp.zeros((1, tc), F32)
        mid_i = jnp.zeros((1, tc), F32)
        for r in range(0, h, rc):
            _, sgn = alt_sign(r)
            mid_r = mid_r + jnp.sum(z_ref[0, r:r + rc, :] * sgn, axis=0, keepdims=True)
            mid_i = mid_i - jnp.sum(z_ref[1, r:r + rc, :] * sgn, axis=0, keepdims=True)
        km_r = km_ref[o, 0:1, :]
        km_i = km_ref[o, 1:2, :]
        pm_r = (mid_r * km_r - mid_i * km_i) * (1.0 / n)
        pm_i = (mid_r * km_i + mid_i * km_r) * (1.0 / n)
        ze = zb_ref[0]
        zo = zb_ref[1]
        for r in range(0, h, rc):
            k, _ = alt_sign(r)
            rows = slice(r, r + rc)
            ce, co = _dot(ce_ref[rows, :], ze), _dot(co_ref[rows, :], zo)
            se, so = _dot(se_ref[rows, :], ze), _dot(so_ref[rows, :], zo)
            xa_r, xb_r, xa_i, xb_i = ce + co, ce - co, se + so, so - se
            wgt = jnp.where(k == 0, 0.5 / n, 1.0 / n)
            ka_r, ka_i = ka_ref[o, 0, rows, :], ka_ref[o, 1, rows, :]
            kb_r, kb_i = kb_ref[o, 0, rows, :], kb_ref[o, 1, rows, :]
            pa_r = (xa_r * ka_r - xa_i * ka_i) * wgt
            pa_i = (xa_r * ka_i + xa_i * ka_r) * wgt
            pb_r = (xb_r * kb_r - xb_i * kb_i) * wgt
            pb_i = (xb_r * kb_i + xb_i * kb_r) * wgt
            p_ref[0, rows, :] = (pa_r + pb_r).astype(BF16)
            p_ref[1, rows, :] = (pa_i - pb_i).astype(BF16)
            p_ref[2, rows, :] = (pa_r - pb_r).astype(BF16)
            p_ref[3, rows, :] = (pa_i + pb_i).astype(BF16)
        stage(gate_ref)
        cw = gate_cw_ref[...]
        bias = bias_ref[o:o + 1, :]
        for parity, (c_ref, s_ref, mid) in enumerate(((ce_ref, se_ref, pm_r), (cot_ref, sot_ref, -pm_i))):
            for r in range(0, h, rc):
                _, sgn = alt_sign(r)
                rows = slice(r, r + rc)
                y = (_dot(c_ref[rows, :], p_ref[2 * parity]) + _dot(s_ref[rows, :], p_ref[2 * parity + 1])
                     + sgn * mid)
                z = conv_rows(cw, parity, r) * (y + z_ref[parity, rows, :] * bias)
                if o == 0:
                    z_ref[parity, rows, :] = z
                    zb_ref[parity, rows, :] = z.astype(BF16)
                else:
                    for g in range(lane_groups):
                        y_ref[g, pl.ds(2 * r + parity, rc, stride=2), :] = z[:, g * LANES:(g + 1) * LANES]
    for r in range(0, n, 2 * rc):
        rows = slice(r, r + 2 * rc)
        o_ref[0, rows, :] = jnp.concatenate([y_ref[g, rows, :] for g in range(lane_groups)],
                                            axis=1).astype(o_ref.dtype)


def _hyena_call(u, conv_w, ka, kb, km, bias, tables, n, row_block, prev_out):
    b, l, _ = u.shape
    ch = bias.shape[-1]
    tc = MXU_WIDTH
    nct = ch // tc
    h = n // 2
    aliased = prev_out is not None
    kern = functools.partial(_hyena_kernel, n=n, aliased=aliased)
    once = pl.Buffered(1)
    in_specs = [
        pl.BlockSpec((1, n, tc), lambda j, i: (i, row_block, j)),
        pl.BlockSpec((1, n, tc), lambda j, i: (i, row_block, nct + j)),
        pl.BlockSpec((1, n, tc), lambda j, i: (i, row_block, 2 * nct + j)),
        pl.BlockSpec((3, tc), lambda j, i: (0, j)),
        pl.BlockSpec((3, tc), lambda j, i: (0, nct + j)),
        pl.BlockSpec((3, tc), lambda j, i: (0, 2 * nct + j)),
        pl.BlockSpec((2, 2, h, tc), lambda j, i: (0, 0, 0, j), pipeline_mode=once),
        pl.BlockSpec((2, 2, h, tc), lambda j, i: (0, 0, 0, j), pipeline_mode=once),
        pl.BlockSpec((2, SUBLANES, tc), lambda j, i: (0, 0, j)),
        pl.BlockSpec((2, tc), lambda j, i: (0, j)),
    ] + [_resident((h, h))] * 6
    args = [u, u, u, conv_w, conv_w, conv_w, ka, kb, km, bias, *tables]
    aliases = {}
    if aliased:
        in_specs = [pl.BlockSpec(memory_space=pl.ANY)] + in_specs
        args = [prev_out] + args
        aliases = {0: 0}
    return pl.pallas_call(
        kern,
        grid=(nct, b),
        in_specs=in_specs,
        out_specs=pl.BlockSpec((1, n, tc), lambda j, i: (i, row_block, j)),
        out_shape=jax.ShapeDtypeStruct((b, l, ch), MIXER_OUT_DTYPE),
        scratch_shapes=[
            pltpu.VMEM((tc // LANES, n + 2 * SUBLANES, LANES), F32),
            pltpu.VMEM((2, h, tc), F32),
            pltpu.VMEM((2, h, tc), BF16),
            pltpu.VMEM((4, h, tc), BF16),
            pltpu.VMEM((tc // LANES, n, LANES), F32),
        ],
        input_output_aliases=aliases,
        compiler_params=_cparams(("arbitrary", "arbitrary")),
        name="hyena_conv_n%d" % n,
    )(*args)


def _rope_tables(n, nc):
    rows = n // GRID_W
    row = jnp.repeat(jnp.arange(rows, dtype=F32), GRID_W)
    col = jnp.tile(jnp.arange(GRID_W, dtype=F32), rows)
    half = HEAD_DIM // 2
    inv = ROPE_BASE ** (-jnp.arange(0, half, 2, dtype=F32) / half)
    ar = row[:, None] * inv
    ac = col[:, None] * inv
    cos = jnp.concatenate([jnp.cos(ar), jnp.cos(ar), jnp.cos(ac), jnp.cos(ac)], axis=-1)
    sin = jnp.concatenate([-jnp.sin(ar), jnp.sin(ar), -jnp.sin(ac), jnp.sin(ac)], axis=-1)
    cos = jnp.concatenate([cos, jnp.ones((nc, HEAD_DIM), F32)], axis=0)
    sin = jnp.concatenate([sin, jnp.zeros((nc, HEAD_DIM), F32)], axis=0)
    return jnp.tile(cos, (1, LANES // HEAD_DIM)), jnp.tile(sin, (1, LANES // HEAD_DIM))


def _rope_partner_cols(width):
    d = np.arange(width)
    quarter = HEAD_DIM // 4
    return np.where((d % (2 * quarter)) < quarter, d + quarter, d - quarter)


def _hyena_feats(n):
    pos = jnp.arange(n, dtype=F32)
    t = pos / max(n - 1, 1)
    ang = (2.0 * math.pi * pos / n)[:, None] * jnp.linspace(1e-4, HY_BANDS - 1, HY_BANDS, dtype=F32)[None, :]
    feats = jnp.concatenate([t[:, None], jnp.cos(ang), -jnp.sin(ang)], axis=-1)
    feats = jnp.pad(feats, ((0, 0), (0, 64 - feats.shape[-1])))
    back = jnp.concatenate([feats[0:1], jnp.flip(feats[1:], axis=0)], axis=0)
    return jnp.stack([feats[0::2], feats[1::2], back[0::2], back[1::2]])


def _pad_cols(w, width):
    return jnp.pad(w, ((0, 0), (0, width - w.shape[-1])))


def _layer_ab(xz, mod, norm_g0, w_in, conv_w, a_log, dt_bias, gdn_g, lam_p, diff_g, lam_init, rope, n, nc):
    hd = GDN_HEADS * GDN_DIM
    wq, wk, wv, wg = (w_in[:, i * hd:(i + 1) * hd] for i in range(4))
    o = 4 * hd
    w_beta, w_alpha = w_in[:, o:o + 16], w_in[:, o + 16:o + 32]
    o += 32
    dd = DIFF_HEADS * 2 * DIFF_DIM
    wdq, wdk, wdv = (w_in[:, o + i * dd:o + (i + 1) * dd] for i in range(3))
    pairs = GDN_HEADS // 2
    pair_cols = lambda w: [w[:, p * LANES:(p + 1) * LANES] for p in range(pairs)]
    w_qkvg = jnp.concatenate([blk for grp in zip(pair_cols(wq), pair_cols(wk), pair_cols(wv), pair_cols(wg))
                              for blk in grp], axis=1)
    perm = _rope_partner_cols(dd)
    w_all = jnp.concatenate([w_qkvg, _pad_cols(jnp.concatenate([w_beta, w_alpha], axis=1), LANES),
                             wdq, wdk, wdv, wdq[:, perm], wdk[:, perm]], axis=1).astype(BF16)
    c0 = 4 * hd
    c1 = c0 + LANES
    segs = (_Seg(0, c0), _Seg(c0, LANES),
            _Seg(c1, dd, rot_start=c1 + 3 * dd, scale=DIFF_DIM ** -0.5, dtype=BF16),
            _Seg(c1 + dd, dd, rot_start=c1 + 4 * dd, dtype=BF16),
            _Seg(c1 + 2 * dd, dd, dtype=BF16, transposed=True))
    qkvg, ba, dq, dk, dvt = _proj_call(xz, mod, norm_g0, w_all, rope[0], rope[1], segs, n // ROW_TILE, "proj_ab")

    cq, ck, cv = (conv_w[:, i * hd:(i + 1) * hd] for i in range(3))
    zeros = jnp.zeros((3, LANES), F32)
    conv_l = jnp.concatenate([blk for p in range(pairs) for blk in
                              (cq[:, p * LANES:(p + 1) * LANES], ck[:, p * LANES:(p + 1) * LANES],
                               cv[:, p * LANES:(p + 1) * LANES], zeros)], axis=1)
    n_gate = 2 * GDN_HEADS
    on_decay_lanes = lambda t: jnp.pad(t.reshape(1, n_gate), ((0, 0), (n_gate, LANES - 2 * n_gate)))
    gate_params = jnp.concatenate([on_decay_lanes(a_log), on_decay_lanes(dt_bias)], axis=0)
    ng = jnp.tile(gdn_g.reshape(1, GDN_DIM), (1, 2))
    oa = _gdn_call(qkvg, ba, conv_l, gate_params, ng, n, nc)
    q_rows = DIFF_SUB_TILES * ROW_TILE
    ob = _diff_call(dq, dk, dvt, lam_p, diff_g, lam_init, q_rows, 0, n // q_rows, 0, DIFF_SUB_TILES, None)
    ob = _diff_call(dq, dk, dvt, lam_p, diff_g, lam_init, nc, n // nc, 1, n, 1, ob)
    return oa, ob


def _layer_cd(xz, mod, norm_g0, w_in, sink, hy_conv, hy_w1, hy_b1, hy_w2, hy_b2, hy_w3, hy_freq, hy_bias,
              rope, n, nc, last, dft_x, dft_c):
    qd = SWA_HEADS * HEAD_DIM
    kd = SWA_KV_HEADS * HEAD_DIM
    wq, wk, wv, wu = w_in[:, 0:qd], w_in[:, qd:qd + kd], w_in[:, qd + kd:qd + 2 * kd], w_in[:, qd + 2 * kd:]
    dup = lambda w: jnp.concatenate([w[:, 0:HEAD_DIM], w[:, 0:HEAD_DIM], w[:, HEAD_DIM:], w[:, HEAD_DIM:]], axis=1)
    wk2, wv2 = dup(wk), dup(wv)
    ud = wu.shape[1]
    w_all = jnp.concatenate([wq, wk2, wv2, wu, wq[:, _rope_partner_cols(qd)], wk2[:, _rope_partner_cols(2 * kd)]],
                            axis=1).astype(BF16)
    o_u = qd + 4 * kd
    segs = (_Seg(0, qd, rot_start=o_u + ud, scale=HEAD_DIM ** -0.5, dtype=BF16),
            _Seg(qd, 2 * kd, rot_start=o_u + ud + qd, dtype=BF16),
            _Seg(qd + 2 * kd, 2 * kd, dtype=BF16), _Seg(o_u, ud))
    q, k, v, u = _proj_call(xz, mod, norm_g0, w_all, rope[0], rope[1], segs, n // ROW_TILE, "proj_cd")
    oc = _swa_call(q, k, v, _pad_cols(sink.reshape(1, SWA_HEADS), LANES), n, nc, not last)

    ch = hy_bias.shape[-1]
    deltas = jnp.abs(jnp.linspace(HY_MIN_DECAY, HY_MAX_DECAY, ch, dtype=F32)).reshape(1, ch)
    hid = hy_w2.shape[0]
    w1p = jnp.pad(hy_w1, ((0, hid - hy_w1.shape[0]), (0, 0)))
    filt = lambda m, dft: _hyena_filter_call(_hyena_feats(m), w1p, hy_b1.reshape(1, hid), hy_w2,
                                             hy_b2.reshape(1, hid), hy_freq, hy_w3, deltas, dft)
    od = _hyena_call(u, hy_conv, *filt(n, dft_x), hy_bias, dft_x, n, 0, None)
    if not last:
        od = _hyena_call(u, hy_conv, *filt(nc, dft_c), hy_bias, dft_c, nc, n // nc, od)
    return oc, od


def kernel(x, c, ctx, c_ctx, w_mod, b_mod, norm_g, ffn_w_up, ffn_conv, ffn_w_down, ab_w_in, ab_w_out, gdn_conv, gdn_a_log, gdn_dt_bias, gdn_norm_g, diff_lambda, diff_norm_g, cd_w_in, cd_w_out, swa_sink, hy_conv, hy_w1, hy_b1, hy_w2, hy_b2, hy_w3, hy_freq, hy_bias):
    b, n, d = x.shape
    nc = ctx.shape[1]
    depth = w_mod.shape[0]
    assert n % ROW_TILE == 0 and nc == ROW_TILE and n % GRID_W == 0
    xz = jnp.concatenate([x, ctx], axis=1)
    rows = -(-(b + 1) // SUBLANES) * SUBLANES
    cc = jnp.concatenate([c, c_ctx[None], jnp.zeros((rows - b - 1, d), F32)], axis=0)
    mods = _mod_call(cc, w_mod, b_mod)
    mod_all = jnp.concatenate([mods[:, :b].reshape(depth, b, 1, 6, d),
                               jnp.broadcast_to(mods[:, b].reshape(depth, 1, 1, 6, d), (depth, b, 1, 6, d))], axis=2)
    rope = _rope_tables(n, nc)
    dft_x = _dft_tables(n)
    dft_c = _dft_tables(nc)
    n_x_tiles = n // ROW_TILE
    for l in range(depth):
        last = l == depth - 1
        i = l // 2
        mod = mod_all[l]
        if l % 2 == 0:
            lam_init = 0.8 - 0.6 * math.exp(-0.3 * l)
            o1, o2 = _layer_ab(xz, mod, norm_g[l, 0], ab_w_in[i], gdn_conv[i], gdn_a_log[i], gdn_dt_bias[i],
                               gdn_norm_g[i], diff_lambda[i], diff_norm_g[i], lam_init, rope, n, nc)
            w_out = ab_w_out[i]
        else:
            o1, o2 = _layer_cd(xz, mod, norm_g[l, 0], cd_w_in[i], swa_sink[i], hy_conv[i], hy_w1[i], hy_b1[i],
                               hy_w2[i], hy_b2[i], hy_w3[i], hy_freq[i], hy_bias[i], rope, n, nc, last, dft_x, dft_c)
            w_out = cd_w_out[i]
        n_tiles = (n if last else n + nc) // ROW_TILE
        xz = _post_call(o1, o2, xz, mod, norm_g[l, 1], norm_g[l, 2], norm_g[l, 3], w_out.astype(BF16),
                        ffn_w_up[l].astype(BF16), ffn_conv[l], ffn_w_down[l].astype(BF16), n_tiles, n_x_tiles)
    return xz
```

```python
import functools
import math
from typing import NamedTuple, Optional

import jax
import jax.numpy as jnp
import numpy as np
from jax import lax
from jax.experimental import pallas as pl
from jax.experimental.pallas import tpu as pltpu

F32 = jnp.float32
BF16 = jnp.bfloat16
MIXER_OUT_DTYPE = BF16

EPS = 1e-6
NEG_INF = -1e30
GRID_W = 64
HEAD_DIM = 64
ROPE_BASE = 10000.0
GDN_HEADS = 8
GDN_DIM = 64
GDN_CHUNK = 64
GDN_CHUNKS_PER_GROUP = 9
DIFF_HEADS = 4
DIFF_DIM = 64
DIFF_SUB_TILES = 8
SWA_HEADS = 8
SWA_KV_HEADS = 2
SWA_WINDOW = 128
SWA_BLOCK = 128
HY_BANDS = 16
HY_MIN_DECAY = math.log(1e-2) / 1.5
HY_MAX_DECAY = math.log(1e-2) / 0.3
HY_ROW_CHUNK = 512

LANES = 128
SUBLANES = 8
MXU_WIDTH = 256
FFN_COL_CHUNK = 11 * MXU_WIDTH
ROW_TILE = 256
VMEM_LIMIT = 56 * 1024 * 1024


def _cparams(sem):
    return pltpu.CompilerParams(dimension_semantics=sem, vmem_limit_bytes=VMEM_LIMIT)


def _resident(shape):
    zeros = (0,) * len(shape)
    return pl.BlockSpec(shape, lambda *_: zeros, pipeline_mode=pl.Buffered(1))


def _log2(v):
    assert v & (v - 1) == 0
    return v.bit_length() - 1


def _sigmoid(x):
    return 1.0 / (1.0 + jnp.exp(-x))


def _silu(x):
    return x * _sigmoid(x)


def _softplus(x):
    return jnp.maximum(x, 0.0) + jnp.log1p(jnp.exp(-jnp.abs(x)))


def _dot(a, b):
    return jnp.dot(a, b, preferred_element_type=F32)


def _dot_nt(a, b):
    return lax.dot_general(a, b, (((1,), (1,)), ((), ())), preferred_element_type=F32)


def _dot_tn(a, b):
    return lax.dot_general(a, b, (((0,), (0,)), ((), ())), preferred_element_type=F32)


def _dot_f32(a, b):
    return jnp.dot(a, b, preferred_element_type=F32, precision=lax.Precision.HIGHEST)


def _split2(x):
    hi = x.astype(BF16)
    lo = (x - hi.astype(F32)).astype(BF16)
    return hi, lo


def _dot_sel(x, sel_bf16):
    hi, lo = _split2(x)
    return _dot(hi, sel_bf16) + _dot(lo, sel_bf16)


def _mm(a, b):
    return _dot(a.astype(BF16), b.astype(BF16))


def _rms(y, g):
    return y * lax.rsqrt(jnp.mean(y * y, axis=-1, keepdims=True) + EPS) * g


def _modnorm(x, g, shift, scale):
    return _rms(x, g) * (1.0 + scale) + shift


def _mod_kernel(cc_ref, w_ref, b_ref, o_ref):
    s = _silu(cc_ref[...])
    o_ref[0] = _dot(s.astype(BF16), w_ref[0].astype(BF16)) + b_ref[0]


def _mod_call(cc, w_mod, b_mod):
    depth, d, nm = w_mod.shape
    rows = cc.shape[0]
    ct = 1536
    return pl.pallas_call(
        _mod_kernel,
        grid=(depth, nm // ct),
        in_specs=[
            pl.BlockSpec((rows, d), lambda l, j: (0, 0)),
            pl.BlockSpec((1, d, ct), lambda l, j: (l, 0, j)),
            pl.BlockSpec((1, 1, ct), lambda l, j: (l, 0, j)),
        ],
        out_specs=pl.BlockSpec((1, rows, ct), lambda l, j: (l, 0, j)),
        out_shape=jax.ShapeDtypeStruct((depth, rows, nm), F32),
        compiler_params=_cparams(("arbitrary", "arbitrary")),
        name="adaln_mod",
    )(cc, w_mod, b_mod.reshape(depth, 1, nm))


class _Seg(NamedTuple):
    start: int
    width: int
    rot_start: Optional[int] = None
    scale: float = 1.0
    dtype: type = F32
    transposed: bool = False


def _proj_kernel(x_ref, mod_ref, g_ref, w_ref, cos_ref, sin_ref, *out_refs, segs):
    m = mod_ref[0, 0]
    h = _modnorm(x_ref[0], g_ref[...], m[0:1], m[1:2]).astype(BF16)
    for o_ref, seg in zip(out_refs, segs):
        y = _dot(h, w_ref[:, seg.start:seg.start + seg.width])
        if seg.rot_start is not None:
            yr = _dot(h, w_ref[:, seg.rot_start:seg.rot_start + seg.width])
            reps = seg.width // LANES
            cos = jnp.concatenate([cos_ref[...]] * reps, axis=1)
            sin = jnp.concatenate([sin_ref[...]] * reps, axis=1)
            y = y * cos + yr * sin
        if seg.scale != 1.0:
            y = y * seg.scale
        if seg.transposed:
            y = y.T
        o_ref[0] = y.astype(seg.dtype)


def _proj_call(xz, mod, g, w, cos_t, sin_t, segs, n_x_tiles, name):
    b, l, d = xz.shape
    tm = ROW_TILE
    nt = l // tm
    p = w.shape[1]
    return pl.pallas_call(
        functools.partial(_proj_kernel, segs=segs),
        grid=(nt, b),
        in_specs=[
            pl.BlockSpec((1, tm, d), lambda t, i: (i, t, 0)),
            pl.BlockSpec((1, 1, 6, d), lambda t, i: (i, t // n_x_tiles, 0, 0)),
            pl.BlockSpec((1, d), lambda t, i: (0, 0)),
            _resident((d, p)),
            pl.BlockSpec((tm, LANES), lambda t, i: (t, 0)),
            pl.BlockSpec((tm, LANES), lambda t, i: (t, 0)),
        ],
        out_specs=[pl.BlockSpec((1, s.width, tm), lambda t, i: (i, 0, t)) if s.transposed
                   else pl.BlockSpec((1, tm, s.width), lambda t, i: (i, t, 0)) for s in segs],
        out_shape=[jax.ShapeDtypeStruct((b, s.width, l) if s.transposed else (b, l, s.width), s.dtype)
                   for s in segs],
        compiler_params=_cparams(("arbitrary", "arbitrary")),
        name=name,
    )(xz, mod, g.reshape(1, d), w, cos_t, sin_t)


def _post_kernel(o1p_ref, o1_ref, o1n_ref, o2p_ref, o2_ref, o2n_ref, xp_ref, x_ref, xn_ref, mod_ref,
                 g1_ref, g2_ref, g3_ref, wout_ref, wup_ref, cw_ref, wdn_ref, out_ref, up_ref,
                 *, tm, n_x_tiles, n_tiles, cf, dff):
    t = pl.program_id(0)
    first = jnp.logical_or(t == 0, t == n_x_tiles)
    last = jnp.logical_or(t == n_x_tiles - 1, t == n_tiles - 1)
    m = mod_ref[0, 0]
    halo = SUBLANES
    ohalo = o1p_ref.shape[1]
    k1 = o1_ref.shape[-1]
    o1e = jnp.concatenate([o1p_ref[0], o1_ref[0], o1n_ref[0]], axis=0)
    o2e = jnp.concatenate([o2p_ref[0], o2_ref[0], o2n_ref[0]], axis=0)
    y = _dot(o1e, wout_ref[0:k1, :]) + _dot(o2e, wout_ref[k1:, :])
    y = y[ohalo - halo:ohalo + tm + halo]
    xe = jnp.concatenate([xp_ref[0], x_ref[0], xn_ref[0]], axis=0)
    x1 = xe + m[2:3] * _rms(y, g1_ref[...])
    h = _modnorm(x1, g2_ref[...], m[3:4], m[4:5]).astype(BF16)
    acc = jnp.zeros((tm, x_ref.shape[-1]), F32)
    for c0 in range(0, dff, cf):
        wd = min(cf, dff - c0)
        halves = []
        for half, base in enumerate((c0, dff + c0)):
            u = _dot(h, wup_ref[:, base:base + wd])
            up_ref[half, :, 0:wd] = u
            up_ref[half, 0:halo, 0:wd] = jnp.where(first, 0.0, u[0:halo])
            up_ref[half, tm + halo:tm + 2 * halo, 0:wd] = jnp.where(last, 0.0, u[tm + halo:])
            cw = cw_ref[:, base:base + wd]
            halves.append(cw[0:1] * up_ref[half, halo - 1:halo - 1 + tm, 0:wd]
                          + cw[1:2] * up_ref[half, halo:halo + tm, 0:wd]
                          + cw[2:3] * up_ref[half, halo + 1:halo + 1 + tm, 0:wd])
        act = (_silu(halves[1]) * halves[0]).astype(BF16)
        acc = acc + _dot(act, wdn_ref[c0:c0 + wd, :])
    out_ref[0] = x1[halo:halo + tm] + m[5:6] * _rms(acc, g3_ref[...])


def _post_call(o1, o2, xz, mod, g1, g2, g3, w_out, w_up, conv_w, w_down, n_tiles, n_x_tiles):
    b, _, d = xz.shape
    tm = ROW_TILE
    rows = n_tiles * tm
    dff = w_down.shape[0]
    cf = FFN_COL_CHUNK
    k1, k2 = o1.shape[-1], o2.shape[-1]
    ohalo = 2 * SUBLANES
    kern = functools.partial(_post_kernel, tm=tm, n_x_tiles=n_x_tiles, n_tiles=n_tiles, cf=cf, dff=dff)

    def with_halos(width, halo_rows):
        per_tile = tm // halo_rows
        n_blocks = rows // halo_rows
        return [
            pl.BlockSpec((1, halo_rows, width), lambda t, i: (i, jnp.maximum(t * per_tile - 1, 0), 0)),
            pl.BlockSpec((1, tm, width), lambda t, i: (i, t, 0)),
            pl.BlockSpec((1, halo_rows, width), lambda t, i: (i, jnp.minimum((t + 1) * per_tile, n_blocks - 1), 0)),
        ]

    row_vec = pl.BlockSpec((1, d), lambda t, i: (0, 0))
    return pl.pallas_call(
        kern,
        grid=(n_tiles, b),
        in_specs=with_halos(k1, ohalo) + with_halos(k2, ohalo) + with_halos(d, SUBLANES) + [
            pl.BlockSpec((1, 1, 6, d), lambda t, i: (i, t // n_x_tiles, 0, 0)),
            row_vec, row_vec, row_vec,
            _resident((k1 + k2, d)),
            _resident((d, 2 * dff)),
            pl.BlockSpec((3, 2 * dff), lambda t, i: (0, 0)),
            _resident((dff, d)),
        ],
        out_specs=pl.BlockSpec((1, tm, d), lambda t, i: (i, t, 0)),
        out_shape=jax.ShapeDtypeStruct((b, rows, d), F32),
        scratch_shapes=[pltpu.VMEM((2, tm + 2 * SUBLANES, cf), F32)],
        compiler_params=_cparams(("arbitrary", "arbitrary")),
        name="mixer_out_conv_ffn",
    )(o1, o1, o1, o2, o2, o2, xz, xz, xz, mod, g1.reshape(1, d), g2.reshape(1, d), g3.reshape(1, d),
      w_out, w_up, conv_w, w_down)


def _half_sums(x2, lane_lo):
    s0 = jnp.sum(jnp.where(lane_lo, x2, 0.0), axis=-1, keepdims=True)
    s1 = jnp.sum(jnp.where(lane_lo, 0.0, x2), axis=-1, keepdims=True)
    return jnp.where(lane_lo, s0, s1)


def _gdn_kernel(qkvg_ref, ba_ref, cw_ref, gp_ref, ng_ref, out_ref,
                pad_ref, q_ref, k_ref, v_ref, bb_ref, gb_ref, qe_ref, mp_ref, ou_ref, nn_ref, egl_ref, o_ref,
                *, n, nc, chunks_per_iter):
    l = n + nc
    c = GDN_CHUNK
    n_chunks = l // c
    pair = pl.program_id(1)
    halo = SUBLANES
    lane = lax.broadcasted_iota(jnp.int32, (1, LANES), 1)
    lane_lo = lane < GDN_DIM

    cw = cw_ref[:, 0:3 * LANES]
    zero_rows = jnp.zeros((halo, 3 * LANES), F32)
    for seq_start, seq_len in ((0, n), (n, nc)):
        base = halo + seq_start + (2 * halo if seq_start else 0)
        pad_ref[base - halo:base, :] = zero_rows
        pad_ref[base + seq_len:base + seq_len + halo, :] = zero_rows
        step = 256
        for r in range(0, seq_len, step):
            pad_ref[base + r:base + r + step, :] = qkvg_ref[0, seq_start + r:seq_start + r + step, 0:3 * LANES]
        for r in range(0, seq_len, step):
            y = (cw[0:1] * pad_ref[base + r - 1:base + r - 1 + step, :]
                 + cw[1:2] * pad_ref[base + r:base + r + step, :]
                 + cw[2:3] * pad_ref[base + r + 1:base + r + 1 + step, :])
            y = _silu(y)
            q = y[:, 0:LANES]
            k = y[:, LANES:2 * LANES]
            rows = slice(seq_start + r, seq_start + r + step)
            q_ref[rows, :] = q * lax.rsqrt(_half_sums(q * q, lane_lo) + EPS) * (GDN_DIM ** -0.5)
            k_ref[rows, :] = k * lax.rsqrt(_half_sums(k * k, lane_lo) + EPS)
            v_ref[rows, :] = y[:, 2 * LANES:3 * LANES]

    sel_r = lax.broadcasted_iota(jnp.int32, (LANES, 4 * LANES), 0)
    sel_c = lax.broadcasted_iota(jnp.int32, (LANES, 4 * LANES), 1)
    quarter = sel_c >> _log2(LANES)
    src_lane = (quarter & 1) * 2 * GDN_HEADS + (quarter >> 1) * GDN_HEADS + 2 * pair + ((sel_c >> _log2(GDN_DIM)) & 1)
    sel = (sel_r == src_lane).astype(BF16)
    gblk = 256
    bi = lax.broadcasted_iota(jnp.int32, (gblk, gblk), 0)
    bj = lax.broadcasted_iota(jnp.int32, (gblk, gblk), 1)
    same_chunk = (bi >> _log2(c)) == (bj >> _log2(c))
    csum = (jnp.logical_and(same_chunk, bi >= bj).astype(BF16), jnp.logical_and(same_chunk, bi <= bj).astype(BF16))
    neg_a = -jnp.exp(gp_ref[0:1, :])
    dt_bias = gp_ref[1:2, :]
    for r in range(0, l, gblk):
        ba = ba_ref[0, r:r + gblk, :]
        gates = jnp.where(lane < 2 * GDN_HEADS, _sigmoid(ba), neg_a * _softplus(ba + dt_bias))
        x = _dot_sel(gates, sel)
        for d in range(2):
            bb_ref[d, r:r + gblk, :] = x[:, 2 * d * LANES:(2 * d + 1) * LANES]
            gb_ref[d, r:r + gblk, :] = _dot_sel_lhs(csum[d], x[:, (2 * d + 1) * LANES:(2 * d + 2) * LANES])

    r2 = lax.broadcasted_iota(jnp.int32, (2 * c, 2 * c), 0)
    c2 = lax.broadcasted_iota(jnp.int32, (2 * c, 2 * c), 1)
    same_head = (r2 >= c) == (c2 >= c)
    eye = (r2 == c2).astype(F32)
    masks = ((jnp.logical_and(same_head, r2 >= c2), jnp.logical_and(same_head, r2 > c2)),
             (jnp.logical_and(same_head, r2 <= c2), jnp.logical_and(same_head, r2 < c2)))
    m0 = lane_lo.astype(F32)
    m1 = 1.0 - m0

    def pair_mask(lv, lower):
        same_block = (r2 >> (lv + 1)) == (c2 >> (lv + 1))
        r_hi = ((r2 >> lv) & 1) == 1
        c_hi = ((c2 >> lv) & 1) == 1
        off = jnp.logical_and(r_hi, jnp.logical_not(c_hi)) if lower else jnp.logical_and(c_hi, jnp.logical_not(r_hi))
        return jnp.logical_and(same_block, off)

    pair_masks = tuple(tuple(pair_mask(lv, lower) for lv in range(_log2(c))) for lower in (True, False))

    def stack_heads(x2):
        return jnp.concatenate([x2 * m0, x2 * m1], axis=0)

    def fold_heads(x):
        return x[0:c] + x[c:2 * c]

    def local_stages(dirs, qs, ks, vs, betas, gcs, out):
        each = lambda f, *cols: [f(*args) for args in zip(*cols)]
        incl = [masks[d][0] for d in dirs]
        strict = [masks[d][1] for d in dirs]
        g1 = each(lambda gc2: jnp.concatenate([gc2, gc2], axis=0), gcs)
        decay = each(lambda g, m: jnp.where(m, jnp.exp(jnp.where(m, g - g.T, 0.0)), 0.0), g1, incl)
        kb = each(lambda k, b: k * b, ks, betas)
        kst = each(lambda k: stack_heads(k).astype(BF16), ks)
        a_raw = each(lambda x, y: _dot_nt(stack_heads(x).astype(BF16), y), kb, kst)
        qk_raw = each(lambda x, y: _dot_nt(stack_heads(x).astype(BF16), y), qs, kst)
        yield
        qk = each(lambda m, x, dc: jnp.where(m, x * dc, 0.0).astype(BF16), incl, qk_raw, decay)
        a = each(lambda m, x, dc: jnp.where(m, x * dc, 0.0), strict, a_raw, decay)
        tinv = each(lambda d, x: eye - jnp.where(pair_masks[d][0], x, 0.0), dirs, a)
        for lv in range(1, _log2(c)):
            ta = each(lambda d, t, x: _mm(t, jnp.where(pair_masks[d][lv], x, 0.0)), dirs, tinv, a)
            yield
            tat = each(_mm, ta, tinv)
            yield
            tinv = each(lambda t, x: t - x, tinv, tat)
        egc = each(jnp.exp, gcs)
        rhs = each(lambda v, b, x, e: jnp.concatenate([stack_heads(v * b), stack_heads(x * e)], axis=1),
                   vs, betas, kb, egc)
        sol = each(_mm, tinv, rhs)
        yield
        u2 = each(lambda x: fold_heads(x[:, 0:LANES]), sol)
        w2 = each(lambda x: fold_heads(x[:, LANES:2 * LANES]), sol)
        gl = each(lambda d, gc2: gc2[c - 1:c, :] if d == 0 else gc2[0:1, :], dirs, gcs)
        ktail = each(lambda k, g, gc2: (k * jnp.exp(g - gc2)).astype(BF16), ks, gl, gcs)
        qwu = each(lambda x, w, u: _dot(x, jnp.concatenate([stack_heads(w), stack_heads(u)], axis=1).astype(BF16)),
                   qk, w2, u2)
        kwu = each(lambda x, w, u: _dot_tn(x, jnp.concatenate([w, u], axis=1).astype(BF16)), ktail, w2, u2)
        yield
        q_eff = each(lambda q, e, x: (q * e - fold_heads(x[:, 0:LANES])).astype(BF16), qs, egc, qwu)
        m_neg = each(lambda x: jnp.where(same_head, -x[:, 0:LANES], 0.0).astype(BF16), kwu)
        o_loc = each(lambda x: fold_heads(x[:, LANES:2 * LANES]), qwu)
        s_loc = each(lambda x: jnp.where(same_head, x[:, LANES:2 * LANES], 0.0), kwu)
        egl = each(lambda g: jnp.broadcast_to(jnp.exp(g), (SUBLANES, LANES)), gl)
        out.extend(zip(q_eff, m_neg, o_loc, s_loc, egl))

    def chunk_rows(chunk, rows_per_chunk):
        return pl.ds(pl.multiple_of(chunk * rows_per_chunk, rows_per_chunk), rows_per_chunk)

    ctx_chunks = nc // c
    per_group = chunks_per_iter
    n_groups = n_chunks // per_group

    def chunks_at(step):
        return jnp.where(step < ctx_chunks, step + n // c, step - ctx_chunks), n_chunks - 1 - step

    def run_group(local_group, scan_group, states):
        dirs, chunks, qs, ks, vs, betas, gcs = [], [], [], [], [], [], []
        if local_group is not None:
            for g in range(per_group):
                for d, chunk in enumerate(chunks_at(per_group * local_group + g)):
                    rows = chunk_rows(chunk, c)
                    dirs.append(d)
                    chunks.append(chunk)
                    qs.append(q_ref[rows, :])
                    ks.append(k_ref[rows, :])
                    vs.append(v_ref[rows, :])
                    betas.append(bb_ref[d, rows, :])
                    gcs.append(gb_ref[d, rows, :])
        scan_chunks, scan_in = [], []
        if scan_group is not None:
            for g in range(per_group):
                step_chunks = chunks_at(per_group * scan_group + g)
                scan_chunks.append(step_chunks)
                scan_in.append([(qe_ref[d, chunk_rows(ch, c), :], mp_ref[d, chunk_rows(ch, 2 * c), :],
                                 ou_ref[d, chunk_rows(ch, c), :], nn_ref[d, chunk_rows(ch, 2 * c), :],
                                 egl_ref[d, chunk_rows(ch, SUBLANES), :]) for d, ch in enumerate(step_chunks)])
        scan_out = []

        def scan_step(states):
            loaded = scan_in[len(scan_out)]
            res = [_dot(jnp.concatenate([ld[0], ld[1]], axis=0), s2.astype(BF16)) for ld, s2 in zip(loaded, states)]
            scan_out.append([r[0:c] + ld[2] for r, ld in zip(res, loaded)])
            return tuple(s2 * ld[4][0:1] + r[c:3 * c] + ld[3] for s2, ld, r in zip(states, loaded, res))

        local_out = []
        stages = local_stages(dirs, qs, ks, vs, betas, gcs, local_out) if local_group is not None else iter(())
        n_stages = 2 * _log2(c) + 1
        every = max(1, n_stages // per_group)
        for stage, _ in enumerate(stages):
            if scan_group is not None and stage % every == 0 and len(scan_out) < per_group:
                states = scan_step(states)
        while scan_group is not None and len(scan_out) < per_group:
            states = scan_step(states)
        for d, chunk, (q_eff, m_neg, o_loc, s_loc, egl) in zip(dirs, chunks, local_out):
            qe_ref[d, chunk_rows(chunk, c), :] = q_eff
            mp_ref[d, chunk_rows(chunk, 2 * c), :] = m_neg
            ou_ref[d, chunk_rows(chunk, c), :] = o_loc
            nn_ref[d, chunk_rows(chunk, 2 * c), :] = s_loc
            egl_ref[d, chunk_rows(chunk, SUBLANES), :] = egl
        for step_chunks, outs in zip(scan_chunks, scan_out):
            for d, ch in enumerate(step_chunks):
                o_ref[d, chunk_rows(ch, c), :] = outs[d]
        return states

    zero_state = jnp.zeros((2 * c, 2 * c), F32)
    states = run_group(0, None, (zero_state, zero_state))
    states = lax.fori_loop(1, n_groups, lambda j, st: run_group(j, j - 1, st), states)
    run_group(None, n_groups - 1, states)

    ng = ng_ref[...]
    step = 256
    for r in range(0, l, step):
        o = o_ref[0, r:r + step, :] + o_ref[1, r:r + step, :]
        ms = _half_sums(o * o, lane_lo) * (1.0 / GDN_DIM)
        gate = qkvg_ref[0, r:r + step, 3 * LANES:4 * LANES]
        out_ref[0, r:r + step, :] = (o * lax.rsqrt(ms + EPS) * ng * _silu(gate)).astype(out_ref.dtype)


def _dot_sel_lhs(sel_bf16, x):
    hi, lo = _split2(x)
    return _dot(sel_bf16, hi) + _dot(sel_bf16, lo)


def _gdn_call(qkvg, ba, conv_w, gate_params, ng, n, nc):
    b, l, _ = qkvg.shape
    pairs = GDN_HEADS // 2
    n_chunks = l // GDN_CHUNK
    kern = functools.partial(_gdn_kernel, n=n, nc=nc, chunks_per_iter=GDN_CHUNKS_PER_GROUP)
    return pl.pallas_call(
        kern,
        grid=(b, pairs),
        in_specs=[
            pl.BlockSpec((1, l, 4 * LANES), lambda i, p: (i, 0, p)),
            pl.BlockSpec((1, l, LANES), lambda i, p: (i, 0, 0)),
            pl.BlockSpec((3, 4 * LANES), lambda i, p: (0, p)),
            pl.BlockSpec((2, LANES), lambda i, p: (0, 0)),
            pl.BlockSpec((1, LANES), lambda i, p: (0, 0)),
        ],
        out_specs=pl.BlockSpec((1, l, LANES), lambda i, p: (i, 0, p)),
        out_shape=jax.ShapeDtypeStruct((b, l, pairs * LANES), MIXER_OUT_DTYPE),
        scratch_shapes=[
            pltpu.VMEM((l + 5 * SUBLANES, 3 * LANES), F32),
            pltpu.VMEM((l, LANES), F32),
            pltpu.VMEM((l, LANES), F32),
            pltpu.VMEM((l, LANES), F32),
            pltpu.VMEM((2, l, LANES), F32),
            pltpu.VMEM((2, l, LANES), F32),
            pltpu.VMEM((2, l, LANES), BF16),
            pltpu.VMEM((2, 2 * l, LANES), BF16),
            pltpu.VMEM((2, l, LANES), F32),
            pltpu.VMEM((2, 2 * l, LANES), F32),
            pltpu.VMEM((2, n_chunks * SUBLANES, LANES), F32),
            pltpu.VMEM((2, l, LANES), F32),
        ],
        compiler_params=_cparams(("arbitrary", "arbitrary")),
        name="gated_deltanet",
    )(qkvg, ba, conv_w, gate_params, ng)


def _diff_kernel(*refs, key_start, n_sub, lam_init, aliased):
    if aliased:
        refs = refs[1:]
    q_ref, k_ref, vt_ref, lam_ref, ng_ref, o_ref = refs
    lp = lam_ref[...]
    lam = (jnp.exp(jnp.sum(lp[0:1] * lp[1:2], axis=-1, keepdims=True))
           - jnp.exp(jnp.sum(lp[2:3] * lp[3:4], axis=-1, keepdims=True)) + lam_init)
    lane = lax.broadcasted_iota(jnp.int32, (1, LANES), 1)
    halves = (lane < DIFF_DIM, lane >= DIFF_DIM)
    ng = ng_ref[...]
    k = k_ref[0, key_start:, :]
    vt = vt_ref[0, :, key_start:]
    tq = q_ref.shape[1] // n_sub

    def scores_of(i):
        q = q_ref[0, i * tq:(i + 1) * tq, :]
        return [_dot_nt(k, jnp.where(m, q, jnp.zeros_like(q))) for m in halves]

    ahead = scores_of(0)
    for i in range(n_sub):
        s = ahead
        if i + 1 < n_sub:
            ahead = scores_of(i + 1)
        e = [jnp.exp(x - jnp.max(x, axis=0, keepdims=True)) for x in s]
        pv = [_dot(vt, x.astype(BF16)) for x in e]
        parts = [x * (1.0 / jnp.sum(y, axis=0, keepdims=True)) for x, y in zip(pv, e)]
        ot = parts[0] - lam * parts[1]
        ot = ot * lax.rsqrt(jnp.mean(ot * ot, axis=0, keepdims=True) + EPS)
        o_ref[0, i * tq:(i + 1) * tq, :] = (ot.T * ng * (1.0 - lam_init)).astype(o_ref.dtype)


def _diff_call(dq, dk, dvt, lam_p, ng, lam_init, q_rows, first_block, n_q_blocks, key_start, n_sub, prev_out):
    b, l, _ = dq.shape
    aliased = prev_out is not None
    kern = functools.partial(_diff_kernel, key_start=key_start, n_sub=n_sub, lam_init=lam_init, aliased=aliased)
    row_of = lambda t: first_block + t
    in_specs = [
        pl.BlockSpec((1, q_rows, LANES), lambda i, h, t: (i, row_of(t), h)),
        pl.BlockSpec((1, l, LANES), lambda i, h, t: (i, 0, h)),
        pl.BlockSpec((1, LANES, l), lambda i, h, t: (i, h, 0)),
        pl.BlockSpec((4, DIFF_DIM), lambda i, h, t: (0, 0)),
        pl.BlockSpec((1, LANES), lambda i, h, t: (0, 0)),
    ]
    args = [dq, dk, dvt, lam_p, ng.reshape(1, LANES)]
    aliases = {}
    if aliased:
        in_specs = [pl.BlockSpec(memory_space=pl.ANY)] + in_specs
        args = [prev_out] + args
        aliases = {0: 0}
    return pl.pallas_call(
        kern,
        grid=(b, DIFF_HEADS, n_q_blocks),
        in_specs=in_specs,
        out_specs=pl.BlockSpec((1, q_rows, LANES), lambda i, h, t: (i, row_of(t), h)),
        out_shape=jax.ShapeDtypeStruct((b, l, DIFF_HEADS * LANES), MIXER_OUT_DTYPE),
        input_output_aliases=aliases,
        compiler_params=_cparams(("arbitrary", "arbitrary", "arbitrary")),
        name="diff_attention_ctx" if aliased else "diff_attention",
    )(*args)


def _swa_kernel(q_ref, k_ref, v_ref, sink_ref, o_ref, *, n, nc):
    t = pl.program_id(1)
    blk = SWA_BLOCK
    n_x = n // blk
    q = q_ref[0]
    lane = lax.broadcasted_iota(jnp.int32, (1, LANES), 1)
    lane_lo = lane < HEAD_DIM
    sink = sink_ref[...]
    group = SWA_HEADS // SWA_KV_HEADS

    def run(keys, vals, valid):
        head_of_row = lax.broadcasted_iota(jnp.int32, (group * blk, 1), 0) >> _log2(blk)
        kvs = range(SWA_KV_HEADS)
        kk = [keys[:, kvh * LANES:(kvh + 1) * LANES] for kvh in kvs]
        vv = [vals[:, kvh * LANES:(kvh + 1) * LANES] for kvh in kvs]
        qst, sk = [], []
        for kvh in kvs:
            q_rows = []
            sk_rows = jnp.zeros((group * blk, 1), F32)
            for g in range(group):
                h = kvh * group + g
                qp = q[:, (h // 2) * LANES:(h // 2 + 1) * LANES]
                q_rows.append(jnp.where(lane_lo if h % 2 == 0 else jnp.logical_not(lane_lo), qp, jnp.zeros_like(qp)))
                sk_rows = jnp.where(head_of_row == g, sink[:, h:h + 1], sk_rows)
            qst.append(jnp.concatenate(q_rows, axis=0))
            sk.append(sk_rows)
        s = [_dot_nt(x, y) for x, y in zip(qst, kk)]
        if valid is not None:
            s = [jnp.where(valid, x, NEG_INF) for x in s]
        mx = [jnp.maximum(jnp.max(x, axis=-1, keepdims=True), y) for x, y in zip(s, sk)]
        e = [jnp.exp(x - m) for x, m in zip(s, mx)]
        pv = [_dot(x.astype(BF16), y) for x, y in zip(e, vv)]
        den = [jnp.sum(x, axis=-1, keepdims=True) + jnp.exp(y - m) for x, y, m in zip(e, sk, mx)]
        outs = []
        for o, dn in zip(pv, den):
            o = o * (1.0 / dn)
            for g in range(0, group, 2):
                outs.append(jnp.where(lane_lo, o[g * blk:(g + 1) * blk], o[(g + 1) * blk:(g + 2) * blk]))
        o_ref[0] = jnp.concatenate(outs, axis=1).astype(o_ref.dtype)

    @pl.when(t < n_x)
    def _():
        start = pl.multiple_of(jnp.clip((t - 1) * blk, 0, n - 3 * blk), blk)
        keys = jnp.concatenate([k_ref[0, pl.ds(start, 3 * blk), :], k_ref[0, n:n + nc, :]], axis=0)
        vals = jnp.concatenate([v_ref[0, pl.ds(start, 3 * blk), :], v_ref[0, n:n + nc, :]], axis=0)
        shape = (group * blk, 3 * blk + nc)
        qpos = t * blk + (lax.broadcasted_iota(jnp.int32, shape, 0) & (blk - 1))
        col = lax.broadcasted_iota(jnp.int32, shape, 1)
        dist = qpos - (start + col)
        in_window = jnp.logical_and(dist <= SWA_WINDOW, dist >= -SWA_WINDOW)
        valid = jnp.logical_or(col >= 3 * blk, in_window)
        run(keys, vals, valid)

    @pl.when(t >= n_x)
    def _():
        run(k_ref[0, n:n + nc, :], v_ref[0, n:n + nc, :], None)


def _swa_call(q, k, v, sink, n, nc, with_ctx):
    b, l, _ = q.shape
    blk = SWA_BLOCK
    nt = (l if with_ctx else n) // blk
    kern = functools.partial(_swa_kernel, n=n, nc=nc)
    return pl.pallas_call(
        kern,
        grid=(b, nt),
        in_specs=[
            pl.BlockSpec((1, blk, SWA_HEADS * HEAD_DIM), lambda i, t: (i, t, 0)),
            pl.BlockSpec((1, l, 2 * LANES), lambda i, t: (i, 0, 0)),
            pl.BlockSpec((1, l, 2 * LANES), lambda i, t: (i, 0, 0)),
            pl.BlockSpec((1, LANES), lambda i, t: (0, 0)),
        ],
        out_specs=pl.BlockSpec((1, blk, SWA_HEADS * HEAD_DIM), lambda i, t: (i, t, 0)),
        out_shape=jax.ShapeDtypeStruct((b, l, SWA_HEADS * HEAD_DIM), MIXER_OUT_DTYPE),
        compiler_params=_cparams(("arbitrary", "arbitrary")),
        name="window_attention",
    )(q, k, v, sink)


def _dft_tables(n):
    h = n // 2
    r = 1 << (_log2(h) // 2)
    j = jnp.arange(h, dtype=jnp.int32)

    def tables(m):
        thin = lambda k: ((k[:, None] * m[None, :]) % (2 * n)).astype(F32) * (math.pi / n)
        a = thin(r * jnp.arange(h // r, dtype=jnp.int32))[:, None, :]
        b = thin(jnp.arange(r, dtype=jnp.int32))[None, :, :]
        cos = jnp.cos(a) * jnp.cos(b) - jnp.sin(a) * jnp.sin(b)
        sin = jnp.sin(a) * jnp.cos(b) + jnp.cos(a) * jnp.sin(b)
        return cos.reshape(h, h).astype(BF16), (-sin).reshape(h, h).astype(BF16)

    ce, se = tables(2 * j)
    co, so = tables(2 * j + 1)
    return ce, se, co, so, co.T, so.T


def _hyena_filter_kernel(feat_ref, w1_ref, b1_ref, w2_ref, b2_ref, freq_ref, w3f_ref, w3b_ref, dl_ref,
                         ce_ref, se_ref, co_ref, so_ref, ka_ref, kb_ref, km_ref):
    h = feat_ref.shape[1]
    assert h % 2 == 0
    freq = freq_ref[...]
    dl = dl_ref[...]
    row = lax.broadcasted_iota(jnp.int32, (h, 1), 0)

    def taps(part, w3_ref):
        feat = feat_ref[part]
        x = jnp.sin(freq[0:1] * (_dot_f32(feat, w1_ref[...]) + b1_ref[...]))
        x = jnp.sin(freq[1:2] * (_dot_f32(x, w2_ref[...]) + b2_ref[...]))
        return _dot_f32(x, w3_ref[...]) * jnp.exp(-feat[:, 0:1] * dl)

    fe, fo = taps(0, w3f_ref), taps(1, w3f_ref)
    be, bo = jnp.where(row == 0, 0.0, taps(2, w3b_ref)), taps(3, w3b_ref)
    ss = sum(jnp.sum(x * x, axis=0, keepdims=True) for x in (fe, fo, be, bo))
    sc = lax.rsqrt(ss + EPS)
    fe, fo, be, bo = (x * sc for x in (fe, fo, be, bo))
    sgn = jnp.where((row & 1) == 0, 1.0, -1.0)

    def dot2(t_ref, x):
        hi, lo = _split2(x)
        return _dot(t_ref[...], hi) + _dot(t_ref[...], lo)

    def bins(x_even, x_odd):
        ce, co = dot2(ce_ref, x_even), dot2(co_ref, x_odd)
        se, so = dot2(se_ref, x_even), dot2(so_ref, x_odd)
        return ce + co, se + so, ce - co, so - se

    f = bins(fe, fo)
    g = bins(be, bo)
    ka_ref[0, 0], ka_ref[0, 1], kb_ref[0, 0], kb_ref[0, 1] = (x + sgn * y for x, y in zip(f, g))
    mid_r = jnp.sum((fe + be) * sgn, axis=0, keepdims=True)
    mid_i = -jnp.sum((fo + bo) * sgn, axis=0, keepdims=True)
    km_ref[0] = jnp.concatenate([mid_r, mid_i, jnp.zeros((SUBLANES - 2, mid_r.shape[-1]), F32)], axis=0)


def _hyena_filter_call(feats, w1, b1, w2, b2, freq, w3, deltas, tables):
    h = feats.shape[1]
    hid = w2.shape[0]
    ch = deltas.shape[-1]
    tc = MXU_WIDTH
    nct = ch // tc
    const = lambda shape: pl.BlockSpec(shape, lambda o, j: (0,) * len(shape))
    spectrum = pl.BlockSpec((1, 2, h, tc), lambda o, j: (o, 0, 0, j))
    return pl.pallas_call(
        _hyena_filter_kernel,
        grid=(2, nct),
        in_specs=[
            const((4, h, hid)), const((hid, hid)), const((1, hid)), const((hid, hid)), const((1, hid)),
            const((2, hid)),
            pl.BlockSpec((hid, tc), lambda o, j: (0, (2 * o) * nct + j)),
            pl.BlockSpec((hid, tc), lambda o, j: (0, (2 * o + 1) * nct + j)),
            pl.BlockSpec((1, tc), lambda o, j: (0, j)),
        ] + [_resident((h, h))] * 4,
        out_specs=[spectrum, spectrum, pl.BlockSpec((1, SUBLANES, tc), lambda o, j: (o, 0, j))],
        out_shape=[
            jax.ShapeDtypeStruct((2, 2, h, ch), F32),
            jax.ShapeDtypeStruct((2, 2, h, ch), F32),
            jax.ShapeDtypeStruct((2, SUBLANES, ch), F32),
        ],
        compiler_params=_cparams(("arbitrary", "arbitrary")),
        name="hyena_filters",
    )(feats, w1, b1, w2, b2, freq, w3, w3, deltas, *tables[:4])


def _hyena_kernel(*refs, n, aliased):
    if aliased:
        refs = refs[1:]
    (v_ref, x1_ref, x2_ref, cwv_ref, cw1_ref, cw2_ref, ka_ref, kb_ref, km_ref, bias_ref,
     ce_ref, se_ref, co_ref, so_ref, cot_ref, sot_ref, o_ref, pad_ref, z_ref, zb_ref, p_ref, y_ref) = refs
    halo = SUBLANES
    tc = o_ref.shape[-1]
    h = n // 2
    rc = min(h, HY_ROW_CHUNK)
    lane_groups = tc // LANES
    zero_rows = jnp.zeros((halo, LANES), F32)
    for g in range(lane_groups):
        pad_ref[g, 0:halo, :] = zero_rows
        pad_ref[g, halo + n:2 * halo + n, :] = zero_rows

    def stage(ref):
        for r in range(0, n, 2 * rc):
            for g in range(lane_groups):
                pad_ref[g, halo + r:halo + r + 2 * rc, :] = ref[0, r:r + 2 * rc, g * LANES:(g + 1) * LANES]

    def conv_rows(cw, parity, r):
        first = halo + 2 * r + parity - 1
        taps = [jnp.concatenate([pad_ref[g, pl.ds(first + i, rc, stride=2), :] for g in range(lane_groups)], axis=1)
                for i in range(3)]
        return cw[0:1] * taps[0] + cw[1:2] * taps[1] + cw[2:3] * taps[2]

    def alt_sign(r):
        j = r + lax.broadcasted_iota(jnp.int32, (rc, 1), 0)
        return j, jnp.where((j & 1) == 0, 1.0, -1.0)

    stage(v_ref)
    cw = cwv_ref[...]
    for parity in range(2):
        for r in range(0, h, rc):
            z = conv_rows(cw, parity, r)
            z_ref[parity, r:r + rc, :] = z
            zb_ref[parity, r:r + rc, :] = z.astype(BF16)

    for o, (gate_ref, gate_cw_ref) in enumerate(((x1_ref, cw1_ref), (x2_ref, cw2_ref))):
        mid_r = jnp.zeros((1, tc), F32)
        mid_i = jnp.zeros((1, tc), F32)
        for r in range(0, h, rc):
            _, sgn = alt_sign(r)
            mid_r = mid_r + jnp.sum(z_ref[0, r:r + rc, :] * sgn, axis=0, keepdims=True)
            mid_i = mid_i - jnp.sum(z_ref[1, r:r + rc, :] * sgn, axis=0, keepdims=True)
        km_r = km_ref[o, 0:1, :]
        km_i = km_ref[o, 1:2, :]
        pm_r = (mid_r * km_r - mid_i * km_i) * (1.0 / n)
        pm_i = (mid_r * km_i + mid_i * km_r) * (1.0 / n)
        ze = zb_ref[0]
        zo = zb_ref[1]
        for r in range(0, h, rc):
            k, _ = alt_sign(r)
            rows = slice(r, r + rc)
            ce, co = _dot(ce_ref[rows, :], ze), _dot(co_ref[rows, :], zo)
            se, so = _dot(se_ref[rows, :], ze), _dot(so_ref[rows, :], zo)
            xa_r, xb_r, xa_i, xb_i = ce + co, ce - co, se + so, so - se
            wgt = jnp.where(k == 0, 0.5 / n, 1.0 / n)
            ka_r, ka_i = ka_ref[o, 0, rows, :], ka_ref[o, 1, rows, :]
            kb_r, kb_i = kb_ref[o, 0, rows, :], kb_ref[o, 1, rows, :]
            pa_r = (xa_r * ka_r - xa_i * ka_i) * wgt
            pa_i = (xa_r * ka_i + xa_i * ka_r) * wgt
            pb_r = (xb_r * kb_r - xb_i * kb_i) * wgt
            pb_i = (xb_r * kb_i + xb_i * kb_r) * wgt
            p_ref[0, rows, :] = (pa_r + pb_r).astype(BF16)
            p_ref[1, rows, :] = (pa_i - pb_i).astype(BF16)
            p_ref[2, rows, :] = (pa_r - pb_r).astype(BF16)
            p_ref[3, rows, :] = (pa_i + pb_i).astype(BF16)
        stage(gate_ref)
        cw = gate_cw_ref[...]
        bias = bias_ref[o:o + 1, :]
        for parity, (c_ref, s_ref, mid) in enumerate(((ce_ref, se_ref, pm_r), (cot_ref, sot_ref, -pm_i))):
            for r in range(0, h, rc):
                _, sgn = alt_sign(r)
                rows = slice(r, r + rc)
                y = (_dot(c_ref[rows, :], p_ref[2 * parity]) + _dot(s_ref[rows, :], p_ref[2 * parity + 1])
                     + sgn * mid)
                z = conv_rows(cw, parity, r) * (y + z_ref[parity, rows, :] * bias)
                if o == 0:
                    z_ref[parity, rows, :] = z
                    zb_ref[parity, rows, :] = z.astype(BF16)
                else:
                    for g in range(lane_groups):
                        y_ref[g, pl.ds(2 * r + parity, rc, stride=2), :] = z[:, g * LANES:(g + 1) * LANES]
    for r in range(0, n, 2 * rc):
        rows = slice(r, r + 2 * rc)
        o_ref[0, rows, :] = jnp.concatenate([y_ref[g, rows, :] for g in range(lane_groups)],
                                            axis=1).astype(o_ref.dtype)


def _hyena_call(u, conv_w, ka, kb, km, bias, tables, n, row_block, prev_out):
    b, l, _ = u.shape
    ch = bias.shape[-1]
    tc = MXU_WIDTH
    nct = ch // tc
    h = n // 2
    aliased = prev_out is not None
    kern = functools.partial(_hyena_kernel, n=n, aliased=aliased)
    once = pl.Buffered(1)
    in_specs = [
        pl.BlockSpec((1, n, tc), lambda j, i: (i, row_block, j)),
        pl.BlockSpec((1, n, tc), lambda j, i: (i, row_block, nct + j)),
        pl.BlockSpec((1, n, tc), lambda j, i: (i, row_block, 2 * nct + j)),
        pl.BlockSpec((3, tc), lambda j, i: (0, j)),
        pl.BlockSpec((3, tc), lambda j, i: (0, nct + j)),
        pl.BlockSpec((3, tc), lambda j, i: (0, 2 * nct + j)),
        pl.BlockSpec((2, 2, h, tc), lambda j, i: (0, 0, 0, j), pipeline_mode=once),
        pl.BlockSpec((2, 2, h, tc), lambda j, i: (0, 0, 0, j), pipeline_mode=once),
        pl.BlockSpec((2, SUBLANES, tc), lambda j, i: (0, 0, j)),
        pl.BlockSpec((2, tc), lambda j, i: (0, j)),
    ] + [_resident((h, h))] * 6
    args = [u, u, u, conv_w, conv_w, conv_w, ka, kb, km, bias, *tables]
    aliases = {}
    if aliased:
        in_specs = [pl.BlockSpec(memory_space=pl.ANY)] + in_specs
        args = [prev_out] + args
        aliases = {0: 0}
    return pl.pallas_call(
        kern,
        grid=(nct, b),
        in_specs=in_specs,
        out_specs=pl.BlockSpec((1, n, tc), lambda j, i: (i, row_block, j)),
        out_shape=jax.ShapeDtypeStruct((b, l, ch), MIXER_OUT_DTYPE),
        scratch_shapes=[
            pltpu.VMEM((tc // LANES, n + 2 * SUBLANES, LANES), F32),
            pltpu.VMEM((2, h, tc), F32),
            pltpu.VMEM((2, h, tc), BF16),
            pltpu.VMEM((4, h, tc), BF16),
            pltpu.VMEM((tc // LANES, n, LANES), F32),
        ],
        input_output_aliases=aliases,
        compiler_params=_cparams(("arbitrary", "arbitrary")),
        name="hyena_conv_n%d" % n,
    )(*args)


def _rope_tables(n, nc):
    rows = n // GRID_W
    row = jnp.repeat(jnp.arange(rows, dtype=F32), GRID_W)
    col = jnp.tile(jnp.arange(GRID_W, dtype=F32), rows)
    half = HEAD_DIM // 2
    inv = ROPE_BASE ** (-jnp.arange(0, half, 2, dtype=F32) / half)
    ar = row[:, None] * inv
    ac = col[:, None] * inv
    cos = jnp.concatenate([jnp.cos(ar), jnp.cos(ar), jnp.cos(ac), jnp.cos(ac)], axis=-1)
    sin = jnp.concatenate([-jnp.sin(ar), jnp.sin(ar), -jnp.sin(ac), jnp.sin(ac)], axis=-1)
    cos = jnp.concatenate([cos, jnp.ones((nc, HEAD_DIM), F32)], axis=0)
    sin = jnp.concatenate([sin, jnp.zeros((nc, HEAD_DIM), F32)], axis=0)
    return jnp.tile(cos, (1, LANES // HEAD_DIM)), jnp.tile(sin, (1, LANES // HEAD_DIM))


def _rope_partner_cols(width):
    d = np.arange(width)
    quarter = HEAD_DIM // 4
    return np.where((d % (2 * quarter)) < quarter, d + quarter, d - quarter)


def _hyena_feats(n):
    pos = jnp.arange(n, dtype=F32)
    t = pos / max(n - 1, 1)
    ang = (2.0 * math.pi * pos / n)[:, None] * jnp.linspace(1e-4, HY_BANDS - 1, HY_BANDS, dtype=F32)[None, :]
    feats = jnp.concatenate([t[:, None], jnp.cos(ang), -jnp.sin(ang)], axis=-1)
    feats = jnp.pad(feats, ((0, 0), (0, 64 - feats.shape[-1])))
    back = jnp.concatenate([feats[0:1], jnp.flip(feats[1:], axis=0)], axis=0)
    return jnp.stack([feats[0::2], feats[1::2], back[0::2], back[1::2]])


def _pad_cols(w, width):
    return jnp.pad(w, ((0, 0), (0, width - w.shape[-1])))


def _layer_ab(xz, mod, norm_g0, w_in, conv_w, a_log, dt_bias, gdn_g, lam_p, diff_g, lam_init, rope, n, nc):
    hd = GDN_HEADS * GDN_DIM
    wq, wk, wv, wg = (w_in[:, i * hd:(i + 1) * hd] for i in range(4))
    o = 4 * hd
    w_beta, w_alpha = w_in[:, o:o + 16], w_in[:, o + 16:o + 32]
    o += 32
    dd = DIFF_HEADS * 2 * DIFF_DIM
    wdq, wdk, wdv = (w_in[:, o + i * dd:o + (i + 1) * dd] for i in range(3))
    pairs = GDN_HEADS // 2
    pair_cols = lambda w: [w[:, p * LANES:(p + 1) * LANES] for p in range(pairs)]
    w_qkvg = jnp.concatenate([blk for grp in zip(pair_cols(wq), pair_cols(wk), pair_cols(wv), pair_cols(wg))
                              for blk in grp], axis=1)
    perm = _rope_partner_cols(dd)
    w_all = jnp.concatenate([w_qkvg, _pad_cols(jnp.concatenate([w_beta, w_alpha], axis=1), LANES),
                             wdq, wdk, wdv, wdq[:, perm], wdk[:, perm]], axis=1).astype(BF16)
    c0 = 4 * hd
    c1 = c0 + LANES
    segs = (_Seg(0, c0), _Seg(c0, LANES),
            _Seg(c1, dd, rot_start=c1 + 3 * dd, scale=DIFF_DIM ** -0.5, dtype=BF16),
            _Seg(c1 + dd, dd, rot_start=c1 + 4 * dd, dtype=BF16),
            _Seg(c1 + 2 * dd, dd, dtype=BF16, transposed=True))
    qkvg, ba, dq, dk, dvt = _proj_call(xz, mod, norm_g0, w_all, rope[0], rope[1], segs, n // ROW_TILE, "proj_ab")

    cq, ck, cv = (conv_w[:, i * hd:(i + 1) * hd] for i in range(3))
    zeros = jnp.zeros((3, LANES), F32)
    conv_l = jnp.concatenate([blk for p in range(pairs) for blk in
                              (cq[:, p * LANES:(p + 1) * LANES], ck[:, p * LANES:(p + 1) * LANES],
                               cv[:, p * LANES:(p + 1) * LANES], zeros)], axis=1)
    n_gate = 2 * GDN_HEADS
    on_decay_lanes = lambda t: jnp.pad(t.reshape(1, n_gate), ((0, 0), (n_gate, LANES - 2 * n_gate)))
    gate_params = jnp.concatenate([on_decay_lanes(a_log), on_decay_lanes(dt_bias)], axis=0)
    ng = jnp.tile(gdn_g.reshape(1, GDN_DIM), (1, 2))
    oa = _gdn_call(qkvg, ba, conv_l, gate_params, ng, n, nc)
    q_rows = DIFF_SUB_TILES * ROW_TILE
    ob = _diff_call(dq, dk, dvt, lam_p, diff_g, lam_init, q_rows, 0, n // q_rows, 0, DIFF_SUB_TILES, None)
    ob = _diff_call(dq, dk, dvt, lam_p, diff_g, lam_init, nc, n // nc, 1, n, 1, ob)
    return oa, ob


def _layer_cd(xz, mod, norm_g0, w_in, sink, hy_conv, hy_w1, hy_b1, hy_w2, hy_b2, hy_w3, hy_freq, hy_bias,
              rope, n, nc, last, dft_x, dft_c):
    qd = SWA_HEADS * HEAD_DIM
    kd = SWA_KV_HEADS * HEAD_DIM
    wq, wk, wv, wu = w_in[:, 0:qd], w_in[:, qd:qd + kd], w_in[:, qd + kd:qd + 2 * kd], w_in[:, qd + 2 * kd:]
    dup = lambda w: jnp.concatenate([w[:, 0:HEAD_DIM], w[:, 0:HEAD_DIM], w[:, HEAD_DIM:], w[:, HEAD_DIM:]], axis=1)
    wk2, wv2 = dup(wk), dup(wv)
    ud = wu.shape[1]
    w_all = jnp.concatenate([wq, wk2, wv2, wu, wq[:, _rope_partner_cols(qd)], wk2[:, _rope_partner_cols(2 * kd)]],
                            axis=1).astype(BF16)
    o_u = qd + 4 * kd
    segs = (_Seg(0, qd, rot_start=o_u + ud, scale=HEAD_DIM ** -0.5, dtype=BF16),
            _Seg(qd, 2 * kd, rot_start=o_u + ud + qd, dtype=BF16),
            _Seg(qd + 2 * kd, 2 * kd, dtype=BF16), _Seg(o_u, ud))
    q, k, v, u = _proj_call(xz, mod, norm_g0, w_all, rope[0], rope[1], segs, n // ROW_TILE, "proj_cd")
    oc = _swa_call(q, k, v, _pad_cols(sink.reshape(1, SWA_HEADS), LANES), n, nc, not last)

    ch = hy_bias.shape[-1]
    deltas = jnp.abs(jnp.linspace(HY_MIN_DECAY, HY_MAX_DECAY, ch, dtype=F32)).reshape(1, ch)
    hid = hy_w2.shape[0]
    w1p = jnp.pad(hy_w1, ((0, hid - hy_w1.shape[0]), (0, 0)))
    filt = lambda m, dft: _hyena_filter_call(_hyena_feats(m), w1p, hy_b1.reshape(1, hid), hy_w2,
                                             hy_b2.reshape(1, hid), hy_freq, hy_w3, deltas, dft)
    od = _hyena_call(u, hy_conv, *filt(n, dft_x), hy_bias, dft_x, n, 0, None)
    if not last:
        od = _hyena_call(u, hy_conv, *filt(nc, dft_c), hy_bias, dft_c, nc, n // nc, od)
    return oc, od


def kernel(x, c, ctx, c_ctx, w_mod, b_mod, norm_g, ffn_w_up, ffn_conv, ffn_w_down, ab_w_in, ab_w_out, gdn_conv, gdn_a_log, gdn_dt_bias, gdn_norm_g, diff_lambda, diff_norm_g, cd_w_in, cd_w_out, swa_sink, hy_conv, hy_w1, hy_b1, hy_w2, hy_b2, hy_w3, hy_freq, hy_bias):
    b, n, d = x.shape
    nc = ctx.shape[1]
    depth = w_mod.shape[0]
    assert n % ROW_TILE == 0 and nc == ROW_TILE and n % GRID_W == 0
    xz = jnp.concatenate([x, ctx], axis=1)
    rows = -(-(b + 1) // SUBLANES) * SUBLANES
    cc = jnp.concatenate([c, c_ctx[None], jnp.zeros((rows - b - 1, d), F32)], axis=0)
    mods = _mod_call(cc, w_mod, b_mod)
    mod_all = jnp.concatenate([mods[:, :b].reshape(depth, b, 1, 6, d),
                               jnp.broadcast_to(mods[:, b].reshape(depth, 1, 1, 6, d), (depth, b, 1, 6, d))], axis=2)
    rope = _rope_tables(n, nc)
    dft_x = _dft_tables(n)
    dft_c = _dft_tables(nc)
    n_x_tiles = n // ROW_TILE
    for l in range(depth):
        last = l == depth - 1
        i = l // 2
        mod = mod_all[l]
        if l % 2 == 0:
            lam_init = 0.8 - 0.6 * math.exp(-0.3 * l)
            o1, o2 = _layer_ab(xz, mod, norm_g[l, 0], ab_w_in[i], gdn_conv[i], gdn_a_log[i], gdn_dt_bias[i],
                               gdn_norm_g[i], diff_lambda[i], diff_norm_g[i], lam_init, rope, n, nc)
            w_out = ab_w_out[i]
        else:
            o1, o2 = _layer_cd(xz, mod, norm_g[l, 0], cd_w_in[i], swa_sink[i], hy_conv[i], hy_w1[i], hy_b1[i],
                               hy_w2[i], hy_b2[i], hy_w3[i], hy_freq[i], hy_bias[i], rope, n, nc, last, dft_x, dft_c)
            w_out = cd_w_out[i]
        n_tiles = (n if last else n + nc) // ROW_TILE
        xz = _post_call(o1, o2, xz, mod, norm_g[l, 1], norm_g[l, 2], norm_g[l, 3], w_out.astype(BF16),
                        ffn_w_up[l].astype(BF16), ffn_conv[l], ffn_w_down[l].astype(BF16), n_tiles, n_x_tiles)
    return xz
```

```python
import functools
import math
from typing import NamedTuple, Optional

import jax
import jax.numpy as jnp
import numpy as np
from jax import lax
from jax.experimental import pallas as pl
from jax.experimental.pallas import tpu as pltpu

F32 = jnp.float32
BF16 = jnp.bfloat16
MIXER_OUT_DTYPE = BF16

EPS = 1e-6
NEG_INF = -1e30
GRID_W = 64
HEAD_DIM = 64
ROPE_BASE = 10000.0
GDN_HEADS = 8
GDN_DIM = 64
GDN_CHUNK = 64
GDN_CHUNKS_PER_GROUP = 12
DIFF_HEADS = 4
DIFF_DIM = 64
DIFF_SUB_TILES = 8
SWA_HEADS = 8
SWA_KV_HEADS = 2
SWA_WINDOW = 128
SWA_BLOCK = 128
HY_BANDS = 16
HY_MIN_DECAY = math.log(1e-2) / 1.5
HY_MAX_DECAY = math.log(1e-2) / 0.3
HY_ROW_CHUNK = 512

LANES = 128
SUBLANES = 8
MXU_WIDTH = 256
FFN_COL_CHUNK = 11 * MXU_WIDTH
ROW_TILE = 256
VMEM_LIMIT = 56 * 1024 * 1024


def _cparams(sem):
    return pltpu.CompilerParams(dimension_semantics=sem, vmem_limit_bytes=VMEM_LIMIT)


def _resident(shape):
    zeros = (0,) * len(shape)
    return pl.BlockSpec(shape, lambda *_: zeros, pipeline_mode=pl.Buffered(1))


def _log2(v):
    assert v & (v - 1) == 0
    return v.bit_length() - 1


def _sigmoid(x):
    return 1.0 / (1.0 + jnp.exp(-x))


def _silu(x):
    return x * _sigmoid(x)


def _softplus(x):
    return jnp.maximum(x, 0.0) + jnp.log1p(jnp.exp(-jnp.abs(x)))


def _dot(a, b):
    return jnp.dot(a, b, preferred_element_type=F32)


def _dot_nt(a, b):
    return lax.dot_general(a, b, (((1,), (1,)), ((), ())), preferred_element_type=F32)


def _dot_tn(a, b):
    return lax.dot_general(a, b, (((0,), (0,)), ((), ())), preferred_element_type=F32)


def _dot_f32(a, b):
    return jnp.dot(a, b, preferred_element_type=F32, precision=lax.Precision.HIGHEST)


def _split2(x):
    hi = x.astype(BF16)
    lo = (x - hi.astype(F32)).astype(BF16)
    return hi, lo


def _dot_sel(x, sel_bf16):
    hi, lo = _split2(x)
    return _dot(hi, sel_bf16) + _dot(lo, sel_bf16)


def _mm(a, b):
    return _dot(a.astype(BF16), b.astype(BF16))


def _rms(y, g):
    return y * lax.rsqrt(jnp.mean(y * y, axis=-1, keepdims=True) + EPS) * g


def _modnorm(x, g, shift, scale):
    return _rms(x, g) * (1.0 + scale) + shift


def _mod_kernel(cc_ref, w_ref, b_ref, o_ref):
    s = _silu(cc_ref[...])
    o_ref[0] = _dot(s.astype(BF16), w_ref[0].astype(BF16)) + b_ref[0]


def _mod_call(cc, w_mod, b_mod):
    depth, d, nm = w_mod.shape
    rows = cc.shape[0]
    ct = 1536
    return pl.pallas_call(
        _mod_kernel,
        grid=(depth, nm // ct),
        in_specs=[
            pl.BlockSpec((rows, d), lambda l, j: (0, 0)),
            pl.BlockSpec((1, d, ct), lambda l, j: (l, 0, j)),
            pl.BlockSpec((1, 1, ct), lambda l, j: (l, 0, j)),
        ],
        out_specs=pl.BlockSpec((1, rows, ct), lambda l, j: (l, 0, j)),
        out_shape=jax.ShapeDtypeStruct((depth, rows, nm), F32),
        compiler_params=_cparams(("arbitrary", "arbitrary")),
        name="adaln_mod",
    )(cc, w_mod, b_mod.reshape(depth, 1, nm))


class _Seg(NamedTuple):
    start: int
    width: int
    rot_start: Optional[int] = None
    scale: float = 1.0
    dtype: type = F32
    transposed: bool = False


def _proj_kernel(x_ref, mod_ref, g_ref, w_ref, cos_ref, sin_ref, *out_refs, segs):
    m = mod_ref[0, 0]
    h = _modnorm(x_ref[0], g_ref[...], m[0:1], m[1:2]).astype(BF16)
    for o_ref, seg in zip(out_refs, segs):
        y = _dot(h, w_ref[:, seg.start:seg.start + seg.width])
        if seg.rot_start is not None:
            yr = _dot(h, w_ref[:, seg.rot_start:seg.rot_start + seg.width])
            reps = seg.width // LANES
            cos = jnp.concatenate([cos_ref[...]] * reps, axis=1)
            sin = jnp.concatenate([sin_ref[...]] * reps, axis=1)
            y = y * cos + yr * sin
        if seg.scale != 1.0:
            y = y * seg.scale
        if seg.transposed:
            y = y.T
        o_ref[0] = y.astype(seg.dtype)


def _proj_call(xz, mod, g, w, cos_t, sin_t, segs, n_x_tiles, name):
    b, l, d = xz.shape
    tm = ROW_TILE
    nt = l // tm
    p = w.shape[1]
    return pl.pallas_call(
        functools.partial(_proj_kernel, segs=segs),
        grid=(nt, b),
        in_specs=[
            pl.BlockSpec((1, tm, d), lambda t, i: (i, t, 0)),
            pl.BlockSpec((1, 1, 6, d), lambda t, i: (i, t // n_x_tiles, 0, 0)),
            pl.BlockSpec((1, d), lambda t, i: (0, 0)),
            _resident((d, p)),
            pl.BlockSpec((tm, LANES), lambda t, i: (t, 0)),
            pl.BlockSpec((tm, LANES), lambda t, i: (t, 0)),
        ],
        out_specs=[pl.BlockSpec((1, s.width, tm), lambda t, i: (i, 0, t)) if s.transposed
                   else pl.BlockSpec((1, tm, s.width), lambda t, i: (i, t, 0)) for s in segs],
        out_shape=[jax.ShapeDtypeStruct((b, s.width, l) if s.transposed else (b, l, s.width), s.dtype)
                   for s in segs],
        compiler_params=_cparams(("arbitrary", "arbitrary")),
        name=name,
    )(xz, mod, g.reshape(1, d), w, cos_t, sin_t)


def _post_kernel(o1p_ref, o1_ref, o1n_ref, o2p_ref, o2_ref, o2n_ref, xp_ref, x_ref, xn_ref, mod_ref,
                 g1_ref, g2_ref, g3_ref, wout_ref, wup_ref, cw_ref, wdn_ref, out_ref, up_ref,
                 *, tm, n_x_tiles, n_tiles, cf, dff):
    t = pl.program_id(0)
    first = jnp.logical_or(t == 0, t == n_x_tiles)
    last = jnp.logical_or(t == n_x_tiles - 1, t == n_tiles - 1)
    m = mod_ref[0, 0]
    halo = SUBLANES
    ohalo = o1p_ref.shape[1]
    k1 = o1_ref.shape[-1]
    o1e = jnp.concatenate([o1p_ref[0], o1_ref[0], o1n_ref[0]], axis=0)
    o2e = jnp.concatenate([o2p_ref[0], o2_ref[0], o2n_ref[0]], axis=0)
    y = _dot(o1e, wout_ref[0:k1, :]) + _dot(o2e, wout_ref[k1:, :])
    y = y[ohalo - halo:ohalo + tm + halo]
    xe = jnp.concatenate([xp_ref[0], x_ref[0], xn_ref[0]], axis=0)
    x1 = xe + m[2:3] * _rms(y, g1_ref[...])
    h = _modnorm(x1, g2_ref[...], m[3:4], m[4:5]).astype(BF16)
    acc = jnp.zeros((tm, x_ref.shape[-1]), F32)
    for c0 in range(0, dff, cf):
        wd = min(cf, dff - c0)
        halves = []
        for half, base in enumerate((c0, dff + c0)):
            u = _dot(h, wup_ref[:, base:base + wd])
            up_ref[half, :, 0:wd] = u
            up_ref[half, 0:halo, 0:wd] = jnp.where(first, 0.0, u[0:halo])
            up_ref[half, tm + halo:tm + 2 * halo, 0:wd] = jnp.where(last, 0.0, u[tm + halo:])
            cw = cw_ref[:, base:base + wd]
            halves.append(cw[0:1] * up_ref[half, halo - 1:halo - 1 + tm, 0:wd]
                          + cw[1:2] * up_ref[half, halo:halo + tm, 0:wd]
                          + cw[2:3] * up_ref[half, halo + 1:halo + 1 + tm, 0:wd])
        act = (_silu(halves[1]) * halves[0]).astype(BF16)
        acc = acc + _dot(act, wdn_ref[c0:c0 + wd, :])
    out_ref[0] = x1[halo:halo + tm] + m[5:6] * _rms(acc, g3_ref[...])


def _post_call(o1, o2, xz, mod, g1, g2, g3, w_out, w_up, conv_w, w_down, n_tiles, n_x_tiles):
    b, _, d = xz.shape
    tm = ROW_TILE
    rows = n_tiles * tm
    dff = w_down.shape[0]
    cf = FFN_COL_CHUNK
    k1, k2 = o1.shape[-1], o2.shape[-1]
    ohalo = 2 * SUBLANES
    kern = functools.partial(_post_kernel, tm=tm, n_x_tiles=n_x_tiles, n_tiles=n_tiles, cf=cf, dff=dff)

    def with_halos(width, halo_rows):
        per_tile = tm // halo_rows
        n_blocks = rows // halo_rows
        return [
            pl.BlockSpec((1, halo_rows, width), lambda t, i: (i, jnp.maximum(t * per_tile - 1, 0), 0)),
            pl.BlockSpec((1, tm, width), lambda t, i: (i, t, 0)),
            pl.BlockSpec((1, halo_rows, width), lambda t, i: (i, jnp.minimum((t + 1) * per_tile, n_blocks - 1), 0)),
        ]

    row_vec = pl.BlockSpec((1, d), lambda t, i: (0, 0))
    return pl.pallas_call(
        kern,
        grid=(n_tiles, b),
        in_specs=with_halos(k1, ohalo) + with_halos(k2, ohalo) + with_halos(d, SUBLANES) + [
            pl.BlockSpec((1, 1, 6, d), lambda t, i: (i, t // n_x_tiles, 0, 0)),
            row_vec, row_vec, row_vec,
            _resident((k1 + k2, d)),
            _resident((d, 2 * dff)),
            pl.BlockSpec((3, 2 * dff), lambda t, i: (0, 0)),
            _resident((dff, d)),
        ],
        out_specs=pl.BlockSpec((1, tm, d), lambda t, i: (i, t, 0)),
        out_shape=jax.ShapeDtypeStruct((b, rows, d), F32),
        scratch_shapes=[pltpu.VMEM((2, tm + 2 * SUBLANES, cf), F32)],
        compiler_params=_cparams(("arbitrary", "arbitrary")),
        name="mixer_out_conv_ffn",
    )(o1, o1, o1, o2, o2, o2, xz, xz, xz, mod, g1.reshape(1, d), g2.reshape(1, d), g3.reshape(1, d),
      w_out, w_up, conv_w, w_down)


def _half_sums(x2, lane_lo):
    s0 = jnp.sum(jnp.where(lane_lo, x2, 0.0), axis=-1, keepdims=True)
    s1 = jnp.sum(jnp.where(lane_lo, 0.0, x2), axis=-1, keepdims=True)
    return jnp.where(lane_lo, s0, s1)


def _gdn_kernel(qkvg_ref, ba_ref, cw_ref, gp_ref, ng_ref, out_ref,
                pad_ref, q_ref, k_ref, v_ref, bb_ref, gb_ref, qe_ref, mp_ref, ou_ref, nn_ref, egl_ref, o_ref,
                *, n, nc, chunks_per_iter):
    l = n + nc
    c = GDN_CHUNK
    n_chunks = l // c
    pair = pl.program_id(1)
    halo = SUBLANES
    lane = lax.broadcasted_iota(jnp.int32, (1, LANES), 1)
    lane_lo = lane < GDN_DIM

    cw = cw_ref[:, 0:3 * LANES]
    zero_rows = jnp.zeros((halo, 3 * LANES), F32)
    for seq_start, seq_len in ((0, n), (n, nc)):
        base = halo + seq_start + (2 * halo if seq_start else 0)
        pad_ref[base - halo:base, :] = zero_rows
        pad_ref[base + seq_len:base + seq_len + halo, :] = zero_rows
        step = 256
        for r in range(0, seq_len, step):
            pad_ref[base + r:base + r + step, :] = qkvg_ref[0, seq_start + r:seq_start + r + step, 0:3 * LANES]
        for r in range(0, seq_len, step):
            y = (cw[0:1] * pad_ref[base + r - 1:base + r - 1 + step, :]
                 + cw[1:2] * pad_ref[base + r:base + r + step, :]
                 + cw[2:3] * pad_ref[base + r + 1:base + r + 1 + step, :])
            y = _silu(y)
            q = y[:, 0:LANES]
            k = y[:, LANES:2 * LANES]
            rows = slice(seq_start + r, seq_start + r + step)
            q_ref[rows, :] = q * lax.rsqrt(_half_sums(q * q, lane_lo) + EPS) * (GDN_DIM ** -0.5)
            k_ref[rows, :] = k * lax.rsqrt(_half_sums(k * k, lane_lo) + EPS)
            v_ref[rows, :] = y[:, 2 * LANES:3 * LANES]

    sel_r = lax.broadcasted_iota(jnp.int32, (LANES, 4 * LANES), 0)
    sel_c = lax.broadcasted_iota(jnp.int32, (LANES, 4 * LANES), 1)
    quarter = sel_c >> _log2(LANES)
    src_lane = (quarter & 1) * 2 * GDN_HEADS + (quarter >> 1) * GDN_HEADS + 2 * pair + ((sel_c >> _log2(GDN_DIM)) & 1)
    sel = (sel_r == src_lane).astype(BF16)
    gblk = 256
    bi = lax.broadcasted_iota(jnp.int32, (gblk, gblk), 0)
    bj = lax.broadcasted_iota(jnp.int32, (gblk, gblk), 1)
    same_chunk = (bi >> _log2(c)) == (bj >> _log2(c))
    csum = (jnp.logical_and(same_chunk, bi >= bj).astype(BF16), jnp.logical_and(same_chunk, bi <= bj).astype(BF16))
    neg_a = -jnp.exp(gp_ref[0:1, :])
    dt_bias = gp_ref[1:2, :]
    for r in range(0, l, gblk):
        ba = ba_ref[0, r:r + gblk, :]
        gates = jnp.where(lane < 2 * GDN_HEADS, _sigmoid(ba), neg_a * _softplus(ba + dt_bias))
        x = _dot_sel(gates, sel)
        for d in range(2):
            bb_ref[d, r:r + gblk, :] = x[:, 2 * d * LANES:(2 * d + 1) * LANES]
            gb_ref[d, r:r + gblk, :] = _dot_sel_lhs(csum[d], x[:, (2 * d + 1) * LANES:(2 * d + 2) * LANES])

    r2 = lax.broadcasted_iota(jnp.int32, (2 * c, 2 * c), 0)
    c2 = lax.broadcasted_iota(jnp.int32, (2 * c, 2 * c), 1)
    same_head = (r2 >= c) == (c2 >= c)
    eye = (r2 == c2).astype(F32)
    masks = ((jnp.logical_and(same_head, r2 >= c2), jnp.logical_and(same_head, r2 > c2)),
             (jnp.logical_and(same_head, r2 <= c2), jnp.logical_and(same_head, r2 < c2)))
    m0 = lane_lo.astype(F32)
    m1 = 1.0 - m0

    def pair_mask(lv, lower):
        same_block = (r2 >> (lv + 1)) == (c2 >> (lv + 1))
        r_hi = ((r2 >> lv) & 1) == 1
        c_hi = ((c2 >> lv) & 1) == 1
        off = jnp.logical_and(r_hi, jnp.logical_not(c_hi)) if lower else jnp.logical_and(c_hi, jnp.logical_not(r_hi))
        return jnp.logical_and(same_block, off)

    pair_masks = tuple(tuple(pair_mask(lv, lower) for lv in range(_log2(c))) for lower in (True, False))

    def stack_heads(x2):
        return jnp.concatenate([x2 * m0, x2 * m1], axis=0)

    def fold_heads(x):
        return x[0:c] + x[c:2 * c]

    def local_stages(dirs, qs, ks, vs, betas, gcs, out):
        each = lambda f, *cols: [f(*args) for args in zip(*cols)]
        incl = [masks[d][0] for d in dirs]
        strict = [masks[d][1] for d in dirs]
        g1 = each(lambda gc2: jnp.concatenate([gc2, gc2], axis=0), gcs)
        decay = each(lambda g, m: jnp.where(m, jnp.exp(jnp.where(m, g - g.T, 0.0)), 0.0), g1, incl)
        kb = each(lambda k, b: k * b, ks, betas)
        kst = each(lambda k: stack_heads(k).astype(BF16), ks)
        a_raw = each(lambda x, y: _dot_nt(stack_heads(x).astype(BF16), y), kb, kst)
        qk_raw = each(lambda x, y: _dot_nt(stack_heads(x).astype(BF16), y), qs, kst)
        yield
        qk = each(lambda m, x, dc: jnp.where(m, x * dc, 0.0).astype(BF16), incl, qk_raw, decay)
        a = each(lambda m, x, dc: jnp.where(m, x * dc, 0.0), strict, a_raw, decay)
        tinv = each(lambda d, x: eye - jnp.where(pair_masks[d][0], x, 0.0), dirs, a)
        for lv in range(1, _log2(c)):
            ta = each(lambda d, t, x: _mm(t, jnp.where(pair_masks[d][lv], x, 0.0)), dirs, tinv, a)
            yield
            tat = each(_mm, ta, tinv)
            yield
            tinv = each(lambda t, x: t - x, tinv, tat)
        egc = each(jnp.exp, gcs)
        rhs = each(lambda v, b, x, e: jnp.concatenate([stack_heads(v * b), stack_heads(x * e)], axis=1),
                   vs, betas, kb, egc)
        sol = each(_mm, tinv, rhs)
        yield
        u2 = each(lambda x: fold_heads(x[:, 0:LANES]), sol)
        w2 = each(lambda x: fold_heads(x[:, LANES:2 * LANES]), sol)
        gl = each(lambda d, gc2: gc2[c - 1:c, :] if d == 0 else gc2[0:1, :], dirs, gcs)
        ktail = each(lambda k, g, gc2: (k * jnp.exp(g - gc2)).astype(BF16), ks, gl, gcs)
        qwu = each(lambda x, w, u: _dot(x, jnp.concatenate([stack_heads(w), stack_heads(u)], axis=1).astype(BF16)),
                   qk, w2, u2)
        kwu = each(lambda x, w, u: _dot_tn(x, jnp.concatenate([w, u], axis=1).astype(BF16)), ktail, w2, u2)
        yield
        q_eff = each(lambda q, e, x: (q * e - fold_heads(x[:, 0:LANES])).astype(BF16), qs, egc, qwu)
        m_neg = each(lambda x: jnp.where(same_head, -x[:, 0:LANES], 0.0).astype(BF16), kwu)
        o_loc = each(lambda x: fold_heads(x[:, LANES:2 * LANES]), qwu)
        s_loc = each(lambda x: jnp.where(same_head, x[:, LANES:2 * LANES], 0.0), kwu)
        egl = each(lambda g: jnp.broadcast_to(jnp.exp(g), (SUBLANES, LANES)), gl)
        out.extend(zip(q_eff, m_neg, o_loc, s_loc, egl))

    def chunk_rows(chunk, rows_per_chunk):
        return pl.ds(pl.multiple_of(chunk * rows_per_chunk, rows_per_chunk), rows_per_chunk)

    ctx_chunks = nc // c
    per_group = chunks_per_iter
    n_groups = n_chunks // per_group

    def chunks_at(step):
        return jnp.where(step < ctx_chunks, step + n // c, step - ctx_chunks), n_chunks - 1 - step

    def run_group(local_group, scan_group, states):
        dirs, chunks, qs, ks, vs, betas, gcs = [], [], [], [], [], [], []
        if local_group is not None:
            for g in range(per_group):
                for d, chunk in enumerate(chunks_at(per_group * local_group + g)):
                    rows = chunk_rows(chunk, c)
                    dirs.append(d)
                    chunks.append(chunk)
                    qs.append(q_ref[rows, :])
                    ks.append(k_ref[rows, :])
                    vs.append(v_ref[rows, :])
                    betas.append(bb_ref[d, rows, :])
                    gcs.append(gb_ref[d, rows, :])
        scan_chunks, scan_in = [], []
        if scan_group is not None:
            for g in range(per_group):
                step_chunks = chunks_at(per_group * scan_group + g)
                scan_chunks.append(step_chunks)
                scan_in.append([(qe_ref[d, chunk_rows(ch, c), :], mp_ref[d, chunk_rows(ch, 2 * c), :],
                                 ou_ref[d, chunk_rows(ch, c), :], nn_ref[d, chunk_rows(ch, 2 * c), :],
                                 egl_ref[d, chunk_rows(ch, SUBLANES), :]) for d, ch in enumerate(step_chunks)])
        scan_out = []

        def scan_step(states):
            loaded = scan_in[len(scan_out)]
            res = [_dot(jnp.concatenate([ld[0], ld[1]], axis=0), s2.astype(BF16)) for ld, s2 in zip(loaded, states)]
            scan_out.append([r[0:c] + ld[2] for r, ld in zip(res, loaded)])
            return tuple(s2 * ld[4][0:1] + r[c:3 * c] + ld[3] for s2, ld, r in zip(states, loaded, res))

        local_out = []
        stages = local_stages(dirs, qs, ks, vs, betas, gcs, local_out) if local_group is not None else iter(())
        n_stages = 2 * _log2(c) + 1
        every = max(1, n_stages // per_group)
        for stage, _ in enumerate(stages):
            if scan_group is not None and stage % every == 0 and len(scan_out) < per_group:
                states = scan_step(states)
        while scan_group is not None and len(scan_out) < per_group:
            states = scan_step(states)
        for d, chunk, (q_eff, m_neg, o_loc, s_loc, egl) in zip(dirs, chunks, local_out):
            qe_ref[d, chunk_rows(chunk, c), :] = q_eff
            mp_ref[d, chunk_rows(chunk, 2 * c), :] = m_neg
            ou_ref[d, chunk_rows(chunk, c), :] = o_loc
            nn_ref[d, chunk_rows(chunk, 2 * c), :] = s_loc
            egl_ref[d, chunk_rows(chunk, SUBLANES), :] = egl
        for step_chunks, outs in zip(scan_chunks, scan_out):
            for d, ch in enumerate(step_chunks):
                o_ref[d, chunk_rows(ch, c), :] = outs[d]
        return states

    zero_state = jnp.zeros((2 * c, 2 * c), F32)
    states = run_group(0, None, (zero_state, zero_state))
    states = lax.fori_loop(1, n_groups, lambda j, st: run_group(j, j - 1, st), states)
    run_group(None, n_groups - 1, states)

    ng = ng_ref[...]
    step = 256
    for r in range(0, l, step):
        o = o_ref[0, r:r + step, :] + o_ref[1, r:r + step, :]
        ms = _half_sums(o * o, lane_lo) * (1.0 / GDN_DIM)
        gate = qkvg_ref[0, r:r + step, 3 * LANES:4 * LANES]
        out_ref[0, r:r + step, :] = (o * lax.rsqrt(ms + EPS) * ng * _silu(gate)).astype(out_ref.dtype)


def _dot_sel_lhs(sel_bf16, x):
    hi, lo = _split2(x)
    return _dot(sel_bf16, hi) + _dot(sel_bf16, lo)


def _gdn_call(qkvg, ba, conv_w, gate_params, ng, n, nc):
    b, l, _ = qkvg.shape
    pairs = GDN_HEADS // 2
    n_chunks = l // GDN_CHUNK
    kern = functools.partial(_gdn_kernel, n=n, nc=nc, chunks_per_iter=GDN_CHUNKS_PER_GROUP)
    return pl.pallas_call(
        kern,
        grid=(b, pairs),
        in_specs=[
            pl.BlockSpec((1, l, 4 * LANES), lambda i, p: (i, 0, p)),
            pl.BlockSpec((1, l, LANES), lambda i, p: (i, 0, 0)),
            pl.BlockSpec((3, 4 * LANES), lambda i, p: (0, p)),
            pl.BlockSpec((2, LANES), lambda i, p: (0, 0)),
            pl.BlockSpec((1, LANES), lambda i, p: (0, 0)),
        ],
        out_specs=pl.BlockSpec((1, l, LANES), lambda i, p: (i, 0, p)),
        out_shape=jax.ShapeDtypeStruct((b, l, pairs * LANES), MIXER_OUT_DTYPE),
        scratch_shapes=[
            pltpu.VMEM((l + 5 * SUBLANES, 3 * LANES), F32),
            pltpu.VMEM((l, LANES), F32),
            pltpu.VMEM((l, LANES), F32),
            pltpu.VMEM((l, LANES), F32),
            pltpu.VMEM((2, l, LANES), F32),
            pltpu.VMEM((2, l, LANES), F32),
            pltpu.VMEM((2, l, LANES), BF16),
            pltpu.VMEM((2, 2 * l, LANES), BF16),
            pltpu.VMEM((2, l, LANES), F32),
            pltpu.VMEM((2, 2 * l, LANES), F32),
            pltpu.VMEM((2, n_chunks * SUBLANES, LANES), F32),
            pltpu.VMEM((2, l, LANES), F32),
        ],
        compiler_params=_cparams(("arbitrary", "arbitrary")),
        name="gated_deltanet",
    )(qkvg, ba, conv_w, gate_params, ng)


def _diff_kernel(*refs, key_start, n_sub, lam_init, aliased):
    if aliased:
        refs = refs[1:]
    q_ref, k_ref, vt_ref, lam_ref, ng_ref, o_ref = refs
    lp = lam_ref[...]
    lam = (jnp.exp(jnp.sum(lp[0:1] * lp[1:2], axis=-1, keepdims=True))
           - jnp.exp(jnp.sum(lp[2:3] * lp[3:4], axis=-1, keepdims=True)) + lam_init)
    lane = lax.broadcasted_iota(jnp.int32, (1, LANES), 1)
    halves = (lane < DIFF_DIM, lane >= DIFF_DIM)
    ng = ng_ref[...]
    k = k_ref[0, key_start:, :]
    vt = vt_ref[0, :, key_start:]
    tq = q_ref.shape[1] // n_sub

    def scores_of(i):
        q = q_ref[0, i * tq:(i + 1) * tq, :]
        return [_dot_nt(k, jnp.where(m, q, jnp.zeros_like(q))) for m in halves]

    ahead = scores_of(0)
    for i in range(n_sub):
        s = ahead
        if i + 1 < n_sub:
            ahead = scores_of(i + 1)
        e = [jnp.exp(x - jnp.max(x, axis=0, keepdims=True)) for x in s]
        pv = [_dot(vt, x.astype(BF16)) for x in e]
        parts = [x * (1.0 / jnp.sum(y, axis=0, keepdims=True)) for x, y in zip(pv, e)]
        ot = parts[0] - lam * parts[1]
        ot = ot * lax.rsqrt(jnp.mean(ot * ot, axis=0, keepdims=True) + EPS)
        o_ref[0, i * tq:(i + 1) * tq, :] = (ot.T * ng * (1.0 - lam_init)).astype(o_ref.dtype)


def _diff_call(dq, dk, dvt, lam_p, ng, lam_init, q_rows, first_block, n_q_blocks, key_start, n_sub, prev_out):
    b, l, _ = dq.shape
    aliased = prev_out is not None
    kern = functools.partial(_diff_kernel, key_start=key_start, n_sub=n_sub, lam_init=lam_init, aliased=aliased)
    row_of = lambda t: first_block + t
    in_specs = [
        pl.BlockSpec((1, q_rows, LANES), lambda i, h, t: (i, row_of(t), h)),
        pl.BlockSpec((1, l, LANES), lambda i, h, t: (i, 0, h)),
        pl.BlockSpec((1, LANES, l), lambda i, h, t: (i, h, 0)),
        pl.BlockSpec((4, DIFF_DIM), lambda i, h, t: (0, 0)),
        pl.BlockSpec((1, LANES), lambda i, h, t: (0, 0)),
    ]
    args = [dq, dk, dvt, lam_p, ng.reshape(1, LANES)]
    aliases = {}
    if aliased:
        in_specs = [pl.BlockSpec(memory_space=pl.ANY)] + in_specs
        args = [prev_out] + args
        aliases = {0: 0}
    return pl.pallas_call(
        kern,
        grid=(b, DIFF_HEADS, n_q_blocks),
        in_specs=in_specs,
        out_specs=pl.BlockSpec((1, q_rows, LANES), lambda i, h, t: (i, row_of(t), h)),
        out_shape=jax.ShapeDtypeStruct((b, l, DIFF_HEADS * LANES), MIXER_OUT_DTYPE),
        input_output_aliases=aliases,
        compiler_params=_cparams(("arbitrary", "arbitrary", "arbitrary")),
        name="diff_attention_ctx" if aliased else "diff_attention",
    )(*args)


def _swa_kernel(q_ref, k_ref, v_ref, sink_ref, o_ref, *, n, nc):
    t = pl.program_id(1)
    blk = SWA_BLOCK
    n_x = n // blk
    q = q_ref[0]
    lane = lax.broadcasted_iota(jnp.int32, (1, LANES), 1)
    lane_lo = lane < HEAD_DIM
    sink = sink_ref[...]
    group = SWA_HEADS // SWA_KV_HEADS

    def run(keys, vals, valid):
        head_of_row = lax.broadcasted_iota(jnp.int32, (group * blk, 1), 0) >> _log2(blk)
        kvs = range(SWA_KV_HEADS)
        kk = [keys[:, kvh * LANES:(kvh + 1) * LANES] for kvh in kvs]
        vv = [vals[:, kvh * LANES:(kvh + 1) * LANES] for kvh in kvs]
        qst, sk = [], []
        for kvh in kvs:
            q_rows = []
            sk_rows = jnp.zeros((group * blk, 1), F32)
            for g in range(group):
                h = kvh * group + g
                qp = q[:, (h // 2) * LANES:(h // 2 + 1) * LANES]
                q_rows.append(jnp.where(lane_lo if h % 2 == 0 else jnp.logical_not(lane_lo), qp, jnp.zeros_like(qp)))
                sk_rows = jnp.where(head_of_row == g, sink[:, h:h + 1], sk_rows)
            qst.append(jnp.concatenate(q_rows, axis=0))
            sk.append(sk_rows)
        s = [_dot_nt(x, y) for x, y in zip(qst, kk)]
        if valid is not None:
            s = [jnp.where(valid, x, NEG_INF) for x in s]
        mx = [jnp.maximum(jnp.max(x, axis=-1, keepdims=True), y) for x, y in zip(s, sk)]
        e = [jnp.exp(x - m) for x, m in zip(s, mx)]
        pv = [_dot(x.astype(BF16), y) for x, y in zip(e, vv)]
        den = [jnp.sum(x, axis=-1, keepdims=True) + jnp.exp(y - m) for x, y, m in zip(e, sk, mx)]
        outs = []
        for o, dn in zip(pv, den):
            o = o * (1.0 / dn)
            for g in range(0, group, 2):
                outs.append(jnp.where(lane_lo, o[g * blk:(g + 1) * blk], o[(g + 1) * blk:(g + 2) * blk]))
        o_ref[0] = jnp.concatenate(outs, axis=1).astype(o_ref.dtype)

    @pl.when(t < n_x)
    def _():
        start = pl.multiple_of(jnp.clip((t - 1) * blk, 0, n - 3 * blk), blk)
        keys = jnp.concatenate([k_ref[0, pl.ds(start, 3 * blk), :], k_ref[0, n:n + nc, :]], axis=0)
        vals = jnp.concatenate([v_ref[0, pl.ds(start, 3 * blk), :], v_ref[0, n:n + nc, :]], axis=0)
        shape = (group * blk, 3 * blk + nc)
        qpos = t * blk + (lax.broadcasted_iota(jnp.int32, shape, 0) & (blk - 1))
        col = lax.broadcasted_iota(jnp.int32, shape, 1)
        dist = qpos - (start + col)
        in_window = jnp.logical_and(dist <= SWA_WINDOW, dist >= -SWA_WINDOW)
        valid = jnp.logical_or(col >= 3 * blk, in_window)
        run(keys, vals, valid)

    @pl.when(t >= n_x)
    def _():
        run(k_ref[0, n:n + nc, :], v_ref[0, n:n + nc, :], None)


def _swa_call(q, k, v, sink, n, nc, with_ctx):
    b, l, _ = q.shape
    blk = SWA_BLOCK
    nt = (l if with_ctx else n) // blk
    kern = functools.partial(_swa_kernel, n=n, nc=nc)
    return pl.pallas_call(
        kern,
        grid=(b, nt),
        in_specs=[
            pl.BlockSpec((1, blk, SWA_HEADS * HEAD_DIM), lambda i, t: (i, t, 0)),
            pl.BlockSpec((1, l, 2 * LANES), lambda i, t: (i, 0, 0)),
            pl.BlockSpec((1, l, 2 * LANES), lambda i, t: (i, 0, 0)),
            pl.BlockSpec((1, LANES), lambda i, t: (0, 0)),
        ],
        out_specs=pl.BlockSpec((1, blk, SWA_HEADS * HEAD_DIM), lambda i, t: (i, t, 0)),
        out_shape=jax.ShapeDtypeStruct((b, l, SWA_HEADS * HEAD_DIM), MIXER_OUT_DTYPE),
        compiler_params=_cparams(("arbitrary", "arbitrary")),
        name="window_attention",
    )(q, k, v, sink)


def _dft_tables(n):
    h = n // 2
    r = 1 << (_log2(h) // 2)
    j = jnp.arange(h, dtype=jnp.int32)

    def tables(m):
        thin = lambda k: ((k[:, None] * m[None, :]) % (2 * n)).astype(F32) * (math.pi / n)
        a = thin(r * jnp.arange(h // r, dtype=jnp.int32))[:, None, :]
        b = thin(jnp.arange(r, dtype=jnp.int32))[None, :, :]
        cos = jnp.cos(a) * jnp.cos(b) - jnp.sin(a) * jnp.sin(b)
        sin = jnp.sin(a) * jnp.cos(b) + jnp.cos(a) * jnp.sin(b)
        return cos.reshape(h, h).astype(BF16), (-sin).reshape(h, h).astype(BF16)

    ce, se = tables(2 * j)
    co, so = tables(2 * j + 1)
    return ce, se, co, so, co.T, so.T


def _hyena_filter_kernel(feat_ref, w1_ref, b1_ref, w2_ref, b2_ref, freq_ref, w3f_ref, w3b_ref, dl_ref,
                         ce_ref, se_ref, co_ref, so_ref, ka_ref, kb_ref, km_ref):
    h = feat_ref.shape[1]
    assert h % 2 == 0
    freq = freq_ref[...]
    dl = dl_ref[...]
    row = lax.broadcasted_iota(jnp.int32, (h, 1), 0)

    def taps(part, w3_ref):
        feat = feat_ref[part]
        x = jnp.sin(freq[0:1] * (_dot_f32(feat, w1_ref[...]) + b1_ref[...]))
        x = jnp.sin(freq[1:2] * (_dot_f32(x, w2_ref[...]) + b2_ref[...]))
        return _dot_f32(x, w3_ref[...]) * jnp.exp(-feat[:, 0:1] * dl)

    fe, fo = taps(0, w3f_ref), taps(1, w3f_ref)
    be, bo = jnp.where(row == 0, 0.0, taps(2, w3b_ref)), taps(3, w3b_ref)
    ss = sum(jnp.sum(x * x, axis=0, keepdims=True) for x in (fe, fo, be, bo))
    sc = lax.rsqrt(ss + EPS)
    fe, fo, be, bo = (x * sc for x in (fe, fo, be, bo))
    sgn = jnp.where((row & 1) == 0, 1.0, -1.0)

    def dot2(t_ref, x):
        hi, lo = _split2(x)
        return _dot(t_ref[...], hi) + _dot(t_ref[...], lo)

    def bins(x_even, x_odd):
        ce, co = dot2(ce_ref, x_even), dot2(co_ref, x_odd)
        se, so = dot2(se_ref, x_even), dot2(so_ref, x_odd)
        return ce + co, se + so, ce - co, so - se

    f = bins(fe, fo)
    g = bins(be, bo)
    ka_ref[0, 0], ka_ref[0, 1], kb_ref[0, 0], kb_ref[0, 1] = (x + sgn * y for x, y in zip(f, g))
    mid_r = jnp.sum((fe + be) * sgn, axis=0, keepdims=True)
    mid_i = -jnp.sum((fo + bo) * sgn, axis=0, keepdims=True)
    km_ref[0] = jnp.concatenate([mid_r, mid_i, jnp.zeros((SUBLANES - 2, mid_r.shape[-1]), F32)], axis=0)


def _hyena_filter_call(feats, w1, b1, w2, b2, freq, w3, deltas, tables):
    h = feats.shape[1]
    hid = w2.shape[0]
    ch = deltas.shape[-1]
    tc = MXU_WIDTH
    nct = ch // tc
    const = lambda shape: pl.BlockSpec(shape, lambda o, j: (0,) * len(shape))
    spectrum = pl.BlockSpec((1, 2, h, tc), lambda o, j: (o, 0, 0, j))
    return pl.pallas_call(
        _hyena_filter_kernel,
        grid=(2, nct),
        in_specs=[
            const((4, h, hid)), const((hid, hid)), const((1, hid)), const((hid, hid)), const((1, hid)),
            const((2, hid)),
            pl.BlockSpec((hid, tc), lambda o, j: (0, (2 * o) * nct + j)),
            pl.BlockSpec((hid, tc), lambda o, j: (0, (2 * o + 1) * nct + j)),
            pl.BlockSpec((1, tc), lambda o, j: (0, j)),
        ] + [_resident((h, h))] * 4,
        out_specs=[spectrum, spectrum, pl.BlockSpec((1, SUBLANES, tc), lambda o, j: (o, 0, j))],
        out_shape=[
            jax.ShapeDtypeStruct((2, 2, h, ch), F32),
            jax.ShapeDtypeStruct((2, 2, h, ch), F32),
            jax.ShapeDtypeStruct((2, SUBLANES, ch), F32),
        ],
        compiler_params=_cparams(("arbitrary", "arbitrary")),
        name="hyena_filters",
    )(feats, w1, b1, w2, b2, freq, w3, w3, deltas, *tables[:4])


def _hyena_kernel(*refs, n, aliased):
    if aliased:
        refs = refs[1:]
    (v_ref, x1_ref, x2_ref, cwv_ref, cw1_ref, cw2_ref, ka_ref, kb_ref, km_ref, bias_ref,
     ce_ref, se_ref, co_ref, so_ref, cot_ref, sot_ref, o_ref, pad_ref, z_ref, zb_ref, p_ref, y_ref) = refs
    halo = SUBLANES
    tc = o_ref.shape[-1]
    h = n // 2
    rc = min(h, HY_ROW_CHUNK)
    lane_groups = tc // LANES
    zero_rows = jnp.zeros((halo, LANES), F32)
    for g in range(lane_groups):
        pad_ref[g, 0:halo, :] = zero_rows
        pad_ref[g, halo + n:2 * halo + n, :] = zero_rows

    def stage(ref):
        for r in range(0, n, 2 * rc):
            for g in range(lane_groups):
                pad_ref[g, halo + r:halo + r + 2 * rc, :] = ref[0, r:r + 2 * rc, g * LANES:(g + 1) * LANES]

    def conv_rows(cw, parity, r):
        first = halo + 2 * r + parity - 1
        taps = [jnp.concatenate([pad_ref[g, pl.ds(first + i, rc, stride=2), :] for g in range(lane_groups)], axis=1)
                for i in range(3)]
        return cw[0:1] * taps[0] + cw[1:2] * taps[1] + cw[2:3] * taps[2]

    def alt_sign(r):
        j = r + lax.broadcasted_iota(jnp.int32, (rc, 1), 0)
        return j, jnp.where((j & 1) == 0, 1.0, -1.0)

    stage(v_ref)
    cw = cwv_ref[...]
    for parity in range(2):
        for r in range(0, h, rc):
            z = conv_rows(cw, parity, r)
            z_ref[parity, r:r + rc, :] = z
            zb_ref[parity, r:r + rc, :] = z.astype(BF16)

    for o, (gate_ref, gate_cw_ref) in enumerate(((x1_ref, cw1_ref), (x2_ref, cw2_ref))):
        mid_r = jnp.zeros((1, tc), F32)
        mid_i = jnp.zeros((1, tc), F32)
        for r in range(0, h, rc):
            _, sgn = alt_sign(r)
            mid_r = mid_r + jnp.sum(z_ref[0, r:r + rc, :] * sgn, axis=0, keepdims=True)
            mid_i = mid_i - jnp.sum(z_ref[1, r:r + rc, :] * sgn, axis=0, keepdims=True)
        km_r = km_ref[o, 0:1, :]
        km_i = km_ref[o, 1:2, :]
        pm_r = (mid_r * km_r - mid_i * km_i) * (1.0 / n)
        pm_i = (mid_r * km_i + mid_i * km_r) * (1.0 / n)
        ze = zb_ref[0]
        zo = zb_ref[1]
        for r in range(0, h, rc):
            k, _ = alt_sign(r)
            rows = slice(r, r + rc)
            ce, co = _dot(ce_ref[rows, :], ze), _dot(co_ref[rows, :], zo)
            se, so = _dot(se_ref[rows, :], ze), _dot(so_ref[rows, :], zo)
            xa_r, xb_r, xa_i, xb_i = ce + co, ce - co, se + so, so - se
            wgt = jnp.where(k == 0, 0.5 / n, 1.0 / n)
            ka_r, ka_i = ka_ref[o, 0, rows, :], ka_ref[o, 1, rows, :]
            kb_r, kb_i = kb_ref[o, 0, rows, :], kb_ref[o, 1, rows, :]
            pa_r = (xa_r * ka_r - xa_i * ka_i) * wgt
            pa_i = (xa_r * ka_i + xa_i * ka_r) * wgt
            pb_r = (xb_r * kb_r - xb_i * kb_i) * wgt
            pb_i = (xb_r * kb_i + xb_i * kb_r) * wgt
            p_ref[0, rows, :] = (pa_r + pb_r).astype(BF16)
            p_ref[1, rows, :] = (pa_i - pb_i).astype(BF16)
            p_ref[2, rows, :] = (pa_r - pb_r).astype(BF16)
            p_ref[3, rows, :] = (pa_i + pb_i).astype(BF16)
        stage(gate_ref)
        cw = gate_cw_ref[...]
        bias = bias_ref[o:o + 1, :]
        for parity, (c_ref, s_ref, mid) in enumerate(((ce_ref, se_ref, pm_r), (cot_ref, sot_ref, -pm_i))):
            for r in range(0, h, rc):
                _, sgn = alt_sign(r)
                rows = slice(r, r + rc)
                y = (_dot(c_ref[rows, :], p_ref[2 * parity]) + _dot(s_ref[rows, :], p_ref[2 * parity + 1])
                     + sgn * mid)
                z = conv_rows(cw, parity, r) * (y + z_ref[parity, rows, :] * bias)
                if o == 0:
                    z_ref[parity, rows, :] = z
                    zb_ref[parity, rows, :] = z.astype(BF16)
                else:
                    for g in range(lane_groups):
                        y_ref[g, pl.ds(2 * r + parity, rc, stride=2), :] = z[:, g * LANES:(g + 1) * LANES]
    for r in range(0, n, 2 * rc):
        rows = slice(r, r + 2 * rc)
        o_ref[0, rows, :] = jnp.concatenate([y_ref[g, rows, :] for g in range(lane_groups)],
                                            axis=1).astype(o_ref.dtype)


def _hyena_call(u, conv_w, ka, kb, km, bias, tables, n, row_block, prev_out):
    b, l, _ = u.shape
    ch = bias.shape[-1]
    tc = MXU_WIDTH
    nct = ch // tc
    h = n // 2
    aliased = prev_out is not None
    kern = functools.partial(_hyena_kernel, n=n, aliased=aliased)
    once = pl.Buffered(1)
    in_specs = [
        pl.BlockSpec((1, n, tc), lambda j, i: (i, row_block, j)),
        pl.BlockSpec((1, n, tc), lambda j, i: (i, row_block, nct + j)),
        pl.BlockSpec((1, n, tc), lambda j, i: (i, row_block, 2 * nct + j)),
        pl.BlockSpec((3, tc), lambda j, i: (0, j)),
        pl.BlockSpec((3, tc), lambda j, i: (0, nct + j)),
        pl.BlockSpec((3, tc), lambda j, i: (0, 2 * nct + j)),
        pl.BlockSpec((2, 2, h, tc), lambda j, i: (0, 0, 0, j), pipeline_mode=once),
        pl.BlockSpec((2, 2, h, tc), lambda j, i: (0, 0, 0, j), pipeline_mode=once),
        pl.BlockSpec((2, SUBLANES, tc), lambda j, i: (0, 0, j)),
        pl.BlockSpec((2, tc), lambda j, i: (0, j)),
    ] + [_resident((h, h))] * 6
    args = [u, u, u, conv_w, conv_w, conv_w, ka, kb, km, bias, *tables]
    aliases = {}
    if aliased:
        in_specs = [pl.BlockSpec(memory_space=pl.ANY)] + in_specs
        args = [prev_out] + args
        aliases = {0: 0}
    return pl.pallas_call(
        kern,
        grid=(nct, b),
        in_specs=in_specs,
        out_specs=pl.BlockSpec((1, n, tc), lambda j, i: (i, row_block, j)),
        out_shape=jax.ShapeDtypeStruct((b, l, ch), MIXER_OUT_DTYPE),
        scratch_shapes=[
            pltpu.VMEM((tc // LANES, n + 2 * SUBLANES, LANES), F32),
            pltpu.VMEM((2, h, tc), F32),
            pltpu.VMEM((2, h, tc), BF16),
            pltpu.VMEM((4, h, tc), BF16),
            pltpu.VMEM((tc // LANES, n, LANES), F32),
        ],
        input_output_aliases=aliases,
        compiler_params=_cparams(("arbitrary", "arbitrary")),
        name="hyena_conv_n%d" % n,
    )(*args)


def _rope_tables(n, nc):
    rows = n // GRID_W
    row = jnp.repeat(jnp.arange(rows, dtype=F32), GRID_W)
    col = jnp.tile(jnp.arange(GRID_W, dtype=F32), rows)
    half = HEAD_DIM // 2
    inv = ROPE_BASE ** (-jnp.arange(0, half, 2, dtype=F32) / half)
    ar = row[:, None] * inv
    ac = col[:, None] * inv
    cos = jnp.concatenate([jnp.cos(ar), jnp.cos(ar), jnp.cos(ac), jnp.cos(ac)], axis=-1)
    sin = jnp.concatenate([-jnp.sin(ar), jnp.sin(ar), -jnp.sin(ac), jnp.sin(ac)], axis=-1)
    cos = jnp.concatenate([cos, jnp.ones((nc, HEAD_DIM), F32)], axis=0)
    sin = jnp.concatenate([sin, jnp.zeros((nc, HEAD_DIM), F32)], axis=0)
    return jnp.tile(cos, (1, LANES // HEAD_DIM)), jnp.tile(sin, (1, LANES // HEAD_DIM))


def _rope_partner_cols(width):
    d = np.arange(width)
    quarter = HEAD_DIM // 4
    return np.where((d % (2 * quarter)) < quarter, d + quarter, d - quarter)


def _hyena_feats(n):
    pos = jnp.arange(n, dtype=F32)
    t = pos / max(n - 1, 1)
    ang = (2.0 * math.pi * pos / n)[:, None] * jnp.linspace(1e-4, HY_BANDS - 1, HY_BANDS, dtype=F32)[None, :]
    feats = jnp.concatenate([t[:, None], jnp.cos(ang), -jnp.sin(ang)], axis=-1)
    feats = jnp.pad(feats, ((0, 0), (0, 64 - feats.shape[-1])))
    back = jnp.concatenate([feats[0:1], jnp.flip(feats[1:], axis=0)], axis=0)
    return jnp.stack([feats[0::2], feats[1::2], back[0::2], back[1::2]])


def _pad_cols(w, width):
    return jnp.pad(w, ((0, 0), (0, width - w.shape[-1])))


def _layer_ab(xz, mod, norm_g0, w_in, conv_w, a_log, dt_bias, gdn_g, lam_p, diff_g, lam_init, rope, n, nc):
    hd = GDN_HEADS * GDN_DIM
    wq, wk, wv, wg = (w_in[:, i * hd:(i + 1) * hd] for i in range(4))
    o = 4 * hd
    w_beta, w_alpha = w_in[:, o:o + 16], w_in[:, o + 16:o + 32]
    o += 32
    dd = DIFF_HEADS * 2 * DIFF_DIM
    wdq, wdk, wdv = (w_in[:, o + i * dd:o + (i + 1) * dd] for i in range(3))
    pairs = GDN_HEADS // 2
    pair_cols = lambda w: [w[:, p * LANES:(p + 1) * LANES] for p in range(pairs)]
    w_qkvg = jnp.concatenate([blk for grp in zip(pair_cols(wq), pair_cols(wk), pair_cols(wv), pair_cols(wg))
                              for blk in grp], axis=1)
    perm = _rope_partner_cols(dd)
    w_all = jnp.concatenate([w_qkvg, _pad_cols(jnp.concatenate([w_beta, w_alpha], axis=1), LANES),
                             wdq, wdk, wdv, wdq[:, perm], wdk[:, perm]], axis=1).astype(BF16)
    c0 = 4 * hd
    c1 = c0 + LANES
    segs = (_Seg(0, c0), _Seg(c0, LANES),
            _Seg(c1, dd, rot_start=c1 + 3 * dd, scale=DIFF_DIM ** -0.5, dtype=BF16),
            _Seg(c1 + dd, dd, rot_start=c1 + 4 * dd, dtype=BF16),
            _Seg(c1 + 2 * dd, dd, dtype=BF16, transposed=True))
    qkvg, ba, dq, dk, dvt = _proj_call(xz, mod, norm_g0, w_all, rope[0], rope[1], segs, n // ROW_TILE, "proj_ab")

    cq, ck, cv = (conv_w[:, i * hd:(i + 1) * hd] for i in range(3))
    zeros = jnp.zeros((3, LANES), F32)
    conv_l = jnp.concatenate([blk for p in range(pairs) for blk in
                              (cq[:, p * LANES:(p + 1) * LANES], ck[:, p * LANES:(p + 1) * LANES],
                               cv[:, p * LANES:(p + 1) * LANES], zeros)], axis=1)
    n_gate = 2 * GDN_HEADS
    on_decay_lanes = lambda t: jnp.pad(t.reshape(1, n_gate), ((0, 0), (n_gate, LANES - 2 * n_gate)))
    gate_params = jnp.concatenate([on_decay_lanes(a_log), on_decay_lanes(dt_bias)], axis=0)
    ng = jnp.tile(gdn_g.reshape(1, GDN_DIM), (1, 2))
    oa = _gdn_call(qkvg, ba, conv_l, gate_params, ng, n, nc)
    q_rows = DIFF_SUB_TILES * ROW_TILE
    ob = _diff_call(dq, dk, dvt, lam_p, diff_g, lam_init, q_rows, 0, n // q_rows, 0, DIFF_SUB_TILES, None)
    ob = _diff_call(dq, dk, dvt, lam_p, diff_g, lam_init, nc, n // nc, 1, n, 1, ob)
    return oa, ob


def _layer_cd(xz, mod, norm_g0, w_in, sink, hy_conv, hy_w1, hy_b1, hy_w2, hy_b2, hy_w3, hy_freq, hy_bias,
              rope, n, nc, last, dft_x, dft_c):
    qd = SWA_HEADS * HEAD_DIM
    kd = SWA_KV_HEADS * HEAD_DIM
    wq, wk, wv, wu = w_in[:, 0:qd], w_in[:, qd:qd + kd], w_in[:, qd + kd:qd + 2 * kd], w_in[:, qd + 2 * kd:]
    dup = lambda w: jnp.concatenate([w[:, 0:HEAD_DIM], w[:, 0:HEAD_DIM], w[:, HEAD_DIM:], w[:, HEAD_DIM:]], axis=1)
    wk2, wv2 = dup(wk), dup(wv)
    ud = wu.shape[1]
    w_all = jnp.concatenate([wq, wk2, wv2, wu, wq[:, _rope_partner_cols(qd)], wk2[:, _rope_partner_cols(2 * kd)]],
                            axis=1).astype(BF16)
    o_u = qd + 4 * kd
    segs = (_Seg(0, qd, rot_start=o_u + ud, scale=HEAD_DIM ** -0.5, dtype=BF16),
            _Seg(qd, 2 * kd, rot_start=o_u + ud + qd, dtype=BF16),
            _Seg(qd + 2 * kd, 2 * kd, dtype=BF16), _Seg(o_u, ud))
    q, k, v, u = _proj_call(xz, mod, norm_g0, w_all, rope[0], rope[1], segs, n // ROW_TILE, "proj_cd")
    oc = _swa_call(q, k, v, _pad_cols(sink.reshape(1, SWA_HEADS), LANES), n, nc, not last)

    ch = hy_bias.shape[-1]
    deltas = jnp.abs(jnp.linspace(HY_MIN_DECAY, HY_MAX_DECAY, ch, dtype=F32)).reshape(1, ch)
    hid = hy_w2.shape[0]
    w1p = jnp.pad(hy_w1, ((0, hid - hy_w1.shape[0]), (0, 0)))
    filt = lambda m, dft: _hyena_filter_call(_hyena_feats(m), w1p, hy_b1.reshape(1, hid), hy_w2,
                                             hy_b2.reshape(1, hid), hy_freq, hy_w3, deltas, dft)
    od = _hyena_call(u, hy_conv, *filt(n, dft_x), hy_bias, dft_x, n, 0, None)
    if not last:
        od = _hyena_call(u, hy_conv, *filt(nc, dft_c), hy_bias, dft_c, nc, n // nc, od)
    return oc, od


def kernel(x, c, ctx, c_ctx, w_mod, b_mod, norm_g, ffn_w_up, ffn_conv, ffn_w_down, ab_w_in, ab_w_out, gdn_conv, gdn_a_log, gdn_dt_bias, gdn_norm_g, diff_lambda, diff_norm_g, cd_w_in, cd_w_out, swa_sink, hy_conv, hy_w1, hy_b1, hy_w2, hy_b2, hy_w3, hy_freq, hy_bias):
    b, n, d = x.shape
    nc = ctx.shape[1]
    depth = w_mod.shape[0]
    assert n % ROW_TILE == 0 and nc == ROW_TILE and n % GRID_W == 0
    xz = jnp.concatenate([x, ctx], axis=1)
    rows = -(-(b + 1) // SUBLANES) * SUBLANES
    cc = jnp.concatenate([c, c_ctx[None], jnp.zeros((rows - b - 1, d), F32)], axis=0)
    mods = _mod_call(cc, w_mod, b_mod)
    mod_all = jnp.concatenate([mods[:, :b].reshape(depth, b, 1, 6, d),
                               jnp.broadcast_to(mods[:, b].reshape(depth, 1, 1, 6, d), (depth, b, 1, 6, d))], axis=2)
    rope = _rope_tables(n, nc)
    dft_x = _dft_tables(n)
    dft_c = _dft_tables(nc)
    n_x_tiles = n // ROW_TILE
    for l in range(depth):
        last = l == depth - 1
        i = l // 2
        mod = mod_all[l]
        if l % 2 == 0:
            lam_init = 0.8 - 0.6 * math.exp(-0.3 * l)
            o1, o2 = _layer_ab(xz, mod, norm_g[l, 0], ab_w_in[i], gdn_conv[i], gdn_a_log[i], gdn_dt_bias[i],
                               gdn_norm_g[i], diff_lambda[i], diff_norm_g[i], lam_init, rope, n, nc)
            w_out = ab_w_out[i]
        else:
            o1, o2 = _layer_cd(xz, mod, norm_g[l, 0], cd_w_in[i], swa_sink[i], hy_conv[i], hy_w1[i], hy_b1[i],
                               hy_w2[i], hy_b2[i], hy_w3[i], hy_freq[i], hy_bias[i], rope, n, nc, last, dft_x, dft_c)
            w_out = cd_w_out[i]
        n_tiles = (n if last else n + nc) // ROW_TILE
        xz = _post_call(o1, o2, xz, mod, norm_g[l, 1], norm_g[l, 2], norm_g[l, 3], w_out.astype(BF16),
                        ffn_w_up[l].astype(BF16), ffn_conv[l], ffn_w_down[l].astype(BF16), n_tiles, n_x_tiles)
    return xz
```

```python
import functools
import math
from typing import NamedTuple, Optional

import jax
import jax.numpy as jnp
import numpy as np
from jax import lax
from jax.experimental import pallas as pl
from jax.experimental.pallas import tpu as pltpu

F32 = jnp.float32
BF16 = jnp.bfloat16
MIXER_OUT_DTYPE = BF16

EPS = 1e-6
NEG_INF = -1e30
GRID_W = 64
HEAD_DIM = 64
ROPE_BASE = 10000.0
GDN_HEADS = 8
GDN_DIM = 64
GDN_CHUNK = 64
GDN_CHUNKS_PER_GROUP = 9
DIFF_HEADS = 4
DIFF_DIM = 64
DIFF_SUB_TILES = 8
SWA_HEADS = 8
SWA_KV_HEADS = 2
SWA_WINDOW = 128
SWA_BLOCK = 128
HY_BANDS = 16
HY_MIN_DECAY = math.log(1e-2) / 1.5
HY_MAX_DECAY = math.log(1e-2) / 0.3
HY_ROW_CHUNK = 512

LANES = 128
SUBLANES = 8
MXU_WIDTH = 256
FFN_COL_CHUNK = 11 * MXU_WIDTH
ROW_TILE = 256
VMEM_LIMIT = 56 * 1024 * 1024


def _cparams(sem):
    return pltpu.CompilerParams(dimension_semantics=sem, vmem_limit_bytes=VMEM_LIMIT)


def _resident(shape):
    zeros = (0,) * len(shape)
    return pl.BlockSpec(shape, lambda *_: zeros, pipeline_mode=pl.Buffered(1))


def _log2(v):
    assert v & (v - 1) == 0
    return v.bit_length() - 1


def _sigmoid(x):
    return 1.0 / (1.0 + jnp.exp(-x))


def _silu(x):
    return x * _sigmoid(x)


def _softplus(x):
    return jnp.maximum(x, 0.0) + jnp.log1p(jnp.exp(-jnp.abs(x)))


def _dot(a, b):
    return jnp.dot(a, b, preferred_element_type=F32)


def _dot_nt(a, b):
    return lax.dot_general(a, b, (((1,), (1,)), ((), ())), preferred_element_type=F32)


def _dot_tn(a, b):
    return lax.dot_general(a, b, (((0,), (0,)), ((), ())), preferred_element_type=F32)


def _dot_f32(a, b):
    return jnp.dot(a, b, preferred_element_type=F32, precision=lax.Precision.HIGHEST)


def _split2(x):
    hi = x.astype(BF16)
    lo = (x - hi.astype(F32)).astype(BF16)
    return hi, lo


def _dot_sel(x, sel_bf16):
    hi, lo = _split2(x)
    return _dot(hi, sel_bf16) + _dot(lo, sel_bf16)


def _mm(a, b):
    return _dot(a.astype(BF16), b.astype(BF16))


def _rms(y, g):
    return y * lax.rsqrt(jnp.mean(y * y, axis=-1, keepdims=True) + EPS) * g


def _modnorm(x, g, shift, scale):
    return _rms(x, g) * (1.0 + scale) + shift


def _mod_kernel(cc_ref, w_ref, b_ref, o_ref):
    s = _silu(cc_ref[...])
    o_ref[0] = _dot(s.astype(BF16), w_ref[0].astype(BF16)) + b_ref[0]


def _mod_call(cc, w_mod, b_mod):
    depth, d, nm = w_mod.shape
    rows = cc.shape[0]
    ct = 1536
    return pl.pallas_call(
        _mod_kernel,
        grid=(depth, nm // ct),
        in_specs=[
            pl.BlockSpec((rows, d), lambda l, j: (0, 0)),
            pl.BlockSpec((1, d, ct), lambda l, j: (l, 0, j)),
            pl.BlockSpec((1, 1, ct), lambda l, j: (l, 0, j)),
        ],
        out_specs=pl.BlockSpec((1, rows, ct), lambda l, j: (l, 0, j)),
        out_shape=jax.ShapeDtypeStruct((depth, rows, nm), F32),
        compiler_params=_cparams(("arbitrary", "arbitrary")),
        name="adaln_mod",
    )(cc, w_mod, b_mod.reshape(depth, 1, nm))


class _Seg(NamedTuple):
    start: int
    width: int
    rot_start: Optional[int] = None
    scale: float = 1.0
    dtype: type = F32
    transposed: bool = False


def _proj_kernel(x_ref, mod_ref, g_ref, w_ref, cos_ref, sin_ref, *out_refs, segs):
    m = mod_ref[0, 0]
    h = _modnorm(x_ref[0], g_ref[...], m[0:1], m[1:2]).astype(BF16)
    for o_ref, seg in zip(out_refs, segs):
        y = _dot(h, w_ref[:, seg.start:seg.start + seg.width])
        if seg.rot_start is not None:
            yr = _dot(h, w_ref[:, seg.rot_start:seg.rot_start + seg.width])
            reps = seg.width // LANES
            cos = jnp.concatenate([cos_ref[...]] * reps, axis=1)
            sin = jnp.concatenate([sin_ref[...]] * reps, axis=1)
            y = y * cos + yr * sin
        if seg.scale != 1.0:
            y = y * seg.scale
        if seg.transposed:
            y = y.T
        o_ref[0] = y.astype(seg.dtype)


def _proj_call(xz, mod, g, w, cos_t, sin_t, segs, n_x_tiles, name):
    b, l, d = xz.shape
    tm = ROW_TILE
    nt = l // tm
    p = w.shape[1]
    return pl.pallas_call(
        functools.partial(_proj_kernel, segs=segs),
        grid=(nt, b),
        in_specs=[
            pl.BlockSpec((1, tm, d), lambda t, i: (i, t, 0)),
            pl.BlockSpec((1, 1, 6, d), lambda t, i: (i, t // n_x_tiles, 0, 0)),
            pl.BlockSpec((1, d), lambda t, i: (0, 0)),
            _resident((d, p)),
            pl.BlockSpec((tm, LANES), lambda t, i: (t, 0)),
            pl.BlockSpec((tm, LANES), lambda t, i: (t, 0)),
        ],
        out_specs=[pl.BlockSpec((1, s.width, tm), lambda t, i: (i, 0, t)) if s.transposed
                   else pl.BlockSpec((1, tm, s.width), lambda t, i: (i, t, 0)) for s in segs],
        out_shape=[jax.ShapeDtypeStruct((b, s.width, l) if s.transposed else (b, l, s.width), s.dtype)
                   for s in segs],
        compiler_params=_cparams(("arbitrary", "arbitrary")),
        name=name,
    )(xz, mod, g.reshape(1, d), w, cos_t, sin_t)


def _post_kernel(o1p_ref, o1_ref, o1n_ref, o2p_ref, o2_ref, o2n_ref, xp_ref, x_ref, xn_ref, mod_ref,
                 g1_ref, g2_ref, g3_ref, wout_ref, wup_ref, cw_ref, wdn_ref, out_ref, up_ref,
                 *, tm, n_x_tiles, n_tiles, cf, dff):
    t = pl.program_id(0)
    first = jnp.logical_or(t == 0, t == n_x_tiles)
    last = jnp.logical_or(t == n_x_tiles - 1, t == n_tiles - 1)
    m = mod_ref[0, 0]
    halo = SUBLANES
    ohalo = o1p_ref.shape[1]
    k1 = o1_ref.shape[-1]
    o1e = jnp.concatenate([o1p_ref[0], o1_ref[0], o1n_ref[0]], axis=0)
    o2e = jnp.concatenate([o2p_ref[0], o2_ref[0], o2n_ref[0]], axis=0)
    y = _dot(o1e, wout_ref[0:k1, :]) + _dot(o2e, wout_ref[k1:, :])
    y = y[ohalo - halo:ohalo + tm + halo]
    xe = jnp.concatenate([xp_ref[0], x_ref[0], xn_ref[0]], axis=0)
    x1 = xe + m[2:3] * _rms(y, g1_ref[...])
    h = _modnorm(x1, g2_ref[...], m[3:4], m[4:5]).astype(BF16)
    acc = jnp.zeros((tm, x_ref.shape[-1]), F32)
    for c0 in range(0, dff, cf):
        wd = min(cf, dff - c0)
        halves = []
        for half, base in enumerate((c0, dff + c0)):
            u = _dot(h, wup_ref[:, base:base + wd])
            up_ref[half, :, 0:wd] = u
            up_ref[half, 0:halo, 0:wd] = jnp.where(first, 0.0, u[0:halo])
            up_ref[half, tm + halo:tm + 2 * halo, 0:wd] = jnp.where(last, 0.0, u[tm + halo:])
            cw = cw_ref[:, base:base + wd]
            halves.append(cw[0:1] * up_ref[half, halo - 1:halo - 1 + tm, 0:wd]
                          + cw[1:2] * up_ref[half, halo:halo + tm, 0:wd]
                          + cw[2:3] * up_ref[half, halo + 1:halo + 1 + tm, 0:wd])
        act = (_silu(halves[1]) * halves[0]).astype(BF16)
        acc = acc + _dot(act, wdn_ref[c0:c0 + wd, :])
    out_ref[0] = x1[halo:halo + tm] + m[5:6] * _rms(acc, g3_ref[...])


def _post_call(o1, o2, xz, mod, g1, g2, g3, w_out, w_up, conv_w, w_down, n_tiles, n_x_tiles):
    b, _, d = xz.shape
    tm = ROW_TILE
    rows = n_tiles * tm
    dff = w_down.shape[0]
    cf = FFN_COL_CHUNK
    k1, k2 = o1.shape[-1], o2.shape[-1]
    ohalo = 2 * SUBLANES
    kern = functools.partial(_post_kernel, tm=tm, n_x_tiles=n_x_tiles, n_tiles=n_tiles, cf=cf, dff=dff)

    def with_halos(width, halo_rows):
        per_tile = tm // halo_rows
        n_blocks = rows // halo_rows
        return [
            pl.BlockSpec((1, halo_rows, width), lambda t, i: (i, jnp.maximum(t * per_tile - 1, 0), 0)),
            pl.BlockSpec((1, tm, width), lambda t, i: (i, t, 0)),
            pl.BlockSpec((1, halo_rows, width), lambda t, i: (i, jnp.minimum((t + 1) * per_tile, n_blocks - 1), 0)),
        ]

    row_vec = pl.BlockSpec((1, d), lambda t, i: (0, 0))
    return pl.pallas_call(
        kern,
        grid=(n_tiles, b),
        in_specs=with_halos(k1, ohalo) + with_halos(k2, ohalo) + with_halos(d, SUBLANES) + [
            pl.BlockSpec((1, 1, 6, d), lambda t, i: (i, t // n_x_tiles, 0, 0)),
            row_vec, row_vec, row_vec,
            _resident((k1 + k2, d)),
            _resident((d, 2 * dff)),
            pl.BlockSpec((3, 2 * dff), lambda t, i: (0, 0)),
            _resident((dff, d)),
        ],
        out_specs=pl.BlockSpec((1, tm, d), lambda t, i: (i, t, 0)),
        out_shape=jax.ShapeDtypeStruct((b, rows, d), F32),
        scratch_shapes=[pltpu.VMEM((2, tm + 2 * SUBLANES, cf), F32)],
        compiler_params=_cparams(("arbitrary", "arbitrary")),
        name="mixer_out_conv_ffn",
    )(o1, o1, o1, o2, o2, o2, xz, xz, xz, mod, g1.reshape(1, d), g2.reshape(1, d), g3.reshape(1, d),
      w_out, w_up, conv_w, w_down)


def _half_sums(x2, lane_lo):
    s0 = jnp.sum(jnp.where(lane_lo, x2, 0.0), axis=-1, keepdims=True)
    s1 = jnp.sum(jnp.where(lane_lo, 0.0, x2), axis=-1, keepdims=True)
    return jnp.where(lane_lo, s0, s1)


def _gdn_kernel(qkvg_ref, ba_ref, cw_ref, gp_ref, ng_ref, out_ref,
                pad_ref, q_ref, k_ref, v_ref, bb_ref, gb_ref, qe_ref, mp_ref, ou_ref, nn_ref, egl_ref, o_ref,
                *, n, nc, chunks_per_iter):
    l = n + nc
    c = GDN_CHUNK
    n_chunks = l // c
    pair = pl.program_id(1)
    halo = SUBLANES
    lane = lax.broadcasted_iota(jnp.int32, (1, LANES), 1)
    lane_lo = lane < GDN_DIM

    cw = cw_ref[:, 0:3 * LANES]
    zero_rows = jnp.zeros((halo, 3 * LANES), F32)
    for seq_start, seq_len in ((0, n), (n, nc)):
        base = halo + seq_start + (2 * halo if seq_start else 0)
        pad_ref[base - halo:base, :] = zero_rows
        pad_ref[base + seq_len:base + seq_len + halo, :] = zero_rows
        step = 256
        for r in range(0, seq_len, step):
            pad_ref[base + r:base + r + step, :] = qkvg_ref[0, seq_start + r:seq_start + r + step, 0:3 * LANES]
        for r in range(0, seq_len, step):
            y = (cw[0:1] * pad_ref[base + r - 1:base + r - 1 + step, :]
                 + cw[1:2] * pad_ref[base + r:base + r + step, :]
                 + cw[2:3] * pad_ref[base + r + 1:base + r + 1 + step, :])
            y = _silu(y)
            q = y[:, 0:LANES]
            k = y[:, LANES:2 * LANES]
            rows = slice(seq_start + r, seq_start + r + step)
            q_ref[rows, :] = q * lax.rsqrt(_half_sums(q * q, lane_lo) + EPS) * (GDN_DIM ** -0.5)
            k_ref[rows, :] = k * lax.rsqrt(_half_sums(k * k, lane_lo) + EPS)
            v_ref[rows, :] = y[:, 2 * LANES:3 * LANES]

    sel_r = lax.broadcasted_iota(jnp.int32, (LANES, 4 * LANES), 0)
    sel_c = lax.broadcasted_iota(jnp.int32, (LANES, 4 * LANES), 1)
    quarter = sel_c >> _log2(LANES)
    src_lane = (quarter & 1) * 2 * GDN_HEADS + (quarter >> 1) * GDN_HEADS + 2 * pair + ((sel_c >> _log2(GDN_DIM)) & 1)
    sel = (sel_r == src_lane).astype(BF16)
    gblk = 256
    bi = lax.broadcasted_iota(jnp.int32, (gblk, gblk), 0)
    bj = lax.broadcasted_iota(jnp.int32, (gblk, gblk), 1)
    same_chunk = (bi >> _log2(c)) == (bj >> _log2(c))
    csum = (jnp.logical_and(same_chunk, bi >= bj).astype(BF16), jnp.logical_and(same_chunk, bi <= bj).astype(BF16))
    neg_a = -jnp.exp(gp_ref[0:1, :])
    dt_bias = gp_ref[1:2, :]
    for r in range(0, l, gblk):
        ba = ba_ref[0, r:r + gblk, :]
        gates = jnp.where(lane < 2 * GDN_HEADS, _sigmoid(ba), neg_a * _softplus(ba + dt_bias))
        x = _dot_sel(gates, sel)
        for d in range(2):
            bb_ref[d, r:r + gblk, :] = x[:, 2 * d * LANES:(2 * d + 1) * LANES]
            gb_ref[d, r:r + gblk, :] = _dot_sel_lhs(csum[d], x[:, (2 * d + 1) * LANES:(2 * d + 2) * LANES])

    r2 = lax.broadcasted_iota(jnp.int32, (2 * c, 2 * c), 0)
    c2 = lax.broadcasted_iota(jnp.int32, (2 * c, 2 * c), 1)
    same_head = (r2 >= c) == (c2 >= c)
    eye = (r2 == c2).astype(F32)
    masks = ((jnp.logical_and(same_head, r2 >= c2), jnp.logical_and(same_head, r2 > c2)),
             (jnp.logical_and(same_head, r2 <= c2), jnp.logical_and(same_head, r2 < c2)))
    m0 = lane_lo.astype(F32)
    m1 = 1.0 - m0

    def pair_mask(lv, lower):
        same_block = (r2 >> (lv + 1)) == (c2 >> (lv + 1))
        r_hi = ((r2 >> lv) & 1) == 1
        c_hi = ((c2 >> lv) & 1) == 1
        off = jnp.logical_and(r_hi, jnp.logical_not(c_hi)) if lower else jnp.logical_and(c_hi, jnp.logical_not(r_hi))
        return jnp.logical_and(same_block, off)

    pair_masks = tuple(tuple(pair_mask(lv, lower) for lv in range(_log2(c))) for lower in (True, False))

    def stack_heads(x2):
        xb = x2.astype(BF16)
        zero = jnp.zeros_like(xb)
        return jnp.concatenate([jnp.where(lane_lo, xb, zero), jnp.where(lane_lo, zero, xb)], axis=0)

    def fold_heads(x):
        return x[0:c] + x[c:2 * c]

    def local_stages(dirs, qs, ks, vs, betas, gcs, out):
        each = lambda f, *cols: [f(*args) for args in zip(*cols)]
        incl = [masks[d][0] for d in dirs]
        strict = [masks[d][1] for d in dirs]
        g1 = each(lambda gc2: jnp.concatenate([gc2, gc2], axis=0), gcs)
        decay = each(lambda g, m: jnp.where(m, jnp.exp(jnp.where(m, g - g.T, 0.0)), 0.0), g1, incl)
        kb = each(lambda k, b: k * b, ks, betas)
        kst = each(lambda k: stack_heads(k).astype(BF16), ks)
        a_raw = each(lambda x, y: _dot_nt(stack_heads(x).astype(BF16), y), kb, kst)
        qk_raw = each(lambda x, y: _dot_nt(stack_heads(x).astype(BF16), y), qs, kst)
        yield
        qk = each(lambda m, x, dc: jnp.where(m, x * dc, 0.0).astype(BF16), incl, qk_raw, decay)
        a = each(lambda m, x, dc: jnp.where(m, x * dc, 0.0), strict, a_raw, decay)
        tinv = each(lambda d, x: eye - jnp.where(pair_masks[d][0], x, 0.0), dirs, a)
        for lv in range(1, _log2(c)):
            ta = each(lambda d, t, x: _mm(t, jnp.where(pair_masks[d][lv], x, 0.0)), dirs, tinv, a)
            yield
            tat = each(_mm, ta, tinv)
            yield
            tinv = each(lambda t, x: t - x, tinv, tat)
        egc = each(jnp.exp, gcs)
        rhs = each(lambda v, b, x, e: jnp.concatenate([stack_heads(v * b), stack_heads(x * e)], axis=1),
                   vs, betas, kb, egc)
        sol = each(_mm, tinv, rhs)
        yield
        u2 = each(lambda x: fold_heads(x[:, 0:LANES]), sol)
        w2 = each(lambda x: fold_heads(x[:, LANES:2 * LANES]), sol)
        gl = each(lambda d, gc2: gc2[c - 1:c, :] if d == 0 else gc2[0:1, :], dirs, gcs)
        ktail = each(lambda k, g, gc2: (k * jnp.exp(g - gc2)).astype(BF16), ks, gl, gcs)
        qwu = each(lambda x, w, u: _dot(x, jnp.concatenate([stack_heads(w), stack_heads(u)], axis=1).astype(BF16)),
                   qk, w2, u2)
        kwu = each(lambda x, w, u: _dot_tn(x, jnp.concatenate([w, u], axis=1).astype(BF16)), ktail, w2, u2)
        yield
        q_eff = each(lambda q, e, x: (q * e - fold_heads(x[:, 0:LANES])).astype(BF16), qs, egc, qwu)
        m_neg = each(lambda x: jnp.where(same_head, -x[:, 0:LANES], 0.0).astype(BF16), kwu)
        o_loc = each(lambda x: fold_heads(x[:, LANES:2 * LANES]), qwu)
        s_loc = each(lambda x: jnp.where(same_head, x[:, LANES:2 * LANES], 0.0), kwu)
        egl = each(lambda g: jnp.broadcast_to(jnp.exp(g), (SUBLANES, LANES)), gl)
        out.extend(zip(q_eff, m_neg, o_loc, s_loc, egl))

    def chunk_rows(chunk, rows_per_chunk):
        return pl.ds(pl.multiple_of(chunk * rows_per_chunk, rows_per_chunk), rows_per_chunk)

    ctx_chunks = nc // c
    per_group = chunks_per_iter
    n_groups = n_chunks // per_group

    def chunks_at(step):
        return jnp.where(step < ctx_chunks, step + n // c, step - ctx_chunks), n_chunks - 1 - step

    def run_group(local_group, scan_group, states):
        dirs, chunks, qs, ks, vs, betas, gcs = [], [], [], [], [], [], []
        if local_group is not None:
            for g in range(per_group):
                for d, chunk in enumerate(chunks_at(per_group * local_group + g)):
                    rows = chunk_rows(chunk, c)
                    dirs.append(d)
                    chunks.append(chunk)
                    qs.append(q_ref[rows, :])
                    ks.append(k_ref[rows, :])
                    vs.append(v_ref[rows, :])
                    betas.append(bb_ref[d, rows, :])
                    gcs.append(gb_ref[d, rows, :])
        scan_chunks, scan_in = [], []
        if scan_group is not None:
            for g in range(per_group):
                step_chunks = chunks_at(per_group * scan_group + g)
                scan_chunks.append(step_chunks)
                scan_in.append([(qe_ref[d, chunk_rows(ch, c), :], mp_ref[d, chunk_rows(ch, 2 * c), :],
                                 ou_ref[d, chunk_rows(ch, c), :], nn_ref[d, chunk_rows(ch, 2 * c), :],
                                 egl_ref[d, chunk_rows(ch, SUBLANES), :]) for d, ch in enumerate(step_chunks)])
        scan_out = []

        def scan_step(states):
            loaded = scan_in[len(scan_out)]
            res = [_dot(jnp.concatenate([ld[0], ld[1]], axis=0), s2.astype(BF16)) for ld, s2 in zip(loaded, states)]
            scan_out.append([r[0:c] + ld[2] for r, ld in zip(res, loaded)])
            return tuple(s2 * ld[4][0:1] + r[c:3 * c] + ld[3] for s2, ld, r in zip(states, loaded, res))

        local_out = []
        stages = local_stages(dirs, qs, ks, vs, betas, gcs, local_out) if local_group is not None else iter(())
        n_stages = 2 * _log2(c) + 1
        every = max(1, n_stages // per_group)
        for stage, _ in enumerate(stages):
            if scan_group is not None and stage % every == 0 and len(scan_out) < per_group:
                states = scan_step(states)
        while scan_group is not None and len(scan_out) < per_group:
            states = scan_step(states)
        for d, chunk, (q_eff, m_neg, o_loc, s_loc, egl) in zip(dirs, chunks, local_out):
            qe_ref[d, chunk_rows(chunk, c), :] = q_eff
            mp_ref[d, chunk_rows(chunk, 2 * c), :] = m_neg
            ou_ref[d, chunk_rows(chunk, c), :] = o_loc
            nn_ref[d, chunk_rows(chunk, 2 * c), :] = s_loc
            egl_ref[d, chunk_rows(chunk, SUBLANES), :] = egl
        for step_chunks, outs in zip(scan_chunks, scan_out):
            for d, ch in enumerate(step_chunks):
                o_ref[d, chunk_rows(ch, c), :] = outs[d]
        return states

    zero_state = jnp.zeros((2 * c, 2 * c), F32)
    states = run_group(0, None, (zero_state, zero_state))
    states = lax.fori_loop(1, n_groups, lambda j, st: run_group(j, j - 1, st), states)
    run_group(None, n_groups - 1, states)

    ng = ng_ref[...]
    step = 256
    for r in range(0, l, step):
        o = o_ref[0, r:r + step, :] + o_ref[1, r:r + step, :]
        ms = _half_sums(o * o, lane_lo) * (1.0 / GDN_DIM)
        gate = qkvg_ref[0, r:r + step, 3 * LANES:4 * LANES]
        out_ref[0, r:r + step, :] = (o * lax.rsqrt(ms + EPS) * ng * _silu(gate)).astype(out_ref.dtype)


def _dot_sel_lhs(sel_bf16, x):
    hi, lo = _split2(x)
    return _dot(sel_bf16, hi) + _dot(sel_bf16, lo)


def _gdn_call(qkvg, ba, conv_w, gate_params, ng, n, nc):
    b, l, _ = qkvg.shape
    pairs = GDN_HEADS // 2
    n_chunks = l // GDN_CHUNK
    kern = functools.partial(_gdn_kernel, n=n, nc=nc, chunks_per_iter=GDN_CHUNKS_PER_GROUP)
    return pl.pallas_call(
        kern,
        grid=(b, pairs),
        in_specs=[
            pl.BlockSpec((1, l, 4 * LANES), lambda i, p: (i, 0, p)),
            pl.BlockSpec((1, l, LANES), lambda i, p: (i, 0, 0)),
            pl.BlockSpec((3, 4 * LANES), lambda i, p: (0, p)),
            pl.BlockSpec((2, LANES), lambda i, p: (0, 0)),
            pl.BlockSpec((1, LANES), lambda i, p: (0, 0)),
        ],
        out_specs=pl.BlockSpec((1, l, LANES), lambda i, p: (i, 0, p)),
        out_shape=jax.ShapeDtypeStruct((b, l, pairs * LANES), MIXER_OUT_DTYPE),
        scratch_shapes=[
            pltpu.VMEM((l + 5 * SUBLANES, 3 * LANES), F32),
            pltpu.VMEM((l, LANES), F32),
            pltpu.VMEM((l, LANES), F32),
            pltpu.VMEM((l, LANES), F32),
            pltpu.VMEM((2, l, LANES), F32),
            pltpu.VMEM((2, l, LANES), F32),
            pltpu.VMEM((2, l, LANES), BF16),
            pltpu.VMEM((2, 2 * l, LANES), BF16),
            pltpu.VMEM((2, l, LANES), F32),
            pltpu.VMEM((2, 2 * l, LANES), F32),
            pltpu.VMEM((2, n_chunks * SUBLANES, LANES), F32),
            pltpu.VMEM((2, l, LANES), F32),
        ],
        compiler_params=_cparams(("arbitrary", "arbitrary")),
        name="gated_deltanet",
    )(qkvg, ba, conv_w, gate_params, ng)


def _diff_kernel(*refs, key_start, n_sub, lam_init, aliased):
    if aliased:
        refs = refs[1:]
    q_ref, k_ref, vt_ref, lam_ref, ng_ref, o_ref = refs
    lp = lam_ref[...]
    lam = (jnp.exp(jnp.sum(lp[0:1] * lp[1:2], axis=-1, keepdims=True))
           - jnp.exp(jnp.sum(lp[2:3] * lp[3:4], axis=-1, keepdims=True)) + lam_init)
    lane = lax.broadcasted_iota(jnp.int32, (1, LANES), 1)
    halves = (lane < DIFF_DIM, lane >= DIFF_DIM)
    ng = ng_ref[...]
    k = k_ref[0, key_start:, :]
    vt = vt_ref[0, :, key_start:]
    tq = q_ref.shape[1] // n_sub

    def scores_of(i):
        q = q_ref[0, i * tq:(i + 1) * tq, :]
        return [_dot_nt(k, jnp.where(m, q, jnp.zeros_like(q))) for m in halves]

    ahead = scores_of(0)
    for i in range(n_sub):
        s = ahead
        if i + 1 < n_sub:
            ahead = scores_of(i + 1)
        e = [jnp.exp(x - jnp.max(x, axis=0, keepdims=True)) for x in s]
        pv = [_dot(vt, x.astype(BF16)) for x in e]
        parts = [x * (1.0 / jnp.sum(y, axis=0, keepdims=True)) for x, y in zip(pv, e)]
        ot = parts[0] - lam * parts[1]
        ot = ot * lax.rsqrt(jnp.mean(ot * ot, axis=0, keepdims=True) + EPS)
        o_ref[0, i * tq:(i + 1) * tq, :] = (ot.T * ng * (1.0 - lam_init)).astype(o_ref.dtype)


def _diff_call(dq, dk, dvt, lam_p, ng, lam_init, q_rows, first_block, n_q_blocks, key_start, n_sub, prev_out):
    b, l, _ = dq.shape
    aliased = prev_out is not None
    kern = functools.partial(_diff_kernel, key_start=key_start, n_sub=n_sub, lam_init=lam_init, aliased=aliased)
    row_of = lambda t: first_block + t
    in_specs = [
        pl.BlockSpec((1, q_rows, LANES), lambda i, h, t: (i, row_of(t), h)),
        pl.BlockSpec((1, l, LANES), lambda i, h, t: (i, 0, h)),
        pl.BlockSpec((1, LANES, l), lambda i, h, t: (i, h, 0)),
        pl.BlockSpec((4, DIFF_DIM), lambda i, h, t: (0, 0)),
        pl.BlockSpec((1, LANES), lambda i, h, t: (0, 0)),
    ]
    args = [dq, dk, dvt, lam_p, ng.reshape(1, LANES)]
    aliases = {}
    if aliased:
        in_specs = [pl.BlockSpec(memory_space=pl.ANY)] + in_specs
        args = [prev_out] + args
        aliases = {0: 0}
    return pl.pallas_call(
        kern,
        grid=(b, DIFF_HEADS, n_q_blocks),
        in_specs=in_specs,
        out_specs=pl.BlockSpec((1, q_rows, LANES), lambda i, h, t: (i, row_of(t), h)),
        out_shape=jax.ShapeDtypeStruct((b, l, DIFF_HEADS * LANES), MIXER_OUT_DTYPE),
        input_output_aliases=aliases,
        compiler_params=_cparams(("arbitrary", "arbitrary", "arbitrary")),
        name="diff_attention_ctx" if aliased else "diff_attention",
    )(*args)


def _swa_kernel(q_ref, k_ref, v_ref, sink_ref, o_ref, *, n, nc):
    t = pl.program_id(1)
    blk = SWA_BLOCK
    n_x = n // blk
    q = q_ref[0]
    lane = lax.broadcasted_iota(jnp.int32, (1, LANES), 1)
    lane_lo = lane < HEAD_DIM
    sink = sink_ref[...]
    group = SWA_HEADS // SWA_KV_HEADS

    def run(keys, vals, valid):
        head_of_row = lax.broadcasted_iota(jnp.int32, (group * blk, 1), 0) >> _log2(blk)
        kvs = range(SWA_KV_HEADS)
        kk = [keys[:, kvh * LANES:(kvh + 1) * LANES] for kvh in kvs]
        vv = [vals[:, kvh * LANES:(kvh + 1) * LANES] for kvh in kvs]
        qst, sk = [], []
        for kvh in kvs:
            q_rows = []
            sk_rows = jnp.zeros((group * blk, 1), F32)
            for g in range(group):
                h = kvh * group + g
                qp = q[:, (h // 2) * LANES:(h // 2 + 1) * LANES]
                q_rows.append(jnp.where(lane_lo if h % 2 == 0 else jnp.logical_not(lane_lo), qp, jnp.zeros_like(qp)))
                sk_rows = jnp.where(head_of_row == g, sink[:, h:h + 1], sk_rows)
            qst.append(jnp.concatenate(q_rows, axis=0))
            sk.append(sk_rows)
        s = [_dot_nt(x, y) for x, y in zip(qst, kk)]
        if valid is not None:
            s = [jnp.where(valid, x, NEG_INF) for x in s]
        mx = [jnp.maximum(jnp.max(x, axis=-1, keepdims=True), y) for x, y in zip(s, sk)]
        e = [jnp.exp(x - m) for x, m in zip(s, mx)]
        pv = [_dot(x.astype(BF16), y) for x, y in zip(e, vv)]
        den = [jnp.sum(x, axis=-1, keepdims=True) + jnp.exp(y - m) for x, y, m in zip(e, sk, mx)]
        outs = []
        for o, dn in zip(pv, den):
            o = o * (1.0 / dn)
            for g in range(0, group, 2):
                outs.append(jnp.where(lane_lo, o[g * blk:(g + 1) * blk], o[(g + 1) * blk:(g + 2) * blk]))
        o_ref[0] = jnp.concatenate(outs, axis=1).astype(o_ref.dtype)

    @pl.when(t < n_x)
    def _():
        start = pl.multiple_of(jnp.clip((t - 1) * blk, 0, n - 3 * blk), blk)
        keys = jnp.concatenate([k_ref[0, pl.ds(start, 3 * blk), :], k_ref[0, n:n + nc, :]], axis=0)
        vals = jnp.concatenate([v_ref[0, pl.ds(start, 3 * blk), :], v_ref[0, n:n + nc, :]], axis=0)
        shape = (group * blk, 3 * blk + nc)
        qpos = t * blk + (lax.broadcasted_iota(jnp.int32, shape, 0) & (blk - 1))
        col = lax.broadcasted_iota(jnp.int32, shape, 1)
        dist = qpos - (start + col)
        in_window = jnp.logical_and(dist <= SWA_WINDOW, dist >= -SWA_WINDOW)
        valid = jnp.logical_or(col >= 3 * blk, in_window)
        run(keys, vals, valid)

    @pl.when(t >= n_x)
    def _():
        run(k_ref[0, n:n + nc, :], v_ref[0, n:n + nc, :], None)


def _swa_call(q, k, v, sink, n, nc, with_ctx):
    b, l, _ = q.shape
    blk = SWA_BLOCK
    nt = (l if with_ctx else n) // blk
    kern = functools.partial(_swa_kernel, n=n, nc=nc)
    return pl.pallas_call(
        kern,
        grid=(b, nt),
        in_specs=[
            pl.BlockSpec((1, blk, SWA_HEADS * HEAD_DIM), lambda i, t: (i, t, 0)),
            pl.BlockSpec((1, l, 2 * LANES), lambda i, t: (i, 0, 0)),
            pl.BlockSpec((1, l, 2 * LANES), lambda i, t: (i, 0, 0)),
            pl.BlockSpec((1, LANES), lambda i, t: (0, 0)),
        ],
        out_specs=pl.BlockSpec((1, blk, SWA_HEADS * HEAD_DIM), lambda i, t: (i, t, 0)),
        out_shape=jax.ShapeDtypeStruct((b, l, SWA_HEADS * HEAD_DIM), MIXER_OUT_DTYPE),
        compiler_params=_cparams(("arbitrary", "arbitrary")),
        name="window_attention",
    )(q, k, v, sink)


def _dft_tables(n):
    h = n // 2
    r = 1 << (_log2(h) // 2)
    j = jnp.arange(h, dtype=jnp.int32)

    def tables(m):
        thin = lambda k: ((k[:, None] * m[None, :]) % (2 * n)).astype(F32) * (math.pi / n)
        a = thin(r * jnp.arange(h // r, dtype=jnp.int32))[:, None, :]
        b = thin(jnp.arange(r, dtype=jnp.int32))[None, :, :]
        cos = jnp.cos(a) * jnp.cos(b) - jnp.sin(a) * jnp.sin(b)
        sin = jnp.sin(a) * jnp.cos(b) + jnp.cos(a) * jnp.sin(b)
        return cos.reshape(h, h).astype(BF16), (-sin).reshape(h, h).astype(BF16)

    ce, se = tables(2 * j)
    co, so = tables(2 * j + 1)
    return ce, se, co, so, co.T, so.T


def _hyena_filter_kernel(feat_ref, w1_ref, b1_ref, w2_ref, b2_ref, freq_ref, w3f_ref, w3b_ref, dl_ref,
                         ce_ref, se_ref, co_ref, so_ref, ka_ref, kb_ref, km_ref):
    h = feat_ref.shape[1]
    assert h % 2 == 0
    freq = freq_ref[...]
    dl = dl_ref[...]
    row = lax.broadcasted_iota(jnp.int32, (h, 1), 0)

    def taps(part, w3_ref):
        feat = feat_ref[part]
        x = jnp.sin(freq[0:1] * (_dot_f32(feat, w1_ref[...]) + b1_ref[...]))
        x = jnp.sin(freq[1:2] * (_dot_f32(x, w2_ref[...]) + b2_ref[...]))
        return _dot_f32(x, w3_ref[...]) * jnp.exp(-feat[:, 0:1] * dl)

    fe, fo = taps(0, w3f_ref), taps(1, w3f_ref)
    be, bo = jnp.where(row == 0, 0.0, taps(2, w3b_ref)), taps(3, w3b_ref)
    ss = sum(jnp.sum(x * x, axis=0, keepdims=True) for x in (fe, fo, be, bo))
    sc = lax.rsqrt(ss + EPS)
    fe, fo, be, bo = (x * sc for x in (fe, fo, be, bo))
    sgn = jnp.where((row & 1) == 0, 1.0, -1.0)

    def dot2(t_ref, x):
        hi, lo = _split2(x)
        return _dot(t_ref[...], hi) + _dot(t_ref[...], lo)

    def bins(x_even, x_odd):
        ce, co = dot2(ce_ref, x_even), dot2(co_ref, x_odd)
        se, so = dot2(se_ref, x_even), dot2(so_ref, x_odd)
        return ce + co, se + so, ce - co, so - se

    f = bins(fe, fo)
    g = bins(be, bo)
    ka_ref[0, 0], ka_ref[0, 1], kb_ref[0, 0], kb_ref[0, 1] = (x + sgn * y for x, y in zip(f, g))
    mid_r = jnp.sum((fe + be) * sgn, axis=0, keepdims=True)
    mid_i = -jnp.sum((fo + bo) * sgn, axis=0, keepdims=True)
    km_ref[0] = jnp.concatenate([mid_r, mid_i, jnp.zeros((SUBLANES - 2, mid_r.shape[-1]), F32)], axis=0)


def _hyena_filter_call(feats, w1, b1, w2, b2, freq, w3, deltas, tables):
    h = feats.shape[1]
    hid = w2.shape[0]
    ch = deltas.shape[-1]
    tc = MXU_WIDTH
    nct = ch // tc
    const = lambda shape: pl.BlockSpec(shape, lambda o, j: (0,) * len(shape))
    spectrum = pl.BlockSpec((1, 2, h, tc), lambda o, j: (o, 0, 0, j))
    return pl.pallas_call(
        _hyena_filter_kernel,
        grid=(2, nct),
        in_specs=[
            const((4, h, hid)), const((hid, hid)), const((1, hid)), const((hid, hid)), const((1, hid)),
            const((2, hid)),
            pl.BlockSpec((hid, tc), lambda o, j: (0, (2 * o) * nct + j)),
            pl.BlockSpec((hid, tc), lambda o, j: (0, (2 * o + 1) * nct + j)),
            pl.BlockSpec((1, tc), lambda o, j: (0, j)),
        ] + [_resident((h, h))] * 4,
        out_specs=[spectrum, spectrum, pl.BlockSpec((1, SUBLANES, tc), lambda o, j: (o, 0, j))],
        out_shape=[
            jax.ShapeDtypeStruct((2, 2, h, ch), F32),
            jax.ShapeDtypeStruct((2, 2, h, ch), F32),
            jax.ShapeDtypeStruct((2, SUBLANES, ch), F32),
        ],
        compiler_params=_cparams(("arbitrary", "arbitrary")),
        name="hyena_filters",
    )(feats, w1, b1, w2, b2, freq, w3, w3, deltas, *tables[:4])


def _hyena_kernel(*refs, n, aliased):
    if aliased:
        refs = refs[1:]
    (v_ref, x1_ref, x2_ref, cwv_ref, cw1_ref, cw2_ref, ka_ref, kb_ref, km_ref, bias_ref,
     ce_ref, se_ref, co_ref, so_ref, cot_ref, sot_ref, o_ref, pad_ref, z_ref, zb_ref, p_ref, y_ref) = refs
    halo = SUBLANES
    tc = o_ref.shape[-1]
    h = n // 2
    rc = min(h, HY_ROW_CHUNK)
    lane_groups = tc // LANES
    zero_rows = jnp.zeros((halo, LANES), F32)
    for g in range(lane_groups):
        pad_ref[g, 0:halo, :] = zero_rows
        pad_ref[g, halo + n:2 * halo + n, :] = zero_rows

    def stage(ref):
        for r in range(0, n, 2 * rc):
            for g in range(lane_groups):
                pad_ref[g, halo + r:halo + r + 2 * rc, :] = ref[0, r:r + 2 * rc, g * LANES:(g + 1) * LANES]

    def conv_rows(cw, parity, r):
        first = halo + 2 * r + parity - 1
        taps = [jnp.concatenate([pad_ref[g, pl.ds(first + i, rc, stride=2), :] for g in range(lane_groups)], axis=1)
                for i in range(3)]
        return cw[0:1] * taps[0] + cw[1:2] * taps[1] + cw[2:3] * taps[2]

    def alt_sign(r):
        j = r + lax.broadcasted_iota(jnp.int32, (rc, 1), 0)
        return j, jnp.where((j & 1) == 0, 1.0, -1.0)

    stage(v_ref)
    cw = cwv_ref[...]
    for parity in range(2):
        for r in range(0, h, rc):
            z = conv_rows(cw, parity, r)
            z_ref[parity, r:r + rc, :] = z
            zb_ref[parity, r:r + rc, :] = z.astype(BF16)

    for o, (gate_ref, gate_cw_ref) in enumerate(((x1_ref, cw1_ref), (x2_ref, cw2_ref))):
        mid_r = jnp.zeros((1, tc), F32)
        mid_i = jnp.zeros((1, tc), F32)
        for r in range(0, h, rc):
            _, sgn = alt_sign(r)
            mid_r = mid_r + jnp.sum(z_ref[0, r:r + rc, :] * sgn, axis=0, keepdims=True)
            mid_i = mid_i - jnp.sum(z_ref[1, r:r + rc, :] * sgn, axis=0, keepdims=True)
        km_r = km_ref[o, 0:1, :]
        km_i = km_ref[o, 1:2, :]
        pm_r = (mid_r * km_r - mid_i * km_i) * (1.0 / n)
        pm_i = (mid_r * km_i + mid_i * km_r) * (1.0 / n)
        ze = zb_ref[0]
        zo = zb_ref[1]
        for r in range(0, h, rc):
            k, _ = alt_sign(r)
            rows = slice(r, r + rc)
            ce, co = _dot(ce_ref[rows, :], ze), _dot(co_ref[rows, :], zo)
            se, so = _dot(se_ref[rows, :], ze), _dot(so_ref[rows, :], zo)
            xa_r, xb_r, xa_i, xb_i = ce + co, ce - co, se + so, so - se
            wgt = jnp.where(k == 0, 0.5 / n, 1.0 / n)
            ka_r, ka_i = ka_ref[o, 0, rows, :], ka_ref[o, 1, rows, :]
            kb_r, kb_i = kb_ref[o, 0, rows, :], kb_ref[o, 1, rows, :]
            pa_r = (xa_r * ka_r - xa_i * ka_i) * wgt
            pa_i = (xa_r * ka_i + xa_i * ka_r) * wgt
            pb_r = (xb_r * kb_r - xb_i * kb_i) * wgt
            pb_i = (xb_r * kb_i + xb_i * kb_r) * wgt
            p_ref[0, rows, :] = (pa_r + pb_r).astype(BF16)
            p_ref[1, rows, :] = (pa_i - pb_i).astype(BF16)
            p_ref[2, rows, :] = (pa_r - pb_r).astype(BF16)
            p_ref[3, rows, :] = (pa_i + pb_i).astype(BF16)
        stage(gate_ref)
        cw = gate_cw_ref[...]
        bias = bias_ref[o:o + 1, :]
        for parity, (c_ref, s_ref, mid) in enumerate(((ce_ref, se_ref, pm_r), (cot_ref, sot_ref, -pm_i))):
            for r in range(0, h, rc):
                _, sgn = alt_sign(r)
                rows = slice(r, r + rc)
                y = (_dot(c_ref[rows, :], p_ref[2 * parity]) + _dot(s_ref[rows, :], p_ref[2 * parity + 1])
                     + sgn * mid)
                z = conv_rows(cw, parity, r) * (y + z_ref[parity, rows, :] * bias)
                if o == 0:
                    z_ref[parity, rows, :] = z
                    zb_ref[parity, rows, :] = z.astype(BF16)
                else:
                    for g in range(lane_groups):
                        y_ref[g, pl.ds(2 * r + parity, rc, stride=2), :] = z[:, g * LANES:(g + 1) * LANES]
    for r in range(0, n, 2 * rc):
        rows = slice(r, r + 2 * rc)
        o_ref[0, rows, :] = jnp.concatenate([y_ref[g, rows, :] for g in range(lane_groups)],
                                            axis=1).astype(o_ref.dtype)


def _hyena_call(u, conv_w, ka, kb, km, bias, tables, n, row_block, prev_out):
    b, l, _ = u.shape
    ch = bias.shape[-1]
    tc = MXU_WIDTH
    nct = ch // tc
    h = n // 2
    aliased = prev_out is not None
    kern = functools.partial(_hyena_kernel, n=n, aliased=aliased)
    once = pl.Buffered(1)
    in_specs = [
        pl.BlockSpec((1, n, tc), lambda j, i: (i, row_block, j)),
        pl.BlockSpec((1, n, tc), lambda j, i: (i, row_block, nct + j)),
        pl.BlockSpec((1, n, tc), lambda j, i: (i, row_block, 2 * nct + j)),
        pl.BlockSpec((3, tc), lambda j, i: (0, j)),
        pl.BlockSpec((3, tc), lambda j, i: (0, nct + j)),
        pl.BlockSpec((3, tc), lambda j, i: (0, 2 * nct + j)),
        pl.BlockSpec((2, 2, h, tc), lambda j, i: (0, 0, 0, j), pipeline_mode=once),
        pl.BlockSpec((2, 2, h, tc), lambda j, i: (0, 0, 0, j), pipeline_mode=once),
        pl.BlockSpec((2, SUBLANES, tc), lambda j, i: (0, 0, j)),
        pl.BlockSpec((2, tc), lambda j, i: (0, j)),
    ] + [_resident((h, h))] * 6
    args = [u, u, u, conv_w, conv_w, conv_w, ka, kb, km, bias, *tables]
    aliases = {}
    if aliased:
        in_specs = [pl.BlockSpec(memory_space=pl.ANY)] + in_specs
        args = [prev_out] + args
        aliases = {0: 0}
    return pl.pallas_call(
        kern,
        grid=(nct, b),
        in_specs=in_specs,
        out_specs=pl.BlockSpec((1, n, tc), lambda j, i: (i, row_block, j)),
        out_shape=jax.ShapeDtypeStruct((b, l, ch), MIXER_OUT_DTYPE),
        scratch_shapes=[
            pltpu.VMEM((tc // LANES, n + 2 * SUBLANES, LANES), F32),
            pltpu.VMEM((2, h, tc), F32),
            pltpu.VMEM((2, h, tc), BF16),
            pltpu.VMEM((4, h, tc), BF16),
            pltpu.VMEM((tc // LANES, n, LANES), F32),
        ],
        input_output_aliases=aliases,
        compiler_params=_cparams(("arbitrary", "arbitrary")),
        name="hyena_conv_n%d" % n,
    )(*args)


def _rope_tables(n, nc):
    rows = n // GRID_W
    row = jnp.repeat(jnp.arange(rows, dtype=F32), GRID_W)
    col = jnp.tile(jnp.arange(GRID_W, dtype=F32), rows)
    half = HEAD_DIM // 2
    inv = ROPE_BASE ** (-jnp.arange(0, half, 2, dtype=F32) / half)
    ar = row[:, None] * inv
    ac = col[:, None] * inv
    cos = jnp.concatenate([jnp.cos(ar), jnp.cos(ar), jnp.cos(ac), jnp.cos(ac)], axis=-1)
    sin = jnp.concatenate([-jnp.sin(ar), jnp.sin(ar), -jnp.sin(ac), jnp.sin(ac)], axis=-1)
    cos = jnp.concatenate([cos, jnp.ones((nc, HEAD_DIM), F32)], axis=0)
    sin = jnp.concatenate([sin, jnp.zeros((nc, HEAD_DIM), F32)], axis=0)
    return jnp.tile(cos, (1, LANES // HEAD_DIM)), jnp.tile(sin, (1, LANES // HEAD_DIM))


def _rope_partner_cols(width):
    d = np.arange(width)
    quarter = HEAD_DIM // 4
    return np.where((d % (2 * quarter)) < quarter, d + quarter, d - quarter)


def _hyena_feats(n):
    pos = jnp.arange(n, dtype=F32)
    t = pos / max(n - 1, 1)
    ang = (2.0 * math.pi * pos / n)[:, None] * jnp.linspace(1e-4, HY_BANDS - 1, HY_BANDS, dtype=F32)[None, :]
    feats = jnp.concatenate([t[:, None], jnp.cos(ang), -jnp.sin(ang)], axis=-1)
    feats = jnp.pad(feats, ((0, 0), (0, 64 - feats.shape[-1])))
    back = jnp.concatenate([feats[0:1], jnp.flip(feats[1:], axis=0)], axis=0)
    return jnp.stack([feats[0::2], feats[1::2], back[0::2], back[1::2]])


def _pad_cols(w, width):
    return jnp.pad(w, ((0, 0), (0, width - w.shape[-1])))


def _layer_ab(xz, mod, norm_g0, w_in, conv_w, a_log, dt_bias, gdn_g, lam_p, diff_g, lam_init, rope, n, nc):
    hd = GDN_HEADS * GDN_DIM
    wq, wk, wv, wg = (w_in[:, i * hd:(i + 1) * hd] for i in range(4))
    o = 4 * hd
    w_beta, w_alpha = w_in[:, o:o + 16], w_in[:, o + 16:o + 32]
    o += 32
    dd = DIFF_HEADS * 2 * DIFF_DIM
    wdq, wdk, wdv = (w_in[:, o + i * dd:o + (i + 1) * dd] for i in range(3))
    pairs = GDN_HEADS // 2
    pair_cols = lambda w: [w[:, p * LANES:(p + 1) * LANES] for p in range(pairs)]
    w_qkvg = jnp.concatenate([blk for grp in zip(pair_cols(wq), pair_cols(wk), pair_cols(wv), pair_cols(wg))
                              for blk in grp], axis=1)
    perm = _rope_partner_cols(dd)
    w_all = jnp.concatenate([w_qkvg, _pad_cols(jnp.concatenate([w_beta, w_alpha], axis=1), LANES),
                             wdq, wdk, wdv, wdq[:, perm], wdk[:, perm]], axis=1).astype(BF16)
    c0 = 4 * hd
    c1 = c0 + LANES
    segs = (_Seg(0, c0), _Seg(c0, LANES),
            _Seg(c1, dd, rot_start=c1 + 3 * dd, scale=DIFF_DIM ** -0.5, dtype=BF16),
            _Seg(c1 + dd, dd, rot_start=c1 + 4 * dd, dtype=BF16),
            _Seg(c1 + 2 * dd, dd, dtype=BF16, transposed=True))
    qkvg, ba, dq, dk, dvt = _proj_call(xz, mod, norm_g0, w_all, rope[0], rope[1], segs, n // ROW_TILE, "proj_ab")

    cq, ck, cv = (conv_w[:, i * hd:(i + 1) * hd] for i in range(3))
    zeros = jnp.zeros((3, LANES), F32)
    conv_l = jnp.concatenate([blk for p in range(pairs) for blk in
                              (cq[:, p * LANES:(p + 1) * LANES], ck[:, p * LANES:(p + 1) * LANES],
                               cv[:, p * LANES:(p + 1) * LANES], zeros)], axis=1)
    n_gate = 2 * GDN_HEADS
    on_decay_lanes = lambda t: jnp.pad(t.reshape(1, n_gate), ((0, 0), (n_gate, LANES - 2 * n_gate)))
    gate_params = jnp.concatenate([on_decay_lanes(a_log), on_decay_lanes(dt_bias)], axis=0)
    ng = jnp.tile(gdn_g.reshape(1, GDN_DIM), (1, 2))
    oa = _gdn_call(qkvg, ba, conv_l, gate_params, ng, n, nc)
    q_rows = DIFF_SUB_TILES * ROW_TILE
    ob = _diff_call(dq, dk, dvt, lam_p, diff_g, lam_init, q_rows, 0, n // q_rows, 0, DIFF_SUB_TILES, None)
    ob = _diff_call(dq, dk, dvt, lam_p, diff_g, lam_init, nc, n // nc, 1, n, 1, ob)
    return oa, ob


def _layer_cd(xz, mod, norm_g0, w_in, sink, hy_conv, hy_w1, hy_b1, hy_w2, hy_b2, hy_w3, hy_freq, hy_bias,
              rope, n, nc, last, dft_x, dft_c):
    qd = SWA_HEADS * HEAD_DIM
    kd = SWA_KV_HEADS * HEAD_DIM
    wq, wk, wv, wu = w_in[:, 0:qd], w_in[:, qd:qd + kd], w_in[:, qd + kd:qd + 2 * kd], w_in[:, qd + 2 * kd:]
    dup = lambda w: jnp.concatenate([w[:, 0:HEAD_DIM], w[:, 0:HEAD_DIM], w[:, HEAD_DIM:], w[:, HEAD_DIM:]], axis=1)
    wk2, wv2 = dup(wk), dup(wv)
    ud = wu.shape[1]
    w_all = jnp.concatenate([wq, wk2, wv2, wu, wq[:, _rope_partner_cols(qd)], wk2[:, _rope_partner_cols(2 * kd)]],
                            axis=1).astype(BF16)
    o_u = qd + 4 * kd
    segs = (_Seg(0, qd, rot_start=o_u + ud, scale=HEAD_DIM ** -0.5, dtype=BF16),
            _Seg(qd, 2 * kd, rot_start=o_u + ud + qd, dtype=BF16),
            _Seg(qd + 2 * kd, 2 * kd, dtype=BF16), _Seg(o_u, ud))
    q, k, v, u = _proj_call(xz, mod, norm_g0, w_all, rope[0], rope[1], segs, n // ROW_TILE, "proj_cd")
    oc = _swa_call(q, k, v, _pad_cols(sink.reshape(1, SWA_HEADS), LANES), n, nc, not last)

    ch = hy_bias.shape[-1]
    deltas = jnp.abs(jnp.linspace(HY_MIN_DECAY, HY_MAX_DECAY, ch, dtype=F32)).reshape(1, ch)
    hid = hy_w2.shape[0]
    w1p = jnp.pad(hy_w1, ((0, hid - hy_w1.shape[0]), (0, 0)))
    filt = lambda m, dft: _hyena_filter_call(_hyena_feats(m), w1p, hy_b1.reshape(1, hid), hy_w2,
                                             hy_b2.reshape(1, hid), hy_freq, hy_w3, deltas, dft)
    od = _hyena_call(u, hy_conv, *filt(n, dft_x), hy_bias, dft_x, n, 0, None)
    if not last:
        od = _hyena_call(u, hy_conv, *filt(nc, dft_c), hy_bias, dft_c, nc, n // nc, od)
    return oc, od


def kernel(x, c, ctx, c_ctx, w_mod, b_mod, norm_g, ffn_w_up, ffn_conv, ffn_w_down, ab_w_in, ab_w_out, gdn_conv, gdn_a_log, gdn_dt_bias, gdn_norm_g, diff_lambda, diff_norm_g, cd_w_in, cd_w_out, swa_sink, hy_conv, hy_w1, hy_b1, hy_w2, hy_b2, hy_w3, hy_freq, hy_bias):
    b, n, d = x.shape
    nc = ctx.shape[1]
    depth = w_mod.shape[0]
    assert n % ROW_TILE == 0 and nc == ROW_TILE and n % GRID_W == 0
    xz = jnp.concatenate([x, ctx], axis=1)
    rows = -(-(b + 1) // SUBLANES) * SUBLANES
    cc = jnp.concatenate([c, c_ctx[None], jnp.zeros((rows - b - 1, d), F32)], axis=0)
    mods = _mod_call(cc, w_mod, b_mod)
    mod_all = jnp.concatenate([mods[:, :b].reshape(depth, b, 1, 6, d),
                               jnp.broadcast_to(mods[:, b].reshape(depth, 1, 1, 6, d), (depth, b, 1, 6, d))], axis=2)
    rope = _rope_tables(n, nc)
    dft_x = _dft_tables(n)
    dft_c = _dft_tables(nc)
    n_x_tiles = n // ROW_TILE
    for l in range(depth):
        last = l == depth - 1
        i = l // 2
        mod = mod_all[l]
        if l % 2 == 0:
            lam_init = 0.8 - 0.6 * math.exp(-0.3 * l)
            o1, o2 = _layer_ab(xz, mod, norm_g[l, 0], ab_w_in[i], gdn_conv[i], gdn_a_log[i], gdn_dt_bias[i],
                               gdn_norm_g[i], diff_lambda[i], diff_norm_g[i], lam_init, rope, n, nc)
            w_out = ab_w_out[i]
        else:
            o1, o2 = _layer_cd(xz, mod, norm_g[l, 0], cd_w_in[i], swa_sink[i], hy_conv[i], hy_w1[i], hy_b1[i],
                               hy_w2[i], hy_b2[i], hy_w3[i], hy_freq[i], hy_bias[i], rope, n, nc, last, dft_x, dft_c)
            w_out = cd_w_out[i]
        n_tiles = (n if last else n + nc) // ROW_TILE
        xz = _post_call(o1, o2, xz, mod, norm_g[l, 1], norm_g[l, 2], norm_g[l, 3], w_out.astype(BF16),
                        ffn_w_up[l].astype(BF16), ffn_conv[l], ffn_w_down[l].astype(BF16), n_tiles, n_x_tiles)
    return xz
```

```python
import functools
import math
from typing import NamedTuple, Optional

import jax
import jax.numpy as jnp
import numpy as np
from jax import lax
from jax.experimental import pallas as pl
from jax.experimental.pallas import tpu as pltpu

F32 = jnp.float32
BF16 = jnp.bfloat16
MIXER_OUT_DTYPE = BF16

EPS = 1e-6
NEG_INF = -1e30
GRID_W = 64
HEAD_DIM = 64
ROPE_BASE = 10000.0
GDN_HEADS = 8
GDN_DIM = 64
GDN_CHUNK = 64
GDN_CHUNKS_PER_GROUP = 9
DIFF_HEADS = 4
DIFF_DIM = 64
DIFF_SUB_TILES = 8
SWA_HEADS = 8
SWA_KV_HEADS = 2
SWA_WINDOW = 128
SWA_BLOCK = 128
HY_BANDS = 16
HY_MIN_DECAY = math.log(1e-2) / 1.5
HY_MAX_DECAY = math.log(1e-2) / 0.3
HY_ROW_CHUNK = 512

LANES = 128
SUBLANES = 8
MXU_WIDTH = 256
FFN_COL_CHUNK = 11 * MXU_WIDTH
ROW_TILE = 256
VMEM_LIMIT = 56 * 1024 * 1024


def _cparams(sem):
    return pltpu.CompilerParams(dimension_semantics=sem, vmem_limit_bytes=VMEM_LIMIT)


def _resident(shape):
    zeros = (0,) * len(shape)
    return pl.BlockSpec(shape, lambda *_: zeros, pipeline_mode=pl.Buffered(1))


def _log2(v):
    assert v & (v - 1) == 0
    return v.bit_length() - 1


def _sigmoid(x):
    return 1.0 / (1.0 + jnp.exp(-x))


def _silu(x):
    return x * _sigmoid(x)


def _softplus(x):
    return jnp.maximum(x, 0.0) + jnp.log1p(jnp.exp(-jnp.abs(x)))


def _dot(a, b):
    return jnp.dot(a, b, preferred_element_type=F32)


def _dot_nt(a, b):
    return lax.dot_general(a, b, (((1,), (1,)), ((), ())), preferred_element_type=F32)


def _dot_tn(a, b):
    return lax.dot_general(a, b, (((0,), (0,)), ((), ())), preferred_element_type=F32)


def _dot_f32(a, b):
    return jnp.dot(a, b, preferred_element_type=F32, precision=lax.Precision.HIGHEST)


def _split2(x):
    hi = x.astype(BF16)
    lo = (x - hi.astype(F32)).astype(BF16)
    return hi, lo


def _dot_sel(x, sel_bf16):
    hi, lo = _split2(x)
    return _dot(hi, sel_bf16) + _dot(lo, sel_bf16)


def _mm(a, b):
    return _dot(a.astype(BF16), b.astype(BF16))


def _rms(y, g):
    return y * lax.rsqrt(jnp.mean(y * y, axis=-1, keepdims=True) + EPS) * g


def _modnorm(x, g, shift, scale):
    return _rms(x, g) * (1.0 + scale) + shift


def _mod_kernel(cc_ref, w_ref, b_ref, o_ref):
    s = _silu(cc_ref[...])
    o_ref[0] = _dot(s.astype(BF16), w_ref[0].astype(BF16)) + b_ref[0]


def _mod_call(cc, w_mod, b_mod):
    depth, d, nm = w_mod.shape
    rows = cc.shape[0]
    ct = 1536
    return pl.pallas_call(
        _mod_kernel,
        grid=(depth, nm // ct),
        in_specs=[
            pl.BlockSpec((rows, d), lambda l, j: (0, 0)),
            pl.BlockSpec((1, d, ct), lambda l, j: (l, 0, j)),
            pl.BlockSpec((1, 1, ct), lambda l, j: (l, 0, j)),
        ],
        out_specs=pl.BlockSpec((1, rows, ct), lambda l, j: (l, 0, j)),
        out_shape=jax.ShapeDtypeStruct((depth, rows, nm), F32),
        compiler_params=_cparams(("arbitrary", "arbitrary")),
        name="adaln_mod",
    )(cc, w_mod, b_mod.reshape(depth, 1, nm))


class _Seg(NamedTuple):
    start: int
    width: int
    rot_start: Optional[int] = None
    scale: float = 1.0
    dtype: type = F32
    transposed: bool = False


def _proj_kernel(x_ref, mod_ref, g_ref, w_ref, cos_ref, sin_ref, *out_refs, segs):
    m = mod_ref[0, 0]
    h = _modnorm(x_ref[0], g_ref[...], m[0:1], m[1:2]).astype(BF16)
    for o_ref, seg in zip(out_refs, segs):
        y = _dot(h, w_ref[:, seg.start:seg.start + seg.width])
        if seg.rot_start is not None:
            yr = _dot(h, w_ref[:, seg.rot_start:seg.rot_start + seg.width])
            reps = seg.width // LANES
            cos = jnp.concatenate([cos_ref[...]] * reps, axis=1)
            sin = jnp.concatenate([sin_ref[...]] * reps, axis=1)
            y = y * cos + yr * sin
        if seg.scale != 1.0:
            y = y * seg.scale
        if seg.transposed:
            y = y.T
        o_ref[0] = y.astype(seg.dtype)


def _proj_call(xz, mod, g, w, cos_t, sin_t, segs, n_x_tiles, name):
    b, l, d = xz.shape
    tm = ROW_TILE
    nt = l // tm
    p = w.shape[1]
    return pl.pallas_call(
        functools.partial(_proj_kernel, segs=segs),
        grid=(nt, b),
        in_specs=[
            pl.BlockSpec((1, tm, d), lambda t, i: (i, t, 0)),
            pl.BlockSpec((1, 1, 6, d), lambda t, i: (i, t // n_x_tiles, 0, 0)),
            pl.BlockSpec((1, d), lambda t, i: (0, 0)),
            _resident((d, p)),
            pl.BlockSpec((tm, LANES), lambda t, i: (t, 0)),
            pl.BlockSpec((tm, LANES), lambda t, i: (t, 0)),
        ],
        out_specs=[pl.BlockSpec((1, s.width, tm), lambda t, i: (i, 0, t)) if s.transposed
                   else pl.BlockSpec((1, tm, s.width), lambda t, i: (i, t, 0)) for s in segs],
        out_shape=[jax.ShapeDtypeStruct((b, s.width, l) if s.transposed else (b, l, s.width), s.dtype)
                   for s in segs],
        compiler_params=_cparams(("arbitrary", "arbitrary")),
        name=name,
    )(xz, mod, g.reshape(1, d), w, cos_t, sin_t)


def _post_kernel(o1p_ref, o1_ref, o1n_ref, o2p_ref, o2_ref, o2n_ref, xp_ref, x_ref, xn_ref, mod_ref,
                 g1_ref, g2_ref, g3_ref, wout_ref, wup_ref, cw_ref, wdn_ref, out_ref, up_ref,
                 *, tm, n_x_tiles, n_tiles, cf, dff):
    t = pl.program_id(0)
    first = jnp.logical_or(t == 0, t == n_x_tiles)
    last = jnp.logical_or(t == n_x_tiles - 1, t == n_tiles - 1)
    m = mod_ref[0, 0]
    halo = SUBLANES
    ohalo = o1p_ref.shape[1]
    k1 = o1_ref.shape[-1]
    o1e = jnp.concatenate([o1p_ref[0], o1_ref[0], o1n_ref[0]], axis=0)
    o2e = jnp.concatenate([o2p_ref[0], o2_ref[0], o2n_ref[0]], axis=0)
    y = _dot(o1e, wout_ref[0:k1, :]) + _dot(o2e, wout_ref[k1:, :])
    y = y[ohalo - halo:ohalo + tm + halo]
    xe = jnp.concatenate([xp_ref[0], x_ref[0], xn_ref[0]], axis=0)
    x1 = xe + m[2:3] * _rms(y, g1_ref[...])
    h = _modnorm(x1, g2_ref[...], m[3:4], m[4:5]).astype(BF16)
    acc = jnp.zeros((tm, x_ref.shape[-1]), F32)
    for c0 in range(0, dff, cf):
        wd = min(cf, dff - c0)
        halves = []
        for half, base in enumerate((c0, dff + c0)):
            u = _dot(h, wup_ref[:, base:base + wd])
            up_ref[half, :, 0:wd] = u
            up_ref[half, 0:halo, 0:wd] = jnp.where(first, 0.0, u[0:halo])
            up_ref[half, tm + halo:tm + 2 * halo, 0:wd] = jnp.where(last, 0.0, u[tm + halo:])
            cw = cw_ref[:, base:base + wd]
            halves.append(cw[0:1] * up_ref[half, halo - 1:halo - 1 + tm, 0:wd]
                          + cw[1:2] * up_ref[half, halo:halo + tm, 0:wd]
                          + cw[2:3] * up_ref[half, halo + 1:halo + 1 + tm, 0:wd])
        act = (_silu(halves[1]) * halves[0]).astype(BF16)
        acc = acc + _dot(act, wdn_ref[c0:c0 + wd, :])
    out_ref[0] = x1[halo:halo + tm] + m[5:6] * _rms(acc, g3_ref[...])


def _post_call(o1, o2, xz, mod, g1, g2, g3, w_out, w_up, conv_w, w_down, n_tiles, n_x_tiles):
    b, _, d = xz.shape
    tm = ROW_TILE
    rows = n_tiles * tm
    dff = w_down.shape[0]
    cf = FFN_COL_CHUNK
    k1, k2 = o1.shape[-1], o2.shape[-1]
    ohalo = 2 * SUBLANES
    kern = functools.partial(_post_kernel, tm=tm, n_x_tiles=n_x_tiles, n_tiles=n_tiles, cf=cf, dff=dff)

    def with_halos(width, halo_rows):
        per_tile = tm // halo_rows
        n_blocks = rows // halo_rows
        return [
            pl.BlockSpec((1, halo_rows, width), lambda t, i: (i, jnp.maximum(t * per_tile - 1, 0), 0)),
            pl.BlockSpec((1, tm, width), lambda t, i: (i, t, 0)),
            pl.BlockSpec((1, halo_rows, width), lambda t, i: (i, jnp.minimum((t + 1) * per_tile, n_blocks - 1), 0)),
        ]

    row_vec = pl.BlockSpec((1, d), lambda t, i: (0, 0))
    return pl.pallas_call(
        kern,
        grid=(n_tiles, b),
        in_specs=with_halos(k1, ohalo) + with_halos(k2, ohalo) + with_halos(d, SUBLANES) + [
            pl.BlockSpec((1, 1, 6, d), lambda t, i: (i, t // n_x_tiles, 0, 0)),
            row_vec, row_vec, row_vec,
            _resident((k1 + k2, d)),
            _resident((d, 2 * dff)),
            pl.BlockSpec((3, 2 * dff), lambda t, i: (0, 0)),
            _resident((dff, d)),
        ],
        out_specs=pl.BlockSpec((1, tm, d), lambda t, i: (i, t, 0)),
        out_shape=jax.ShapeDtypeStruct((b, rows, d), F32),
        scratch_shapes=[pltpu.VMEM((2, tm + 2 * SUBLANES, cf), F32)],
        compiler_params=_cparams(("arbitrary", "arbitrary")),
        name="mixer_out_conv_ffn",
    )(o1, o1, o1, o2, o2, o2, xz, xz, xz, mod, g1.reshape(1, d), g2.reshape(1, d), g3.reshape(1, d),
      w_out, w_up, conv_w, w_down)


def _half_sums(x2, lane_lo):
    s0 = jnp.sum(jnp.where(lane_lo, x2, 0.0), axis=-1, keepdims=True)
    s1 = jnp.sum(jnp.where(lane_lo, 0.0, x2), axis=-1, keepdims=True)
    return jnp.where(lane_lo, s0, s1)


def _gdn_kernel(qkvg_ref, ba_ref, cw_ref, gp_ref, ng_ref, out_ref,
                pad_ref, q_ref, k_ref, v_ref, bb_ref, gb_ref, qe_ref, mp_ref, ou_ref, nn_ref, egl_ref, o_ref,
                *, n, nc, chunks_per_iter):
    l = n + nc
    c = GDN_CHUNK
    n_chunks = l // c
    pair = pl.program_id(1)
    halo = SUBLANES
    lane = lax.broadcasted_iota(jnp.int32, (1, LANES), 1)
    lane_lo = lane < GDN_DIM

    cw = cw_ref[:, 0:3 * LANES]
    zero_rows = jnp.zeros((halo, 3 * LANES), F32)
    for seq_start, seq_len in ((0, n), (n, nc)):
        base = halo + seq_start + (2 * halo if seq_start else 0)
        pad_ref[base - halo:base, :] = zero_rows
        pad_ref[base + seq_len:base + seq_len + halo, :] = zero_rows
        step = 256
        for r in range(0, seq_len, step):
            pad_ref[base + r:base + r + step, :] = qkvg_ref[0, seq_start + r:seq_start + r + step, 0:3 * LANES]
        for r in range(0, seq_len, step):
            y = (cw[0:1] * pad_ref[base + r - 1:base + r - 1 + step, :]
                 + cw[1:2] * pad_ref[base + r:base + r + step, :]
                 + cw[2:3] * pad_ref[base + r + 1:base + r + 1 + step, :])
            y = _silu(y)
            q = y[:, 0:LANES]
            k = y[:, LANES:2 * LANES]
            rows = slice(seq_start + r, seq_start + r + step)
            q_ref[rows, :] = q * lax.rsqrt(_half_sums(q * q, lane_lo) + EPS) * (GDN_DIM ** -0.5)
            k_ref[rows, :] = k * lax.rsqrt(_half_sums(k * k, lane_lo) + EPS)
            v_ref[rows, :] = y[:, 2 * LANES:3 * LANES]

    sel_r = lax.broadcasted_iota(jnp.int32, (LANES, 4 * LANES), 0)
    sel_c = lax.broadcasted_iota(jnp.int32, (LANES, 4 * LANES), 1)
    quarter = sel_c >> _log2(LANES)
    src_lane = (quarter & 1) * 2 * GDN_HEADS + (quarter >> 1) * GDN_HEADS + 2 * pair + ((sel_c >> _log2(GDN_DIM)) & 1)
    sel = (sel_r == src_lane).astype(BF16)
    gblk = 256
    bi = lax.broadcasted_iota(jnp.int32, (gblk, gblk), 0)
    bj = lax.broadcasted_iota(jnp.int32, (gblk, gblk), 1)
    same_chunk = (bi >> _log2(c)) == (bj >> _log2(c))
    csum = (jnp.logical_and(same_chunk, bi >= bj).astype(BF16), jnp.logical_and(same_chunk, bi <= bj).astype(BF16))
    neg_a = -jnp.exp(gp_ref[0:1, :])
    dt_bias = gp_ref[1:2, :]
    for r in range(0, l, gblk):
        ba = ba_ref[0, r:r + gblk, :]
        gates = jnp.where(lane < 2 * GDN_HEADS, _sigmoid(ba), neg_a * _softplus(ba + dt_bias))
        x = _dot_sel(gates, sel)
        for d in range(2):
            bb_ref[d, r:r + gblk, :] = x[:, 2 * d * LANES:(2 * d + 1) * LANES]
            gb_ref[d, r:r + gblk, :] = _dot_sel_lhs(csum[d], x[:, (2 * d + 1) * LANES:(2 * d + 2) * LANES])

    r2 = lax.broadcasted_iota(jnp.int32, (2 * c, 2 * c), 0)
    c2 = lax.broadcasted_iota(jnp.int32, (2 * c, 2 * c), 1)
    same_head = (r2 >= c) == (c2 >= c)
    eye = (r2 == c2).astype(F32)
    masks = ((jnp.logical_and(same_head, r2 >= c2), jnp.logical_and(same_head, r2 > c2)),
             (jnp.logical_and(same_head, r2 <= c2), jnp.logical_and(same_head, r2 < c2)))
    m0 = lane_lo.astype(F32)
    m1 = 1.0 - m0

    def pair_mask(lv, lower):
        same_block = (r2 >> (lv + 1)) == (c2 >> (lv + 1))
        r_hi = ((r2 >> lv) & 1) == 1
        c_hi = ((c2 >> lv) & 1) == 1
        off = jnp.logical_and(r_hi, jnp.logical_not(c_hi)) if lower else jnp.logical_and(c_hi, jnp.logical_not(r_hi))
        return jnp.logical_and(same_block, off)

    pair_masks = tuple(tuple(pair_mask(lv, lower) for lv in range(_log2(c))) for lower in (True, False))

    def stack_heads(x2):
        xb = x2.astype(BF16)
        zero = jnp.zeros_like(xb)
        return jnp.concatenate([jnp.where(lane_lo, xb, zero), jnp.where(lane_lo, zero, xb)], axis=0)

    def fold_heads(x):
        return x[0:c] + x[c:2 * c]

    def local_stages(dirs, qs, ks, vs, betas, gcs, out):
        each = lambda f, *cols: [f(*args) for args in zip(*cols)]
        incl = [masks[d][0] for d in dirs]
        strict = [masks[d][1] for d in dirs]
        g1 = each(lambda gc2: jnp.concatenate([gc2, gc2], axis=0), gcs)
        decay = each(lambda g, m: jnp.where(m, jnp.exp(jnp.where(m, g - g.T, 0.0)), 0.0), g1, incl)
        kb = each(lambda k, b: k * b, ks, betas)
        kst = each(lambda k: stack_heads(k).astype(BF16), ks)
        a_raw = each(lambda x, y: _dot_nt(stack_heads(x).astype(BF16), y), kb, kst)
        qk_raw = each(lambda x, y: _dot_nt(stack_heads(x).astype(BF16), y), qs, kst)
        yield
        qk = each(lambda m, x, dc: jnp.where(m, x * dc, 0.0).astype(BF16), incl, qk_raw, decay)
        a = each(lambda m, x, dc: jnp.where(m, x * dc, 0.0), strict, a_raw, decay)
        tinv = each(lambda d, x: eye - jnp.where(pair_masks[d][0], x, 0.0), dirs, a)
        a16 = each(lambda x: x.astype(BF16), a)
        for lv in range(1, _log2(c)):
            ta = each(lambda d, t, x: _dot(t.astype(BF16), jnp.where(pair_masks[d][lv], x, jnp.zeros_like(x))),
                      dirs, tinv, a16)
            yield
            tat = each(_mm, ta, tinv)
            yield
            tinv = each(lambda t, x: t - x, tinv, tat)
        egc = each(jnp.exp, gcs)
        rhs = each(lambda v, b, x, e: jnp.concatenate([stack_heads(v * b), stack_heads(x * e)], axis=1),
                   vs, betas, kb, egc)
        sol = each(_mm, tinv, rhs)
        yield
        u2 = each(lambda x: fold_heads(x[:, 0:LANES]), sol)
        w2 = each(lambda x: fold_heads(x[:, LANES:2 * LANES]), sol)
        gl = each(lambda d, gc2: gc2[c - 1:c, :] if d == 0 else gc2[0:1, :], dirs, gcs)
        ktail = each(lambda k, g, gc2: (k * jnp.exp(g - gc2)).astype(BF16), ks, gl, gcs)
        qwu = each(lambda x, w, u: _dot(x, jnp.concatenate([stack_heads(w), stack_heads(u)], axis=1).astype(BF16)),
                   qk, w2, u2)
        kwu = each(lambda x, w, u: _dot_tn(x, jnp.concatenate([w, u], axis=1).astype(BF16)), ktail, w2, u2)
        yield
        q_eff = each(lambda q, e, x: (q * e - fold_heads(x[:, 0:LANES])).astype(BF16), qs, egc, qwu)
        m_neg = each(lambda x: jnp.where(same_head, -x[:, 0:LANES], 0.0).astype(BF16), kwu)
        o_loc = each(lambda x: fold_heads(x[:, LANES:2 * LANES]), qwu)
        s_loc = each(lambda x: jnp.where(same_head, x[:, LANES:2 * LANES], 0.0), kwu)
        egl = each(lambda g: jnp.broadcast_to(jnp.exp(g), (SUBLANES, LANES)), gl)
        out.extend(zip(q_eff, m_neg, o_loc, s_loc, egl))

    def chunk_rows(chunk, rows_per_chunk):
        return pl.ds(pl.multiple_of(chunk * rows_per_chunk, rows_per_chunk), rows_per_chunk)

    ctx_chunks = nc // c
    per_group = chunks_per_iter
    n_groups = n_chunks // per_group

    def chunks_at(step):
        return jnp.where(step < ctx_chunks, step + n // c, step - ctx_chunks), n_chunks - 1 - step

    def run_group(local_group, scan_group, states):
        dirs, chunks, qs, ks, vs, betas, gcs = [], [], [], [], [], [], []
        if local_group is not None:
            for g in range(per_group):
                for d, chunk in enumerate(chunks_at(per_group * local_group + g)):
                    rows = chunk_rows(chunk, c)
                    dirs.append(d)
                    chunks.append(chunk)
                    qs.append(q_ref[rows, :])
                    ks.append(k_ref[rows, :])
                    vs.append(v_ref[rows, :])
                    betas.append(bb_ref[d, rows, :])
                    gcs.append(gb_ref[d, rows, :])
        scan_chunks, scan_in = [], []
        if scan_group is not None:
            for g in range(per_group):
                step_chunks = chunks_at(per_group * scan_group + g)
                scan_chunks.append(step_chunks)
                scan_in.append([(qe_ref[d, chunk_rows(ch, c), :], mp_ref[d, chunk_rows(ch, 2 * c), :],
                                 ou_ref[d, chunk_rows(ch, c), :], nn_ref[d, chunk_rows(ch, 2 * c), :],
                                 egl_ref[d, chunk_rows(ch, SUBLANES), :]) for d, ch in enumerate(step_chunks)])
        scan_out = []

        def scan_step(states):
            loaded = scan_in[len(scan_out)]
            res = [_dot(jnp.concatenate([ld[0], ld[1]], axis=0), s2.astype(BF16)) for ld, s2 in zip(loaded, states)]
            scan_out.append([r[0:c] + ld[2] for r, ld in zip(res, loaded)])
            return tuple(s2 * ld[4][0:1] + r[c:3 * c] + ld[3] for s2, ld, r in zip(states, loaded, res))

        local_out = []
        stages = local_stages(dirs, qs, ks, vs, betas, gcs, local_out) if local_group is not None else iter(())
        n_stages = 2 * _log2(c) + 1
        every = max(1, n_stages // per_group)
        for stage, _ in enumerate(stages):
            if scan_group is not None and stage % every == 0 and len(scan_out) < per_group:
                states = scan_step(states)
        while scan_group is not None and len(scan_out) < per_group:
            states = scan_step(states)
        for d, chunk, (q_eff, m_neg, o_loc, s_loc, egl) in zip(dirs, chunks, local_out):
            qe_ref[d, chunk_rows(chunk, c), :] = q_eff
            mp_ref[d, chunk_rows(chunk, 2 * c), :] = m_neg
            ou_ref[d, chunk_rows(chunk, c), :] = o_loc
            nn_ref[d, chunk_rows(chunk, 2 * c), :] = s_loc
            egl_ref[d, chunk_rows(chunk, SUBLANES), :] = egl
        for step_chunks, outs in zip(scan_chunks, scan_out):
            for d, ch in enumerate(step_chunks):
                o_ref[d, chunk_rows(ch, c), :] = outs[d]
        return states

    zero_state = jnp.zeros((2 * c, 2 * c), F32)
    states = run_group(0, None, (zero_state, zero_state))
    states = lax.fori_loop(1, n_groups, lambda j, st: run_group(j, j - 1, st), states)
    run_group(None, n_groups - 1, states)

    ng = ng_ref[...]
    step = 256
    for r in range(0, l, step):
        o = o_ref[0, r:r + step, :] + o_ref[1, r:r + step, :]
        ms = _half_sums(o * o, lane_lo) * (1.0 / GDN_DIM)
        gate = qkvg_ref[0, r:r + step, 3 * LANES:4 * LANES]
        out_ref[0, r:r + step, :] = (o * lax.rsqrt(ms + EPS) * ng * _silu(gate)).astype(out_ref.dtype)


def _dot_sel_lhs(sel_bf16, x):
    hi, lo = _split2(x)
    return _dot(sel_bf16, hi) + _dot(sel_bf16, lo)


def _gdn_call(qkvg, ba, conv_w, gate_params, ng, n, nc):
    b, l, _ = qkvg.shape
    pairs = GDN_HEADS // 2
    n_chunks = l // GDN_CHUNK
    kern = functools.partial(_gdn_kernel, n=n, nc=nc, chunks_per_iter=GDN_CHUNKS_PER_GROUP)
    return pl.pallas_call(
        kern,
        grid=(b, pairs),
        in_specs=[
            pl.BlockSpec((1, l, 4 * LANES), lambda i, p: (i, 0, p)),
            pl.BlockSpec((1, l, LANES), lambda i, p: (i, 0, 0)),
            pl.BlockSpec((3, 4 * LANES), lambda i, p: (0, p)),
            pl.BlockSpec((2, LANES), lambda i, p: (0, 0)),
            pl.BlockSpec((1, LANES), lambda i, p: (0, 0)),
        ],
        out_specs=pl.BlockSpec((1, l, LANES), lambda i, p: (i, 0, p)),
        out_shape=jax.ShapeDtypeStruct((b, l, pairs * LANES), MIXER_OUT_DTYPE),
        scratch_shapes=[
            pltpu.VMEM((l + 5 * SUBLANES, 3 * LANES), F32),
            pltpu.VMEM((l, LANES), F32),
            pltpu.VMEM((l, LANES), F32),
            pltpu.VMEM((l, LANES), F32),
            pltpu.VMEM((2, l, LANES), F32),
            pltpu.VMEM((2, l, LANES), F32),
            pltpu.VMEM((2, l, LANES), BF16),
            pltpu.VMEM((2, 2 * l, LANES), BF16),
            pltpu.VMEM((2, l, LANES), F32),
            pltpu.VMEM((2, 2 * l, LANES), F32),
            pltpu.VMEM((2, n_chunks * SUBLANES, LANES), F32),
            pltpu.VMEM((2, l, LANES), F32),
        ],
        compiler_params=_cparams(("arbitrary", "arbitrary")),
        name="gated_deltanet",
    )(qkvg, ba, conv_w, gate_params, ng)


def _diff_kernel(*refs, key_start, n_sub, lam_init, aliased):
    if aliased:
        refs = refs[1:]
    q_ref, k_ref, vt_ref, lam_ref, ng_ref, o_ref = refs
    lp = lam_ref[...]
    lam = (jnp.exp(jnp.sum(lp[0:1] * lp[1:2], axis=-1, keepdims=True))
           - jnp.exp(jnp.sum(lp[2:3] * lp[3:4], axis=-1, keepdims=True)) + lam_init)
    lane = lax.broadcasted_iota(jnp.int32, (1, LANES), 1)
    halves = (lane < DIFF_DIM, lane >= DIFF_DIM)
    ng = ng_ref[...]
    k = k_ref[0, key_start:, :]
    vt = vt_ref[0, :, key_start:]
    tq = q_ref.shape[1] // n_sub

    def scores_of(i):
        q = q_ref[0, i * tq:(i + 1) * tq, :]
        return [_dot_nt(k, jnp.where(m, q, jnp.zeros_like(q))) for m in halves]

    ahead = scores_of(0)
    for i in range(n_sub):
        s = ahead
        if i + 1 < n_sub:
            ahead = scores_of(i + 1)
        e = [jnp.exp(x - jnp.max(x, axis=0, keepdims=True)) for x in s]
        pv = [_dot(vt, x.astype(BF16)) for x in e]
        parts = [x * (1.0 / jnp.sum(y, axis=0, keepdims=True)) for x, y in zip(pv, e)]
        ot = parts[0] - lam * parts[1]
        ot = ot * lax.rsqrt(jnp.mean(ot * ot, axis=0, keepdims=True) + EPS)
        o_ref[0, i * tq:(i + 1) * tq, :] = (ot.T * ng * (1.0 - lam_init)).astype(o_ref.dtype)


def _diff_call(dq, dk, dvt, lam_p, ng, lam_init, q_rows, first_block, n_q_blocks, key_start, n_sub, prev_out):
    b, l, _ = dq.shape
    aliased = prev_out is not None
    kern = functools.partial(_diff_kernel, key_start=key_start, n_sub=n_sub, lam_init=lam_init, aliased=aliased)
    row_of = lambda t: first_block + t
    in_specs = [
        pl.BlockSpec((1, q_rows, LANES), lambda i, h, t: (i, row_of(t), h)),
        pl.BlockSpec((1, l, LANES), lambda i, h, t: (i, 0, h)),
        pl.BlockSpec((1, LANES, l), lambda i, h, t: (i, h, 0)),
        pl.BlockSpec((4, DIFF_DIM), lambda i, h, t: (0, 0)),
        pl.BlockSpec((1, LANES), lambda i, h, t: (0, 0)),
    ]
    args = [dq, dk, dvt, lam_p, ng.reshape(1, LANES)]
    aliases = {}
    if aliased:
        in_specs = [pl.BlockSpec(memory_space=pl.ANY)] + in_specs
        args = [prev_out] + args
        aliases = {0: 0}
    return pl.pallas_call(
        kern,
        grid=(b, DIFF_HEADS, n_q_blocks),
        in_specs=in_specs,
        out_specs=pl.BlockSpec((1, q_rows, LANES), lambda i, h, t: (i, row_of(t), h)),
        out_shape=jax.ShapeDtypeStruct((b, l, DIFF_HEADS * LANES), MIXER_OUT_DTYPE),
        input_output_aliases=aliases,
        compiler_params=_cparams(("arbitrary", "arbitrary", "arbitrary")),
        name="diff_attention_ctx" if aliased else "diff_attention",
    )(*args)


def _swa_kernel(q_ref, k_ref, v_ref, sink_ref, o_ref, *, n, nc):
    t = pl.program_id(1)
    blk = SWA_BLOCK
    n_x = n // blk
    q = q_ref[0]
    lane = lax.broadcasted_iota(jnp.int32, (1, LANES), 1)
    lane_lo = lane < HEAD_DIM
    sink = sink_ref[...]
    group = SWA_HEADS // SWA_KV_HEADS

    def run(keys, vals, valid):
        head_of_row = lax.broadcasted_iota(jnp.int32, (group * blk, 1), 0) >> _log2(blk)
        kvs = range(SWA_KV_HEADS)
        kk = [keys[:, kvh * LANES:(kvh + 1) * LANES] for kvh in kvs]
        vv = [vals[:, kvh * LANES:(kvh + 1) * LANES] for kvh in kvs]
        qst, sk = [], []
        for kvh in kvs:
            q_rows = []
            sk_rows = jnp.zeros((group * blk, 1), F32)
            for g in range(group):
                h = kvh * group + g
                qp = q[:, (h // 2) * LANES:(h // 2 + 1) * LANES]
                q_rows.append(jnp.where(lane_lo if h % 2 == 0 else jnp.logical_not(lane_lo), qp, jnp.zeros_like(qp)))
                sk_rows = jnp.where(head_of_row == g, sink[:, h:h + 1], sk_rows)
            qst.append(jnp.concatenate(q_rows, axis=0))
            sk.append(sk_rows)
        s = [_dot_nt(x, y) for x, y in zip(qst, kk)]
        if valid is not None:
            s = [jnp.where(valid, x, NEG_INF) for x in s]
        mx = [jnp.maximum(jnp.max(x, axis=-1, keepdims=True), y) for x, y in zip(s, sk)]
        e = [jnp.exp(x - m) for x, m in zip(s, mx)]
        pv = [_dot(x.astype(BF16), y) for x, y in zip(e, vv)]
        den = [jnp.sum(x, axis=-1, keepdims=True) + jnp.exp(y - m) for x, y, m in zip(e, sk, mx)]
        outs = []
        for o, dn in zip(pv, den):
            o = o * (1.0 / dn)
            for g in range(0, group, 2):
                outs.append(jnp.where(lane_lo, o[g * blk:(g + 1) * blk], o[(g + 1) * blk:(g + 2) * blk]))
        o_ref[0] = jnp.concatenate(outs, axis=1).astype(o_ref.dtype)

    @pl.when(t < n_x)
    def _():
        start = pl.multiple_of(jnp.clip((t - 1) * blk, 0, n - 3 * blk), blk)
        keys = jnp.concatenate([k_ref[0, pl.ds(start, 3 * blk), :], k_ref[0, n:n + nc, :]], axis=0)
        vals = jnp.concatenate([v_ref[0, pl.ds(start, 3 * blk), :], v_ref[0, n:n + nc, :]], axis=0)
        shape = (group * blk, 3 * blk + nc)
        qpos = t * blk + (lax.broadcasted_iota(jnp.int32, shape, 0) & (blk - 1))
        col = lax.broadcasted_iota(jnp.int32, shape, 1)
        dist = qpos - (start + col)
        in_window = jnp.logical_and(dist <= SWA_WINDOW, dist >= -SWA_WINDOW)
        valid = jnp.logical_or(col >= 3 * blk, in_window)
        run(keys, vals, valid)

    @pl.when(t >= n_x)
    def _():
        run(k_ref[0, n:n + nc, :], v_ref[0, n:n + nc, :], None)


def _swa_call(q, k, v, sink, n, nc, with_ctx):
    b, l, _ = q.shape
    blk = SWA_BLOCK
    nt = (l if with_ctx else n) // blk
    kern = functools.partial(_swa_kernel, n=n, nc=nc)
    return pl.pallas_call(
        kern,
        grid=(b, nt),
        in_specs=[
            pl.BlockSpec((1, blk, SWA_HEADS * HEAD_DIM), lambda i, t: (i, t, 0)),
            pl.BlockSpec((1, l, 2 * LANES), lambda i, t: (i, 0, 0)),
            pl.BlockSpec((1, l, 2 * LANES), lambda i, t: (i, 0, 0)),
            pl.BlockSpec((1, LANES), lambda i, t: (0, 0)),
        ],
        out_specs=pl.BlockSpec((1, blk, SWA_HEADS * HEAD_DIM), lambda i, t: (i, t, 0)),
        out_shape=jax.ShapeDtypeStruct((b, l, SWA_HEADS * HEAD_DIM), MIXER_OUT_DTYPE),
        compiler_params=_cparams(("arbitrary", "arbitrary")),
        name="window_attention",
    )(q, k, v, sink)


def _dft_tables(n):
    h = n // 2
    r = 1 << (_log2(h) // 2)
    j = jnp.arange(h, dtype=jnp.int32)

    def tables(m):
        thin = lambda k: ((k[:, None] * m[None, :]) % (2 * n)).astype(F32) * (math.pi / n)
        a = thin(r * jnp.arange(h // r, dtype=jnp.int32))[:, None, :]
        b = thin(jnp.arange(r, dtype=jnp.int32))[None, :, :]
        cos = jnp.cos(a) * jnp.cos(b) - jnp.sin(a) * jnp.sin(b)
        sin = jnp.sin(a) * jnp.cos(b) + jnp.cos(a) * jnp.sin(b)
        return cos.reshape(h, h).astype(BF16), (-sin).reshape(h, h).astype(BF16)

    ce, se = tables(2 * j)
    co, so = tables(2 * j + 1)
    return ce, se, co, so, co.T, so.T


def _hyena_filter_kernel(feat_ref, w1_ref, b1_ref, w2_ref, b2_ref, freq_ref, w3f_ref, w3b_ref, dl_ref,
                         ce_ref, se_ref, co_ref, so_ref, ka_ref, kb_ref, km_ref):
    h = feat_ref.shape[1]
    assert h % 2 == 0
    freq = freq_ref[...]
    dl = dl_ref[...]
    row = lax.broadcasted_iota(jnp.int32, (h, 1), 0)

    def taps(part, w3_ref):
        feat = feat_ref[part]
        x = jnp.sin(freq[0:1] * (_dot_f32(feat, w1_ref[...]) + b1_ref[...]))
        x = jnp.sin(freq[1:2] * (_dot_f32(x, w2_ref[...]) + b2_ref[...]))
        return _dot_f32(x, w3_ref[...]) * jnp.exp(-feat[:, 0:1] * dl)

    fe, fo = taps(0, w3f_ref), taps(1, w3f_ref)
    be, bo = jnp.where(row == 0, 0.0, taps(2, w3b_ref)), taps(3, w3b_ref)
    ss = sum(jnp.sum(x * x, axis=0, keepdims=True) for x in (fe, fo, be, bo))
    sc = lax.rsqrt(ss + EPS)
    fe, fo, be, bo = (x * sc for x in (fe, fo, be, bo))
    sgn = jnp.where((row & 1) == 0, 1.0, -1.0)

    def dot2(t_ref, x):
        hi, lo = _split2(x)
        return _dot(t_ref[...], hi) + _dot(t_ref[...], lo)

    def bins(x_even, x_odd):
        ce, co = dot2(ce_ref, x_even), dot2(co_ref, x_odd)
        se, so = dot2(se_ref, x_even), dot2(so_ref, x_odd)
        return ce + co, se + so, ce - co, so - se

    f = bins(fe, fo)
    g = bins(be, bo)
    ka_ref[0, 0], ka_ref[0, 1], kb_ref[0, 0], kb_ref[0, 1] = (x + sgn * y for x, y in zip(f, g))
    mid_r = jnp.sum((fe + be) * sgn, axis=0, keepdims=True)
    mid_i = -jnp.sum((fo + bo) * sgn, axis=0, keepdims=True)
    km_ref[0] = jnp.concatenate([mid_r, mid_i, jnp.zeros((SUBLANES - 2, mid_r.shape[-1]), F32)], axis=0)


def _hyena_filter_call(feats, w1, b1, w2, b2, freq, w3, deltas, tables):
    h = feats.shape[1]
    hid = w2.shape[0]
    ch = deltas.shape[-1]
    tc = MXU_WIDTH
    nct = ch // tc
    const = lambda shape: pl.BlockSpec(shape, lambda o, j: (0,) * len(shape))
    spectrum = pl.BlockSpec((1, 2, h, tc), lambda o, j: (o, 0, 0, j))
    return pl.pallas_call(
        _hyena_filter_kernel,
        grid=(2, nct),
        in_specs=[
            const((4, h, hid)), const((hid, hid)), const((1, hid)), const((hid, hid)), const((1, hid)),
            const((2, hid)),
            pl.BlockSpec((hid, tc), lambda o, j: (0, (2 * o) * nct + j)),
            pl.BlockSpec((hid, tc), lambda o, j: (0, (2 * o + 1) * nct + j)),
            pl.BlockSpec((1, tc), lambda o, j: (0, j)),
        ] + [_resident((h, h))] * 4,
        out_specs=[spectrum, spectrum, pl.BlockSpec((1, SUBLANES, tc), lambda o, j: (o, 0, j))],
        out_shape=[
            jax.ShapeDtypeStruct((2, 2, h, ch), F32),
            jax.ShapeDtypeStruct((2, 2, h, ch), F32),
            jax.ShapeDtypeStruct((2, SUBLANES, ch), F32),
        ],
        compiler_params=_cparams(("arbitrary", "arbitrary")),
        name="hyena_filters",
    )(feats, w1, b1, w2, b2, freq, w3, w3, deltas, *tables[:4])


def _hyena_kernel(*refs, n, aliased):
    if aliased:
        refs = refs[1:]
    (v_ref, x1_ref, x2_ref, cwv_ref, cw1_ref, cw2_ref, ka_ref, kb_ref, km_ref, bias_ref,
     ce_ref, se_ref, co_ref, so_ref, cot_ref, sot_ref, o_ref, pad_ref, z_ref, zb_ref, p_ref, y_ref) = refs
    halo = SUBLANES
    tc = o_ref.shape[-1]
    h = n // 2
    rc = min(h, HY_ROW_CHUNK)
    lane_groups = tc // LANES
    zero_rows = jnp.zeros((halo, LANES), F32)
    for g in range(lane_groups):
        pad_ref[g, 0:halo, :] = zero_rows
        pad_ref[g, halo + n:2 * halo + n, :] = zero_rows

    def stage(ref):
        for r in range(0, n, 2 * rc):
            for g in range(lane_groups):
                pad_ref[g, halo + r:halo + r + 2 * rc, :] = ref[0, r:r + 2 * rc, g * LANES:(g + 1) * LANES]

    def conv_rows(cw, parity, r):
        first = halo + 2 * r + parity - 1
        taps = [jnp.concatenate([pad_ref[g, pl.ds(first + i, rc, stride=2), :] for g in range(lane_groups)], axis=1)
                for i in range(3)]
        return cw[0:1] * taps[0] + cw[1:2] * taps[1] + cw[2:3] * taps[2]

    def alt_sign(r):
        j = r + lax.broadcasted_iota(jnp.int32, (rc, 1), 0)
        return j, jnp.where((j & 1) == 0, 1.0, -1.0)

    stage(v_ref)
    cw = cwv_ref[...]
    for parity in range(2):
        for r in range(0, h, rc):
            z = conv_rows(cw, parity, r)
            z_ref[parity, r:r + rc, :] = z
            zb_ref[parity, r:r + rc, :] = z.astype(BF16)

    for o, (gate_ref, gate_cw_ref) in enumerate(((x1_ref, cw1_ref), (x2_ref, cw2_ref))):
        mid_r = jnp.zeros((1, tc), F32)
        mid_i = jnp.zeros((1, tc), F32)
        for r in range(0, h, rc):
            _, sgn = alt_sign(r)
            mid_r = mid_r + jnp.sum(z_ref[0, r:r + rc, :] * sgn, axis=0, keepdims=True)
            mid_i = mid_i - jnp.sum(z_ref[1, r:r + rc, :] * sgn, axis=0, keepdims=True)
        km_r = km_ref[o, 0:1, :]
        km_i = km_ref[o, 1:2, :]
        pm_r = (mid_r * km_r - mid_i * km_i) * (1.0 / n)
        pm_i = (mid_r * km_i + mid_i * km_r) * (1.0 / n)
        ze = zb_ref[0]
        zo = zb_ref[1]
        for r in range(0, h, rc):
            k, _ = alt_sign(r)
            rows = slice(r, r + rc)
            ce, co = _dot(ce_ref[rows, :], ze), _dot(co_ref[rows, :], zo)
            se, so = _dot(se_ref[rows, :], ze), _dot(so_ref[rows, :], zo)
            xa_r, xb_r, xa_i, xb_i = ce + co, ce - co, se + so, so - se
            wgt = jnp.where(k == 0, 0.5 / n, 1.0 / n)
            ka_r, ka_i = ka_ref[o, 0, rows, :], ka_ref[o, 1, rows, :]
            kb_r, kb_i = kb_ref[o, 0, rows, :], kb_ref[o, 1, rows, :]
            pa_r = (xa_r * ka_r - xa_i * ka_i) * wgt
            pa_i = (xa_r * ka_i + xa_i * ka_r) * wgt
            pb_r = (xb_r * kb_r - xb_i * kb_i) * wgt
            pb_i = (xb_r * kb_i + xb_i * kb_r) * wgt
            p_ref[0, rows, :] = (pa_r + pb_r).astype(BF16)
            p_ref[1, rows, :] = (pa_i - pb_i).astype(BF16)
            p_ref[2, rows, :] = (pa_r - pb_r).astype(BF16)
            p_ref[3, rows, :] = (pa_i + pb_i).astype(BF16)
        stage(gate_ref)
        cw = gate_cw_ref[...]
        bias = bias_ref[o:o + 1, :]
        for parity, (c_ref, s_ref, mid) in enumerate(((ce_ref, se_ref, pm_r), (cot_ref, sot_ref, -pm_i))):
            for r in range(0, h, rc):
                _, sgn = alt_sign(r)
                rows = slice(r, r + rc)
                y = (_dot(c_ref[rows, :], p_ref[2 * parity]) + _dot(s_ref[rows, :], p_ref[2 * parity + 1])
                     + sgn * mid)
                z = conv_rows(cw, parity, r) * (y + z_ref[parity, rows, :] * bias)
                if o == 0:
                    z_ref[parity, rows, :] = z
                    zb_ref[parity, rows, :] = z.astype(BF16)
                else:
                    for g in range(lane_groups):
                        y_ref[g, pl.ds(2 * r + parity, rc, stride=2), :] = z[:, g * LANES:(g + 1) * LANES]
    for r in range(0, n, 2 * rc):
        rows = slice(r, r + 2 * rc)
        o_ref[0, rows, :] = jnp.concatenate([y_ref[g, rows, :] for g in range(lane_groups)],
                                            axis=1).astype(o_ref.dtype)


def _hyena_call(u, conv_w, ka, kb, km, bias, tables, n, row_block, prev_out):
    b, l, _ = u.shape
    ch = bias.shape[-1]
    tc = MXU_WIDTH
    nct = ch // tc
    h = n // 2
    aliased = prev_out is not None
    kern = functools.partial(_hyena_kernel, n=n, aliased=aliased)
    once = pl.Buffered(1)
    in_specs = [
        pl.BlockSpec((1, n, tc), lambda j, i: (i, row_block, j)),
        pl.BlockSpec((1, n, tc), lambda j, i: (i, row_block, nct + j)),
        pl.BlockSpec((1, n, tc), lambda j, i: (i, row_block, 2 * nct + j)),
        pl.BlockSpec((3, tc), lambda j, i: (0, j)),
        pl.BlockSpec((3, tc), lambda j, i: (0, nct + j)),
        pl.BlockSpec((3, tc), lambda j, i: (0, 2 * nct + j)),
        pl.BlockSpec((2, 2, h, tc), lambda j, i: (0, 0, 0, j), pipeline_mode=once),
        pl.BlockSpec((2, 2, h, tc), lambda j, i: (0, 0, 0, j), pipeline_mode=once),
        pl.BlockSpec((2, SUBLANES, tc), lambda j, i: (0, 0, j)),
        pl.BlockSpec((2, tc), lambda j, i: (0, j)),
    ] + [_resident((h, h))] * 6
    args = [u, u, u, conv_w, conv_w, conv_w, ka, kb, km, bias, *tables]
    aliases = {}
    if aliased:
        in_specs = [pl.BlockSpec(memory_space=pl.ANY)] + in_specs
        args = [prev_out] + args
        aliases = {0: 0}
    return pl.pallas_call(
        kern,
        grid=(nct, b),
        in_specs=in_specs,
        out_specs=pl.BlockSpec((1, n, tc), lambda j, i: (i, row_block, j)),
        out_shape=jax.ShapeDtypeStruct((b, l, ch), MIXER_OUT_DTYPE),
        scratch_shapes=[
            pltpu.VMEM((tc // LANES, n + 2 * SUBLANES, LANES), F32),
            pltpu.VMEM((2, h, tc), F32),
            pltpu.VMEM((2, h, tc), BF16),
            pltpu.VMEM((4, h, tc), BF16),
            pltpu.VMEM((tc // LANES, n, LANES), F32),
        ],
        input_output_aliases=aliases,
        compiler_params=_cparams(("arbitrary", "arbitrary")),
        name="hyena_conv_n%d" % n,
    )(*args)


def _rope_tables(n, nc):
    rows = n // GRID_W
    row = jnp.repeat(jnp.arange(rows, dtype=F32), GRID_W)
    col = jnp.tile(jnp.arange(GRID_W, dtype=F32), rows)
    half = HEAD_DIM // 2
    inv = ROPE_BASE ** (-jnp.arange(0, half, 2, dtype=F32) / half)
    ar = row[:, None] * inv
    ac = col[:, None] * inv
    cos = jnp.concatenate([jnp.cos(ar), jnp.cos(ar), jnp.cos(ac), jnp.cos(ac)], axis=-1)
    sin = jnp.concatenate([-jnp.sin(ar), jnp.sin(ar), -jnp.sin(ac), jnp.sin(ac)], axis=-1)
    cos = jnp.concatenate([cos, jnp.ones((nc, HEAD_DIM), F32)], axis=0)
    sin = jnp.concatenate([sin, jnp.zeros((nc, HEAD_DIM), F32)], axis=0)
    return jnp.tile(cos, (1, LANES // HEAD_DIM)), jnp.tile(sin, (1, LANES // HEAD_DIM))


def _rope_partner_cols(width):
    d = np.arange(width)
    quarter = HEAD_DIM // 4
    return np.where((d % (2 * quarter)) < quarter, d + quarter, d - quarter)


def _hyena_feats(n):
    pos = jnp.arange(n, dtype=F32)
    t = pos / max(n - 1, 1)
    ang = (2.0 * math.pi * pos / n)[:, None] * jnp.linspace(1e-4, HY_BANDS - 1, HY_BANDS, dtype=F32)[None, :]
    feats = jnp.concatenate([t[:, None], jnp.cos(ang), -jnp.sin(ang)], axis=-1)
    feats = jnp.pad(feats, ((0, 0), (0, 64 - feats.shape[-1])))
    back = jnp.concatenate([feats[0:1], jnp.flip(feats[1:], axis=0)], axis=0)
    return jnp.stack([feats[0::2], feats[1::2], back[0::2], back[1::2]])


def _pad_cols(w, width):
    return jnp.pad(w, ((0, 0), (0, width - w.shape[-1])))


def _layer_ab(xz, mod, norm_g0, w_in, conv_w, a_log, dt_bias, gdn_g, lam_p, diff_g, lam_init, rope, n, nc):
    hd = GDN_HEADS * GDN_DIM
    wq, wk, wv, wg = (w_in[:, i * hd:(i + 1) * hd] for i in range(4))
    o = 4 * hd
    w_beta, w_alpha = w_in[:, o:o + 16], w_in[:, o + 16:o + 32]
    o += 32
    dd = DIFF_HEADS * 2 * DIFF_DIM
    wdq, wdk, wdv = (w_in[:, o + i * dd:o + (i + 1) * dd] for i in range(3))
    pairs = GDN_HEADS // 2
    pair_cols = lambda w: [w[:, p * LANES:(p + 1) * LANES] for p in range(pairs)]
    w_qkvg = jnp.concatenate([blk for grp in zip(pair_cols(wq), pair_cols(wk), pair_cols(wv), pair_cols(wg))
                              for blk in grp], axis=1)
    perm = _rope_partner_cols(dd)
    w_all = jnp.concatenate([w_qkvg, _pad_cols(jnp.concatenate([w_beta, w_alpha], axis=1), LANES),
                             wdq, wdk, wdv, wdq[:, perm], wdk[:, perm]], axis=1).astype(BF16)
    c0 = 4 * hd
    c1 = c0 + LANES
    segs = (_Seg(0, c0), _Seg(c0, LANES),
            _Seg(c1, dd, rot_start=c1 + 3 * dd, scale=DIFF_DIM ** -0.5, dtype=BF16),
            _Seg(c1 + dd, dd, rot_start=c1 + 4 * dd, dtype=BF16),
            _Seg(c1 + 2 * dd, dd, dtype=BF16, transposed=True))
    qkvg, ba, dq, dk, dvt = _proj_call(xz, mod, norm_g0, w_all, rope[0], rope[1], segs, n // ROW_TILE, "proj_ab")

    cq, ck, cv = (conv_w[:, i * hd:(i + 1) * hd] for i in range(3))
    zeros = jnp.zeros((3, LANES), F32)
    conv_l = jnp.concatenate([blk for p in range(pairs) for blk in
                              (cq[:, p * LANES:(p + 1) * LANES], ck[:, p * LANES:(p + 1) * LANES],
                               cv[:, p * LANES:(p + 1) * LANES], zeros)], axis=1)
    n_gate = 2 * GDN_HEADS
    on_decay_lanes = lambda t: jnp.pad(t.reshape(1, n_gate), ((0, 0), (n_gate, LANES - 2 * n_gate)))
    gate_params = jnp.concatenate([on_decay_lanes(a_log), on_decay_lanes(dt_bias)], axis=0)
    ng = jnp.tile(gdn_g.reshape(1, GDN_DIM), (1, 2))
    oa = _gdn_call(qkvg, ba, conv_l, gate_params, ng, n, nc)
    q_rows = DIFF_SUB_TILES * ROW_TILE
    ob = _diff_call(dq, dk, dvt, lam_p, diff_g, lam_init, q_rows, 0, n // q_rows, 0, DIFF_SUB_TILES, None)
    ob = _diff_call(dq, dk, dvt, lam_p, diff_g, lam_init, nc, n // nc, 1, n, 1, ob)
    return oa, ob


def _layer_cd(xz, mod, norm_g0, w_in, sink, hy_conv, hy_w1, hy_b1, hy_w2, hy_b2, hy_w3, hy_freq, hy_bias,
              rope, n, nc, last, dft_x, dft_c):
    qd = SWA_HEADS * HEAD_DIM
    kd = SWA_KV_HEADS * HEAD_DIM
    wq, wk, wv, wu = w_in[:, 0:qd], w_in[:, qd:qd + kd], w_in[:, qd + kd:qd + 2 * kd], w_in[:, qd + 2 * kd:]
    dup = lambda w: jnp.concatenate([w[:, 0:HEAD_DIM], w[:, 0:HEAD_DIM], w[:, HEAD_DIM:], w[:, HEAD_DIM:]], axis=1)
    wk2, wv2 = dup(wk), dup(wv)
    ud = wu.shape[1]
    w_all = jnp.concatenate([wq, wk2, wv2, wu, wq[:, _rope_partner_cols(qd)], wk2[:, _rope_partner_cols(2 * kd)]],
                            axis=1).astype(BF16)
    o_u = qd + 4 * kd
    segs = (_Seg(0, qd, rot_start=o_u + ud, scale=HEAD_DIM ** -0.5, dtype=BF16),
            _Seg(qd, 2 * kd, rot_start=o_u + ud + qd, dtype=BF16),
            _Seg(qd + 2 * kd, 2 * kd, dtype=BF16), _Seg(o_u, ud))
    q, k, v, u = _proj_call(xz, mod, norm_g0, w_all, rope[0], rope[1], segs, n // ROW_TILE, "proj_cd")
    oc = _swa_call(q, k, v, _pad_cols(sink.reshape(1, SWA_HEADS), LANES), n, nc, not last)

    ch = hy_bias.shape[-1]
    deltas = jnp.abs(jnp.linspace(HY_MIN_DECAY, HY_MAX_DECAY, ch, dtype=F32)).reshape(1, ch)
    hid = hy_w2.shape[0]
    w1p = jnp.pad(hy_w1, ((0, hid - hy_w1.shape[0]), (0, 0)))
    filt = lambda m, dft: _hyena_filter_call(_hyena_feats(m), w1p, hy_b1.reshape(1, hid), hy_w2,
                                             hy_b2.reshape(1, hid), hy_freq, hy_w3, deltas, dft)
    od = _hyena_call(u, hy_conv, *filt(n, dft_x), hy_bias, dft_x, n, 0, None)
    if not last:
        od = _hyena_call(u, hy_conv, *filt(nc, dft_c), hy_bias, dft_c, nc, n // nc, od)
    return oc, od


def kernel(x, c, ctx, c_ctx, w_mod, b_mod, norm_g, ffn_w_up, ffn_conv, ffn_w_down, ab_w_in, ab_w_out, gdn_conv, gdn_a_log, gdn_dt_bias, gdn_norm_g, diff_lambda, diff_norm_g, cd_w_in, cd_w_out, swa_sink, hy_conv, hy_w1, hy_b1, hy_w2, hy_b2, hy_w3, hy_freq, hy_bias):
    b, n, d = x.shape
    nc = ctx.shape[1]
    depth = w_mod.shape[0]
    assert n % ROW_TILE == 0 and nc == ROW_TILE and n % GRID_W == 0
    xz = jnp.concatenate([x, ctx], axis=1)
    rows = -(-(b + 1) // SUBLANES) * SUBLANES
    cc = jnp.concatenate([c, c_ctx[None], jnp.zeros((rows - b - 1, d), F32)], axis=0)
    mods = _mod_call(cc, w_mod, b_mod)
    mod_all = jnp.concatenate([mods[:, :b].reshape(depth, b, 1, 6, d),
                               jnp.broadcast_to(mods[:, b].reshape(depth, 1, 1, 6, d), (depth, b, 1, 6, d))], axis=2)
    rope = _rope_tables(n, nc)
    dft_x = _dft_tables(n)
    dft_c = _dft_tables(nc)
    n_x_tiles = n // ROW_TILE
    for l in range(depth):
        last = l == depth - 1
        i = l // 2
        mod = mod_all[l]
        if l % 2 == 0:
            lam_init = 0.8 - 0.6 * math.exp(-0.3 * l)
            o1, o2 = _layer_ab(xz, mod, norm_g[l, 0], ab_w_in[i], gdn_conv[i], gdn_a_log[i], gdn_dt_bias[i],
                               gdn_norm_g[i], diff_lambda[i], diff_norm_g[i], lam_init, rope, n, nc)
            w_out = ab_w_out[i]
        else:
            o1, o2 = _layer_cd(xz, mod, norm_g[l, 0], cd_w_in[i], swa_sink[i], hy_conv[i], hy_w1[i], hy_b1[i],
                               hy_w2[i], hy_b2[i], hy_w3[i], hy_freq[i], hy_bias[i], rope, n, nc, last, dft_x, dft_c)
            w_out = cd_w_out[i]
        n_tiles = (n if last else n + nc) // ROW_TILE
        xz = _post_call(o1, o2, xz, mod, norm_g[l, 1], norm_g[l, 2], norm_g[l, 3], w_out.astype(BF16),
                        ffn_w_up[l].astype(BF16), ffn_conv[l], ffn_w_down[l].astype(BF16), n_tiles, n_x_tiles)
    return xz
```
